```python
import jax, jax.numpy as jnp
from jax import lax
import numpy as np

D_MODEL = 1024
BATCH = 2
SEQ = 8192
DEPTH = 1
DEC_BATCH = 16
DEC_SEQ = 64
PAST_LEN = 4096

CHUNK = 64
LEFT_CHUNKS = 8
BAND_CHUNKS = LEFT_CHUNKS + 1
WINDOW_ROWS = LEFT_CHUNKS * CHUNK
HEAD_DIM = 64
N_HEADS_A = 8
N_HEADS_B = 8
WIDTH_A = N_HEADS_A * HEAD_DIM
WIDTH_B = N_HEADS_B * HEAD_DIM
MAX_REL = 256
Q_BLOCK = 128
N_GROUPS = 4
EXPERTS_PER_GROUP = 8
N_EXPERTS = N_GROUPS * EXPERTS_PER_GROUP
TOP_K = 2
D_EXPERT = 128
IN_COLS = 3 * WIDTH_A + 3 * WIDTH_B + N_HEADS_B + 2 * D_MODEL
EPS = 1e-6
NEG = -1e30

kernel_name = "hybrid_chunkband_fox_hiermoe_step"


def rmsnorm(x, g):
    xf = x.astype(jnp.float32)
    y = xf * lax.rsqrt(jnp.mean(xf * xf, axis=-1, keepdims=True) + EPS)
    return (y * g.astype(jnp.float32)).astype(x.dtype)


def mixer_inputs(h, w_in, b_f, q_norm_a, k_norm_a, q_norm_b, k_norm_b):
    n, t, _ = h.shape
    proj = h @ w_in
    sizes = [WIDTH_A, WIDTH_A, WIDTH_A, WIDTH_B, WIDTH_B, WIDTH_B, N_HEADS_B, 2 * D_MODEL]
    points = [int(p) for p in np.cumsum(sizes)[:-1]]
    q_a, k_a, v_a, q_b, k_b, v_b, f_logit, gate_logit = jnp.split(proj, points, axis=-1)
    q_a = rmsnorm(q_a.reshape(n, t, N_HEADS_A, HEAD_DIM), q_norm_a)
    k_a = rmsnorm(k_a.reshape(n, t, N_HEADS_A, HEAD_DIM), k_norm_a)
    v_a = v_a.reshape(n, t, N_HEADS_A, HEAD_DIM)
    q_b = rmsnorm(q_b.reshape(n, t, N_HEADS_B, HEAD_DIM), q_norm_b)
    k_b = rmsnorm(k_b.reshape(n, t, N_HEADS_B, HEAD_DIM), k_norm_b)
    v_b = v_b.reshape(n, t, N_HEADS_B, HEAD_DIM)
    logf = jax.nn.log_sigmoid((f_logit + b_f).astype(jnp.float32))
    return q_a, k_a, v_a, q_b, k_b, v_b, logf, gate_logit


def rel_bias_block(rel_bias, n_q, n_k, offset):
    dist = jnp.arange(n_q)[:, None] + offset - jnp.arange(n_k)[None, :]
    idx = jnp.clip(dist, -MAX_REL, MAX_REL) + MAX_REL
    return rel_bias[:, idx].astype(jnp.float32)


def band_attention_prompt(q, k, v, rel_bias):
    n, s, h, d = q.shape
    nc = s // CHUNK
    qc = q.reshape(n, nc, CHUNK, h, d)
    pad = ((0, 0), (LEFT_CHUNKS, 0), (0, 0), (0, 0), (0, 0))
    kp = jnp.pad(k.reshape(n, nc, CHUNK, h, d), pad)
    vp = jnp.pad(v.reshape(n, nc, CHUNK, h, d), pad)
    band_idx = jnp.arange(nc)[:, None] + jnp.arange(BAND_CHUNKS)[None, :]
    kb = kp[:, band_idx].reshape(n, nc, BAND_CHUNKS * CHUNK, h, d)
    vb = vp[:, band_idx].reshape(n, nc, BAND_CHUNKS * CHUNK, h, d)
    key_chunk = jnp.arange(nc)[:, None] - LEFT_CHUNKS + jnp.arange(BAND_CHUNKS * CHUNK)[None, :] // CHUNK
    valid = key_chunk >= 0
    bias = rel_bias_block(rel_bias, CHUNK, BAND_CHUNKS * CHUNK, WINDOW_ROWS)
    sc = jnp.einsum('bnqhd,bnkhd->bnhqk', qc, kb).astype(jnp.float32) * (HEAD_DIM ** -0.5) + bias[None, None]
    sc = jnp.where(valid[None, :, None, None, :], sc, NEG)
    p = jax.nn.softmax(sc, axis=-1).astype(v.dtype)
    o = jnp.einsum('bnhqk,bnkhd->bnqhd', p, vb)
    return o.reshape(n, s, h * d)


def band_attention_sample(q, k, v, k_cache, v_cache, rel_bias):
    n, t, h, d = q.shape
    p_rows = k_cache.shape[1]
    kk = jnp.concatenate([k_cache.astype(k.dtype), k], axis=1)
    vv = jnp.concatenate([v_cache.astype(v.dtype), v], axis=1)
    bias = rel_bias_block(rel_bias, t, p_rows + t, p_rows)
    sc = jnp.einsum('bqhd,bkhd->bhqk', q, kk).astype(jnp.float32) * (HEAD_DIM ** -0.5) + bias[None]
    p = jax.nn.softmax(sc, axis=-1).astype(v.dtype)
    o = jnp.einsum('bhqk,bkhd->bqhd', p, vv)
    return o.reshape(n, t, h * d), kk[:, -p_rows:], vv[:, -p_rows:]


def fox_prompt(q, k, v, logf):
    n, s, h, d = q.shape
    nqb = s // Q_BLOCK
    c = jnp.cumsum(logf, axis=1)
    c_k = jnp.transpose(c, (0, 2, 1))
    k_pos = jnp.arange(s)
    qb = jnp.moveaxis(q.reshape(n, nqb, Q_BLOCK, h, d), 1, 0)
    cb = jnp.moveaxis(c.reshape(n, nqb, Q_BLOCK, h), 1, 0)

    def block(args):
        q_i, c_i, i = args
        q_pos = i * Q_BLOCK + jnp.arange(Q_BLOCK)
        decay = jnp.transpose(c_i, (0, 2, 1))[..., None] - c_k[:, :, None, :]
        sc = jnp.einsum('bqhd,bkhd->bhqk', q_i, k).astype(jnp.float32) * (HEAD_DIM ** -0.5) + decay
        sc = jnp.where(k_pos[None, :] <= q_pos[:, None], sc, NEG)
        p = jax.nn.softmax(sc, axis=-1).astype(v.dtype)
        return jnp.einsum('bhqk,bkhd->bqhd', p, v)

    o = lax.map(block, (qb, cb, jnp.arange(nqb)))
    return jnp.moveaxis(o, 0, 1).reshape(n, s, h * d)


def fox_sample(q, k, v, logf, k_cache, v_cache, logf_cache):
    n, t, h, d = q.shape
    p_rows = k_cache.shape[1]
    kk = jnp.concatenate([k_cache.astype(k.dtype), k], axis=1)
    vv = jnp.concatenate([v_cache.astype(v.dtype), v], axis=1)
    lf = jnp.concatenate([logf_cache.astype(jnp.float32), logf], axis=1)
    c_k = jnp.transpose(jnp.cumsum(lf, axis=1), (0, 2, 1))
    c_q = c_k[:, :, p_rows:]
    sc = jnp.einsum('bqhd,bkhd->bhqk', q, kk).astype(jnp.float32) * (HEAD_DIM ** -0.5)
    sc = sc + c_q[..., None] - c_k[:, :, None, :]
    mask = jnp.arange(p_rows + t)[None, :] <= p_rows + jnp.arange(t)[:, None]
    sc = jnp.where(mask, sc, NEG)
    p = jax.nn.softmax(sc, axis=-1).astype(v.dtype)
    o = jnp.einsum('bhqk,bkhd->bqhd', p, vv)
    return o.reshape(n, t, h * d)


def merge_branches(o_a, o_b, gate_logit, w_pa, w_pb, w_o):
    g_a, g_b = jnp.split(jax.nn.sigmoid(gate_logit), 2, axis=-1)
    return (g_a * (o_a @ w_pa) + g_b * (o_b @ w_pb)) @ w_o


def mixer_prompt(x, g_mix, w_in, b_f, q_norm_a, k_norm_a, q_norm_b, k_norm_b, rel_bias, w_pa, w_pb, w_o):
    h = rmsnorm(x, g_mix)
    q_a, k_a, v_a, q_b, k_b, v_b, logf, gate_logit = mixer_inputs(h, w_in, b_f, q_norm_a, k_norm_a, q_norm_b, k_norm_b)
    o_a = band_attention_prompt(q_a, k_a, v_a, rel_bias)
    o_b = fox_prompt(q_b, k_b, v_b, logf)
    y = x + merge_branches(o_a, o_b, gate_logit, w_pa, w_pb, w_o)
    rows = min(WINDOW_ROWS, x.shape[1])
    return y, (k_a[:, -rows:], v_a[:, -rows:], k_b, v_b, logf)


def mixer_sample(x, c_a_k, c_a_v, c_b_k, c_b_v, c_b_logf, g_mix, w_in, b_f, q_norm_a, k_norm_a, q_norm_b, k_norm_b, rel_bias, w_pa, w_pb, w_o):
    h = rmsnorm(x, g_mix)
    q_a, k_a, v_a, q_b, k_b, v_b, logf, gate_logit = mixer_inputs(h, w_in, b_f, q_norm_a, k_norm_a, q_norm_b, k_norm_b)
    o_a, new_a_k, new_a_v = band_attention_sample(q_a, k_a, v_a, c_a_k, c_a_v, rel_bias)
    o_b = fox_sample(q_b, k_b, v_b, logf, c_b_k, c_b_v, c_b_logf)
    y = x + merge_branches(o_a, o_b, gate_logit, w_pa, w_pb, w_o)
    return y, (new_a_k, new_a_v, k_b, v_b, logf)


def hier_moe(x, g_ffn, w_rg, b_rg, w_re, b_re, w1, w3, w2):
    n, t, d = x.shape
    hx = rmsnorm(x, g_ffn).reshape(n * t, d)
    coarse = (hx @ w_rg).astype(jnp.float32) + b_rg.astype(jnp.float32)
    p_g = jax.nn.softmax(coarse, axis=-1)
    grp = jnp.argmax(coarse, axis=-1)
    pg_sel = jnp.take_along_axis(p_g, grp[:, None], axis=-1)
    fine = jnp.einsum('td,gde->tge', hx, w_re).astype(jnp.float32) + b_re.astype(jnp.float32)
    fine_sel = jnp.take_along_axis(fine, grp[:, None, None], axis=1)[:, 0]
    top_p, top_i = lax.top_k(jax.nn.softmax(fine_sel, axis=-1), TOP_K)
    top_p = top_p / jnp.sum(top_p, axis=-1, keepdims=True)
    expert = grp[:, None] * EXPERTS_PER_GROUP + top_i
    combine = jnp.sum(jax.nn.one_hot(expert, N_EXPERTS, dtype=jnp.float32) * (pg_sel * top_p)[..., None], axis=1)
    hid = jax.nn.silu(jnp.einsum('td,edf->tef', hx, w1)) * jnp.einsum('td,edf->tef', hx, w3)
    hid = hid * combine.astype(hid.dtype)[..., None]
    y = jnp.einsum('tef,efd->td', hid, w2)
    return y.reshape(n, t, d)


def setup_inputs(seed: int = 0) -> dict:
    key = jax.random.key(seed)
    ks = jax.random.split(key, 32)
    f32 = jnp.float32
    nrm = lambda k, shape, scale: jax.random.normal(k, shape, f32) * scale
    a_rows = min(WINDOW_ROWS, PAST_LEN)
    return {
        'x_prompt': nrm(ks[0], (BATCH, SEQ, D_MODEL), 1.0),
        'x_sample': nrm(ks[1], (DEC_BATCH, DEC_SEQ, D_MODEL), 1.0),
        'cache_a_k': nrm(ks[2], (DEPTH, DEC_BATCH, a_rows, N_HEADS_A, HEAD_DIM), 1.0),
        'cache_a_v': nrm(ks[3], (DEPTH, DEC_BATCH, a_rows, N_HEADS_A, HEAD_DIM), 1.0),
        'cache_b_k': nrm(ks[4], (DEPTH, DEC_BATCH, PAST_LEN, N_HEADS_B, HEAD_DIM), 1.0),
        'cache_b_v': nrm(ks[5], (DEPTH, DEC_BATCH, PAST_LEN, N_HEADS_B, HEAD_DIM), 1.0),
        'cache_b_logf': jax.nn.log_sigmoid(3.0 + nrm(ks[6], (DEPTH, DEC_BATCH, PAST_LEN, N_HEADS_B), 1.0)),
        'g_mix': 1.0 + nrm(ks[7], (DEPTH, D_MODEL), 0.02),
        'w_in': nrm(ks[8], (DEPTH, D_MODEL, IN_COLS), D_MODEL ** -0.5),
        'b_f': 2.0 + 2.0 * jax.random.uniform(ks[9], (DEPTH, N_HEADS_B), f32),
        'q_norm_a': 1.0 + nrm(ks[10], (DEPTH, HEAD_DIM), 0.02),
        'k_norm_a': 1.0 + nrm(ks[11], (DEPTH, HEAD_DIM), 0.02),
        'q_norm_b': 1.0 + nrm(ks[12], (DEPTH, HEAD_DIM), 0.02),
        'k_norm_b': 1.0 + nrm(ks[13], (DEPTH, HEAD_DIM), 0.02),
        'rel_bias': nrm(ks[14], (DEPTH, N_HEADS_A, 2 * MAX_REL + 1), 0.2),
        'w_pa': nrm(ks[15], (DEPTH, WIDTH_A, D_MODEL), WIDTH_A ** -0.5),
        'w_pb': nrm(ks[16], (DEPTH, WIDTH_B, D_MODEL), WIDTH_B ** -0.5),
        'w_o': nrm(ks[17], (DEPTH, D_MODEL, D_MODEL), D_MODEL ** -0.5),
        'g_ffn': 1.0 + nrm(ks[18], (DEPTH, D_MODEL), 0.02),
        'w_rg': nrm(ks[19], (DEPTH, D_MODEL, N_GROUPS), D_MODEL ** -0.5),
        'b_rg': nrm(ks[20], (DEPTH, N_GROUPS), 0.01),
        'w_re': nrm(ks[21], (DEPTH, N_GROUPS, D_MODEL, EXPERTS_PER_GROUP), D_MODEL ** -0.5),
        'b_re': nrm(ks[22], (DEPTH, N_GROUPS, EXPERTS_PER_GROUP), 0.01),
        'w1': nrm(ks[23], (DEPTH, N_EXPERTS, D_MODEL, D_EXPERT), D_MODEL ** -0.5),
        'w3': nrm(ks[24], (DEPTH, N_EXPERTS, D_MODEL, D_EXPERT), D_MODEL ** -0.5),
        'w2': nrm(ks[25], (DEPTH, N_EXPERTS, D_EXPERT, D_MODEL), D_EXPERT ** -0.5),
    }


def reference(x_prompt, x_sample, cache_a_k, cache_a_v, cache_b_k, cache_b_v, cache_b_logf,
              g_mix, w_in, b_f, q_norm_a, k_norm_a, q_norm_b, k_norm_b, rel_bias, w_pa, w_pb, w_o,
              g_ffn, w_rg, b_rg, w_re, b_re, w1, w3, w2):
    x_p, x_s = x_prompt, x_sample
    st_p, st_s = [], []
    for l in range(DEPTH):
        mix = (g_mix[l], w_in[l], b_f[l], q_norm_a[l], k_norm_a[l], q_norm_b[l], k_norm_b[l],
               rel_bias[l], w_pa[l], w_pb[l], w_o[l])
        ffn = (g_ffn[l], w_rg[l], b_rg[l], w_re[l], b_re[l], w1[l], w3[l], w2[l])
        x_p, sp = mixer_prompt(x_p, *mix)
        x_p = x_p + hier_moe(x_p, *ffn)
        x_s, ss = mixer_sample(x_s, cache_a_k[l], cache_a_v[l], cache_b_k[l], cache_b_v[l], cache_b_logf[l], *mix)
        x_s = x_s + hier_moe(x_s, *ffn)
        st_p.append(sp)
        st_s.append(ss)

    def stack(states, i):
        return jnp.stack([s[i] for s in states])

    return (x_p, x_s,
            stack(st_p, 0), stack(st_p, 1), stack(st_p, 2), stack(st_p, 3), stack(st_p, 4),
            stack(st_s, 0), stack(st_s, 1), stack(st_s, 2), stack(st_s, 3), stack(st_s, 4))
```

```python
import functools

import numpy as np
import jax
import jax.numpy as jnp
from jax import lax
from jax.experimental import pallas as pl
from jax.experimental.pallas import tpu as pltpu

F32 = jnp.float32
BF16 = jnp.bfloat16

D_MODEL = 1024
HEAD_DIM = 64
N_HEADS = 8
WIDTH = N_HEADS * HEAD_DIM
N_PAIRS = N_HEADS // 2
CHUNK = 64
LEFT_CHUNKS = 8
WINDOW_ROWS = LEFT_CHUNKS * CHUNK
MAX_REL = 256
N_GROUPS = 4
EXPERTS_PER_GROUP = 8
N_EXPERTS = N_GROUPS * EXPERTS_PER_GROUP
D_EXPERT = 128
EPS = 1e-6
NEG = -1e30
LANES = 128
AUG_STRIDE = 8
VMEM_LIMIT = 56 * 1024 * 1024

_NT = (((1,), (1,)), ((), ()))


def _dot(a, b):
    return jnp.dot(a, b, preferred_element_type=F32)


def _dot_nt(a, b):
    return lax.dot_general(a, b, _NT, preferred_element_type=F32)


def _split3(x):
    hi = x.astype(BF16)
    r = x - hi.astype(F32)
    mid = r.astype(BF16)
    lo = (r - mid.astype(F32)).astype(BF16)
    return hi, mid, lo


def _dot3(a_bf, x):
    hi, mid, lo = _split3(x)
    return _dot(a_bf, hi) + _dot(a_bf, mid) + _dot(a_bf, lo)


def _rms(x, g):
    ms = jnp.mean(x * x, axis=-1, keepdims=True)
    return x * lax.rsqrt(ms + EPS) * g


def _params(n_axes):
    return pltpu.CompilerParams(dimension_semantics=("arbitrary",) * n_axes,
                                vmem_limit_bytes=VMEM_LIMIT)


def _const_spec(shape):
    nd = len(shape)
    return pl.BlockSpec(shape, lambda *_: (0,) * nd)


def _inproj_kernel(x_ref, g_ref, wqkv_ref, wf_ref, bf_ref, gains_ref, bd_ref, ltri_ref,
                   selq_ref, selk_ref, oneq_ref, onek_ref,
                   qa_ref, ka_ref, va_ref, kaf_ref, vaf_ref, qcat_ref, kcat_ref, vb_ref,
                   kbf_ref, vbf_ref, logf_ref, logf8_ref, carry_ref, *, tiles_per_seq):
    i = pl.program_id(0)

    @pl.when(i % tiles_per_seq == 0)
    def _():
        carry_ref[...] = jnp.zeros_like(carry_ref)

    h = _rms(x_ref[...], g_ref[...]).astype(BF16)

    def seg(s):
        return _dot(h, wqkv_ref[:, s * WIDTH:(s + 1) * WIDTH])

    def headnorm(y, n):
        ss = _dot((y * y).astype(BF16), bd_ref[...])
        return y * lax.rsqrt(ss * (1.0 / HEAD_DIM) + EPS) * gains_ref[n:n + 1, :]

    q_a = headnorm(seg(0), 0)
    qa_ref[...] = (q_a * (HEAD_DIM ** -0.5)).astype(BF16)
    k_a = headnorm(seg(1), 1)
    ka_ref[...] = k_a.astype(BF16)
    kaf_ref[...] = k_a
    v_a = seg(2)
    va_ref[...] = v_a.astype(BF16)
    vaf_ref[...] = v_a

    z = _dot(h, wf_ref[...]) + bf_ref[...]
    logf = jnp.minimum(z, 0.0) - jnp.log(1.0 + jnp.exp(-jnp.abs(z)))
    logf_ref[...] = logf
    logf8_ref[...] = logf[:, :N_HEADS]
    c = _dot3(ltri_ref[...], logf) + carry_ref[0:1, :]
    carry_ref[...] = jnp.broadcast_to(c[-1:, :], carry_ref.shape)
    cs = jnp.concatenate(_split3(c), axis=-1)
    q_aug = (_dot(cs, selq_ref[...]) + oneq_ref[...]).astype(BF16)
    k_aug = (_dot(cs, selk_ref[...]) + onek_ref[...]).astype(BF16)

    q_b = (headnorm(seg(3), 2) * (HEAD_DIM ** -0.5)).astype(BF16)
    k_b = headnorm(seg(4), 3)
    kbf_ref[...] = k_b
    k_b = k_b.astype(BF16)
    v_b = seg(5)
    vbf_ref[...] = v_b
    vb_ref[...] = v_b.astype(BF16)
    for p in range(N_PAIRS):
        cols = slice(p * LANES, (p + 1) * LANES)
        qcat_ref[p, :, 0:LANES] = q_b[:, cols]
        qcat_ref[p, :, LANES:2 * LANES] = q_aug
        kcat_ref[p, :, 0:LANES] = k_b[:, cols]
        kcat_ref[p, :, LANES:2 * LANES] = k_aug


def _aug_constants():
    selq = np.zeros((3 * LANES, LANES), np.float32)
    selk = np.zeros((3 * LANES, LANES), np.float32)
    oneq = np.zeros((1, LANES), np.float32)
    onek = np.zeros((1, LANES), np.float32)
    for h in range(N_HEADS):
        for k in range(3):
            selq[k * LANES + h, AUG_STRIDE * h + k] = 1.0
            selk[k * LANES + h, AUG_STRIDE * h + 3 + k] = -1.0
            oneq[0, AUG_STRIDE * h + 3 + k] = 1.0
            onek[0, AUG_STRIDE * h + k] = 1.0
    return (jnp.asarray(selq, BF16), jnp.asarray(selk, BF16), jnp.asarray(oneq), jnp.asarray(onek))


def _inproj(x, n_seq, w, a_rows_last):
    t_total = x.shape[0]
    tm = 512
    n_tiles = t_total // tm
    seq = t_total // n_seq
    tps = seq // tm
    if a_rows_last:
        assert tm == WINDOW_ROWS
        a_rows = n_seq * WINDOW_ROWS
        a_map = lambda i: (i // tps, 0)
    else:
        a_rows = t_total
        a_map = lambda i: (i, 0)
    row = lambda i: (i, 0)
    cat_map = lambda i: (i // tps, 0, i % tps, 0)
    bd = jnp.asarray(np.kron(np.eye(N_HEADS), np.ones((HEAD_DIM, HEAD_DIM))), BF16)
    ltri = jnp.asarray(np.tril(np.ones((tm, tm))), BF16)
    selq, selk, oneq, onek = _aug_constants()
    out_shape = [
        jax.ShapeDtypeStruct((t_total, WIDTH), BF16),
        jax.ShapeDtypeStruct((t_total, WIDTH), BF16),
        jax.ShapeDtypeStruct((t_total, WIDTH), BF16),
        jax.ShapeDtypeStruct((a_rows, WIDTH), F32),
        jax.ShapeDtypeStruct((a_rows, WIDTH), F32),
        jax.ShapeDtypeStruct((n_seq, N_PAIRS, seq, 2 * LANES), BF16),
        jax.ShapeDtypeStruct((n_seq, N_PAIRS, seq, 2 * LANES), BF16),
        jax.ShapeDtypeStruct((t_total, WIDTH), BF16),
        jax.ShapeDtypeStruct((t_total, WIDTH), F32),
        jax.ShapeDtypeStruct((t_total, WIDTH), F32),
        jax.ShapeDtypeStruct((t_total, LANES), F32),
        jax.ShapeDtypeStruct((t_total, N_HEADS), F32),
    ]
    cat_spec = pl.BlockSpec((None, N_PAIRS, tm, 2 * LANES), cat_map)
    out_specs = [
        pl.BlockSpec((tm, WIDTH), row), pl.BlockSpec((tm, WIDTH), row), pl.BlockSpec((tm, WIDTH), row),
        pl.BlockSpec((tm, WIDTH), a_map), pl.BlockSpec((tm, WIDTH), a_map),
        cat_spec, cat_spec,
        pl.BlockSpec((tm, WIDTH), row), pl.BlockSpec((tm, WIDTH), row), pl.BlockSpec((tm, WIDTH), row),
        pl.BlockSpec((tm, LANES), row), pl.BlockSpec((tm, N_HEADS), row),
    ]
    consts = [w["g_mix"], w["w_qkv"], w["w_f"], w["b_f"], w["gains"], bd, ltri, selq, selk, oneq, onek]
    in_specs = [pl.BlockSpec((tm, D_MODEL), row)] + [_const_spec(c.shape) for c in consts]
    return pl.pallas_call(
        functools.partial(_inproj_kernel, tiles_per_seq=tps),
        grid=(n_tiles,),
        in_specs=in_specs,
        out_specs=out_specs,
        out_shape=out_shape,
        scratch_shapes=[pltpu.VMEM((8, LANES), F32)],
        compiler_params=_params(1),
        name="inproj",
    )(x, *consts)


def _softmax_pv(scores, values):
    m = scores[0].max(axis=-1, keepdims=True)
    for s in scores[1:]:
        m = jnp.maximum(m, s.max(axis=-1, keepdims=True))
    pv = None
    l = None
    for s, v in zip(scores, values):
        p = jnp.exp(s - m)
        ls = p.sum(axis=-1, keepdims=True)
        o = _dot(p.astype(BF16), v)
        pv = o if pv is None else pv + o
        l = ls if l is None else l + ls
    return pv, l


def _head_mask(hh):
    lane = lax.broadcasted_iota(jnp.int32, (1, LANES), 1)
    return (lane // HEAD_DIM) == hh


def _band_prompt_kernel(q_ref, k0_ref, k1_ref, k2_ref, v0_ref, v1_ref, v2_ref, bias_ref, o_ref, *, tq):
    i = pl.program_id(1)
    k_refs = (k0_ref, k1_ref, k2_ref)
    v_refs = (v0_ref, v1_ref, v2_ref)
    pad = [jnp.where(i - 2 + j >= 0, 0.0, NEG) for j in range(2)] + [0.0]
    for p in range(N_PAIRS):
        cols = slice(p * LANES, (p + 1) * LANES)
        q = q_ref[:, cols]
        ks = [r[:, cols] for r in k_refs]
        vs = [r[:, cols] for r in v_refs]
        out = jnp.zeros((tq, LANES), F32)
        for hh in range(2):
            msk = _head_mask(hh)
            qh = jnp.where(msk, q, jnp.zeros_like(q))
            scores = [_dot_nt(qh, ks[j]) + bias_ref[2 * p + hh, :, j * tq:(j + 1) * tq] + pad[j]
                      for j in range(3)]
            pv, l = _softmax_pv(scores, vs)
            out = jnp.where(msk, pv / l, out)
        o_ref[:, cols] = out.astype(BF16)


def _band_prompt(qa, ka, va, bias, n_seq):
    t_total = qa.shape[0]
    tq = 256
    seq = t_total // n_seq
    nq = seq // tq
    qmap = lambda b, i: (b * nq + i, 0)

    def kmap(j):
        return lambda b, i: (b * nq + jnp.maximum(i - 2 + j, 0), 0)

    blk = lambda m: pl.BlockSpec((tq, WIDTH), m)
    return pl.pallas_call(
        functools.partial(_band_prompt_kernel, tq=tq),
        grid=(n_seq, nq),
        in_specs=[blk(qmap)] + [blk(kmap(j)) for j in range(3)] + [blk(kmap(j)) for j in range(3)]
        + [_const_spec(bias.shape)],
        out_specs=blk(qmap),
        out_shape=jax.ShapeDtypeStruct((t_total, WIDTH), BF16),
        compiler_params=_params(2),
        name="band_prompt",
    )(qa, ka, ka, ka, va, va, va, bias)


def _band_sample_kernel(q_ref, kc_ref, vc_ref, kn_ref, vn_ref, bias_ref, o_ref, ko_ref, vo_ref):
    p_rows = kc_ref.shape[0]
    t = q_ref.shape[0]
    for p in range(N_PAIRS):
        cols = slice(p * LANES, (p + 1) * LANES)
        q = q_ref[:, cols]
        ks = [kc_ref[:, cols].astype(BF16), kn_ref[:, cols].astype(BF16)]
        vs = [vc_ref[:, cols].astype(BF16), vn_ref[:, cols].astype(BF16)]
        out = jnp.zeros((t, LANES), F32)
        for hh in range(2):
            msk = _head_mask(hh)
            qh = jnp.where(msk, q, jnp.zeros_like(q))
            scores = [_dot_nt(qh, ks[0]) + bias_ref[2 * p + hh, :, 0:p_rows],
                      _dot_nt(qh, ks[1]) + bias_ref[2 * p + hh, :, p_rows:p_rows + t]]
            pv, l = _softmax_pv(scores, vs)
            out = jnp.where(msk, pv / l, out)
        o_ref[:, cols] = out.astype(BF16)
    ko_ref[0:p_rows - t, :] = kc_ref[t:p_rows, :]
    ko_ref[p_rows - t:p_rows, :] = kn_ref[...]
    vo_ref[0:p_rows - t, :] = vc_ref[t:p_rows, :]
    vo_ref[p_rows - t:p_rows, :] = vn_ref[...]


def _band_sample(qa, ka_f, va_f, cache_k, cache_v, bias):
    n, p_rows, _ = cache_k.shape
    t = qa.shape[0] // n
    row = pl.BlockSpec((t, WIDTH), lambda b: (b, 0))
    cache = pl.BlockSpec((None, p_rows, WIDTH), lambda b: (b, 0, 0))
    return pl.pallas_call(
        _band_sample_kernel,
        grid=(n,),
        in_specs=[row, cache, cache, row, row, _const_spec(bias.shape)],
        out_specs=[row, cache, cache],
        out_shape=[jax.ShapeDtypeStruct(qa.shape, BF16),
                   jax.ShapeDtypeStruct(cache_k.shape, F32),
                   jax.ShapeDtypeStruct(cache_v.shape, F32)],
        compiler_params=_params(1),
        name="band_sample",
    )(qa, cache_k, cache_v, ka_f, va_f, bias)


def _band_bias(rel_bias, n_q, n_k, offset, band_mask):
    qi = np.arange(n_q)[:, None]
    kj = np.arange(n_k)[None, :]
    idx = np.clip(qi + offset - kj, -MAX_REL, MAX_REL) + MAX_REL
    bias = rel_bias[:, idx].astype(F32)
    if band_mask:
        rel_chunk = (qi + offset) // CHUNK - kj // CHUNK
        valid = (rel_chunk >= 0) & (rel_chunk <= LEFT_CHUNKS)
        bias = jnp.where(jnp.asarray(valid)[None], bias, NEG)
    return bias


def _aug_head_mask(h_in_pair, pair):
    lane = lax.broadcasted_iota(jnp.int32, (1, 2 * LANES), 1)
    head = 2 * pair + h_in_pair
    in_q = (lane < LANES) & ((lane // HEAD_DIM) == h_in_pair)
    in_aug = (lane >= LANES) & (((lane - LANES) // AUG_STRIDE) == head)
    return in_q | in_aug


def _fox_prompt_kernel(qi_ref, kj_ref, q_ref, k_ref, v_ref, o_ref, qh_sc, m_sc, l_sc, acc_sc, *, tq, tk):
    pair = pl.program_id(1)
    t = pl.program_id(2)
    i = qi_ref[t]
    j = kj_ref[t]

    @pl.when(j == 0)
    def _():
        q = q_ref[...]
        for hh in range(2):
            qh_sc[hh] = jnp.where(_aug_head_mask(hh, pair), q, jnp.zeros_like(q))
        m_sc[...] = jnp.full_like(m_sc, NEG)
        l_sc[...] = jnp.zeros_like(l_sc)
        acc_sc[...] = jnp.zeros_like(acc_sc)

    def step(diagonal):
        k = k_ref[...]
        v = v_ref[...]
        for hh in range(2):
            s = _dot_nt(qh_sc[hh], k)
            if diagonal:
                r = lax.broadcasted_iota(jnp.int32, (tq, tk), 0)
                c = lax.broadcasted_iota(jnp.int32, (tq, tk), 1)
                s = jnp.where(c <= r, s, NEG)
            m_prev = m_sc[hh]
            m_new = jnp.maximum(m_prev, s.max(axis=-1, keepdims=True))
            alpha = jnp.exp(m_prev - m_new)
            p = jnp.exp(s - m_new)
            l_sc[hh] = alpha * l_sc[hh] + p.sum(axis=-1, keepdims=True)
            acc_sc[hh] = alpha * acc_sc[hh] + _dot(p.astype(BF16), v)
            m_sc[hh] = m_new

    @pl.when(j < i)
    def _():
        step(False)

    @pl.when(j == i)
    def _():
        step(True)
        o = jnp.where(_head_mask(0), acc_sc[0] / l_sc[0], acc_sc[1] / l_sc[1])
        o_ref[...] = o.astype(BF16)


def _fox_prompt(qcat, kcat, vb, n_seq):
    seq = qcat.shape[2]
    tq = tk = 512
    nt = seq // tq
    qi = np.concatenate([np.full(i + 1, i) for i in range(nt)]).astype(np.int32)
    kj = np.concatenate([np.arange(i + 1) for i in range(nt)]).astype(np.int32)
    grid_spec = pltpu.PrefetchScalarGridSpec(
        num_scalar_prefetch=2,
        grid=(n_seq, N_PAIRS, len(qi)),
        in_specs=[
            pl.BlockSpec((None, None, tq, 2 * LANES), lambda b, p, t, qi, kj: (b, p, qi[t], 0)),
            pl.BlockSpec((None, None, tk, 2 * LANES), lambda b, p, t, qi, kj: (b, p, kj[t], 0)),
            pl.BlockSpec((tk, LANES), lambda b, p, t, qi, kj: (b * nt + kj[t], p)),
        ],
        out_specs=pl.BlockSpec((tq, LANES), lambda b, p, t, qi, kj: (b * nt + qi[t], p)),
        scratch_shapes=[pltpu.VMEM((2, tq, 2 * LANES), BF16), pltpu.VMEM((2, tq, 1), F32),
                        pltpu.VMEM((2, tq, 1), F32), pltpu.VMEM((2, tq, LANES), F32)],
    )
    return pl.pallas_call(
        functools.partial(_fox_prompt_kernel, tq=tq, tk=tk),
        grid_spec=grid_spec,
        out_shape=jax.ShapeDtypeStruct(vb.shape, BF16),
        compiler_params=_params(3),
        name="fox_prompt",
    )(jnp.asarray(qi), jnp.asarray(kj), qcat, kcat, vb)


def _fox_sample_kernel(q_ref, kc_ref, vc_ref, kn_ref, vn_ref, lfc_ref, lfn_ref, sl_ref, u_ref,
                       o_ref, cq_sc, car_sc, m_sc, l_sc, acc_sc, *, n_cache_tiles):
    j = pl.program_id(1)
    t = q_ref.shape[1]

    def cum_new():
        hi, mid, lo = _split3(lfn_ref[...])
        u = u_ref[...]
        return _dot(hi, u) + _dot(mid, u) + _dot(lo, u)

    def to_col(row):
        eye = (lax.broadcasted_iota(jnp.int32, (t, t), 0) == lax.broadcasted_iota(jnp.int32, (t, t), 1))
        return jnp.sum(jnp.where(eye, jnp.broadcast_to(row, (t, t)), 0.0), axis=-1, keepdims=True)

    def update(h, s, v):
        m_prev = m_sc[h]
        m_new = jnp.maximum(m_prev, s.max(axis=-1, keepdims=True))
        alpha = jnp.exp(m_prev - m_new)
        p = jnp.exp(s - m_new)
        l_sc[h] = alpha * l_sc[h] + p.sum(axis=-1, keepdims=True)
        acc_sc[h] = alpha * acc_sc[h] + _dot(p.astype(BF16), v)
        m_sc[h] = m_new

    @pl.when(j == 0)
    def _():
        cn = cum_new()
        for h in range(N_HEADS):
            cq_sc[h] = to_col(cn[h:h + 1, :])
        car_sc[...] = jnp.zeros_like(car_sc)
        m_sc[...] = jnp.full_like(m_sc, NEG)
        l_sc[...] = jnp.zeros_like(l_sc)
        acc_sc[...] = jnp.zeros_like(acc_sc)

    @pl.when(j < n_cache_tiles)
    def _():
        hi, mid, lo = _split3(lfc_ref[...])
        sl = sl_ref[...]
        within = _dot(hi, sl) + _dot(mid, sl) + _dot(lo, sl)
        r = within + car_sc[:, 0:1]
        car_sc[...] = jnp.broadcast_to(r[:, 0:1] + lfc_ref[:, 0:1], car_sc.shape)
        for p in range(N_PAIRS):
            cols = slice(p * LANES, (p + 1) * LANES)
            q = q_ref[p, :, 0:LANES]
            k = kc_ref[:, cols].astype(BF16)
            v = vc_ref[:, cols].astype(BF16)
            for hh in range(2):
                h = 2 * p + hh
                qh = jnp.where(_head_mask(hh), q, jnp.zeros_like(q))
                s = _dot_nt(qh, k) + cq_sc[h] + r[h:h + 1, :]
                update(h, s, v)

    @pl.when(j == n_cache_tiles)
    def _():
        cn = cum_new()
        causal = (lax.broadcasted_iota(jnp.int32, (t, t), 1) <= lax.broadcasted_iota(jnp.int32, (t, t), 0))
        for p in range(N_PAIRS):
            cols = slice(p * LANES, (p + 1) * LANES)
            q = q_ref[p, :, 0:LANES]
            k = kn_ref[p, :, 0:LANES]
            v = vn_ref[:, cols]
            out = jnp.zeros((t, LANES), F32)
            for hh in range(2):
                h = 2 * p + hh
                qh = jnp.where(_head_mask(hh), q, jnp.zeros_like(q))
                s = _dot_nt(qh, k) + cq_sc[h] - cn[h:h + 1, :]
                s = jnp.where(causal, s, NEG)
                update(h, s, v)
                out = jnp.where(_head_mask(hh), acc_sc[h] / l_sc[h], out)
            o_ref[:, cols] = out.astype(BF16)


def _fox_sample(qcat, kcat, vb, cache_k, cache_v, lf_cache_t, lf_new_t):
    n, p_rows, _ = cache_k.shape
    t = vb.shape[0] // n
    tk = 1024
    nct = p_rows // tk
    sl = jnp.asarray(np.tril(np.ones((tk, tk)), -1), BF16)
    u = jnp.asarray(np.triu(np.ones((t, t))), BF16)
    cmap = lambda b, j: (b, jnp.maximum(nct - 1 - j, 0), 0)
    new_cat = pl.BlockSpec((None, N_PAIRS, t, 2 * LANES), lambda b, j: (0, 0, b, 0))
    return pl.pallas_call(
        functools.partial(_fox_sample_kernel, n_cache_tiles=nct),
        grid=(n, nct + 1),
        in_specs=[
            new_cat,
            pl.BlockSpec((None, tk, WIDTH), cmap), pl.BlockSpec((None, tk, WIDTH), cmap),
            new_cat,
            pl.BlockSpec((t, WIDTH), lambda b, j: (b, 0)),
            pl.BlockSpec((None, N_HEADS, tk), lambda b, j: (b, 0, jnp.maximum(nct - 1 - j, 0))),
            pl.BlockSpec((None, N_HEADS, t), lambda b, j: (b, 0, 0)),
            _const_spec(sl.shape), _const_spec(u.shape),
        ],
        out_specs=pl.BlockSpec((t, WIDTH), lambda b, j: (b, 0)),
        out_shape=jax.ShapeDtypeStruct(vb.shape, BF16),
        scratch_shapes=[pltpu.VMEM((N_HEADS, t, 1), F32), pltpu.VMEM((N_HEADS, LANES), F32),
                        pltpu.VMEM((N_HEADS, t, 1), F32), pltpu.VMEM((N_HEADS, t, 1), F32),
                        pltpu.VMEM((N_HEADS, t, LANES), F32)],
        compiler_params=_params(2),
        name="fox_sample",
    )(qcat, cache_k, cache_v, kcat, vb, lf_cache_t, lf_new_t, sl, u)


def _merge_kernel(x_ref, oa_ref, ob_ref, g_ref, wg_ref, wpa_ref, wpb_ref, wo_ref, y_ref):
    x = x_ref[...]
    h = _rms(x, g_ref[...]).astype(BF16)
    gate = jax.nn.sigmoid(_dot(h, wg_ref[...]))
    mix = (gate[:, :D_MODEL] * _dot(oa_ref[...], wpa_ref[...])
           + gate[:, D_MODEL:] * _dot(ob_ref[...], wpb_ref[...]))
    y_ref[...] = x + _dot(mix.astype(BF16), wo_ref[...])


def _merge(x, oa, ob, w):
    t_total = x.shape[0]
    tm = 512
    row = lambda n: pl.BlockSpec((tm, n), lambda i: (i, 0))
    consts = [w["g_mix"], w["w_gate"], w["w_pa"], w["w_pb"], w["w_o"]]
    return pl.pallas_call(
        _merge_kernel,
        grid=(t_total // tm,),
        in_specs=[row(D_MODEL), row(WIDTH), row(WIDTH)] + [_const_spec(c.shape) for c in consts],
        out_specs=row(D_MODEL),
        out_shape=jax.ShapeDtypeStruct(x.shape, F32),
        compiler_params=_params(1),
        name="merge",
    )(x, oa, ob, *consts)


def _route(r):
    lane_i = lax.broadcasted_iota(jnp.int32, r.shape, 1)
    lane = lane_i.astype(F32)
    lane_grp = (lane_i // EXPERTS_PER_GROUP).astype(F32)
    big = float(LANES)
    is_coarse = (lane_i >= N_EXPERTS) & (lane_i < N_EXPERTS + N_GROUPS)
    cm = jnp.where(is_coarse, r, NEG)
    cmax = cm.max(axis=-1, keepdims=True)
    grp = jnp.min(jnp.where(cm == cmax, lane - N_EXPERTS, big), axis=-1, keepdims=True)
    pg_sel = 1.0 / jnp.sum(jnp.exp(cm - cmax), axis=-1, keepdims=True)
    in_grp = (lane_i < N_EXPERTS) & (lane_grp == grp)
    fm = jnp.where(in_grp, r, NEG)
    m1 = fm.max(axis=-1, keepdims=True)
    denom = jnp.sum(jnp.exp(fm - m1), axis=-1, keepdims=True)
    i1 = jnp.min(jnp.where(fm == m1, lane, big), axis=-1, keepdims=True)
    fm2 = jnp.where(lane == i1, NEG, fm)
    m2 = fm2.max(axis=-1, keepdims=True)
    i2 = jnp.min(jnp.where(fm2 == m2, lane, big), axis=-1, keepdims=True)
    p1 = 1.0 / denom
    p2 = jnp.exp(m2 - m1) / denom
    tot = p1 + p2
    return (jnp.where(lane == i1, pg_sel * (p1 / tot), 0.0)
            + jnp.where(lane == i2, pg_sel * (p2 / tot), 0.0))


def _moe_kernel(x_ref, g_ref, wr_ref, br_ref, w1_ref, w3_ref, w2_ref, y_ref, hx_sc, comb_sc, *, e_blk):
    j = pl.program_id(1)

    @pl.when(j == 0)
    def _():
        x = x_ref[...]
        hx = _rms(x, g_ref[...])
        hx_sc[...] = hx.astype(BF16)
        h_hi, h_mid, _ = _split3(hx)
        w_hi, w_mid, _ = _split3(wr_ref[...])
        r = _dot(h_hi, w_hi) + _dot(h_hi, w_mid) + _dot(h_mid, w_hi) + br_ref[...]
        comb_sc[...] = _route(r)
        y_ref[...] = x

    hx = hx_sc[...]
    hid = jax.nn.silu(_dot(hx, w1_ref[...])) * _dot(hx, w3_ref[...])
    comb = comb_sc[...]
    lane = lax.broadcasted_iota(jnp.int32, comb.shape, 1)
    scale = []
    for e in range(e_blk):
        w_e = jnp.sum(jnp.where(lane == j * e_blk + e, comb, 0.0), axis=-1, keepdims=True)
        scale.append(jnp.broadcast_to(w_e, (comb.shape[0], D_EXPERT)))
    hid = hid * jnp.concatenate(scale, axis=-1)
    y_ref[...] += _dot(hid.astype(BF16), w2_ref[...])


def _moe(x, w):
    t_total = x.shape[0]
    tm = 512
    e_blk = 4
    cols = e_blk * D_EXPERT
    consts = [w["g_ffn"], w["w_router"], w["b_router"]]
    return pl.pallas_call(
        functools.partial(_moe_kernel, e_blk=e_blk),
        grid=(t_total // tm, N_EXPERTS // e_blk),
        in_specs=[pl.BlockSpec((tm, D_MODEL), lambda i, j: (i, 0))]
        + [_const_spec(c.shape) for c in consts]
        + [pl.BlockSpec((D_MODEL, cols), lambda i, j: (0, j)),
           pl.BlockSpec((D_MODEL, cols), lambda i, j: (0, j)),
           pl.BlockSpec((cols, D_MODEL), lambda i, j: (j, 0))],
        out_specs=pl.BlockSpec((tm, D_MODEL), lambda i, j: (i, 0)),
        out_shape=jax.ShapeDtypeStruct(x.shape, F32),
        scratch_shapes=[pltpu.VMEM((tm, D_MODEL), BF16), pltpu.VMEM((tm, LANES), F32)],
        compiler_params=_params(2),
        name="moe",
    )(x, *consts, w["w1"], w["w3"], w["w2"])


def _prep_weights(g_mix, w_in, b_f, q_norm_a, k_norm_a, q_norm_b, k_norm_b, w_pa, w_pb, w_o,
                  g_ffn, w_rg, b_rg, w_re, b_re, w1, w3, w2):
    n_qkv = 6 * WIDTH
    tile = lambda g: jnp.tile(g, N_HEADS)
    w_router = jnp.concatenate(
        [jnp.transpose(w_re, (1, 0, 2)).reshape(D_MODEL, N_EXPERTS), w_rg,
         jnp.zeros((D_MODEL, LANES - N_EXPERTS - N_GROUPS), F32)], axis=1)
    b_router = jnp.concatenate(
        [b_re.reshape(N_EXPERTS), b_rg, jnp.zeros((LANES - N_EXPERTS - N_GROUPS,), F32)])[None, :]
    return {
        "g_mix": g_mix[None, :],
        "w_qkv": w_in[:, :n_qkv].astype(BF16),
        "w_f": jnp.pad(w_in[:, n_qkv:n_qkv + N_HEADS], ((0, 0), (0, LANES - N_HEADS))).astype(BF16),
        "b_f": jnp.pad(b_f, (0, LANES - N_HEADS))[None, :],
        "w_gate": w_in[:, n_qkv + N_HEADS:].astype(BF16),
        "gains": jnp.stack([tile(q_norm_a), tile(k_norm_a), tile(q_norm_b), tile(k_norm_b)]),
        "w_pa": w_pa.astype(BF16), "w_pb": w_pb.astype(BF16), "w_o": w_o.astype(BF16),
        "g_ffn": g_ffn[None, :],
        "w_router": w_router, "b_router": b_router,
        "w1": jnp.transpose(w1, (1, 0, 2)).reshape(D_MODEL, N_EXPERTS * D_EXPERT).astype(BF16),
        "w3": jnp.transpose(w3, (1, 0, 2)).reshape(D_MODEL, N_EXPERTS * D_EXPERT).astype(BF16),
        "w2": w2.reshape(N_EXPERTS * D_EXPERT, D_MODEL).astype(BF16),
    }


def kernel(x_prompt, x_sample, cache_a_k, cache_a_v, cache_b_k, cache_b_v, cache_b_logf, g_mix, w_in, b_f, q_norm_a, k_norm_a, q_norm_b, k_norm_b, rel_bias, w_pa, w_pb, w_o, g_ffn, w_rg, b_rg, w_re, b_re, w1, w3, w2):
    assert g_mix.shape[0] == 1, "single-layer step"
    n_p, seq, _ = x_prompt.shape
    n_s, t_s, _ = x_sample.shape
    a_rows = cache_a_k.shape[2]
    past = cache_b_k.shape[2]
    w = _prep_weights(g_mix[0], w_in[0], b_f[0], q_norm_a[0], k_norm_a[0], q_norm_b[0], k_norm_b[0],
                      w_pa[0], w_pb[0], w_o[0], g_ffn[0], w_rg[0], b_rg[0], w_re[0], b_re[0],
                      w1[0], w3[0], w2[0])
    band_tq = 256
    bias_prompt = _band_bias(rel_bias[0], band_tq, 3 * band_tq, 2 * band_tq, band_mask=True)
    bias_sample = _band_bias(rel_bias[0], t_s, a_rows + t_s, a_rows, band_mask=False)

    xp = x_prompt.reshape(n_p * seq, D_MODEL)
    (qa, ka, va, ka_f, va_f, qcat, kcat, vb, kb_f, vb_f, _, logf) = _inproj(xp, n_p, w, a_rows_last=True)
    o_a = _band_prompt(qa, ka, va, bias_prompt, n_p)
    o_b = _fox_prompt(qcat, kcat, vb, n_p)
    y_p = _moe(_merge(xp, o_a, o_b, w), w)

    xs = x_sample.reshape(n_s * t_s, D_MODEL)
    (qa_s, _, _, ka_fs, va_fs, qcat_s, kcat_s, vb_s, kb_fs, vb_fs, _, logf_s) = _inproj(
        xs, 1, w, a_rows_last=False)
    o_as, new_ak, new_av = _band_sample(
        qa_s, ka_fs, va_fs, cache_a_k[0].reshape(n_s, a_rows, WIDTH),
        cache_a_v[0].reshape(n_s, a_rows, WIDTH), bias_sample)
    lf_cache_t = jnp.transpose(cache_b_logf[0], (0, 2, 1))
    lf_new_t = jnp.transpose(logf_s.reshape(n_s, t_s, N_HEADS), (0, 2, 1))
    o_bs = _fox_sample(qcat_s, kcat_s, vb_s, cache_b_k[0].reshape(n_s, past, WIDTH),
                       cache_b_v[0].reshape(n_s, past, WIDTH), lf_cache_t, lf_new_t)
    y_s = _moe(_merge(xs, o_as, o_bs, w), w)

    heads = lambda a, n, r: a.reshape(1, n, r, N_HEADS, HEAD_DIM)
    rows_p = min(WINDOW_ROWS, seq)
    return (y_p.reshape(n_p, seq, D_MODEL), y_s.reshape(n_s, t_s, D_MODEL),
            heads(ka_f, n_p, rows_p), heads(va_f, n_p, rows_p),
            heads(kb_f, n_p, seq), heads(vb_f, n_p, seq), logf.reshape(1, n_p, seq, N_HEADS),
            heads(new_ak, n_s, a_rows), heads(new_av, n_s, a_rows),
            heads(kb_fs, n_s, t_s), heads(vb_fs, n_s, t_s), logf_s.reshape(1, n_s, t_s, N_HEADS))
```

```python
import functools

import numpy as np
import jax
import jax.numpy as jnp
from jax import lax
from jax.experimental import pallas as pl
from jax.experimental.pallas import tpu as pltpu

F32 = jnp.float32
BF16 = jnp.bfloat16

D_MODEL = 1024
HEAD_DIM = 64
N_HEADS = 8
WIDTH = N_HEADS * HEAD_DIM
N_PAIRS = N_HEADS // 2
CHUNK = 64
LEFT_CHUNKS = 8
WINDOW_ROWS = LEFT_CHUNKS * CHUNK
MAX_REL = 256
N_GROUPS = 4
EXPERTS_PER_GROUP = 8
N_EXPERTS = N_GROUPS * EXPERTS_PER_GROUP
D_EXPERT = 128
EPS = 1e-6
NEG = -1e30
LANES = 128
AUG_STRIDE = 8
VMEM_LIMIT = 56 * 1024 * 1024

_NT = (((1,), (1,)), ((), ()))


def _dot(a, b):
    return jnp.dot(a, b, preferred_element_type=F32)


def _dot_nt(a, b):
    return lax.dot_general(a, b, _NT, preferred_element_type=F32)


def _split3(x):
    hi = x.astype(BF16)
    r = x - hi.astype(F32)
    mid = r.astype(BF16)
    lo = (r - mid.astype(F32)).astype(BF16)
    return hi, mid, lo


def _dot3(a_bf, x):
    hi, mid, lo = _split3(x)
    return _dot(a_bf, hi) + _dot(a_bf, mid) + _dot(a_bf, lo)


def _rms(x, g):
    ms = jnp.mean(x * x, axis=-1, keepdims=True)
    return x * lax.rsqrt(ms + EPS) * g


def _params(n_axes):
    return pltpu.CompilerParams(dimension_semantics=("arbitrary",) * n_axes,
                                vmem_limit_bytes=VMEM_LIMIT)


def _const_spec(shape):
    nd = len(shape)
    return pl.BlockSpec(shape, lambda *_: (0,) * nd)


def _inproj_kernel(x_ref, g_ref, wqkv_ref, wf_ref, bf_ref, gains_ref, bd_ref, ltri_ref,
                   selq_ref, selk_ref, oneq_ref, onek_ref,
                   qa_ref, ka_ref, va_ref, kaf_ref, vaf_ref, qcat_ref, kcat_ref, vb_ref,
                   kbf_ref, vbf_ref, logf_ref, logf8_ref, carry_ref, *, tiles_per_seq):
    i = pl.program_id(0)

    @pl.when(i % tiles_per_seq == 0)
    def _():
        carry_ref[...] = jnp.zeros_like(carry_ref)

    h = _rms(x_ref[...], g_ref[...]).astype(BF16)

    def seg(s):
        return _dot(h, wqkv_ref[:, s * WIDTH:(s + 1) * WIDTH])

    def headnorm(y, n):
        ss = _dot((y * y).astype(BF16), bd_ref[...])
        return y * lax.rsqrt(ss * (1.0 / HEAD_DIM) + EPS) * gains_ref[n:n + 1, :]

    q_a = headnorm(seg(0), 0)
    qa_ref[...] = (q_a * (HEAD_DIM ** -0.5)).astype(BF16)
    k_a = headnorm(seg(1), 1)
    ka_ref[...] = k_a.astype(BF16)
    kaf_ref[...] = k_a
    v_a = seg(2)
    va_ref[...] = v_a.astype(BF16)
    vaf_ref[...] = v_a

    z = _dot(h, wf_ref[...]) + bf_ref[...]
    logf = jnp.minimum(z, 0.0) - jnp.log(1.0 + jnp.exp(-jnp.abs(z)))
    logf_ref[...] = logf
    logf8_ref[...] = logf[:, :N_HEADS]
    c = _dot3(ltri_ref[...], logf) + carry_ref[0:1, :]
    carry_ref[...] = jnp.broadcast_to(c[-1:, :], carry_ref.shape)
    cs = jnp.concatenate(_split3(c), axis=-1)
    q_aug = (_dot(cs, selq_ref[...]) + oneq_ref[...]).astype(BF16)
    k_aug = (_dot(cs, selk_ref[...]) + onek_ref[...]).astype(BF16)

    q_b = (headnorm(seg(3), 2) * (HEAD_DIM ** -0.5)).astype(BF16)
    k_b = headnorm(seg(4), 3)
    kbf_ref[...] = k_b
    k_b = k_b.astype(BF16)
    v_b = seg(5)
    vbf_ref[...] = v_b
    vb_ref[...] = v_b.astype(BF16)
    for p in range(N_PAIRS):
        cols = slice(p * LANES, (p + 1) * LANES)
        qcat_ref[p, :, 0:LANES] = q_b[:, cols]
        qcat_ref[p, :, LANES:2 * LANES] = q_aug
        kcat_ref[p, :, 0:LANES] = k_b[:, cols]
        kcat_ref[p, :, LANES:2 * LANES] = k_aug


def _aug_constants():
    selq = np.zeros((3 * LANES, LANES), np.float32)
    selk = np.zeros((3 * LANES, LANES), np.float32)
    oneq = np.zeros((1, LANES), np.float32)
    onek = np.zeros((1, LANES), np.float32)
    for h in range(N_HEADS):
        for k in range(3):
            selq[k * LANES + h, AUG_STRIDE * h + k] = 1.0
            selk[k * LANES + h, AUG_STRIDE * h + 3 + k] = -1.0
            oneq[0, AUG_STRIDE * h + 3 + k] = 1.0
            onek[0, AUG_STRIDE * h + k] = 1.0
    return (jnp.asarray(selq, BF16), jnp.asarray(selk, BF16), jnp.asarray(oneq), jnp.asarray(onek))


def _inproj(x, n_seq, w, a_rows_last):
    t_total = x.shape[0]
    tm = 512
    n_tiles = t_total // tm
    seq = t_total // n_seq
    tps = seq // tm
    if a_rows_last:
        assert tm == WINDOW_ROWS
        a_rows = n_seq * WINDOW_ROWS
        a_map = lambda i: (i // tps, 0)
    else:
        a_rows = t_total
        a_map = lambda i: (i, 0)
    row = lambda i: (i, 0)
    cat_map = lambda i: (i // tps, 0, i % tps, 0)
    bd = jnp.asarray(np.kron(np.eye(N_HEADS), np.ones((HEAD_DIM, HEAD_DIM))), BF16)
    ltri = jnp.asarray(np.tril(np.ones((tm, tm))), BF16)
    selq, selk, oneq, onek = _aug_constants()
    out_shape = [
        jax.ShapeDtypeStruct((t_total, WIDTH), BF16),
        jax.ShapeDtypeStruct((t_total, WIDTH), BF16),
        jax.ShapeDtypeStruct((t_total, WIDTH), BF16),
        jax.ShapeDtypeStruct((a_rows, WIDTH), F32),
        jax.ShapeDtypeStruct((a_rows, WIDTH), F32),
        jax.ShapeDtypeStruct((n_seq, N_PAIRS, seq, 2 * LANES), BF16),
        jax.ShapeDtypeStruct((n_seq, N_PAIRS, seq, 2 * LANES), BF16),
        jax.ShapeDtypeStruct((t_total, WIDTH), BF16),
        jax.ShapeDtypeStruct((t_total, WIDTH), F32),
        jax.ShapeDtypeStruct((t_total, WIDTH), F32),
        jax.ShapeDtypeStruct((t_total, LANES), F32),
        jax.ShapeDtypeStruct((t_total, N_HEADS), F32),
    ]
    cat_spec = pl.BlockSpec((None, N_PAIRS, tm, 2 * LANES), cat_map)
    out_specs = [
        pl.BlockSpec((tm, WIDTH), row), pl.BlockSpec((tm, WIDTH), row), pl.BlockSpec((tm, WIDTH), row),
        pl.BlockSpec((tm, WIDTH), a_map), pl.BlockSpec((tm, WIDTH), a_map),
        cat_spec, cat_spec,
        pl.BlockSpec((tm, WIDTH), row), pl.BlockSpec((tm, WIDTH), row), pl.BlockSpec((tm, WIDTH), row),
        pl.BlockSpec((tm, LANES), row), pl.BlockSpec((tm, N_HEADS), row),
    ]
    consts = [w["g_mix"], w["w_qkv"], w["w_f"], w["b_f"], w["gains"], bd, ltri, selq, selk, oneq, onek]
    in_specs = [pl.BlockSpec((tm, D_MODEL), row)] + [_const_spec(c.shape) for c in consts]
    return pl.pallas_call(
        functools.partial(_inproj_kernel, tiles_per_seq=tps),
        grid=(n_tiles,),
        in_specs=in_specs,
        out_specs=out_specs,
        out_shape=out_shape,
        scratch_shapes=[pltpu.VMEM((8, LANES), F32)],
        compiler_params=_params(1),
        name="inproj",
    )(x, *consts)


def _softmax_pv(scores, values):
    m = scores[0].max(axis=-1, keepdims=True)
    for s in scores[1:]:
        m = jnp.maximum(m, s.max(axis=-1, keepdims=True))
    pv = None
    l = None
    for s, v in zip(scores, values):
        p = jnp.exp(s - m)
        ls = p.sum(axis=-1, keepdims=True)
        o = _dot(p.astype(BF16), v)
        pv = o if pv is None else pv + o
        l = ls if l is None else l + ls
    return pv, l


def _head_mask(hh):
    lane = lax.broadcasted_iota(jnp.int32, (1, LANES), 1)
    return (lane // HEAD_DIM) == hh


def _band_prompt_kernel(q_ref, k0_ref, k1_ref, k2_ref, v0_ref, v1_ref, v2_ref, bias_ref, o_ref, *, tq):
    i = pl.program_id(1)
    k_refs = (k0_ref, k1_ref, k2_ref)
    v_refs = (v0_ref, v1_ref, v2_ref)
    pad = [jnp.where(i - 2 + j >= 0, 0.0, NEG) for j in range(2)] + [0.0]
    for p in range(N_PAIRS):
        cols = slice(p * LANES, (p + 1) * LANES)
        q = q_ref[:, cols]
        ks = [r[:, cols] for r in k_refs]
        vs = [r[:, cols] for r in v_refs]
        out = jnp.zeros((tq, LANES), F32)
        for hh in range(2):
            msk = _head_mask(hh)
            qh = jnp.where(msk, q, jnp.zeros_like(q))
            scores = [_dot_nt(qh, ks[j]) + bias_ref[2 * p + hh, :, j * tq:(j + 1) * tq] + pad[j]
                      for j in range(3)]
            pv, l = _softmax_pv(scores, vs)
            out = jnp.where(msk, pv / l, out)
        o_ref[:, cols] = out.astype(BF16)


def _band_prompt(qa, ka, va, bias, n_seq):
    t_total = qa.shape[0]
    tq = 256
    seq = t_total // n_seq
    nq = seq // tq
    qmap = lambda b, i: (b * nq + i, 0)

    def kmap(j):
        return lambda b, i: (b * nq + jnp.maximum(i - 2 + j, 0), 0)

    blk = lambda m: pl.BlockSpec((tq, WIDTH), m)
    return pl.pallas_call(
        functools.partial(_band_prompt_kernel, tq=tq),
        grid=(n_seq, nq),
        in_specs=[blk(qmap)] + [blk(kmap(j)) for j in range(3)] + [blk(kmap(j)) for j in range(3)]
        + [_const_spec(bias.shape)],
        out_specs=blk(qmap),
        out_shape=jax.ShapeDtypeStruct((t_total, WIDTH), BF16),
        compiler_params=_params(2),
        name="band_prompt",
    )(qa, ka, ka, ka, va, va, va, bias)


def _band_sample_kernel(q_ref, kc_ref, vc_ref, kn_ref, vn_ref, bias_ref, o_ref, ko_ref, vo_ref):
    p_rows = kc_ref.shape[0]
    t = q_ref.shape[0]
    for p in range(N_PAIRS):
        cols = slice(p * LANES, (p + 1) * LANES)
        q = q_ref[:, cols]
        ks = [kc_ref[:, cols].astype(BF16), kn_ref[:, cols].astype(BF16)]
        vs = [vc_ref[:, cols].astype(BF16), vn_ref[:, cols].astype(BF16)]
        out = jnp.zeros((t, LANES), F32)
        for hh in range(2):
            msk = _head_mask(hh)
            qh = jnp.where(msk, q, jnp.zeros_like(q))
            scores = [_dot_nt(qh, ks[0]) + bias_ref[2 * p + hh, :, 0:p_rows],
                      _dot_nt(qh, ks[1]) + bias_ref[2 * p + hh, :, p_rows:p_rows + t]]
            pv, l = _softmax_pv(scores, vs)
            out = jnp.where(msk, pv / l, out)
        o_ref[:, cols] = out.astype(BF16)
    ko_ref[0:p_rows - t, :] = kc_ref[t:p_rows, :]
    ko_ref[p_rows - t:p_rows, :] = kn_ref[...]
    vo_ref[0:p_rows - t, :] = vc_ref[t:p_rows, :]
    vo_ref[p_rows - t:p_rows, :] = vn_ref[...]


def _band_sample(qa, ka_f, va_f, cache_k, cache_v, bias):
    n, p_rows, _ = cache_k.shape
    t = qa.shape[0] // n
    row = pl.BlockSpec((t, WIDTH), lambda b: (b, 0))
    cache = pl.BlockSpec((None, p_rows, WIDTH), lambda b: (b, 0, 0))
    return pl.pallas_call(
        _band_sample_kernel,
        grid=(n,),
        in_specs=[row, cache, cache, row, row, _const_spec(bias.shape)],
        out_specs=[row, cache, cache],
        out_shape=[jax.ShapeDtypeStruct(qa.shape, BF16),
                   jax.ShapeDtypeStruct(cache_k.shape, F32),
                   jax.ShapeDtypeStruct(cache_v.shape, F32)],
        compiler_params=_params(1),
        name="band_sample",
    )(qa, cache_k, cache_v, ka_f, va_f, bias)


def _band_bias(rel_bias, n_q, n_k, offset, band_mask):
    qi = np.arange(n_q)[:, None]
    kj = np.arange(n_k)[None, :]
    n_u = n_q + n_k - 1
    idx_u = np.clip(np.arange(n_u) + offset - (n_k - 1), -MAX_REL, MAX_REL) + MAX_REL
    n_lo = int(np.sum(idx_u == 0)) - 1 if idx_u[0] == 0 else 0
    n_hi = int(np.sum(idx_u == 2 * MAX_REL)) - 1 if idx_u[-1] == 2 * MAX_REL else 0
    mid = rel_bias[:, int(idx_u[n_lo]):int(idx_u[n_u - 1 - n_hi]) + 1]
    ext = jnp.concatenate([jnp.repeat(rel_bias[:, :1], n_lo, axis=1), mid,
                           jnp.repeat(rel_bias[:, -1:], n_hi, axis=1)], axis=1).astype(F32)
    n_h = rel_bias.shape[0]
    hankel = jnp.tile(ext, (1, n_q + 1))[:, :n_q * (n_u + 1)].reshape(n_h, n_q, n_u + 1)[:, :, :n_k]
    bias = hankel[:, :, ::-1]
    if band_mask:
        rel_chunk = (qi + offset) // CHUNK - kj // CHUNK
        valid = (rel_chunk >= 0) & (rel_chunk <= LEFT_CHUNKS)
        bias = jnp.where(jnp.asarray(valid)[None], bias, NEG)
    return bias


def _aug_head_mask(h_in_pair, pair):
    lane = lax.broadcasted_iota(jnp.int32, (1, 2 * LANES), 1)
    head = 2 * pair + h_in_pair
    in_q = (lane < LANES) & ((lane // HEAD_DIM) == h_in_pair)
    in_aug = (lane >= LANES) & (((lane - LANES) // AUG_STRIDE) == head)
    return in_q | in_aug


def _flash_update(s, v, m_ref, l_ref, acc_ref, h, row_bias=None):
    chunks = [s[:, c * LANES:(c + 1) * LANES] for c in range(s.shape[1] // LANES)]
    if row_bias is not None:
        chunks = [c + row_bias for c in chunks]
    cmax = chunks[0]
    for c in chunks[1:]:
        cmax = jnp.maximum(cmax, c)
    m_prev = m_ref[h]
    m_new = jnp.maximum(m_prev, jnp.max(cmax, axis=-1, keepdims=True))
    alpha = jnp.exp(m_prev - m_new)
    ps = [jnp.exp(c - m_new) for c in chunks]
    lsum = ps[0]
    for p in ps[1:]:
        lsum = lsum + p
    l_ref[h] = alpha * l_ref[h] + lsum
    acc_ref[h] = alpha * acc_ref[h] + _dot(jnp.concatenate(ps, axis=-1).astype(BF16), v)
    m_ref[h] = m_new


def _flash_result(l_ref, acc_ref, pair):
    outs = [acc_ref[2 * pair + hh] / jnp.sum(l_ref[2 * pair + hh], axis=-1, keepdims=True)
            for hh in range(2)]
    return jnp.where(_head_mask(0), outs[0], outs[1])


def _fox_prompt_kernel(qi_ref, kj_ref, q_ref, k_ref, v_ref, o_ref, qh_sc, m_sc, l_sc, acc_sc, *, tq, tk):
    t = pl.program_id(1)
    i = qi_ref[t]
    j = kj_ref[t]

    @pl.when(j == 0)
    def _():
        for p in range(N_PAIRS):
            q = q_ref[p]
            for hh in range(2):
                qh_sc[2 * p + hh] = jnp.where(_aug_head_mask(hh, p), q, jnp.zeros_like(q))
        m_sc[...] = jnp.full_like(m_sc, NEG)
        l_sc[...] = jnp.zeros_like(l_sc)
        acc_sc[...] = jnp.zeros_like(acc_sc)

    def step(diagonal):
        if diagonal:
            keep = (lax.broadcasted_iota(jnp.int32, (tq, tk), 1)
                    <= lax.broadcasted_iota(jnp.int32, (tq, tk), 0))
        for p in range(N_PAIRS):
            k = k_ref[p]
            v = v_ref[:, p * LANES:(p + 1) * LANES]
            for hh in range(2):
                s = _dot_nt(qh_sc[2 * p + hh], k)
                if diagonal:
                    s = jnp.where(keep, s, NEG)
                _flash_update(s, v, m_sc, l_sc, acc_sc, 2 * p + hh)

    @pl.when(j < i)
    def _():
        step(False)

    @pl.when(j == i)
    def _():
        step(True)
        for p in range(N_PAIRS):
            o_ref[:, p * LANES:(p + 1) * LANES] = _flash_result(l_sc, acc_sc, p).astype(BF16)


def _fox_prompt(qcat, kcat, vb, n_seq):
    seq = qcat.shape[2]
    tq = tk = 512
    nt = seq // tq
    qi = np.concatenate([np.full(i + 1, i) for i in range(nt)]).astype(np.int32)
    kj = np.concatenate([np.arange(i + 1) for i in range(nt)]).astype(np.int32)
    grid_spec = pltpu.PrefetchScalarGridSpec(
        num_scalar_prefetch=2,
        grid=(n_seq, len(qi)),
        in_specs=[
            pl.BlockSpec((None, N_PAIRS, tq, 2 * LANES), lambda b, t, qi, kj: (b, 0, qi[t], 0)),
            pl.BlockSpec((None, N_PAIRS, tk, 2 * LANES), lambda b, t, qi, kj: (b, 0, kj[t], 0)),
            pl.BlockSpec((tk, WIDTH), lambda b, t, qi, kj: (b * nt + kj[t], 0)),
        ],
        out_specs=pl.BlockSpec((tq, WIDTH), lambda b, t, qi, kj: (b * nt + qi[t], 0)),
        scratch_shapes=[pltpu.VMEM((N_HEADS, tq, 2 * LANES), BF16), pltpu.VMEM((N_HEADS, tq, LANES), F32),
                        pltpu.VMEM((N_HEADS, tq, LANES), F32), pltpu.VMEM((N_HEADS, tq, LANES), F32)],
    )
    return pl.pallas_call(
        functools.partial(_fox_prompt_kernel, tq=tq, tk=tk),
        grid_spec=grid_spec,
        out_shape=jax.ShapeDtypeStruct(vb.shape, BF16),
        compiler_params=_params(2),
        name="fox_prompt",
    )(jnp.asarray(qi), jnp.asarray(kj), qcat, kcat, vb)


def _fox_sample_kernel(q_ref, kc_ref, vc_ref, kn_ref, vn_ref, lfc_ref, lfn_ref, sl_ref, u_ref,
                       o_ref, cq_sc, car_sc, m_sc, l_sc, acc_sc, *, n_cache_tiles):
    j = pl.program_id(1)
    t = q_ref.shape[1]

    def cum_new():
        hi, mid, lo = _split3(lfn_ref[...])
        u = u_ref[...]
        return _dot(hi, u) + _dot(mid, u) + _dot(lo, u)

    @pl.when(j == 0)
    def _():
        cn = cum_new()
        eye = (lax.broadcasted_iota(jnp.int32, (t, LANES), 0)
               == lax.broadcasted_iota(jnp.int32, (t, LANES), 1))
        for h in range(N_HEADS):
            col = jnp.sum(jnp.where(eye, jnp.broadcast_to(cn[h:h + 1, :], (t, LANES)), 0.0),
                          axis=-1, keepdims=True)
            cq_sc[h] = jnp.broadcast_to(col, (t, LANES))
        car_sc[...] = jnp.zeros_like(car_sc)
        m_sc[...] = jnp.full_like(m_sc, NEG)
        l_sc[...] = jnp.zeros_like(l_sc)
        acc_sc[...] = jnp.zeros_like(acc_sc)

    @pl.when(j < n_cache_tiles)
    def _():
        hi, mid, lo = _split3(lfc_ref[...])
        sl = sl_ref[...]
        within = _dot(hi, sl) + _dot(mid, sl) + _dot(lo, sl)
        r = within + car_sc[:, 0:1]
        car_sc[...] = jnp.broadcast_to(r[:, 0:1] + lfc_ref[:, 0:1], car_sc.shape)
        for p in range(N_PAIRS):
            cols = slice(p * LANES, (p + 1) * LANES)
            q = q_ref[p, :, 0:LANES]
            k = kc_ref[:, cols].astype(BF16)
            v = vc_ref[:, cols].astype(BF16)
            for hh in range(2):
                h = 2 * p + hh
                qh = jnp.where(_head_mask(hh), q, jnp.zeros_like(q))
                s = _dot_nt(qh, k) + r[h:h + 1, :]
                _flash_update(s, v, m_sc, l_sc, acc_sc, h, row_bias=cq_sc[h])

    @pl.when(j == n_cache_tiles)
    def _():
        cn = cum_new()
        causal = (lax.broadcasted_iota(jnp.int32, (t, LANES), 1)
                  <= lax.broadcasted_iota(jnp.int32, (t, LANES), 0))
        for p in range(N_PAIRS):
            cols = slice(p * LANES, (p + 1) * LANES)
            q = q_ref[p, :, 0:LANES]
            k = kn_ref[p, :, 0:LANES]
            k = jnp.concatenate([k, jnp.zeros((LANES - t, LANES), BF16)], axis=0)
            v = jnp.concatenate([vn_ref[:, cols], jnp.zeros((LANES - t, LANES), BF16)], axis=0)
            for hh in range(2):
                h = 2 * p + hh
                qh = jnp.where(_head_mask(hh), q, jnp.zeros_like(q))
                s = _dot_nt(qh, k) + cq_sc[h] - cn[h:h + 1, :]
                _flash_update(jnp.where(causal, s, NEG), v, m_sc, l_sc, acc_sc, h)
            o_ref[:, cols] = _flash_result(l_sc, acc_sc, p).astype(BF16)


def _fox_sample(qcat, kcat, vb, cache_k, cache_v, lf_cache_t, lf_new_t):
    n, p_rows, _ = cache_k.shape
    t = vb.shape[0] // n
    tk = 1024
    nct = p_rows // tk
    sl = jnp.asarray(np.tril(np.ones((tk, tk)), -1), BF16)
    u = jnp.asarray(np.pad(np.triu(np.ones((t, t))), ((0, 0), (0, LANES - t))), BF16)
    cmap = lambda b, j: (b, jnp.maximum(nct - 1 - j, 0), 0)
    new_cat = pl.BlockSpec((None, N_PAIRS, t, 2 * LANES), lambda b, j: (0, 0, b, 0))
    return pl.pallas_call(
        functools.partial(_fox_sample_kernel, n_cache_tiles=nct),
        grid=(n, nct + 1),
        in_specs=[
            new_cat,
            pl.BlockSpec((None, tk, WIDTH), cmap), pl.BlockSpec((None, tk, WIDTH), cmap),
            new_cat,
            pl.BlockSpec((t, WIDTH), lambda b, j: (b, 0)),
            pl.BlockSpec((None, N_HEADS, tk), lambda b, j: (b, 0, jnp.maximum(nct - 1 - j, 0))),
            pl.BlockSpec((None, N_HEADS, t), lambda b, j: (b, 0, 0)),
            _const_spec(sl.shape), _const_spec(u.shape),
        ],
        out_specs=pl.BlockSpec((t, WIDTH), lambda b, j: (b, 0)),
        out_shape=jax.ShapeDtypeStruct(vb.shape, BF16),
        scratch_shapes=[pltpu.VMEM((N_HEADS, t, LANES), F32), pltpu.VMEM((N_HEADS, LANES), F32),
                        pltpu.VMEM((N_HEADS, t, LANES), F32), pltpu.VMEM((N_HEADS, t, LANES), F32),
                        pltpu.VMEM((N_HEADS, t, LANES), F32)],
        compiler_params=_params(2),
        name="fox_sample",
    )(qcat, cache_k, cache_v, kcat, vb, lf_cache_t, lf_new_t, sl, u)


def _merge_kernel(x_ref, oa_ref, ob_ref, g_ref, wg_ref, wpa_ref, wpb_ref, wo_ref, y_ref):
    x = x_ref[...]
    h = _rms(x, g_ref[...]).astype(BF16)
    gate = jax.nn.sigmoid(_dot(h, wg_ref[...]))
    mix = (gate[:, :D_MODEL] * _dot(oa_ref[...], wpa_ref[...])
           + gate[:, D_MODEL:] * _dot(ob_ref[...], wpb_ref[...]))
    y_ref[...] = x + _dot(mix.astype(BF16), wo_ref[...])


def _merge(x, oa, ob, w):
    t_total = x.shape[0]
    tm = 512
    row = lambda n: pl.BlockSpec((tm, n), lambda i: (i, 0))
    consts = [w["g_mix"], w["w_gate"], w["w_pa"], w["w_pb"], w["w_o"]]
    return pl.pallas_call(
        _merge_kernel,
        grid=(t_total // tm,),
        in_specs=[row(D_MODEL), row(WIDTH), row(WIDTH)] + [_const_spec(c.shape) for c in consts],
        out_specs=row(D_MODEL),
        out_shape=jax.ShapeDtypeStruct(x.shape, F32),
        compiler_params=_params(1),
        name="merge",
    )(x, oa, ob, *consts)


def _route(r):
    lane_i = lax.broadcasted_iota(jnp.int32, r.shape, 1)
    lane = lane_i.astype(F32)
    lane_grp = (lane_i // EXPERTS_PER_GROUP).astype(F32)
    big = float(LANES)
    is_coarse = (lane_i >= N_EXPERTS) & (lane_i < N_EXPERTS + N_GROUPS)
    cm = jnp.where(is_coarse, r, NEG)
    cmax = cm.max(axis=-1, keepdims=True)
    grp = jnp.min(jnp.where(cm == cmax, lane - N_EXPERTS, big), axis=-1, keepdims=True)
    pg_sel = 1.0 / jnp.sum(jnp.exp(cm - cmax), axis=-1, keepdims=True)
    in_grp = (lane_i < N_EXPERTS) & (lane_grp == grp)
    fm = jnp.where(in_grp, r, NEG)
    m1 = fm.max(axis=-1, keepdims=True)
    denom = jnp.sum(jnp.exp(fm - m1), axis=-1, keepdims=True)
    i1 = jnp.min(jnp.where(fm == m1, lane, big), axis=-1, keepdims=True)
    fm2 = jnp.where(lane == i1, NEG, fm)
    m2 = fm2.max(axis=-1, keepdims=True)
    i2 = jnp.min(jnp.where(fm2 == m2, lane, big), axis=-1, keepdims=True)
    p1 = 1.0 / denom
    p2 = jnp.exp(m2 - m1) / denom
    tot = p1 + p2
    return (jnp.where(lane == i1, pg_sel * (p1 / tot), 0.0)
            + jnp.where(lane == i2, pg_sel * (p2 / tot), 0.0))


def _moe_kernel(x_ref, g_ref, wr_ref, br_ref, w1_ref, w3_ref, w2_ref, y_ref, hx_sc, comb_sc, *, e_blk):
    j = pl.program_id(1)

    @pl.when(j == 0)
    def _():
        x = x_ref[...]
        hx = _rms(x, g_ref[...])
        hx_sc[...] = hx.astype(BF16)
        h_hi, h_mid, _ = _split3(hx)
        w_hi, w_mid, _ = _split3(wr_ref[...])
        r = _dot(h_hi, w_hi) + _dot(h_hi, w_mid) + _dot(h_mid, w_hi) + br_ref[...]
        comb_sc[...] = _route(r)
        y_ref[...] = x

    hx = hx_sc[...]
    hid = jax.nn.silu(_dot(hx, w1_ref[...])) * _dot(hx, w3_ref[...])
    comb = comb_sc[...]
    lane = lax.broadcasted_iota(jnp.int32, comb.shape, 1)
    scale = []
    for e in range(e_blk):
        w_e = jnp.sum(jnp.where(lane == j * e_blk + e, comb, 0.0), axis=-1, keepdims=True)
        scale.append(jnp.broadcast_to(w_e, (comb.shape[0], D_EXPERT)))
    hid = hid * jnp.concatenate(scale, axis=-1)
    y_ref[...] += _dot(hid.astype(BF16), w2_ref[...])


def _moe(x, w):
    t_total = x.shape[0]
    tm = 512
    e_blk = 4
    cols = e_blk * D_EXPERT
    consts = [w["g_ffn"], w["w_router"], w["b_router"]]
    return pl.pallas_call(
        functools.partial(_moe_kernel, e_blk=e_blk),
        grid=(t_total // tm, N_EXPERTS // e_blk),
        in_specs=[pl.BlockSpec((tm, D_MODEL), lambda i, j: (i, 0))]
        + [_const_spec(c.shape) for c in consts]
        + [pl.BlockSpec((D_MODEL, cols), lambda i, j: (0, j)),
           pl.BlockSpec((D_MODEL, cols), lambda i, j: (0, j)),
           pl.BlockSpec((cols, D_MODEL), lambda i, j: (j, 0))],
        out_specs=pl.BlockSpec((tm, D_MODEL), lambda i, j: (i, 0)),
        out_shape=jax.ShapeDtypeStruct(x.shape, F32),
        scratch_shapes=[pltpu.VMEM((tm, D_MODEL), BF16), pltpu.VMEM((tm, LANES), F32)],
        compiler_params=_params(2),
        name="moe",
    )(x, *consts, w["w1"], w["w3"], w["w2"])


def _prep_weights(g_mix, w_in, b_f, q_norm_a, k_norm_a, q_norm_b, k_norm_b, w_pa, w_pb, w_o,
                  g_ffn, w_rg, b_rg, w_re, b_re, w1, w3, w2):
    n_qkv = 6 * WIDTH
    tile = lambda g: jnp.tile(g, N_HEADS)
    w_router = jnp.concatenate(
        [jnp.transpose(w_re, (1, 0, 2)).reshape(D_MODEL, N_EXPERTS), w_rg,
         jnp.zeros((D_MODEL, LANES - N_EXPERTS - N_GROUPS), F32)], axis=1)
    b_router = jnp.concatenate(
        [b_re.reshape(N_EXPERTS), b_rg, jnp.zeros((LANES - N_EXPERTS - N_GROUPS,), F32)])[None, :]
    return {
        "g_mix": g_mix[None, :],
        "w_qkv": w_in[:, :n_qkv].astype(BF16),
        "w_f": jnp.pad(w_in[:, n_qkv:n_qkv + N_HEADS], ((0, 0), (0, LANES - N_HEADS))).astype(BF16),
        "b_f": jnp.pad(b_f, (0, LANES - N_HEADS))[None, :],
        "w_gate": w_in[:, n_qkv + N_HEADS:].astype(BF16),
        "gains": jnp.stack([tile(q_norm_a), tile(k_norm_a), tile(q_norm_b), tile(k_norm_b)]),
        "w_pa": w_pa.astype(BF16), "w_pb": w_pb.astype(BF16), "w_o": w_o.astype(BF16),
        "g_ffn": g_ffn[None, :],
        "w_router": w_router, "b_router": b_router,
        "w1": jnp.transpose(w1, (1, 0, 2)).reshape(D_MODEL, N_EXPERTS * D_EXPERT).astype(BF16),
        "w3": jnp.transpose(w3, (1, 0, 2)).reshape(D_MODEL, N_EXPERTS * D_EXPERT).astype(BF16),
        "w2": w2.reshape(N_EXPERTS * D_EXPERT, D_MODEL).astype(BF16),
    }


def kernel(x_prompt, x_sample, cache_a_k, cache_a_v, cache_b_k, cache_b_v, cache_b_logf, g_mix, w_in, b_f, q_norm_a, k_norm_a, q_norm_b, k_norm_b, rel_bias, w_pa, w_pb, w_o, g_ffn, w_rg, b_rg, w_re, b_re, w1, w3, w2):
    assert g_mix.shape[0] == 1, "single-layer step"
    n_p, seq, _ = x_prompt.shape
    n_s, t_s, _ = x_sample.shape
    a_rows = cache_a_k.shape[2]
    past = cache_b_k.shape[2]
    w = _prep_weights(g_mix[0], w_in[0], b_f[0], q_norm_a[0], k_norm_a[0], q_norm_b[0], k_norm_b[0],
                      w_pa[0], w_pb[0], w_o[0], g_ffn[0], w_rg[0], b_rg[0], w_re[0], b_re[0],
                      w1[0], w3[0], w2[0])
    band_tq = 256
    bias_prompt = _band_bias(rel_bias[0], band_tq, 3 * band_tq, 2 * band_tq, band_mask=True)
    bias_sample = _band_bias(rel_bias[0], t_s, a_rows + t_s, a_rows, band_mask=False)

    xp = x_prompt.reshape(n_p * seq, D_MODEL)
    (qa, ka, va, ka_f, va_f, qcat, kcat, vb, kb_f, vb_f, _, logf) = _inproj(xp, n_p, w, a_rows_last=True)
    o_a = _band_prompt(qa, ka, va, bias_prompt, n_p)
    o_b = _fox_prompt(qcat, kcat, vb, n_p)
    y_p = _moe(_merge(xp, o_a, o_b, w), w)

    xs = x_sample.reshape(n_s * t_s, D_MODEL)
    (qa_s, _, _, ka_fs, va_fs, qcat_s, kcat_s, vb_s, kb_fs, vb_fs, _, logf_s) = _inproj(
        xs, 1, w, a_rows_last=False)
    o_as, new_ak, new_av = _band_sample(
        qa_s, ka_fs, va_fs, cache_a_k[0].reshape(n_s, a_rows, WIDTH),
        cache_a_v[0].reshape(n_s, a_rows, WIDTH), bias_sample)
    lf_cache_t = jnp.transpose(cache_b_logf[0], (0, 2, 1))
    lf_new_t = jnp.transpose(logf_s.reshape(n_s, t_s, N_HEADS), (0, 2, 1))
    o_bs = _fox_sample(qcat_s, kcat_s, vb_s, cache_b_k[0].reshape(n_s, past, WIDTH),
                       cache_b_v[0].reshape(n_s, past, WIDTH), lf_cache_t, lf_new_t)
    y_s = _moe(_merge(xs, o_as, o_bs, w), w)

    heads = lambda a, n, r: a.reshape(1, n, r, N_HEADS, HEAD_DIM)
    rows_p = min(WINDOW_ROWS, seq)
    return (y_p.reshape(n_p, seq, D_MODEL), y_s.reshape(n_s, t_s, D_MODEL),
            heads(ka_f, n_p, rows_p), heads(va_f, n_p, rows_p),
            heads(kb_f, n_p, seq), heads(vb_f, n_p, seq), logf.reshape(1, n_p, seq, N_HEADS),
            heads(new_ak, n_s, a_rows), heads(new_av, n_s, a_rows),
            heads(kb_fs, n_s, t_s), heads(vb_fs, n_s, t_s), logf_s.reshape(1, n_s, t_s, N_HEADS))
```

```python
import functools

import numpy as np
import jax
import jax.numpy as jnp
from jax import lax
from jax.experimental import pallas as pl
from jax.experimental.pallas import tpu as pltpu

F32 = jnp.float32
BF16 = jnp.bfloat16

D_MODEL = 1024
HEAD_DIM = 64
N_HEADS = 8
WIDTH = N_HEADS * HEAD_DIM
N_PAIRS = N_HEADS // 2
CHUNK = 64
LEFT_CHUNKS = 8
WINDOW_ROWS = LEFT_CHUNKS * CHUNK
MAX_REL = 256
N_GROUPS = 4
EXPERTS_PER_GROUP = 8
N_EXPERTS = N_GROUPS * EXPERTS_PER_GROUP
D_EXPERT = 128
EPS = 1e-6
NEG = -1e30
LANES = 128
AUG_STRIDE = 8
VMEM_LIMIT = 56 * 1024 * 1024

_NT = (((1,), (1,)), ((), ()))


def _dot(a, b):
    return jnp.dot(a, b, preferred_element_type=F32)


def _dot_nt(a, b):
    return lax.dot_general(a, b, _NT, preferred_element_type=F32)


def _split3(x):
    hi = x.astype(BF16)
    r = x - hi.astype(F32)
    mid = r.astype(BF16)
    lo = (r - mid.astype(F32)).astype(BF16)
    return hi, mid, lo


def _dot3(a_bf, x):
    hi, mid, lo = _split3(x)
    return _dot(a_bf, hi) + _dot(a_bf, mid) + _dot(a_bf, lo)


def _rms(x, g):
    ms = jnp.mean(x * x, axis=-1, keepdims=True)
    return x * lax.rsqrt(ms + EPS) * g


def _params(n_axes):
    return pltpu.CompilerParams(dimension_semantics=("arbitrary",) * n_axes,
                                vmem_limit_bytes=VMEM_LIMIT)


def _const_spec(shape):
    nd = len(shape)
    return pl.BlockSpec(shape, lambda *_: (0,) * nd)


def _inproj_kernel(x_ref, g_ref, wqkv_ref, wf_ref, bf_ref, gains_ref, bd_ref, ltri_ref,
                   selq_ref, selk_ref, oneq_ref, onek_ref,
                   qa_ref, ka_ref, va_ref, kaf_ref, vaf_ref, qcat_ref, kcat_ref, vb_ref,
                   kbf_ref, vbf_ref, logf_ref, carry_ref, *, tiles_per_seq, seq_minor):
    i = pl.program_id(0)
    tm = x_ref.shape[0]

    @pl.when(i % tiles_per_seq == 0)
    def _():
        carry_ref[...] = jnp.zeros_like(carry_ref)

    def put_f32(ref, y, last_tile_only=False):
        if not seq_minor:
            ref[...] = y
        elif last_tile_only:
            @pl.when(i % tiles_per_seq == tiles_per_seq - 1)
            def _():
                ref[...] = y.T.reshape(N_HEADS, HEAD_DIM, tm)
        else:
            ref[...] = y.T.reshape(N_HEADS, HEAD_DIM, tm)

    h = _rms(x_ref[...], g_ref[...]).astype(BF16)

    def seg(s):
        return _dot(h, wqkv_ref[:, s * WIDTH:(s + 1) * WIDTH])

    def headnorm(y, n):
        ss = _dot((y * y).astype(BF16), bd_ref[...])
        return y * lax.rsqrt(ss * (1.0 / HEAD_DIM) + EPS) * gains_ref[n:n + 1, :]

    q_a = headnorm(seg(0), 0)
    qa_ref[...] = (q_a * (HEAD_DIM ** -0.5)).astype(BF16)
    k_a = headnorm(seg(1), 1)
    ka_ref[...] = k_a.astype(BF16)
    put_f32(kaf_ref, k_a, last_tile_only=True)
    v_a = seg(2)
    va_ref[...] = v_a.astype(BF16)
    put_f32(vaf_ref, v_a, last_tile_only=True)

    z = _dot(h, wf_ref[...]) + bf_ref[...]
    logf = jnp.minimum(z, 0.0) - jnp.log(1.0 + jnp.exp(-jnp.abs(z)))
    if seq_minor:
        logf_ref[...] = logf.T[0:N_HEADS, :]
    else:
        logf_ref[...] = logf[:, :N_HEADS]
    c = _dot3(ltri_ref[...], logf) + carry_ref[0:1, :]
    carry_ref[...] = jnp.broadcast_to(c[-1:, :], carry_ref.shape)
    cs = jnp.concatenate(_split3(c), axis=-1)
    q_aug = (_dot(cs, selq_ref[...]) + oneq_ref[...]).astype(BF16)
    k_aug = (_dot(cs, selk_ref[...]) + onek_ref[...]).astype(BF16)

    q_b = (headnorm(seg(3), 2) * (HEAD_DIM ** -0.5)).astype(BF16)
    k_b = headnorm(seg(4), 3)
    put_f32(kbf_ref, k_b)
    k_b = k_b.astype(BF16)
    v_b = seg(5)
    put_f32(vbf_ref, v_b)
    vb_ref[...] = v_b.astype(BF16)
    for p in range(N_PAIRS):
        cols = slice(p * LANES, (p + 1) * LANES)
        qcat_ref[p, :, 0:LANES] = q_b[:, cols]
        qcat_ref[p, :, LANES:2 * LANES] = q_aug
        kcat_ref[p, :, 0:LANES] = k_b[:, cols]
        kcat_ref[p, :, LANES:2 * LANES] = k_aug


def _aug_constants():
    selq = np.zeros((3 * LANES, LANES), np.float32)
    selk = np.zeros((3 * LANES, LANES), np.float32)
    oneq = np.zeros((1, LANES), np.float32)
    onek = np.zeros((1, LANES), np.float32)
    for h in range(N_HEADS):
        for k in range(3):
            selq[k * LANES + h, AUG_STRIDE * h + k] = 1.0
            selk[k * LANES + h, AUG_STRIDE * h + 3 + k] = -1.0
            oneq[0, AUG_STRIDE * h + 3 + k] = 1.0
            onek[0, AUG_STRIDE * h + k] = 1.0
    return (jnp.asarray(selq, BF16), jnp.asarray(selk, BF16), jnp.asarray(oneq), jnp.asarray(onek))


def _inproj(x, n_seq, w, seq_minor):
    t_total = x.shape[0]
    tm = 512
    n_tiles = t_total // tm
    seq = t_total // n_seq
    tps = seq // tm
    row = lambda i: (i, 0)
    flat = lambda dt: (jax.ShapeDtypeStruct((t_total, WIDTH), dt), pl.BlockSpec((tm, WIDTH), row))
    if seq_minor:
        assert tm == WINDOW_ROWS
        band_f32 = (jax.ShapeDtypeStruct((n_seq, N_HEADS, HEAD_DIM, WINDOW_ROWS), F32),
                    pl.BlockSpec((None, N_HEADS, HEAD_DIM, tm), lambda i: (i // tps, 0, 0, 0)))
        fox_f32 = (jax.ShapeDtypeStruct((n_seq, N_HEADS, HEAD_DIM, seq), F32),
                   pl.BlockSpec((None, N_HEADS, HEAD_DIM, tm), lambda i: (i // tps, 0, 0, i % tps)))
        logf = (jax.ShapeDtypeStruct((n_seq, N_HEADS, seq), F32),
                pl.BlockSpec((None, N_HEADS, tm), lambda i: (i // tps, 0, i % tps)))
    else:
        band_f32 = fox_f32 = flat(F32)
        logf = (jax.ShapeDtypeStruct((t_total, N_HEADS), F32), pl.BlockSpec((tm, N_HEADS), row))
    cat = (jax.ShapeDtypeStruct((n_seq, N_PAIRS, seq, 2 * LANES), BF16),
           pl.BlockSpec((None, N_PAIRS, tm, 2 * LANES), lambda i: (i // tps, 0, i % tps, 0)))
    outs = [flat(BF16), flat(BF16), flat(BF16), band_f32, band_f32, cat, cat, flat(BF16),
            fox_f32, fox_f32, logf]
    bd = jnp.asarray(np.kron(np.eye(N_HEADS), np.ones((HEAD_DIM, HEAD_DIM))), BF16)
    ltri = jnp.asarray(np.tril(np.ones((tm, tm))), BF16)
    consts = [w["g_mix"], w["w_qkv"], w["w_f"], w["b_f"], w["gains"], bd, ltri, *_aug_constants()]
    in_specs = [pl.BlockSpec((tm, D_MODEL), row)] + [_const_spec(c.shape) for c in consts]
    return pl.pallas_call(
        functools.partial(_inproj_kernel, tiles_per_seq=tps, seq_minor=seq_minor),
        grid=(n_tiles,),
        in_specs=in_specs,
        out_specs=[o[1] for o in outs],
        out_shape=[o[0] for o in outs],
        scratch_shapes=[pltpu.VMEM((8, LANES), F32)],
        compiler_params=_params(1),
        name="inproj",
    )(x, *consts)


def _softmax_pv(scores, values):
    m = scores[0].max(axis=-1, keepdims=True)
    for s in scores[1:]:
        m = jnp.maximum(m, s.max(axis=-1, keepdims=True))
    pv = None
    l = None
    for s, v in zip(scores, values):
        p = jnp.exp(s - m)
        ls = p.sum(axis=-1, keepdims=True)
        o = _dot(p.astype(BF16), v)
        pv = o if pv is None else pv + o
        l = ls if l is None else l + ls
    return pv, l


def _head_mask(hh):
    lane = lax.broadcasted_iota(jnp.int32, (1, LANES), 1)
    return (lane // HEAD_DIM) == hh


def _band_prompt_kernel(q_ref, k0_ref, k1_ref, k2_ref, v0_ref, v1_ref, v2_ref, bias_ref, o_ref, *, tq):
    i = pl.program_id(1)
    k_refs = (k0_ref, k1_ref, k2_ref)
    v_refs = (v0_ref, v1_ref, v2_ref)
    pad = [jnp.where(i - 2 + j >= 0, 0.0, NEG) for j in range(2)] + [0.0]
    for p in range(N_PAIRS):
        cols = slice(p * LANES, (p + 1) * LANES)
        q = q_ref[:, cols]
        ks = [r[:, cols] for r in k_refs]
        vs = [r[:, cols] for r in v_refs]
        out = jnp.zeros((tq, LANES), F32)
        for hh in range(2):
            msk = _head_mask(hh)
            qh = jnp.where(msk, q, jnp.zeros_like(q))
            scores = [_dot_nt(qh, ks[j]) + bias_ref[2 * p + hh, :, j * tq:(j + 1) * tq] + pad[j]
                      for j in range(3)]
            pv, l = _softmax_pv(scores, vs)
            out = jnp.where(msk, pv / l, out)
        o_ref[:, cols] = out.astype(BF16)


def _band_prompt(qa, ka, va, bias, n_seq):
    t_total = qa.shape[0]
    tq = 256
    seq = t_total // n_seq
    nq = seq // tq
    qmap = lambda b, i: (b * nq + i, 0)

    def kmap(j):
        return lambda b, i: (b * nq + jnp.maximum(i - 2 + j, 0), 0)

    blk = lambda m: pl.BlockSpec((tq, WIDTH), m)
    return pl.pallas_call(
        functools.partial(_band_prompt_kernel, tq=tq),
        grid=(n_seq, nq),
        in_specs=[blk(qmap)] + [blk(kmap(j)) for j in range(3)] + [blk(kmap(j)) for j in range(3)]
        + [_const_spec(bias.shape)],
        out_specs=blk(qmap),
        out_shape=jax.ShapeDtypeStruct((t_total, WIDTH), BF16),
        compiler_params=_params(2),
        name="band_prompt",
    )(qa, ka, ka, ka, va, va, va, bias)


def _band_sample_kernel(q_ref, kc_ref, vc_ref, kn_ref, vn_ref, bias_ref, o_ref, ko_ref, vo_ref):
    p_rows = kc_ref.shape[0]
    t = q_ref.shape[0]
    for p in range(N_PAIRS):
        cols = slice(p * LANES, (p + 1) * LANES)
        q = q_ref[:, cols]
        ks = [kc_ref[:, cols].astype(BF16), kn_ref[:, cols].astype(BF16)]
        vs = [vc_ref[:, cols].astype(BF16), vn_ref[:, cols].astype(BF16)]
        out = jnp.zeros((t, LANES), F32)
        for hh in range(2):
            msk = _head_mask(hh)
            qh = jnp.where(msk, q, jnp.zeros_like(q))
            scores = [_dot_nt(qh, ks[0]) + bias_ref[2 * p + hh, :, 0:p_rows],
                      _dot_nt(qh, ks[1]) + bias_ref[2 * p + hh, :, p_rows:p_rows + t]]
            pv, l = _softmax_pv(scores, vs)
            out = jnp.where(msk, pv / l, out)
        o_ref[:, cols] = out.astype(BF16)
    ko_ref[0:p_rows - t, :] = kc_ref[t:p_rows, :]
    ko_ref[p_rows - t:p_rows, :] = kn_ref[...]
    vo_ref[0:p_rows - t, :] = vc_ref[t:p_rows, :]
    vo_ref[p_rows - t:p_rows, :] = vn_ref[...]


def _band_sample(qa, ka_f, va_f, cache_k, cache_v, bias):
    n, p_rows, _ = cache_k.shape
    t = qa.shape[0] // n
    row = pl.BlockSpec((t, WIDTH), lambda b: (b, 0))
    cache = pl.BlockSpec((None, p_rows, WIDTH), lambda b: (b, 0, 0))
    return pl.pallas_call(
        _band_sample_kernel,
        grid=(n,),
        in_specs=[row, cache, cache, row, row, _const_spec(bias.shape)],
        out_specs=[row, cache, cache],
        out_shape=[jax.ShapeDtypeStruct(qa.shape, BF16),
                   jax.ShapeDtypeStruct(cache_k.shape, F32),
                   jax.ShapeDtypeStruct(cache_v.shape, F32)],
        compiler_params=_params(1),
        name="band_sample",
    )(qa, cache_k, cache_v, ka_f, va_f, bias)


def _band_bias(rel_bias, n_q, n_k, offset, band_mask):
    qi = np.arange(n_q)[:, None]
    kj = np.arange(n_k)[None, :]
    n_u = n_q + n_k - 1
    w_of_m = np.where(np.arange(n_u) < n_k, np.arange(n_u), np.arange(n_u) - n_u)
    idx_m = np.clip(offset - w_of_m, -MAX_REL, MAX_REL) + MAX_REL
    e = jnp.take(rel_bias.astype(F32), jnp.asarray(idx_m), axis=1)
    n_h = rel_bias.shape[0]
    bias = jnp.tile(e, (1, n_q + 1))[:, :n_q * (n_u - 1)].reshape(n_h, n_q, n_u - 1)[:, :, :n_k]
    if band_mask:
        rel_chunk = (qi + offset) // CHUNK - kj // CHUNK
        valid = (rel_chunk >= 0) & (rel_chunk <= LEFT_CHUNKS)
        bias = jnp.where(jnp.asarray(valid)[None], bias, NEG)
    return bias


def _aug_head_mask(h_in_pair, pair):
    lane = lax.broadcasted_iota(jnp.int32, (1, 2 * LANES), 1)
    head = 2 * pair + h_in_pair
    in_q = (lane < LANES) & ((lane // HEAD_DIM) == h_in_pair)
    in_aug = (lane >= LANES) & (((lane - LANES) // AUG_STRIDE) == head)
    return in_q | in_aug


def _flash_update(s, v, m_ref, l_ref, acc_ref, h, row_bias=None, v_seq_minor=False):
    chunks = [s[:, c * LANES:(c + 1) * LANES] for c in range(s.shape[1] // LANES)]
    if row_bias is not None:
        chunks = [c + row_bias for c in chunks]
    cmax = chunks[0]
    for c in chunks[1:]:
        cmax = jnp.maximum(cmax, c)
    m_prev = m_ref[h]
    m_new = jnp.maximum(m_prev, jnp.max(cmax, axis=-1, keepdims=True))
    alpha = jnp.exp(m_prev - m_new)
    ps = [jnp.exp(c - m_new) for c in chunks]
    lsum = ps[0]
    for p in ps[1:]:
        lsum = lsum + p
    l_ref[h] = alpha * l_ref[h] + lsum
    p = jnp.concatenate(ps, axis=-1).astype(BF16)
    pv = _dot_nt(p, v) if v_seq_minor else _dot(p, v)
    acc_ref[h] = alpha[:, :pv.shape[1]] * acc_ref[h] + pv
    m_ref[h] = m_new


def _flash_result(l_ref, acc_ref, pair):
    outs = [acc_ref[2 * pair + hh] / jnp.sum(l_ref[2 * pair + hh], axis=-1, keepdims=True)
            for hh in range(2)]
    return jnp.where(_head_mask(0), outs[0], outs[1])


def _fox_prompt_kernel(qi_ref, kj_ref, q_ref, k_ref, v_ref, o_ref, qh_sc, m_sc, l_sc, acc_sc, *, tq, tk):
    t = pl.program_id(1)
    i = qi_ref[t]
    j = kj_ref[t]

    @pl.when(j == 0)
    def _():
        for p in range(N_PAIRS):
            q = q_ref[p]
            for hh in range(2):
                qh_sc[2 * p + hh] = jnp.where(_aug_head_mask(hh, p), q, jnp.zeros_like(q))
        m_sc[...] = jnp.full_like(m_sc, NEG)
        l_sc[...] = jnp.zeros_like(l_sc)
        acc_sc[...] = jnp.zeros_like(acc_sc)

    def step(diagonal):
        if diagonal:
            keep = (lax.broadcasted_iota(jnp.int32, (tq, tk), 1)
                    <= lax.broadcasted_iota(jnp.int32, (tq, tk), 0))
        for p in range(N_PAIRS):
            k = k_ref[p]
            v = v_ref[:, p * LANES:(p + 1) * LANES]
            for hh in range(2):
                s = _dot_nt(qh_sc[2 * p + hh], k)
                if diagonal:
                    s = jnp.where(keep, s, NEG)
                _flash_update(s, v, m_sc, l_sc, acc_sc, 2 * p + hh)

    @pl.when(j < i)
    def _():
        step(False)

    @pl.when(j == i)
    def _():
        step(True)
        for p in range(N_PAIRS):
            o_ref[:, p * LANES:(p + 1) * LANES] = _flash_result(l_sc, acc_sc, p).astype(BF16)


def _fox_prompt(qcat, kcat, vb, n_seq):
    seq = qcat.shape[2]
    tq = tk = 512
    nt = seq // tq
    qi = np.concatenate([np.full(i + 1, i) for i in range(nt)]).astype(np.int32)
    kj = np.concatenate([np.arange(i + 1) for i in range(nt)]).astype(np.int32)
    grid_spec = pltpu.PrefetchScalarGridSpec(
        num_scalar_prefetch=2,
        grid=(n_seq, len(qi)),
        in_specs=[
            pl.BlockSpec((None, N_PAIRS, tq, 2 * LANES), lambda b, t, qi, kj: (b, 0, qi[t], 0)),
            pl.BlockSpec((None, N_PAIRS, tk, 2 * LANES), lambda b, t, qi, kj: (b, 0, kj[t], 0)),
            pl.BlockSpec((tk, WIDTH), lambda b, t, qi, kj: (b * nt + kj[t], 0)),
        ],
        out_specs=pl.BlockSpec((tq, WIDTH), lambda b, t, qi, kj: (b * nt + qi[t], 0)),
        scratch_shapes=[pltpu.VMEM((N_HEADS, tq, 2 * LANES), BF16), pltpu.VMEM((N_HEADS, tq, LANES), F32),
                        pltpu.VMEM((N_HEADS, tq, LANES), F32), pltpu.VMEM((N_HEADS, tq, LANES), F32)],
    )
    return pl.pallas_call(
        functools.partial(_fox_prompt_kernel, tq=tq, tk=tk),
        grid_spec=grid_spec,
        out_shape=jax.ShapeDtypeStruct(vb.shape, BF16),
        compiler_params=_params(2),
        name="fox_prompt",
    )(jnp.asarray(qi), jnp.asarray(kj), qcat, kcat, vb)


def _suffix_sum_exclusive(x):
    n = x.shape[1]
    lane = lax.broadcasted_iota(jnp.int32, x.shape, 1)
    y = jnp.where(lane + 1 < n, pltpu.roll(x, n - 1, axis=1), 0.0)
    shift = 1
    while shift < n:
        y = y + jnp.where(lane + shift < n, pltpu.roll(y, n - shift, axis=1), 0.0)
        shift *= 2
    return y


def _fox_sample_kernel(q_ref, kc_ref, vc_ref, kn_ref, vn_ref, lfc_ref, lfn_ref, u_ref,
                       o_ref, cq_sc, car_sc, m_sc, l_sc, acc_sc, *, n_cache_tiles):
    j = pl.program_id(1)
    t = q_ref.shape[1]

    def cum_new():
        hi, mid, lo = _split3(lfn_ref[...])
        u = u_ref[...]
        return _dot(hi, u) + _dot(mid, u) + _dot(lo, u)

    def q_head(h):
        return q_ref[h // 2, :, (h % 2) * HEAD_DIM:(h % 2 + 1) * HEAD_DIM]

    @pl.when(j == 0)
    def _():
        cn = cum_new()
        eye = (lax.broadcasted_iota(jnp.int32, (t, LANES), 0)
               == lax.broadcasted_iota(jnp.int32, (t, LANES), 1))
        for h in range(N_HEADS):
            col = jnp.sum(jnp.where(eye, jnp.broadcast_to(cn[h:h + 1, :], (t, LANES)), 0.0),
                          axis=-1, keepdims=True)
            cq_sc[h] = jnp.broadcast_to(col, (t, LANES))
        car_sc[...] = jnp.zeros_like(car_sc)
        m_sc[...] = jnp.full_like(m_sc, NEG)
        l_sc[...] = jnp.zeros_like(l_sc)
        acc_sc[...] = jnp.zeros_like(acc_sc)

    @pl.when(j < n_cache_tiles)
    def _():
        lf = lfc_ref[...]
        r = _suffix_sum_exclusive(lf) + car_sc[:, 0:1]
        car_sc[...] = jnp.broadcast_to(r[:, 0:1] + lf[:, 0:1], car_sc.shape)
        for h in range(N_HEADS):
            s = _dot(q_head(h), kc_ref[h].astype(BF16)) + r[h:h + 1, :]
            _flash_update(s, vc_ref[h].astype(BF16), m_sc, l_sc, acc_sc, h,
                          row_bias=cq_sc[h], v_seq_minor=True)

    @pl.when(j == n_cache_tiles)
    def _():
        cn = cum_new()
        causal = (lax.broadcasted_iota(jnp.int32, (t, LANES), 1)
                  <= lax.broadcasted_iota(jnp.int32, (t, LANES), 0))
        pad = jnp.zeros((LANES - t, HEAD_DIM), BF16)
        for h in range(N_HEADS):
            cols = slice(h * HEAD_DIM, (h + 1) * HEAD_DIM)
            k = jnp.concatenate([kn_ref[h // 2, :, (h % 2) * HEAD_DIM:(h % 2 + 1) * HEAD_DIM], pad], axis=0)
            v = jnp.concatenate([vn_ref[:, cols], pad], axis=0)
            s = _dot_nt(q_head(h), k) + cq_sc[h] - cn[h:h + 1, :]
            _flash_update(jnp.where(causal, s, NEG), v, m_sc, l_sc, acc_sc, h)
            o_ref[:, cols] = (acc_sc[h] / jnp.sum(l_sc[h], axis=-1, keepdims=True)).astype(BF16)


def _fox_sample(qcat, kcat, vb, cache_kt, cache_vt, lf_cache_t, lf_new_t):
    n, _, _, p_rows = cache_kt.shape
    t = vb.shape[0] // n
    tk = 2048
    nct = p_rows // tk
    u = jnp.asarray(np.pad(np.triu(np.ones((t, t))), ((0, 0), (0, LANES - t))), BF16)
    rev = lambda j: jnp.maximum(nct - 1 - j, 0)
    cache = pl.BlockSpec((None, N_HEADS, HEAD_DIM, tk), lambda b, j: (b, 0, 0, rev(j)))
    new_cat = pl.BlockSpec((None, N_PAIRS, t, 2 * LANES), lambda b, j: (0, 0, b, 0))
    return pl.pallas_call(
        functools.partial(_fox_sample_kernel, n_cache_tiles=nct),
        grid=(n, nct + 1),
        in_specs=[
            new_cat, cache, cache, new_cat,
            pl.BlockSpec((t, WIDTH), lambda b, j: (b, 0)),
            pl.BlockSpec((None, N_HEADS, tk), lambda b, j: (b, 0, rev(j))),
            pl.BlockSpec((None, N_HEADS, t), lambda b, j: (b, 0, 0)),
            _const_spec(u.shape),
        ],
        out_specs=pl.BlockSpec((t, WIDTH), lambda b, j: (b, 0)),
        out_shape=jax.ShapeDtypeStruct(vb.shape, BF16),
        scratch_shapes=[pltpu.VMEM((N_HEADS, t, LANES), F32), pltpu.VMEM((N_HEADS, LANES), F32),
                        pltpu.VMEM((N_HEADS, t, LANES), F32), pltpu.VMEM((N_HEADS, t, LANES), F32),
                        pltpu.VMEM((N_HEADS, t, HEAD_DIM), F32)],
        compiler_params=_params(2),
        name="fox_sample",
    )(qcat, cache_kt, cache_vt, kcat, vb, lf_cache_t, lf_new_t, u)


def _merge_kernel(x_ref, oa_ref, ob_ref, g_ref, wg_ref, wpa_ref, wpb_ref, wo_ref, y_ref):
    x = x_ref[...]
    h = _rms(x, g_ref[...]).astype(BF16)
    gate = jax.nn.sigmoid(_dot(h, wg_ref[...]))
    mix = (gate[:, :D_MODEL] * _dot(oa_ref[...], wpa_ref[...])
           + gate[:, D_MODEL:] * _dot(ob_ref[...], wpb_ref[...]))
    y_ref[...] = x + _dot(mix.astype(BF16), wo_ref[...])


def _merge(x, oa, ob, w):
    t_total = x.shape[0]
    tm = 512
    row = lambda n: pl.BlockSpec((tm, n), lambda i: (i, 0))
    consts = [w["g_mix"], w["w_gate"], w["w_pa"], w["w_pb"], w["w_o"]]
    return pl.pallas_call(
        _merge_kernel,
        grid=(t_total // tm,),
        in_specs=[row(D_MODEL), row(WIDTH), row(WIDTH)] + [_const_spec(c.shape) for c in consts],
        out_specs=row(D_MODEL),
        out_shape=jax.ShapeDtypeStruct(x.shape, F32),
        compiler_params=_params(1),
        name="merge",
    )(x, oa, ob, *consts)


def _route(r):
    lane_i = lax.broadcasted_iota(jnp.int32, r.shape, 1)
    lane = lane_i.astype(F32)
    lane_grp = (lane_i // EXPERTS_PER_GROUP).astype(F32)
    big = float(LANES)
    is_coarse = (lane_i >= N_EXPERTS) & (lane_i < N_EXPERTS + N_GROUPS)
    cm = jnp.where(is_coarse, r, NEG)
    cmax = cm.max(axis=-1, keepdims=True)
    grp = jnp.min(jnp.where(cm == cmax, lane - N_EXPERTS, big), axis=-1, keepdims=True)
    pg_sel = 1.0 / jnp.sum(jnp.exp(cm - cmax), axis=-1, keepdims=True)
    in_grp = (lane_i < N_EXPERTS) & (lane_grp == grp)
    fm = jnp.where(in_grp, r, NEG)
    m1 = fm.max(axis=-1, keepdims=True)
    denom = jnp.sum(jnp.exp(fm - m1), axis=-1, keepdims=True)
    i1 = jnp.min(jnp.where(fm == m1, lane, big), axis=-1, keepdims=True)
    fm2 = jnp.where(lane == i1, NEG, fm)
    m2 = fm2.max(axis=-1, keepdims=True)
    i2 = jnp.min(jnp.where(fm2 == m2, lane, big), axis=-1, keepdims=True)
    p1 = 1.0 / denom
    p2 = jnp.exp(m2 - m1) / denom
    tot = p1 + p2
    return (jnp.where(lane == i1, pg_sel * (p1 / tot), 0.0)
            + jnp.where(lane == i2, pg_sel * (p2 / tot), 0.0))


def _moe_kernel(x_ref, g_ref, wr_ref, br_ref, w1_ref, w3_ref, w2_ref, y_ref, hx_sc, comb_sc, *, e_blk):
    j = pl.program_id(1)

    @pl.when(j == 0)
    def _():
        x = x_ref[...]
        hx = _rms(x, g_ref[...])
        hx_sc[...] = hx.astype(BF16)
        h_hi, h_mid, _ = _split3(hx)
        w_hi, w_mid, _ = _split3(wr_ref[...])
        r = _dot(h_hi, w_hi) + _dot(h_hi, w_mid) + _dot(h_mid, w_hi) + br_ref[...]
        comb_sc[...] = _route(r)
        y_ref[...] = x

    hx = hx_sc[...]
    hid = jax.nn.silu(_dot(hx, w1_ref[...])) * _dot(hx, w3_ref[...])
    comb = comb_sc[...]
    lane = lax.broadcasted_iota(jnp.int32, comb.shape, 1)
    scale = []
    for e in range(e_blk):
        w_e = jnp.sum(jnp.where(lane == j * e_blk + e, comb, 0.0), axis=-1, keepdims=True)
        scale.append(jnp.broadcast_to(w_e, (comb.shape[0], D_EXPERT)))
    hid = hid * jnp.concatenate(scale, axis=-1)
    y_ref[...] += _dot(hid.astype(BF16), w2_ref[...])


def _moe(x, w):
    t_total = x.shape[0]
    tm = 512
    e_blk = 4
    cols = e_blk * D_EXPERT
    consts = [w["g_ffn"], w["w_router"], w["b_router"]]
    return pl.pallas_call(
        functools.partial(_moe_kernel, e_blk=e_blk),
        grid=(t_total // tm, N_EXPERTS // e_blk),
        in_specs=[pl.BlockSpec((tm, D_MODEL), lambda i, j: (i, 0))]
        + [_const_spec(c.shape) for c in consts]
        + [pl.BlockSpec((D_MODEL, cols), lambda i, j: (0, j)),
           pl.BlockSpec((D_MODEL, cols), lambda i, j: (0, j)),
           pl.BlockSpec((cols, D_MODEL), lambda i, j: (j, 0))],
        out_specs=pl.BlockSpec((tm, D_MODEL), lambda i, j: (i, 0)),
        out_shape=jax.ShapeDtypeStruct(x.shape, F32),
        scratch_shapes=[pltpu.VMEM((tm, D_MODEL), BF16), pltpu.VMEM((tm, LANES), F32)],
        compiler_params=_params(2),
        name="moe",
    )(x, *consts, w["w1"], w["w3"], w["w2"])


def _prep_weights(g_mix, w_in, b_f, q_norm_a, k_norm_a, q_norm_b, k_norm_b, w_pa, w_pb, w_o,
                  g_ffn, w_rg, b_rg, w_re, b_re, w1, w3, w2):
    n_qkv = 6 * WIDTH
    tile = lambda g: jnp.tile(g, N_HEADS)
    w_router = jnp.concatenate(
        [jnp.transpose(w_re, (1, 0, 2)).reshape(D_MODEL, N_EXPERTS), w_rg,
         jnp.zeros((D_MODEL, LANES - N_EXPERTS - N_GROUPS), F32)], axis=1)
    b_router = jnp.concatenate(
        [b_re.reshape(N_EXPERTS), b_rg, jnp.zeros((LANES - N_EXPERTS - N_GROUPS,), F32)])[None, :]
    return {
        "g_mix": g_mix[None, :],
        "w_qkv": w_in[:, :n_qkv].astype(BF16),
        "w_f": jnp.pad(w_in[:, n_qkv:n_qkv + N_HEADS], ((0, 0), (0, LANES - N_HEADS))).astype(BF16),
        "b_f": jnp.pad(b_f, (0, LANES - N_HEADS))[None, :],
        "w_gate": w_in[:, n_qkv + N_HEADS:].astype(BF16),
        "gains": jnp.stack([tile(q_norm_a), tile(k_norm_a), tile(q_norm_b), tile(k_norm_b)]),
        "w_pa": w_pa.astype(BF16), "w_pb": w_pb.astype(BF16), "w_o": w_o.astype(BF16),
        "g_ffn": g_ffn[None, :],
        "w_router": w_router, "b_router": b_router,
        "w1": jnp.transpose(w1, (1, 0, 2)).reshape(D_MODEL, N_EXPERTS * D_EXPERT).astype(BF16),
        "w3": jnp.transpose(w3, (1, 0, 2)).reshape(D_MODEL, N_EXPERTS * D_EXPERT).astype(BF16),
        "w2": w2.reshape(N_EXPERTS * D_EXPERT, D_MODEL).astype(BF16),
    }


def kernel(x_prompt, x_sample, cache_a_k, cache_a_v, cache_b_k, cache_b_v, cache_b_logf, g_mix, w_in, b_f, q_norm_a, k_norm_a, q_norm_b, k_norm_b, rel_bias, w_pa, w_pb, w_o, g_ffn, w_rg, b_rg, w_re, b_re, w1, w3, w2):
    assert g_mix.shape[0] == 1, "single-layer step"
    n_p, seq, _ = x_prompt.shape
    n_s, t_s, _ = x_sample.shape
    a_rows = cache_a_k.shape[2]
    w = _prep_weights(g_mix[0], w_in[0], b_f[0], q_norm_a[0], k_norm_a[0], q_norm_b[0], k_norm_b[0],
                      w_pa[0], w_pb[0], w_o[0], g_ffn[0], w_rg[0], b_rg[0], w_re[0], b_re[0],
                      w1[0], w3[0], w2[0])
    band_tq = 256
    bias_prompt = _band_bias(rel_bias[0], band_tq, 3 * band_tq, 2 * band_tq, band_mask=True)
    bias_sample = _band_bias(rel_bias[0], t_s, a_rows + t_s, a_rows, band_mask=False)

    seq_minor = lambda a: jnp.transpose(a, (0, 2, 3, 1))
    seq_major = lambda a: jnp.transpose(a, (0, 3, 1, 2))[None]

    xp = x_prompt.reshape(n_p * seq, D_MODEL)
    (qa, ka, va, ka_t, va_t, qcat, kcat, vb, kb_t, vb_t, logf_t) = _inproj(xp, n_p, w, seq_minor=True)
    o_a = _band_prompt(qa, ka, va, bias_prompt, n_p)
    o_b = _fox_prompt(qcat, kcat, vb, n_p)
    y_p = _moe(_merge(xp, o_a, o_b, w), w)

    xs = x_sample.reshape(n_s * t_s, D_MODEL)
    (qa_s, _, _, ka_fs, va_fs, qcat_s, kcat_s, vb_s, kb_fs, vb_fs, logf_s) = _inproj(
        xs, 1, w, seq_minor=False)
    o_as, new_ak, new_av = _band_sample(
        qa_s, ka_fs, va_fs, cache_a_k[0].reshape(n_s, a_rows, WIDTH),
        cache_a_v[0].reshape(n_s, a_rows, WIDTH), bias_sample)
    lf_cache_t = jnp.transpose(cache_b_logf[0], (0, 2, 1))
    lf_new_t = jnp.transpose(logf_s.reshape(n_s, t_s, N_HEADS), (0, 2, 1))
    o_bs = _fox_sample(qcat_s, kcat_s, vb_s, seq_minor(cache_b_k[0]), seq_minor(cache_b_v[0]),
                       lf_cache_t, lf_new_t)
    y_s = _moe(_merge(xs, o_as, o_bs, w), w)

    heads = lambda a, n, r: a.reshape(1, n, r, N_HEADS, HEAD_DIM)
    return (y_p.reshape(n_p, seq, D_MODEL), y_s.reshape(n_s, t_s, D_MODEL),
            seq_major(ka_t), seq_major(va_t), seq_major(kb_t), seq_major(vb_t),
            jnp.transpose(logf_t, (0, 2, 1))[None],
            heads(new_ak, n_s, a_rows), heads(new_av, n_s, a_rows),
            heads(kb_fs, n_s, t_s), heads(vb_fs, n_s, t_s), logf_s.reshape(1, n_s, t_s, N_HEADS))
```

```python
import functools

import numpy as np
import jax
import jax.numpy as jnp
from jax import lax
from jax.experimental import pallas as pl
from jax.experimental.pallas import tpu as pltpu

F32 = jnp.float32
BF16 = jnp.bfloat16

D_MODEL = 1024
HEAD_DIM = 64
N_HEADS = 8
WIDTH = N_HEADS * HEAD_DIM
N_PAIRS = N_HEADS // 2
CHUNK = 64
LEFT_CHUNKS = 8
WINDOW_ROWS = LEFT_CHUNKS * CHUNK
MAX_REL = 256
N_GROUPS = 4
EXPERTS_PER_GROUP = 8
N_EXPERTS = N_GROUPS * EXPERTS_PER_GROUP
D_EXPERT = 128
EPS = 1e-6
NEG = -1e30
LOG2E = 1.4426950408889634
LANES = 128
AUG_STRIDE = 8
VMEM_LIMIT = 56 * 1024 * 1024

_NT = (((1,), (1,)), ((), ()))


def _dot(a, b):
    return jnp.dot(a, b, preferred_element_type=F32)


def _dot_nt(a, b):
    return lax.dot_general(a, b, _NT, preferred_element_type=F32)


def _split3(x):
    hi = x.astype(BF16)
    r = x - hi.astype(F32)
    mid = r.astype(BF16)
    lo = (r - mid.astype(F32)).astype(BF16)
    return hi, mid, lo


def _dot3(a_bf, x):
    hi, mid, lo = _split3(x)
    return _dot(a_bf, hi) + _dot(a_bf, mid) + _dot(a_bf, lo)


def _rms(x, g):
    ms = jnp.mean(x * x, axis=-1, keepdims=True)
    return x * lax.rsqrt(ms + EPS) * g


def _params(n_axes):
    return pltpu.CompilerParams(dimension_semantics=("arbitrary",) * n_axes,
                                vmem_limit_bytes=VMEM_LIMIT)


def _const_spec(shape):
    nd = len(shape)
    return pl.BlockSpec(shape, lambda *_: (0,) * nd)


def _inproj_kernel(x_ref, g_ref, wqkv_ref, wf_ref, bf_ref, gains_ref, bd_ref, ltri_ref,
                   selq_ref, selk_ref, oneq_ref, onek_ref,
                   qa_ref, ka_ref, va_ref, kaf_ref, vaf_ref, qcat_ref, kcat_ref, vb_ref,
                   kbf_ref, vbf_ref, logf_ref, carry_ref, *, tiles_per_seq, seq_minor):
    i = pl.program_id(0)
    tm = x_ref.shape[0]

    @pl.when(i % tiles_per_seq == 0)
    def _():
        carry_ref[...] = jnp.zeros_like(carry_ref)

    def put_f32(ref, y, last_tile_only=False):
        if not seq_minor:
            ref[...] = y
        elif last_tile_only:
            @pl.when(i % tiles_per_seq == tiles_per_seq - 1)
            def _():
                ref[...] = y.T.reshape(N_HEADS, HEAD_DIM, tm)
        else:
            ref[...] = y.T.reshape(N_HEADS, HEAD_DIM, tm)

    h = _rms(x_ref[...], g_ref[...]).astype(BF16)

    def seg(s):
        return _dot(h, wqkv_ref[:, s * WIDTH:(s + 1) * WIDTH])

    def headnorm(y, n):
        ss = _dot((y * y).astype(BF16), bd_ref[...])
        return y * lax.rsqrt(ss * (1.0 / HEAD_DIM) + EPS) * gains_ref[n:n + 1, :]

    q_a = headnorm(seg(0), 0)
    qa_ref[...] = (q_a * (HEAD_DIM ** -0.5)).astype(BF16)
    k_a = headnorm(seg(1), 1)
    ka_ref[...] = k_a.astype(BF16)
    put_f32(kaf_ref, k_a, last_tile_only=True)
    v_a = seg(2)
    va_ref[...] = v_a.astype(BF16)
    put_f32(vaf_ref, v_a, last_tile_only=True)

    z = _dot(h, wf_ref[...]) + bf_ref[...]
    logf = jnp.minimum(z, 0.0) - jnp.log(1.0 + jnp.exp(-jnp.abs(z)))
    if seq_minor:
        logf_ref[...] = logf.T[0:N_HEADS, :]
    else:
        logf_ref[...] = logf[:, :N_HEADS]
    c = _dot3(ltri_ref[...], logf) + carry_ref[0:1, :]
    carry_ref[...] = jnp.broadcast_to(c[-1:, :], carry_ref.shape)
    cs = jnp.concatenate(_split3(c * LOG2E), axis=-1)
    q_aug = (_dot(cs, selq_ref[...]) + oneq_ref[...]).astype(BF16)
    k_aug = (_dot(cs, selk_ref[...]) + onek_ref[...]).astype(BF16)

    q_b = (headnorm(seg(3), 2) * (HEAD_DIM ** -0.5 * LOG2E)).astype(BF16)
    k_b = headnorm(seg(4), 3)
    put_f32(kbf_ref, k_b)
    k_b = k_b.astype(BF16)
    v_b = seg(5)
    put_f32(vbf_ref, v_b)
    vb_ref[...] = v_b.astype(BF16)
    for p in range(N_PAIRS):
        cols = slice(p * LANES, (p + 1) * LANES)
        qcat_ref[p, :, 0:LANES] = q_b[:, cols]
        qcat_ref[p, :, LANES:2 * LANES] = q_aug
        kcat_ref[p, :, 0:LANES] = k_b[:, cols]
        kcat_ref[p, :, LANES:2 * LANES] = k_aug


def _aug_constants():
    selq = np.zeros((3 * LANES, LANES), np.float32)
    selk = np.zeros((3 * LANES, LANES), np.float32)
    oneq = np.zeros((1, LANES), np.float32)
    onek = np.zeros((1, LANES), np.float32)
    for h in range(N_HEADS):
        for k in range(3):
            selq[k * LANES + h, AUG_STRIDE * h + k] = 1.0
            selk[k * LANES + h, AUG_STRIDE * h + 3 + k] = -1.0
            oneq[0, AUG_STRIDE * h + 3 + k] = 1.0
            onek[0, AUG_STRIDE * h + k] = 1.0
    return (jnp.asarray(selq, BF16), jnp.asarray(selk, BF16), jnp.asarray(oneq), jnp.asarray(onek))


def _inproj(x, n_seq, w, seq_minor):
    t_total = x.shape[0]
    tm = 512
    n_tiles = t_total // tm
    seq = t_total // n_seq
    tps = seq // tm
    row = lambda i: (i, 0)
    flat = lambda dt: (jax.ShapeDtypeStruct((t_total, WIDTH), dt), pl.BlockSpec((tm, WIDTH), row))
    if seq_minor:
        assert tm == WINDOW_ROWS
        band_f32 = (jax.ShapeDtypeStruct((n_seq, N_HEADS, HEAD_DIM, WINDOW_ROWS), F32),
                    pl.BlockSpec((None, N_HEADS, HEAD_DIM, tm), lambda i: (i // tps, 0, 0, 0)))
        fox_f32 = (jax.ShapeDtypeStruct((n_seq, N_HEADS, HEAD_DIM, seq), F32),
                   pl.BlockSpec((None, N_HEADS, HEAD_DIM, tm), lambda i: (i // tps, 0, 0, i % tps)))
        logf = (jax.ShapeDtypeStruct((n_seq, N_HEADS, seq), F32),
                pl.BlockSpec((None, N_HEADS, tm), lambda i: (i // tps, 0, i % tps)))
    else:
        band_f32 = fox_f32 = flat(F32)
        logf = (jax.ShapeDtypeStruct((t_total, N_HEADS), F32), pl.BlockSpec((tm, N_HEADS), row))
    cat = (jax.ShapeDtypeStruct((n_seq, N_PAIRS, seq, 2 * LANES), BF16),
           pl.BlockSpec((None, N_PAIRS, tm, 2 * LANES), lambda i: (i // tps, 0, i % tps, 0)))
    outs = [flat(BF16), flat(BF16), flat(BF16), band_f32, band_f32, cat, cat, flat(BF16),
            fox_f32, fox_f32, logf]
    bd = jnp.asarray(np.kron(np.eye(N_HEADS), np.ones((HEAD_DIM, HEAD_DIM))), BF16)
    ltri = jnp.asarray(np.tril(np.ones((tm, tm))), BF16)
    consts = [w["g_mix"], w["w_qkv"], w["w_f"], w["b_f"], w["gains"], bd, ltri, *_aug_constants()]
    in_specs = [pl.BlockSpec((tm, D_MODEL), row)] + [_const_spec(c.shape) for c in consts]
    return pl.pallas_call(
        functools.partial(_inproj_kernel, tiles_per_seq=tps, seq_minor=seq_minor),
        grid=(n_tiles,),
        in_specs=in_specs,
        out_specs=[o[1] for o in outs],
        out_shape=[o[0] for o in outs],
        scratch_shapes=[pltpu.VMEM((8, LANES), F32)],
        compiler_params=_params(1),
        name="inproj",
    )(x, *consts)


def _softmax_pv(scores, values):
    chunks = [[s[:, c * LANES:(c + 1) * LANES] for c in range(s.shape[1] // LANES)] for s in scores]
    cmax = None
    for cs in chunks:
        for c in cs:
            cmax = c if cmax is None else jnp.maximum(cmax, c)
    m = jnp.max(cmax, axis=-1, keepdims=True)
    pv = None
    lsum = None
    for cs, v in zip(chunks, values):
        ps = [jnp.exp(c - m) for c in cs]
        for p in ps:
            lsum = p if lsum is None else lsum + p
        o = _dot(jnp.concatenate(ps, axis=-1).astype(BF16), v)
        pv = o if pv is None else pv + o
    return pv, jnp.sum(lsum, axis=-1, keepdims=True)


def _head_mask(hh):
    lane = lax.broadcasted_iota(jnp.int32, (1, LANES), 1)
    return (lane // HEAD_DIM) == hh


def _band_prompt_kernel(q_ref, k0_ref, k1_ref, k2_ref, v0_ref, v1_ref, v2_ref, bias_ref, o_ref, *, tq):
    i = pl.program_id(1)
    k_refs = (k0_ref, k1_ref, k2_ref)
    v_refs = (v0_ref, v1_ref, v2_ref)
    pad = [jnp.where(i - 2 + j >= 0, 0.0, NEG) for j in range(2)] + [0.0]
    for p in range(N_PAIRS):
        cols = slice(p * LANES, (p + 1) * LANES)
        q = q_ref[:, cols]
        ks = [r[:, cols] for r in k_refs]
        vs = [r[:, cols] for r in v_refs]
        out = jnp.zeros((tq, LANES), F32)
        for hh in range(2):
            msk = _head_mask(hh)
            qh = jnp.where(msk, q, jnp.zeros_like(q))
            scores = [_dot_nt(qh, ks[j]) + bias_ref[2 * p + hh, :, j * tq:(j + 1) * tq] + pad[j]
                      for j in range(3)]
            pv, l = _softmax_pv(scores, vs)
            out = jnp.where(msk, pv / l, out)
        o_ref[:, cols] = out.astype(BF16)


def _band_prompt(qa, ka, va, bias, n_seq):
    t_total = qa.shape[0]
    tq = 256
    seq = t_total // n_seq
    nq = seq // tq
    qmap = lambda b, i: (b * nq + i, 0)

    def kmap(j):
        return lambda b, i: (b * nq + jnp.maximum(i - 2 + j, 0), 0)

    blk = lambda m: pl.BlockSpec((tq, WIDTH), m)
    return pl.pallas_call(
        functools.partial(_band_prompt_kernel, tq=tq),
        grid=(n_seq, nq),
        in_specs=[blk(qmap)] + [blk(kmap(j)) for j in range(3)] + [blk(kmap(j)) for j in range(3)]
        + [_const_spec(bias.shape)],
        out_specs=blk(qmap),
        out_shape=jax.ShapeDtypeStruct((t_total, WIDTH), BF16),
        compiler_params=_params(2),
        name="band_prompt",
    )(qa, ka, ka, ka, va, va, va, bias)


def _band_sample_kernel(q_ref, kc_ref, vc_ref, kn_ref, vn_ref, bias_ref, o_ref, ko_ref, vo_ref):
    p_rows = kc_ref.shape[0]
    t = q_ref.shape[0]
    for p in range(N_PAIRS):
        cols = slice(p * LANES, (p + 1) * LANES)
        q = q_ref[:, cols]
        pad = jnp.zeros((LANES - t, LANES), BF16)
        ks = [kc_ref[:, cols].astype(BF16), jnp.concatenate([kn_ref[:, cols].astype(BF16), pad], axis=0)]
        vs = [vc_ref[:, cols].astype(BF16), jnp.concatenate([vn_ref[:, cols].astype(BF16), pad], axis=0)]
        out = jnp.zeros((t, LANES), F32)
        for hh in range(2):
            msk = _head_mask(hh)
            qh = jnp.where(msk, q, jnp.zeros_like(q))
            scores = [_dot_nt(qh, ks[0]) + bias_ref[2 * p + hh, :, 0:p_rows],
                      _dot_nt(qh, ks[1]) + bias_ref[2 * p + hh, :, p_rows:p_rows + LANES]]
            pv, l = _softmax_pv(scores, vs)
            out = jnp.where(msk, pv / l, out)
        o_ref[:, cols] = out.astype(BF16)
    ko_ref[0:p_rows - t, :] = kc_ref[t:p_rows, :]
    ko_ref[p_rows - t:p_rows, :] = kn_ref[...]
    vo_ref[0:p_rows - t, :] = vc_ref[t:p_rows, :]
    vo_ref[p_rows - t:p_rows, :] = vn_ref[...]


def _band_sample(qa, ka_f, va_f, cache_k, cache_v, bias):
    n, p_rows, _ = cache_k.shape
    t = qa.shape[0] // n
    row = pl.BlockSpec((t, WIDTH), lambda b: (b, 0))
    cache = pl.BlockSpec((None, p_rows, WIDTH), lambda b: (b, 0, 0))
    return pl.pallas_call(
        _band_sample_kernel,
        grid=(n,),
        in_specs=[row, cache, cache, row, row, _const_spec(bias.shape)],
        out_specs=[row, cache, cache],
        out_shape=[jax.ShapeDtypeStruct(qa.shape, BF16),
                   jax.ShapeDtypeStruct(cache_k.shape, F32),
                   jax.ShapeDtypeStruct(cache_v.shape, F32)],
        compiler_params=_params(1),
        name="band_sample",
    )(qa, cache_k, cache_v, ka_f, va_f, bias)


def _band_bias(rel_bias, n_q, n_k, offset, band_mask):
    qi = np.arange(n_q)[:, None]
    kj = np.arange(n_k)[None, :]
    n_u = n_q + n_k - 1
    w_of_m = np.where(np.arange(n_u) < n_k, np.arange(n_u), np.arange(n_u) - n_u)
    idx_m = np.clip(offset - w_of_m, -MAX_REL, MAX_REL) + MAX_REL
    e = jnp.take(rel_bias.astype(F32), jnp.asarray(idx_m), axis=1)
    n_h = rel_bias.shape[0]
    bias = jnp.tile(e, (1, n_q + 1))[:, :n_q * (n_u - 1)].reshape(n_h, n_q, n_u - 1)[:, :, :n_k]
    if band_mask:
        rel_chunk = (qi + offset) // CHUNK - kj // CHUNK
        valid = (rel_chunk >= 0) & (rel_chunk <= LEFT_CHUNKS)
        bias = jnp.where(jnp.asarray(valid)[None], bias, NEG)
    return bias


def _aug_head_mask(h_in_pair, pair):
    lane = lax.broadcasted_iota(jnp.int32, (1, 2 * LANES), 1)
    head = 2 * pair + h_in_pair
    in_q = (lane < LANES) & ((lane // HEAD_DIM) == h_in_pair)
    in_aug = (lane >= LANES) & (((lane - LANES) // AUG_STRIDE) == head)
    return in_q | in_aug


def _flash_update(s, v, m_ref, l_ref, acc_ref, h, row_bias=None, v_seq_minor=False):
    chunks = [s[:, c * LANES:(c + 1) * LANES] for c in range(s.shape[1] // LANES)]
    if row_bias is not None:
        chunks = [c + row_bias for c in chunks]
    cmax = chunks[0]
    for c in chunks[1:]:
        cmax = jnp.maximum(cmax, c)
    m_prev = m_ref[h]
    m_new = jnp.maximum(m_prev, jnp.max(cmax, axis=-1, keepdims=True))
    alpha = jnp.exp2(m_prev - m_new)
    ps = [jnp.exp2(c - m_new) for c in chunks]
    lsum = ps[0]
    for p in ps[1:]:
        lsum = lsum + p
    l_ref[h] = alpha * l_ref[h] + lsum
    p = jnp.concatenate(ps, axis=-1).astype(BF16)
    pv = _dot_nt(p, v) if v_seq_minor else _dot(p, v)
    acc_ref[h] = alpha[:, :pv.shape[1]] * acc_ref[h] + pv
    m_ref[h] = m_new


def _flash_result(l_ref, acc_ref, pair):
    outs = [acc_ref[2 * pair + hh] / jnp.sum(l_ref[2 * pair + hh], axis=-1, keepdims=True)
            for hh in range(2)]
    return jnp.where(_head_mask(0), outs[0], outs[1])


def _fox_prompt_kernel(qi_ref, kj_ref, q_ref, k_ref, v_ref, o_ref, qh_sc, m_sc, l_sc, acc_sc, *, tq, tk):
    t = pl.program_id(1)
    i = qi_ref[t]
    j = kj_ref[t]

    @pl.when(j == 0)
    def _():
        for p in range(N_PAIRS):
            q = q_ref[p]
            for hh in range(2):
                qh_sc[2 * p + hh] = jnp.where(_aug_head_mask(hh, p), q, jnp.zeros_like(q))
        m_sc[...] = jnp.full_like(m_sc, NEG)
        l_sc[...] = jnp.zeros_like(l_sc)
        acc_sc[...] = jnp.zeros_like(acc_sc)

    def step(diagonal):
        if diagonal:
            keep = (lax.broadcasted_iota(jnp.int32, (tq, tk), 1)
                    <= lax.broadcasted_iota(jnp.int32, (tq, tk), 0))
        for p in range(N_PAIRS):
            k = k_ref[p]
            v = v_ref[:, p * LANES:(p + 1) * LANES]
            for hh in range(2):
                s = _dot_nt(qh_sc[2 * p + hh], k)
                if diagonal:
                    s = jnp.where(keep, s, NEG)
                _flash_update(s, v, m_sc, l_sc, acc_sc, 2 * p + hh)

    @pl.when(j < i)
    def _():
        step(False)

    @pl.when(j == i)
    def _():
        step(True)
        for p in range(N_PAIRS):
            o_ref[:, p * LANES:(p + 1) * LANES] = _flash_result(l_sc, acc_sc, p).astype(BF16)


def _fox_prompt(qcat, kcat, vb, n_seq):
    seq = qcat.shape[2]
    tq = tk = 512
    nt = seq // tq
    qi = np.concatenate([np.full(i + 1, i) for i in range(nt)]).astype(np.int32)
    kj = np.concatenate([np.arange(i + 1) for i in range(nt)]).astype(np.int32)
    grid_spec = pltpu.PrefetchScalarGridSpec(
        num_scalar_prefetch=2,
        grid=(n_seq, len(qi)),
        in_specs=[
            pl.BlockSpec((None, N_PAIRS, tq, 2 * LANES), lambda b, t, qi, kj: (b, 0, qi[t], 0)),
            pl.BlockSpec((None, N_PAIRS, tk, 2 * LANES), lambda b, t, qi, kj: (b, 0, kj[t], 0)),
            pl.BlockSpec((tk, WIDTH), lambda b, t, qi, kj: (b * nt + kj[t], 0)),
        ],
        out_specs=pl.BlockSpec((tq, WIDTH), lambda b, t, qi, kj: (b * nt + qi[t], 0)),
        scratch_shapes=[pltpu.VMEM((N_HEADS, tq, 2 * LANES), BF16), pltpu.VMEM((N_HEADS, tq, LANES), F32),
                        pltpu.VMEM((N_HEADS, tq, LANES), F32), pltpu.VMEM((N_HEADS, tq, LANES), F32)],
    )
    return pl.pallas_call(
        functools.partial(_fox_prompt_kernel, tq=tq, tk=tk),
        grid_spec=grid_spec,
        out_shape=jax.ShapeDtypeStruct(vb.shape, BF16),
        compiler_params=_params(2),
        name="fox_prompt",
    )(jnp.asarray(qi), jnp.asarray(kj), qcat, kcat, vb)


def _suffix_sum_exclusive(x):
    n = x.shape[1]
    lane = lax.broadcasted_iota(jnp.int32, x.shape, 1)
    y = jnp.where(lane + 1 < n, pltpu.roll(x, n - 1, axis=1), 0.0)
    shift = 1
    while shift < n:
        y = y + jnp.where(lane + shift < n, pltpu.roll(y, n - shift, axis=1), 0.0)
        shift *= 2
    return y


def _fox_sample_kernel(q_ref, kc_ref, vc_ref, kn_ref, vn_ref, lfc_ref, lfn_ref, u_ref,
                       o_ref, cq_sc, car_sc, m_sc, l_sc, acc_sc, *, n_cache_tiles, sub_keys):
    j = pl.program_id(1)
    t = q_ref.shape[1]

    def cum_new():
        hi, mid, lo = _split3(lfn_ref[...])
        u = u_ref[...]
        return _dot(hi, u) + _dot(mid, u) + _dot(lo, u)

    def q_head(h):
        return q_ref[h // 2, :, (h % 2) * HEAD_DIM:(h % 2 + 1) * HEAD_DIM]

    @pl.when(j == 0)
    def _():
        cn = cum_new() * LOG2E
        eye = (lax.broadcasted_iota(jnp.int32, (t, LANES), 0)
               == lax.broadcasted_iota(jnp.int32, (t, LANES), 1))
        for h in range(N_HEADS):
            col = jnp.sum(jnp.where(eye, jnp.broadcast_to(cn[h:h + 1, :], (t, LANES)), 0.0),
                          axis=-1, keepdims=True)
            cq_sc[h] = jnp.broadcast_to(col, (t, LANES))
        car_sc[...] = jnp.zeros_like(car_sc)
        m_sc[...] = jnp.full_like(m_sc, NEG)
        l_sc[...] = jnp.zeros_like(l_sc)
        acc_sc[...] = jnp.zeros_like(acc_sc)

    @pl.when(j < n_cache_tiles)
    def _():
        lf = lfc_ref[...]
        r = _suffix_sum_exclusive(lf) + car_sc[:, 0:1]
        car_sc[...] = jnp.broadcast_to(r[:, 0:1] + lf[:, 0:1], car_sc.shape)
        r = r * LOG2E
        for sub in range(lf.shape[1] // sub_keys):
            keys = slice(sub * sub_keys, (sub + 1) * sub_keys)
            for h in range(N_HEADS):
                s = _dot(q_head(h), kc_ref[h, :, keys].astype(BF16)) + r[h:h + 1, keys]
                _flash_update(s, vc_ref[h, :, keys].astype(BF16), m_sc, l_sc, acc_sc, h,
                              row_bias=cq_sc[h], v_seq_minor=True)

    @pl.when(j == n_cache_tiles)
    def _():
        cn = cum_new() * LOG2E
        causal = (lax.broadcasted_iota(jnp.int32, (t, LANES), 1)
                  <= lax.broadcasted_iota(jnp.int32, (t, LANES), 0))
        pad = jnp.zeros((LANES - t, HEAD_DIM), BF16)
        for h in range(N_HEADS):
            cols = slice(h * HEAD_DIM, (h + 1) * HEAD_DIM)
            k = jnp.concatenate([kn_ref[h // 2, :, (h % 2) * HEAD_DIM:(h % 2 + 1) * HEAD_DIM], pad], axis=0)
            v = jnp.concatenate([vn_ref[:, cols], pad], axis=0)
            s = _dot_nt(q_head(h), k) + cq_sc[h] - cn[h:h + 1, :]
            _flash_update(jnp.where(causal, s, NEG), v, m_sc, l_sc, acc_sc, h)
            o_ref[:, cols] = (acc_sc[h] / jnp.sum(l_sc[h], axis=-1, keepdims=True)).astype(BF16)


def _fox_sample(qcat, kcat, vb, cache_kt, cache_vt, lf_cache_t, lf_new_t):
    n, _, _, p_rows = cache_kt.shape
    t = vb.shape[0] // n
    tk = 2048
    nct = p_rows // tk
    u = jnp.asarray(np.pad(np.triu(np.ones((t, t))), ((0, 0), (0, LANES - t))), BF16)
    rev = lambda j: jnp.maximum(nct - 1 - j, 0)
    cache = pl.BlockSpec((None, N_HEADS, HEAD_DIM, tk), lambda b, j: (b, 0, 0, rev(j)))
    new_cat = pl.BlockSpec((None, N_PAIRS, t, 2 * LANES), lambda b, j: (0, 0, b, 0))
    return pl.pallas_call(
        functools.partial(_fox_sample_kernel, n_cache_tiles=nct, sub_keys=tk),
        grid=(n, nct + 1),
        in_specs=[
            new_cat, cache, cache, new_cat,
            pl.BlockSpec((t, WIDTH), lambda b, j: (b, 0)),
            pl.BlockSpec((None, N_HEADS, tk), lambda b, j: (b, 0, rev(j))),
            pl.BlockSpec((None, N_HEADS, t), lambda b, j: (b, 0, 0)),
            _const_spec(u.shape),
        ],
        out_specs=pl.BlockSpec((t, WIDTH), lambda b, j: (b, 0)),
        out_shape=jax.ShapeDtypeStruct(vb.shape, BF16),
        scratch_shapes=[pltpu.VMEM((N_HEADS, t, LANES), F32), pltpu.VMEM((N_HEADS, LANES), F32),
                        pltpu.VMEM((N_HEADS, t, LANES), F32), pltpu.VMEM((N_HEADS, t, LANES), F32),
                        pltpu.VMEM((N_HEADS, t, HEAD_DIM), F32)],
        compiler_params=_params(2),
        name="fox_sample",
    )(qcat, cache_kt, cache_vt, kcat, vb, lf_cache_t, lf_new_t, u)


def _merge_kernel(x_ref, oa_ref, ob_ref, g_ref, wg_ref, wpa_ref, wpb_ref, wo_ref, y_ref):
    x = x_ref[...]
    h = _rms(x, g_ref[...]).astype(BF16)
    gate = jax.nn.sigmoid(_dot(h, wg_ref[...]))
    mix = (gate[:, :D_MODEL] * _dot(oa_ref[...], wpa_ref[...])
           + gate[:, D_MODEL:] * _dot(ob_ref[...], wpb_ref[...]))
    y_ref[...] = x + _dot(mix.astype(BF16), wo_ref[...])


def _merge(x, oa, ob, w):
    t_total = x.shape[0]
    tm = 512
    row = lambda n: pl.BlockSpec((tm, n), lambda i: (i, 0))
    consts = [w["g_mix"], w["w_gate"], w["w_pa"], w["w_pb"], w["w_o"]]
    return pl.pallas_call(
        _merge_kernel,
        grid=(t_total // tm,),
        in_specs=[row(D_MODEL), row(WIDTH), row(WIDTH)] + [_const_spec(c.shape) for c in consts],
        out_specs=row(D_MODEL),
        out_shape=jax.ShapeDtypeStruct(x.shape, F32),
        compiler_params=_params(1),
        name="merge",
    )(x, oa, ob, *consts)


def _route(r):
    lane_i = lax.broadcasted_iota(jnp.int32, r.shape, 1)
    lane = lane_i.astype(F32)
    lane_grp = (lane_i // EXPERTS_PER_GROUP).astype(F32)
    big = float(LANES)
    is_coarse = (lane_i >= N_EXPERTS) & (lane_i < N_EXPERTS + N_GROUPS)
    cm = jnp.where(is_coarse, r, NEG)
    cmax = cm.max(axis=-1, keepdims=True)
    grp = jnp.min(jnp.where(cm == cmax, lane - N_EXPERTS, big), axis=-1, keepdims=True)
    pg_sel = 1.0 / jnp.sum(jnp.exp(cm - cmax), axis=-1, keepdims=True)
    in_grp = (lane_i < N_EXPERTS) & (lane_grp == grp)
    fm = jnp.where(in_grp, r, NEG)
    m1 = fm.max(axis=-1, keepdims=True)
    denom = jnp.sum(jnp.exp(fm - m1), axis=-1, keepdims=True)
    i1 = jnp.min(jnp.where(fm == m1, lane, big), axis=-1, keepdims=True)
    fm2 = jnp.where(lane == i1, NEG, fm)
    m2 = fm2.max(axis=-1, keepdims=True)
    i2 = jnp.min(jnp.where(fm2 == m2, lane, big), axis=-1, keepdims=True)
    p1 = 1.0 / denom
    p2 = jnp.exp(m2 - m1) / denom
    tot = p1 + p2
    return (jnp.where(lane == i1, pg_sel * (p1 / tot), 0.0)
            + jnp.where(lane == i2, pg_sel * (p2 / tot), 0.0))


def _moe_kernel(x_ref, g_ref, wr_ref, br_ref, w1_ref, w3_ref, w2_ref, y_ref, hx_sc, comb_sc, *, e_blk):
    j = pl.program_id(1)

    @pl.when(j == 0)
    def _():
        x = x_ref[...]
        hx = _rms(x, g_ref[...])
        hx_sc[...] = hx.astype(BF16)
        h_hi, h_mid, _ = _split3(hx)
        w_hi, w_mid, _ = _split3(wr_ref[...])
        r = _dot(h_hi, w_hi) + _dot(h_hi, w_mid) + _dot(h_mid, w_hi) + br_ref[...]
        comb_sc[...] = _route(r)
        y_ref[...] = x

    hx = hx_sc[...]
    hid = jax.nn.silu(_dot(hx, w1_ref[...])) * _dot(hx, w3_ref[...])
    comb = comb_sc[...]
    lane = lax.broadcasted_iota(jnp.int32, comb.shape, 1)
    scale = []
    for e in range(e_blk):
        w_e = jnp.sum(jnp.where(lane == j * e_blk + e, comb, 0.0), axis=-1, keepdims=True)
        scale.append(jnp.broadcast_to(w_e, (comb.shape[0], D_EXPERT)))
    hid = hid * jnp.concatenate(scale, axis=-1)
    y_ref[...] += _dot(hid.astype(BF16), w2_ref[...])


def _moe(x, w):
    t_total = x.shape[0]
    tm = 512
    e_blk = 4
    cols = e_blk * D_EXPERT
    consts = [w["g_ffn"], w["w_router"], w["b_router"]]
    return pl.pallas_call(
        functools.partial(_moe_kernel, e_blk=e_blk),
        grid=(t_total // tm, N_EXPERTS // e_blk),
        in_specs=[pl.BlockSpec((tm, D_MODEL), lambda i, j: (i, 0))]
        + [_const_spec(c.shape) for c in consts]
        + [pl.BlockSpec((D_MODEL, cols), lambda i, j: (0, j)),
           pl.BlockSpec((D_MODEL, cols), lambda i, j: (0, j)),
           pl.BlockSpec((cols, D_MODEL), lambda i, j: (j, 0))],
        out_specs=pl.BlockSpec((tm, D_MODEL), lambda i, j: (i, 0)),
        out_shape=jax.ShapeDtypeStruct(x.shape, F32),
        scratch_shapes=[pltpu.VMEM((tm, D_MODEL), BF16), pltpu.VMEM((tm, LANES), F32)],
        compiler_params=_params(2),
        name="moe",
    )(x, *consts, w["w1"], w["w3"], w["w2"])


def _prep_weights(g_mix, w_in, b_f, q_norm_a, k_norm_a, q_norm_b, k_norm_b, w_pa, w_pb, w_o,
                  g_ffn, w_rg, b_rg, w_re, b_re, w1, w3, w2):
    n_qkv = 6 * WIDTH
    tile = lambda g: jnp.tile(g, N_HEADS)
    w_router = jnp.concatenate(
        [jnp.transpose(w_re, (1, 0, 2)).reshape(D_MODEL, N_EXPERTS), w_rg,
         jnp.zeros((D_MODEL, LANES - N_EXPERTS - N_GROUPS), F32)], axis=1)
    b_router = jnp.concatenate(
        [b_re.reshape(N_EXPERTS), b_rg, jnp.zeros((LANES - N_EXPERTS - N_GROUPS,), F32)])[None, :]
    return {
        "g_mix": g_mix[None, :],
        "w_qkv": w_in[:, :n_qkv].astype(BF16),
        "w_f": jnp.pad(w_in[:, n_qkv:n_qkv + N_HEADS], ((0, 0), (0, LANES - N_HEADS))).astype(BF16),
        "b_f": jnp.pad(b_f, (0, LANES - N_HEADS))[None, :],
        "w_gate": w_in[:, n_qkv + N_HEADS:].astype(BF16),
        "gains": jnp.stack([tile(q_norm_a), tile(k_norm_a), tile(q_norm_b), tile(k_norm_b)]),
        "w_pa": w_pa.astype(BF16), "w_pb": w_pb.astype(BF16), "w_o": w_o.astype(BF16),
        "g_ffn": g_ffn[None, :],
        "w_router": w_router, "b_router": b_router,
        "w1": jnp.transpose(w1, (1, 0, 2)).reshape(D_MODEL, N_EXPERTS * D_EXPERT).astype(BF16),
        "w3": jnp.transpose(w3, (1, 0, 2)).reshape(D_MODEL, N_EXPERTS * D_EXPERT).astype(BF16),
        "w2": w2.reshape(N_EXPERTS * D_EXPERT, D_MODEL).astype(BF16),
    }


def kernel(x_prompt, x_sample, cache_a_k, cache_a_v, cache_b_k, cache_b_v, cache_b_logf, g_mix, w_in, b_f, q_norm_a, k_norm_a, q_norm_b, k_norm_b, rel_bias, w_pa, w_pb, w_o, g_ffn, w_rg, b_rg, w_re, b_re, w1, w3, w2):
    assert g_mix.shape[0] == 1, "single-layer step"
    n_p, seq, _ = x_prompt.shape
    n_s, t_s, _ = x_sample.shape
    a_rows = cache_a_k.shape[2]
    w = _prep_weights(g_mix[0], w_in[0], b_f[0], q_norm_a[0], k_norm_a[0], q_norm_b[0], k_norm_b[0],
                      w_pa[0], w_pb[0], w_o[0], g_ffn[0], w_rg[0], b_rg[0], w_re[0], b_re[0],
                      w1[0], w3[0], w2[0])
    band_tq = 256
    bias_prompt = _band_bias(rel_bias[0], band_tq, 3 * band_tq, 2 * band_tq, band_mask=True)
    bias_sample = jnp.pad(_band_bias(rel_bias[0], t_s, a_rows + t_s, a_rows, band_mask=False),
                          ((0, 0), (0, 0), (0, LANES - t_s)), constant_values=NEG)

    seq_minor = lambda a: jnp.transpose(a, (0, 2, 3, 1))
    seq_major = lambda a: jnp.transpose(a, (0, 3, 1, 2))[None]

    xp = x_prompt.reshape(n_p * seq, D_MODEL)
    (qa, ka, va, ka_t, va_t, qcat, kcat, vb, kb_t, vb_t, logf_t) = _inproj(xp, n_p, w, seq_minor=True)
    o_a = _band_prompt(qa, ka, va, bias_prompt, n_p)
    o_b = _fox_prompt(qcat, kcat, vb, n_p)
    y_p = _moe(_merge(xp, o_a, o_b, w), w)

    xs = x_sample.reshape(n_s * t_s, D_MODEL)
    (qa_s, _, _, ka_fs, va_fs, qcat_s, kcat_s, vb_s, kb_fs, vb_fs, logf_s) = _inproj(
        xs, 1, w, seq_minor=False)
    o_as, new_ak, new_av = _band_sample(
        qa_s, ka_fs, va_fs, cache_a_k[0].reshape(n_s, a_rows, WIDTH),
        cache_a_v[0].reshape(n_s, a_rows, WIDTH), bias_sample)
    lf_cache_t = jnp.transpose(cache_b_logf[0], (0, 2, 1))
    lf_new_t = jnp.transpose(logf_s.reshape(n_s, t_s, N_HEADS), (0, 2, 1))
    o_bs = _fox_sample(qcat_s, kcat_s, vb_s, seq_minor(cache_b_k[0]), seq_minor(cache_b_v[0]),
                       lf_cache_t, lf_new_t)
    y_s = _moe(_merge(xs, o_as, o_bs, w), w)

    heads = lambda a, n, r: a.reshape(1, n, r, N_HEADS, HEAD_DIM)
    return (y_p.reshape(n_p, seq, D_MODEL), y_s.reshape(n_s, t_s, D_MODEL),
            seq_major(ka_t), seq_major(va_t), seq_major(kb_t), seq_major(vb_t),
            jnp.transpose(logf_t, (0, 2, 1))[None],
            heads(new_ak, n_s, a_rows), heads(new_av, n_s, a_rows),
            heads(kb_fs, n_s, t_s), heads(vb_fs, n_s, t_s), logf_s.reshape(1, n_s, t_s, N_HEADS))
```

```python
import functools

import numpy as np
import jax
import jax.numpy as jnp
from jax import lax
from jax.experimental import pallas as pl
from jax.experimental.pallas import tpu as pltpu

F32 = jnp.float32
BF16 = jnp.bfloat16

D_MODEL = 1024
HEAD_DIM = 64
N_HEADS = 8
WIDTH = N_HEADS * HEAD_DIM
N_PAIRS = N_HEADS // 2
CHUNK = 64
LEFT_CHUNKS = 8
WINDOW_ROWS = LEFT_CHUNKS * CHUNK
MAX_REL = 256
N_GROUPS = 4
EXPERTS_PER_GROUP = 8
N_EXPERTS = N_GROUPS * EXPERTS_PER_GROUP
D_EXPERT = 128
EPS = 1e-6
NEG = -1e30
LOG2E = 1.4426950408889634
LANES = 128
AUG_STRIDE = 8
VMEM_LIMIT = 56 * 1024 * 1024

_NT = (((1,), (1,)), ((), ()))


def _dot(a, b):
    return jnp.dot(a, b, preferred_element_type=F32)


def _dot_nt(a, b):
    return lax.dot_general(a, b, _NT, preferred_element_type=F32)


def _split3(x):
    hi = x.astype(BF16)
    r = x - hi.astype(F32)
    mid = r.astype(BF16)
    lo = (r - mid.astype(F32)).astype(BF16)
    return hi, mid, lo


def _dot3(a_bf, x):
    hi, mid, lo = _split3(x)
    return _dot(a_bf, hi) + _dot(a_bf, mid) + _dot(a_bf, lo)


def _rms(x, g):
    ms = jnp.mean(x * x, axis=-1, keepdims=True)
    return x * lax.rsqrt(ms + EPS) * g


def _params(n_axes):
    return pltpu.CompilerParams(dimension_semantics=("arbitrary",) * n_axes,
                                vmem_limit_bytes=VMEM_LIMIT)


def _const_spec(shape):
    nd = len(shape)
    return pl.BlockSpec(shape, lambda *_: (0,) * nd)


def _inproj_kernel(x_ref, g_ref, wqkv_ref, wf_ref, bf_ref, gains_ref, bd_ref, ltri_ref,
                   selq_ref, selk_ref, oneq_ref, onek_ref,
                   qa_ref, ka_ref, va_ref, kaf_ref, vaf_ref, qcat_ref, kcat_ref, vb_ref,
                   kbf_ref, vbf_ref, logf_ref, carry_ref, *, tiles_per_seq, seq_minor):
    i = pl.program_id(0)
    tm = x_ref.shape[0]

    @pl.when(i % tiles_per_seq == 0)
    def _():
        carry_ref[...] = jnp.zeros_like(carry_ref)

    def put_f32(ref, y, last_tile_only=False):
        if not seq_minor:
            ref[...] = y
        elif last_tile_only:
            @pl.when(i % tiles_per_seq == tiles_per_seq - 1)
            def _():
                ref[...] = y.T.reshape(N_HEADS, HEAD_DIM, tm)
        else:
            ref[...] = y.T.reshape(N_HEADS, HEAD_DIM, tm)

    h = _rms(x_ref[...], g_ref[...]).astype(BF16)

    def seg(s):
        return _dot(h, wqkv_ref[:, s * WIDTH:(s + 1) * WIDTH])

    def headnorm(y, n):
        ss = _dot((y * y).astype(BF16), bd_ref[...])
        return y * lax.rsqrt(ss * (1.0 / HEAD_DIM) + EPS) * gains_ref[n:n + 1, :]

    q_a = headnorm(seg(0), 0)
    qa_ref[...] = (q_a * (HEAD_DIM ** -0.5)).astype(BF16)
    k_a = headnorm(seg(1), 1)
    ka_ref[...] = k_a.astype(BF16)
    put_f32(kaf_ref, k_a, last_tile_only=True)
    v_a = seg(2)
    va_ref[...] = v_a.astype(BF16)
    put_f32(vaf_ref, v_a, last_tile_only=True)

    z = _dot(h, wf_ref[...]) + bf_ref[...]
    logf = jnp.minimum(z, 0.0) - jnp.log(1.0 + jnp.exp(-jnp.abs(z)))
    if seq_minor:
        logf_ref[...] = logf.T[0:N_HEADS, :]
    else:
        logf_ref[...] = logf[:, :N_HEADS]
    c = _dot3(ltri_ref[...], logf) + carry_ref[0:1, :]
    carry_ref[...] = jnp.broadcast_to(c[-1:, :], carry_ref.shape)
    cs = jnp.concatenate(_split3(c * LOG2E), axis=-1)
    q_aug = (_dot(cs, selq_ref[...]) + oneq_ref[...]).astype(BF16)
    k_aug = (_dot(cs, selk_ref[...]) + onek_ref[...]).astype(BF16)

    q_b = (headnorm(seg(3), 2) * (HEAD_DIM ** -0.5 * LOG2E)).astype(BF16)
    k_b = headnorm(seg(4), 3)
    put_f32(kbf_ref, k_b)
    k_b = k_b.astype(BF16)
    v_b = seg(5)
    put_f32(vbf_ref, v_b)
    vb_ref[...] = v_b.astype(BF16)
    for p in range(N_PAIRS):
        cols = slice(p * LANES, (p + 1) * LANES)
        qcat_ref[p, :, 0:LANES] = q_b[:, cols]
        qcat_ref[p, :, LANES:2 * LANES] = q_aug
        kcat_ref[p, :, 0:LANES] = k_b[:, cols]
        kcat_ref[p, :, LANES:2 * LANES] = k_aug


def _aug_constants():
    selq = np.zeros((3 * LANES, LANES), np.float32)
    selk = np.zeros((3 * LANES, LANES), np.float32)
    oneq = np.zeros((1, LANES), np.float32)
    onek = np.zeros((1, LANES), np.float32)
    for h in range(N_HEADS):
        for k in range(3):
            selq[k * LANES + h, AUG_STRIDE * h + k] = 1.0
            selk[k * LANES + h, AUG_STRIDE * h + 3 + k] = -1.0
            oneq[0, AUG_STRIDE * h + 3 + k] = 1.0
            onek[0, AUG_STRIDE * h + k] = 1.0
    return (jnp.asarray(selq, BF16), jnp.asarray(selk, BF16), jnp.asarray(oneq), jnp.asarray(onek))


def _inproj(x, n_seq, w, seq_minor):
    t_total = x.shape[0]
    tm = 512
    n_tiles = t_total // tm
    seq = t_total // n_seq
    tps = seq // tm
    row = lambda i: (i, 0)
    flat = lambda dt: (jax.ShapeDtypeStruct((t_total, WIDTH), dt), pl.BlockSpec((tm, WIDTH), row))
    if seq_minor:
        assert tm == WINDOW_ROWS
        band_f32 = (jax.ShapeDtypeStruct((n_seq, N_HEADS, HEAD_DIM, WINDOW_ROWS), F32),
                    pl.BlockSpec((None, N_HEADS, HEAD_DIM, tm), lambda i: (i // tps, 0, 0, 0)))
        fox_f32 = (jax.ShapeDtypeStruct((n_seq, N_HEADS, HEAD_DIM, seq), F32),
                   pl.BlockSpec((None, N_HEADS, HEAD_DIM, tm), lambda i: (i // tps, 0, 0, i % tps)))
        logf = (jax.ShapeDtypeStruct((n_seq, N_HEADS, seq), F32),
                pl.BlockSpec((None, N_HEADS, tm), lambda i: (i // tps, 0, i % tps)))
    else:
        band_f32 = fox_f32 = flat(F32)
        logf = (jax.ShapeDtypeStruct((t_total, N_HEADS), F32), pl.BlockSpec((tm, N_HEADS), row))
    cat = (jax.ShapeDtypeStruct((n_seq, N_PAIRS, seq, 2 * LANES), BF16),
           pl.BlockSpec((None, N_PAIRS, tm, 2 * LANES), lambda i: (i // tps, 0, i % tps, 0)))
    outs = [flat(BF16), flat(BF16), flat(BF16), band_f32, band_f32, cat, cat, flat(BF16),
            fox_f32, fox_f32, logf]
    bd = jnp.asarray(np.kron(np.eye(N_HEADS), np.ones((HEAD_DIM, HEAD_DIM))), BF16)
    ltri = jnp.asarray(np.tril(np.ones((tm, tm))), BF16)
    consts = [w["g_mix"], w["w_qkv"], w["w_f"], w["b_f"], w["gains"], bd, ltri, *_aug_constants()]
    in_specs = [pl.BlockSpec((tm, D_MODEL), row)] + [_const_spec(c.shape) for c in consts]
    return pl.pallas_call(
        functools.partial(_inproj_kernel, tiles_per_seq=tps, seq_minor=seq_minor),
        grid=(n_tiles,),
        in_specs=in_specs,
        out_specs=[o[1] for o in outs],
        out_shape=[o[0] for o in outs],
        scratch_shapes=[pltpu.VMEM((8, LANES), F32)],
        compiler_params=_params(1),
        name="inproj",
    )(x, *consts)


def _softmax_pv(scores, values):
    chunks = [[s[:, c * LANES:(c + 1) * LANES] for c in range(s.shape[1] // LANES)] for s in scores]
    cmax = None
    for cs in chunks:
        for c in cs:
            cmax = c if cmax is None else jnp.maximum(cmax, c)
    m = jnp.max(cmax, axis=-1, keepdims=True)
    pv = None
    lsum = None
    for cs, v in zip(chunks, values):
        ps = [jnp.exp(c - m) for c in cs]
        for p in ps:
            lsum = p if lsum is None else lsum + p
        o = _dot(jnp.concatenate(ps, axis=-1).astype(BF16), v)
        pv = o if pv is None else pv + o
    return pv, jnp.sum(lsum, axis=-1, keepdims=True)


def _head_mask(hh):
    lane = lax.broadcasted_iota(jnp.int32, (1, LANES), 1)
    return (lane // HEAD_DIM) == hh


def _band_prompt_kernel(q_ref, k0_ref, k1_ref, k2_ref, v0_ref, v1_ref, v2_ref, bias_ref, o_ref, *, tq):
    i = pl.program_id(1)
    k_refs = (k0_ref, k1_ref, k2_ref)
    v_refs = (v0_ref, v1_ref, v2_ref)
    pad = [jnp.where(i - 2 + j >= 0, 0.0, NEG) for j in range(2)] + [0.0]
    for p in range(N_PAIRS):
        cols = slice(p * LANES, (p + 1) * LANES)
        q = q_ref[:, cols]
        ks = [r[:, cols] for r in k_refs]
        vs = [r[:, cols] for r in v_refs]
        out = jnp.zeros((tq, LANES), F32)
        for hh in range(2):
            msk = _head_mask(hh)
            qh = jnp.where(msk, q, jnp.zeros_like(q))
            scores = [_dot_nt(qh, ks[j]) + bias_ref[2 * p + hh, :, j * tq:(j + 1) * tq] + pad[j]
                      for j in range(3)]
            pv, l = _softmax_pv(scores, vs)
            out = jnp.where(msk, pv / l, out)
        o_ref[:, cols] = out.astype(BF16)


def _band_prompt(qa, ka, va, bias, n_seq):
    t_total = qa.shape[0]
    tq = 256
    seq = t_total // n_seq
    nq = seq // tq
    qmap = lambda b, i: (b * nq + i, 0)

    def kmap(j):
        return lambda b, i: (b * nq + jnp.maximum(i - 2 + j, 0), 0)

    blk = lambda m: pl.BlockSpec((tq, WIDTH), m)
    return pl.pallas_call(
        functools.partial(_band_prompt_kernel, tq=tq),
        grid=(n_seq, nq),
        in_specs=[blk(qmap)] + [blk(kmap(j)) for j in range(3)] + [blk(kmap(j)) for j in range(3)]
        + [_const_spec(bias.shape)],
        out_specs=blk(qmap),
        out_shape=jax.ShapeDtypeStruct((t_total, WIDTH), BF16),
        compiler_params=_params(2),
        name="band_prompt",
    )(qa, ka, ka, ka, va, va, va, bias)


def _band_sample_kernel(q_ref, kc_ref, vc_ref, kn_ref, vn_ref, bias_ref, o_ref, ko_ref, vo_ref):
    p_rows = kc_ref.shape[0]
    t = q_ref.shape[0]
    for p in range(N_PAIRS):
        cols = slice(p * LANES, (p + 1) * LANES)
        q = q_ref[:, cols]
        pad = jnp.zeros((LANES - t, LANES), BF16)
        ks = [kc_ref[:, cols].astype(BF16), jnp.concatenate([kn_ref[:, cols].astype(BF16), pad], axis=0)]
        vs = [vc_ref[:, cols].astype(BF16), jnp.concatenate([vn_ref[:, cols].astype(BF16), pad], axis=0)]
        out = jnp.zeros((t, LANES), F32)
        for hh in range(2):
            msk = _head_mask(hh)
            qh = jnp.where(msk, q, jnp.zeros_like(q))
            scores = [_dot_nt(qh, ks[0]) + bias_ref[2 * p + hh, :, 0:p_rows],
                      _dot_nt(qh, ks[1]) + bias_ref[2 * p + hh, :, p_rows:p_rows + LANES]]
            pv, l = _softmax_pv(scores, vs)
            out = jnp.where(msk, pv / l, out)
        o_ref[:, cols] = out.astype(BF16)
    ko_ref[0:p_rows - t, :] = kc_ref[t:p_rows, :]
    ko_ref[p_rows - t:p_rows, :] = kn_ref[...]
    vo_ref[0:p_rows - t, :] = vc_ref[t:p_rows, :]
    vo_ref[p_rows - t:p_rows, :] = vn_ref[...]


def _band_sample(qa, ka_f, va_f, cache_k, cache_v, bias):
    n, p_rows, _ = cache_k.shape
    t = qa.shape[0] // n
    row = pl.BlockSpec((t, WIDTH), lambda b: (b, 0))
    cache = pl.BlockSpec((None, p_rows, WIDTH), lambda b: (b, 0, 0))
    return pl.pallas_call(
        _band_sample_kernel,
        grid=(n,),
        in_specs=[row, cache, cache, row, row, _const_spec(bias.shape)],
        out_specs=[row, cache, cache],
        out_shape=[jax.ShapeDtypeStruct(qa.shape, BF16),
                   jax.ShapeDtypeStruct(cache_k.shape, F32),
                   jax.ShapeDtypeStruct(cache_v.shape, F32)],
        compiler_params=_params(1),
        name="band_sample",
    )(qa, cache_k, cache_v, ka_f, va_f, bias)


def _band_bias(rel_bias, n_q, n_k, offset, band_mask):
    qi = np.arange(n_q)[:, None]
    kj = np.arange(n_k)[None, :]
    n_u = n_q + n_k - 1
    w_of_m = np.where(np.arange(n_u) < n_k, np.arange(n_u), np.arange(n_u) - n_u)
    idx_m = np.clip(offset - w_of_m, -MAX_REL, MAX_REL) + MAX_REL
    e = jnp.take(rel_bias.astype(F32), jnp.asarray(idx_m), axis=1)
    n_h = rel_bias.shape[0]
    bias = jnp.tile(e, (1, n_q + 1))[:, :n_q * (n_u - 1)].reshape(n_h, n_q, n_u - 1)[:, :, :n_k]
    if band_mask:
        rel_chunk = (qi + offset) // CHUNK - kj // CHUNK
        valid = (rel_chunk >= 0) & (rel_chunk <= LEFT_CHUNKS)
        bias = jnp.where(jnp.asarray(valid)[None], bias, NEG)
    return bias


def _aug_head_mask(h_in_pair, pair):
    lane = lax.broadcasted_iota(jnp.int32, (1, 2 * LANES), 1)
    head = 2 * pair + h_in_pair
    in_q = (lane < LANES) & ((lane // HEAD_DIM) == h_in_pair)
    in_aug = (lane >= LANES) & (((lane - LANES) // AUG_STRIDE) == head)
    return in_q | in_aug


def _flash_update(s, v, m_ref, l_ref, acc_ref, h, row_bias=None, v_seq_minor=False):
    chunks = [s[:, c * LANES:(c + 1) * LANES] for c in range(s.shape[1] // LANES)]
    if row_bias is not None:
        chunks = [c + row_bias for c in chunks]
    cmax = chunks[0]
    for c in chunks[1:]:
        cmax = jnp.maximum(cmax, c)
    m_prev = m_ref[h]
    m_new = jnp.maximum(m_prev, jnp.max(cmax, axis=-1, keepdims=True))
    alpha = jnp.exp2(m_prev - m_new)
    ps = [jnp.exp2(c - m_new) for c in chunks]
    lsum = ps[0]
    for p in ps[1:]:
        lsum = lsum + p
    l_ref[h] = alpha * l_ref[h] + lsum
    p = jnp.concatenate(ps, axis=-1).astype(BF16)
    pv = _dot_nt(p, v) if v_seq_minor else _dot(p, v)
    acc_ref[h] = alpha[:, :pv.shape[1]] * acc_ref[h] + pv
    m_ref[h] = m_new


def _flash_result(l_ref, acc_ref, pair):
    outs = [acc_ref[2 * pair + hh] / jnp.sum(l_ref[2 * pair + hh], axis=-1, keepdims=True)
            for hh in range(2)]
    return jnp.where(_head_mask(0), outs[0], outs[1])


def _fox_prompt_kernel(qi_ref, kj_ref, q_ref, k_ref, v_ref, o_ref, qh_sc, m_sc, l_sc, acc_sc, *, tq, tk):
    t = pl.program_id(1)
    i = qi_ref[t]
    j = kj_ref[t]

    @pl.when(j == 0)
    def _():
        for p in range(N_PAIRS):
            q = q_ref[p]
            for hh in range(2):
                qh_sc[2 * p + hh] = jnp.where(_aug_head_mask(hh, p), q, jnp.zeros_like(q))
        m_sc[...] = jnp.full_like(m_sc, NEG)
        l_sc[...] = jnp.zeros_like(l_sc)
        acc_sc[...] = jnp.zeros_like(acc_sc)

    def step(diagonal):
        if diagonal:
            keep = (lax.broadcasted_iota(jnp.int32, (tq, tk), 1)
                    <= lax.broadcasted_iota(jnp.int32, (tq, tk), 0))
        for p in range(N_PAIRS):
            k = k_ref[p]
            v = v_ref[:, p * LANES:(p + 1) * LANES]
            for hh in range(2):
                s = _dot_nt(qh_sc[2 * p + hh], k)
                if diagonal:
                    s = jnp.where(keep, s, NEG)
                _flash_update(s, v, m_sc, l_sc, acc_sc, 2 * p + hh)

    @pl.when(j < i)
    def _():
        step(False)

    @pl.when(j == i)
    def _():
        step(True)
        for p in range(N_PAIRS):
            o_ref[:, p * LANES:(p + 1) * LANES] = _flash_result(l_sc, acc_sc, p).astype(BF16)


def _fox_prompt(qcat, kcat, vb, n_seq):
    seq = qcat.shape[2]
    tq = tk = 512
    nt = seq // tq
    qi = np.concatenate([np.full(i + 1, i) for i in range(nt)]).astype(np.int32)
    kj = np.concatenate([np.arange(i + 1) for i in range(nt)]).astype(np.int32)
    grid_spec = pltpu.PrefetchScalarGridSpec(
        num_scalar_prefetch=2,
        grid=(n_seq, len(qi)),
        in_specs=[
            pl.BlockSpec((None, N_PAIRS, tq, 2 * LANES), lambda b, t, qi, kj: (b, 0, qi[t], 0)),
            pl.BlockSpec((None, N_PAIRS, tk, 2 * LANES), lambda b, t, qi, kj: (b, 0, kj[t], 0)),
            pl.BlockSpec((tk, WIDTH), lambda b, t, qi, kj: (b * nt + kj[t], 0)),
        ],
        out_specs=pl.BlockSpec((tq, WIDTH), lambda b, t, qi, kj: (b * nt + qi[t], 0)),
        scratch_shapes=[pltpu.VMEM((N_HEADS, tq, 2 * LANES), BF16), pltpu.VMEM((N_HEADS, tq, LANES), F32),
                        pltpu.VMEM((N_HEADS, tq, LANES), F32), pltpu.VMEM((N_HEADS, tq, LANES), F32)],
    )
    return pl.pallas_call(
        functools.partial(_fox_prompt_kernel, tq=tq, tk=tk),
        grid_spec=grid_spec,
        out_shape=jax.ShapeDtypeStruct(vb.shape, BF16),
        compiler_params=_params(2),
        name="fox_prompt",
    )(jnp.asarray(qi), jnp.asarray(kj), qcat, kcat, vb)


def _suffix_sum_exclusive(x):
    n = x.shape[1]
    lane = lax.broadcasted_iota(jnp.int32, x.shape, 1)
    y = jnp.where(lane + 1 < n, pltpu.roll(x, n - 1, axis=1), 0.0)
    shift = 1
    while shift < n:
        y = y + jnp.where(lane + shift < n, pltpu.roll(y, n - shift, axis=1), 0.0)
        shift *= 2
    return y


def _fox_sample_kernel(q_ref, kc_ref, vc_ref, kn_ref, vn_ref, lfc_ref, lfn_ref, u_ref,
                       o_ref, cq_sc, car_sc, m_sc, l_sc, acc_sc, *, n_cache_tiles, sub_keys):
    j = pl.program_id(1)
    t = q_ref.shape[1]

    def cum_new():
        hi, mid, lo = _split3(lfn_ref[...])
        u = u_ref[...]
        return _dot(hi, u) + _dot(mid, u) + _dot(lo, u)

    def q_head(h):
        return q_ref[h // 2, :, (h % 2) * HEAD_DIM:(h % 2 + 1) * HEAD_DIM]

    @pl.when(j == 0)
    def _():
        cn = cum_new() * LOG2E
        eye = (lax.broadcasted_iota(jnp.int32, (t, LANES), 0)
               == lax.broadcasted_iota(jnp.int32, (t, LANES), 1))
        for h in range(N_HEADS):
            col = jnp.sum(jnp.where(eye, jnp.broadcast_to(cn[h:h + 1, :], (t, LANES)), 0.0),
                          axis=-1, keepdims=True)
            cq_sc[h] = jnp.broadcast_to(col, (t, LANES))
        car_sc[...] = jnp.zeros_like(car_sc)
        m_sc[...] = jnp.full_like(m_sc, NEG)
        l_sc[...] = jnp.zeros_like(l_sc)
        acc_sc[...] = jnp.zeros_like(acc_sc)

    @pl.when(j < n_cache_tiles)
    def _():
        lf = lfc_ref[...]
        r = _suffix_sum_exclusive(lf) + car_sc[:, 0:1]
        car_sc[...] = jnp.broadcast_to(r[:, 0:1] + lf[:, 0:1], car_sc.shape)
        r = r * LOG2E
        for sub in range(lf.shape[1] // sub_keys):
            keys = slice(sub * sub_keys, (sub + 1) * sub_keys)
            for h in range(N_HEADS):
                s = _dot(q_head(h), kc_ref[h, :, keys].astype(BF16)) + r[h:h + 1, keys]
                _flash_update(s, vc_ref[h, :, keys].astype(BF16), m_sc, l_sc, acc_sc, h,
                              row_bias=cq_sc[h], v_seq_minor=True)

    @pl.when(j == n_cache_tiles)
    def _():
        cn = cum_new() * LOG2E
        causal = (lax.broadcasted_iota(jnp.int32, (t, LANES), 1)
                  <= lax.broadcasted_iota(jnp.int32, (t, LANES), 0))
        pad = jnp.zeros((LANES - t, HEAD_DIM), BF16)
        for h in range(N_HEADS):
            cols = slice(h * HEAD_DIM, (h + 1) * HEAD_DIM)
            k = jnp.concatenate([kn_ref[h // 2, :, (h % 2) * HEAD_DIM:(h % 2 + 1) * HEAD_DIM], pad], axis=0)
            v = jnp.concatenate([vn_ref[:, cols], pad], axis=0)
            s = _dot_nt(q_head(h), k) + cq_sc[h] - cn[h:h + 1, :]
            _flash_update(jnp.where(causal, s, NEG), v, m_sc, l_sc, acc_sc, h)
            o_ref[:, cols] = (acc_sc[h] / jnp.sum(l_sc[h], axis=-1, keepdims=True)).astype(BF16)


def _fox_sample(qcat, kcat, vb, cache_kt, cache_vt, lf_cache_t, lf_new_t):
    n, _, _, p_rows = cache_kt.shape
    t = vb.shape[0] // n
    tk = 2048
    nct = p_rows // tk
    u = jnp.asarray(np.pad(np.triu(np.ones((t, t))), ((0, 0), (0, LANES - t))), BF16)
    rev = lambda j: jnp.maximum(nct - 1 - j, 0)
    cache = pl.BlockSpec((None, N_HEADS, HEAD_DIM, tk), lambda b, j: (b, 0, 0, rev(j)))
    new_cat = pl.BlockSpec((None, N_PAIRS, t, 2 * LANES), lambda b, j: (0, 0, b, 0))
    return pl.pallas_call(
        functools.partial(_fox_sample_kernel, n_cache_tiles=nct, sub_keys=tk),
        grid=(n, nct + 1),
        in_specs=[
            new_cat, cache, cache, new_cat,
            pl.BlockSpec((t, WIDTH), lambda b, j: (b, 0)),
            pl.BlockSpec((None, N_HEADS, tk), lambda b, j: (b, 0, rev(j))),
            pl.BlockSpec((None, N_HEADS, t), lambda b, j: (b, 0, 0)),
            _const_spec(u.shape),
        ],
        out_specs=pl.BlockSpec((t, WIDTH), lambda b, j: (b, 0)),
        out_shape=jax.ShapeDtypeStruct(vb.shape, BF16),
        scratch_shapes=[pltpu.VMEM((N_HEADS, t, LANES), F32), pltpu.VMEM((N_HEADS, LANES), F32),
                        pltpu.VMEM((N_HEADS, t, LANES), F32), pltpu.VMEM((N_HEADS, t, LANES), F32),
                        pltpu.VMEM((N_HEADS, t, HEAD_DIM), F32)],
        compiler_params=_params(2),
        name="fox_sample",
    )(qcat, cache_kt, cache_vt, kcat, vb, lf_cache_t, lf_new_t, u)


GROUP_LANE = 64
ROW_ALIGN = 16
MOE_TILE = 512
SORT_ROWS = 640


def _route(r):
    lane_i = lax.broadcasted_iota(jnp.int32, r.shape, 1)
    lane = lane_i.astype(F32)
    lane_grp = (lane_i // EXPERTS_PER_GROUP).astype(F32)
    big = float(LANES)
    is_coarse = (lane_i >= N_EXPERTS) & (lane_i < N_EXPERTS + N_GROUPS)
    cm = jnp.where(is_coarse, r, NEG)
    cmax = cm.max(axis=-1, keepdims=True)
    grp = jnp.min(jnp.where(cm == cmax, lane - N_EXPERTS, big), axis=-1, keepdims=True)
    pg_sel = 1.0 / jnp.sum(jnp.exp(cm - cmax), axis=-1, keepdims=True)
    in_grp = (lane_i < N_EXPERTS) & (lane_grp == grp)
    fm = jnp.where(in_grp, r, NEG)
    m1 = fm.max(axis=-1, keepdims=True)
    denom = jnp.sum(jnp.exp(fm - m1), axis=-1, keepdims=True)
    i1 = jnp.min(jnp.where(fm == m1, lane, big), axis=-1, keepdims=True)
    fm2 = jnp.where(lane == i1, NEG, fm)
    m2 = fm2.max(axis=-1, keepdims=True)
    i2 = jnp.min(jnp.where(fm2 == m2, lane, big), axis=-1, keepdims=True)
    p1 = 1.0 / denom
    p2 = jnp.exp(m2 - m1) / denom
    tot = p1 + p2
    comb = (jnp.where(lane == i1, pg_sel * (p1 / tot), 0.0)
            + jnp.where(lane == i2, pg_sel * (p2 / tot), 0.0))
    return comb, grp


def _merge_kernel(x_ref, oa_ref, ob_ref, g_ref, wg_ref, wpa_ref, wpb_ref, wo_ref, gf_ref, wr_ref, br_ref,
                  y_ref, hx_ref, route_ref, grp_t_ref, cnt_ref):
    x = x_ref[...]
    h = _rms(x, g_ref[...]).astype(BF16)
    gate = jax.nn.sigmoid(_dot(h, wg_ref[...]))
    mix = (gate[:, :D_MODEL] * _dot(oa_ref[...], wpa_ref[...])
           + gate[:, D_MODEL:] * _dot(ob_ref[...], wpb_ref[...]))
    y = x + _dot(mix.astype(BF16), wo_ref[...])
    y_ref[...] = y

    hx = _rms(y, gf_ref[...])
    hx_ref[...] = hx.astype(BF16)
    h_hi, h_mid, _ = _split3(hx)
    w_hi, w_mid, _ = _split3(wr_ref[...])
    r = _dot(h_hi, w_hi) + _dot(h_hi, w_mid) + _dot(h_mid, w_hi) + br_ref[...]
    comb, grp = _route(r)
    lane = lax.broadcasted_iota(jnp.int32, comb.shape, 1)
    route = jnp.where(lane == GROUP_LANE, grp, comb)
    route_ref[...] = route
    grp_t_ref[...] = route.T[GROUP_LANE:GROUP_LANE + 8, :]
    cnt = jnp.sum(jnp.where(lane.astype(F32) == grp, 1.0, 0.0), axis=0, keepdims=True)
    cnt_ref[...] = jnp.broadcast_to(cnt, cnt_ref.shape)


def _merge(x, oa, ob, w):
    t_total = x.shape[0]
    tm = MOE_TILE
    n_tiles = t_total // tm
    row = lambda n: pl.BlockSpec((tm, n), lambda i: (i, 0))
    consts = [w["g_mix"], w["w_gate"], w["w_pa"], w["w_pb"], w["w_o"], w["g_ffn"], w["w_router"],
              w["b_router"]]
    return pl.pallas_call(
        _merge_kernel,
        grid=(n_tiles,),
        in_specs=[row(D_MODEL), row(WIDTH), row(WIDTH)] + [_const_spec(c.shape) for c in consts],
        out_specs=[row(D_MODEL), row(D_MODEL), row(LANES), pl.BlockSpec((8, tm), lambda i: (0, i)),
                   pl.BlockSpec((None, 8, LANES), lambda i: (i, 0, 0))],
        out_shape=[jax.ShapeDtypeStruct(x.shape, F32), jax.ShapeDtypeStruct(x.shape, BF16),
                   jax.ShapeDtypeStruct((t_total, LANES), F32), jax.ShapeDtypeStruct((8, t_total), F32),
                   jax.ShapeDtypeStruct((n_tiles, 8, LANES), F32)],
        compiler_params=_params(1),
        name="merge",
    )(x, oa, ob, *consts)


def _moe_plan(cnt, t_total):
    n_tiles = cnt.shape[0]
    excl = lambda a, axis: jnp.cumsum(a, axis=axis) - a
    n_al = (cnt + ROW_ALIGN - 1) // ROW_ALIGN * ROW_ALIGN
    in_base = excl(n_al, 1)
    tot = jnp.sum(n_al, axis=0)
    n_ffn = (tot + MOE_TILE - 1) // MOE_TILE
    g_base = excl((n_ffn + 1) * MOE_TILE, 0)
    dst = g_base[None, :] + excl(n_al, 0)
    n_steps = t_total // MOE_TILE + N_GROUPS + (N_GROUPS * (ROW_ALIGN - 1) * n_tiles + MOE_TILE - 1) // MOE_TILE
    ends = jnp.cumsum(n_ffn)
    k = jnp.minimum(jnp.arange(n_steps), ends[-1] - 1)
    g_of_k = jnp.sum(k[:, None] >= ends[None, :], axis=1)
    blk = g_base[g_of_k] // MOE_TILE + (k - (ends - n_ffn)[g_of_k])
    cap_rows = (n_steps + N_GROUPS) * MOE_TILE
    i32 = lambda a: a.astype(jnp.int32)
    return (i32(in_base).reshape(-1), i32(dst).reshape(-1), i32(blk), i32(g_of_k), i32(ends[-1:]),
            n_steps, cap_rows)


def _seg_copies(src_ref, dst_ref, src_rows, dst_rows, sems):
    return [pltpu.make_async_copy(src_ref(g, src_rows[g]), dst_ref(g, dst_rows[g]), sems.at[g])
            for g in range(N_GROUPS)]


def _sort_kernel(ib_ref, dst_ref, hx_ref, route_ref, grp_t_ref, tri_ref, xz_ref, rz_ref,
                 xs_ref, rs_ref, xsort, rsort, sems):
    del xz_ref, rz_ref
    i = pl.program_id(0)
    tm = hx_ref.shape[0]

    @pl.when(i == 0)
    def _():
        xsort[SORT_ROWS:, :] = jnp.zeros((xsort.shape[0] - SORT_ROWS, D_MODEL), BF16)
        rsort[SORT_ROWS:, :] = jnp.zeros((rsort.shape[0] - SORT_ROWS, LANES), F32)

    g_row = grp_t_ref[0:1, :]
    sub = lax.broadcasted_iota(jnp.int32, (8, tm), 0).astype(F32)
    mine = sub == g_row
    before = _dot(jnp.where(mine, 1.0, 0.0).astype(BF16), tri_ref[...])
    dest = jnp.sum(jnp.where(mine, before, 0.0), axis=0, keepdims=True)
    for g in range(N_GROUPS):
        base = jnp.full(dest.shape, ib_ref[i * N_GROUPS + g], jnp.int32).astype(F32)
        dest = dest + jnp.where(g_row == g, base, 0.0)
    rows = lax.broadcasted_iota(jnp.int32, (SORT_ROWS, tm), 0).astype(F32)
    perm = jnp.where(rows == dest, 1.0, 0.0).astype(BF16)
    xsort[0:SORT_ROWS, :] = _dot(perm, hx_ref[...]).astype(BF16)
    rsort[0:SORT_ROWS, :] = _dot3(perm, route_ref[...])

    src = [pl.multiple_of(ib_ref[i * N_GROUPS + g], ROW_ALIGN) for g in range(N_GROUPS)]
    dst = [pl.multiple_of(dst_ref[i * N_GROUPS + g], ROW_ALIGN) for g in range(N_GROUPS)]
    copies = (_seg_copies(lambda g, r: xsort.at[pl.ds(r, tm)], lambda g, r: xs_ref.at[pl.ds(r, tm)],
                          src, dst, sems.at[0])
              + _seg_copies(lambda g, r: rsort.at[pl.ds(r, tm)], lambda g, r: rs_ref.at[pl.ds(r, tm)],
                            src, dst, sems.at[1]))
    for c in copies:
        c.start()
    for c in copies:
        c.wait()


def _expert_kernel(blk_ref, grp_ref, n_ref, xs_ref, rs_ref, w1_ref, w3_ref, w2_ref, yz_ref, ys_ref):
    del blk_ref, yz_ref
    k = pl.program_id(0)

    @pl.when(k < n_ref[0])
    def _():
        x = xs_ref[...]
        hid = jax.nn.silu(_dot(x, w1_ref[...])) * _dot(x, w3_ref[...])
        comb = rs_ref[...]
        lane = lax.broadcasted_iota(jnp.int32, comb.shape, 1)
        first = grp_ref[k] * EXPERTS_PER_GROUP
        scale = []
        for e in range(EXPERTS_PER_GROUP):
            w_e = jnp.sum(jnp.where(lane == first + e, comb, 0.0), axis=-1, keepdims=True)
            scale.append(jnp.broadcast_to(w_e, (comb.shape[0], D_EXPERT)))
        hid = hid * jnp.concatenate(scale, axis=-1)
        ys_ref[...] = _dot(hid.astype(BF16), w2_ref[...]).astype(BF16)


def _unsort_kernel(ib_ref, dst_ref, y_ref, route_ref, tri_ref, ys_ref, o_ref, ybuf, yasm, sems):
    i = pl.program_id(0)
    tm = y_ref.shape[0]

    @pl.when(i == 0)
    def _():
        yasm[...] = jnp.zeros_like(yasm)

    src = [pl.multiple_of(dst_ref[i * N_GROUPS + g], ROW_ALIGN) for g in range(N_GROUPS)]
    copies = _seg_copies(lambda g, r: ys_ref.at[pl.ds(r, tm)], lambda g, r: ybuf.at[g],
                         src, [0] * N_GROUPS, sems)
    for c in copies:
        c.start()

    route = route_ref[...]
    lane = lax.broadcasted_iota(jnp.int32, route.shape, 1).astype(F32)
    grp = jnp.sum(jnp.where(lane == GROUP_LANE, route, 0.0), axis=-1, keepdims=True)
    mine = lane == grp
    before = _dot(tri_ref[...], jnp.where(mine, 1.0, 0.0).astype(BF16))
    dest = jnp.sum(jnp.where(mine, before, 0.0), axis=-1, keepdims=True)
    for g in range(N_GROUPS):
        base = jnp.full(dest.shape, ib_ref[i * N_GROUPS + g], jnp.int32).astype(F32)
        dest = dest + jnp.where(grp == g, base, 0.0)
    cols = lax.broadcasted_iota(jnp.int32, (tm, SORT_ROWS), 1).astype(F32)
    perm_t = jnp.where(cols == dest, 1.0, 0.0).astype(BF16)

    for c in copies:
        c.wait()
    for g in range(N_GROUPS):
        yasm[pl.ds(pl.multiple_of(ib_ref[i * N_GROUPS + g], ROW_ALIGN), tm), :] = ybuf[g]
    o_ref[...] = y_ref[...] + _dot(perm_t, yasm[0:SORT_ROWS, :])


def _moe(y, hx, route, grp_t, cnt, w):
    t_total = y.shape[0]
    tm = MOE_TILE
    n_tiles = t_total // tm
    assert SORT_ROWS >= tm + N_GROUPS * (ROW_ALIGN - 1)
    in_base, dst, blk, grp_of_step, n_used, n_steps, cap_rows = _moe_plan(
        cnt[:, 0, :N_GROUPS].astype(jnp.int32), t_total)
    lower = jnp.asarray(np.tril(np.ones((tm, tm)), -1), BF16)
    upper = jnp.asarray(np.triu(np.ones((tm, tm)), 1), BF16)
    stage_rows = SORT_ROWS + tm
    any_spec = pl.BlockSpec(memory_space=pl.ANY)

    xs, rs = pl.pallas_call(
        _sort_kernel,
        grid_spec=pltpu.PrefetchScalarGridSpec(
            num_scalar_prefetch=2,
            grid=(n_tiles,),
            in_specs=[pl.BlockSpec((tm, D_MODEL), lambda i, *_: (i, 0)),
                      pl.BlockSpec((tm, LANES), lambda i, *_: (i, 0)),
                      pl.BlockSpec((8, tm), lambda i, *_: (0, i)),
                      pl.BlockSpec((tm, tm), lambda i, *_: (0, 0)),
                      any_spec, any_spec],
            out_specs=[any_spec, any_spec],
            scratch_shapes=[pltpu.VMEM((stage_rows, D_MODEL), BF16), pltpu.VMEM((stage_rows, LANES), F32),
                            pltpu.SemaphoreType.DMA((2, N_GROUPS))],
        ),
        out_shape=[jax.ShapeDtypeStruct((cap_rows, D_MODEL), BF16),
                   jax.ShapeDtypeStruct((cap_rows, LANES), F32)],
        input_output_aliases={6: 0, 7: 1},
        compiler_params=_params(1),
        name="moe_sort",
    )(in_base, dst, hx, route, grp_t, upper,
      jnp.zeros((cap_rows, D_MODEL), BF16), jnp.zeros((cap_rows, LANES), F32))

    wspec = pl.BlockSpec((None, D_MODEL, EXPERTS_PER_GROUP * D_EXPERT), lambda k, blk, grp, n: (grp[k], 0, 0))
    ys = pl.pallas_call(
        _expert_kernel,
        grid_spec=pltpu.PrefetchScalarGridSpec(
            num_scalar_prefetch=3,
            grid=(n_steps,),
            in_specs=[pl.BlockSpec((tm, D_MODEL), lambda k, blk, grp, n: (blk[k], 0)),
                      pl.BlockSpec((tm, LANES), lambda k, blk, grp, n: (blk[k], 0)),
                      wspec, wspec,
                      pl.BlockSpec((None, EXPERTS_PER_GROUP * D_EXPERT, D_MODEL),
                                   lambda k, blk, grp, n: (grp[k], 0, 0)),
                      any_spec],
            out_specs=pl.BlockSpec((tm, D_MODEL), lambda k, blk, grp, n: (blk[k], 0)),
        ),
        out_shape=jax.ShapeDtypeStruct((cap_rows, D_MODEL), BF16),
        input_output_aliases={8: 0},
        compiler_params=_params(1),
        name="moe_experts",
    )(blk, grp_of_step, n_used, xs, rs, w["w1"], w["w3"], w["w2"], jnp.zeros((cap_rows, D_MODEL), BF16))

    return pl.pallas_call(
        _unsort_kernel,
        grid_spec=pltpu.PrefetchScalarGridSpec(
            num_scalar_prefetch=2,
            grid=(n_tiles,),
            in_specs=[pl.BlockSpec((tm, D_MODEL), lambda i, *_: (i, 0)),
                      pl.BlockSpec((tm, LANES), lambda i, *_: (i, 0)),
                      pl.BlockSpec((tm, tm), lambda i, *_: (0, 0)),
                      any_spec],
            out_specs=pl.BlockSpec((tm, D_MODEL), lambda i, *_: (i, 0)),
            scratch_shapes=[pltpu.VMEM((N_GROUPS, tm, D_MODEL), BF16), pltpu.VMEM((stage_rows, D_MODEL), BF16),
                            pltpu.SemaphoreType.DMA((N_GROUPS,))],
        ),
        out_shape=jax.ShapeDtypeStruct(y.shape, F32),
        compiler_params=_params(1),
        name="moe_unsort",
    )(in_base, dst, y, route, lower, ys)


def _group_stack(w_up):
    g = w_up.reshape(N_GROUPS, EXPERTS_PER_GROUP, D_MODEL, D_EXPERT)
    return jnp.transpose(g, (0, 2, 1, 3)).reshape(N_GROUPS, D_MODEL, EXPERTS_PER_GROUP * D_EXPERT).astype(BF16)


def _prep_weights(g_mix, w_in, b_f, q_norm_a, k_norm_a, q_norm_b, k_norm_b, w_pa, w_pb, w_o,
                  g_ffn, w_rg, b_rg, w_re, b_re, w1, w3, w2):
    n_qkv = 6 * WIDTH
    tile = lambda g: jnp.tile(g, N_HEADS)
    w_router = jnp.concatenate(
        [jnp.transpose(w_re, (1, 0, 2)).reshape(D_MODEL, N_EXPERTS), w_rg,
         jnp.zeros((D_MODEL, LANES - N_EXPERTS - N_GROUPS), F32)], axis=1)
    b_router = jnp.concatenate(
        [b_re.reshape(N_EXPERTS), b_rg, jnp.zeros((LANES - N_EXPERTS - N_GROUPS,), F32)])[None, :]
    return {
        "g_mix": g_mix[None, :],
        "w_qkv": w_in[:, :n_qkv].astype(BF16),
        "w_f": jnp.pad(w_in[:, n_qkv:n_qkv + N_HEADS], ((0, 0), (0, LANES - N_HEADS))).astype(BF16),
        "b_f": jnp.pad(b_f, (0, LANES - N_HEADS))[None, :],
        "w_gate": w_in[:, n_qkv + N_HEADS:].astype(BF16),
        "gains": jnp.stack([tile(q_norm_a), tile(k_norm_a), tile(q_norm_b), tile(k_norm_b)]),
        "w_pa": w_pa.astype(BF16), "w_pb": w_pb.astype(BF16), "w_o": w_o.astype(BF16),
        "g_ffn": g_ffn[None, :],
        "w_router": w_router, "b_router": b_router,
        "w1": _group_stack(w1), "w3": _group_stack(w3),
        "w2": w2.reshape(N_GROUPS, EXPERTS_PER_GROUP * D_EXPERT, D_MODEL).astype(BF16),
    }


def kernel(x_prompt, x_sample, cache_a_k, cache_a_v, cache_b_k, cache_b_v, cache_b_logf, g_mix, w_in, b_f, q_norm_a, k_norm_a, q_norm_b, k_norm_b, rel_bias, w_pa, w_pb, w_o, g_ffn, w_rg, b_rg, w_re, b_re, w1, w3, w2):
    assert g_mix.shape[0] == 1, "single-layer step"
    n_p, seq, _ = x_prompt.shape
    n_s, t_s, _ = x_sample.shape
    a_rows = cache_a_k.shape[2]
    w = _prep_weights(g_mix[0], w_in[0], b_f[0], q_norm_a[0], k_norm_a[0], q_norm_b[0], k_norm_b[0],
                      w_pa[0], w_pb[0], w_o[0], g_ffn[0], w_rg[0], b_rg[0], w_re[0], b_re[0],
                      w1[0], w3[0], w2[0])
    band_tq = 256
    bias_prompt = _band_bias(rel_bias[0], band_tq, 3 * band_tq, 2 * band_tq, band_mask=True)
    bias_sample = jnp.pad(_band_bias(rel_bias[0], t_s, a_rows + t_s, a_rows, band_mask=False),
                          ((0, 0), (0, 0), (0, LANES - t_s)), constant_values=NEG)

    seq_minor = lambda a: jnp.transpose(a, (0, 2, 3, 1))
    seq_major = lambda a: jnp.transpose(a, (0, 3, 1, 2))[None]

    xp = x_prompt.reshape(n_p * seq, D_MODEL)
    (qa, ka, va, ka_t, va_t, qcat, kcat, vb, kb_t, vb_t, logf_t) = _inproj(xp, n_p, w, seq_minor=True)
    o_a = _band_prompt(qa, ka, va, bias_prompt, n_p)
    o_b = _fox_prompt(qcat, kcat, vb, n_p)
    y_p = _moe(*_merge(xp, o_a, o_b, w), w)

    xs = x_sample.reshape(n_s * t_s, D_MODEL)
    (qa_s, _, _, ka_fs, va_fs, qcat_s, kcat_s, vb_s, kb_fs, vb_fs, logf_s) = _inproj(
        xs, 1, w, seq_minor=False)
    o_as, new_ak, new_av = _band_sample(
        qa_s, ka_fs, va_fs, cache_a_k[0].reshape(n_s, a_rows, WIDTH),
        cache_a_v[0].reshape(n_s, a_rows, WIDTH), bias_sample)
    lf_cache_t = jnp.transpose(cache_b_logf[0], (0, 2, 1))
    lf_new_t = jnp.transpose(logf_s.reshape(n_s, t_s, N_HEADS), (0, 2, 1))
    o_bs = _fox_sample(qcat_s, kcat_s, vb_s, seq_minor(cache_b_k[0]), seq_minor(cache_b_v[0]),
                       lf_cache_t, lf_new_t)
    y_s = _moe(*_merge(xs, o_as, o_bs, w), w)

    heads = lambda a, n, r: a.reshape(1, n, r, N_HEADS, HEAD_DIM)
    return (y_p.reshape(n_p, seq, D_MODEL), y_s.reshape(n_s, t_s, D_MODEL),
            seq_major(ka_t), seq_major(va_t), seq_major(kb_t), seq_major(vb_t),
            jnp.transpose(logf_t, (0, 2, 1))[None],
            heads(new_ak, n_s, a_rows), heads(new_av, n_s, a_rows),
            heads(kb_fs, n_s, t_s), heads(vb_fs, n_s, t_s), logf_s.reshape(1, n_s, t_s, N_HEADS))
```

```python
import functools

import numpy as np
import jax
import jax.numpy as jnp
from jax import lax
from jax.experimental import pallas as pl
from jax.experimental.pallas import tpu as pltpu

F32 = jnp.float32
BF16 = jnp.bfloat16

D_MODEL = 1024
HEAD_DIM = 64
N_HEADS = 8
WIDTH = N_HEADS * HEAD_DIM
N_PAIRS = N_HEADS // 2
CHUNK = 64
LEFT_CHUNKS = 8
WINDOW_ROWS = LEFT_CHUNKS * CHUNK
MAX_REL = 256
N_GROUPS = 4
EXPERTS_PER_GROUP = 8
N_EXPERTS = N_GROUPS * EXPERTS_PER_GROUP
D_EXPERT = 128
EPS = 1e-6
NEG = -1e30
LOG2E = 1.4426950408889634
LANES = 128
AUG_STRIDE = 8
VMEM_LIMIT = 56 * 1024 * 1024

_NT = (((1,), (1,)), ((), ()))


def _dot(a, b):
    return jnp.dot(a, b, preferred_element_type=F32)


def _dot_nt(a, b):
    return lax.dot_general(a, b, _NT, preferred_element_type=F32)


def _split3(x):
    hi = x.astype(BF16)
    r = x - hi.astype(F32)
    mid = r.astype(BF16)
    lo = (r - mid.astype(F32)).astype(BF16)
    return hi, mid, lo


def _dot3(a_bf, x):
    hi, mid, lo = _split3(x)
    return _dot(a_bf, hi) + _dot(a_bf, mid) + _dot(a_bf, lo)


def _rms(x, g):
    ms = jnp.mean(x * x, axis=-1, keepdims=True)
    return x * lax.rsqrt(ms + EPS) * g


def _params(n_axes):
    return pltpu.CompilerParams(dimension_semantics=("arbitrary",) * n_axes,
                                vmem_limit_bytes=VMEM_LIMIT)


def _const_spec(shape):
    nd = len(shape)
    return pl.BlockSpec(shape, lambda *_: (0,) * nd)


def _inproj_kernel(x_ref, g_ref, wqkv_ref, wf_ref, bf_ref, gains_ref, bd_ref, ltri_ref,
                   selq_ref, selk_ref, oneq_ref, onek_ref,
                   qa_ref, ka_ref, va_ref, kaf_ref, vaf_ref, qcat_ref, kcat_ref, vb_ref,
                   kbf_ref, vbf_ref, logf_ref, carry_ref, *, tiles_per_seq, seq_minor):
    i = pl.program_id(0)
    tm = x_ref.shape[0]

    @pl.when(i % tiles_per_seq == 0)
    def _():
        carry_ref[...] = jnp.zeros_like(carry_ref)

    def put_f32(ref, y, last_tile_only=False):
        if not seq_minor:
            ref[...] = y
        elif last_tile_only:
            @pl.when(i % tiles_per_seq == tiles_per_seq - 1)
            def _():
                ref[...] = y.T.reshape(N_HEADS, HEAD_DIM, tm)
        else:
            ref[...] = y.T.reshape(N_HEADS, HEAD_DIM, tm)

    h = _rms(x_ref[...], g_ref[...]).astype(BF16)

    def seg(s):
        return _dot(h, wqkv_ref[:, s * WIDTH:(s + 1) * WIDTH])

    def headnorm(y, n):
        ss = _dot((y * y).astype(BF16), bd_ref[...])
        return y * lax.rsqrt(ss * (1.0 / HEAD_DIM) + EPS) * gains_ref[n:n + 1, :]

    q_a = headnorm(seg(0), 0)
    qa_ref[...] = (q_a * (HEAD_DIM ** -0.5)).astype(BF16)
    k_a = headnorm(seg(1), 1)
    ka_ref[...] = k_a.astype(BF16)
    put_f32(kaf_ref, k_a, last_tile_only=True)
    v_a = seg(2)
    va_ref[...] = v_a.astype(BF16)
    put_f32(vaf_ref, v_a, last_tile_only=True)

    z = _dot(h, wf_ref[...]) + bf_ref[...]
    logf = jnp.minimum(z, 0.0) - jnp.log(1.0 + jnp.exp(-jnp.abs(z)))
    if seq_minor:
        logf_ref[...] = logf.T[0:N_HEADS, :]
    else:
        logf_ref[...] = logf[:, :N_HEADS]
    c = _dot3(ltri_ref[...], logf) + carry_ref[0:1, :]
    carry_ref[...] = jnp.broadcast_to(c[-1:, :], carry_ref.shape)
    cs = jnp.concatenate(_split3(c * LOG2E), axis=-1)
    q_aug = (_dot(cs, selq_ref[...]) + oneq_ref[...]).astype(BF16)
    k_aug = (_dot(cs, selk_ref[...]) + onek_ref[...]).astype(BF16)

    q_b = (headnorm(seg(3), 2) * (HEAD_DIM ** -0.5 * LOG2E)).astype(BF16)
    k_b = headnorm(seg(4), 3)
    put_f32(kbf_ref, k_b)
    k_b = k_b.astype(BF16)
    v_b = seg(5)
    put_f32(vbf_ref, v_b)
    vb_ref[...] = v_b.astype(BF16)
    for p in range(N_PAIRS):
        cols = slice(p * LANES, (p + 1) * LANES)
        qcat_ref[p, :, 0:LANES] = q_b[:, cols]
        qcat_ref[p, :, LANES:2 * LANES] = q_aug
        kcat_ref[p, :, 0:LANES] = k_b[:, cols]
        kcat_ref[p, :, LANES:2 * LANES] = k_aug


def _aug_constants():
    selq = np.zeros((3 * LANES, LANES), np.float32)
    selk = np.zeros((3 * LANES, LANES), np.float32)
    oneq = np.zeros((1, LANES), np.float32)
    onek = np.zeros((1, LANES), np.float32)
    for h in range(N_HEADS):
        for k in range(3):
            selq[k * LANES + h, AUG_STRIDE * h + k] = 1.0
            selk[k * LANES + h, AUG_STRIDE * h + 3 + k] = -1.0
            oneq[0, AUG_STRIDE * h + 3 + k] = 1.0
            onek[0, AUG_STRIDE * h + k] = 1.0
    return (jnp.asarray(selq, BF16), jnp.asarray(selk, BF16), jnp.asarray(oneq), jnp.asarray(onek))


def _inproj(x, n_seq, w, seq_minor):
    t_total = x.shape[0]
    tm = 512
    n_tiles = t_total // tm
    seq = t_total // n_seq
    tps = seq // tm
    row = lambda i: (i, 0)
    flat = lambda dt: (jax.ShapeDtypeStruct((t_total, WIDTH), dt), pl.BlockSpec((tm, WIDTH), row))
    if seq_minor:
        assert tm == WINDOW_ROWS
        band_f32 = (jax.ShapeDtypeStruct((n_seq, N_HEADS, HEAD_DIM, WINDOW_ROWS), F32),
                    pl.BlockSpec((None, N_HEADS, HEAD_DIM, tm), lambda i: (i // tps, 0, 0, 0)))
        fox_f32 = (jax.ShapeDtypeStruct((n_seq, N_HEADS, HEAD_DIM, seq), F32),
                   pl.BlockSpec((None, N_HEADS, HEAD_DIM, tm), lambda i: (i // tps, 0, 0, i % tps)))
        logf = (jax.ShapeDtypeStruct((n_seq, N_HEADS, seq), F32),
                pl.BlockSpec((None, N_HEADS, tm), lambda i: (i // tps, 0, i % tps)))
    else:
        band_f32 = fox_f32 = flat(F32)
        logf = (jax.ShapeDtypeStruct((t_total, N_HEADS), F32), pl.BlockSpec((tm, N_HEADS), row))
    cat = (jax.ShapeDtypeStruct((n_seq, N_PAIRS, seq, 2 * LANES), BF16),
           pl.BlockSpec((None, N_PAIRS, tm, 2 * LANES), lambda i: (i // tps, 0, i % tps, 0)))
    outs = [flat(BF16), flat(BF16), flat(BF16), band_f32, band_f32, cat, cat, flat(BF16),
            fox_f32, fox_f32, logf]
    bd = jnp.asarray(np.kron(np.eye(N_HEADS), np.ones((HEAD_DIM, HEAD_DIM))), BF16)
    ltri = jnp.asarray(np.tril(np.ones((tm, tm))), BF16)
    consts = [w["g_mix"], w["w_qkv"], w["w_f"], w["b_f"], w["gains"], bd, ltri, *_aug_constants()]
    in_specs = [pl.BlockSpec((tm, D_MODEL), row)] + [_const_spec(c.shape) for c in consts]
    return pl.pallas_call(
        functools.partial(_inproj_kernel, tiles_per_seq=tps, seq_minor=seq_minor),
        grid=(n_tiles,),
        in_specs=in_specs,
        out_specs=[o[1] for o in outs],
        out_shape=[o[0] for o in outs],
        scratch_shapes=[pltpu.VMEM((8, LANES), F32)],
        compiler_params=_params(1),
        name="inproj",
    )(x, *consts)


def _softmax_pv(scores, values):
    chunks = [[s[:, c * LANES:(c + 1) * LANES] for c in range(s.shape[1] // LANES)] for s in scores]
    cmax = None
    for cs in chunks:
        for c in cs:
            cmax = c if cmax is None else jnp.maximum(cmax, c)
    m = jnp.max(cmax, axis=-1, keepdims=True)
    pv = None
    lsum = None
    for cs, v in zip(chunks, values):
        ps = [jnp.exp(c - m) for c in cs]
        for p in ps:
            lsum = p if lsum is None else lsum + p
        o = _dot(jnp.concatenate(ps, axis=-1).astype(BF16), v)
        pv = o if pv is None else pv + o
    return pv, jnp.sum(lsum, axis=-1, keepdims=True)


def _head_mask(hh):
    lane = lax.broadcasted_iota(jnp.int32, (1, LANES), 1)
    return (lane // HEAD_DIM) == hh


def _band_prompt_kernel(q_ref, k0_ref, k1_ref, k2_ref, v0_ref, v1_ref, v2_ref, bias_ref, o_ref, *, tq):
    i = pl.program_id(1)
    k_refs = (k0_ref, k1_ref, k2_ref)
    v_refs = (v0_ref, v1_ref, v2_ref)
    pad = [jnp.where(i - 2 + j >= 0, 0.0, NEG) for j in range(2)] + [0.0]
    for p in range(N_PAIRS):
        cols = slice(p * LANES, (p + 1) * LANES)
        q = q_ref[:, cols]
        ks = [r[:, cols] for r in k_refs]
        vs = [r[:, cols] for r in v_refs]
        out = jnp.zeros((tq, LANES), F32)
        for hh in range(2):
            msk = _head_mask(hh)
            qh = jnp.where(msk, q, jnp.zeros_like(q))
            scores = [_dot_nt(qh, ks[j]) + bias_ref[2 * p + hh, :, j * tq:(j + 1) * tq] + pad[j]
                      for j in range(3)]
            pv, l = _softmax_pv(scores, vs)
            out = jnp.where(msk, pv / l, out)
        o_ref[:, cols] = out.astype(BF16)


def _band_prompt(qa, ka, va, bias, n_seq):
    t_total = qa.shape[0]
    tq = 256
    seq = t_total // n_seq
    nq = seq // tq
    qmap = lambda b, i: (b * nq + i, 0)

    def kmap(j):
        return lambda b, i: (b * nq + jnp.maximum(i - 2 + j, 0), 0)

    blk = lambda m: pl.BlockSpec((tq, WIDTH), m)
    return pl.pallas_call(
        functools.partial(_band_prompt_kernel, tq=tq),
        grid=(n_seq, nq),
        in_specs=[blk(qmap)] + [blk(kmap(j)) for j in range(3)] + [blk(kmap(j)) for j in range(3)]
        + [_const_spec(bias.shape)],
        out_specs=blk(qmap),
        out_shape=jax.ShapeDtypeStruct((t_total, WIDTH), BF16),
        compiler_params=_params(2),
        name="band_prompt",
    )(qa, ka, ka, ka, va, va, va, bias)


def _band_sample_kernel(q_ref, kc_ref, vc_ref, kn_ref, vn_ref, bias_ref, o_ref, ko_ref, vo_ref):
    p_rows = kc_ref.shape[0]
    t = q_ref.shape[0]
    for p in range(N_PAIRS):
        cols = slice(p * LANES, (p + 1) * LANES)
        q = q_ref[:, cols]
        pad = jnp.zeros((LANES - t, LANES), BF16)
        ks = [kc_ref[:, cols].astype(BF16), jnp.concatenate([kn_ref[:, cols].astype(BF16), pad], axis=0)]
        vs = [vc_ref[:, cols].astype(BF16), jnp.concatenate([vn_ref[:, cols].astype(BF16), pad], axis=0)]
        out = jnp.zeros((t, LANES), F32)
        for hh in range(2):
            msk = _head_mask(hh)
            qh = jnp.where(msk, q, jnp.zeros_like(q))
            scores = [_dot_nt(qh, ks[0]) + bias_ref[2 * p + hh, :, 0:p_rows],
                      _dot_nt(qh, ks[1]) + bias_ref[2 * p + hh, :, p_rows:p_rows + LANES]]
            pv, l = _softmax_pv(scores, vs)
            out = jnp.where(msk, pv / l, out)
        o_ref[:, cols] = out.astype(BF16)
    ko_ref[0:p_rows - t, :] = kc_ref[t:p_rows, :]
    ko_ref[p_rows - t:p_rows, :] = kn_ref[...]
    vo_ref[0:p_rows - t, :] = vc_ref[t:p_rows, :]
    vo_ref[p_rows - t:p_rows, :] = vn_ref[...]


def _band_sample(qa, ka_f, va_f, cache_k, cache_v, bias):
    n, p_rows, _ = cache_k.shape
    t = qa.shape[0] // n
    row = pl.BlockSpec((t, WIDTH), lambda b: (b, 0))
    cache = pl.BlockSpec((None, p_rows, WIDTH), lambda b: (b, 0, 0))
    return pl.pallas_call(
        _band_sample_kernel,
        grid=(n,),
        in_specs=[row, cache, cache, row, row, _const_spec(bias.shape)],
        out_specs=[row, cache, cache],
        out_shape=[jax.ShapeDtypeStruct(qa.shape, BF16),
                   jax.ShapeDtypeStruct(cache_k.shape, F32),
                   jax.ShapeDtypeStruct(cache_v.shape, F32)],
        compiler_params=_params(1),
        name="band_sample",
    )(qa, cache_k, cache_v, ka_f, va_f, bias)


def _band_bias(rel_bias, n_q, n_k, offset, band_mask):
    qi = np.arange(n_q)[:, None]
    kj = np.arange(n_k)[None, :]
    n_u = n_q + n_k - 1
    w_of_m = np.where(np.arange(n_u) < n_k, np.arange(n_u), np.arange(n_u) - n_u)
    idx_m = np.clip(offset - w_of_m, -MAX_REL, MAX_REL) + MAX_REL
    e = jnp.take(rel_bias.astype(F32), jnp.asarray(idx_m), axis=1)
    n_h = rel_bias.shape[0]
    bias = jnp.tile(e, (1, n_q + 1))[:, :n_q * (n_u - 1)].reshape(n_h, n_q, n_u - 1)[:, :, :n_k]
    if band_mask:
        rel_chunk = (qi + offset) // CHUNK - kj // CHUNK
        valid = (rel_chunk >= 0) & (rel_chunk <= LEFT_CHUNKS)
        bias = jnp.where(jnp.asarray(valid)[None], bias, NEG)
    return bias


def _aug_head_mask(h_in_pair, pair):
    lane = lax.broadcasted_iota(jnp.int32, (1, 2 * LANES), 1)
    head = 2 * pair + h_in_pair
    in_q = (lane < LANES) & ((lane // HEAD_DIM) == h_in_pair)
    in_aug = (lane >= LANES) & (((lane - LANES) // AUG_STRIDE) == head)
    return in_q | in_aug


def _flash_update(s, v, m_ref, l_ref, acc_ref, h, row_bias=None, v_seq_minor=False):
    chunks = [s[:, c * LANES:(c + 1) * LANES] for c in range(s.shape[1] // LANES)]
    if row_bias is not None:
        chunks = [c + row_bias for c in chunks]
    cmax = chunks[0]
    for c in chunks[1:]:
        cmax = jnp.maximum(cmax, c)
    m_prev = m_ref[h]
    m_new = jnp.maximum(m_prev, jnp.max(cmax, axis=-1, keepdims=True))
    alpha = jnp.exp2(m_prev - m_new)
    ps = [jnp.exp2(c - m_new) for c in chunks]
    lsum = ps[0]
    for p in ps[1:]:
        lsum = lsum + p
    l_ref[h] = alpha * l_ref[h] + lsum
    p = jnp.concatenate(ps, axis=-1).astype(BF16)
    pv = _dot_nt(p, v) if v_seq_minor else _dot(p, v)
    acc_ref[h] = alpha[:, :pv.shape[1]] * acc_ref[h] + pv
    m_ref[h] = m_new


def _flash_result(l_ref, acc_ref, pair):
    outs = [acc_ref[2 * pair + hh] / jnp.sum(l_ref[2 * pair + hh], axis=-1, keepdims=True)
            for hh in range(2)]
    return jnp.where(_head_mask(0), outs[0], outs[1])


def _fox_prompt_kernel(qi_ref, kj_ref, q_ref, k_ref, v_ref, o_ref, qh_sc, m_sc, l_sc, acc_sc, *, tq, tk):
    t = pl.program_id(1)
    i = qi_ref[t]
    j = kj_ref[t]

    @pl.when(j == 0)
    def _():
        for p in range(N_PAIRS):
            q = q_ref[p]
            for hh in range(2):
                qh_sc[2 * p + hh] = jnp.where(_aug_head_mask(hh, p), q, jnp.zeros_like(q))
        m_sc[...] = jnp.full_like(m_sc, NEG)
        l_sc[...] = jnp.zeros_like(l_sc)
        acc_sc[...] = jnp.zeros_like(acc_sc)

    def step(diagonal):
        if diagonal:
            keep = (lax.broadcasted_iota(jnp.int32, (tq, tk), 1)
                    <= lax.broadcasted_iota(jnp.int32, (tq, tk), 0))
        for p in range(N_PAIRS):
            k = k_ref[p]
            v = v_ref[:, p * LANES:(p + 1) * LANES]
            for hh in range(2):
                s = _dot_nt(qh_sc[2 * p + hh], k)
                if diagonal:
                    s = jnp.where(keep, s, NEG)
                _flash_update(s, v, m_sc, l_sc, acc_sc, 2 * p + hh)

    @pl.when(j < i)
    def _():
        step(False)

    @pl.when(j == i)
    def _():
        step(True)
        for p in range(N_PAIRS):
            o_ref[:, p * LANES:(p + 1) * LANES] = _flash_result(l_sc, acc_sc, p).astype(BF16)


def _fox_prompt(qcat, kcat, vb, n_seq):
    seq = qcat.shape[2]
    tq = tk = 512
    nt = seq // tq
    qi = np.concatenate([np.full(i + 1, i) for i in range(nt)]).astype(np.int32)
    kj = np.concatenate([np.arange(i + 1) for i in range(nt)]).astype(np.int32)
    grid_spec = pltpu.PrefetchScalarGridSpec(
        num_scalar_prefetch=2,
        grid=(n_seq, len(qi)),
        in_specs=[
            pl.BlockSpec((None, N_PAIRS, tq, 2 * LANES), lambda b, t, qi, kj: (b, 0, qi[t], 0)),
            pl.BlockSpec((None, N_PAIRS, tk, 2 * LANES), lambda b, t, qi, kj: (b, 0, kj[t], 0)),
            pl.BlockSpec((tk, WIDTH), lambda b, t, qi, kj: (b * nt + kj[t], 0)),
        ],
        out_specs=pl.BlockSpec((tq, WIDTH), lambda b, t, qi, kj: (b * nt + qi[t], 0)),
        scratch_shapes=[pltpu.VMEM((N_HEADS, tq, 2 * LANES), BF16), pltpu.VMEM((N_HEADS, tq, LANES), F32),
                        pltpu.VMEM((N_HEADS, tq, LANES), F32), pltpu.VMEM((N_HEADS, tq, LANES), F32)],
    )
    return pl.pallas_call(
        functools.partial(_fox_prompt_kernel, tq=tq, tk=tk),
        grid_spec=grid_spec,
        out_shape=jax.ShapeDtypeStruct(vb.shape, BF16),
        compiler_params=_params(2),
        name="fox_prompt",
    )(jnp.asarray(qi), jnp.asarray(kj), qcat, kcat, vb)


def _suffix_sum_exclusive(x):
    n = x.shape[1]
    lane = lax.broadcasted_iota(jnp.int32, x.shape, 1)
    y = jnp.where(lane + 1 < n, pltpu.roll(x, n - 1, axis=1), 0.0)
    shift = 1
    while shift < n:
        y = y + jnp.where(lane + shift < n, pltpu.roll(y, n - shift, axis=1), 0.0)
        shift *= 2
    return y


def _fox_sample_kernel(q_ref, kc_ref, vc_ref, kn_ref, vn_ref, lfc_ref, lfn_ref, u_ref,
                       o_ref, cq_sc, car_sc, m_sc, l_sc, acc_sc, *, n_cache_tiles, sub_keys):
    j = pl.program_id(1)
    t = q_ref.shape[1]

    def cum_new():
        hi, mid, lo = _split3(lfn_ref[...])
        u = u_ref[...]
        return _dot(hi, u) + _dot(mid, u) + _dot(lo, u)

    def q_head(h):
        return q_ref[h // 2, :, (h % 2) * HEAD_DIM:(h % 2 + 1) * HEAD_DIM]

    @pl.when(j == 0)
    def _():
        cn = cum_new() * LOG2E
        eye = (lax.broadcasted_iota(jnp.int32, (t, LANES), 0)
               == lax.broadcasted_iota(jnp.int32, (t, LANES), 1))
        for h in range(N_HEADS):
            col = jnp.sum(jnp.where(eye, jnp.broadcast_to(cn[h:h + 1, :], (t, LANES)), 0.0),
                          axis=-1, keepdims=True)
            cq_sc[h] = jnp.broadcast_to(col, (t, LANES))
        car_sc[...] = jnp.zeros_like(car_sc)
        m_sc[...] = jnp.full_like(m_sc, NEG)
        l_sc[...] = jnp.zeros_like(l_sc)
        acc_sc[...] = jnp.zeros_like(acc_sc)

    @pl.when(j < n_cache_tiles)
    def _():
        lf = lfc_ref[...]
        r = _suffix_sum_exclusive(lf) + car_sc[:, 0:1]
        car_sc[...] = jnp.broadcast_to(r[:, 0:1] + lf[:, 0:1], car_sc.shape)
        r = r * LOG2E
        for sub in range(lf.shape[1] // sub_keys):
            keys = slice(sub * sub_keys, (sub + 1) * sub_keys)
            for h in range(N_HEADS):
                s = _dot(q_head(h), kc_ref[h, :, keys].astype(BF16)) + r[h:h + 1, keys]
                _flash_update(s, vc_ref[h, :, keys].astype(BF16), m_sc, l_sc, acc_sc, h,
                              row_bias=cq_sc[h], v_seq_minor=True)

    @pl.when(j == n_cache_tiles)
    def _():
        cn = cum_new() * LOG2E
        causal = (lax.broadcasted_iota(jnp.int32, (t, LANES), 1)
                  <= lax.broadcasted_iota(jnp.int32, (t, LANES), 0))
        pad = jnp.zeros((LANES - t, HEAD_DIM), BF16)
        for h in range(N_HEADS):
            cols = slice(h * HEAD_DIM, (h + 1) * HEAD_DIM)
            k = jnp.concatenate([kn_ref[h // 2, :, (h % 2) * HEAD_DIM:(h % 2 + 1) * HEAD_DIM], pad], axis=0)
            v = jnp.concatenate([vn_ref[:, cols], pad], axis=0)
            s = _dot_nt(q_head(h), k) + cq_sc[h] - cn[h:h + 1, :]
            _flash_update(jnp.where(causal, s, NEG), v, m_sc, l_sc, acc_sc, h)
            o_ref[:, cols] = (acc_sc[h] / jnp.sum(l_sc[h], axis=-1, keepdims=True)).astype(BF16)


def _fox_sample(qcat, kcat, vb, cache_kt, cache_vt, lf_cache_t, lf_new_t):
    n, _, _, p_rows = cache_kt.shape
    t = vb.shape[0] // n
    tk = 2048
    nct = p_rows // tk
    u = jnp.asarray(np.pad(np.triu(np.ones((t, t))), ((0, 0), (0, LANES - t))), BF16)
    rev = lambda j: jnp.maximum(nct - 1 - j, 0)
    cache = pl.BlockSpec((None, N_HEADS, HEAD_DIM, tk), lambda b, j: (b, 0, 0, rev(j)))
    new_cat = pl.BlockSpec((None, N_PAIRS, t, 2 * LANES), lambda b, j: (0, 0, b, 0))
    return pl.pallas_call(
        functools.partial(_fox_sample_kernel, n_cache_tiles=nct, sub_keys=tk),
        grid=(n, nct + 1),
        in_specs=[
            new_cat, cache, cache, new_cat,
            pl.BlockSpec((t, WIDTH), lambda b, j: (b, 0)),
            pl.BlockSpec((None, N_HEADS, tk), lambda b, j: (b, 0, rev(j))),
            pl.BlockSpec((None, N_HEADS, t), lambda b, j: (b, 0, 0)),
            _const_spec(u.shape),
        ],
        out_specs=pl.BlockSpec((t, WIDTH), lambda b, j: (b, 0)),
        out_shape=jax.ShapeDtypeStruct(vb.shape, BF16),
        scratch_shapes=[pltpu.VMEM((N_HEADS, t, LANES), F32), pltpu.VMEM((N_HEADS, LANES), F32),
                        pltpu.VMEM((N_HEADS, t, LANES), F32), pltpu.VMEM((N_HEADS, t, LANES), F32),
                        pltpu.VMEM((N_HEADS, t, HEAD_DIM), F32)],
        compiler_params=_params(2),
        name="fox_sample",
    )(qcat, cache_kt, cache_vt, kcat, vb, lf_cache_t, lf_new_t, u)


GROUP_LANE = 64
ROW_ALIGN = 16
MOE_TILE = 512
SORT_ROWS = 640


def _route(r):
    lane_i = lax.broadcasted_iota(jnp.int32, r.shape, 1)
    lane = lane_i.astype(F32)
    lane_grp = (lane_i // EXPERTS_PER_GROUP).astype(F32)
    big = float(LANES)
    is_coarse = (lane_i >= N_EXPERTS) & (lane_i < N_EXPERTS + N_GROUPS)
    cm = jnp.where(is_coarse, r, NEG)
    cmax = cm.max(axis=-1, keepdims=True)
    grp = jnp.min(jnp.where(cm == cmax, lane - N_EXPERTS, big), axis=-1, keepdims=True)
    pg_sel = 1.0 / jnp.sum(jnp.exp(cm - cmax), axis=-1, keepdims=True)
    in_grp = (lane_i < N_EXPERTS) & (lane_grp == grp)
    fm = jnp.where(in_grp, r, NEG)
    m1 = fm.max(axis=-1, keepdims=True)
    denom = jnp.sum(jnp.exp(fm - m1), axis=-1, keepdims=True)
    i1 = jnp.min(jnp.where(fm == m1, lane, big), axis=-1, keepdims=True)
    fm2 = jnp.where(lane == i1, NEG, fm)
    m2 = fm2.max(axis=-1, keepdims=True)
    i2 = jnp.min(jnp.where(fm2 == m2, lane, big), axis=-1, keepdims=True)
    p1 = 1.0 / denom
    p2 = jnp.exp(m2 - m1) / denom
    tot = p1 + p2
    comb = (jnp.where(lane == i1, pg_sel * (p1 / tot), 0.0)
            + jnp.where(lane == i2, pg_sel * (p2 / tot), 0.0))
    return comb, grp


def _merge_kernel(x_ref, oa_ref, ob_ref, g_ref, wg_ref, wpa_ref, wpb_ref, wo_ref, gf_ref, wrh_ref, wrm_ref,
                  br_ref, y_ref, hx_ref, route_ref, grp_t_ref, cnt_ref):
    x = x_ref[...]
    h = _rms(x, g_ref[...]).astype(BF16)
    gate = jax.nn.sigmoid(_dot(h, wg_ref[...]))
    mix = (gate[:, :D_MODEL] * _dot(oa_ref[...], wpa_ref[...])
           + gate[:, D_MODEL:] * _dot(ob_ref[...], wpb_ref[...]))
    y = x + _dot(mix.astype(BF16), wo_ref[...])
    y_ref[...] = y

    hx = _rms(y, gf_ref[...])
    hx_ref[...] = hx.astype(BF16)
    h_hi, h_mid, _ = _split3(hx)
    r = _dot(h_hi, wrh_ref[...]) + _dot(h_hi, wrm_ref[...]) + _dot(h_mid, wrh_ref[...]) + br_ref[...]
    comb, grp = _route(r)
    lane = lax.broadcasted_iota(jnp.int32, comb.shape, 1)
    route = jnp.where(lane == GROUP_LANE, grp, comb)
    route_ref[...] = route
    grp_t_ref[...] = route.T[GROUP_LANE:GROUP_LANE + 8, :]
    cnt = jnp.sum(jnp.where(lane.astype(F32) == grp, 1.0, 0.0), axis=0, keepdims=True)
    cnt_ref[...] = jnp.broadcast_to(cnt, cnt_ref.shape)


def _merge(x, oa, ob, w):
    t_total = x.shape[0]
    tm = MOE_TILE
    n_tiles = t_total // tm
    row = lambda n: pl.BlockSpec((tm, n), lambda i: (i, 0))
    consts = [w["g_mix"], w["w_gate"], w["w_pa"], w["w_pb"], w["w_o"], w["g_ffn"], w["w_router_hi"],
              w["w_router_mid"], w["b_router"]]
    return pl.pallas_call(
        _merge_kernel,
        grid=(n_tiles,),
        in_specs=[row(D_MODEL), row(WIDTH), row(WIDTH)] + [_const_spec(c.shape) for c in consts],
        out_specs=[row(D_MODEL), row(D_MODEL), row(LANES), pl.BlockSpec((8, tm), lambda i: (0, i)),
                   pl.BlockSpec((None, 8, LANES), lambda i: (i, 0, 0))],
        out_shape=[jax.ShapeDtypeStruct(x.shape, F32), jax.ShapeDtypeStruct(x.shape, BF16),
                   jax.ShapeDtypeStruct((t_total, LANES), F32), jax.ShapeDtypeStruct((8, t_total), F32),
                   jax.ShapeDtypeStruct((n_tiles, 8, LANES), F32)],
        compiler_params=_params(1),
        name="merge",
    )(x, oa, ob, *consts)


def _moe_plan(cnt, t_total):
    n_tiles = cnt.shape[0]
    excl = lambda a, axis: jnp.cumsum(a, axis=axis) - a
    n_al = (cnt + ROW_ALIGN - 1) // ROW_ALIGN * ROW_ALIGN
    in_base = excl(n_al, 1)
    tot = jnp.sum(n_al, axis=0)
    n_ffn = (tot + MOE_TILE - 1) // MOE_TILE
    g_base = excl((n_ffn + 1) * MOE_TILE, 0)
    dst = g_base[None, :] + excl(n_al, 0)
    tail = g_base + tot
    last_tile = g_base + n_ffn * MOE_TILE
    n_steps = (t_total // MOE_TILE + 2 * N_GROUPS
               + (N_GROUPS * (ROW_ALIGN - 1) * n_tiles + MOE_TILE - 1) // MOE_TILE)
    ends = jnp.cumsum(n_ffn + 1)
    k = jnp.minimum(jnp.arange(n_steps), ends[-1] - 1)
    g_of_k = jnp.sum(k[:, None] >= ends[None, :], axis=1)
    j = k - (ends - n_ffn - 1)[g_of_k]
    out_blk = g_base[g_of_k] // MOE_TILE + j
    is_expert = j < n_ffn[g_of_k]
    first_blk = g_base[jnp.argmax(n_ffn > 0)] // MOE_TILE
    in_blk = jnp.where(is_expert, out_blk, first_blk)
    out_blk = jnp.where(jnp.arange(n_steps) < ends[-1], out_blk, jnp.arange(n_steps))
    fill = jnp.concatenate([tail, last_tile, ends[-1:]])
    i32 = lambda a: a.astype(jnp.int32)
    return (i32(in_base).reshape(-1), i32(dst).reshape(-1), i32(fill), i32(in_blk), i32(out_blk),
            i32(g_of_k), i32(is_expert), n_steps)


def _sort_kernel(ib_ref, dst_ref, fill_ref, hx_ref, route_ref, grp_t_ref, tri_ref,
                 xs_ref, rs_ref, xsort, rsort, sems):
    i = pl.program_id(0)
    n = pl.num_programs(0)
    tm = hx_ref.shape[0]
    slot = i % 2

    def block_copies(src, dst, sl, g):
        src = pl.multiple_of(src, ROW_ALIGN)
        dst = pl.multiple_of(dst, ROW_ALIGN)
        return [pltpu.make_async_copy(xsort.at[sl, pl.ds(src, tm)], xs_ref.at[pl.ds(dst, tm)], sems.at[sl, 0, g]),
                pltpu.make_async_copy(rsort.at[sl, pl.ds(src, tm)], rs_ref.at[pl.ds(dst, tm)], sems.at[sl, 1, g])]

    def copies(src_rows, dst_rows, sl):
        return [c for g in range(N_GROUPS) for c in block_copies(src_rows(g), dst_rows(g), sl, g)]

    def run(cs):
        for c in cs:
            c.start()
        for c in cs:
            c.wait()

    def step_copies(step, sl):
        return copies(lambda g: ib_ref[step * N_GROUPS + g], lambda g: dst_ref[step * N_GROUPS + g], sl)

    @pl.when(i == 0)
    def _():
        for sl in range(2):
            xsort[sl, SORT_ROWS:, :] = jnp.zeros((xsort.shape[1] - SORT_ROWS, D_MODEL), BF16)
            rsort[sl, SORT_ROWS:, :] = jnp.zeros((rsort.shape[1] - SORT_ROWS, LANES), F32)

    g_row = grp_t_ref[0:1, :]
    sub = lax.broadcasted_iota(jnp.int32, (8, tm), 0).astype(F32)
    mine = sub == g_row
    before = _dot(jnp.where(mine, 1.0, 0.0).astype(BF16), tri_ref[...])
    dest = jnp.sum(jnp.where(mine, before, 0.0), axis=0, keepdims=True)
    for g in range(N_GROUPS):
        base = jnp.full(dest.shape, ib_ref[i * N_GROUPS + g], jnp.int32).astype(F32)
        dest = dest + jnp.where(g_row == g, base, 0.0)
    rows = lax.broadcasted_iota(jnp.int32, (SORT_ROWS, tm), 0).astype(F32)
    perm = jnp.where(rows == dest, 1.0, 0.0).astype(BF16)
    xsort[slot, 0:SORT_ROWS, :] = _dot(perm, hx_ref[...]).astype(BF16)
    rsort[slot, 0:SORT_ROWS, :] = _dot3(perm, route_ref[...])

    @pl.when(i > 0)
    def _():
        for c in step_copies(i - 1, 1 - slot):
            c.wait()

    for c in step_copies(i, slot):
        c.start()

    @pl.when(i == n - 1)
    def _():
        for c in step_copies(i, slot):
            c.wait()
        run(copies(lambda g: SORT_ROWS, lambda g: fill_ref[g], slot))
        run(copies(lambda g: SORT_ROWS, lambda g: fill_ref[N_GROUPS + g], slot))

        @pl.loop(fill_ref[2 * N_GROUPS], xs_ref.shape[0] // tm)
        def _(b):
            run(block_copies(SORT_ROWS, b * tm, slot, 0))


def _expert_kernel(ib_ref, ob_ref, grp_ref, kind_ref, xs_ref, rs_ref, w1_ref, w3_ref, w2_ref, ys_ref):
    del ib_ref, ob_ref
    k = pl.program_id(0)

    @pl.when(kind_ref[k] == 1)
    def _():
        x = xs_ref[...]
        hid = jax.nn.silu(_dot(x, w1_ref[...])) * _dot(x, w3_ref[...])
        comb = rs_ref[...]
        lane = lax.broadcasted_iota(jnp.int32, comb.shape, 1)
        first = grp_ref[k] * EXPERTS_PER_GROUP
        scale = []
        for e in range(EXPERTS_PER_GROUP):
            w_e = jnp.sum(jnp.where(lane == first + e, comb, 0.0), axis=-1, keepdims=True)
            scale.append(jnp.broadcast_to(w_e, (comb.shape[0], D_EXPERT)))
        hid = hid * jnp.concatenate(scale, axis=-1)
        ys_ref[...] = _dot(hid.astype(BF16), w2_ref[...]).astype(BF16)

    @pl.when(kind_ref[k] == 0)
    def _():
        ys_ref[...] = jnp.zeros_like(ys_ref)


def _unsort_kernel(ib_ref, dst_ref, y_ref, route_ref, tri_ref, ys_ref, o_ref, ybuf, yasm, sems):
    i = pl.program_id(0)
    n = pl.num_programs(0)
    tm = y_ref.shape[0]
    slot = i % 2

    def fetch(step, sl):
        return [pltpu.make_async_copy(
            ys_ref.at[pl.ds(pl.multiple_of(dst_ref[step * N_GROUPS + g], ROW_ALIGN), tm)],
            ybuf.at[sl, g], sems.at[sl, g]) for g in range(N_GROUPS)]

    @pl.when(i == 0)
    def _():
        yasm[...] = jnp.zeros_like(yasm)
        for c in fetch(0, 0):
            c.start()

    @pl.when(i + 1 < n)
    def _():
        for c in fetch(i + 1, 1 - slot):
            c.start()

    route = route_ref[...]
    lane = lax.broadcasted_iota(jnp.int32, route.shape, 1).astype(F32)
    grp = jnp.sum(jnp.where(lane == GROUP_LANE, route, 0.0), axis=-1, keepdims=True)
    mine = lane == grp
    before = _dot(tri_ref[...], jnp.where(mine, 1.0, 0.0).astype(BF16))
    dest = jnp.sum(jnp.where(mine, before, 0.0), axis=-1, keepdims=True)
    for g in range(N_GROUPS):
        base = jnp.full(dest.shape, ib_ref[i * N_GROUPS + g], jnp.int32).astype(F32)
        dest = dest + jnp.where(grp == g, base, 0.0)
    cols = lax.broadcasted_iota(jnp.int32, (tm, SORT_ROWS), 1).astype(F32)
    perm_t = jnp.where(cols == dest, 1.0, 0.0).astype(BF16)

    for c in fetch(i, slot):
        c.wait()
    for g in range(N_GROUPS):
        yasm[pl.ds(pl.multiple_of(ib_ref[i * N_GROUPS + g], ROW_ALIGN), tm), :] = ybuf[slot, g]
    o_ref[...] = y_ref[...] + _dot(perm_t, yasm[0:SORT_ROWS, :])


def _moe(y, hx, route, grp_t, cnt, w):
    t_total = y.shape[0]
    tm = MOE_TILE
    n_tiles = t_total // tm
    assert SORT_ROWS >= tm + N_GROUPS * (ROW_ALIGN - 1)
    in_base, dst, fill, in_blk, out_blk, grp_of_step, is_expert, n_steps = _moe_plan(
        cnt[:, 0, :N_GROUPS].astype(jnp.int32), t_total)
    cap_rows = n_steps * tm
    lower = jnp.asarray(np.tril(np.ones((tm, tm)), -1), BF16)
    upper = jnp.asarray(np.triu(np.ones((tm, tm)), 1), BF16)
    stage_rows = SORT_ROWS + tm
    any_spec = pl.BlockSpec(memory_space=pl.ANY)

    xs, rs = pl.pallas_call(
        _sort_kernel,
        grid_spec=pltpu.PrefetchScalarGridSpec(
            num_scalar_prefetch=3,
            grid=(n_tiles,),
            in_specs=[pl.BlockSpec((tm, D_MODEL), lambda i, *_: (i, 0)),
                      pl.BlockSpec((tm, LANES), lambda i, *_: (i, 0)),
                      pl.BlockSpec((8, tm), lambda i, *_: (0, i)),
                      pl.BlockSpec((tm, tm), lambda i, *_: (0, 0))],
            out_specs=[any_spec, any_spec],
            scratch_shapes=[pltpu.VMEM((2, stage_rows, D_MODEL), BF16), pltpu.VMEM((2, stage_rows, LANES), F32),
                            pltpu.SemaphoreType.DMA((2, 2, N_GROUPS))],
        ),
        out_shape=[jax.ShapeDtypeStruct((cap_rows, D_MODEL), BF16),
                   jax.ShapeDtypeStruct((cap_rows, LANES), F32)],
        compiler_params=_params(1),
        name="moe_sort",
    )(in_base, dst, fill, hx, route, grp_t, upper)

    step_map = lambda which: (lambda k, ib, ob, grp, kind: ((ib, ob, grp)[which][k], 0))
    wmap = lambda k, ib, ob, grp, kind: (grp[k], 0, 0)
    wspec = pl.BlockSpec((None, D_MODEL, EXPERTS_PER_GROUP * D_EXPERT), wmap)
    ys = pl.pallas_call(
        _expert_kernel,
        grid_spec=pltpu.PrefetchScalarGridSpec(
            num_scalar_prefetch=4,
            grid=(n_steps,),
            in_specs=[pl.BlockSpec((tm, D_MODEL), step_map(0)), pl.BlockSpec((tm, LANES), step_map(0)),
                      wspec, wspec, pl.BlockSpec((None, EXPERTS_PER_GROUP * D_EXPERT, D_MODEL), wmap)],
            out_specs=pl.BlockSpec((tm, D_MODEL), step_map(1)),
        ),
        out_shape=jax.ShapeDtypeStruct((cap_rows, D_MODEL), BF16),
        compiler_params=_params(1),
        name="moe_experts",
    )(in_blk, out_blk, grp_of_step, is_expert, xs, rs, w["w1"], w["w3"], w["w2"])

    return pl.pallas_call(
        _unsort_kernel,
        grid_spec=pltpu.PrefetchScalarGridSpec(
            num_scalar_prefetch=2,
            grid=(n_tiles,),
            in_specs=[pl.BlockSpec((tm, D_MODEL), lambda i, *_: (i, 0)),
                      pl.BlockSpec((tm, LANES), lambda i, *_: (i, 0)),
                      pl.BlockSpec((tm, tm), lambda i, *_: (0, 0)),
                      any_spec],
            out_specs=pl.BlockSpec((tm, D_MODEL), lambda i, *_: (i, 0)),
            scratch_shapes=[pltpu.VMEM((2, N_GROUPS, tm, D_MODEL), BF16),
                            pltpu.VMEM((stage_rows, D_MODEL), BF16),
                            pltpu.SemaphoreType.DMA((2, N_GROUPS))],
        ),
        out_shape=jax.ShapeDtypeStruct(y.shape, F32),
        compiler_params=_params(1),
        name="moe_unsort",
    )(in_base, dst, y, route, lower, ys)


def _group_stack(w_up):
    g = w_up.reshape(N_GROUPS, EXPERTS_PER_GROUP, D_MODEL, D_EXPERT)
    return jnp.transpose(g, (0, 2, 1, 3)).reshape(N_GROUPS, D_MODEL, EXPERTS_PER_GROUP * D_EXPERT).astype(BF16)


def _prep_weights(g_mix, w_in, b_f, q_norm_a, k_norm_a, q_norm_b, k_norm_b, w_pa, w_pb, w_o,
                  g_ffn, w_rg, b_rg, w_re, b_re, w1, w3, w2):
    n_qkv = 6 * WIDTH
    tile = lambda g: jnp.tile(g, N_HEADS)
    w_router = jnp.concatenate(
        [jnp.transpose(w_re, (1, 0, 2)).reshape(D_MODEL, N_EXPERTS), w_rg,
         jnp.zeros((D_MODEL, LANES - N_EXPERTS - N_GROUPS), F32)], axis=1)
    b_router = jnp.concatenate(
        [b_re.reshape(N_EXPERTS), b_rg, jnp.zeros((LANES - N_EXPERTS - N_GROUPS,), F32)])[None, :]
    return {
        "g_mix": g_mix[None, :],
        "w_qkv": w_in[:, :n_qkv].astype(BF16),
        "w_f": jnp.pad(w_in[:, n_qkv:n_qkv + N_HEADS], ((0, 0), (0, LANES - N_HEADS))).astype(BF16),
        "b_f": jnp.pad(b_f, (0, LANES - N_HEADS))[None, :],
        "w_gate": w_in[:, n_qkv + N_HEADS:].astype(BF16),
        "gains": jnp.stack([tile(q_norm_a), tile(k_norm_a), tile(q_norm_b), tile(k_norm_b)]),
        "w_pa": w_pa.astype(BF16), "w_pb": w_pb.astype(BF16), "w_o": w_o.astype(BF16),
        "g_ffn": g_ffn[None, :],
        "w_router_hi": w_router.astype(BF16),
        "w_router_mid": (w_router - w_router.astype(BF16).astype(F32)).astype(BF16),
        "b_router": b_router,
        "w1": _group_stack(w1), "w3": _group_stack(w3),
        "w2": w2.reshape(N_GROUPS, EXPERTS_PER_GROUP * D_EXPERT, D_MODEL).astype(BF16),
    }


def kernel(x_prompt, x_sample, cache_a_k, cache_a_v, cache_b_k, cache_b_v, cache_b_logf, g_mix, w_in, b_f, q_norm_a, k_norm_a, q_norm_b, k_norm_b, rel_bias, w_pa, w_pb, w_o, g_ffn, w_rg, b_rg, w_re, b_re, w1, w3, w2):
    assert g_mix.shape[0] == 1, "single-layer step"
    n_p, seq, _ = x_prompt.shape
    n_s, t_s, _ = x_sample.shape
    a_rows = cache_a_k.shape[2]
    w = _prep_weights(g_mix[0], w_in[0], b_f[0], q_norm_a[0], k_norm_a[0], q_norm_b[0], k_norm_b[0],
                      w_pa[0], w_pb[0], w_o[0], g_ffn[0], w_rg[0], b_rg[0], w_re[0], b_re[0],
                      w1[0], w3[0], w2[0])
    band_tq = 256
    bias_prompt = _band_bias(rel_bias[0], band_tq, 3 * band_tq, 2 * band_tq, band_mask=True)
    bias_sample = jnp.pad(_band_bias(rel_bias[0], t_s, a_rows + t_s, a_rows, band_mask=False),
                          ((0, 0), (0, 0), (0, LANES - t_s)), constant_values=NEG)

    seq_minor = lambda a: jnp.transpose(a, (0, 2, 3, 1))
    seq_major = lambda a: jnp.transpose(a, (0, 3, 1, 2))[None]

    xp = x_prompt.reshape(n_p * seq, D_MODEL)
    (qa, ka, va, ka_t, va_t, qcat, kcat, vb, kb_t, vb_t, logf_t) = _inproj(xp, n_p, w, seq_minor=True)
    o_a = _band_prompt(qa, ka, va, bias_prompt, n_p)
    o_b = _fox_prompt(qcat, kcat, vb, n_p)
    y_p = _moe(*_merge(xp, o_a, o_b, w), w)

    xs = x_sample.reshape(n_s * t_s, D_MODEL)
    (qa_s, _, _, ka_fs, va_fs, qcat_s, kcat_s, vb_s, kb_fs, vb_fs, logf_s) = _inproj(
        xs, 1, w, seq_minor=False)
    o_as, new_ak, new_av = _band_sample(
        qa_s, ka_fs, va_fs, cache_a_k[0].reshape(n_s, a_rows, WIDTH),
        cache_a_v[0].reshape(n_s, a_rows, WIDTH), bias_sample)
    lf_cache_t = jnp.transpose(cache_b_logf[0], (0, 2, 1))
    lf_new_t = jnp.transpose(logf_s.reshape(n_s, t_s, N_HEADS), (0, 2, 1))
    o_bs = _fox_sample(qcat_s, kcat_s, vb_s, seq_minor(cache_b_k[0]), seq_minor(cache_b_v[0]),
                       lf_cache_t, lf_new_t)
    y_s = _moe(*_merge(xs, o_as, o_bs, w), w)

    heads = lambda a, n, r: a.reshape(1, n, r, N_HEADS, HEAD_DIM)
    return (y_p.reshape(n_p, seq, D_MODEL), y_s.reshape(n_s, t_s, D_MODEL),
            seq_major(ka_t), seq_major(va_t), seq_major(kb_t), seq_major(vb_t),
            jnp.transpose(logf_t, (0, 2, 1))[None],
            heads(new_ak, n_s, a_rows), heads(new_av, n_s, a_rows),
            heads(kb_fs, n_s, t_s), heads(vb_fs, n_s, t_s), logf_s.reshape(1, n_s, t_s, N_HEADS))
```

```python
import functools

import numpy as np
import jax
import jax.numpy as jnp
from jax import lax
from jax.experimental import pallas as pl
from jax.experimental.pallas import tpu as pltpu

F32 = jnp.float32
BF16 = jnp.bfloat16

D_MODEL = 1024
HEAD_DIM = 64
N_HEADS = 8
WIDTH = N_HEADS * HEAD_DIM
N_PAIRS = N_HEADS // 2
CHUNK = 64
LEFT_CHUNKS = 8
WINDOW_ROWS = LEFT_CHUNKS * CHUNK
MAX_REL = 256
N_GROUPS = 4
EXPERTS_PER_GROUP = 8
N_EXPERTS = N_GROUPS * EXPERTS_PER_GROUP
D_EXPERT = 128
EPS = 1e-6
NEG = -1e30
LOG2E = 1.4426950408889634
LANES = 128
BIAS_PERIOD = 1024
AUG_STRIDE = 8
VMEM_LIMIT = 56 * 1024 * 1024

_NT = (((1,), (1,)), ((), ()))


def _dot(a, b):
    return jnp.dot(a, b, preferred_element_type=F32)


def _dot_nt(a, b):
    return lax.dot_general(a, b, _NT, preferred_element_type=F32)


def _split3(x):
    hi = x.astype(BF16)
    r = x - hi.astype(F32)
    mid = r.astype(BF16)
    lo = (r - mid.astype(F32)).astype(BF16)
    return hi, mid, lo


def _dot3(a_bf, x):
    hi, mid, lo = _split3(x)
    return _dot(a_bf, hi) + _dot(a_bf, mid) + _dot(a_bf, lo)


def _rms(x, g):
    ms = jnp.mean(x * x, axis=-1, keepdims=True)
    return x * lax.rsqrt(ms + EPS) * g


def _params(n_axes):
    return pltpu.CompilerParams(dimension_semantics=("arbitrary",) * n_axes,
                                vmem_limit_bytes=VMEM_LIMIT)


def _const_spec(shape):
    nd = len(shape)
    return pl.BlockSpec(shape, lambda *_: (0,) * nd)


def _inproj_kernel(x_ref, g_ref, wqkv_ref, wf_ref, bf_ref, gains_ref, bd_ref, ltri_ref,
                   selq_ref, selk_ref, oneq_ref, onek_ref,
                   qa_ref, ka_ref, va_ref, kaf_ref, vaf_ref, qcat_ref, kcat_ref, vb_ref,
                   kbf_ref, vbf_ref, logf_ref, carry_ref, *, tiles_per_seq, seq_minor):
    i = pl.program_id(0)
    tm = x_ref.shape[0]

    @pl.when(i % tiles_per_seq == 0)
    def _():
        carry_ref[...] = jnp.zeros_like(carry_ref)

    def put_f32(ref, y, last_tile_only=False):
        if not seq_minor:
            ref[...] = y
        elif last_tile_only:
            @pl.when(i % tiles_per_seq == tiles_per_seq - 1)
            def _():
                ref[...] = y.T.reshape(N_HEADS, HEAD_DIM, tm)
        else:
            ref[...] = y.T.reshape(N_HEADS, HEAD_DIM, tm)

    h = _rms(x_ref[...], g_ref[...]).astype(BF16)

    def seg(s):
        return _dot(h, wqkv_ref[:, s * WIDTH:(s + 1) * WIDTH])

    def headnorm(y, n):
        ss = _dot((y * y).astype(BF16), bd_ref[...])
        return y * lax.rsqrt(ss * (1.0 / HEAD_DIM) + EPS) * gains_ref[n:n + 1, :]

    q_a = headnorm(seg(0), 0)
    qa_ref[...] = (q_a * (HEAD_DIM ** -0.5)).astype(BF16)
    k_a = headnorm(seg(1), 1)
    ka_ref[...] = k_a.astype(BF16)
    put_f32(kaf_ref, k_a, last_tile_only=True)
    v_a = seg(2)
    va_ref[...] = v_a.astype(BF16)
    put_f32(vaf_ref, v_a, last_tile_only=True)

    z = _dot(h, wf_ref[...]) + bf_ref[...]
    logf = jnp.minimum(z, 0.0) - jnp.log(1.0 + jnp.exp(-jnp.abs(z)))
    if seq_minor:
        logf_ref[...] = logf.T[0:N_HEADS, :]
    else:
        logf_ref[...] = logf[:, :N_HEADS]
    c = _dot3(ltri_ref[...], logf) + carry_ref[0:1, :]
    carry_ref[...] = jnp.broadcast_to(c[-1:, :], carry_ref.shape)
    cs = jnp.concatenate(_split3(c * LOG2E), axis=-1)
    q_aug = (_dot(cs, selq_ref[...]) + oneq_ref[...]).astype(BF16)
    k_aug = (_dot(cs, selk_ref[...]) + onek_ref[...]).astype(BF16)

    q_b = (headnorm(seg(3), 2) * (HEAD_DIM ** -0.5 * LOG2E)).astype(BF16)
    k_b = headnorm(seg(4), 3)
    put_f32(kbf_ref, k_b)
    k_b = k_b.astype(BF16)
    v_b = seg(5)
    put_f32(vbf_ref, v_b)
    vb_ref[...] = v_b.astype(BF16)
    for p in range(N_PAIRS):
        cols = slice(p * LANES, (p + 1) * LANES)
        qcat_ref[p, :, 0:LANES] = q_b[:, cols]
        qcat_ref[p, :, LANES:2 * LANES] = q_aug
        kcat_ref[p, :, 0:LANES] = k_b[:, cols]
        kcat_ref[p, :, LANES:2 * LANES] = k_aug


def _aug_constants():
    selq = np.zeros((3 * LANES, LANES), np.float32)
    selk = np.zeros((3 * LANES, LANES), np.float32)
    oneq = np.zeros((1, LANES), np.float32)
    onek = np.zeros((1, LANES), np.float32)
    for h in range(N_HEADS):
        for k in range(3):
            selq[k * LANES + h, AUG_STRIDE * h + k] = 1.0
            selk[k * LANES + h, AUG_STRIDE * h + 3 + k] = -1.0
            oneq[0, AUG_STRIDE * h + 3 + k] = 1.0
            onek[0, AUG_STRIDE * h + k] = 1.0
    return (jnp.asarray(selq, BF16), jnp.asarray(selk, BF16), jnp.asarray(oneq), jnp.asarray(onek))


def _inproj(x, n_seq, w, seq_minor):
    t_total = x.shape[0]
    tm = 512
    n_tiles = t_total // tm
    seq = t_total // n_seq
    tps = seq // tm
    row = lambda i: (i, 0)
    flat = lambda dt: (jax.ShapeDtypeStruct((t_total, WIDTH), dt), pl.BlockSpec((tm, WIDTH), row))
    if seq_minor:
        assert tm == WINDOW_ROWS
        band_f32 = (jax.ShapeDtypeStruct((n_seq, N_HEADS, HEAD_DIM, WINDOW_ROWS), F32),
                    pl.BlockSpec((None, N_HEADS, HEAD_DIM, tm), lambda i: (i // tps, 0, 0, 0)))
        fox_f32 = (jax.ShapeDtypeStruct((n_seq, N_HEADS, HEAD_DIM, seq), F32),
                   pl.BlockSpec((None, N_HEADS, HEAD_DIM, tm), lambda i: (i // tps, 0, 0, i % tps)))
        logf = (jax.ShapeDtypeStruct((n_seq, N_HEADS, seq), F32),
                pl.BlockSpec((None, N_HEADS, tm), lambda i: (i // tps, 0, i % tps)))
    else:
        band_f32 = fox_f32 = flat(F32)
        logf = (jax.ShapeDtypeStruct((t_total, N_HEADS), F32), pl.BlockSpec((tm, N_HEADS), row))
    cat = (jax.ShapeDtypeStruct((n_seq, N_PAIRS, seq, 2 * LANES), BF16),
           pl.BlockSpec((None, N_PAIRS, tm, 2 * LANES), lambda i: (i // tps, 0, i % tps, 0)))
    outs = [flat(BF16), flat(BF16), flat(BF16), band_f32, band_f32, cat, cat, flat(BF16),
            fox_f32, fox_f32, logf]
    bd = jnp.asarray(np.kron(np.eye(N_HEADS), np.ones((HEAD_DIM, HEAD_DIM))), BF16)
    ltri = jnp.asarray(np.tril(np.ones((tm, tm))), BF16)
    consts = [w["g_mix"], w["w_qkv"], w["w_f"], w["b_f"], w["gains"], bd, ltri, *_aug_constants()]
    in_specs = [pl.BlockSpec((tm, D_MODEL), row)] + [_const_spec(c.shape) for c in consts]
    return pl.pallas_call(
        functools.partial(_inproj_kernel, tiles_per_seq=tps, seq_minor=seq_minor),
        grid=(n_tiles,),
        in_specs=in_specs,
        out_specs=[o[1] for o in outs],
        out_shape=[o[0] for o in outs],
        scratch_shapes=[pltpu.VMEM((8, LANES), F32)],
        compiler_params=_params(1),
        name="inproj",
    )(x, *consts)


def _softmax_pv(scores, values):
    chunks = [[s[:, c * LANES:(c + 1) * LANES] for c in range(s.shape[1] // LANES)] for s in scores]
    cmax = None
    for cs in chunks:
        for c in cs:
            cmax = c if cmax is None else jnp.maximum(cmax, c)
    m = jnp.max(cmax, axis=-1, keepdims=True)
    pv = None
    lsum = None
    for cs, v in zip(chunks, values):
        ps = [jnp.exp(c - m) for c in cs]
        for p in ps:
            lsum = p if lsum is None else lsum + p
        o = _dot(jnp.concatenate(ps, axis=-1).astype(BF16), v)
        pv = o if pv is None else pv + o
    return pv, jnp.sum(lsum, axis=-1, keepdims=True)


def _head_mask(hh):
    lane = lax.broadcasted_iota(jnp.int32, (1, LANES), 1)
    return (lane // HEAD_DIM) == hh


def _band_prompt_kernel(q_ref, k0_ref, k1_ref, k2_ref, v0_ref, v1_ref, v2_ref, e_ref, mask_ref, o_ref,
                        bias_ref, *, tq):
    i = pl.program_id(1)

    @pl.when((pl.program_id(0) == 0) & (i == 0))
    def _():
        _build_bias(e_ref, mask_ref, bias_ref)

    k_refs = (k0_ref, k1_ref, k2_ref)
    v_refs = (v0_ref, v1_ref, v2_ref)
    pad = [jnp.where(i - 2 + j >= 0, 0.0, NEG) for j in range(2)] + [0.0]
    for p in range(N_PAIRS):
        cols = slice(p * LANES, (p + 1) * LANES)
        q = q_ref[:, cols]
        ks = [r[:, cols] for r in k_refs]
        vs = [r[:, cols] for r in v_refs]
        out = jnp.zeros((tq, LANES), F32)
        for hh in range(2):
            msk = _head_mask(hh)
            qh = jnp.where(msk, q, jnp.zeros_like(q))
            scores = [_dot_nt(qh, ks[j]) + bias_ref[2 * p + hh, :, j * tq:(j + 1) * tq] + pad[j]
                      for j in range(3)]
            pv, l = _softmax_pv(scores, vs)
            out = jnp.where(msk, pv / l, out)
        o_ref[:, cols] = out.astype(BF16)


def _band_prompt(qa, ka, va, bias_e, bias_mask, n_seq):
    t_total = qa.shape[0]
    tq = 256
    seq = t_total // n_seq
    nq = seq // tq
    qmap = lambda b, i: (b * nq + i, 0)

    def kmap(j):
        return lambda b, i: (b * nq + jnp.maximum(i - 2 + j, 0), 0)

    blk = lambda m: pl.BlockSpec((tq, WIDTH), m)
    return pl.pallas_call(
        functools.partial(_band_prompt_kernel, tq=tq),
        grid=(n_seq, nq),
        in_specs=[blk(qmap)] + [blk(kmap(j)) for j in range(3)] + [blk(kmap(j)) for j in range(3)]
        + [_const_spec(bias_e.shape), _const_spec(bias_mask.shape)],
        out_specs=blk(qmap),
        out_shape=jax.ShapeDtypeStruct((t_total, WIDTH), BF16),
        scratch_shapes=[pltpu.VMEM((N_HEADS,) + bias_mask.shape, F32)],
        compiler_params=_params(2),
        name="band_prompt",
    )(qa, ka, ka, ka, va, va, va, bias_e, bias_mask)


def _band_sample_kernel(q_ref, kc_ref, vc_ref, kn_ref, vn_ref, e_ref, mask_ref, o_ref, ko_ref, vo_ref,
                        bias_ref):
    p_rows = kc_ref.shape[0]
    t = q_ref.shape[0]

    @pl.when(pl.program_id(0) == 0)
    def _():
        _build_bias(e_ref, mask_ref, bias_ref)

    for p in range(N_PAIRS):
        cols = slice(p * LANES, (p + 1) * LANES)
        q = q_ref[:, cols]
        pad = jnp.zeros((LANES - t, LANES), BF16)
        ks = [kc_ref[:, cols].astype(BF16), jnp.concatenate([kn_ref[:, cols].astype(BF16), pad], axis=0)]
        vs = [vc_ref[:, cols].astype(BF16), jnp.concatenate([vn_ref[:, cols].astype(BF16), pad], axis=0)]
        out = jnp.zeros((t, LANES), F32)
        for hh in range(2):
            msk = _head_mask(hh)
            qh = jnp.where(msk, q, jnp.zeros_like(q))
            scores = [_dot_nt(qh, ks[0]) + bias_ref[2 * p + hh, :, 0:p_rows],
                      _dot_nt(qh, ks[1]) + bias_ref[2 * p + hh, :, p_rows:p_rows + LANES]]
            pv, l = _softmax_pv(scores, vs)
            out = jnp.where(msk, pv / l, out)
        o_ref[:, cols] = out.astype(BF16)
    ko_ref[0:p_rows - t, :] = kc_ref[t:p_rows, :]
    ko_ref[p_rows - t:p_rows, :] = kn_ref[...]
    vo_ref[0:p_rows - t, :] = vc_ref[t:p_rows, :]
    vo_ref[p_rows - t:p_rows, :] = vn_ref[...]


def _band_sample(qa, ka_f, va_f, cache_k, cache_v, bias_e, bias_mask):
    n, p_rows, _ = cache_k.shape
    t = qa.shape[0] // n
    row = pl.BlockSpec((t, WIDTH), lambda b: (b, 0))
    cache = pl.BlockSpec((None, p_rows, WIDTH), lambda b: (b, 0, 0))
    return pl.pallas_call(
        _band_sample_kernel,
        grid=(n,),
        in_specs=[row, cache, cache, row, row, _const_spec(bias_e.shape), _const_spec(bias_mask.shape)],
        out_specs=[row, cache, cache],
        out_shape=[jax.ShapeDtypeStruct(qa.shape, BF16),
                   jax.ShapeDtypeStruct(cache_k.shape, F32),
                   jax.ShapeDtypeStruct(cache_v.shape, F32)],
        scratch_shapes=[pltpu.VMEM((N_HEADS,) + bias_mask.shape, F32)],
        compiler_params=_params(1),
        name="band_sample",
    )(qa, cache_k, cache_v, ka_f, va_f, bias_e, bias_mask)


def _band_bias(rel_bias, n_q, n_k, offset, band_mask, valid_k):
    assert n_q + n_k - 1 <= BIAS_PERIOD
    m = np.arange(BIAS_PERIOD)
    w = np.where(m < n_k, m, m - BIAS_PERIOD)
    idx = np.clip(offset - w, -MAX_REL, MAX_REL) + MAX_REL
    e = jnp.take(rel_bias.astype(F32), jnp.asarray(idx), axis=1)
    qi = np.arange(n_q)[:, None]
    kj = np.arange(n_k)[None, :]
    valid = np.broadcast_to(kj < valid_k, (n_q, n_k))
    if band_mask:
        rel_chunk = (qi + offset) // CHUNK - kj // CHUNK
        valid = valid & (rel_chunk >= 0) & (rel_chunk <= LEFT_CHUNKS)
    return e, jnp.asarray(np.where(valid, 0.0, NEG), F32)


def _build_bias(e_ref, mask_ref, bias_sc):
    rows, cols = mask_ref.shape
    for h in range(N_HEADS):
        spread = jnp.broadcast_to(e_ref[h:h + 1, :], (rows, BIAS_PERIOD))
        bias_sc[h] = pltpu.roll(spread, 0, 1, stride=1, stride_axis=0)[:, :cols] + mask_ref[...]


def _aug_head_mask(h_in_pair, pair):
    lane = lax.broadcasted_iota(jnp.int32, (1, 2 * LANES), 1)
    head = 2 * pair + h_in_pair
    in_q = (lane < LANES) & ((lane // HEAD_DIM) == h_in_pair)
    in_aug = (lane >= LANES) & (((lane - LANES) // AUG_STRIDE) == head)
    return in_q | in_aug


def _flash_update(s, v, m_ref, l_ref, acc_ref, h, row_bias=None, v_seq_minor=False):
    chunks = [s[:, c * LANES:(c + 1) * LANES] for c in range(s.shape[1] // LANES)]
    if row_bias is not None:
        chunks = [c + row_bias for c in chunks]
    cmax = chunks[0]
    for c in chunks[1:]:
        cmax = jnp.maximum(cmax, c)
    m_prev = m_ref[h]
    m_new = jnp.maximum(m_prev, jnp.max(cmax, axis=-1, keepdims=True))
    alpha = jnp.exp2(m_prev - m_new)
    ps = [jnp.exp2(c - m_new) for c in chunks]
    lsum = ps[0]
    for p in ps[1:]:
        lsum = lsum + p
    l_ref[h] = alpha * l_ref[h] + lsum
    p = jnp.concatenate(ps, axis=-1).astype(BF16)
    pv = _dot_nt(p, v) if v_seq_minor else _dot(p, v)
    acc_ref[h] = alpha[:, :pv.shape[1]] * acc_ref[h] + pv
    m_ref[h] = m_new


def _flash_result(l_ref, acc_ref, pair):
    outs = [acc_ref[2 * pair + hh] / jnp.sum(l_ref[2 * pair + hh], axis=-1, keepdims=True)
            for hh in range(2)]
    return jnp.where(_head_mask(0), outs[0], outs[1])


def _fox_prompt_kernel(qi_ref, kj_ref, q_ref, k_ref, v_ref, o_ref, qh_sc, m_sc, l_sc, acc_sc, *, tq, tk):
    t = pl.program_id(1)
    i = qi_ref[t]
    j = kj_ref[t]

    @pl.when(j == 0)
    def _():
        for p in range(N_PAIRS):
            q = q_ref[p]
            for hh in range(2):
                qh_sc[2 * p + hh] = jnp.where(_aug_head_mask(hh, p), q, jnp.zeros_like(q))
        m_sc[...] = jnp.full_like(m_sc, NEG)
        l_sc[...] = jnp.zeros_like(l_sc)
        acc_sc[...] = jnp.zeros_like(acc_sc)

    def step(diagonal):
        if diagonal:
            keep = (lax.broadcasted_iota(jnp.int32, (tq, tk), 1)
                    <= lax.broadcasted_iota(jnp.int32, (tq, tk), 0))
        for p in range(N_PAIRS):
            k = k_ref[p]
            v = v_ref[:, p * LANES:(p + 1) * LANES]
            for hh in range(2):
                s = _dot_nt(qh_sc[2 * p + hh], k)
                if diagonal:
                    s = jnp.where(keep, s, NEG)
                _flash_update(s, v, m_sc, l_sc, acc_sc, 2 * p + hh)

    @pl.when(j < i)
    def _():
        step(False)

    @pl.when(j == i)
    def _():
        step(True)
        for p in range(N_PAIRS):
            o_ref[:, p * LANES:(p + 1) * LANES] = _flash_result(l_sc, acc_sc, p).astype(BF16)


def _fox_prompt(qcat, kcat, vb, n_seq):
    seq = qcat.shape[2]
    tq = tk = 512
    nt = seq // tq
    qi = np.concatenate([np.full(i + 1, i) for i in range(nt)]).astype(np.int32)
    kj = np.concatenate([np.arange(i + 1) for i in range(nt)]).astype(np.int32)
    grid_spec = pltpu.PrefetchScalarGridSpec(
        num_scalar_prefetch=2,
        grid=(n_seq, len(qi)),
        in_specs=[
            pl.BlockSpec((None, N_PAIRS, tq, 2 * LANES), lambda b, t, qi, kj: (b, 0, qi[t], 0)),
            pl.BlockSpec((None, N_PAIRS, tk, 2 * LANES), lambda b, t, qi, kj: (b, 0, kj[t], 0)),
            pl.BlockSpec((tk, WIDTH), lambda b, t, qi, kj: (b * nt + kj[t], 0)),
        ],
        out_specs=pl.BlockSpec((tq, WIDTH), lambda b, t, qi, kj: (b * nt + qi[t], 0)),
        scratch_shapes=[pltpu.VMEM((N_HEADS, tq, 2 * LANES), BF16), pltpu.VMEM((N_HEADS, tq, LANES), F32),
                        pltpu.VMEM((N_HEADS, tq, LANES), F32), pltpu.VMEM((N_HEADS, tq, LANES), F32)],
    )
    return pl.pallas_call(
        functools.partial(_fox_prompt_kernel, tq=tq, tk=tk),
        grid_spec=grid_spec,
        out_shape=jax.ShapeDtypeStruct(vb.shape, BF16),
        compiler_params=_params(2),
        name="fox_prompt",
    )(jnp.asarray(qi), jnp.asarray(kj), qcat, kcat, vb)


def _suffix_sum_exclusive(x):
    n = x.shape[1]
    lane = lax.broadcasted_iota(jnp.int32, x.shape, 1)
    y = jnp.where(lane + 1 < n, pltpu.roll(x, n - 1, axis=1), 0.0)
    shift = 1
    while shift < n:
        y = y + jnp.where(lane + shift < n, pltpu.roll(y, n - shift, axis=1), 0.0)
        shift *= 2
    return y


def _fox_sample_kernel(q_ref, kc_ref, vc_ref, kn_ref, vn_ref, lfc_ref, lfn_ref, u_ref,
                       o_ref, cq_sc, car_sc, m_sc, l_sc, acc_sc, *, n_cache_tiles, sub_keys):
    j = pl.program_id(1)
    t = q_ref.shape[1]

    def cum_new():
        hi, mid, lo = _split3(lfn_ref[...])
        u = u_ref[...]
        return _dot(hi, u) + _dot(mid, u) + _dot(lo, u)

    def q_head(h):
        return q_ref[h // 2, :, (h % 2) * HEAD_DIM:(h % 2 + 1) * HEAD_DIM]

    @pl.when(j == 0)
    def _():
        cn = cum_new() * LOG2E
        eye = (lax.broadcasted_iota(jnp.int32, (t, LANES), 0)
               == lax.broadcasted_iota(jnp.int32, (t, LANES), 1))
        for h in range(N_HEADS):
            col = jnp.sum(jnp.where(eye, jnp.broadcast_to(cn[h:h + 1, :], (t, LANES)), 0.0),
                          axis=-1, keepdims=True)
            cq_sc[h] = jnp.broadcast_to(col, (t, LANES))
        car_sc[...] = jnp.zeros_like(car_sc)
        m_sc[...] = jnp.full_like(m_sc, NEG)
        l_sc[...] = jnp.zeros_like(l_sc)
        acc_sc[...] = jnp.zeros_like(acc_sc)

    @pl.when(j < n_cache_tiles)
    def _():
        lf = lfc_ref[...]
        r = _suffix_sum_exclusive(lf) + car_sc[:, 0:1]
        car_sc[...] = jnp.broadcast_to(r[:, 0:1] + lf[:, 0:1], car_sc.shape)
        r = r * LOG2E
        for sub in range(lf.shape[1] // sub_keys):
            keys = slice(sub * sub_keys, (sub + 1) * sub_keys)
            for h in range(N_HEADS):
                s = _dot(q_head(h), kc_ref[h, :, keys].astype(BF16)) + r[h:h + 1, keys]
                _flash_update(s, vc_ref[h, :, keys].astype(BF16), m_sc, l_sc, acc_sc, h,
                              row_bias=cq_sc[h], v_seq_minor=True)

    @pl.when(j == n_cache_tiles)
    def _():
        cn = cum_new() * LOG2E
        causal = (lax.broadcasted_iota(jnp.int32, (t, LANES), 1)
                  <= lax.broadcasted_iota(jnp.int32, (t, LANES), 0))
        pad = jnp.zeros((LANES - t, HEAD_DIM), BF16)
        for h in range(N_HEADS):
            cols = slice(h * HEAD_DIM, (h + 1) * HEAD_DIM)
            k = jnp.concatenate([kn_ref[h // 2, :, (h % 2) * HEAD_DIM:(h % 2 + 1) * HEAD_DIM], pad], axis=0)
            v = jnp.concatenate([vn_ref[:, cols], pad], axis=0)
            s = _dot_nt(q_head(h), k) + cq_sc[h] - cn[h:h + 1, :]
            _flash_update(jnp.where(causal, s, NEG), v, m_sc, l_sc, acc_sc, h)
            o_ref[:, cols] = (acc_sc[h] / jnp.sum(l_sc[h], axis=-1, keepdims=True)).astype(BF16)


def _fox_sample(qcat, kcat, vb, cache_kt, cache_vt, lf_cache_t, lf_new_t):
    n, _, _, p_rows = cache_kt.shape
    t = vb.shape[0] // n
    tk = 2048
    nct = p_rows // tk
    u = jnp.asarray(np.pad(np.triu(np.ones((t, t))), ((0, 0), (0, LANES - t))), BF16)
    rev = lambda j: jnp.maximum(nct - 1 - j, 0)
    cache = pl.BlockSpec((None, N_HEADS, HEAD_DIM, tk), lambda b, j: (b, 0, 0, rev(j)))
    new_cat = pl.BlockSpec((None, N_PAIRS, t, 2 * LANES), lambda b, j: (0, 0, b, 0))
    return pl.pallas_call(
        functools.partial(_fox_sample_kernel, n_cache_tiles=nct, sub_keys=tk),
        grid=(n, nct + 1),
        in_specs=[
            new_cat, cache, cache, new_cat,
            pl.BlockSpec((t, WIDTH), lambda b, j: (b, 0)),
            pl.BlockSpec((None, N_HEADS, tk), lambda b, j: (b, 0, rev(j))),
            pl.BlockSpec((None, N_HEADS, t), lambda b, j: (b, 0, 0)),
            _const_spec(u.shape),
        ],
        out_specs=pl.BlockSpec((t, WIDTH), lambda b, j: (b, 0)),
        out_shape=jax.ShapeDtypeStruct(vb.shape, BF16),
        scratch_shapes=[pltpu.VMEM((N_HEADS, t, LANES), F32), pltpu.VMEM((N_HEADS, LANES), F32),
                        pltpu.VMEM((N_HEADS, t, LANES), F32), pltpu.VMEM((N_HEADS, t, LANES), F32),
                        pltpu.VMEM((N_HEADS, t, HEAD_DIM), F32)],
        compiler_params=_params(2),
        name="fox_sample",
    )(qcat, cache_kt, cache_vt, kcat, vb, lf_cache_t, lf_new_t, u)


GROUP_LANE = 64
ROW_ALIGN = 16
MOE_TILE = 512
SORT_ROWS = 640


def _route(r):
    lane_i = lax.broadcasted_iota(jnp.int32, r.shape, 1)
    lane = lane_i.astype(F32)
    lane_grp = (lane_i // EXPERTS_PER_GROUP).astype(F32)
    big = float(LANES)
    is_coarse = (lane_i >= N_EXPERTS) & (lane_i < N_EXPERTS + N_GROUPS)
    cm = jnp.where(is_coarse, r, NEG)
    cmax = cm.max(axis=-1, keepdims=True)
    grp = jnp.min(jnp.where(cm == cmax, lane - N_EXPERTS, big), axis=-1, keepdims=True)
    pg_sel = 1.0 / jnp.sum(jnp.exp(cm - cmax), axis=-1, keepdims=True)
    in_grp = (lane_i < N_EXPERTS) & (lane_grp == grp)
    fm = jnp.where(in_grp, r, NEG)
    m1 = fm.max(axis=-1, keepdims=True)
    denom = jnp.sum(jnp.exp(fm - m1), axis=-1, keepdims=True)
    i1 = jnp.min(jnp.where(fm == m1, lane, big), axis=-1, keepdims=True)
    fm2 = jnp.where(lane == i1, NEG, fm)
    m2 = fm2.max(axis=-1, keepdims=True)
    i2 = jnp.min(jnp.where(fm2 == m2, lane, big), axis=-1, keepdims=True)
    p1 = 1.0 / denom
    p2 = jnp.exp(m2 - m1) / denom
    tot = p1 + p2
    comb = (jnp.where(lane == i1, pg_sel * (p1 / tot), 0.0)
            + jnp.where(lane == i2, pg_sel * (p2 / tot), 0.0))
    return comb, grp


def _merge_kernel(x_ref, oa_ref, ob_ref, g_ref, wg_ref, wpa_ref, wpb_ref, wo_ref, gf_ref, wrh_ref, wrm_ref,
                  br_ref, y_ref, hx_ref, route_ref, grp_t_ref, cnt_ref):
    x = x_ref[...]
    h = _rms(x, g_ref[...]).astype(BF16)
    gate = jax.nn.sigmoid(_dot(h, wg_ref[...]))
    mix = (gate[:, :D_MODEL] * _dot(oa_ref[...], wpa_ref[...])
           + gate[:, D_MODEL:] * _dot(ob_ref[...], wpb_ref[...]))
    y = x + _dot(mix.astype(BF16), wo_ref[...])
    y_ref[...] = y

    hx = _rms(y, gf_ref[...])
    hx_ref[...] = hx.astype(BF16)
    h_hi, h_mid, _ = _split3(hx)
    r = _dot(h_hi, wrh_ref[...]) + _dot(h_hi, wrm_ref[...]) + _dot(h_mid, wrh_ref[...]) + br_ref[...]
    comb, grp = _route(r)
    lane = lax.broadcasted_iota(jnp.int32, comb.shape, 1)
    route = jnp.where(lane == GROUP_LANE, grp, comb)
    route_ref[...] = route
    grp_t_ref[...] = route.T[GROUP_LANE:GROUP_LANE + 8, :]
    cnt = jnp.sum(jnp.where(lane.astype(F32) == grp, 1.0, 0.0), axis=0, keepdims=True)
    cnt_ref[...] = jnp.broadcast_to(cnt, cnt_ref.shape)


def _merge(x, oa, ob, w):
    t_total = x.shape[0]
    tm = MOE_TILE
    n_tiles = t_total // tm
    row = lambda n: pl.BlockSpec((tm, n), lambda i: (i, 0))
    consts = [w["g_mix"], w["w_gate"], w["w_pa"], w["w_pb"], w["w_o"], w["g_ffn"], w["w_router_hi"],
              w["w_router_mid"], w["b_router"]]
    return pl.pallas_call(
        _merge_kernel,
        grid=(n_tiles,),
        in_specs=[row(D_MODEL), row(WIDTH), row(WIDTH)] + [_const_spec(c.shape) for c in consts],
        out_specs=[row(D_MODEL), row(D_MODEL), row(LANES), pl.BlockSpec((8, tm), lambda i: (0, i)),
                   pl.BlockSpec((None, 8, LANES), lambda i: (i, 0, 0))],
        out_shape=[jax.ShapeDtypeStruct(x.shape, F32), jax.ShapeDtypeStruct(x.shape, BF16),
                   jax.ShapeDtypeStruct((t_total, LANES), F32), jax.ShapeDtypeStruct((8, t_total), F32),
                   jax.ShapeDtypeStruct((n_tiles, 8, LANES), F32)],
        compiler_params=_params(1),
        name="merge",
    )(x, oa, ob, *consts)


def _plan_kernel(cnt_ref, ib_ref, dst_ref, fill_ref, inblk_ref, outblk_ref, grp_ref, kind_ref, *,
                 n_tiles, n_steps):
    align_bits = ROW_ALIGN.bit_length() - 1
    tile_bits = MOE_TILE.bit_length() - 1
    zero = jnp.int32(0)

    def per_tile(t, tot):
        run = zero
        new_tot = []
        for g in range(N_GROUPS):
            n = ((cnt_ref[t * N_GROUPS + g] + (ROW_ALIGN - 1)) >> align_bits) << align_bits
            ib_ref[t * N_GROUPS + g] = run
            dst_ref[t * N_GROUPS + g] = tot[g]
            run = run + n
            new_tot.append(tot[g] + n)
        return tuple(new_tot)

    tot = lax.fori_loop(0, n_tiles, per_tile, (zero,) * N_GROUPS)
    n_ffn = [(tot[g] + (MOE_TILE - 1)) >> tile_bits for g in range(N_GROUPS)]
    g_base, ends, acc_rows, acc_tiles = [], [], zero, zero
    for g in range(N_GROUPS):
        g_base.append(acc_rows)
        acc_rows = acc_rows + ((n_ffn[g] + 1) << tile_bits)
        acc_tiles = acc_tiles + n_ffn[g] + 1
        ends.append(acc_tiles)

    def add_base(t, carry):
        for g in range(N_GROUPS):
            dst_ref[t * N_GROUPS + g] = dst_ref[t * N_GROUPS + g] + g_base[g]
        return carry

    lax.fori_loop(0, n_tiles, add_base, zero)
    for g in range(N_GROUPS):
        fill_ref[g] = g_base[g] + tot[g]
        fill_ref[N_GROUPS + g] = g_base[g] + (n_ffn[g] << tile_bits)
    fill_ref[2 * N_GROUPS] = ends[-1]
    first_blk = g_base[N_GROUPS - 1]
    for g in reversed(range(N_GROUPS - 1)):
        first_blk = jnp.where(n_ffn[g] > 0, g_base[g], first_blk)
    first_blk = first_blk >> tile_bits

    def pick(vals, g):
        out = vals[N_GROUPS - 1]
        for i in reversed(range(N_GROUPS - 1)):
            out = jnp.where(g == i, vals[i], out)
        return out

    def per_step(k, carry):
        kk = jnp.minimum(k, ends[-1] - 1)
        g = zero
        for i in range(N_GROUPS - 1):
            g = g + jnp.where(kk >= ends[i], 1, 0)
        j = kk - pick([zero] + ends[:-1], g)
        is_expert = j < pick(n_ffn, g)
        out_blk = jnp.where(k < ends[-1], (pick(g_base, g) >> tile_bits) + j, k)
        outblk_ref[k] = out_blk
        inblk_ref[k] = jnp.where(is_expert, out_blk, first_blk)
        grp_ref[k] = g
        kind_ref[k] = jnp.where(is_expert, 1, 0)
        return carry

    lax.fori_loop(0, n_steps, per_step, zero)


def _moe_plan(cnt, t_total):
    n_tiles = t_total // MOE_TILE
    n_steps = (n_tiles + 2 * N_GROUPS + (N_GROUPS * (ROW_ALIGN - 1) * n_tiles + MOE_TILE - 1) // MOE_TILE)
    smem = pl.BlockSpec(memory_space=pltpu.SMEM)
    sizes = [n_tiles * N_GROUPS, n_tiles * N_GROUPS, 2 * N_GROUPS + 1, n_steps, n_steps, n_steps, n_steps]
    outs = pl.pallas_call(
        functools.partial(_plan_kernel, n_tiles=n_tiles, n_steps=n_steps),
        grid_spec=pltpu.PrefetchScalarGridSpec(num_scalar_prefetch=1, grid=(1,), in_specs=[],
                                               out_specs=[smem] * len(sizes)),
        out_shape=[jax.ShapeDtypeStruct((n,), jnp.int32) for n in sizes],
        compiler_params=_params(1),
        name="moe_plan",
    )(cnt.reshape(-1))
    return (*outs, n_steps)


def _sort_kernel(ib_ref, dst_ref, fill_ref, hx_ref, route_ref, grp_t_ref, tri_ref,
                 xs_ref, rs_ref, xsort, rsort, sems):
    i = pl.program_id(0)
    n = pl.num_programs(0)
    tm = hx_ref.shape[0]
    slot = i % 2

    def block_copies(src, dst, sl, g):
        src = pl.multiple_of(src, ROW_ALIGN)
        dst = pl.multiple_of(dst, ROW_ALIGN)
        return [pltpu.make_async_copy(xsort.at[sl, pl.ds(src, tm)], xs_ref.at[pl.ds(dst, tm)], sems.at[sl, 0, g]),
                pltpu.make_async_copy(rsort.at[sl, pl.ds(src, tm)], rs_ref.at[pl.ds(dst, tm)], sems.at[sl, 1, g])]

    def copies(src_rows, dst_rows, sl):
        return [c for g in range(N_GROUPS) for c in block_copies(src_rows(g), dst_rows(g), sl, g)]

    def run(cs):
        for c in cs:
            c.start()
        for c in cs:
            c.wait()

    def step_copies(step, sl):
        return copies(lambda g: ib_ref[step * N_GROUPS + g], lambda g: dst_ref[step * N_GROUPS + g], sl)

    @pl.when(i == 0)
    def _():
        for sl in range(2):
            xsort[sl, SORT_ROWS:, :] = jnp.zeros((xsort.shape[1] - SORT_ROWS, D_MODEL), BF16)
            rsort[sl, SORT_ROWS:, :] = jnp.zeros((rsort.shape[1] - SORT_ROWS, LANES), F32)

    g_row = grp_t_ref[0:1, :]
    sub = lax.broadcasted_iota(jnp.int32, (8, tm), 0).astype(F32)
    mine = sub == g_row
    before = _dot(jnp.where(mine, 1.0, 0.0).astype(BF16), tri_ref[...])
    dest = jnp.sum(jnp.where(mine, before, 0.0), axis=0, keepdims=True)
    for g in range(N_GROUPS):
        base = jnp.full(dest.shape, ib_ref[i * N_GROUPS + g], jnp.int32).astype(F32)
        dest = dest + jnp.where(g_row == g, base, 0.0)
    rows = lax.broadcasted_iota(jnp.int32, (SORT_ROWS, tm), 0).astype(F32)
    perm = jnp.where(rows == dest, 1.0, 0.0).astype(BF16)
    xsort[slot, 0:SORT_ROWS, :] = _dot(perm, hx_ref[...]).astype(BF16)
    rsort[slot, 0:SORT_ROWS, :] = _dot3(perm, route_ref[...])

    @pl.when(i > 0)
    def _():
        for c in step_copies(i - 1, 1 - slot):
            c.wait()

    for c in step_copies(i, slot):
        c.start()

    @pl.when(i == n - 1)
    def _():
        for c in step_copies(i, slot):
            c.wait()
        run(copies(lambda g: SORT_ROWS, lambda g: fill_ref[g], slot))
        run(copies(lambda g: SORT_ROWS, lambda g: fill_ref[N_GROUPS + g], slot))

        @pl.loop(fill_ref[2 * N_GROUPS], xs_ref.shape[0] // tm)
        def _(b):
            run(block_copies(SORT_ROWS, b * tm, slot, 0))


def _expert_kernel(ib_ref, ob_ref, grp_ref, kind_ref, xs_ref, rs_ref, w1_ref, w3_ref, w2_ref, ys_ref):
    del ib_ref, ob_ref
    k = pl.program_id(0)

    @pl.when(kind_ref[k] == 1)
    def _():
        x = xs_ref[...]
        side = lambda w_ref: jnp.concatenate([w_ref[e].astype(BF16) for e in range(EXPERTS_PER_GROUP)], axis=1)
        hid = jax.nn.silu(_dot(x, side(w1_ref))) * _dot(x, side(w3_ref))
        comb = rs_ref[...]
        lane = lax.broadcasted_iota(jnp.int32, comb.shape, 1)
        first = grp_ref[k] * EXPERTS_PER_GROUP
        scale = []
        for e in range(EXPERTS_PER_GROUP):
            w_e = jnp.sum(jnp.where(lane == first + e, comb, 0.0), axis=-1, keepdims=True)
            scale.append(jnp.broadcast_to(w_e, (comb.shape[0], D_EXPERT)))
        hid = hid * jnp.concatenate(scale, axis=-1)
        w2 = w2_ref[...].astype(BF16).reshape(EXPERTS_PER_GROUP * D_EXPERT, D_MODEL)
        ys_ref[...] = _dot(hid.astype(BF16), w2).astype(BF16)

    @pl.when(kind_ref[k] == 0)
    def _():
        ys_ref[...] = jnp.zeros_like(ys_ref)


def _unsort_kernel(ib_ref, dst_ref, y_ref, route_ref, tri_ref, ys_ref, o_ref, ybuf, yasm, sems):
    i = pl.program_id(0)
    n = pl.num_programs(0)
    tm = y_ref.shape[0]
    slot = i % 2

    def fetch(step, sl):
        return [pltpu.make_async_copy(
            ys_ref.at[pl.ds(pl.multiple_of(dst_ref[step * N_GROUPS + g], ROW_ALIGN), tm)],
            ybuf.at[sl, g], sems.at[sl, g]) for g in range(N_GROUPS)]

    @pl.when(i == 0)
    def _():
        yasm[...] = jnp.zeros_like(yasm)
        for c in fetch(0, 0):
            c.start()

    @pl.when(i + 1 < n)
    def _():
        for c in fetch(i + 1, 1 - slot):
            c.start()

    route = route_ref[...]
    lane = lax.broadcasted_iota(jnp.int32, route.shape, 1).astype(F32)
    grp = jnp.sum(jnp.where(lane == GROUP_LANE, route, 0.0), axis=-1, keepdims=True)
    mine = lane == grp
    before = _dot(tri_ref[...], jnp.where(mine, 1.0, 0.0).astype(BF16))
    dest = jnp.sum(jnp.where(mine, before, 0.0), axis=-1, keepdims=True)
    for g in range(N_GROUPS):
        base = jnp.full(dest.shape, ib_ref[i * N_GROUPS + g], jnp.int32).astype(F32)
        dest = dest + jnp.where(grp == g, base, 0.0)
    cols = lax.broadcasted_iota(jnp.int32, (tm, SORT_ROWS), 1).astype(F32)
    perm_t = jnp.where(cols == dest, 1.0, 0.0).astype(BF16)

    for c in fetch(i, slot):
        c.wait()
    for g in range(N_GROUPS):
        yasm[pl.ds(pl.multiple_of(ib_ref[i * N_GROUPS + g], ROW_ALIGN), tm), :] = ybuf[slot, g]
    o_ref[...] = y_ref[...] + _dot(perm_t, yasm[0:SORT_ROWS, :])


def _moe(y, hx, route, grp_t, cnt, w):
    t_total = y.shape[0]
    tm = MOE_TILE
    n_tiles = t_total // tm
    assert SORT_ROWS >= tm + N_GROUPS * (ROW_ALIGN - 1)
    in_base, dst, fill, in_blk, out_blk, grp_of_step, is_expert, n_steps = _moe_plan(
        cnt[:, 0, :N_GROUPS].astype(jnp.int32), t_total)
    cap_rows = n_steps * tm
    lower = jnp.asarray(np.tril(np.ones((tm, tm)), -1), BF16)
    upper = jnp.asarray(np.triu(np.ones((tm, tm)), 1), BF16)
    stage_rows = SORT_ROWS + tm
    any_spec = pl.BlockSpec(memory_space=pl.ANY)

    xs, rs = pl.pallas_call(
        _sort_kernel,
        grid_spec=pltpu.PrefetchScalarGridSpec(
            num_scalar_prefetch=3,
            grid=(n_tiles,),
            in_specs=[pl.BlockSpec((tm, D_MODEL), lambda i, *_: (i, 0)),
                      pl.BlockSpec((tm, LANES), lambda i, *_: (i, 0)),
                      pl.BlockSpec((8, tm), lambda i, *_: (0, i)),
                      pl.BlockSpec((tm, tm), lambda i, *_: (0, 0))],
            out_specs=[any_spec, any_spec],
            scratch_shapes=[pltpu.VMEM((2, stage_rows, D_MODEL), BF16), pltpu.VMEM((2, stage_rows, LANES), F32),
                            pltpu.SemaphoreType.DMA((2, 2, N_GROUPS))],
        ),
        out_shape=[jax.ShapeDtypeStruct((cap_rows, D_MODEL), BF16),
                   jax.ShapeDtypeStruct((cap_rows, LANES), F32)],
        compiler_params=_params(1),
        name="moe_sort",
    )(in_base, dst, fill, hx, route, grp_t, upper)

    step_map = lambda which: (lambda k, ib, ob, grp, kind: ((ib, ob, grp)[which][k], 0))
    wmap = lambda k, ib, ob, grp, kind: (grp[k], 0, 0)
    wspec = pl.BlockSpec((EXPERTS_PER_GROUP, D_MODEL, D_EXPERT), wmap)
    ys = pl.pallas_call(
        _expert_kernel,
        grid_spec=pltpu.PrefetchScalarGridSpec(
            num_scalar_prefetch=4,
            grid=(n_steps,),
            in_specs=[pl.BlockSpec((tm, D_MODEL), step_map(0)), pl.BlockSpec((tm, LANES), step_map(0)),
                      wspec, wspec, pl.BlockSpec((EXPERTS_PER_GROUP, D_EXPERT, D_MODEL), wmap)],
            out_specs=pl.BlockSpec((tm, D_MODEL), step_map(1)),
        ),
        out_shape=jax.ShapeDtypeStruct((cap_rows, D_MODEL), BF16),
        compiler_params=_params(1),
        name="moe_experts",
    )(in_blk, out_blk, grp_of_step, is_expert, xs, rs, w["w1"], w["w3"], w["w2"])

    return pl.pallas_call(
        _unsort_kernel,
        grid_spec=pltpu.PrefetchScalarGridSpec(
            num_scalar_prefetch=2,
            grid=(n_tiles,),
            in_specs=[pl.BlockSpec((tm, D_MODEL), lambda i, *_: (i, 0)),
                      pl.BlockSpec((tm, LANES), lambda i, *_: (i, 0)),
                      pl.BlockSpec((tm, tm), lambda i, *_: (0, 0)),
                      any_spec],
            out_specs=pl.BlockSpec((tm, D_MODEL), lambda i, *_: (i, 0)),
            scratch_shapes=[pltpu.VMEM((2, N_GROUPS, tm, D_MODEL), BF16),
                            pltpu.VMEM((stage_rows, D_MODEL), BF16),
                            pltpu.SemaphoreType.DMA((2, N_GROUPS))],
        ),
        out_shape=jax.ShapeDtypeStruct(y.shape, F32),
        compiler_params=_params(1),
        name="moe_unsort",
    )(in_base, dst, y, route, lower, ys)


def _prep_weights(g_mix, w_in, b_f, q_norm_a, k_norm_a, q_norm_b, k_norm_b, w_pa, w_pb, w_o,
                  g_ffn, w_rg, b_rg, w_re, b_re, w1, w3, w2):
    n_qkv = 6 * WIDTH
    tile = lambda g: jnp.tile(g, N_HEADS)
    w_router = jnp.concatenate(
        [jnp.transpose(w_re, (1, 0, 2)).reshape(D_MODEL, N_EXPERTS), w_rg,
         jnp.zeros((D_MODEL, LANES - N_EXPERTS - N_GROUPS), F32)], axis=1)
    b_router = jnp.concatenate(
        [b_re.reshape(N_EXPERTS), b_rg, jnp.zeros((LANES - N_EXPERTS - N_GROUPS,), F32)])[None, :]
    return {
        "g_mix": g_mix[None, :],
        "w_qkv": w_in[:, :n_qkv].astype(BF16),
        "w_f": jnp.pad(w_in[:, n_qkv:n_qkv + N_HEADS], ((0, 0), (0, LANES - N_HEADS))).astype(BF16),
        "b_f": jnp.pad(b_f, (0, LANES - N_HEADS))[None, :],
        "w_gate": w_in[:, n_qkv + N_HEADS:].astype(BF16),
        "gains": jnp.stack([tile(q_norm_a), tile(k_norm_a), tile(q_norm_b), tile(k_norm_b)]),
        "w_pa": w_pa.astype(BF16), "w_pb": w_pb.astype(BF16), "w_o": w_o.astype(BF16),
        "g_ffn": g_ffn[None, :],
        "w_router_hi": w_router.astype(BF16),
        "w_router_mid": (w_router - w_router.astype(BF16).astype(F32)).astype(BF16),
        "b_router": b_router,
        "w1": w1, "w3": w3, "w2": w2,
    }


def kernel(x_prompt, x_sample, cache_a_k, cache_a_v, cache_b_k, cache_b_v, cache_b_logf, g_mix, w_in, b_f, q_norm_a, k_norm_a, q_norm_b, k_norm_b, rel_bias, w_pa, w_pb, w_o, g_ffn, w_rg, b_rg, w_re, b_re, w1, w3, w2):
    assert g_mix.shape[0] == 1, "single-layer step"
    n_p, seq, _ = x_prompt.shape
    n_s, t_s, _ = x_sample.shape
    a_rows = cache_a_k.shape[2]
    w = _prep_weights(g_mix[0], w_in[0], b_f[0], q_norm_a[0], k_norm_a[0], q_norm_b[0], k_norm_b[0],
                      w_pa[0], w_pb[0], w_o[0], g_ffn[0], w_rg[0], b_rg[0], w_re[0], b_re[0],
                      w1[0], w3[0], w2[0])
    band_tq = 256
    bias_prompt = _band_bias(rel_bias[0], band_tq, 3 * band_tq, 2 * band_tq, True, 3 * band_tq)
    bias_sample = _band_bias(rel_bias[0], t_s, a_rows + LANES, a_rows, False, a_rows + t_s)

    seq_minor = lambda a: jnp.transpose(a, (0, 2, 3, 1))
    seq_major = lambda a: jnp.transpose(a, (0, 3, 1, 2))[None]

    xp = x_prompt.reshape(n_p * seq, D_MODEL)
    (qa, ka, va, ka_t, va_t, qcat, kcat, vb, kb_t, vb_t, logf_t) = _inproj(xp, n_p, w, seq_minor=True)
    o_a = _band_prompt(qa, ka, va, *bias_prompt, n_p)
    o_b = _fox_prompt(qcat, kcat, vb, n_p)
    y_p = _moe(*_merge(xp, o_a, o_b, w), w)

    xs = x_sample.reshape(n_s * t_s, D_MODEL)
    (qa_s, _, _, ka_fs, va_fs, qcat_s, kcat_s, vb_s, kb_fs, vb_fs, logf_s) = _inproj(
        xs, 1, w, seq_minor=False)
    o_as, new_ak, new_av = _band_sample(
        qa_s, ka_fs, va_fs, cache_a_k[0].reshape(n_s, a_rows, WIDTH),
        cache_a_v[0].reshape(n_s, a_rows, WIDTH), *bias_sample)
    lf_cache_t = jnp.transpose(cache_b_logf[0], (0, 2, 1))
    lf_new_t = jnp.transpose(logf_s.reshape(n_s, t_s, N_HEADS), (0, 2, 1))
    o_bs = _fox_sample(qcat_s, kcat_s, vb_s, seq_minor(cache_b_k[0]), seq_minor(cache_b_v[0]),
                       lf_cache_t, lf_new_t)
    y_s = _moe(*_merge(xs, o_as, o_bs, w), w)

    heads = lambda a, n, r: a.reshape(1, n, r, N_HEADS, HEAD_DIM)
    return (y_p.reshape(n_p, seq, D_MODEL), y_s.reshape(n_s, t_s, D_MODEL),
            seq_major(ka_t), seq_major(va_t), seq_major(kb_t), seq_major(vb_t),
            jnp.transpose(logf_t, (0, 2, 1))[None],
            heads(new_ak, n_s, a_rows), heads(new_av, n_s, a_rows),
            heads(kb_fs, n_s, t_s), heads(vb_fs, n_s, t_s), logf_s.reshape(1, n_s, t_s, N_HEADS))
```

```python
import functools

import numpy as np
import jax
import jax.numpy as jnp
from jax import lax
from jax.experimental import pallas as pl
from jax.experimental.pallas import tpu as pltpu

F32 = jnp.float32
BF16 = jnp.bfloat16

D_MODEL = 1024
HEAD_DIM = 64
N_HEADS = 8
WIDTH = N_HEADS * HEAD_DIM
N_PAIRS = N_HEADS // 2
CHUNK = 64
LEFT_CHUNKS = 8
WINDOW_ROWS = LEFT_CHUNKS * CHUNK
MAX_REL = 256
N_GROUPS = 4
EXPERTS_PER_GROUP = 8
N_EXPERTS = N_GROUPS * EXPERTS_PER_GROUP
D_EXPERT = 128
EPS = 1e-6
NEG = -1e30
LOG2E = 1.4426950408889634
LANES = 128
BIAS_PERIOD = 1024
AUG_STRIDE = 8
VMEM_LIMIT = 56 * 1024 * 1024

_NT = (((1,), (1,)), ((), ()))


def _dot(a, b):
    return jnp.dot(a, b, preferred_element_type=F32)


def _dot_nt(a, b):
    return lax.dot_general(a, b, _NT, preferred_element_type=F32)


def _split3(x):
    hi = x.astype(BF16)
    r = x - hi.astype(F32)
    mid = r.astype(BF16)
    lo = (r - mid.astype(F32)).astype(BF16)
    return hi, mid, lo


def _dot3(a_bf, x):
    hi, mid, lo = _split3(x)
    return _dot(a_bf, hi) + _dot(a_bf, mid) + _dot(a_bf, lo)


def _rms(x, g):
    ms = jnp.mean(x * x, axis=-1, keepdims=True)
    return x * lax.rsqrt(ms + EPS) * g


def _params(n_axes):
    return pltpu.CompilerParams(dimension_semantics=("arbitrary",) * n_axes,
                                vmem_limit_bytes=VMEM_LIMIT)


def _const_spec(shape):
    nd = len(shape)
    return pl.BlockSpec(shape, lambda *_: (0,) * nd)


def _inproj_kernel(x_ref, g_ref, wqkv_ref, wf_ref, bf_ref, gains_ref, bd_ref, ltri_ref,
                   selq_ref, selk_ref, oneq_ref, onek_ref,
                   qa_ref, ka_ref, va_ref, kaf_ref, vaf_ref, qcat_ref, kcat_ref, vb_ref,
                   kbf_ref, vbf_ref, logf_ref, carry_ref, *, tiles_per_seq, seq_minor):
    i = pl.program_id(0)
    tm = x_ref.shape[0]

    @pl.when(i % tiles_per_seq == 0)
    def _():
        carry_ref[...] = jnp.zeros_like(carry_ref)

    def put_f32(ref, y, last_tile_only=False):
        if not seq_minor:
            ref[...] = y
        elif last_tile_only:
            @pl.when(i % tiles_per_seq == tiles_per_seq - 1)
            def _():
                ref[...] = y.T.reshape(N_HEADS, HEAD_DIM, tm)
        else:
            ref[...] = y.T.reshape(N_HEADS, HEAD_DIM, tm)

    h = _rms(x_ref[...], g_ref[...]).astype(BF16)

    def seg(s):
        return _dot(h, wqkv_ref[:, s * WIDTH:(s + 1) * WIDTH])

    def headnorm(y, n):
        ss = _dot((y * y).astype(BF16), bd_ref[...])
        return y * lax.rsqrt(ss * (1.0 / HEAD_DIM) + EPS) * gains_ref[n:n + 1, :]

    q_a = headnorm(seg(0), 0)
    qa_ref[...] = (q_a * (HEAD_DIM ** -0.5)).astype(BF16)
    k_a = headnorm(seg(1), 1)
    ka_ref[...] = k_a.astype(BF16)
    put_f32(kaf_ref, k_a, last_tile_only=True)
    v_a = seg(2)
    va_ref[...] = v_a.astype(BF16)
    put_f32(vaf_ref, v_a, last_tile_only=True)

    z = _dot(h, wf_ref[...]) + bf_ref[...]
    logf = jnp.minimum(z, 0.0) - jnp.log(1.0 + jnp.exp(-jnp.abs(z)))
    if seq_minor:
        logf_ref[...] = logf.T[0:N_HEADS, :]
    else:
        logf_ref[...] = logf[:, :N_HEADS]
    c = _dot3(ltri_ref[...], logf) + carry_ref[0:1, :]
    carry_ref[...] = jnp.broadcast_to(c[-1:, :], carry_ref.shape)
    cs = jnp.concatenate(_split3(c * LOG2E), axis=-1)
    q_aug = (_dot(cs, selq_ref[...]) + oneq_ref[...]).astype(BF16)
    k_aug = (_dot(cs, selk_ref[...]) + onek_ref[...]).astype(BF16)

    q_b = (headnorm(seg(3), 2) * (HEAD_DIM ** -0.5 * LOG2E)).astype(BF16)
    k_b = headnorm(seg(4), 3)
    put_f32(kbf_ref, k_b)
    k_b = k_b.astype(BF16)
    v_b = seg(5)
    put_f32(vbf_ref, v_b)
    vb_ref[...] = v_b.astype(BF16)
    for p in range(N_PAIRS):
        cols = slice(p * LANES, (p + 1) * LANES)
        qcat_ref[p, :, 0:LANES] = q_b[:, cols]
        qcat_ref[p, :, LANES:2 * LANES] = q_aug
        kcat_ref[p, :, 0:LANES] = k_b[:, cols]
        kcat_ref[p, :, LANES:2 * LANES] = k_aug


def _aug_constants():
    selq = np.zeros((3 * LANES, LANES), np.float32)
    selk = np.zeros((3 * LANES, LANES), np.float32)
    oneq = np.zeros((1, LANES), np.float32)
    onek = np.zeros((1, LANES), np.float32)
    for h in range(N_HEADS):
        for k in range(3):
            selq[k * LANES + h, AUG_STRIDE * h + k] = 1.0
            selk[k * LANES + h, AUG_STRIDE * h + 3 + k] = -1.0
            oneq[0, AUG_STRIDE * h + 3 + k] = 1.0
            onek[0, AUG_STRIDE * h + k] = 1.0
    return (jnp.asarray(selq, BF16), jnp.asarray(selk, BF16), jnp.asarray(oneq), jnp.asarray(onek))


def _inproj(x, n_seq, w, seq_minor):
    t_total = x.shape[0]
    tm = 512
    n_tiles = t_total // tm
    seq = t_total // n_seq
    tps = seq // tm
    row = lambda i: (i, 0)
    flat = lambda dt: (jax.ShapeDtypeStruct((t_total, WIDTH), dt), pl.BlockSpec((tm, WIDTH), row))
    if seq_minor:
        assert tm == WINDOW_ROWS
        band_f32 = (jax.ShapeDtypeStruct((n_seq, N_HEADS, HEAD_DIM, WINDOW_ROWS), F32),
                    pl.BlockSpec((None, N_HEADS, HEAD_DIM, tm), lambda i: (i // tps, 0, 0, 0)))
        fox_f32 = (jax.ShapeDtypeStruct((n_seq, N_HEADS, HEAD_DIM, seq), F32),
                   pl.BlockSpec((None, N_HEADS, HEAD_DIM, tm), lambda i: (i // tps, 0, 0, i % tps)))
        logf = (jax.ShapeDtypeStruct((n_seq, N_HEADS, seq), F32),
                pl.BlockSpec((None, N_HEADS, tm), lambda i: (i // tps, 0, i % tps)))
    else:
        band_f32 = fox_f32 = flat(F32)
        logf = (jax.ShapeDtypeStruct((t_total, N_HEADS), F32), pl.BlockSpec((tm, N_HEADS), row))
    cat = (jax.ShapeDtypeStruct((n_seq, N_PAIRS, seq, 2 * LANES), BF16),
           pl.BlockSpec((None, N_PAIRS, tm, 2 * LANES), lambda i: (i // tps, 0, i % tps, 0)))
    outs = [flat(BF16), flat(BF16), flat(BF16), band_f32, band_f32, cat, cat, flat(BF16),
            fox_f32, fox_f32, logf]
    bd = jnp.asarray(np.kron(np.eye(N_HEADS), np.ones((HEAD_DIM, HEAD_DIM))), BF16)
    ltri = jnp.asarray(np.tril(np.ones((tm, tm))), BF16)
    consts = [w["g_mix"], w["w_qkv"], w["w_f"], w["b_f"], w["gains"], bd, ltri, *_aug_constants()]
    in_specs = [pl.BlockSpec((tm, D_MODEL), row)] + [_const_spec(c.shape) for c in consts]
    return pl.pallas_call(
        functools.partial(_inproj_kernel, tiles_per_seq=tps, seq_minor=seq_minor),
        grid=(n_tiles,),
        in_specs=in_specs,
        out_specs=[o[1] for o in outs],
        out_shape=[o[0] for o in outs],
        scratch_shapes=[pltpu.VMEM((8, LANES), F32)],
        compiler_params=_params(1),
        name="inproj",
    )(x, *consts)


def _softmax_pv(scores, values):
    chunks = [[s[:, c * LANES:(c + 1) * LANES] for c in range(s.shape[1] // LANES)] for s in scores]
    cmax = None
    for cs in chunks:
        for c in cs:
            cmax = c if cmax is None else jnp.maximum(cmax, c)
    m = jnp.max(cmax, axis=-1, keepdims=True)
    pv = None
    lsum = None
    for cs, v in zip(chunks, values):
        ps = [jnp.exp(c - m) for c in cs]
        for p in ps:
            lsum = p if lsum is None else lsum + p
        o = _dot(jnp.concatenate(ps, axis=-1).astype(BF16), v)
        pv = o if pv is None else pv + o
    return pv, jnp.sum(lsum, axis=-1, keepdims=True)


def _head_mask(hh):
    lane = lax.broadcasted_iota(jnp.int32, (1, LANES), 1)
    return (lane // HEAD_DIM) == hh


def _band_prompt_kernel(q_ref, k0_ref, k1_ref, k2_ref, v0_ref, v1_ref, v2_ref, e_ref, mask_ref, o_ref,
                        bias_ref, *, tq):
    i = pl.program_id(1)

    @pl.when((pl.program_id(0) == 0) & (i == 0))
    def _():
        _build_bias(e_ref, mask_ref, bias_ref)

    k_refs = (k0_ref, k1_ref, k2_ref)
    v_refs = (v0_ref, v1_ref, v2_ref)
    pad = [jnp.where(i - 2 + j >= 0, 0.0, NEG) for j in range(2)] + [0.0]
    for p in range(N_PAIRS):
        cols = slice(p * LANES, (p + 1) * LANES)
        q = q_ref[:, cols]
        ks = [r[:, cols] for r in k_refs]
        vs = [r[:, cols] for r in v_refs]
        out = jnp.zeros((tq, LANES), F32)
        for hh in range(2):
            msk = _head_mask(hh)
            qh = jnp.where(msk, q, jnp.zeros_like(q))
            scores = [_dot_nt(qh, ks[j]) + bias_ref[2 * p + hh, :, j * tq:(j + 1) * tq] + pad[j]
                      for j in range(3)]
            pv, l = _softmax_pv(scores, vs)
            out = jnp.where(msk, pv / l, out)
        o_ref[:, cols] = out.astype(BF16)


def _band_prompt(qa, ka, va, bias_e, bias_mask, n_seq):
    t_total = qa.shape[0]
    tq = 256
    seq = t_total // n_seq
    nq = seq // tq
    qmap = lambda b, i: (b * nq + i, 0)

    def kmap(j):
        return lambda b, i: (b * nq + jnp.maximum(i - 2 + j, 0), 0)

    blk = lambda m: pl.BlockSpec((tq, WIDTH), m)
    return pl.pallas_call(
        functools.partial(_band_prompt_kernel, tq=tq),
        grid=(n_seq, nq),
        in_specs=[blk(qmap)] + [blk(kmap(j)) for j in range(3)] + [blk(kmap(j)) for j in range(3)]
        + [_const_spec(bias_e.shape), _const_spec(bias_mask.shape)],
        out_specs=blk(qmap),
        out_shape=jax.ShapeDtypeStruct((t_total, WIDTH), BF16),
        scratch_shapes=[pltpu.VMEM((N_HEADS,) + bias_mask.shape, F32)],
        compiler_params=_params(2),
        name="band_prompt",
    )(qa, ka, ka, ka, va, va, va, bias_e, bias_mask)


def _band_sample_kernel(q_ref, kc_ref, vc_ref, kn_ref, vn_ref, e_ref, mask_ref, o_ref, ko_ref, vo_ref,
                        bias_ref):
    p_rows = kc_ref.shape[0]
    t = q_ref.shape[0]

    @pl.when(pl.program_id(0) == 0)
    def _():
        _build_bias(e_ref, mask_ref, bias_ref)

    for p in range(N_PAIRS):
        cols = slice(p * LANES, (p + 1) * LANES)
        q = q_ref[:, cols]
        pad = jnp.zeros((LANES - t, LANES), BF16)
        ks = [kc_ref[:, cols].astype(BF16), jnp.concatenate([kn_ref[:, cols].astype(BF16), pad], axis=0)]
        vs = [vc_ref[:, cols].astype(BF16), jnp.concatenate([vn_ref[:, cols].astype(BF16), pad], axis=0)]
        out = jnp.zeros((t, LANES), F32)
        for hh in range(2):
            msk = _head_mask(hh)
            qh = jnp.where(msk, q, jnp.zeros_like(q))
            scores = [_dot_nt(qh, ks[0]) + bias_ref[2 * p + hh, :, 0:p_rows],
                      _dot_nt(qh, ks[1]) + bias_ref[2 * p + hh, :, p_rows:p_rows + LANES]]
            pv, l = _softmax_pv(scores, vs)
            out = jnp.where(msk, pv / l, out)
        o_ref[:, cols] = out.astype(BF16)
    ko_ref[0:p_rows - t, :] = kc_ref[t:p_rows, :]
    ko_ref[p_rows - t:p_rows, :] = kn_ref[...]
    vo_ref[0:p_rows - t, :] = vc_ref[t:p_rows, :]
    vo_ref[p_rows - t:p_rows, :] = vn_ref[...]


def _band_sample(qa, ka_f, va_f, cache_k, cache_v, bias_e, bias_mask):
    n, p_rows, _ = cache_k.shape
    t = qa.shape[0] // n
    row = pl.BlockSpec((t, WIDTH), lambda b: (b, 0))
    cache = pl.BlockSpec((None, p_rows, WIDTH), lambda b: (b, 0, 0))
    return pl.pallas_call(
        _band_sample_kernel,
        grid=(n,),
        in_specs=[row, cache, cache, row, row, _const_spec(bias_e.shape), _const_spec(bias_mask.shape)],
        out_specs=[row, cache, cache],
        out_shape=[jax.ShapeDtypeStruct(qa.shape, BF16),
                   jax.ShapeDtypeStruct(cache_k.shape, F32),
                   jax.ShapeDtypeStruct(cache_v.shape, F32)],
        scratch_shapes=[pltpu.VMEM((N_HEADS,) + bias_mask.shape, F32)],
        compiler_params=_params(1),
        name="band_sample",
    )(qa, cache_k, cache_v, ka_f, va_f, bias_e, bias_mask)


def _band_bias(rel_bias, n_q, n_k, offset, band_mask, valid_k):
    assert n_q + n_k - 1 <= BIAS_PERIOD
    m = np.arange(BIAS_PERIOD)
    w = np.where(m < n_k, m, m - BIAS_PERIOD)
    idx = np.clip(offset - w, -MAX_REL, MAX_REL) + MAX_REL
    e = jnp.take(rel_bias.astype(F32), jnp.asarray(idx), axis=1)
    qi = np.arange(n_q)[:, None]
    kj = np.arange(n_k)[None, :]
    valid = np.broadcast_to(kj < valid_k, (n_q, n_k))
    if band_mask:
        rel_chunk = (qi + offset) // CHUNK - kj // CHUNK
        valid = valid & (rel_chunk >= 0) & (rel_chunk <= LEFT_CHUNKS)
    return e, jnp.asarray(np.where(valid, 0.0, NEG), F32)


def _build_bias(e_ref, mask_ref, bias_sc):
    rows, cols = mask_ref.shape
    for h in range(N_HEADS):
        spread = jnp.broadcast_to(e_ref[h:h + 1, :], (rows, BIAS_PERIOD))
        bias_sc[h] = pltpu.roll(spread, 0, 1, stride=1, stride_axis=0)[:, :cols] + mask_ref[...]


def _aug_head_mask(h_in_pair, pair):
    lane = lax.broadcasted_iota(jnp.int32, (1, 2 * LANES), 1)
    head = 2 * pair + h_in_pair
    in_q = (lane < LANES) & ((lane // HEAD_DIM) == h_in_pair)
    in_aug = (lane >= LANES) & (((lane - LANES) // AUG_STRIDE) == head)
    return in_q | in_aug


def _flash_update(s, v, m_ref, l_ref, acc_ref, h, row_bias=None, v_seq_minor=False):
    chunks = [s[:, c * LANES:(c + 1) * LANES] for c in range(s.shape[1] // LANES)]
    if row_bias is not None:
        chunks = [c + row_bias for c in chunks]
    cmax = chunks[0]
    for c in chunks[1:]:
        cmax = jnp.maximum(cmax, c)
    m_prev = m_ref[h]
    m_new = jnp.maximum(m_prev, jnp.max(cmax, axis=-1, keepdims=True))
    alpha = jnp.exp2(m_prev - m_new)
    ps = [jnp.exp2(c - m_new) for c in chunks]
    lsum = ps[0]
    for p in ps[1:]:
        lsum = lsum + p
    l_ref[h] = alpha * l_ref[h] + lsum
    p = jnp.concatenate(ps, axis=-1).astype(BF16)
    pv = _dot_nt(p, v) if v_seq_minor else _dot(p, v)
    acc_ref[h] = alpha[:, :pv.shape[1]] * acc_ref[h] + pv
    m_ref[h] = m_new


def _flash_result(l_ref, acc_ref, pair):
    outs = [acc_ref[2 * pair + hh] / jnp.sum(l_ref[2 * pair + hh], axis=-1, keepdims=True)
            for hh in range(2)]
    return jnp.where(_head_mask(0), outs[0], outs[1])


def _fox_prompt_kernel(qi_ref, kj_ref, q_ref, k_ref, v_ref, o_ref, qh_sc, m_sc, l_sc, acc_sc, *, tq, tk):
    t = pl.program_id(1)
    i = qi_ref[t]
    j = kj_ref[t]

    @pl.when(j == 0)
    def _():
        for p in range(N_PAIRS):
            q = q_ref[p]
            for hh in range(2):
                qh_sc[2 * p + hh] = jnp.where(_aug_head_mask(hh, p), q, jnp.zeros_like(q))
        m_sc[...] = jnp.full_like(m_sc, NEG)
        l_sc[...] = jnp.zeros_like(l_sc)
        acc_sc[...] = jnp.zeros_like(acc_sc)

    def step(diagonal):
        if diagonal:
            keep = (lax.broadcasted_iota(jnp.int32, (tq, tk), 1)
                    <= lax.broadcasted_iota(jnp.int32, (tq, tk), 0))
        for p in range(N_PAIRS):
            k = k_ref[p]
            v = v_ref[:, p * LANES:(p + 1) * LANES]
            for hh in range(2):
                s = _dot_nt(qh_sc[2 * p + hh], k)
                if diagonal:
                    s = jnp.where(keep, s, NEG)
                _flash_update(s, v, m_sc, l_sc, acc_sc, 2 * p + hh)

    @pl.when(j < i)
    def _():
        step(False)

    @pl.when(j == i)
    def _():
        step(True)
        for p in range(N_PAIRS):
            o_ref[:, p * LANES:(p + 1) * LANES] = _flash_result(l_sc, acc_sc, p).astype(BF16)


def _fox_prompt(qcat, kcat, vb, n_seq):
    seq = qcat.shape[2]
    tq = tk = 1024
    nt = seq // tq
    qi = np.concatenate([np.full(i + 1, i) for i in range(nt)]).astype(np.int32)
    kj = np.concatenate([np.arange(i + 1) for i in range(nt)]).astype(np.int32)
    grid_spec = pltpu.PrefetchScalarGridSpec(
        num_scalar_prefetch=2,
        grid=(n_seq, len(qi)),
        in_specs=[
            pl.BlockSpec((None, N_PAIRS, tq, 2 * LANES), lambda b, t, qi, kj: (b, 0, qi[t], 0)),
            pl.BlockSpec((None, N_PAIRS, tk, 2 * LANES), lambda b, t, qi, kj: (b, 0, kj[t], 0)),
            pl.BlockSpec((tk, WIDTH), lambda b, t, qi, kj: (b * nt + kj[t], 0)),
        ],
        out_specs=pl.BlockSpec((tq, WIDTH), lambda b, t, qi, kj: (b * nt + qi[t], 0)),
        scratch_shapes=[pltpu.VMEM((N_HEADS, tq, 2 * LANES), BF16), pltpu.VMEM((N_HEADS, tq, LANES), F32),
                        pltpu.VMEM((N_HEADS, tq, LANES), F32), pltpu.VMEM((N_HEADS, tq, LANES), F32)],
    )
    return pl.pallas_call(
        functools.partial(_fox_prompt_kernel, tq=tq, tk=tk),
        grid_spec=grid_spec,
        out_shape=jax.ShapeDtypeStruct(vb.shape, BF16),
        compiler_params=_params(2),
        name="fox_prompt",
    )(jnp.asarray(qi), jnp.asarray(kj), qcat, kcat, vb)


def _suffix_sum_exclusive(x):
    n = x.shape[1]
    lane = lax.broadcasted_iota(jnp.int32, x.shape, 1)
    y = jnp.where(lane + 1 < n, pltpu.roll(x, n - 1, axis=1), 0.0)
    shift = 1
    while shift < n:
        y = y + jnp.where(lane + shift < n, pltpu.roll(y, n - shift, axis=1), 0.0)
        shift *= 2
    return y


def _fox_sample_kernel(q_ref, kc_ref, vc_ref, kn_ref, vn_ref, lfc_ref, lfn_ref, u_ref,
                       o_ref, cq_sc, car_sc, m_sc, l_sc, acc_sc, *, n_cache_tiles, sub_keys):
    j = pl.program_id(1)
    t = q_ref.shape[1]

    def cum_new():
        hi, mid, lo = _split3(lfn_ref[...])
        u = u_ref[...]
        return _dot(hi, u) + _dot(mid, u) + _dot(lo, u)

    def q_head(h):
        return q_ref[h // 2, :, (h % 2) * HEAD_DIM:(h % 2 + 1) * HEAD_DIM]

    @pl.when(j == 0)
    def _():
        cn = cum_new() * LOG2E
        eye = (lax.broadcasted_iota(jnp.int32, (t, LANES), 0)
               == lax.broadcasted_iota(jnp.int32, (t, LANES), 1))
        for h in range(N_HEADS):
            col = jnp.sum(jnp.where(eye, jnp.broadcast_to(cn[h:h + 1, :], (t, LANES)), 0.0),
                          axis=-1, keepdims=True)
            cq_sc[h] = jnp.broadcast_to(col, (t, LANES))
        car_sc[...] = jnp.zeros_like(car_sc)
        m_sc[...] = jnp.full_like(m_sc, NEG)
        l_sc[...] = jnp.zeros_like(l_sc)
        acc_sc[...] = jnp.zeros_like(acc_sc)

    @pl.when(j < n_cache_tiles)
    def _():
        lf = lfc_ref[...]
        r = _suffix_sum_exclusive(lf) + car_sc[:, 0:1]
        car_sc[...] = jnp.broadcast_to(r[:, 0:1] + lf[:, 0:1], car_sc.shape)
        r = r * LOG2E
        for sub in range(lf.shape[1] // sub_keys):
            keys = slice(sub * sub_keys, (sub + 1) * sub_keys)
            for h in range(N_HEADS):
                s = _dot(q_head(h), kc_ref[h, :, keys].astype(BF16)) + r[h:h + 1, keys]
                _flash_update(s, vc_ref[h, :, keys].astype(BF16), m_sc, l_sc, acc_sc, h,
                              row_bias=cq_sc[h], v_seq_minor=True)

    @pl.when(j == n_cache_tiles)
    def _():
        cn = cum_new() * LOG2E
        causal = (lax.broadcasted_iota(jnp.int32, (t, LANES), 1)
                  <= lax.broadcasted_iota(jnp.int32, (t, LANES), 0))
        pad = jnp.zeros((LANES - t, HEAD_DIM), BF16)
        for h in range(N_HEADS):
            cols = slice(h * HEAD_DIM, (h + 1) * HEAD_DIM)
            k = jnp.concatenate([kn_ref[h // 2, :, (h % 2) * HEAD_DIM:(h % 2 + 1) * HEAD_DIM], pad], axis=0)
            v = jnp.concatenate([vn_ref[:, cols], pad], axis=0)
            s = _dot_nt(q_head(h), k) + cq_sc[h] - cn[h:h + 1, :]
            _flash_update(jnp.where(causal, s, NEG), v, m_sc, l_sc, acc_sc, h)
            o_ref[:, cols] = (acc_sc[h] / jnp.sum(l_sc[h], axis=-1, keepdims=True)).astype(BF16)


def _fox_sample(qcat, kcat, vb, cache_kt, cache_vt, lf_cache_t, lf_new_t):
    n, _, _, p_rows = cache_kt.shape
    t = vb.shape[0] // n
    tk = 2048
    nct = p_rows // tk
    u = jnp.asarray(np.pad(np.triu(np.ones((t, t))), ((0, 0), (0, LANES - t))), BF16)
    rev = lambda j: jnp.maximum(nct - 1 - j, 0)
    cache = pl.BlockSpec((None, N_HEADS, HEAD_DIM, tk), lambda b, j: (b, 0, 0, rev(j)))
    new_cat = pl.BlockSpec((None, N_PAIRS, t, 2 * LANES), lambda b, j: (0, 0, b, 0))
    return pl.pallas_call(
        functools.partial(_fox_sample_kernel, n_cache_tiles=nct, sub_keys=tk),
        grid=(n, nct + 1),
        in_specs=[
            new_cat, cache, cache, new_cat,
            pl.BlockSpec((t, WIDTH), lambda b, j: (b, 0)),
            pl.BlockSpec((None, N_HEADS, tk), lambda b, j: (b, 0, rev(j))),
            pl.BlockSpec((None, N_HEADS, t), lambda b, j: (b, 0, 0)),
            _const_spec(u.shape),
        ],
        out_specs=pl.BlockSpec((t, WIDTH), lambda b, j: (b, 0)),
        out_shape=jax.ShapeDtypeStruct(vb.shape, BF16),
        scratch_shapes=[pltpu.VMEM((N_HEADS, t, LANES), F32), pltpu.VMEM((N_HEADS, LANES), F32),
                        pltpu.VMEM((N_HEADS, t, LANES), F32), pltpu.VMEM((N_HEADS, t, LANES), F32),
                        pltpu.VMEM((N_HEADS, t, HEAD_DIM), F32)],
        compiler_params=_params(2),
        name="fox_sample",
    )(qcat, cache_kt, cache_vt, kcat, vb, lf_cache_t, lf_new_t, u)


GROUP_LANE = 64
ROW_ALIGN = 16
MOE_TILE = 512
SORT_ROWS = 640


def _route(r):
    lane_i = lax.broadcasted_iota(jnp.int32, r.shape, 1)
    lane = lane_i.astype(F32)
    lane_grp = (lane_i // EXPERTS_PER_GROUP).astype(F32)
    big = float(LANES)
    is_coarse = (lane_i >= N_EXPERTS) & (lane_i < N_EXPERTS + N_GROUPS)
    cm = jnp.where(is_coarse, r, NEG)
    cmax = cm.max(axis=-1, keepdims=True)
    grp = jnp.min(jnp.where(cm == cmax, lane - N_EXPERTS, big), axis=-1, keepdims=True)
    pg_sel = 1.0 / jnp.sum(jnp.exp(cm - cmax), axis=-1, keepdims=True)
    in_grp = (lane_i < N_EXPERTS) & (lane_grp == grp)
    fm = jnp.where(in_grp, r, NEG)
    m1 = fm.max(axis=-1, keepdims=True)
    denom = jnp.sum(jnp.exp(fm - m1), axis=-1, keepdims=True)
    i1 = jnp.min(jnp.where(fm == m1, lane, big), axis=-1, keepdims=True)
    fm2 = jnp.where(lane == i1, NEG, fm)
    m2 = fm2.max(axis=-1, keepdims=True)
    i2 = jnp.min(jnp.where(fm2 == m2, lane, big), axis=-1, keepdims=True)
    p1 = 1.0 / denom
    p2 = jnp.exp(m2 - m1) / denom
    tot = p1 + p2
    comb = (jnp.where(lane == i1, pg_sel * (p1 / tot), 0.0)
            + jnp.where(lane == i2, pg_sel * (p2 / tot), 0.0))
    return comb, grp


def _merge_kernel(x_ref, oa_ref, ob_ref, g_ref, wg_ref, wpa_ref, wpb_ref, wo_ref, gf_ref, wrh_ref, wrm_ref,
                  br_ref, y_ref, hx_ref, route_ref, grp_t_ref, cnt_ref):
    x = x_ref[...]
    h = _rms(x, g_ref[...]).astype(BF16)
    gate = jax.nn.sigmoid(_dot(h, wg_ref[...]))
    mix = (gate[:, :D_MODEL] * _dot(oa_ref[...], wpa_ref[...])
           + gate[:, D_MODEL:] * _dot(ob_ref[...], wpb_ref[...]))
    y = x + _dot(mix.astype(BF16), wo_ref[...])
    y_ref[...] = y

    hx = _rms(y, gf_ref[...])
    hx_ref[...] = hx.astype(BF16)
    h_hi, h_mid, _ = _split3(hx)
    r = _dot(h_hi, wrh_ref[...]) + _dot(h_hi, wrm_ref[...]) + _dot(h_mid, wrh_ref[...]) + br_ref[...]
    comb, grp = _route(r)
    lane = lax.broadcasted_iota(jnp.int32, comb.shape, 1)
    route = jnp.where(lane == GROUP_LANE, grp, comb)
    route_ref[...] = route
    grp_t_ref[...] = route.T[GROUP_LANE:GROUP_LANE + 8, :]
    cnt = jnp.sum(jnp.where(lane.astype(F32) == grp, 1.0, 0.0), axis=0, keepdims=True)
    cnt_ref[...] = jnp.broadcast_to(cnt, cnt_ref.shape)


def _merge(x, oa, ob, w):
    t_total = x.shape[0]
    tm = MOE_TILE
    n_tiles = t_total // tm
    row = lambda n: pl.BlockSpec((tm, n), lambda i: (i, 0))
    consts = [w["g_mix"], w["w_gate"], w["w_pa"], w["w_pb"], w["w_o"], w["g_ffn"], w["w_router_hi"],
              w["w_router_mid"], w["b_router"]]
    return pl.pallas_call(
        _merge_kernel,
        grid=(n_tiles,),
        in_specs=[row(D_MODEL), row(WIDTH), row(WIDTH)] + [_const_spec(c.shape) for c in consts],
        out_specs=[row(D_MODEL), row(D_MODEL), row(LANES), pl.BlockSpec((8, tm), lambda i: (0, i)),
                   pl.BlockSpec((None, 8, LANES), lambda i: (i, 0, 0))],
        out_shape=[jax.ShapeDtypeStruct(x.shape, F32), jax.ShapeDtypeStruct(x.shape, BF16),
                   jax.ShapeDtypeStruct((t_total, LANES), F32), jax.ShapeDtypeStruct((8, t_total), F32),
                   jax.ShapeDtypeStruct((n_tiles, 8, LANES), F32)],
        compiler_params=_params(1),
        name="merge",
    )(x, oa, ob, *consts)


def _plan_kernel(cnt_ref, ib_ref, dst_ref, fill_ref, inblk_ref, outblk_ref, grp_ref, kind_ref, *,
                 n_tiles, n_steps):
    align_bits = ROW_ALIGN.bit_length() - 1
    tile_bits = MOE_TILE.bit_length() - 1
    zero = jnp.int32(0)

    def per_tile(t, tot):
        run = zero
        new_tot = []
        for g in range(N_GROUPS):
            n = ((cnt_ref[t * N_GROUPS + g] + (ROW_ALIGN - 1)) >> align_bits) << align_bits
            ib_ref[t * N_GROUPS + g] = run
            dst_ref[t * N_GROUPS + g] = tot[g]
            run = run + n
            new_tot.append(tot[g] + n)
        return tuple(new_tot)

    tot = lax.fori_loop(0, n_tiles, per_tile, (zero,) * N_GROUPS)
    n_ffn = [(tot[g] + (MOE_TILE - 1)) >> tile_bits for g in range(N_GROUPS)]
    g_base, ends, acc_rows, acc_tiles = [], [], zero, zero
    for g in range(N_GROUPS):
        g_base.append(acc_rows)
        acc_rows = acc_rows + ((n_ffn[g] + 1) << tile_bits)
        acc_tiles = acc_tiles + n_ffn[g] + 1
        ends.append(acc_tiles)

    def add_base(t, carry):
        for g in range(N_GROUPS):
            dst_ref[t * N_GROUPS + g] = dst_ref[t * N_GROUPS + g] + g_base[g]
        return carry

    lax.fori_loop(0, n_tiles, add_base, zero)
    for g in range(N_GROUPS):
        fill_ref[g] = g_base[g] + tot[g]
        fill_ref[N_GROUPS + g] = g_base[g] + (n_ffn[g] << tile_bits)
    fill_ref[2 * N_GROUPS] = ends[-1]
    first_blk = g_base[N_GROUPS - 1]
    for g in reversed(range(N_GROUPS - 1)):
        first_blk = jnp.where(n_ffn[g] > 0, g_base[g], first_blk)
    first_blk = first_blk >> tile_bits

    def pick(vals, g):
        out = vals[N_GROUPS - 1]
        for i in reversed(range(N_GROUPS - 1)):
            out = jnp.where(g == i, vals[i], out)
        return out

    def per_step(k, carry):
        kk = jnp.minimum(k, ends[-1] - 1)
        g = zero
        for i in range(N_GROUPS - 1):
            g = g + jnp.where(kk >= ends[i], 1, 0)
        j = kk - pick([zero] + ends[:-1], g)
        is_expert = j < pick(n_ffn, g)
        out_blk = jnp.where(k < ends[-1], (pick(g_base, g) >> tile_bits) + j, k)
        outblk_ref[k] = out_blk
        inblk_ref[k] = jnp.where(is_expert, out_blk, first_blk)
        grp_ref[k] = g
        kind_ref[k] = jnp.where(is_expert, 1, 0)
        return carry

    lax.fori_loop(0, n_steps, per_step, zero)


def _moe_plan(cnt, t_total):
    n_tiles = t_total // MOE_TILE
    n_steps = (n_tiles + 2 * N_GROUPS + (N_GROUPS * (ROW_ALIGN - 1) * n_tiles + MOE_TILE - 1) // MOE_TILE)
    smem = pl.BlockSpec(memory_space=pltpu.SMEM)
    sizes = [n_tiles * N_GROUPS, n_tiles * N_GROUPS, 2 * N_GROUPS + 1, n_steps, n_steps, n_steps, n_steps]
    outs = pl.pallas_call(
        functools.partial(_plan_kernel, n_tiles=n_tiles, n_steps=n_steps),
        grid_spec=pltpu.PrefetchScalarGridSpec(num_scalar_prefetch=1, grid=(1,), in_specs=[],
                                               out_specs=[smem] * len(sizes)),
        out_shape=[jax.ShapeDtypeStruct((n,), jnp.int32) for n in sizes],
        compiler_params=_params(1),
        name="moe_plan",
    )(cnt.reshape(-1))
    return (*outs, n_steps)


def _sort_kernel(ib_ref, dst_ref, fill_ref, hx_ref, route_ref, grp_t_ref, tri_ref,
                 xs_ref, rs_ref, xsort, rsort, sems):
    i = pl.program_id(0)
    n = pl.num_programs(0)
    tm = hx_ref.shape[0]
    slot = i % 2

    def block_copies(src, dst, sl, g):
        src = pl.multiple_of(src, ROW_ALIGN)
        dst = pl.multiple_of(dst, ROW_ALIGN)
        return [pltpu.make_async_copy(xsort.at[sl, pl.ds(src, tm)], xs_ref.at[pl.ds(dst, tm)], sems.at[sl, 0, g]),
                pltpu.make_async_copy(rsort.at[sl, pl.ds(src, tm)], rs_ref.at[pl.ds(dst, tm)], sems.at[sl, 1, g])]

    def copies(src_rows, dst_rows, sl):
        return [c for g in range(N_GROUPS) for c in block_copies(src_rows(g), dst_rows(g), sl, g)]

    def run(cs):
        for c in cs:
            c.start()
        for c in cs:
            c.wait()

    def step_copies(step, sl):
        return copies(lambda g: ib_ref[step * N_GROUPS + g], lambda g: dst_ref[step * N_GROUPS + g], sl)

    @pl.when(i == 0)
    def _():
        for sl in range(2):
            xsort[sl, SORT_ROWS:, :] = jnp.zeros((xsort.shape[1] - SORT_ROWS, D_MODEL), BF16)
            rsort[sl, SORT_ROWS:, :] = jnp.zeros((rsort.shape[1] - SORT_ROWS, LANES), F32)

    g_row = grp_t_ref[0:1, :]
    sub = lax.broadcasted_iota(jnp.int32, (8, tm), 0).astype(F32)
    mine = sub == g_row
    before = _dot(jnp.where(mine, 1.0, 0.0).astype(BF16), tri_ref[...])
    dest = jnp.sum(jnp.where(mine, before, 0.0), axis=0, keepdims=True)
    for g in range(N_GROUPS):
        base = jnp.full(dest.shape, ib_ref[i * N_GROUPS + g], jnp.int32).astype(F32)
        dest = dest + jnp.where(g_row == g, base, 0.0)
    rows = lax.broadcasted_iota(jnp.int32, (SORT_ROWS, tm), 0).astype(F32)
    perm = jnp.where(rows == dest, 1.0, 0.0).astype(BF16)
    xsort[slot, 0:SORT_ROWS, :] = _dot(perm, hx_ref[...]).astype(BF16)
    rsort[slot, 0:SORT_ROWS, :] = _dot3(perm, route_ref[...])

    @pl.when(i > 0)
    def _():
        for c in step_copies(i - 1, 1 - slot):
            c.wait()

    for c in step_copies(i, slot):
        c.start()

    @pl.when(i == n - 1)
    def _():
        for c in step_copies(i, slot):
            c.wait()
        run(copies(lambda g: SORT_ROWS, lambda g: fill_ref[g], slot))
        run(copies(lambda g: SORT_ROWS, lambda g: fill_ref[N_GROUPS + g], slot))

        @pl.loop(fill_ref[2 * N_GROUPS], xs_ref.shape[0] // tm)
        def _(b):
            run(block_copies(SORT_ROWS, b * tm, slot, 0))


def _expert_kernel(ib_ref, ob_ref, grp_ref, kind_ref, xs_ref, rs_ref, w1_ref, w3_ref, w2_ref, ys_ref):
    del ib_ref, ob_ref
    k = pl.program_id(0)

    @pl.when(kind_ref[k] == 1)
    def _():
        x = xs_ref[...]
        side = lambda w_ref: jnp.concatenate([w_ref[e].astype(BF16) for e in range(EXPERTS_PER_GROUP)], axis=1)
        hid = jax.nn.silu(_dot(x, side(w1_ref))) * _dot(x, side(w3_ref))
        comb = rs_ref[...]
        lane = lax.broadcasted_iota(jnp.int32, comb.shape, 1)
        first = grp_ref[k] * EXPERTS_PER_GROUP
        scale = []
        for e in range(EXPERTS_PER_GROUP):
            w_e = jnp.sum(jnp.where(lane == first + e, comb, 0.0), axis=-1, keepdims=True)
            scale.append(jnp.broadcast_to(w_e, (comb.shape[0], D_EXPERT)))
        hid = hid * jnp.concatenate(scale, axis=-1)
        w2 = w2_ref[...].astype(BF16).reshape(EXPERTS_PER_GROUP * D_EXPERT, D_MODEL)
        ys_ref[...] = _dot(hid.astype(BF16), w2).astype(BF16)

    @pl.when(kind_ref[k] == 0)
    def _():
        ys_ref[...] = jnp.zeros_like(ys_ref)


def _unsort_kernel(ib_ref, dst_ref, y_ref, route_ref, tri_ref, ys_ref, o_ref, ybuf, yasm, sems):
    i = pl.program_id(0)
    n = pl.num_programs(0)
    tm = y_ref.shape[0]
    slot = i % 2

    def fetch(step, sl):
        return [pltpu.make_async_copy(
            ys_ref.at[pl.ds(pl.multiple_of(dst_ref[step * N_GROUPS + g], ROW_ALIGN), tm)],
            ybuf.at[sl, g], sems.at[sl, g]) for g in range(N_GROUPS)]

    @pl.when(i == 0)
    def _():
        yasm[...] = jnp.zeros_like(yasm)
        for c in fetch(0, 0):
            c.start()

    @pl.when(i + 1 < n)
    def _():
        for c in fetch(i + 1, 1 - slot):
            c.start()

    route = route_ref[...]
    lane = lax.broadcasted_iota(jnp.int32, route.shape, 1).astype(F32)
    grp = jnp.sum(jnp.where(lane == GROUP_LANE, route, 0.0), axis=-1, keepdims=True)
    mine = lane == grp
    before = _dot(tri_ref[...], jnp.where(mine, 1.0, 0.0).astype(BF16))
    dest = jnp.sum(jnp.where(mine, before, 0.0), axis=-1, keepdims=True)
    for g in range(N_GROUPS):
        base = jnp.full(dest.shape, ib_ref[i * N_GROUPS + g], jnp.int32).astype(F32)
        dest = dest + jnp.where(grp == g, base, 0.0)
    cols = lax.broadcasted_iota(jnp.int32, (tm, SORT_ROWS), 1).astype(F32)
    perm_t = jnp.where(cols == dest, 1.0, 0.0).astype(BF16)

    for c in fetch(i, slot):
        c.wait()
    for g in range(N_GROUPS):
        yasm[pl.ds(pl.multiple_of(ib_ref[i * N_GROUPS + g], ROW_ALIGN), tm), :] = ybuf[slot, g]
    o_ref[...] = y_ref[...] + _dot(perm_t, yasm[0:SORT_ROWS, :])


def _moe(y, hx, route, grp_t, cnt, w):
    t_total = y.shape[0]
    tm = MOE_TILE
    n_tiles = t_total // tm
    assert SORT_ROWS >= tm + N_GROUPS * (ROW_ALIGN - 1)
    in_base, dst, fill, in_blk, out_blk, grp_of_step, is_expert, n_steps = _moe_plan(
        cnt[:, 0, :N_GROUPS].astype(jnp.int32), t_total)
    cap_rows = n_steps * tm
    lower = jnp.asarray(np.tril(np.ones((tm, tm)), -1), BF16)
    upper = jnp.asarray(np.triu(np.ones((tm, tm)), 1), BF16)
    stage_rows = SORT_ROWS + tm
    any_spec = pl.BlockSpec(memory_space=pl.ANY)

    xs, rs = pl.pallas_call(
        _sort_kernel,
        grid_spec=pltpu.PrefetchScalarGridSpec(
            num_scalar_prefetch=3,
            grid=(n_tiles,),
            in_specs=[pl.BlockSpec((tm, D_MODEL), lambda i, *_: (i, 0)),
                      pl.BlockSpec((tm, LANES), lambda i, *_: (i, 0)),
                      pl.BlockSpec((8, tm), lambda i, *_: (0, i)),
                      pl.BlockSpec((tm, tm), lambda i, *_: (0, 0))],
            out_specs=[any_spec, any_spec],
            scratch_shapes=[pltpu.VMEM((2, stage_rows, D_MODEL), BF16), pltpu.VMEM((2, stage_rows, LANES), F32),
                            pltpu.SemaphoreType.DMA((2, 2, N_GROUPS))],
        ),
        out_shape=[jax.ShapeDtypeStruct((cap_rows, D_MODEL), BF16),
                   jax.ShapeDtypeStruct((cap_rows, LANES), F32)],
        compiler_params=_params(1),
        name="moe_sort",
    )(in_base, dst, fill, hx, route, grp_t, upper)

    step_map = lambda which: (lambda k, ib, ob, grp, kind: ((ib, ob, grp)[which][k], 0))
    wmap = lambda k, ib, ob, grp, kind: (grp[k], 0, 0)
    wspec = pl.BlockSpec((EXPERTS_PER_GROUP, D_MODEL, D_EXPERT), wmap)
    ys = pl.pallas_call(
        _expert_kernel,
        grid_spec=pltpu.PrefetchScalarGridSpec(
            num_scalar_prefetch=4,
            grid=(n_steps,),
            in_specs=[pl.BlockSpec((tm, D_MODEL), step_map(0)), pl.BlockSpec((tm, LANES), step_map(0)),
                      wspec, wspec, pl.BlockSpec((EXPERTS_PER_GROUP, D_EXPERT, D_MODEL), wmap)],
            out_specs=pl.BlockSpec((tm, D_MODEL), step_map(1)),
        ),
        out_shape=jax.ShapeDtypeStruct((cap_rows, D_MODEL), BF16),
        compiler_params=_params(1),
        name="moe_experts",
    )(in_blk, out_blk, grp_of_step, is_expert, xs, rs, w["w1"], w["w3"], w["w2"])

    return pl.pallas_call(
        _unsort_kernel,
        grid_spec=pltpu.PrefetchScalarGridSpec(
            num_scalar_prefetch=2,
            grid=(n_tiles,),
            in_specs=[pl.BlockSpec((tm, D_MODEL), lambda i, *_: (i, 0)),
                      pl.BlockSpec((tm, LANES), lambda i, *_: (i, 0)),
                      pl.BlockSpec((tm, tm), lambda i, *_: (0, 0)),
                      any_spec],
            out_specs=pl.BlockSpec((tm, D_MODEL), lambda i, *_: (i, 0)),
            scratch_shapes=[pltpu.VMEM((2, N_GROUPS, tm, D_MODEL), BF16),
                            pltpu.VMEM((stage_rows, D_MODEL), BF16),
                            pltpu.SemaphoreType.DMA((2, N_GROUPS))],
        ),
        out_shape=jax.ShapeDtypeStruct(y.shape, F32),
        compiler_params=_params(1),
        name="moe_unsort",
    )(in_base, dst, y, route, lower, ys)


def _prep_weights(g_mix, w_in, b_f, q_norm_a, k_norm_a, q_norm_b, k_norm_b, w_pa, w_pb, w_o,
                  g_ffn, w_rg, b_rg, w_re, b_re, w1, w3, w2):
    n_qkv = 6 * WIDTH
    tile = lambda g: jnp.tile(g, N_HEADS)
    w_router = jnp.concatenate(
        [jnp.transpose(w_re, (1, 0, 2)).reshape(D_MODEL, N_EXPERTS), w_rg,
         jnp.zeros((D_MODEL, LANES - N_EXPERTS - N_GROUPS), F32)], axis=1)
    b_router = jnp.concatenate(
        [b_re.reshape(N_EXPERTS), b_rg, jnp.zeros((LANES - N_EXPERTS - N_GROUPS,), F32)])[None, :]
    return {
        "g_mix": g_mix[None, :],
        "w_qkv": w_in[:, :n_qkv].astype(BF16),
        "w_f": jnp.pad(w_in[:, n_qkv:n_qkv + N_HEADS], ((0, 0), (0, LANES - N_HEADS))).astype(BF16),
        "b_f": jnp.pad(b_f, (0, LANES - N_HEADS))[None, :],
        "w_gate": w_in[:, n_qkv + N_HEADS:].astype(BF16),
        "gains": jnp.stack([tile(q_norm_a), tile(k_norm_a), tile(q_norm_b), tile(k_norm_b)]),
        "w_pa": w_pa.astype(BF16), "w_pb": w_pb.astype(BF16), "w_o": w_o.astype(BF16),
        "g_ffn": g_ffn[None, :],
        "w_router_hi": w_router.astype(BF16),
        "w_router_mid": (w_router - w_router.astype(BF16).astype(F32)).astype(BF16),
        "b_router": b_router,
        "w1": w1, "w3": w3, "w2": w2,
    }


def kernel(x_prompt, x_sample, cache_a_k, cache_a_v, cache_b_k, cache_b_v, cache_b_logf, g_mix, w_in, b_f, q_norm_a, k_norm_a, q_norm_b, k_norm_b, rel_bias, w_pa, w_pb, w_o, g_ffn, w_rg, b_rg, w_re, b_re, w1, w3, w2):
    assert g_mix.shape[0] == 1, "single-layer step"
    n_p, seq, _ = x_prompt.shape
    n_s, t_s, _ = x_sample.shape
    a_rows = cache_a_k.shape[2]
    w = _prep_weights(g_mix[0], w_in[0], b_f[0], q_norm_a[0], k_norm_a[0], q_norm_b[0], k_norm_b[0],
                      w_pa[0], w_pb[0], w_o[0], g_ffn[0], w_rg[0], b_rg[0], w_re[0], b_re[0],
                      w1[0], w3[0], w2[0])
    band_tq = 256
    bias_prompt = _band_bias(rel_bias[0], band_tq, 3 * band_tq, 2 * band_tq, True, 3 * band_tq)
    bias_sample = _band_bias(rel_bias[0], t_s, a_rows + LANES, a_rows, False, a_rows + t_s)

    seq_minor = lambda a: jnp.transpose(a, (0, 2, 3, 1))
    seq_major = lambda a: jnp.transpose(a, (0, 3, 1, 2))[None]

    xp = x_prompt.reshape(n_p * seq, D_MODEL)
    (qa, ka, va, ka_t, va_t, qcat, kcat, vb, kb_t, vb_t, logf_t) = _inproj(xp, n_p, w, seq_minor=True)
    o_a = _band_prompt(qa, ka, va, *bias_prompt, n_p)
    o_b = _fox_prompt(qcat, kcat, vb, n_p)
    y_p = _moe(*_merge(xp, o_a, o_b, w), w)

    xs = x_sample.reshape(n_s * t_s, D_MODEL)
    (qa_s, _, _, ka_fs, va_fs, qcat_s, kcat_s, vb_s, kb_fs, vb_fs, logf_s) = _inproj(
        xs, 1, w, seq_minor=False)
    o_as, new_ak, new_av = _band_sample(
        qa_s, ka_fs, va_fs, cache_a_k[0].reshape(n_s, a_rows, WIDTH),
        cache_a_v[0].reshape(n_s, a_rows, WIDTH), *bias_sample)
    lf_cache_t = jnp.transpose(cache_b_logf[0], (0, 2, 1))
    lf_new_t = jnp.transpose(logf_s.reshape(n_s, t_s, N_HEADS), (0, 2, 1))
    o_bs = _fox_sample(qcat_s, kcat_s, vb_s, seq_minor(cache_b_k[0]), seq_minor(cache_b_v[0]),
                       lf_cache_t, lf_new_t)
    y_s = _moe(*_merge(xs, o_as, o_bs, w), w)

    heads = lambda a, n, r: a.reshape(1, n, r, N_HEADS, HEAD_DIM)
    return (y_p.reshape(n_p, seq, D_MODEL), y_s.reshape(n_s, t_s, D_MODEL),
            seq_major(ka_t), seq_major(va_t), seq_major(kb_t), seq_major(vb_t),
            jnp.transpose(logf_t, (0, 2, 1))[None],
            heads(new_ak, n_s, a_rows), heads(new_av, n_s, a_rows),
            heads(kb_fs, n_s, t_s), heads(vb_fs, n_s, t_s), logf_s.reshape(1, n_s, t_s, N_HEADS))
```

```python
import functools

import numpy as np
import jax
import jax.numpy as jnp
from jax import lax
from jax.experimental import pallas as pl
from jax.experimental.pallas import tpu as pltpu

F32 = jnp.float32
BF16 = jnp.bfloat16

D_MODEL = 1024
HEAD_DIM = 64
N_HEADS = 8
WIDTH = N_HEADS * HEAD_DIM
N_PAIRS = N_HEADS // 2
CHUNK = 64
LEFT_CHUNKS = 8
WINDOW_ROWS = LEFT_CHUNKS * CHUNK
MAX_REL = 256
N_GROUPS = 4
EXPERTS_PER_GROUP = 8
N_EXPERTS = N_GROUPS * EXPERTS_PER_GROUP
D_EXPERT = 128
EPS = 1e-6
NEG = -1e30
LOG2E = 1.4426950408889634
LANES = 128
BIAS_PERIOD = 1024
AUG_STRIDE = 8
VMEM_LIMIT = 56 * 1024 * 1024

_NT = (((1,), (1,)), ((), ()))


def _dot(a, b):
    return jnp.dot(a, b, preferred_element_type=F32)


def _dot_nt(a, b):
    return lax.dot_general(a, b, _NT, preferred_element_type=F32)


def _split3(x):
    hi = x.astype(BF16)
    r = x - hi.astype(F32)
    mid = r.astype(BF16)
    lo = (r - mid.astype(F32)).astype(BF16)
    return hi, mid, lo


def _dot3(a_bf, x):
    hi, mid, lo = _split3(x)
    return _dot(a_bf, hi) + _dot(a_bf, mid) + _dot(a_bf, lo)


def _rms(x, g):
    ms = jnp.mean(x * x, axis=-1, keepdims=True)
    return x * lax.rsqrt(ms + EPS) * g


def _params(n_axes):
    return pltpu.CompilerParams(dimension_semantics=("arbitrary",) * n_axes,
                                vmem_limit_bytes=VMEM_LIMIT)


def _const_spec(shape):
    nd = len(shape)
    return pl.BlockSpec(shape, lambda *_: (0,) * nd)


def _inproj_kernel(x_ref, g_ref, wqkv_ref, wf_ref, bf_ref, gains_ref, bd_ref, ltri_ref,
                   selq_ref, selk_ref, oneq_ref, onek_ref,
                   qa_ref, ka_ref, va_ref, kaf_ref, vaf_ref, qcat_ref, kcat_ref, vb_ref,
                   kbf_ref, vbf_ref, logf_ref, carry_ref, *, tiles_per_seq, seq_minor):
    i = pl.program_id(0)
    tm = x_ref.shape[0]

    @pl.when(i % tiles_per_seq == 0)
    def _():
        carry_ref[...] = jnp.zeros_like(carry_ref)

    def put_f32(ref, y, last_tile_only=False):
        if not seq_minor:
            ref[...] = y
        elif last_tile_only:
            @pl.when(i % tiles_per_seq == tiles_per_seq - 1)
            def _():
                ref[...] = y.T.reshape(N_HEADS, HEAD_DIM, tm)
        else:
            ref[...] = y.T.reshape(N_HEADS, HEAD_DIM, tm)

    h = _rms(x_ref[...], g_ref[...]).astype(BF16)

    def seg(s):
        return _dot(h, wqkv_ref[:, s * WIDTH:(s + 1) * WIDTH])

    def headnorm(y, n):
        ss = _dot((y * y).astype(BF16), bd_ref[...])
        return y * lax.rsqrt(ss * (1.0 / HEAD_DIM) + EPS) * gains_ref[n:n + 1, :]

    q_a = headnorm(seg(0), 0)
    qa_ref[...] = (q_a * (HEAD_DIM ** -0.5)).astype(BF16)
    k_a = headnorm(seg(1), 1)
    ka_ref[...] = k_a.astype(BF16)
    put_f32(kaf_ref, k_a, last_tile_only=True)
    v_a = seg(2)
    va_ref[...] = v_a.astype(BF16)
    put_f32(vaf_ref, v_a, last_tile_only=True)

    z = _dot(h, wf_ref[...]) + bf_ref[...]
    logf = jnp.minimum(z, 0.0) - jnp.log(1.0 + jnp.exp(-jnp.abs(z)))
    if seq_minor:
        logf_ref[...] = logf.T[0:N_HEADS, :]
    else:
        logf_ref[...] = logf[:, :N_HEADS]
    c = _dot3(ltri_ref[...], logf) + carry_ref[0:1, :]
    carry_ref[...] = jnp.broadcast_to(c[-1:, :], carry_ref.shape)
    cs = jnp.concatenate(_split3(c * LOG2E), axis=-1)
    q_aug = (_dot(cs, selq_ref[...]) + oneq_ref[...]).astype(BF16)
    k_aug = (_dot(cs, selk_ref[...]) + onek_ref[...]).astype(BF16)

    q_b = (headnorm(seg(3), 2) * (HEAD_DIM ** -0.5 * LOG2E)).astype(BF16)
    k_b = headnorm(seg(4), 3)
    put_f32(kbf_ref, k_b)
    k_b = k_b.astype(BF16)
    v_b = seg(5)
    put_f32(vbf_ref, v_b)
    vb_ref[...] = v_b.astype(BF16)
    for p in range(N_PAIRS):
        cols = slice(p * LANES, (p + 1) * LANES)
        qcat_ref[p, :, 0:LANES] = q_b[:, cols]
        qcat_ref[p, :, LANES:2 * LANES] = q_aug
        kcat_ref[p, :, 0:LANES] = k_b[:, cols]
        kcat_ref[p, :, LANES:2 * LANES] = k_aug


def _aug_constants():
    selq = np.zeros((3 * LANES, LANES), np.float32)
    selk = np.zeros((3 * LANES, LANES), np.float32)
    oneq = np.zeros((1, LANES), np.float32)
    onek = np.zeros((1, LANES), np.float32)
    for h in range(N_HEADS):
        for k in range(3):
            selq[k * LANES + h, AUG_STRIDE * h + k] = 1.0
            selk[k * LANES + h, AUG_STRIDE * h + 3 + k] = -1.0
            oneq[0, AUG_STRIDE * h + 3 + k] = 1.0
            onek[0, AUG_STRIDE * h + k] = 1.0
    return (jnp.asarray(selq, BF16), jnp.asarray(selk, BF16), jnp.asarray(oneq), jnp.asarray(onek))


def _inproj(x, n_seq, w, seq_minor):
    t_total = x.shape[0]
    tm = 512
    n_tiles = t_total // tm
    seq = t_total // n_seq
    tps = seq // tm
    row = lambda i: (i, 0)
    flat = lambda dt: (jax.ShapeDtypeStruct((t_total, WIDTH), dt), pl.BlockSpec((tm, WIDTH), row))
    if seq_minor:
        assert tm == WINDOW_ROWS
        band_f32 = (jax.ShapeDtypeStruct((n_seq, N_HEADS, HEAD_DIM, WINDOW_ROWS), F32),
                    pl.BlockSpec((None, N_HEADS, HEAD_DIM, tm), lambda i: (i // tps, 0, 0, 0)))
        fox_f32 = (jax.ShapeDtypeStruct((n_seq, N_HEADS, HEAD_DIM, seq), F32),
                   pl.BlockSpec((None, N_HEADS, HEAD_DIM, tm), lambda i: (i // tps, 0, 0, i % tps)))
        logf = (jax.ShapeDtypeStruct((n_seq, N_HEADS, seq), F32),
                pl.BlockSpec((None, N_HEADS, tm), lambda i: (i // tps, 0, i % tps)))
    else:
        band_f32 = fox_f32 = flat(F32)
        logf = (jax.ShapeDtypeStruct((t_total, N_HEADS), F32), pl.BlockSpec((tm, N_HEADS), row))
    cat = (jax.ShapeDtypeStruct((n_seq, N_PAIRS, seq, 2 * LANES), BF16),
           pl.BlockSpec((None, N_PAIRS, tm, 2 * LANES), lambda i: (i // tps, 0, i % tps, 0)))
    outs = [flat(BF16), flat(BF16), flat(BF16), band_f32, band_f32, cat, cat, flat(BF16),
            fox_f32, fox_f32, logf]
    bd = jnp.asarray(np.kron(np.eye(N_HEADS), np.ones((HEAD_DIM, HEAD_DIM))), BF16)
    ltri = jnp.asarray(np.tril(np.ones((tm, tm))), BF16)
    consts = [w["g_mix"], w["w_qkv"], w["w_f"], w["b_f"], w["gains"], bd, ltri, *_aug_constants()]
    in_specs = [pl.BlockSpec((tm, D_MODEL), row)] + [_const_spec(c.shape) for c in consts]
    return pl.pallas_call(
        functools.partial(_inproj_kernel, tiles_per_seq=tps, seq_minor=seq_minor),
        grid=(n_tiles,),
        in_specs=in_specs,
        out_specs=[o[1] for o in outs],
        out_shape=[o[0] for o in outs],
        scratch_shapes=[pltpu.VMEM((8, LANES), F32)],
        compiler_params=_params(1),
        name="inproj",
    )(x, *consts)


def _softmax_pv(scores, values):
    chunks = [[s[:, c * LANES:(c + 1) * LANES] for c in range(s.shape[1] // LANES)] for s in scores]
    cmax = None
    for cs in chunks:
        for c in cs:
            cmax = c if cmax is None else jnp.maximum(cmax, c)
    m = jnp.max(cmax, axis=-1, keepdims=True)
    pv = None
    lsum = None
    for cs, v in zip(chunks, values):
        ps = [jnp.exp(c - m) for c in cs]
        for p in ps:
            lsum = p if lsum is None else lsum + p
        o = _dot(jnp.concatenate(ps, axis=-1).astype(BF16), v)
        pv = o if pv is None else pv + o
    return pv, jnp.sum(lsum, axis=-1, keepdims=True)


def _head_mask(hh):
    lane = lax.broadcasted_iota(jnp.int32, (1, LANES), 1)
    return (lane // HEAD_DIM) == hh


def _band_prompt_kernel(q_ref, k0_ref, k1_ref, k2_ref, v0_ref, v1_ref, v2_ref, e_ref, mask_ref, o_ref,
                        bias_ref, *, tq):
    i = pl.program_id(1)

    @pl.when((pl.program_id(0) == 0) & (i == 0))
    def _():
        _build_bias(e_ref, mask_ref, bias_ref)

    k_refs = (k0_ref, k1_ref, k2_ref)
    v_refs = (v0_ref, v1_ref, v2_ref)
    pad = [jnp.where(i - 2 + j >= 0, 0.0, NEG) for j in range(2)] + [0.0]
    for p in range(N_PAIRS):
        cols = slice(p * LANES, (p + 1) * LANES)
        q = q_ref[:, cols]
        ks = [r[:, cols] for r in k_refs]
        vs = [r[:, cols] for r in v_refs]
        out = jnp.zeros((tq, LANES), F32)
        for hh in range(2):
            msk = _head_mask(hh)
            qh = jnp.where(msk, q, jnp.zeros_like(q))
            scores = [_dot_nt(qh, ks[j]) + bias_ref[2 * p + hh, :, j * tq:(j + 1) * tq] + pad[j]
                      for j in range(3)]
            pv, l = _softmax_pv(scores, vs)
            out = jnp.where(msk, pv / l, out)
        o_ref[:, cols] = out.astype(BF16)


def _band_prompt(qa, ka, va, bias_e, bias_mask, n_seq):
    t_total = qa.shape[0]
    tq = 256
    seq = t_total // n_seq
    nq = seq // tq
    qmap = lambda b, i: (b * nq + i, 0)

    def kmap(j):
        return lambda b, i: (b * nq + jnp.maximum(i - 2 + j, 0), 0)

    blk = lambda m: pl.BlockSpec((tq, WIDTH), m)
    return pl.pallas_call(
        functools.partial(_band_prompt_kernel, tq=tq),
        grid=(n_seq, nq),
        in_specs=[blk(qmap)] + [blk(kmap(j)) for j in range(3)] + [blk(kmap(j)) for j in range(3)]
        + [_const_spec(bias_e.shape), _const_spec(bias_mask.shape)],
        out_specs=blk(qmap),
        out_shape=jax.ShapeDtypeStruct((t_total, WIDTH), BF16),
        scratch_shapes=[pltpu.VMEM((N_HEADS,) + bias_mask.shape, F32)],
        compiler_params=_params(2),
        name="band_prompt",
    )(qa, ka, ka, ka, va, va, va, bias_e, bias_mask)


def _band_sample_kernel(q_ref, kc_ref, vc_ref, kn_ref, vn_ref, e_ref, mask_ref, o_ref, ko_ref, vo_ref,
                        bias_ref):
    p_rows = kc_ref.shape[0]
    t = q_ref.shape[0]

    @pl.when(pl.program_id(0) == 0)
    def _():
        _build_bias(e_ref, mask_ref, bias_ref)

    for p in range(N_PAIRS):
        cols = slice(p * LANES, (p + 1) * LANES)
        q = q_ref[:, cols]
        pad = jnp.zeros((LANES - t, LANES), BF16)
        ks = [kc_ref[:, cols].astype(BF16), jnp.concatenate([kn_ref[:, cols].astype(BF16), pad], axis=0)]
        vs = [vc_ref[:, cols].astype(BF16), jnp.concatenate([vn_ref[:, cols].astype(BF16), pad], axis=0)]
        out = jnp.zeros((t, LANES), F32)
        for hh in range(2):
            msk = _head_mask(hh)
            qh = jnp.where(msk, q, jnp.zeros_like(q))
            scores = [_dot_nt(qh, ks[0]) + bias_ref[2 * p + hh, :, 0:p_rows],
                      _dot_nt(qh, ks[1]) + bias_ref[2 * p + hh, :, p_rows:p_rows + LANES]]
            pv, l = _softmax_pv(scores, vs)
            out = jnp.where(msk, pv / l, out)
        o_ref[:, cols] = out.astype(BF16)
    ko_ref[0:p_rows - t, :] = kc_ref[t:p_rows, :]
    ko_ref[p_rows - t:p_rows, :] = kn_ref[...]
    vo_ref[0:p_rows - t, :] = vc_ref[t:p_rows, :]
    vo_ref[p_rows - t:p_rows, :] = vn_ref[...]


def _band_sample(qa, ka_f, va_f, cache_k, cache_v, bias_e, bias_mask):
    n, p_rows, _ = cache_k.shape
    t = qa.shape[0] // n
    row = pl.BlockSpec((t, WIDTH), lambda b: (b, 0))
    cache = pl.BlockSpec((None, p_rows, WIDTH), lambda b: (b, 0, 0))
    return pl.pallas_call(
        _band_sample_kernel,
        grid=(n,),
        in_specs=[row, cache, cache, row, row, _const_spec(bias_e.shape), _const_spec(bias_mask.shape)],
        out_specs=[row, cache, cache],
        out_shape=[jax.ShapeDtypeStruct(qa.shape, BF16),
                   jax.ShapeDtypeStruct(cache_k.shape, F32),
                   jax.ShapeDtypeStruct(cache_v.shape, F32)],
        scratch_shapes=[pltpu.VMEM((N_HEADS,) + bias_mask.shape, F32)],
        compiler_params=_params(1),
        name="band_sample",
    )(qa, cache_k, cache_v, ka_f, va_f, bias_e, bias_mask)


def _band_bias(rel_bias, n_q, n_k, offset, band_mask, valid_k):
    assert n_q + n_k - 1 <= BIAS_PERIOD
    m = np.arange(BIAS_PERIOD)
    w = np.where(m < n_k, m, m - BIAS_PERIOD)
    idx = np.clip(offset - w, -MAX_REL, MAX_REL) + MAX_REL
    e = jnp.take(rel_bias.astype(F32), jnp.asarray(idx), axis=1)
    qi = np.arange(n_q)[:, None]
    kj = np.arange(n_k)[None, :]
    valid = np.broadcast_to(kj < valid_k, (n_q, n_k))
    if band_mask:
        rel_chunk = (qi + offset) // CHUNK - kj // CHUNK
        valid = valid & (rel_chunk >= 0) & (rel_chunk <= LEFT_CHUNKS)
    return e, jnp.asarray(np.where(valid, 0.0, NEG), F32)


def _build_bias(e_ref, mask_ref, bias_sc):
    rows, cols = mask_ref.shape
    for h in range(N_HEADS):
        spread = jnp.broadcast_to(e_ref[h:h + 1, :], (rows, BIAS_PERIOD))
        bias_sc[h] = pltpu.roll(spread, 0, 1, stride=1, stride_axis=0)[:, :cols] + mask_ref[...]


def _aug_head_mask(h_in_pair, pair):
    lane = lax.broadcasted_iota(jnp.int32, (1, 2 * LANES), 1)
    head = 2 * pair + h_in_pair
    in_q = (lane < LANES) & ((lane // HEAD_DIM) == h_in_pair)
    in_aug = (lane >= LANES) & (((lane - LANES) // AUG_STRIDE) == head)
    return in_q | in_aug


def _flash_update(s, v, m_ref, l_ref, acc_ref, h, row_bias=None, v_seq_minor=False):
    chunks = [s[:, c * LANES:(c + 1) * LANES] for c in range(s.shape[1] // LANES)]
    if row_bias is not None:
        chunks = [c + row_bias for c in chunks]
    cmax = chunks[0]
    for c in chunks[1:]:
        cmax = jnp.maximum(cmax, c)
    m_prev = m_ref[h]
    m_new = jnp.maximum(m_prev, jnp.max(cmax, axis=-1, keepdims=True))
    alpha = jnp.exp2(m_prev - m_new)
    ps = [jnp.exp2(c - m_new) for c in chunks]
    lsum = ps[0]
    for p in ps[1:]:
        lsum = lsum + p
    l_ref[h] = alpha * l_ref[h] + lsum
    p = jnp.concatenate(ps, axis=-1).astype(BF16)
    pv = _dot_nt(p, v) if v_seq_minor else _dot(p, v)
    acc_ref[h] = alpha[:, :pv.shape[1]] * acc_ref[h] + pv
    m_ref[h] = m_new


def _flash_result(l_ref, acc_ref, pair):
    outs = [acc_ref[2 * pair + hh] / jnp.sum(l_ref[2 * pair + hh], axis=-1, keepdims=True)
            for hh in range(2)]
    return jnp.where(_head_mask(0), outs[0], outs[1])


def _fox_prompt_kernel(qi_ref, kj_ref, q_ref, k_ref, v_ref, o_ref, qh_sc, m_sc, l_sc, acc_sc, *, tq, tk):
    t = pl.program_id(1)
    i = qi_ref[t]
    j = kj_ref[t]

    @pl.when(j == 0)
    def _():
        for p in range(N_PAIRS):
            q = q_ref[p]
            for hh in range(2):
                qh_sc[2 * p + hh] = jnp.where(_aug_head_mask(hh, p), q, jnp.zeros_like(q))
        m_sc[...] = jnp.full_like(m_sc, NEG)
        l_sc[...] = jnp.zeros_like(l_sc)
        acc_sc[...] = jnp.zeros_like(acc_sc)

    def step(diagonal):
        if diagonal:
            keep = (lax.broadcasted_iota(jnp.int32, (tq, tk), 1)
                    <= lax.broadcasted_iota(jnp.int32, (tq, tk), 0))
        for p in range(N_PAIRS):
            k = k_ref[p]
            v = v_ref[:, p * LANES:(p + 1) * LANES]
            for hh in range(2):
                s = _dot_nt(qh_sc[2 * p + hh], k)
                if diagonal:
                    s = jnp.where(keep, s, NEG)
                _flash_update(s, v, m_sc, l_sc, acc_sc, 2 * p + hh)

    @pl.when(j < i)
    def _():
        step(False)

    @pl.when(j == i)
    def _():
        step(True)
        for p in range(N_PAIRS):
            o_ref[:, p * LANES:(p + 1) * LANES] = _flash_result(l_sc, acc_sc, p).astype(BF16)


def _fox_prompt(qcat, kcat, vb, n_seq):
    seq = qcat.shape[2]
    tq = tk = 1024
    nt = seq // tq
    qi = np.concatenate([np.full(i + 1, i) for i in range(nt)]).astype(np.int32)
    kj = np.concatenate([np.arange(i + 1) for i in range(nt)]).astype(np.int32)
    grid_spec = pltpu.PrefetchScalarGridSpec(
        num_scalar_prefetch=2,
        grid=(n_seq, len(qi)),
        in_specs=[
            pl.BlockSpec((None, N_PAIRS, tq, 2 * LANES), lambda b, t, qi, kj: (b, 0, qi[t], 0)),
            pl.BlockSpec((None, N_PAIRS, tk, 2 * LANES), lambda b, t, qi, kj: (b, 0, kj[t], 0)),
            pl.BlockSpec((tk, WIDTH), lambda b, t, qi, kj: (b * nt + kj[t], 0)),
        ],
        out_specs=pl.BlockSpec((tq, WIDTH), lambda b, t, qi, kj: (b * nt + qi[t], 0)),
        scratch_shapes=[pltpu.VMEM((N_HEADS, tq, 2 * LANES), BF16), pltpu.VMEM((N_HEADS, tq, LANES), F32),
                        pltpu.VMEM((N_HEADS, tq, LANES), F32), pltpu.VMEM((N_HEADS, tq, LANES), F32)],
    )
    return pl.pallas_call(
        functools.partial(_fox_prompt_kernel, tq=tq, tk=tk),
        grid_spec=grid_spec,
        out_shape=jax.ShapeDtypeStruct(vb.shape, BF16),
        compiler_params=_params(2),
        name="fox_prompt",
    )(jnp.asarray(qi), jnp.asarray(kj), qcat, kcat, vb)


def _suffix_sum_exclusive(x):
    n = x.shape[1]
    lane = lax.broadcasted_iota(jnp.int32, x.shape, 1)
    y = jnp.where(lane + 1 < n, pltpu.roll(x, n - 1, axis=1), 0.0)
    shift = 1
    while shift < n:
        y = y + jnp.where(lane + shift < n, pltpu.roll(y, n - shift, axis=1), 0.0)
        shift *= 2
    return y


def _fox_sample_kernel(q_ref, kc_ref, vc_ref, kn_ref, vn_ref, lfc_ref, lfn_ref, u_ref,
                       o_ref, cq_sc, car_sc, m_sc, l_sc, acc_sc, *, n_cache_tiles, sub_keys):
    j = pl.program_id(1)
    t = q_ref.shape[1]

    def cum_new():
        hi, mid, lo = _split3(lfn_ref[...])
        u = u_ref[...]
        return _dot(hi, u) + _dot(mid, u) + _dot(lo, u)

    def q_head(h):
        return q_ref[h // 2, :, (h % 2) * HEAD_DIM:(h % 2 + 1) * HEAD_DIM]

    @pl.when(j == 0)
    def _():
        cn = cum_new() * LOG2E
        eye = (lax.broadcasted_iota(jnp.int32, (t, LANES), 0)
               == lax.broadcasted_iota(jnp.int32, (t, LANES), 1))
        for h in range(N_HEADS):
            col = jnp.sum(jnp.where(eye, jnp.broadcast_to(cn[h:h + 1, :], (t, LANES)), 0.0),
                          axis=-1, keepdims=True)
            cq_sc[h] = jnp.broadcast_to(col, (t, LANES))
        car_sc[...] = jnp.zeros_like(car_sc)
        m_sc[...] = jnp.full_like(m_sc, NEG)
        l_sc[...] = jnp.zeros_like(l_sc)
        acc_sc[...] = jnp.zeros_like(acc_sc)

    @pl.when(j < n_cache_tiles)
    def _():
        lf = lfc_ref[...]
        r = _suffix_sum_exclusive(lf) + car_sc[:, 0:1]
        car_sc[...] = jnp.broadcast_to(r[:, 0:1] + lf[:, 0:1], car_sc.shape)
        r = r * LOG2E
        for sub in range(lf.shape[1] // sub_keys):
            keys = slice(sub * sub_keys, (sub + 1) * sub_keys)
            for h in range(N_HEADS):
                s = _dot(q_head(h), kc_ref[h, :, keys].astype(BF16)) + r[h:h + 1, keys]
                _flash_update(s, vc_ref[h, :, keys].astype(BF16), m_sc, l_sc, acc_sc, h,
                              row_bias=cq_sc[h], v_seq_minor=True)

    @pl.when(j == n_cache_tiles)
    def _():
        cn = cum_new() * LOG2E
        causal = (lax.broadcasted_iota(jnp.int32, (t, LANES), 1)
                  <= lax.broadcasted_iota(jnp.int32, (t, LANES), 0))
        pad = jnp.zeros((LANES - t, HEAD_DIM), BF16)
        for h in range(N_HEADS):
            cols = slice(h * HEAD_DIM, (h + 1) * HEAD_DIM)
            k = jnp.concatenate([kn_ref[h // 2, :, (h % 2) * HEAD_DIM:(h % 2 + 1) * HEAD_DIM], pad], axis=0)
            v = jnp.concatenate([vn_ref[:, cols], pad], axis=0)
            s = _dot_nt(q_head(h), k) + cq_sc[h] - cn[h:h + 1, :]
            _flash_update(jnp.where(causal, s, NEG), v, m_sc, l_sc, acc_sc, h)
            o_ref[:, cols] = (acc_sc[h] / jnp.sum(l_sc[h], axis=-1, keepdims=True)).astype(BF16)


def _fox_sample(qcat, kcat, vb, cache_kt, cache_vt, lf_cache_t, lf_new_t):
    n, _, _, p_rows = cache_kt.shape
    t = vb.shape[0] // n
    tk = 4096
    nct = p_rows // tk
    u = jnp.asarray(np.pad(np.triu(np.ones((t, t))), ((0, 0), (0, LANES - t))), BF16)
    rev = lambda j: jnp.maximum(nct - 1 - j, 0)
    cache = pl.BlockSpec((None, N_HEADS, HEAD_DIM, tk), lambda b, j: (b, 0, 0, rev(j)))
    new_cat = pl.BlockSpec((None, N_PAIRS, t, 2 * LANES), lambda b, j: (0, 0, b, 0))
    return pl.pallas_call(
        functools.partial(_fox_sample_kernel, n_cache_tiles=nct, sub_keys=tk),
        grid=(n, nct + 1),
        in_specs=[
            new_cat, cache, cache, new_cat,
            pl.BlockSpec((t, WIDTH), lambda b, j: (b, 0)),
            pl.BlockSpec((None, N_HEADS, tk), lambda b, j: (b, 0, rev(j))),
            pl.BlockSpec((None, N_HEADS, t), lambda b, j: (b, 0, 0)),
            _const_spec(u.shape),
        ],
        out_specs=pl.BlockSpec((t, WIDTH), lambda b, j: (b, 0)),
        out_shape=jax.ShapeDtypeStruct(vb.shape, BF16),
        scratch_shapes=[pltpu.VMEM((N_HEADS, t, LANES), F32), pltpu.VMEM((N_HEADS, LANES), F32),
                        pltpu.VMEM((N_HEADS, t, LANES), F32), pltpu.VMEM((N_HEADS, t, LANES), F32),
                        pltpu.VMEM((N_HEADS, t, HEAD_DIM), F32)],
        compiler_params=_params(2),
        name="fox_sample",
    )(qcat, cache_kt, cache_vt, kcat, vb, lf_cache_t, lf_new_t, u)


GROUP_LANE = 64
ROW_ALIGN = 16
MOE_TILE = 512
SORT_ROWS = 640
XS_WIDTH = D_MODEL + 2 * LANES


def _route(r):
    lane_i = lax.broadcasted_iota(jnp.int32, r.shape, 1)
    lane = lane_i.astype(F32)
    lane_grp = (lane_i // EXPERTS_PER_GROUP).astype(F32)
    big = float(LANES)
    is_coarse = (lane_i >= N_EXPERTS) & (lane_i < N_EXPERTS + N_GROUPS)
    cm = jnp.where(is_coarse, r, NEG)
    cmax = cm.max(axis=-1, keepdims=True)
    grp = jnp.min(jnp.where(cm == cmax, lane - N_EXPERTS, big), axis=-1, keepdims=True)
    pg_sel = 1.0 / jnp.sum(jnp.exp(cm - cmax), axis=-1, keepdims=True)
    in_grp = (lane_i < N_EXPERTS) & (lane_grp == grp)
    fm = jnp.where(in_grp, r, NEG)
    m1 = fm.max(axis=-1, keepdims=True)
    denom = jnp.sum(jnp.exp(fm - m1), axis=-1, keepdims=True)
    i1 = jnp.min(jnp.where(fm == m1, lane, big), axis=-1, keepdims=True)
    fm2 = jnp.where(lane == i1, NEG, fm)
    m2 = fm2.max(axis=-1, keepdims=True)
    i2 = jnp.min(jnp.where(fm2 == m2, lane, big), axis=-1, keepdims=True)
    p1 = 1.0 / denom
    p2 = jnp.exp(m2 - m1) / denom
    tot = p1 + p2
    comb = (jnp.where(lane == i1, pg_sel * (p1 / tot), 0.0)
            + jnp.where(lane == i2, pg_sel * (p2 / tot), 0.0))
    return comb, grp


def _merge_kernel(x_ref, oa_ref, ob_ref, g_ref, wg_ref, wpa_ref, wpb_ref, wo_ref, gf_ref, wrh_ref, wrm_ref,
                  br_ref, y_ref, hx_ref, route_ref, grp_t_ref, cnt_ref):
    x = x_ref[...]
    h = _rms(x, g_ref[...]).astype(BF16)
    gate = jax.nn.sigmoid(_dot(h, wg_ref[...]))
    mix = (gate[:, :D_MODEL] * _dot(oa_ref[...], wpa_ref[...])
           + gate[:, D_MODEL:] * _dot(ob_ref[...], wpb_ref[...]))
    y = x + _dot(mix.astype(BF16), wo_ref[...])
    y_ref[...] = y

    hx = _rms(y, gf_ref[...])
    hx_ref[:, 0:D_MODEL] = hx.astype(BF16)
    h_hi, h_mid, _ = _split3(hx)
    r = _dot(h_hi, wrh_ref[...]) + _dot(h_hi, wrm_ref[...]) + _dot(h_mid, wrh_ref[...]) + br_ref[...]
    comb, grp = _route(r)
    lane = lax.broadcasted_iota(jnp.int32, comb.shape, 1)
    route = jnp.where(lane == GROUP_LANE, grp, comb)
    route_ref[...] = route
    r_hi = route.astype(BF16)
    hx_ref[:, D_MODEL:D_MODEL + LANES] = r_hi
    hx_ref[:, D_MODEL + LANES:] = (route - r_hi.astype(F32)).astype(BF16)
    grp_t_ref[...] = route.T[GROUP_LANE:GROUP_LANE + 8, :]
    cnt = jnp.sum(jnp.where(lane.astype(F32) == grp, 1.0, 0.0), axis=0, keepdims=True)
    cnt_ref[...] = jnp.broadcast_to(cnt, cnt_ref.shape)


def _merge(x, oa, ob, w):
    t_total = x.shape[0]
    tm = MOE_TILE
    n_tiles = t_total // tm
    row = lambda n: pl.BlockSpec((tm, n), lambda i: (i, 0))
    consts = [w["g_mix"], w["w_gate"], w["w_pa"], w["w_pb"], w["w_o"], w["g_ffn"], w["w_router_hi"],
              w["w_router_mid"], w["b_router"]]
    return pl.pallas_call(
        _merge_kernel,
        grid=(n_tiles,),
        in_specs=[row(D_MODEL), row(WIDTH), row(WIDTH)] + [_const_spec(c.shape) for c in consts],
        out_specs=[row(D_MODEL), row(XS_WIDTH), row(LANES), pl.BlockSpec((8, tm), lambda i: (0, i)),
                   pl.BlockSpec((None, 8, LANES), lambda i: (i, 0, 0))],
        out_shape=[jax.ShapeDtypeStruct(x.shape, F32), jax.ShapeDtypeStruct((t_total, XS_WIDTH), BF16),
                   jax.ShapeDtypeStruct((t_total, LANES), F32), jax.ShapeDtypeStruct((8, t_total), F32),
                   jax.ShapeDtypeStruct((n_tiles, 8, LANES), F32)],
        compiler_params=_params(1),
        name="merge",
    )(x, oa, ob, *consts)


def _plan_kernel(cnt_ref, ib_ref, dst_ref, fill_ref, inblk_ref, outblk_ref, grp_ref, kind_ref, *,
                 n_tiles, n_steps):
    align_bits = ROW_ALIGN.bit_length() - 1
    tile_bits = MOE_TILE.bit_length() - 1
    zero = jnp.int32(0)

    def per_tile(t, tot):
        run = zero
        new_tot = []
        for g in range(N_GROUPS):
            n = ((cnt_ref[t * N_GROUPS + g] + (ROW_ALIGN - 1)) >> align_bits) << align_bits
            ib_ref[t * N_GROUPS + g] = run
            dst_ref[t * N_GROUPS + g] = tot[g]
            run = run + n
            new_tot.append(tot[g] + n)
        return tuple(new_tot)

    tot = lax.fori_loop(0, n_tiles, per_tile, (zero,) * N_GROUPS)
    n_ffn = [(tot[g] + (MOE_TILE - 1)) >> tile_bits for g in range(N_GROUPS)]
    g_base, ends, acc_rows, acc_tiles = [], [], zero, zero
    for g in range(N_GROUPS):
        g_base.append(acc_rows)
        acc_rows = acc_rows + ((n_ffn[g] + 1) << tile_bits)
        acc_tiles = acc_tiles + n_ffn[g] + 1
        ends.append(acc_tiles)

    def add_base(t, carry):
        for g in range(N_GROUPS):
            dst_ref[t * N_GROUPS + g] = dst_ref[t * N_GROUPS + g] + g_base[g]
        return carry

    lax.fori_loop(0, n_tiles, add_base, zero)
    for g in range(N_GROUPS):
        fill_ref[g] = g_base[g] + tot[g]
        fill_ref[N_GROUPS + g] = g_base[g] + (n_ffn[g] << tile_bits)
    fill_ref[2 * N_GROUPS] = ends[-1]
    first_blk = g_base[N_GROUPS - 1]
    for g in reversed(range(N_GROUPS - 1)):
        first_blk = jnp.where(n_ffn[g] > 0, g_base[g], first_blk)
    first_blk = first_blk >> tile_bits

    def pick(vals, g):
        out = vals[N_GROUPS - 1]
        for i in reversed(range(N_GROUPS - 1)):
            out = jnp.where(g == i, vals[i], out)
        return out

    def per_step(k, carry):
        kk = jnp.minimum(k, ends[-1] - 1)
        g = zero
        for i in range(N_GROUPS - 1):
            g = g + jnp.where(kk >= ends[i], 1, 0)
        j = kk - pick([zero] + ends[:-1], g)
        is_expert = j < pick(n_ffn, g)
        out_blk = jnp.where(k < ends[-1], (pick(g_base, g) >> tile_bits) + j, k)
        outblk_ref[k] = out_blk
        inblk_ref[k] = jnp.where(is_expert, out_blk, first_blk)
        grp_ref[k] = g
        kind_ref[k] = jnp.where(is_expert, 1, 0)
        return carry

    lax.fori_loop(0, n_steps, per_step, zero)


def _moe_plan(cnt, t_total):
    n_tiles = t_total // MOE_TILE
    n_steps = (n_tiles + 2 * N_GROUPS + (N_GROUPS * (ROW_ALIGN - 1) * n_tiles + MOE_TILE - 1) // MOE_TILE)
    smem = pl.BlockSpec(memory_space=pltpu.SMEM)
    sizes = [n_tiles * N_GROUPS, n_tiles * N_GROUPS, 2 * N_GROUPS + 1, n_steps, n_steps, n_steps, n_steps]
    outs = pl.pallas_call(
        functools.partial(_plan_kernel, n_tiles=n_tiles, n_steps=n_steps),
        grid_spec=pltpu.PrefetchScalarGridSpec(num_scalar_prefetch=1, grid=(1,), in_specs=[],
                                               out_specs=[smem] * len(sizes)),
        out_shape=[jax.ShapeDtypeStruct((n,), jnp.int32) for n in sizes],
        compiler_params=_params(1),
        name="moe_plan",
    )(cnt.reshape(-1))
    return (*outs, n_steps)


def _sort_kernel(ib_ref, dst_ref, fill_ref, hx_ref, grp_t_ref, tri_ref, xs_ref, xsort, sems):
    i = pl.program_id(0)
    n = pl.num_programs(0)
    tm = hx_ref.shape[0]
    slot = i % 2

    def block_copy(src, dst, sl, g):
        return pltpu.make_async_copy(xsort.at[sl, pl.ds(pl.multiple_of(src, ROW_ALIGN), tm)],
                                     xs_ref.at[pl.ds(pl.multiple_of(dst, ROW_ALIGN), tm)], sems.at[sl, g])

    def copies(src_rows, dst_rows, sl):
        return [block_copy(src_rows(g), dst_rows(g), sl, g) for g in range(N_GROUPS)]

    def run(cs):
        for c in cs:
            c.start()
        for c in cs:
            c.wait()

    def step_copies(step, sl):
        return copies(lambda g: ib_ref[step * N_GROUPS + g], lambda g: dst_ref[step * N_GROUPS + g], sl)

    @pl.when(i == 0)
    def _():
        for sl in range(2):
            xsort[sl, SORT_ROWS:, :] = jnp.zeros((xsort.shape[1] - SORT_ROWS, XS_WIDTH), BF16)

    g_row = grp_t_ref[0:1, :]
    sub = lax.broadcasted_iota(jnp.int32, (8, tm), 0).astype(F32)
    mine = sub == g_row
    before = _dot(jnp.where(mine, 1.0, 0.0).astype(BF16), tri_ref[...])
    dest = jnp.sum(jnp.where(mine, before, 0.0), axis=0, keepdims=True)
    for g in range(N_GROUPS):
        base = jnp.full(dest.shape, ib_ref[i * N_GROUPS + g], jnp.int32).astype(F32)
        dest = dest + jnp.where(g_row == g, base, 0.0)
    rows = lax.broadcasted_iota(jnp.int32, (SORT_ROWS, tm), 0).astype(F32)
    perm = jnp.where(rows == dest, 1.0, 0.0).astype(BF16)
    xsort[slot, 0:SORT_ROWS, :] = _dot(perm, hx_ref[...]).astype(BF16)

    @pl.when(i > 0)
    def _():
        for c in step_copies(i - 1, 1 - slot):
            c.wait()

    for c in step_copies(i, slot):
        c.start()

    @pl.when(i == n - 1)
    def _():
        for c in step_copies(i, slot):
            c.wait()
        run(copies(lambda g: SORT_ROWS, lambda g: fill_ref[g], slot))
        run(copies(lambda g: SORT_ROWS, lambda g: fill_ref[N_GROUPS + g], slot))

        @pl.loop(fill_ref[2 * N_GROUPS], xs_ref.shape[0] // tm)
        def _(b):
            run([block_copy(SORT_ROWS, b * tm, slot, 0)])


def _expert_kernel(ib_ref, ob_ref, grp_ref, kind_ref, xs_ref, w1_ref, w3_ref, w2_ref, ys_ref):
    del ib_ref, ob_ref
    k = pl.program_id(0)

    @pl.when(kind_ref[k] == 1)
    def _():
        x = xs_ref[:, 0:D_MODEL]
        side = lambda w_ref: jnp.concatenate([w_ref[e].astype(BF16) for e in range(EXPERTS_PER_GROUP)], axis=1)
        hid = jax.nn.silu(_dot(x, side(w1_ref))) * _dot(x, side(w3_ref))
        comb = (xs_ref[:, D_MODEL:D_MODEL + LANES].astype(F32) + xs_ref[:, D_MODEL + LANES:].astype(F32))
        lane = lax.broadcasted_iota(jnp.int32, comb.shape, 1)
        first = grp_ref[k] * EXPERTS_PER_GROUP
        scale = []
        for e in range(EXPERTS_PER_GROUP):
            w_e = jnp.sum(jnp.where(lane == first + e, comb, 0.0), axis=-1, keepdims=True)
            scale.append(jnp.broadcast_to(w_e, (comb.shape[0], D_EXPERT)))
        hid = hid * jnp.concatenate(scale, axis=-1)
        w2 = w2_ref[...].astype(BF16).reshape(EXPERTS_PER_GROUP * D_EXPERT, D_MODEL)
        ys_ref[...] = _dot(hid.astype(BF16), w2).astype(BF16)

    @pl.when(kind_ref[k] == 0)
    def _():
        ys_ref[...] = jnp.zeros_like(ys_ref)


def _unsort_kernel(ib_ref, dst_ref, y_ref, route_ref, tri_ref, ys_ref, o_ref, ybuf, yasm, sems):
    i = pl.program_id(0)
    n = pl.num_programs(0)
    tm = y_ref.shape[0]
    slot = i % 2

    def fetch(step, sl):
        return [pltpu.make_async_copy(
            ys_ref.at[pl.ds(pl.multiple_of(dst_ref[step * N_GROUPS + g], ROW_ALIGN), tm)],
            ybuf.at[sl, g], sems.at[sl, g]) for g in range(N_GROUPS)]

    @pl.when(i == 0)
    def _():
        yasm[...] = jnp.zeros_like(yasm)
        for c in fetch(0, 0):
            c.start()

    @pl.when(i + 1 < n)
    def _():
        for c in fetch(i + 1, 1 - slot):
            c.start()

    route = route_ref[...]
    lane = lax.broadcasted_iota(jnp.int32, route.shape, 1).astype(F32)
    grp = jnp.sum(jnp.where(lane == GROUP_LANE, route, 0.0), axis=-1, keepdims=True)
    mine = lane == grp
    before = _dot(tri_ref[...], jnp.where(mine, 1.0, 0.0).astype(BF16))
    dest = jnp.sum(jnp.where(mine, before, 0.0), axis=-1, keepdims=True)
    for g in range(N_GROUPS):
        base = jnp.full(dest.shape, ib_ref[i * N_GROUPS + g], jnp.int32).astype(F32)
        dest = dest + jnp.where(grp == g, base, 0.0)
    cols = lax.broadcasted_iota(jnp.int32, (tm, SORT_ROWS), 1).astype(F32)
    perm_t = jnp.where(cols == dest, 1.0, 0.0).astype(BF16)

    for c in fetch(i, slot):
        c.wait()
    for g in range(N_GROUPS):
        yasm[pl.ds(pl.multiple_of(ib_ref[i * N_GROUPS + g], ROW_ALIGN), tm), :] = ybuf[slot, g]
    o_ref[...] = y_ref[...] + _dot(perm_t, yasm[0:SORT_ROWS, :])


def _moe(y, hx, route, grp_t, cnt, w):
    t_total = y.shape[0]
    tm = MOE_TILE
    n_tiles = t_total // tm
    assert SORT_ROWS >= tm + N_GROUPS * (ROW_ALIGN - 1)
    in_base, dst, fill, in_blk, out_blk, grp_of_step, is_expert, n_steps = _moe_plan(
        cnt[:, 0, :N_GROUPS].astype(jnp.int32), t_total)
    cap_rows = n_steps * tm
    lower = jnp.asarray(np.tril(np.ones((tm, tm)), -1), BF16)
    upper = jnp.asarray(np.triu(np.ones((tm, tm)), 1), BF16)
    stage_rows = SORT_ROWS + tm
    any_spec = pl.BlockSpec(memory_space=pl.ANY)

    xs = pl.pallas_call(
        _sort_kernel,
        grid_spec=pltpu.PrefetchScalarGridSpec(
            num_scalar_prefetch=3,
            grid=(n_tiles,),
            in_specs=[pl.BlockSpec((tm, XS_WIDTH), lambda i, *_: (i, 0)),
                      pl.BlockSpec((8, tm), lambda i, *_: (0, i)),
                      pl.BlockSpec((tm, tm), lambda i, *_: (0, 0))],
            out_specs=any_spec,
            scratch_shapes=[pltpu.VMEM((2, stage_rows, XS_WIDTH), BF16),
                            pltpu.SemaphoreType.DMA((2, N_GROUPS))],
        ),
        out_shape=jax.ShapeDtypeStruct((cap_rows, XS_WIDTH), BF16),
        compiler_params=_params(1),
        name="moe_sort",
    )(in_base, dst, fill, hx, grp_t, upper)

    step_map = lambda which: (lambda k, ib, ob, grp, kind: ((ib, ob, grp)[which][k], 0))
    wmap = lambda k, ib, ob, grp, kind: (grp[k], 0, 0)
    wspec = pl.BlockSpec((EXPERTS_PER_GROUP, D_MODEL, D_EXPERT), wmap)
    ys = pl.pallas_call(
        _expert_kernel,
        grid_spec=pltpu.PrefetchScalarGridSpec(
            num_scalar_prefetch=4,
            grid=(n_steps,),
            in_specs=[pl.BlockSpec((tm, XS_WIDTH), step_map(0)), wspec, wspec, pl.BlockSpec((EXPERTS_PER_GROUP, D_EXPERT, D_MODEL), wmap)],
            out_specs=pl.BlockSpec((tm, D_MODEL), step_map(1)),
        ),
        out_shape=jax.ShapeDtypeStruct((cap_rows, D_MODEL), BF16),
        compiler_params=_params(1),
        name="moe_experts",
    )(in_blk, out_blk, grp_of_step, is_expert, xs, w["w1"], w["w3"], w["w2"])

    return pl.pallas_call(
        _unsort_kernel,
        grid_spec=pltpu.PrefetchScalarGridSpec(
            num_scalar_prefetch=2,
            grid=(n_tiles,),
            in_specs=[pl.BlockSpec((tm, D_MODEL), lambda i, *_: (i, 0)),
                      pl.BlockSpec((tm, LANES), lambda i, *_: (i, 0)),
                      pl.BlockSpec((tm, tm), lambda i, *_: (0, 0)),
                      any_spec],
            out_specs=pl.BlockSpec((tm, D_MODEL), lambda i, *_: (i, 0)),
            scratch_shapes=[pltpu.VMEM((2, N_GROUPS, tm, D_MODEL), BF16),
                            pltpu.VMEM((stage_rows, D_MODEL), BF16),
                            pltpu.SemaphoreType.DMA((2, N_GROUPS))],
        ),
        out_shape=jax.ShapeDtypeStruct(y.shape, F32),
        compiler_params=_params(1),
        name="moe_unsort",
    )(in_base, dst, y, route, lower, ys)


def _prep_weights(g_mix, w_in, b_f, q_norm_a, k_norm_a, q_norm_b, k_norm_b, w_pa, w_pb, w_o,
                  g_ffn, w_rg, b_rg, w_re, b_re, w1, w3, w2):
    n_qkv = 6 * WIDTH
    tile = lambda g: jnp.tile(g, N_HEADS)
    w_router = jnp.concatenate(
        [jnp.transpose(w_re, (1, 0, 2)).reshape(D_MODEL, N_EXPERTS), w_rg,
         jnp.zeros((D_MODEL, LANES - N_EXPERTS - N_GROUPS), F32)], axis=1)
    b_router = jnp.concatenate(
        [b_re.reshape(N_EXPERTS), b_rg, jnp.zeros((LANES - N_EXPERTS - N_GROUPS,), F32)])[None, :]
    return {
        "g_mix": g_mix[None, :],
        "w_qkv": w_in[:, :n_qkv].astype(BF16),
        "w_f": jnp.pad(w_in[:, n_qkv:n_qkv + N_HEADS], ((0, 0), (0, LANES - N_HEADS))).astype(BF16),
        "b_f": jnp.pad(b_f, (0, LANES - N_HEADS))[None, :],
        "w_gate": w_in[:, n_qkv + N_HEADS:].astype(BF16),
        "gains": jnp.stack([tile(q_norm_a), tile(k_norm_a), tile(q_norm_b), tile(k_norm_b)]),
        "w_pa": w_pa.astype(BF16), "w_pb": w_pb.astype(BF16), "w_o": w_o.astype(BF16),
        "g_ffn": g_ffn[None, :],
        "w_router_hi": w_router.astype(BF16),
        "w_router_mid": (w_router - w_router.astype(BF16).astype(F32)).astype(BF16),
        "b_router": b_router,
        "w1": w1, "w3": w3, "w2": w2,
    }


def kernel(x_prompt, x_sample, cache_a_k, cache_a_v, cache_b_k, cache_b_v, cache_b_logf, g_mix, w_in, b_f, q_norm_a, k_norm_a, q_norm_b, k_norm_b, rel_bias, w_pa, w_pb, w_o, g_ffn, w_rg, b_rg, w_re, b_re, w1, w3, w2):
    assert g_mix.shape[0] == 1, "single-layer step"
    n_p, seq, _ = x_prompt.shape
    n_s, t_s, _ = x_sample.shape
    a_rows = cache_a_k.shape[2]
    w = _prep_weights(g_mix[0], w_in[0], b_f[0], q_norm_a[0], k_norm_a[0], q_norm_b[0], k_norm_b[0],
                      w_pa[0], w_pb[0], w_o[0], g_ffn[0], w_rg[0], b_rg[0], w_re[0], b_re[0],
                      w1[0], w3[0], w2[0])
    band_tq = 256
    bias_prompt = _band_bias(rel_bias[0], band_tq, 3 * band_tq, 2 * band_tq, True, 3 * band_tq)
    bias_sample = _band_bias(rel_bias[0], t_s, a_rows + LANES, a_rows, False, a_rows + t_s)

    seq_minor = lambda a: jnp.transpose(a, (0, 2, 3, 1))
    seq_major = lambda a: jnp.transpose(a, (0, 3, 1, 2))[None]

    xp = x_prompt.reshape(n_p * seq, D_MODEL)
    (qa, ka, va, ka_t, va_t, qcat, kcat, vb, kb_t, vb_t, logf_t) = _inproj(xp, n_p, w, seq_minor=True)
    o_a = _band_prompt(qa, ka, va, *bias_prompt, n_p)
    o_b = _fox_prompt(qcat, kcat, vb, n_p)
    y_p = _moe(*_merge(xp, o_a, o_b, w), w)

    xs = x_sample.reshape(n_s * t_s, D_MODEL)
    (qa_s, _, _, ka_fs, va_fs, qcat_s, kcat_s, vb_s, kb_fs, vb_fs, logf_s) = _inproj(
        xs, 1, w, seq_minor=False)
    o_as, new_ak, new_av = _band_sample(
        qa_s, ka_fs, va_fs, cache_a_k[0].reshape(n_s, a_rows, WIDTH),
        cache_a_v[0].reshape(n_s, a_rows, WIDTH), *bias_sample)
    lf_cache_t = jnp.transpose(cache_b_logf[0], (0, 2, 1))
    lf_new_t = jnp.transpose(logf_s.reshape(n_s, t_s, N_HEADS), (0, 2, 1))
    o_bs = _fox_sample(qcat_s, kcat_s, vb_s, seq_minor(cache_b_k[0]), seq_minor(cache_b_v[0]),
                       lf_cache_t, lf_new_t)
    y_s = _moe(*_merge(xs, o_as, o_bs, w), w)

    heads = lambda a, n, r: a.reshape(1, n, r, N_HEADS, HEAD_DIM)
    return (y_p.reshape(n_p, seq, D_MODEL), y_s.reshape(n_s, t_s, D_MODEL),
            seq_major(ka_t), seq_major(va_t), seq_major(kb_t), seq_major(vb_t),
            jnp.transpose(logf_t, (0, 2, 1))[None],
            heads(new_ak, n_s, a_rows), heads(new_av, n_s, a_rows),
            heads(kb_fs, n_s, t_s), heads(vb_fs, n_s, t_s), logf_s.reshape(1, n_s, t_s, N_HEADS))
```

```python
import functools

import numpy as np
import jax
import jax.numpy as jnp
from jax import lax
from jax.experimental import pallas as pl
from jax.experimental.pallas import tpu as pltpu

F32 = jnp.float32
BF16 = jnp.bfloat16

D_MODEL = 1024
HEAD_DIM = 64
N_HEADS = 8
WIDTH = N_HEADS * HEAD_DIM
N_PAIRS = N_HEADS // 2
CHUNK = 64
LEFT_CHUNKS = 8
WINDOW_ROWS = LEFT_CHUNKS * CHUNK
MAX_REL = 256
N_GROUPS = 4
EXPERTS_PER_GROUP = 8
N_EXPERTS = N_GROUPS * EXPERTS_PER_GROUP
D_EXPERT = 128
EPS = 1e-6
NEG = -1e30
LOG2E = 1.4426950408889634
LANES = 128
BIAS_PERIOD = 1024
AUG_STRIDE = 8
VMEM_LIMIT = 56 * 1024 * 1024

_NT = (((1,), (1,)), ((), ()))


def _dot(a, b):
    return jnp.dot(a, b, preferred_element_type=F32)


def _dot_nt(a, b):
    return lax.dot_general(a, b, _NT, preferred_element_type=F32)


def _split3(x):
    hi = x.astype(BF16)
    r = x - hi.astype(F32)
    mid = r.astype(BF16)
    lo = (r - mid.astype(F32)).astype(BF16)
    return hi, mid, lo


def _dot3(a_bf, x):
    hi, mid, lo = _split3(x)
    return _dot(a_bf, hi) + _dot(a_bf, mid) + _dot(a_bf, lo)


def _rms(x, g):
    ms = jnp.mean(x * x, axis=-1, keepdims=True)
    return x * lax.rsqrt(ms + EPS) * g


def _params(n_axes):
    return pltpu.CompilerParams(dimension_semantics=("arbitrary",) * n_axes,
                                vmem_limit_bytes=VMEM_LIMIT)


def _const_spec(shape):
    nd = len(shape)
    return pl.BlockSpec(shape, lambda *_: (0,) * nd)


def _inproj_kernel(x_ref, g_ref, wqkv_ref, wf_ref, bf_ref, gains_ref, bd_ref, ltri_ref,
                   selq_ref, selk_ref, oneq_ref, onek_ref,
                   qa_ref, ka_ref, va_ref, kaf_ref, vaf_ref, qcat_ref, kcat_ref, vb_ref,
                   kbf_ref, vbf_ref, logf_ref, carry_ref, *, tiles_per_seq, seq_minor):
    i = pl.program_id(0)
    tm = x_ref.shape[0]

    @pl.when(i % tiles_per_seq == 0)
    def _():
        carry_ref[...] = jnp.zeros_like(carry_ref)

    def put_f32(ref, y, last_tile_only=False):
        if not seq_minor:
            ref[...] = y
        elif last_tile_only:
            @pl.when(i % tiles_per_seq == tiles_per_seq - 1)
            def _():
                ref[...] = y.T.reshape(N_HEADS, HEAD_DIM, tm)
        else:
            ref[...] = y.T.reshape(N_HEADS, HEAD_DIM, tm)

    h = _rms(x_ref[...], g_ref[...]).astype(BF16)

    def seg(s):
        return _dot(h, wqkv_ref[:, s * WIDTH:(s + 1) * WIDTH])

    def headnorm(y, n):
        ss = _dot((y * y).astype(BF16), bd_ref[...])
        return y * lax.rsqrt(ss * (1.0 / HEAD_DIM) + EPS) * gains_ref[n:n + 1, :]

    q_a = headnorm(seg(0), 0)
    qa_ref[...] = (q_a * (HEAD_DIM ** -0.5)).astype(BF16)
    k_a = headnorm(seg(1), 1)
    ka_ref[...] = k_a.astype(BF16)
    put_f32(kaf_ref, k_a, last_tile_only=True)
    v_a = seg(2)
    va_ref[...] = v_a.astype(BF16)
    put_f32(vaf_ref, v_a, last_tile_only=True)

    z = _dot(h, wf_ref[...]) + bf_ref[...]
    logf = jnp.minimum(z, 0.0) - jnp.log(1.0 + jnp.exp(-jnp.abs(z)))
    if seq_minor:
        logf_ref[...] = logf.T[0:N_HEADS, :]
    else:
        logf_ref[...] = logf[:, :N_HEADS]
    c = _dot3(ltri_ref[...], logf) + carry_ref[0:1, :]
    carry_ref[...] = jnp.broadcast_to(c[-1:, :], carry_ref.shape)
    cs = jnp.concatenate(_split3(c * LOG2E), axis=-1)
    q_aug = (_dot(cs, selq_ref[...]) + oneq_ref[...]).astype(BF16)
    k_aug = (_dot(cs, selk_ref[...]) + onek_ref[...]).astype(BF16)

    q_b = (headnorm(seg(3), 2) * (HEAD_DIM ** -0.5 * LOG2E)).astype(BF16)
    k_b = headnorm(seg(4), 3)
    put_f32(kbf_ref, k_b)
    k_b = k_b.astype(BF16)
    v_b = seg(5)
    put_f32(vbf_ref, v_b)
    vb_ref[...] = v_b.astype(BF16)
    for p in range(N_PAIRS):
        cols = slice(p * LANES, (p + 1) * LANES)
        qcat_ref[p, :, 0:LANES] = q_b[:, cols]
        qcat_ref[p, :, LANES:2 * LANES] = q_aug
        kcat_ref[p, :, 0:LANES] = k_b[:, cols]
        kcat_ref[p, :, LANES:2 * LANES] = k_aug


def _aug_constants():
    selq = np.zeros((3 * LANES, LANES), np.float32)
    selk = np.zeros((3 * LANES, LANES), np.float32)
    oneq = np.zeros((1, LANES), np.float32)
    onek = np.zeros((1, LANES), np.float32)
    for h in range(N_HEADS):
        for k in range(3):
            selq[k * LANES + h, AUG_STRIDE * h + k] = 1.0
            selk[k * LANES + h, AUG_STRIDE * h + 3 + k] = -1.0
            oneq[0, AUG_STRIDE * h + 3 + k] = 1.0
            onek[0, AUG_STRIDE * h + k] = 1.0
    return (jnp.asarray(selq, BF16), jnp.asarray(selk, BF16), jnp.asarray(oneq), jnp.asarray(onek))


def _inproj(x, n_seq, w, seq_minor):
    t_total = x.shape[0]
    tm = 512
    n_tiles = t_total // tm
    seq = t_total // n_seq
    tps = seq // tm
    row = lambda i: (i, 0)
    flat = lambda dt: (jax.ShapeDtypeStruct((t_total, WIDTH), dt), pl.BlockSpec((tm, WIDTH), row))
    if seq_minor:
        assert tm == WINDOW_ROWS
        band_f32 = (jax.ShapeDtypeStruct((n_seq, N_HEADS, HEAD_DIM, WINDOW_ROWS), F32),
                    pl.BlockSpec((None, N_HEADS, HEAD_DIM, tm), lambda i: (i // tps, 0, 0, 0)))
        fox_f32 = (jax.ShapeDtypeStruct((n_seq, N_HEADS, HEAD_DIM, seq), F32),
                   pl.BlockSpec((None, N_HEADS, HEAD_DIM, tm), lambda i: (i // tps, 0, 0, i % tps)))
        logf = (jax.ShapeDtypeStruct((n_seq, N_HEADS, seq), F32),
                pl.BlockSpec((None, N_HEADS, tm), lambda i: (i // tps, 0, i % tps)))
    else:
        band_f32 = fox_f32 = flat(F32)
        logf = (jax.ShapeDtypeStruct((t_total, N_HEADS), F32), pl.BlockSpec((tm, N_HEADS), row))
    cat = (jax.ShapeDtypeStruct((n_seq, N_PAIRS, seq, 2 * LANES), BF16),
           pl.BlockSpec((None, N_PAIRS, tm, 2 * LANES), lambda i: (i // tps, 0, i % tps, 0)))
    outs = [flat(BF16), flat(BF16), flat(BF16), band_f32, band_f32, cat, cat, flat(BF16),
            fox_f32, fox_f32, logf]
    bd = jnp.asarray(np.kron(np.eye(N_HEADS), np.ones((HEAD_DIM, HEAD_DIM))), BF16)
    ltri = jnp.asarray(np.tril(np.ones((tm, tm))), BF16)
    consts = [w["g_mix"], w["w_qkv"], w["w_f"], w["b_f"], w["gains"], bd, ltri, *_aug_constants()]
    in_specs = [pl.BlockSpec((tm, D_MODEL), row)] + [_const_spec(c.shape) for c in consts]
    return pl.pallas_call(
        functools.partial(_inproj_kernel, tiles_per_seq=tps, seq_minor=seq_minor),
        grid=(n_tiles,),
        in_specs=in_specs,
        out_specs=[o[1] for o in outs],
        out_shape=[o[0] for o in outs],
        scratch_shapes=[pltpu.VMEM((8, LANES), F32)],
        compiler_params=_params(1),
        name="inproj",
    )(x, *consts)


def _softmax_pv(scores, values):
    chunks = [[s[:, c * LANES:(c + 1) * LANES] for c in range(s.shape[1] // LANES)] for s in scores]
    cmax = None
    for cs in chunks:
        for c in cs:
            cmax = c if cmax is None else jnp.maximum(cmax, c)
    m = jnp.max(cmax, axis=-1, keepdims=True)
    pv = None
    lsum = None
    for cs, v in zip(chunks, values):
        ps = [jnp.exp(c - m) for c in cs]
        for p in ps:
            lsum = p if lsum is None else lsum + p
        o = _dot(jnp.concatenate(ps, axis=-1).astype(BF16), v)
        pv = o if pv is None else pv + o
    return pv, jnp.sum(lsum, axis=-1, keepdims=True)


def _head_mask(hh):
    lane = lax.broadcasted_iota(jnp.int32, (1, LANES), 1)
    return (lane // HEAD_DIM) == hh


def _band_prompt_kernel(q_ref, k0_ref, k1_ref, k2_ref, v0_ref, v1_ref, v2_ref, e_ref, mask_ref, o_ref,
                        bias_ref, *, tq):
    i = pl.program_id(1)

    @pl.when((pl.program_id(0) == 0) & (i == 0))
    def _():
        _build_bias(e_ref, mask_ref, bias_ref)

    k_refs = (k0_ref, k1_ref, k2_ref)
    v_refs = (v0_ref, v1_ref, v2_ref)
    pad = [jnp.where(i - 2 + j >= 0, 0.0, NEG) for j in range(2)] + [0.0]
    for p in range(N_PAIRS):
        cols = slice(p * LANES, (p + 1) * LANES)
        q = q_ref[:, cols]
        ks = [r[:, cols] for r in k_refs]
        vs = [r[:, cols] for r in v_refs]
        out = jnp.zeros((tq, LANES), F32)
        for hh in range(2):
            msk = _head_mask(hh)
            qh = jnp.where(msk, q, jnp.zeros_like(q))
            scores = [_dot_nt(qh, ks[j]) + bias_ref[2 * p + hh, :, j * tq:(j + 1) * tq] + pad[j]
                      for j in range(3)]
            pv, l = _softmax_pv(scores, vs)
            out = jnp.where(msk, pv / l, out)
        o_ref[:, cols] = out.astype(BF16)


def _band_prompt(qa, ka, va, bias_e, bias_mask, n_seq):
    t_total = qa.shape[0]
    tq = 256
    seq = t_total // n_seq
    nq = seq // tq
    qmap = lambda b, i: (b * nq + i, 0)

    def kmap(j):
        return lambda b, i: (b * nq + jnp.maximum(i - 2 + j, 0), 0)

    blk = lambda m: pl.BlockSpec((tq, WIDTH), m)
    return pl.pallas_call(
        functools.partial(_band_prompt_kernel, tq=tq),
        grid=(n_seq, nq),
        in_specs=[blk(qmap)] + [blk(kmap(j)) for j in range(3)] + [blk(kmap(j)) for j in range(3)]
        + [_const_spec(bias_e.shape), _const_spec(bias_mask.shape)],
        out_specs=blk(qmap),
        out_shape=jax.ShapeDtypeStruct((t_total, WIDTH), BF16),
        scratch_shapes=[pltpu.VMEM((N_HEADS,) + bias_mask.shape, F32)],
        compiler_params=_params(2),
        name="band_prompt",
    )(qa, ka, ka, ka, va, va, va, bias_e, bias_mask)


def _band_sample_kernel(q_ref, kc_ref, vc_ref, kn_ref, vn_ref, e_ref, mask_ref, o_ref, ko_ref, vo_ref,
                        bias_ref):
    p_rows = kc_ref.shape[0]
    t = q_ref.shape[0]

    @pl.when(pl.program_id(0) == 0)
    def _():
        _build_bias(e_ref, mask_ref, bias_ref)

    for p in range(N_PAIRS):
        cols = slice(p * LANES, (p + 1) * LANES)
        q = q_ref[:, cols]
        pad = jnp.zeros((LANES - t, LANES), BF16)
        ks = [kc_ref[:, cols].astype(BF16), jnp.concatenate([kn_ref[:, cols].astype(BF16), pad], axis=0)]
        vs = [vc_ref[:, cols].astype(BF16), jnp.concatenate([vn_ref[:, cols].astype(BF16), pad], axis=0)]
        out = jnp.zeros((t, LANES), F32)
        for hh in range(2):
            msk = _head_mask(hh)
            qh = jnp.where(msk, q, jnp.zeros_like(q))
            scores = [_dot_nt(qh, ks[0]) + bias_ref[2 * p + hh, :, 0:p_rows],
                      _dot_nt(qh, ks[1]) + bias_ref[2 * p + hh, :, p_rows:p_rows + LANES]]
            pv, l = _softmax_pv(scores, vs)
            out = jnp.where(msk, pv / l, out)
        o_ref[:, cols] = out.astype(BF16)
    ko_ref[0:p_rows - t, :] = kc_ref[t:p_rows, :]
    ko_ref[p_rows - t:p_rows, :] = kn_ref[...]
    vo_ref[0:p_rows - t, :] = vc_ref[t:p_rows, :]
    vo_ref[p_rows - t:p_rows, :] = vn_ref[...]


def _band_sample(qa, ka_f, va_f, cache_k, cache_v, bias_e, bias_mask):
    n, p_rows, _ = cache_k.shape
    t = qa.shape[0] // n
    row = pl.BlockSpec((t, WIDTH), lambda b: (b, 0))
    cache = pl.BlockSpec((None, p_rows, WIDTH), lambda b: (b, 0, 0))
    return pl.pallas_call(
        _band_sample_kernel,
        grid=(n,),
        in_specs=[row, cache, cache, row, row, _const_spec(bias_e.shape), _const_spec(bias_mask.shape)],
        out_specs=[row, cache, cache],
        out_shape=[jax.ShapeDtypeStruct(qa.shape, BF16),
                   jax.ShapeDtypeStruct(cache_k.shape, F32),
                   jax.ShapeDtypeStruct(cache_v.shape, F32)],
        scratch_shapes=[pltpu.VMEM((N_HEADS,) + bias_mask.shape, F32)],
        compiler_params=_params(1),
        name="band_sample",
    )(qa, cache_k, cache_v, ka_f, va_f, bias_e, bias_mask)


def _band_bias(rel_bias, n_q, n_k, offset, band_mask, valid_k):
    assert n_q + n_k - 1 <= BIAS_PERIOD
    m = np.arange(BIAS_PERIOD)
    w = np.where(m < n_k, m, m - BIAS_PERIOD)
    idx = np.clip(offset - w, -MAX_REL, MAX_REL) + MAX_REL
    e = jnp.take(rel_bias.astype(F32), jnp.asarray(idx), axis=1)
    qi = np.arange(n_q)[:, None]
    kj = np.arange(n_k)[None, :]
    valid = np.broadcast_to(kj < valid_k, (n_q, n_k))
    if band_mask:
        rel_chunk = (qi + offset) // CHUNK - kj // CHUNK
        valid = valid & (rel_chunk >= 0) & (rel_chunk <= LEFT_CHUNKS)
    return e, jnp.asarray(np.where(valid, 0.0, NEG), F32)


def _build_bias(e_ref, mask_ref, bias_sc):
    rows, cols = mask_ref.shape
    for h in range(N_HEADS):
        spread = jnp.broadcast_to(e_ref[h:h + 1, :], (rows, BIAS_PERIOD))
        bias_sc[h] = pltpu.roll(spread, 0, 1, stride=1, stride_axis=0)[:, :cols] + mask_ref[...]


def _aug_head_mask(h_in_pair, pair):
    lane = lax.broadcasted_iota(jnp.int32, (1, 2 * LANES), 1)
    head = 2 * pair + h_in_pair
    in_q = (lane < LANES) & ((lane // HEAD_DIM) == h_in_pair)
    in_aug = (lane >= LANES) & (((lane - LANES) // AUG_STRIDE) == head)
    return in_q | in_aug


def _flash_update(s, v, m_ref, l_ref, acc_ref, h, row_bias=None, v_seq_minor=False):
    chunks = [s[:, c * LANES:(c + 1) * LANES] for c in range(s.shape[1] // LANES)]
    if row_bias is not None:
        chunks = [c + row_bias for c in chunks]
    cmax = chunks[0]
    for c in chunks[1:]:
        cmax = jnp.maximum(cmax, c)
    m_prev = m_ref[h]
    m_new = jnp.maximum(m_prev, jnp.max(cmax, axis=-1, keepdims=True))
    alpha = jnp.exp2(m_prev - m_new)
    ps = [jnp.exp2(c - m_new) for c in chunks]
    lsum = ps[0]
    for p in ps[1:]:
        lsum = lsum + p
    l_ref[h] = alpha * l_ref[h] + lsum
    p = jnp.concatenate(ps, axis=-1).astype(BF16)
    pv = _dot_nt(p, v) if v_seq_minor else _dot(p, v)
    acc_ref[h] = alpha[:, :pv.shape[1]] * acc_ref[h] + pv
    m_ref[h] = m_new


def _flash_result(l_ref, acc_ref, pair):
    outs = [acc_ref[2 * pair + hh] / jnp.sum(l_ref[2 * pair + hh], axis=-1, keepdims=True)
            for hh in range(2)]
    return jnp.where(_head_mask(0), outs[0], outs[1])


def _fox_prompt_kernel(qi_ref, kj_ref, q_ref, k_ref, v_ref, o_ref, qh_sc, m_sc, l_sc, acc_sc, *, tq, tk):
    t = pl.program_id(1)
    i = qi_ref[t]
    j = kj_ref[t]

    @pl.when(j == 0)
    def _():
        for p in range(N_PAIRS):
            q = q_ref[p]
            for hh in range(2):
                qh_sc[2 * p + hh] = jnp.where(_aug_head_mask(hh, p), q, jnp.zeros_like(q))
        m_sc[...] = jnp.full_like(m_sc, NEG)
        l_sc[...] = jnp.zeros_like(l_sc)
        acc_sc[...] = jnp.zeros_like(acc_sc)

    def step(diagonal):
        if diagonal:
            keep = (lax.broadcasted_iota(jnp.int32, (tq, tk), 1)
                    <= lax.broadcasted_iota(jnp.int32, (tq, tk), 0))
        for p in range(N_PAIRS):
            k = k_ref[p]
            v = v_ref[:, p * LANES:(p + 1) * LANES]
            for hh in range(2):
                s = _dot_nt(qh_sc[2 * p + hh], k)
                if diagonal:
                    s = jnp.where(keep, s, NEG)
                _flash_update(s, v, m_sc, l_sc, acc_sc, 2 * p + hh)

    @pl.when(j < i)
    def _():
        step(False)

    @pl.when(j == i)
    def _():
        step(True)
        for p in range(N_PAIRS):
            o_ref[:, p * LANES:(p + 1) * LANES] = _flash_result(l_sc, acc_sc, p).astype(BF16)


def _fox_prompt(qcat, kcat, vb, n_seq):
    seq = qcat.shape[2]
    tq = tk = 1024
    nt = seq // tq
    qi = np.concatenate([np.full(i + 1, i) for i in range(nt)]).astype(np.int32)
    kj = np.concatenate([np.arange(i + 1) for i in range(nt)]).astype(np.int32)
    grid_spec = pltpu.PrefetchScalarGridSpec(
        num_scalar_prefetch=2,
        grid=(n_seq, len(qi)),
        in_specs=[
            pl.BlockSpec((None, N_PAIRS, tq, 2 * LANES), lambda b, t, qi, kj: (b, 0, qi[t], 0)),
            pl.BlockSpec((None, N_PAIRS, tk, 2 * LANES), lambda b, t, qi, kj: (b, 0, kj[t], 0)),
            pl.BlockSpec((tk, WIDTH), lambda b, t, qi, kj: (b * nt + kj[t], 0)),
        ],
        out_specs=pl.BlockSpec((tq, WIDTH), lambda b, t, qi, kj: (b * nt + qi[t], 0)),
        scratch_shapes=[pltpu.VMEM((N_HEADS, tq, 2 * LANES), BF16), pltpu.VMEM((N_HEADS, tq, LANES), F32),
                        pltpu.VMEM((N_HEADS, tq, LANES), F32), pltpu.VMEM((N_HEADS, tq, LANES), F32)],
    )
    return pl.pallas_call(
        functools.partial(_fox_prompt_kernel, tq=tq, tk=tk),
        grid_spec=grid_spec,
        out_shape=jax.ShapeDtypeStruct(vb.shape, BF16),
        compiler_params=_params(2),
        name="fox_prompt",
    )(jnp.asarray(qi), jnp.asarray(kj), qcat, kcat, vb)


def _suffix_sum_exclusive(x):
    n = x.shape[1]
    lane = lax.broadcasted_iota(jnp.int32, x.shape, 1)
    y = jnp.where(lane + 1 < n, pltpu.roll(x, n - 1, axis=1), 0.0)
    shift = 1
    while shift < n:
        y = y + jnp.where(lane + shift < n, pltpu.roll(y, n - shift, axis=1), 0.0)
        shift *= 2
    return y


def _fox_sample_kernel(q_ref, kc_ref, vc_ref, kn_ref, vn_ref, lfc_ref, lfn_ref, u_ref,
                       o_ref, cq_sc, car_sc, m_sc, l_sc, acc_sc, *, n_cache_tiles, sub_keys):
    j = pl.program_id(1)
    t = q_ref.shape[1]

    def cum_new():
        hi, mid, lo = _split3(lfn_ref[...])
        u = u_ref[...]
        return _dot(hi, u) + _dot(mid, u) + _dot(lo, u)

    def q_head(h):
        return q_ref[h // 2, :, (h % 2) * HEAD_DIM:(h % 2 + 1) * HEAD_DIM]

    @pl.when(j == 0)
    def _():
        cn = cum_new() * LOG2E
        eye = (lax.broadcasted_iota(jnp.int32, (t, LANES), 0)
               == lax.broadcasted_iota(jnp.int32, (t, LANES), 1))
        for h in range(N_HEADS):
            col = jnp.sum(jnp.where(eye, jnp.broadcast_to(cn[h:h + 1, :], (t, LANES)), 0.0),
                          axis=-1, keepdims=True)
            cq_sc[h] = jnp.broadcast_to(col, (t, LANES))
        car_sc[...] = jnp.zeros_like(car_sc)
        m_sc[...] = jnp.full_like(m_sc, NEG)
        l_sc[...] = jnp.zeros_like(l_sc)
        acc_sc[...] = jnp.zeros_like(acc_sc)

    @pl.when(j < n_cache_tiles)
    def _():
        lf = lfc_ref[...]
        r = _suffix_sum_exclusive(lf) + car_sc[:, 0:1]
        car_sc[...] = jnp.broadcast_to(r[:, 0:1] + lf[:, 0:1], car_sc.shape)
        r = r * LOG2E
        for sub in range(lf.shape[1] // sub_keys):
            keys = slice(sub * sub_keys, (sub + 1) * sub_keys)
            for h in range(N_HEADS):
                s = _dot(q_head(h), kc_ref[h, :, keys].astype(BF16)) + r[h:h + 1, keys]
                _flash_update(s, vc_ref[h, :, keys].astype(BF16), m_sc, l_sc, acc_sc, h,
                              row_bias=cq_sc[h], v_seq_minor=True)

    @pl.when(j == n_cache_tiles)
    def _():
        cn = cum_new() * LOG2E
        causal = (lax.broadcasted_iota(jnp.int32, (t, LANES), 1)
                  <= lax.broadcasted_iota(jnp.int32, (t, LANES), 0))
        pad = jnp.zeros((LANES - t, HEAD_DIM), BF16)
        for h in range(N_HEADS):
            cols = slice(h * HEAD_DIM, (h + 1) * HEAD_DIM)
            k = jnp.concatenate([kn_ref[h // 2, :, (h % 2) * HEAD_DIM:(h % 2 + 1) * HEAD_DIM], pad], axis=0)
            v = jnp.concatenate([vn_ref[:, cols], pad], axis=0)
            s = _dot_nt(q_head(h), k) + cq_sc[h] - cn[h:h + 1, :]
            _flash_update(jnp.where(causal, s, NEG), v, m_sc, l_sc, acc_sc, h)
            o_ref[:, cols] = (acc_sc[h] / jnp.sum(l_sc[h], axis=-1, keepdims=True)).astype(BF16)


def _fox_sample(qcat, kcat, vb, cache_kt, cache_vt, lf_cache_t, lf_new_t):
    n, _, _, p_rows = cache_kt.shape
    t = vb.shape[0] // n
    tk = 2048
    nct = p_rows // tk
    u = jnp.asarray(np.pad(np.triu(np.ones((t, t))), ((0, 0), (0, LANES - t))), BF16)
    rev = lambda j: jnp.maximum(nct - 1 - j, 0)
    cache = pl.BlockSpec((None, N_HEADS, HEAD_DIM, tk), lambda b, j: (b, 0, 0, rev(j)))
    new_cat = pl.BlockSpec((None, N_PAIRS, t, 2 * LANES), lambda b, j: (0, 0, b, 0))
    return pl.pallas_call(
        functools.partial(_fox_sample_kernel, n_cache_tiles=nct, sub_keys=tk),
        grid=(n, nct + 1),
        in_specs=[
            new_cat, cache, cache, new_cat,
            pl.BlockSpec((t, WIDTH), lambda b, j: (b, 0)),
            pl.BlockSpec((None, N_HEADS, tk), lambda b, j: (b, 0, rev(j))),
            pl.BlockSpec((None, N_HEADS, t), lambda b, j: (b, 0, 0)),
            _const_spec(u.shape),
        ],
        out_specs=pl.BlockSpec((t, WIDTH), lambda b, j: (b, 0)),
        out_shape=jax.ShapeDtypeStruct(vb.shape, BF16),
        scratch_shapes=[pltpu.VMEM((N_HEADS, t, LANES), F32), pltpu.VMEM((N_HEADS, LANES), F32),
                        pltpu.VMEM((N_HEADS, t, LANES), F32), pltpu.VMEM((N_HEADS, t, LANES), F32),
                        pltpu.VMEM((N_HEADS, t, HEAD_DIM), F32)],
        compiler_params=_params(2),
        name="fox_sample",
    )(qcat, cache_kt, cache_vt, kcat, vb, lf_cache_t, lf_new_t, u)


GROUP_LANE = 64
ROW_ALIGN = 16
MOE_TILE = 512
SORT_ROWS = 640
XS_WIDTH = D_MODEL + 2 * LANES


def _route(r):
    lane_i = lax.broadcasted_iota(jnp.int32, r.shape, 1)
    lane = lane_i.astype(F32)
    lane_grp = (lane_i // EXPERTS_PER_GROUP).astype(F32)
    big = float(LANES)
    is_coarse = (lane_i >= N_EXPERTS) & (lane_i < N_EXPERTS + N_GROUPS)
    cm = jnp.where(is_coarse, r, NEG)
    cmax = cm.max(axis=-1, keepdims=True)
    grp = jnp.min(jnp.where(cm == cmax, lane - N_EXPERTS, big), axis=-1, keepdims=True)
    pg_sel = 1.0 / jnp.sum(jnp.exp(cm - cmax), axis=-1, keepdims=True)
    in_grp = (lane_i < N_EXPERTS) & (lane_grp == grp)
    fm = jnp.where(in_grp, r, NEG)
    m1 = fm.max(axis=-1, keepdims=True)
    denom = jnp.sum(jnp.exp(fm - m1), axis=-1, keepdims=True)
    i1 = jnp.min(jnp.where(fm == m1, lane, big), axis=-1, keepdims=True)
    fm2 = jnp.where(lane == i1, NEG, fm)
    m2 = fm2.max(axis=-1, keepdims=True)
    i2 = jnp.min(jnp.where(fm2 == m2, lane, big), axis=-1, keepdims=True)
    p1 = 1.0 / denom
    p2 = jnp.exp(m2 - m1) / denom
    tot = p1 + p2
    comb = (jnp.where(lane == i1, pg_sel * (p1 / tot), 0.0)
            + jnp.where(lane == i2, pg_sel * (p2 / tot), 0.0))
    return comb, grp


def _merge_kernel(x_ref, oa_ref, ob_ref, g_ref, wg_ref, wpa_ref, wpb_ref, wo_ref, gf_ref, wrh_ref, wrm_ref,
                  br_ref, y_ref, hx_ref, route_ref, grp_t_ref, cnt_ref):
    x = x_ref[...]
    h = _rms(x, g_ref[...]).astype(BF16)
    gate = jax.nn.sigmoid(_dot(h, wg_ref[...]))
    mix = (gate[:, :D_MODEL] * _dot(oa_ref[...], wpa_ref[...])
           + gate[:, D_MODEL:] * _dot(ob_ref[...], wpb_ref[...]))
    y = x + _dot(mix.astype(BF16), wo_ref[...])
    y_ref[...] = y

    hx = _rms(y, gf_ref[...])
    hx_ref[:, 0:D_MODEL] = hx.astype(BF16)
    h_hi, h_mid, _ = _split3(hx)
    r = _dot(h_hi, wrh_ref[...]) + _dot(h_hi, wrm_ref[...]) + _dot(h_mid, wrh_ref[...]) + br_ref[...]
    comb, grp = _route(r)
    lane = lax.broadcasted_iota(jnp.int32, comb.shape, 1)
    route = jnp.where(lane == GROUP_LANE, grp, comb)
    route_ref[...] = route
    r_hi = route.astype(BF16)
    hx_ref[:, D_MODEL:D_MODEL + LANES] = r_hi
    hx_ref[:, D_MODEL + LANES:] = (route - r_hi.astype(F32)).astype(BF16)
    grp_t_ref[...] = route.T[GROUP_LANE:GROUP_LANE + 8, :]
    cnt = jnp.sum(jnp.where(lane.astype(F32) == grp, 1.0, 0.0), axis=0, keepdims=True)
    cnt_ref[...] = jnp.broadcast_to(cnt, cnt_ref.shape)


def _merge(x, oa, ob, w):
    t_total = x.shape[0]
    tm = MOE_TILE
    n_tiles = t_total // tm
    row = lambda n: pl.BlockSpec((tm, n), lambda i: (i, 0))
    consts = [w["g_mix"], w["w_gate"], w["w_pa"], w["w_pb"], w["w_o"], w["g_ffn"], w["w_router_hi"],
              w["w_router_mid"], w["b_router"]]
    return pl.pallas_call(
        _merge_kernel,
        grid=(n_tiles,),
        in_specs=[row(D_MODEL), row(WIDTH), row(WIDTH)] + [_const_spec(c.shape) for c in consts],
        out_specs=[row(D_MODEL), row(XS_WIDTH), row(LANES), pl.BlockSpec((8, tm), lambda i: (0, i)),
                   pl.BlockSpec((None, 8, LANES), lambda i: (i, 0, 0))],
        out_shape=[jax.ShapeDtypeStruct(x.shape, F32), jax.ShapeDtypeStruct((t_total, XS_WIDTH), BF16),
                   jax.ShapeDtypeStruct((t_total, LANES), F32), jax.ShapeDtypeStruct((8, t_total), F32),
                   jax.ShapeDtypeStruct((n_tiles, 8, LANES), F32)],
        compiler_params=_params(1),
        name="merge",
    )(x, oa, ob, *consts)


def _plan_kernel(cnt_ref, ib_ref, dst_ref, len_ref, fill_ref, inblk_ref, outblk_ref, grp_ref, kind_ref, *,
                 n_tiles, n_steps):
    align_bits = ROW_ALIGN.bit_length() - 1
    tile_bits = MOE_TILE.bit_length() - 1
    zero = jnp.int32(0)

    def per_tile(t, tot):
        run = zero
        new_tot = []
        for g in range(N_GROUPS):
            n = ((cnt_ref[t * N_GROUPS + g] + (ROW_ALIGN - 1)) >> align_bits) << align_bits
            ib_ref[t * N_GROUPS + g] = run
            len_ref[t * N_GROUPS + g] = n
            dst_ref[t * N_GROUPS + g] = tot[g]
            run = run + n
            new_tot.append(tot[g] + n)
        return tuple(new_tot)

    tot = lax.fori_loop(0, n_tiles, per_tile, (zero,) * N_GROUPS)
    n_ffn = [(tot[g] + (MOE_TILE - 1)) >> tile_bits for g in range(N_GROUPS)]
    g_base, ends, acc_rows, acc_tiles = [], [], zero, zero
    for g in range(N_GROUPS):
        g_base.append(acc_rows)
        acc_rows = acc_rows + ((n_ffn[g] + 1) << tile_bits)
        acc_tiles = acc_tiles + n_ffn[g] + 1
        ends.append(acc_tiles)

    def add_base(t, carry):
        for g in range(N_GROUPS):
            dst_ref[t * N_GROUPS + g] = dst_ref[t * N_GROUPS + g] + g_base[g]
        return carry

    lax.fori_loop(0, n_tiles, add_base, zero)
    for g in range(N_GROUPS):
        fill_ref[g] = g_base[g] + tot[g]
        fill_ref[N_GROUPS + g] = g_base[g] + (n_ffn[g] << tile_bits)
    fill_ref[2 * N_GROUPS] = ends[-1]
    first_blk = g_base[N_GROUPS - 1]
    for g in reversed(range(N_GROUPS - 1)):
        first_blk = jnp.where(n_ffn[g] > 0, g_base[g], first_blk)
    first_blk = first_blk >> tile_bits

    def pick(vals, g):
        out = vals[N_GROUPS - 1]
        for i in reversed(range(N_GROUPS - 1)):
            out = jnp.where(g == i, vals[i], out)
        return out

    def per_step(k, carry):
        kk = jnp.minimum(k, ends[-1] - 1)
        g = zero
        for i in range(N_GROUPS - 1):
            g = g + jnp.where(kk >= ends[i], 1, 0)
        j = kk - pick([zero] + ends[:-1], g)
        is_expert = j < pick(n_ffn, g)
        out_blk = jnp.where(k < ends[-1], (pick(g_base, g) >> tile_bits) + j, k)
        outblk_ref[k] = out_blk
        inblk_ref[k] = jnp.where(is_expert, out_blk, first_blk)
        grp_ref[k] = g
        kind_ref[k] = jnp.where(is_expert, 1, 0)
        return carry

    lax.fori_loop(0, n_steps, per_step, zero)


def _moe_plan(cnt, t_total):
    n_tiles = t_total // MOE_TILE
    n_steps = (n_tiles + 2 * N_GROUPS + (N_GROUPS * (ROW_ALIGN - 1) * n_tiles + MOE_TILE - 1) // MOE_TILE)
    smem = pl.BlockSpec(memory_space=pltpu.SMEM)
    sizes = [n_tiles * N_GROUPS] * 3 + [2 * N_GROUPS + 1] + [n_steps] * 4
    outs = pl.pallas_call(
        functools.partial(_plan_kernel, n_tiles=n_tiles, n_steps=n_steps),
        grid_spec=pltpu.PrefetchScalarGridSpec(num_scalar_prefetch=1, grid=(1,), in_specs=[],
                                               out_specs=[smem] * len(sizes)),
        out_shape=[jax.ShapeDtypeStruct((n,), jnp.int32) for n in sizes],
        compiler_params=_params(1),
        name="moe_plan",
    )(cnt.reshape(-1))
    return (*outs, n_steps)


def _sort_kernel(ib_ref, dst_ref, len_ref, fill_ref, hx_ref, grp_t_ref, tri_ref, xs_ref, xsort, sems):
    i = pl.program_id(0)
    n = pl.num_programs(0)
    tm = hx_ref.shape[0]
    half = tm // 2
    slot = i % 2

    def copy(src, dst, rows, sl, g, part):
        return pltpu.make_async_copy(xsort.at[sl, pl.ds(pl.multiple_of(src, ROW_ALIGN), rows)],
                                     xs_ref.at[pl.ds(pl.multiple_of(dst, ROW_ALIGN), rows)], sems.at[sl, g, part])

    def for_segments(step, sl, act):
        for g in range(N_GROUPS):
            src, dst = ib_ref[step * N_GROUPS + g], dst_ref[step * N_GROUPS + g]
            act(copy(src, dst, half, sl, g, 0))

            @pl.when(len_ref[step * N_GROUPS + g] > half)
            def _():
                act(copy(src + half, dst + half, half, sl, g, 1))

    def run(cs):
        for c in cs:
            c.start()
        for c in cs:
            c.wait()

    @pl.when(i == 0)
    def _():
        for sl in range(2):
            xsort[sl, SORT_ROWS:, :] = jnp.zeros((xsort.shape[1] - SORT_ROWS, XS_WIDTH), BF16)

    g_row = grp_t_ref[0:1, :]
    sub = lax.broadcasted_iota(jnp.int32, (8, tm), 0).astype(F32)
    mine = sub == g_row
    before = _dot(jnp.where(mine, 1.0, 0.0).astype(BF16), tri_ref[...])
    dest = jnp.sum(jnp.where(mine, before, 0.0), axis=0, keepdims=True)
    for g in range(N_GROUPS):
        base = jnp.full(dest.shape, ib_ref[i * N_GROUPS + g], jnp.int32).astype(F32)
        dest = dest + jnp.where(g_row == g, base, 0.0)
    rows = lax.broadcasted_iota(jnp.int32, (SORT_ROWS, tm), 0).astype(F32)
    perm = jnp.where(rows == dest, 1.0, 0.0).astype(BF16)
    xsort[slot, 0:SORT_ROWS, :] = _dot(perm, hx_ref[...]).astype(BF16)

    @pl.when(i > 0)
    def _():
        for_segments(i - 1, 1 - slot, lambda c: c.wait())

    for_segments(i, slot, lambda c: c.start())

    @pl.when(i == n - 1)
    def _():
        for_segments(i, slot, lambda c: c.wait())
        run([copy(SORT_ROWS, fill_ref[g], tm, slot, g, 0) for g in range(N_GROUPS)])
        run([copy(SORT_ROWS, fill_ref[N_GROUPS + g], tm, slot, g, 0) for g in range(N_GROUPS)])

        @pl.loop(fill_ref[2 * N_GROUPS], xs_ref.shape[0] // tm)
        def _(b):
            run([copy(SORT_ROWS, b * tm, tm, slot, 0, 0)])


def _expert_kernel(ib_ref, ob_ref, grp_ref, kind_ref, xs_ref, w1_ref, w3_ref, w2_ref, ys_ref):
    del ib_ref, ob_ref
    k = pl.program_id(0)

    @pl.when(kind_ref[k] == 1)
    def _():
        x = xs_ref[:, 0:D_MODEL]
        side = lambda w_ref: jnp.concatenate([w_ref[e].astype(BF16) for e in range(EXPERTS_PER_GROUP)], axis=1)
        hid = jax.nn.silu(_dot(x, side(w1_ref))) * _dot(x, side(w3_ref))
        comb = (xs_ref[:, D_MODEL:D_MODEL + LANES].astype(F32) + xs_ref[:, D_MODEL + LANES:].astype(F32))
        lane = lax.broadcasted_iota(jnp.int32, comb.shape, 1)
        first = grp_ref[k] * EXPERTS_PER_GROUP
        scale = []
        for e in range(EXPERTS_PER_GROUP):
            w_e = jnp.sum(jnp.where(lane == first + e, comb, 0.0), axis=-1, keepdims=True)
            scale.append(jnp.broadcast_to(w_e, (comb.shape[0], D_EXPERT)))
        hid = hid * jnp.concatenate(scale, axis=-1)
        w2 = w2_ref[...].astype(BF16).reshape(EXPERTS_PER_GROUP * D_EXPERT, D_MODEL)
        ys_ref[...] = _dot(hid.astype(BF16), w2).astype(BF16)

    @pl.when(kind_ref[k] == 0)
    def _():
        ys_ref[...] = jnp.zeros_like(ys_ref)


def _unsort_kernel(ib_ref, dst_ref, len_ref, y_ref, route_ref, tri_ref, ys_ref, o_ref, ybuf, yasm, sems):
    i = pl.program_id(0)
    n = pl.num_programs(0)
    tm = y_ref.shape[0]
    half = tm // 2
    slot = i % 2

    def fetch(step, sl, act):
        for g in range(N_GROUPS):
            src = pl.multiple_of(dst_ref[step * N_GROUPS + g], ROW_ALIGN)
            act(pltpu.make_async_copy(ys_ref.at[pl.ds(src, half)], ybuf.at[sl, g, 0:half], sems.at[sl, g, 0]))

            @pl.when(len_ref[step * N_GROUPS + g] > half)
            def _():
                act(pltpu.make_async_copy(ys_ref.at[pl.ds(src + half, half)], ybuf.at[sl, g, half:tm],
                                          sems.at[sl, g, 1]))

    @pl.when(i == 0)
    def _():
        yasm[...] = jnp.zeros_like(yasm)
        fetch(0, 0, lambda c: c.start())

    @pl.when(i + 1 < n)
    def _():
        fetch(i + 1, 1 - slot, lambda c: c.start())

    route = route_ref[...]
    lane = lax.broadcasted_iota(jnp.int32, route.shape, 1).astype(F32)
    grp = jnp.sum(jnp.where(lane == GROUP_LANE, route, 0.0), axis=-1, keepdims=True)
    mine = lane == grp
    before = _dot(tri_ref[...], jnp.where(mine, 1.0, 0.0).astype(BF16))
    dest = jnp.sum(jnp.where(mine, before, 0.0), axis=-1, keepdims=True)
    for g in range(N_GROUPS):
        base = jnp.full(dest.shape, ib_ref[i * N_GROUPS + g], jnp.int32).astype(F32)
        dest = dest + jnp.where(grp == g, base, 0.0)
    cols = lax.broadcasted_iota(jnp.int32, (tm, SORT_ROWS), 1).astype(F32)
    perm_t = jnp.where(cols == dest, 1.0, 0.0).astype(BF16)

    fetch(i, slot, lambda c: c.wait())
    for g in range(N_GROUPS):
        row0 = pl.multiple_of(ib_ref[i * N_GROUPS + g], ROW_ALIGN)
        yasm[pl.ds(row0, half), :] = ybuf[slot, g, 0:half]

        @pl.when(len_ref[i * N_GROUPS + g] > half)
        def _():
            yasm[pl.ds(row0 + half, half), :] = ybuf[slot, g, half:tm]
    o_ref[...] = y_ref[...] + _dot(perm_t, yasm[0:SORT_ROWS, :])


def _moe(y, hx, route, grp_t, cnt, w):
    t_total = y.shape[0]
    tm = MOE_TILE
    n_tiles = t_total // tm
    assert SORT_ROWS >= tm + N_GROUPS * (ROW_ALIGN - 1)
    in_base, dst, seg_len, fill, in_blk, out_blk, grp_of_step, is_expert, n_steps = _moe_plan(
        cnt[:, 0, :N_GROUPS].astype(jnp.int32), t_total)
    cap_rows = n_steps * tm
    lower = jnp.asarray(np.tril(np.ones((tm, tm)), -1), BF16)
    upper = jnp.asarray(np.triu(np.ones((tm, tm)), 1), BF16)
    stage_rows = SORT_ROWS + tm
    any_spec = pl.BlockSpec(memory_space=pl.ANY)

    xs = pl.pallas_call(
        _sort_kernel,
        grid_spec=pltpu.PrefetchScalarGridSpec(
            num_scalar_prefetch=4,
            grid=(n_tiles,),
            in_specs=[pl.BlockSpec((tm, XS_WIDTH), lambda i, *_: (i, 0)),
                      pl.BlockSpec((8, tm), lambda i, *_: (0, i)),
                      pl.BlockSpec((tm, tm), lambda i, *_: (0, 0))],
            out_specs=any_spec,
            scratch_shapes=[pltpu.VMEM((2, stage_rows, XS_WIDTH), BF16),
                            pltpu.SemaphoreType.DMA((2, N_GROUPS, 2))],
        ),
        out_shape=jax.ShapeDtypeStruct((cap_rows, XS_WIDTH), BF16),
        compiler_params=_params(1),
        name="moe_sort",
    )(in_base, dst, seg_len, fill, hx, grp_t, upper)

    step_map = lambda which: (lambda k, ib, ob, grp, kind: ((ib, ob, grp)[which][k], 0))
    wmap = lambda k, ib, ob, grp, kind: (grp[k], 0, 0)
    wspec = pl.BlockSpec((EXPERTS_PER_GROUP, D_MODEL, D_EXPERT), wmap)
    ys = pl.pallas_call(
        _expert_kernel,
        grid_spec=pltpu.PrefetchScalarGridSpec(
            num_scalar_prefetch=4,
            grid=(n_steps,),
            in_specs=[pl.BlockSpec((tm, XS_WIDTH), step_map(0)), wspec, wspec, pl.BlockSpec((EXPERTS_PER_GROUP, D_EXPERT, D_MODEL), wmap)],
            out_specs=pl.BlockSpec((tm, D_MODEL), step_map(1)),
        ),
        out_shape=jax.ShapeDtypeStruct((cap_rows, D_MODEL), BF16),
        compiler_params=_params(1),
        name="moe_experts",
    )(in_blk, out_blk, grp_of_step, is_expert, xs, w["w1"], w["w3"], w["w2"])

    return pl.pallas_call(
        _unsort_kernel,
        grid_spec=pltpu.PrefetchScalarGridSpec(
            num_scalar_prefetch=3,
            grid=(n_tiles,),
            in_specs=[pl.BlockSpec((tm, D_MODEL), lambda i, *_: (i, 0)),
                      pl.BlockSpec((tm, LANES), lambda i, *_: (i, 0)),
                      pl.BlockSpec((tm, tm), lambda i, *_: (0, 0)),
                      any_spec],
            out_specs=pl.BlockSpec((tm, D_MODEL), lambda i, *_: (i, 0)),
            scratch_shapes=[pltpu.VMEM((2, N_GROUPS, tm, D_MODEL), BF16),
                            pltpu.VMEM((stage_rows, D_MODEL), BF16),
                            pltpu.SemaphoreType.DMA((2, N_GROUPS, 2))],
        ),
        out_shape=jax.ShapeDtypeStruct(y.shape, F32),
        compiler_params=_params(1),
        name="moe_unsort",
    )(in_base, dst, seg_len, y, route, lower, ys)


def _prep_weights(g_mix, w_in, b_f, q_norm_a, k_norm_a, q_norm_b, k_norm_b, w_pa, w_pb, w_o,
                  g_ffn, w_rg, b_rg, w_re, b_re, w1, w3, w2):
    n_qkv = 6 * WIDTH
    tile = lambda g: jnp.tile(g, N_HEADS)
    w_router = jnp.concatenate(
        [jnp.transpose(w_re, (1, 0, 2)).reshape(D_MODEL, N_EXPERTS), w_rg,
         jnp.zeros((D_MODEL, LANES - N_EXPERTS - N_GROUPS), F32)], axis=1)
    b_router = jnp.concatenate(
        [b_re.reshape(N_EXPERTS), b_rg, jnp.zeros((LANES - N_EXPERTS - N_GROUPS,), F32)])[None, :]
    return {
        "g_mix": g_mix[None, :],
        "w_qkv": w_in[:, :n_qkv].astype(BF16),
        "w_f": jnp.pad(w_in[:, n_qkv:n_qkv + N_HEADS], ((0, 0), (0, LANES - N_HEADS))).astype(BF16),
        "b_f": jnp.pad(b_f, (0, LANES - N_HEADS))[None, :],
        "w_gate": w_in[:, n_qkv + N_HEADS:].astype(BF16),
        "gains": jnp.stack([tile(q_norm_a), tile(k_norm_a), tile(q_norm_b), tile(k_norm_b)]),
        "w_pa": w_pa.astype(BF16), "w_pb": w_pb.astype(BF16), "w_o": w_o.astype(BF16),
        "g_ffn": g_ffn[None, :],
        "w_router_hi": w_router.astype(BF16),
        "w_router_mid": (w_router - w_router.astype(BF16).astype(F32)).astype(BF16),
        "b_router": b_router,
        "w1": w1, "w3": w3, "w2": w2,
    }


def kernel(x_prompt, x_sample, cache_a_k, cache_a_v, cache_b_k, cache_b_v, cache_b_logf, g_mix, w_in, b_f, q_norm_a, k_norm_a, q_norm_b, k_norm_b, rel_bias, w_pa, w_pb, w_o, g_ffn, w_rg, b_rg, w_re, b_re, w1, w3, w2):
    assert g_mix.shape[0] == 1, "single-layer step"
    n_p, seq, _ = x_prompt.shape
    n_s, t_s, _ = x_sample.shape
    a_rows = cache_a_k.shape[2]
    w = _prep_weights(g_mix[0], w_in[0], b_f[0], q_norm_a[0], k_norm_a[0], q_norm_b[0], k_norm_b[0],
                      w_pa[0], w_pb[0], w_o[0], g_ffn[0], w_rg[0], b_rg[0], w_re[0], b_re[0],
                      w1[0], w3[0], w2[0])
    band_tq = 256
    bias_prompt = _band_bias(rel_bias[0], band_tq, 3 * band_tq, 2 * band_tq, True, 3 * band_tq)
    bias_sample = _band_bias(rel_bias[0], t_s, a_rows + LANES, a_rows, False, a_rows + t_s)

    seq_minor = lambda a: jnp.transpose(a, (0, 2, 3, 1))
    seq_major = lambda a: jnp.transpose(a, (0, 3, 1, 2))[None]

    xp = x_prompt.reshape(n_p * seq, D_MODEL)
    (qa, ka, va, ka_t, va_t, qcat, kcat, vb, kb_t, vb_t, logf_t) = _inproj(xp, n_p, w, seq_minor=True)
    o_a = _band_prompt(qa, ka, va, *bias_prompt, n_p)
    o_b = _fox_prompt(qcat, kcat, vb, n_p)
    y_p = _moe(*_merge(xp, o_a, o_b, w), w)

    xs = x_sample.reshape(n_s * t_s, D_MODEL)
    (qa_s, _, _, ka_fs, va_fs, qcat_s, kcat_s, vb_s, kb_fs, vb_fs, logf_s) = _inproj(
        xs, 1, w, seq_minor=False)
    o_as, new_ak, new_av = _band_sample(
        qa_s, ka_fs, va_fs, cache_a_k[0].reshape(n_s, a_rows, WIDTH),
        cache_a_v[0].reshape(n_s, a_rows, WIDTH), *bias_sample)
    lf_cache_t = jnp.transpose(cache_b_logf[0], (0, 2, 1))
    lf_new_t = jnp.transpose(logf_s.reshape(n_s, t_s, N_HEADS), (0, 2, 1))
    o_bs = _fox_sample(qcat_s, kcat_s, vb_s, seq_minor(cache_b_k[0]), seq_minor(cache_b_v[0]),
                       lf_cache_t, lf_new_t)
    y_s = _moe(*_merge(xs, o_as, o_bs, w), w)

    heads = lambda a, n, r: a.reshape(1, n, r, N_HEADS, HEAD_DIM)
    return (y_p.reshape(n_p, seq, D_MODEL), y_s.reshape(n_s, t_s, D_MODEL),
            seq_major(ka_t), seq_major(va_t), seq_major(kb_t), seq_major(vb_t),
            jnp.transpose(logf_t, (0, 2, 1))[None],
            heads(new_ak, n_s, a_rows), heads(new_av, n_s, a_rows),
            heads(kb_fs, n_s, t_s), heads(vb_fs, n_s, t_s), logf_s.reshape(1, n_s, t_s, N_HEADS))
```

```python
import functools

import numpy as np
import jax
import jax.numpy as jnp
from jax import lax
from jax.experimental import pallas as pl
from jax.experimental.pallas import tpu as pltpu

F32 = jnp.float32
BF16 = jnp.bfloat16

D_MODEL = 1024
HEAD_DIM = 64
N_HEADS = 8
WIDTH = N_HEADS * HEAD_DIM
N_PAIRS = N_HEADS // 2
CHUNK = 64
LEFT_CHUNKS = 8
WINDOW_ROWS = LEFT_CHUNKS * CHUNK
MAX_REL = 256
N_GROUPS = 4
EXPERTS_PER_GROUP = 8
N_EXPERTS = N_GROUPS * EXPERTS_PER_GROUP
D_EXPERT = 128
EPS = 1e-6
NEG = -1e30
LOG2E = 1.4426950408889634
LANES = 128
BIAS_PERIOD = 1024
AUG_STRIDE = 8
VMEM_LIMIT = 56 * 1024 * 1024

_NT = (((1,), (1,)), ((), ()))


def _dot(a, b):
    return jnp.dot(a, b, preferred_element_type=F32)


def _dot_nt(a, b):
    return lax.dot_general(a, b, _NT, preferred_element_type=F32)


def _split3(x):
    hi = x.astype(BF16)
    r = x - hi.astype(F32)
    mid = r.astype(BF16)
    lo = (r - mid.astype(F32)).astype(BF16)
    return hi, mid, lo


def _dot3(a_bf, x):
    hi, mid, lo = _split3(x)
    return _dot(a_bf, hi) + _dot(a_bf, mid) + _dot(a_bf, lo)


def _rms(x, g):
    ms = jnp.mean(x * x, axis=-1, keepdims=True)
    return x * lax.rsqrt(ms + EPS) * g


def _params(n_axes):
    return pltpu.CompilerParams(dimension_semantics=("arbitrary",) * n_axes,
                                vmem_limit_bytes=VMEM_LIMIT)


def _const_spec(shape):
    nd = len(shape)
    return pl.BlockSpec(shape, lambda *_: (0,) * nd)


def _inproj_kernel(x_ref, g_ref, wqkv_ref, wf_ref, bf_ref, gains_ref, bd_ref, ltri_ref,
                   selq_ref, selk_ref, oneq_ref, onek_ref,
                   qa_ref, ka_ref, va_ref, kaf_ref, vaf_ref, qcat_ref, kcat_ref, vb_ref,
                   kbf_ref, vbf_ref, logf_ref, carry_ref, *, tiles_per_seq, seq_minor):
    i = pl.program_id(0)
    tm = x_ref.shape[0]

    @pl.when(i % tiles_per_seq == 0)
    def _():
        carry_ref[...] = jnp.zeros_like(carry_ref)

    def put_f32(ref, y, last_tile_only=False):
        if not seq_minor:
            ref[...] = y
        elif last_tile_only:
            @pl.when(i % tiles_per_seq == tiles_per_seq - 1)
            def _():
                ref[...] = y.T.reshape(N_HEADS, HEAD_DIM, tm)
        else:
            ref[...] = y.T.reshape(N_HEADS, HEAD_DIM, tm)

    h = _rms(x_ref[...], g_ref[...]).astype(BF16)

    def seg(s):
        return _dot(h, wqkv_ref[:, s * WIDTH:(s + 1) * WIDTH])

    def headnorm(y, n):
        ss = _dot((y * y).astype(BF16), bd_ref[...])
        return y * lax.rsqrt(ss * (1.0 / HEAD_DIM) + EPS) * gains_ref[n:n + 1, :]

    q_a = headnorm(seg(0), 0)
    qa_ref[...] = (q_a * (HEAD_DIM ** -0.5 * LOG2E)).astype(BF16)
    k_a = headnorm(seg(1), 1)
    ka_ref[...] = k_a.astype(BF16)
    put_f32(kaf_ref, k_a, last_tile_only=True)
    v_a = seg(2)
    va_ref[...] = v_a.astype(BF16)
    put_f32(vaf_ref, v_a, last_tile_only=True)

    z = _dot(h, wf_ref[...]) + bf_ref[...]
    logf = jnp.minimum(z, 0.0) - jnp.log(1.0 + jnp.exp(-jnp.abs(z)))
    if seq_minor:
        logf_ref[...] = logf.T[0:N_HEADS, :]
    else:
        logf_ref[...] = logf[:, :N_HEADS]
    c = _dot3(ltri_ref[...], logf) + carry_ref[0:1, :]
    carry_ref[...] = jnp.broadcast_to(c[-1:, :], carry_ref.shape)
    cs = jnp.concatenate(_split3(c * LOG2E), axis=-1)
    q_aug = (_dot(cs, selq_ref[...]) + oneq_ref[...]).astype(BF16)
    k_aug = (_dot(cs, selk_ref[...]) + onek_ref[...]).astype(BF16)

    q_b = (headnorm(seg(3), 2) * (HEAD_DIM ** -0.5 * LOG2E)).astype(BF16)
    k_b = headnorm(seg(4), 3)
    put_f32(kbf_ref, k_b)
    k_b = k_b.astype(BF16)
    v_b = seg(5)
    put_f32(vbf_ref, v_b)
    vb_ref[...] = v_b.astype(BF16)
    for p in range(N_PAIRS):
        cols = slice(p * LANES, (p + 1) * LANES)
        qcat_ref[p, :, 0:LANES] = q_b[:, cols]
        qcat_ref[p, :, LANES:2 * LANES] = q_aug
        kcat_ref[p, :, 0:LANES] = k_b[:, cols]
        kcat_ref[p, :, LANES:2 * LANES] = k_aug


def _aug_constants():
    selq = np.zeros((3 * LANES, LANES), np.float32)
    selk = np.zeros((3 * LANES, LANES), np.float32)
    oneq = np.zeros((1, LANES), np.float32)
    onek = np.zeros((1, LANES), np.float32)
    for h in range(N_HEADS):
        for k in range(3):
            selq[k * LANES + h, AUG_STRIDE * h + k] = 1.0
            selk[k * LANES + h, AUG_STRIDE * h + 3 + k] = -1.0
            oneq[0, AUG_STRIDE * h + 3 + k] = 1.0
            onek[0, AUG_STRIDE * h + k] = 1.0
    return (jnp.asarray(selq, BF16), jnp.asarray(selk, BF16), jnp.asarray(oneq), jnp.asarray(onek))


def _inproj(x, n_seq, w, seq_minor):
    t_total = x.shape[0]
    tm = 512
    n_tiles = t_total // tm
    seq = t_total // n_seq
    tps = seq // tm
    row = lambda i: (i, 0)
    flat = lambda dt: (jax.ShapeDtypeStruct((t_total, WIDTH), dt), pl.BlockSpec((tm, WIDTH), row))
    if seq_minor:
        assert tm == WINDOW_ROWS
        band_f32 = (jax.ShapeDtypeStruct((n_seq, N_HEADS, HEAD_DIM, WINDOW_ROWS), F32),
                    pl.BlockSpec((None, N_HEADS, HEAD_DIM, tm), lambda i: (i // tps, 0, 0, 0)))
        fox_f32 = (jax.ShapeDtypeStruct((n_seq, N_HEADS, HEAD_DIM, seq), F32),
                   pl.BlockSpec((None, N_HEADS, HEAD_DIM, tm), lambda i: (i // tps, 0, 0, i % tps)))
        logf = (jax.ShapeDtypeStruct((n_seq, N_HEADS, seq), F32),
                pl.BlockSpec((None, N_HEADS, tm), lambda i: (i // tps, 0, i % tps)))
    else:
        band_f32 = fox_f32 = flat(F32)
        logf = (jax.ShapeDtypeStruct((t_total, N_HEADS), F32), pl.BlockSpec((tm, N_HEADS), row))
    cat = (jax.ShapeDtypeStruct((n_seq, N_PAIRS, seq, 2 * LANES), BF16),
           pl.BlockSpec((None, N_PAIRS, tm, 2 * LANES), lambda i: (i // tps, 0, i % tps, 0)))
    outs = [flat(BF16), flat(BF16), flat(BF16), band_f32, band_f32, cat, cat, flat(BF16),
            fox_f32, fox_f32, logf]
    bd = jnp.asarray(np.kron(np.eye(N_HEADS), np.ones((HEAD_DIM, HEAD_DIM))), BF16)
    ltri = jnp.asarray(np.tril(np.ones((tm, tm))), BF16)
    consts = [w["g_mix"], w["w_qkv"], w["w_f"], w["b_f"], w["gains"], bd, ltri, *_aug_constants()]
    in_specs = [pl.BlockSpec((tm, D_MODEL), row)] + [_const_spec(c.shape) for c in consts]
    return pl.pallas_call(
        functools.partial(_inproj_kernel, tiles_per_seq=tps, seq_minor=seq_minor),
        grid=(n_tiles,),
        in_specs=in_specs,
        out_specs=[o[1] for o in outs],
        out_shape=[o[0] for o in outs],
        scratch_shapes=[pltpu.VMEM((8, LANES), F32)],
        compiler_params=_params(1),
        name="inproj",
    )(x, *consts)


def _softmax_pv(scores, values):
    chunks = [[s[:, c * LANES:(c + 1) * LANES] for c in range(s.shape[1] // LANES)] for s in scores]
    cmax = None
    for cs in chunks:
        for c in cs:
            cmax = c if cmax is None else jnp.maximum(cmax, c)
    m = jnp.max(cmax, axis=-1, keepdims=True)
    pv = None
    lsum = None
    for cs, v in zip(chunks, values):
        ps = [jnp.exp2(c - m) for c in cs]
        for p in ps:
            lsum = p if lsum is None else lsum + p
        o = _dot(jnp.concatenate(ps, axis=-1).astype(BF16), v)
        pv = o if pv is None else pv + o
    return pv, jnp.sum(lsum, axis=-1, keepdims=True)


def _head_mask(hh):
    lane = lax.broadcasted_iota(jnp.int32, (1, LANES), 1)
    return (lane // HEAD_DIM) == hh


def _band_prompt_kernel(q_ref, k0_ref, k1_ref, k2_ref, k3_ref, v0_ref, v1_ref, v2_ref, v3_ref, e_ref, mask_ref,
                        o_ref, bias_ref, *, tq):
    i = pl.program_id(1)

    @pl.when((pl.program_id(0) == 0) & (i == 0))
    def _():
        _build_bias(e_ref, mask_ref, bias_ref)

    k_refs = (k0_ref, k1_ref, k2_ref, k3_ref)
    v_refs = (v0_ref, v1_ref, v2_ref, v3_ref)
    for qb in range(2):
        pad = [jnp.where(2 * i + qb - 2 + j >= 0, 0.0, NEG) for j in range(2)] + [0.0]
        for p in range(N_PAIRS):
            cols = slice(p * LANES, (p + 1) * LANES)
            q = q_ref[qb * tq:(qb + 1) * tq, cols]
            ks = [r[:, cols] for r in k_refs[qb:qb + 3]]
            vs = [r[:, cols] for r in v_refs[qb:qb + 3]]
            out = jnp.zeros((tq, LANES), F32)
            for hh in range(2):
                msk = _head_mask(hh)
                qh = jnp.where(msk, q, jnp.zeros_like(q))
                scores = [_dot_nt(qh, ks[j]) + bias_ref[2 * p + hh, :, j * tq:(j + 1) * tq] + pad[j]
                          for j in range(3)]
                pv, l = _softmax_pv(scores, vs)
                out = jnp.where(msk, pv / l, out)
            o_ref[qb * tq:(qb + 1) * tq, cols] = out.astype(BF16)


def _band_prompt(qa, ka, va, bias_e, bias_mask, n_seq):
    t_total = qa.shape[0]
    tq = 256
    seq = t_total // n_seq
    nq = seq // tq
    qmap = lambda b, i: (b * (nq // 2) + i, 0)

    def kmap(j):
        return lambda b, i: (b * nq + jnp.maximum(2 * i - 2 + j, 0), 0)

    blk = lambda m: pl.BlockSpec((tq, WIDTH), m)
    qblk = pl.BlockSpec((2 * tq, WIDTH), qmap)
    return pl.pallas_call(
        functools.partial(_band_prompt_kernel, tq=tq),
        grid=(n_seq, nq // 2),
        in_specs=[qblk] + [blk(kmap(j)) for j in range(4)] + [blk(kmap(j)) for j in range(4)]
        + [_const_spec(bias_e.shape), _const_spec(bias_mask.shape)],
        out_specs=qblk,
        out_shape=jax.ShapeDtypeStruct((t_total, WIDTH), BF16),
        scratch_shapes=[pltpu.VMEM((N_HEADS,) + bias_mask.shape, F32)],
        compiler_params=_params(2),
        name="band_prompt",
    )(qa, ka, ka, ka, ka, va, va, va, va, bias_e, bias_mask)


def _band_sample_kernel(q_ref, kc_ref, vc_ref, kn_ref, vn_ref, e_ref, mask_ref, o_ref, ko_ref, vo_ref,
                        bias_ref):
    p_rows = kc_ref.shape[0]
    t = q_ref.shape[0]

    @pl.when(pl.program_id(0) == 0)
    def _():
        _build_bias(e_ref, mask_ref, bias_ref)

    for p in range(N_PAIRS):
        cols = slice(p * LANES, (p + 1) * LANES)
        q = q_ref[:, cols]
        pad = jnp.zeros((LANES - t, LANES), BF16)
        ks = [kc_ref[:, cols].astype(BF16), jnp.concatenate([kn_ref[:, cols].astype(BF16), pad], axis=0)]
        vs = [vc_ref[:, cols].astype(BF16), jnp.concatenate([vn_ref[:, cols].astype(BF16), pad], axis=0)]
        out = jnp.zeros((t, LANES), F32)
        for hh in range(2):
            msk = _head_mask(hh)
            qh = jnp.where(msk, q, jnp.zeros_like(q))
            scores = [_dot_nt(qh, ks[0]) + bias_ref[2 * p + hh, :, 0:p_rows],
                      _dot_nt(qh, ks[1]) + bias_ref[2 * p + hh, :, p_rows:p_rows + LANES]]
            pv, l = _softmax_pv(scores, vs)
            out = jnp.where(msk, pv / l, out)
        o_ref[:, cols] = out.astype(BF16)
    ko_ref[0:p_rows - t, :] = kc_ref[t:p_rows, :]
    ko_ref[p_rows - t:p_rows, :] = kn_ref[...]
    vo_ref[0:p_rows - t, :] = vc_ref[t:p_rows, :]
    vo_ref[p_rows - t:p_rows, :] = vn_ref[...]


def _band_sample(qa, ka_f, va_f, cache_k, cache_v, bias_e, bias_mask):
    n, p_rows, _ = cache_k.shape
    t = qa.shape[0] // n
    row = pl.BlockSpec((t, WIDTH), lambda b: (b, 0))
    cache = pl.BlockSpec((None, p_rows, WIDTH), lambda b: (b, 0, 0))
    return pl.pallas_call(
        _band_sample_kernel,
        grid=(n,),
        in_specs=[row, cache, cache, row, row, _const_spec(bias_e.shape), _const_spec(bias_mask.shape)],
        out_specs=[row, cache, cache],
        out_shape=[jax.ShapeDtypeStruct(qa.shape, BF16),
                   jax.ShapeDtypeStruct(cache_k.shape, F32),
                   jax.ShapeDtypeStruct(cache_v.shape, F32)],
        scratch_shapes=[pltpu.VMEM((N_HEADS,) + bias_mask.shape, F32)],
        compiler_params=_params(1),
        name="band_sample",
    )(qa, cache_k, cache_v, ka_f, va_f, bias_e, bias_mask)


def _band_bias(rel_bias, n_q, n_k, offset, band_mask, valid_k):
    assert n_q + n_k - 1 <= BIAS_PERIOD
    m = np.arange(BIAS_PERIOD)
    w = np.where(m < n_k, m, m - BIAS_PERIOD)
    idx = np.clip(offset - w, -MAX_REL, MAX_REL) + MAX_REL
    e = jnp.take(rel_bias.astype(F32), jnp.asarray(idx), axis=1) * LOG2E
    qi = np.arange(n_q)[:, None]
    kj = np.arange(n_k)[None, :]
    valid = np.broadcast_to(kj < valid_k, (n_q, n_k))
    if band_mask:
        rel_chunk = (qi + offset) // CHUNK - kj // CHUNK
        valid = valid & (rel_chunk >= 0) & (rel_chunk <= LEFT_CHUNKS)
    return e, jnp.asarray(np.where(valid, 0.0, NEG), F32)


def _build_bias(e_ref, mask_ref, bias_sc):
    rows, cols = mask_ref.shape
    for h in range(N_HEADS):
        spread = jnp.broadcast_to(e_ref[h:h + 1, :], (rows, BIAS_PERIOD))
        bias_sc[h] = pltpu.roll(spread, 0, 1, stride=1, stride_axis=0)[:, :cols] + mask_ref[...]


def _aug_head_mask(h_in_pair, pair):
    lane = lax.broadcasted_iota(jnp.int32, (1, 2 * LANES), 1)
    head = 2 * pair + h_in_pair
    in_q = (lane < LANES) & ((lane // HEAD_DIM) == h_in_pair)
    in_aug = (lane >= LANES) & (((lane - LANES) // AUG_STRIDE) == head)
    return in_q | in_aug


def _flash_update(s, v, m_ref, l_ref, acc_ref, h, row_bias=None, v_seq_minor=False):
    chunks = [s[:, c * LANES:(c + 1) * LANES] for c in range(s.shape[1] // LANES)]
    if row_bias is not None:
        chunks = [c + row_bias for c in chunks]
    cmax = chunks[0]
    for c in chunks[1:]:
        cmax = jnp.maximum(cmax, c)
    m_prev = m_ref[h]
    m_new = jnp.maximum(m_prev, jnp.max(cmax, axis=-1, keepdims=True))
    alpha = jnp.exp2(m_prev - m_new)
    ps = [jnp.exp2(c - m_new) for c in chunks]
    lsum = ps[0]
    for p in ps[1:]:
        lsum = lsum + p
    l_ref[h] = alpha * l_ref[h] + lsum
    p = jnp.concatenate(ps, axis=-1).astype(BF16)
    pv = _dot_nt(p, v) if v_seq_minor else _dot(p, v)
    acc_ref[h] = alpha[:, :pv.shape[1]] * acc_ref[h] + pv
    m_ref[h] = m_new


def _flash_result(l_ref, acc_ref, pair):
    outs = [acc_ref[2 * pair + hh] / jnp.sum(l_ref[2 * pair + hh], axis=-1, keepdims=True)
            for hh in range(2)]
    return jnp.where(_head_mask(0), outs[0], outs[1])


def _fox_prompt_kernel(qi_ref, kj_ref, q_ref, k_ref, v_ref, o_ref, qh_sc, m_sc, l_sc, acc_sc, *, tq, tk):
    t = pl.program_id(1)
    i = qi_ref[t]
    j = kj_ref[t]

    @pl.when(j == 0)
    def _():
        for p in range(N_PAIRS):
            q = q_ref[p]
            for hh in range(2):
                qh_sc[2 * p + hh] = jnp.where(_aug_head_mask(hh, p), q, jnp.zeros_like(q))
        m_sc[...] = jnp.full_like(m_sc, NEG)
        l_sc[...] = jnp.zeros_like(l_sc)
        acc_sc[...] = jnp.zeros_like(acc_sc)

    def step(diagonal):
        if diagonal:
            keep = (lax.broadcasted_iota(jnp.int32, (tq, tk), 1)
                    <= lax.broadcasted_iota(jnp.int32, (tq, tk), 0))
        for p in range(N_PAIRS):
            k = k_ref[p]
            v = v_ref[:, p * LANES:(p + 1) * LANES]
            for hh in range(2):
                s = _dot_nt(qh_sc[2 * p + hh], k)
                if diagonal:
                    s = jnp.where(keep, s, NEG)
                _flash_update(s, v, m_sc, l_sc, acc_sc, 2 * p + hh)

    @pl.when(j < i)
    def _():
        step(False)

    @pl.when(j == i)
    def _():
        step(True)
        for p in range(N_PAIRS):
            o_ref[:, p * LANES:(p + 1) * LANES] = _flash_result(l_sc, acc_sc, p).astype(BF16)


def _fox_prompt(qcat, kcat, vb, n_seq):
    seq = qcat.shape[2]
    tq = tk = 1024
    nt = seq // tq
    qi = np.concatenate([np.full(i + 1, i) for i in range(nt)]).astype(np.int32)
    kj = np.concatenate([np.arange(i + 1) for i in range(nt)]).astype(np.int32)
    grid_spec = pltpu.PrefetchScalarGridSpec(
        num_scalar_prefetch=2,
        grid=(n_seq, len(qi)),
        in_specs=[
            pl.BlockSpec((None, N_PAIRS, tq, 2 * LANES), lambda b, t, qi, kj: (b, 0, qi[t], 0)),
            pl.BlockSpec((None, N_PAIRS, tk, 2 * LANES), lambda b, t, qi, kj: (b, 0, kj[t], 0)),
            pl.BlockSpec((tk, WIDTH), lambda b, t, qi, kj: (b * nt + kj[t], 0)),
        ],
        out_specs=pl.BlockSpec((tq, WIDTH), lambda b, t, qi, kj: (b * nt + qi[t], 0)),
        scratch_shapes=[pltpu.VMEM((N_HEADS, tq, 2 * LANES), BF16), pltpu.VMEM((N_HEADS, tq, LANES), F32),
                        pltpu.VMEM((N_HEADS, tq, LANES), F32), pltpu.VMEM((N_HEADS, tq, LANES), F32)],
    )
    return pl.pallas_call(
        functools.partial(_fox_prompt_kernel, tq=tq, tk=tk),
        grid_spec=grid_spec,
        out_shape=jax.ShapeDtypeStruct(vb.shape, BF16),
        compiler_params=_params(2),
        name="fox_prompt",
    )(jnp.asarray(qi), jnp.asarray(kj), qcat, kcat, vb)


def _suffix_sum_exclusive(x):
    n = x.shape[1]
    lane = lax.broadcasted_iota(jnp.int32, x.shape, 1)
    y = jnp.where(lane + 1 < n, pltpu.roll(x, n - 1, axis=1), 0.0)
    shift = 1
    while shift < n:
        y = y + jnp.where(lane + shift < n, pltpu.roll(y, n - shift, axis=1), 0.0)
        shift *= 2
    return y


def _fox_sample_kernel(q_ref, kc_ref, vc_ref, kn_ref, vn_ref, lfc_ref, lfn_ref, u_ref,
                       o_ref, cq_sc, car_sc, m_sc, l_sc, acc_sc, *, n_cache_tiles, sub_keys):
    j = pl.program_id(1)
    t = q_ref.shape[1]

    def cum_new():
        hi, mid, lo = _split3(lfn_ref[...])
        u = u_ref[...]
        return _dot(hi, u) + _dot(mid, u) + _dot(lo, u)

    def q_head(h):
        return q_ref[h // 2, :, (h % 2) * HEAD_DIM:(h % 2 + 1) * HEAD_DIM]

    @pl.when(j == 0)
    def _():
        cn = cum_new() * LOG2E
        eye = (lax.broadcasted_iota(jnp.int32, (t, LANES), 0)
               == lax.broadcasted_iota(jnp.int32, (t, LANES), 1))
        for h in range(N_HEADS):
            col = jnp.sum(jnp.where(eye, jnp.broadcast_to(cn[h:h + 1, :], (t, LANES)), 0.0),
                          axis=-1, keepdims=True)
            cq_sc[h] = jnp.broadcast_to(col, (t, LANES))
        car_sc[...] = jnp.zeros_like(car_sc)
        m_sc[...] = jnp.full_like(m_sc, NEG)
        l_sc[...] = jnp.zeros_like(l_sc)
        acc_sc[...] = jnp.zeros_like(acc_sc)

    @pl.when(j < n_cache_tiles)
    def _():
        lf = lfc_ref[...]
        r = _suffix_sum_exclusive(lf) + car_sc[:, 0:1]
        car_sc[...] = jnp.broadcast_to(r[:, 0:1] + lf[:, 0:1], car_sc.shape)
        r = r * LOG2E
        for sub in range(lf.shape[1] // sub_keys):
            keys = slice(sub * sub_keys, (sub + 1) * sub_keys)
            for h in range(N_HEADS):
                s = _dot(q_head(h), kc_ref[h, :, keys].astype(BF16)) + r[h:h + 1, keys]
                _flash_update(s, vc_ref[h, :, keys].astype(BF16), m_sc, l_sc, acc_sc, h,
                              row_bias=cq_sc[h], v_seq_minor=True)

    @pl.when(j == n_cache_tiles)
    def _():
        cn = cum_new() * LOG2E
        causal = (lax.broadcasted_iota(jnp.int32, (t, LANES), 1)
                  <= lax.broadcasted_iota(jnp.int32, (t, LANES), 0))
        pad = jnp.zeros((LANES - t, HEAD_DIM), BF16)
        for h in range(N_HEADS):
            cols = slice(h * HEAD_DIM, (h + 1) * HEAD_DIM)
            k = jnp.concatenate([kn_ref[h // 2, :, (h % 2) * HEAD_DIM:(h % 2 + 1) * HEAD_DIM], pad], axis=0)
            v = jnp.concatenate([vn_ref[:, cols], pad], axis=0)
            s = _dot_nt(q_head(h), k) + cq_sc[h] - cn[h:h + 1, :]
            _flash_update(jnp.where(causal, s, NEG), v, m_sc, l_sc, acc_sc, h)
            o_ref[:, cols] = (acc_sc[h] / jnp.sum(l_sc[h], axis=-1, keepdims=True)).astype(BF16)


def _fox_sample(qcat, kcat, vb, cache_kt, cache_vt, lf_cache_t, lf_new_t):
    n, _, _, p_rows = cache_kt.shape
    t = vb.shape[0] // n
    tk = 2048
    nct = p_rows // tk
    u = jnp.asarray(np.pad(np.triu(np.ones((t, t))), ((0, 0), (0, LANES - t))), BF16)
    rev = lambda j: jnp.maximum(nct - 1 - j, 0)
    cache = pl.BlockSpec((None, N_HEADS, HEAD_DIM, tk), lambda b, j: (b, 0, 0, rev(j)))
    new_cat = pl.BlockSpec((None, N_PAIRS, t, 2 * LANES), lambda b, j: (0, 0, b, 0))
    return pl.pallas_call(
        functools.partial(_fox_sample_kernel, n_cache_tiles=nct, sub_keys=tk),
        grid=(n, nct + 1),
        in_specs=[
            new_cat, cache, cache, new_cat,
            pl.BlockSpec((t, WIDTH), lambda b, j: (b, 0)),
            pl.BlockSpec((None, N_HEADS, tk), lambda b, j: (b, 0, rev(j))),
            pl.BlockSpec((None, N_HEADS, t), lambda b, j: (b, 0, 0)),
            _const_spec(u.shape),
        ],
        out_specs=pl.BlockSpec((t, WIDTH), lambda b, j: (b, 0)),
        out_shape=jax.ShapeDtypeStruct(vb.shape, BF16),
        scratch_shapes=[pltpu.VMEM((N_HEADS, t, LANES), F32), pltpu.VMEM((N_HEADS, LANES), F32),
                        pltpu.VMEM((N_HEADS, t, LANES), F32), pltpu.VMEM((N_HEADS, t, LANES), F32),
                        pltpu.VMEM((N_HEADS, t, HEAD_DIM), F32)],
        compiler_params=_params(2),
        name="fox_sample",
    )(qcat, cache_kt, cache_vt, kcat, vb, lf_cache_t, lf_new_t, u)


GROUP_LANE = 64
ROW_ALIGN = 16
MOE_TILE = 512
SORT_ROWS = 640
XS_WIDTH = D_MODEL + 2 * LANES


def _route(r):
    lane_i = lax.broadcasted_iota(jnp.int32, r.shape, 1)
    lane = lane_i.astype(F32)
    lane_grp = (lane_i // EXPERTS_PER_GROUP).astype(F32)
    big = float(LANES)
    is_coarse = (lane_i >= N_EXPERTS) & (lane_i < N_EXPERTS + N_GROUPS)
    cm = jnp.where(is_coarse, r, NEG)
    cmax = cm.max(axis=-1, keepdims=True)
    grp = jnp.min(jnp.where(cm == cmax, lane - N_EXPERTS, big), axis=-1, keepdims=True)
    pg_sel = 1.0 / jnp.sum(jnp.exp(cm - cmax), axis=-1, keepdims=True)
    in_grp = (lane_i < N_EXPERTS) & (lane_grp == grp)
    fm = jnp.where(in_grp, r, NEG)
    m1 = fm.max(axis=-1, keepdims=True)
    denom = jnp.sum(jnp.exp(fm - m1), axis=-1, keepdims=True)
    i1 = jnp.min(jnp.where(fm == m1, lane, big), axis=-1, keepdims=True)
    fm2 = jnp.where(lane == i1, NEG, fm)
    m2 = fm2.max(axis=-1, keepdims=True)
    i2 = jnp.min(jnp.where(fm2 == m2, lane, big), axis=-1, keepdims=True)
    p1 = 1.0 / denom
    p2 = jnp.exp(m2 - m1) / denom
    tot = p1 + p2
    comb = (jnp.where(lane == i1, pg_sel * (p1 / tot), 0.0)
            + jnp.where(lane == i2, pg_sel * (p2 / tot), 0.0))
    return comb, grp


def _merge_kernel(x_ref, oa_ref, ob_ref, g_ref, wg_ref, wpa_ref, wpb_ref, wo_ref, gf_ref, wr_ref, br_ref,
                  y_ref, hx_ref, route_ref, grp_t_ref, cnt_ref):
    x = x_ref[...]
    h = _rms(x, g_ref[...]).astype(BF16)
    gate = jax.nn.sigmoid(_dot(h, wg_ref[...]))
    mix = (gate[:, :D_MODEL] * _dot(oa_ref[...], wpa_ref[...])
           + gate[:, D_MODEL:] * _dot(ob_ref[...], wpb_ref[...]))
    y = x + _dot(mix.astype(BF16), wo_ref[...])
    y_ref[...] = y

    hx = _rms(y, gf_ref[...])
    hx_ref[:, 0:D_MODEL] = hx.astype(BF16)
    h_hi, h_mid, _ = _split3(hx)
    r2 = _dot(jnp.concatenate([h_hi, h_mid], axis=1), wr_ref[...])
    r = r2[:, :LANES] + r2[:, LANES:] + br_ref[...]
    comb, grp = _route(r)
    lane = lax.broadcasted_iota(jnp.int32, comb.shape, 1)
    route = jnp.where(lane == GROUP_LANE, grp, comb)
    route_ref[...] = route
    r_hi = route.astype(BF16)
    hx_ref[:, D_MODEL:D_MODEL + LANES] = r_hi
    hx_ref[:, D_MODEL + LANES:] = (route - r_hi.astype(F32)).astype(BF16)
    grp_t_ref[...] = route.T[GROUP_LANE:GROUP_LANE + 8, :]
    cnt = jnp.sum(jnp.where(lane.astype(F32) == grp, 1.0, 0.0), axis=0, keepdims=True)
    cnt_ref[...] = jnp.broadcast_to(cnt, cnt_ref.shape)


def _merge(x, oa, ob, w):
    t_total = x.shape[0]
    tm = MOE_TILE
    n_tiles = t_total // tm
    row = lambda n: pl.BlockSpec((tm, n), lambda i: (i, 0))
    consts = [w["g_mix"], w["w_gate"], w["w_pa"], w["w_pb"], w["w_o"], w["g_ffn"], w["w_router"],
              w["b_router"]]
    return pl.pallas_call(
        _merge_kernel,
        grid=(n_tiles,),
        in_specs=[row(D_MODEL), row(WIDTH), row(WIDTH)] + [_const_spec(c.shape) for c in consts],
        out_specs=[row(D_MODEL), row(XS_WIDTH), row(LANES), pl.BlockSpec((8, tm), lambda i: (0, i)),
                   pl.BlockSpec((None, 8, LANES), lambda i: (i, 0, 0))],
        out_shape=[jax.ShapeDtypeStruct(x.shape, F32), jax.ShapeDtypeStruct((t_total, XS_WIDTH), BF16),
                   jax.ShapeDtypeStruct((t_total, LANES), F32), jax.ShapeDtypeStruct((8, t_total), F32),
                   jax.ShapeDtypeStruct((n_tiles, 8, LANES), F32)],
        compiler_params=_params(1),
        name="merge",
    )(x, oa, ob, *consts)


def _plan_kernel(cnt_ref, ib_ref, dst_ref, len_ref, fill_ref, inblk_ref, outblk_ref, grp_ref, kind_ref, *,
                 n_tiles, n_steps):
    align_bits = ROW_ALIGN.bit_length() - 1
    tile_bits = MOE_TILE.bit_length() - 1
    zero = jnp.int32(0)

    def per_tile(t, tot):
        run = zero
        new_tot = []
        for g in range(N_GROUPS):
            n = ((cnt_ref[t * N_GROUPS + g] + (ROW_ALIGN - 1)) >> align_bits) << align_bits
            ib_ref[t * N_GROUPS + g] = run
            len_ref[t * N_GROUPS + g] = n
            dst_ref[t * N_GROUPS + g] = tot[g]
            run = run + n
            new_tot.append(tot[g] + n)
        return tuple(new_tot)

    tot = lax.fori_loop(0, n_tiles, per_tile, (zero,) * N_GROUPS)
    n_ffn = [(tot[g] + (MOE_TILE - 1)) >> tile_bits for g in range(N_GROUPS)]
    g_base, ends, acc_rows, acc_tiles = [], [], zero, zero
    for g in range(N_GROUPS):
        g_base.append(acc_rows)
        acc_rows = acc_rows + ((n_ffn[g] + 1) << tile_bits)
        acc_tiles = acc_tiles + n_ffn[g] + 1
        ends.append(acc_tiles)

    def add_base(t, carry):
        for g in range(N_GROUPS):
            dst_ref[t * N_GROUPS + g] = dst_ref[t * N_GROUPS + g] + g_base[g]
        return carry

    lax.fori_loop(0, n_tiles, add_base, zero)
    for g in range(N_GROUPS):
        fill_ref[g] = g_base[g] + tot[g]
        fill_ref[N_GROUPS + g] = g_base[g] + (n_ffn[g] << tile_bits)
    fill_ref[2 * N_GROUPS] = ends[-1]
    first_blk = g_base[N_GROUPS - 1]
    for g in reversed(range(N_GROUPS - 1)):
        first_blk = jnp.where(n_ffn[g] > 0, g_base[g], first_blk)
    first_blk = first_blk >> tile_bits

    def pick(vals, g):
        out = vals[N_GROUPS - 1]
        for i in reversed(range(N_GROUPS - 1)):
            out = jnp.where(g == i, vals[i], out)
        return out

    def per_step(k, carry):
        kk = jnp.minimum(k, ends[-1] - 1)
        g = zero
        for i in range(N_GROUPS - 1):
            g = g + jnp.where(kk >= ends[i], 1, 0)
        j = kk - pick([zero] + ends[:-1], g)
        is_expert = j < pick(n_ffn, g)
        out_blk = jnp.where(k < ends[-1], (pick(g_base, g) >> tile_bits) + j, k)
        outblk_ref[k] = out_blk
        inblk_ref[k] = jnp.where(is_expert, out_blk, first_blk)
        grp_ref[k] = g
        kind_ref[k] = jnp.where(is_expert, 1, 0)
        return carry

    lax.fori_loop(0, n_steps, per_step, zero)


def _moe_plan(cnt, t_total):
    n_tiles = t_total // MOE_TILE
    n_steps = (n_tiles + 2 * N_GROUPS + (N_GROUPS * (ROW_ALIGN - 1) * n_tiles + MOE_TILE - 1) // MOE_TILE)
    smem = pl.BlockSpec(memory_space=pltpu.SMEM)
    sizes = [n_tiles * N_GROUPS] * 3 + [2 * N_GROUPS + 1] + [n_steps] * 4
    outs = pl.pallas_call(
        functools.partial(_plan_kernel, n_tiles=n_tiles, n_steps=n_steps),
        grid_spec=pltpu.PrefetchScalarGridSpec(num_scalar_prefetch=1, grid=(1,), in_specs=[],
                                               out_specs=[smem] * len(sizes)),
        out_shape=[jax.ShapeDtypeStruct((n,), jnp.int32) for n in sizes],
        compiler_params=_params(1),
        name="moe_plan",
    )(cnt.reshape(-1))
    return (*outs, n_steps)


def _sort_kernel(ib_ref, dst_ref, len_ref, fill_ref, hx_ref, grp_t_ref, tri_ref, xs_ref, xsort, sems):
    i = pl.program_id(0)
    n = pl.num_programs(0)
    tm = hx_ref.shape[0]
    half = tm // 2
    slot = i % 2

    def copy(src, dst, rows, sl, g, part):
        return pltpu.make_async_copy(xsort.at[sl, pl.ds(pl.multiple_of(src, ROW_ALIGN), rows)],
                                     xs_ref.at[pl.ds(pl.multiple_of(dst, ROW_ALIGN), rows)], sems.at[sl, g, part])

    def for_segments(step, sl, act):
        for g in range(N_GROUPS):
            src, dst = ib_ref[step * N_GROUPS + g], dst_ref[step * N_GROUPS + g]
            act(copy(src, dst, half, sl, g, 0))

            @pl.when(len_ref[step * N_GROUPS + g] > half)
            def _():
                act(copy(src + half, dst + half, half, sl, g, 1))

    def run(cs):
        for c in cs:
            c.start()
        for c in cs:
            c.wait()

    @pl.when(i == 0)
    def _():
        for sl in range(2):
            xsort[sl, SORT_ROWS:, :] = jnp.zeros((xsort.shape[1] - SORT_ROWS, XS_WIDTH), BF16)

    g_row = grp_t_ref[0:1, :]
    sub = lax.broadcasted_iota(jnp.int32, (8, tm), 0).astype(F32)
    mine = sub == g_row
    before = _dot(jnp.where(mine, 1.0, 0.0).astype(BF16), tri_ref[...])
    dest = jnp.sum(jnp.where(mine, before, 0.0), axis=0, keepdims=True)
    for g in range(N_GROUPS):
        base = jnp.full(dest.shape, ib_ref[i * N_GROUPS + g], jnp.int32).astype(F32)
        dest = dest + jnp.where(g_row == g, base, 0.0)
    rows = lax.broadcasted_iota(jnp.int32, (SORT_ROWS, tm), 0).astype(F32)
    perm = jnp.where(rows == dest, 1.0, 0.0).astype(BF16)
    xsort[slot, 0:SORT_ROWS, :] = _dot(perm, hx_ref[...]).astype(BF16)

    @pl.when(i > 0)
    def _():
        for_segments(i - 1, 1 - slot, lambda c: c.wait())

    for_segments(i, slot, lambda c: c.start())

    @pl.when(i == n - 1)
    def _():
        for_segments(i, slot, lambda c: c.wait())
        run([copy(SORT_ROWS, fill_ref[g], tm, slot, g, 0) for g in range(N_GROUPS)])
        run([copy(SORT_ROWS, fill_ref[N_GROUPS + g], tm, slot, g, 0) for g in range(N_GROUPS)])

        @pl.loop(fill_ref[2 * N_GROUPS], xs_ref.shape[0] // tm)
        def _(b):
            run([copy(SORT_ROWS, b * tm, tm, slot, 0, 0)])


def _expert_kernel(ib_ref, ob_ref, grp_ref, kind_ref, xs_ref, w1_ref, w3_ref, w2_ref, ys_ref):
    del ib_ref, ob_ref
    k = pl.program_id(0)

    @pl.when(kind_ref[k] == 1)
    def _():
        x = xs_ref[:, 0:D_MODEL]
        side = lambda w_ref: jnp.concatenate([w_ref[e].astype(BF16) for e in range(EXPERTS_PER_GROUP)], axis=1)
        hid = jax.nn.silu(_dot(x, side(w1_ref))) * _dot(x, side(w3_ref))
        comb = (xs_ref[:, D_MODEL:D_MODEL + LANES].astype(F32) + xs_ref[:, D_MODEL + LANES:].astype(F32))
        lane = lax.broadcasted_iota(jnp.int32, comb.shape, 1)
        first = grp_ref[k] * EXPERTS_PER_GROUP
        scale = []
        for e in range(EXPERTS_PER_GROUP):
            w_e = jnp.sum(jnp.where(lane == first + e, comb, 0.0), axis=-1, keepdims=True)
            scale.append(jnp.broadcast_to(w_e, (comb.shape[0], D_EXPERT)))
        hid = hid * jnp.concatenate(scale, axis=-1)
        w2 = w2_ref[...].astype(BF16).reshape(EXPERTS_PER_GROUP * D_EXPERT, D_MODEL)
        ys_ref[...] = _dot(hid.astype(BF16), w2).astype(BF16)

    @pl.when(kind_ref[k] == 0)
    def _():
        ys_ref[...] = jnp.zeros_like(ys_ref)


def _unsort_kernel(ib_ref, dst_ref, len_ref, y_ref, route_ref, tri_ref, ys_ref, o_ref, ybuf, yasm, sems):
    i = pl.program_id(0)
    n = pl.num_programs(0)
    tm = y_ref.shape[0]
    half = tm // 2
    slot = i % 2

    def fetch(step, sl, act):
        for g in range(N_GROUPS):
            src = pl.multiple_of(dst_ref[step * N_GROUPS + g], ROW_ALIGN)
            act(pltpu.make_async_copy(ys_ref.at[pl.ds(src, half)], ybuf.at[sl, g, 0:half], sems.at[sl, g, 0]))

            @pl.when(len_ref[step * N_GROUPS + g] > half)
            def _():
                act(pltpu.make_async_copy(ys_ref.at[pl.ds(src + half, half)], ybuf.at[sl, g, half:tm],
                                          sems.at[sl, g, 1]))

    @pl.when(i == 0)
    def _():
        yasm[...] = jnp.zeros_like(yasm)
        fetch(0, 0, lambda c: c.start())

    @pl.when(i + 1 < n)
    def _():
        fetch(i + 1, 1 - slot, lambda c: c.start())

    route = route_ref[...]
    lane = lax.broadcasted_iota(jnp.int32, route.shape, 1).astype(F32)
    grp = jnp.sum(jnp.where(lane == GROUP_LANE, route, 0.0), axis=-1, keepdims=True)
    mine = lane == grp
    before = _dot(tri_ref[...], jnp.where(mine, 1.0, 0.0).astype(BF16))
    dest = jnp.sum(jnp.where(mine, before, 0.0), axis=-1, keepdims=True)
    for g in range(N_GROUPS):
        base = jnp.full(dest.shape, ib_ref[i * N_GROUPS + g], jnp.int32).astype(F32)
        dest = dest + jnp.where(grp == g, base, 0.0)
    cols = lax.broadcasted_iota(jnp.int32, (tm, SORT_ROWS), 1).astype(F32)
    perm_t = jnp.where(cols == dest, 1.0, 0.0).astype(BF16)

    fetch(i, slot, lambda c: c.wait())
    for g in range(N_GROUPS):
        row0 = pl.multiple_of(ib_ref[i * N_GROUPS + g], ROW_ALIGN)
        yasm[pl.ds(row0, half), :] = ybuf[slot, g, 0:half]

        @pl.when(len_ref[i * N_GROUPS + g] > half)
        def _():
            yasm[pl.ds(row0 + half, half), :] = ybuf[slot, g, half:tm]
    o_ref[...] = y_ref[...] + _dot(perm_t, yasm[0:SORT_ROWS, :])


def _moe(y, hx, route, grp_t, cnt, w):
    t_total = y.shape[0]
    tm = MOE_TILE
    n_tiles = t_total // tm
    assert SORT_ROWS >= tm + N_GROUPS * (ROW_ALIGN - 1)
    in_base, dst, seg_len, fill, in_blk, out_blk, grp_of_step, is_expert, n_steps = _moe_plan(
        cnt[:, 0, :N_GROUPS].astype(jnp.int32), t_total)
    cap_rows = n_steps * tm
    lower = jnp.asarray(np.tril(np.ones((tm, tm)), -1), BF16)
    upper = jnp.asarray(np.triu(np.ones((tm, tm)), 1), BF16)
    stage_rows = SORT_ROWS + tm
    any_spec = pl.BlockSpec(memory_space=pl.ANY)

    xs = pl.pallas_call(
        _sort_kernel,
        grid_spec=pltpu.PrefetchScalarGridSpec(
            num_scalar_prefetch=4,
            grid=(n_tiles,),
            in_specs=[pl.BlockSpec((tm, XS_WIDTH), lambda i, *_: (i, 0)),
                      pl.BlockSpec((8, tm), lambda i, *_: (0, i)),
                      pl.BlockSpec((tm, tm), lambda i, *_: (0, 0))],
            out_specs=any_spec,
            scratch_shapes=[pltpu.VMEM((2, stage_rows, XS_WIDTH), BF16),
                            pltpu.SemaphoreType.DMA((2, N_GROUPS, 2))],
        ),
        out_shape=jax.ShapeDtypeStruct((cap_rows, XS_WIDTH), BF16),
        compiler_params=_params(1),
        name="moe_sort",
    )(in_base, dst, seg_len, fill, hx, grp_t, upper)

    step_map = lambda which: (lambda k, ib, ob, grp, kind: ((ib, ob, grp)[which][k], 0))
    wmap = lambda k, ib, ob, grp, kind: (grp[k], 0, 0)
    wspec = pl.BlockSpec((EXPERTS_PER_GROUP, D_MODEL, D_EXPERT), wmap)
    ys = pl.pallas_call(
        _expert_kernel,
        grid_spec=pltpu.PrefetchScalarGridSpec(
            num_scalar_prefetch=4,
            grid=(n_steps,),
            in_specs=[pl.BlockSpec((tm, XS_WIDTH), step_map(0)), wspec, wspec, pl.BlockSpec((EXPERTS_PER_GROUP, D_EXPERT, D_MODEL), wmap)],
            out_specs=pl.BlockSpec((tm, D_MODEL), step_map(1)),
        ),
        out_shape=jax.ShapeDtypeStruct((cap_rows, D_MODEL), BF16),
        compiler_params=_params(1),
        name="moe_experts",
    )(in_blk, out_blk, grp_of_step, is_expert, xs, w["w1"], w["w3"], w["w2"])

    return pl.pallas_call(
        _unsort_kernel,
        grid_spec=pltpu.PrefetchScalarGridSpec(
            num_scalar_prefetch=3,
            grid=(n_tiles,),
            in_specs=[pl.BlockSpec((tm, D_MODEL), lambda i, *_: (i, 0)),
                      pl.BlockSpec((tm, LANES), lambda i, *_: (i, 0)),
                      pl.BlockSpec((tm, tm), lambda i, *_: (0, 0)),
                      any_spec],
            out_specs=pl.BlockSpec((tm, D_MODEL), lambda i, *_: (i, 0)),
            scratch_shapes=[pltpu.VMEM((2, N_GROUPS, tm, D_MODEL), BF16),
                            pltpu.VMEM((stage_rows, D_MODEL), BF16),
                            pltpu.SemaphoreType.DMA((2, N_GROUPS, 2))],
        ),
        out_shape=jax.ShapeDtypeStruct(y.shape, F32),
        compiler_params=_params(1),
        name="moe_unsort",
    )(in_base, dst, seg_len, y, route, lower, ys)


def _prep_weights(g_mix, w_in, b_f, q_norm_a, k_norm_a, q_norm_b, k_norm_b, w_pa, w_pb, w_o,
                  g_ffn, w_rg, b_rg, w_re, b_re, w1, w3, w2):
    n_qkv = 6 * WIDTH
    tile = lambda g: jnp.tile(g, N_HEADS)
    w_router = jnp.concatenate(
        [jnp.transpose(w_re, (1, 0, 2)).reshape(D_MODEL, N_EXPERTS), w_rg,
         jnp.zeros((D_MODEL, LANES - N_EXPERTS - N_GROUPS), F32)], axis=1)
    b_router = jnp.concatenate(
        [b_re.reshape(N_EXPERTS), b_rg, jnp.zeros((LANES - N_EXPERTS - N_GROUPS,), F32)])[None, :]
    wr_hi = w_router.astype(BF16)
    wr_mid = (w_router - wr_hi.astype(F32)).astype(BF16)
    return {
        "g_mix": g_mix[None, :],
        "w_qkv": w_in[:, :n_qkv].astype(BF16),
        "w_f": jnp.pad(w_in[:, n_qkv:n_qkv + N_HEADS], ((0, 0), (0, LANES - N_HEADS))).astype(BF16),
        "b_f": jnp.pad(b_f, (0, LANES - N_HEADS))[None, :],
        "w_gate": w_in[:, n_qkv + N_HEADS:].astype(BF16),
        "gains": jnp.stack([tile(q_norm_a), tile(k_norm_a), tile(q_norm_b), tile(k_norm_b)]),
        "w_pa": w_pa.astype(BF16), "w_pb": w_pb.astype(BF16), "w_o": w_o.astype(BF16),
        "g_ffn": g_ffn[None, :],
        "w_router": jnp.concatenate([jnp.concatenate([wr_hi, wr_mid], axis=1),
                                     jnp.concatenate([wr_hi, jnp.zeros_like(wr_hi)], axis=1)], axis=0),
        "b_router": b_router,
        "w1": w1, "w3": w3, "w2": w2,
    }


def kernel(x_prompt, x_sample, cache_a_k, cache_a_v, cache_b_k, cache_b_v, cache_b_logf, g_mix, w_in, b_f, q_norm_a, k_norm_a, q_norm_b, k_norm_b, rel_bias, w_pa, w_pb, w_o, g_ffn, w_rg, b_rg, w_re, b_re, w1, w3, w2):
    assert g_mix.shape[0] == 1, "single-layer step"
    n_p, seq, _ = x_prompt.shape
    n_s, t_s, _ = x_sample.shape
    a_rows = cache_a_k.shape[2]
    w = _prep_weights(g_mix[0], w_in[0], b_f[0], q_norm_a[0], k_norm_a[0], q_norm_b[0], k_norm_b[0],
                      w_pa[0], w_pb[0], w_o[0], g_ffn[0], w_rg[0], b_rg[0], w_re[0], b_re[0],
                      w1[0], w3[0], w2[0])
    band_tq = 256
    bias_prompt = _band_bias(rel_bias[0], band_tq, 3 * band_tq, 2 * band_tq, True, 3 * band_tq)
    bias_sample = _band_bias(rel_bias[0], t_s, a_rows + LANES, a_rows, False, a_rows + t_s)

    seq_minor = lambda a: jnp.transpose(a, (0, 2, 3, 1))
    seq_major = lambda a: jnp.transpose(a, (0, 3, 1, 2))[None]

    xp = x_prompt.reshape(n_p * seq, D_MODEL)
    (qa, ka, va, ka_t, va_t, qcat, kcat, vb, kb_t, vb_t, logf_t) = _inproj(xp, n_p, w, seq_minor=True)
    o_a = _band_prompt(qa, ka, va, *bias_prompt, n_p)
    o_b = _fox_prompt(qcat, kcat, vb, n_p)
    y_p = _moe(*_merge(xp, o_a, o_b, w), w)

    xs = x_sample.reshape(n_s * t_s, D_MODEL)
    (qa_s, _, _, ka_fs, va_fs, qcat_s, kcat_s, vb_s, kb_fs, vb_fs, logf_s) = _inproj(
        xs, 1, w, seq_minor=False)
    o_as, new_ak, new_av = _band_sample(
        qa_s, ka_fs, va_fs, cache_a_k[0].reshape(n_s, a_rows, WIDTH),
        cache_a_v[0].reshape(n_s, a_rows, WIDTH), *bias_sample)
    lf_cache_t = jnp.transpose(cache_b_logf[0], (0, 2, 1))
    lf_new_t = jnp.transpose(logf_s.reshape(n_s, t_s, N_HEADS), (0, 2, 1))
    o_bs = _fox_sample(qcat_s, kcat_s, vb_s, seq_minor(cache_b_k[0]), seq_minor(cache_b_v[0]),
                       lf_cache_t, lf_new_t)
    y_s = _moe(*_merge(xs, o_as, o_bs, w), w)

    heads = lambda a, n, r: a.reshape(1, n, r, N_HEADS, HEAD_DIM)
    return (y_p.reshape(n_p, seq, D_MODEL), y_s.reshape(n_s, t_s, D_MODEL),
            seq_major(ka_t), seq_major(va_t), seq_major(kb_t), seq_major(vb_t),
            jnp.transpose(logf_t, (0, 2, 1))[None],
            heads(new_ak, n_s, a_rows), heads(new_av, n_s, a_rows),
            heads(kb_fs, n_s, t_s), heads(vb_fs, n_s, t_s), logf_s.reshape(1, n_s, t_s, N_HEADS))
```

```python
import functools

import numpy as np
import jax
import jax.numpy as jnp
from jax import lax
from jax.experimental import pallas as pl
from jax.experimental.pallas import tpu as pltpu

F32 = jnp.float32
BF16 = jnp.bfloat16

D_MODEL = 1024
HEAD_DIM = 64
N_HEADS = 8
WIDTH = N_HEADS * HEAD_DIM
N_PAIRS = N_HEADS // 2
CHUNK = 64
LEFT_CHUNKS = 8
WINDOW_ROWS = LEFT_CHUNKS * CHUNK
MAX_REL = 256
N_GROUPS = 4
EXPERTS_PER_GROUP = 8
N_EXPERTS = N_GROUPS * EXPERTS_PER_GROUP
D_EXPERT = 128
EPS = 1e-6
NEG = -1e30
LOG2E = 1.4426950408889634
LANES = 128
BIAS_PERIOD = 1024
AUG_STRIDE = 8
VMEM_LIMIT = 56 * 1024 * 1024

_NT = (((1,), (1,)), ((), ()))


def _dot(a, b):
    return jnp.dot(a, b, preferred_element_type=F32)


def _dot_nt(a, b):
    return lax.dot_general(a, b, _NT, preferred_element_type=F32)


def _split3(x):
    hi = x.astype(BF16)
    r = x - hi.astype(F32)
    mid = r.astype(BF16)
    lo = (r - mid.astype(F32)).astype(BF16)
    return hi, mid, lo


def _dot3(a_bf, x):
    hi, mid, lo = _split3(x)
    return _dot(a_bf, hi) + _dot(a_bf, mid) + _dot(a_bf, lo)


def _rms(x, g):
    ms = jnp.mean(x * x, axis=-1, keepdims=True)
    return x * lax.rsqrt(ms + EPS) * g


def _params(n_axes):
    return pltpu.CompilerParams(dimension_semantics=("arbitrary",) * n_axes,
                                vmem_limit_bytes=VMEM_LIMIT)


def _const_spec(shape):
    nd = len(shape)
    return pl.BlockSpec(shape, lambda *_: (0,) * nd)


def _inproj_kernel(x_ref, g_ref, wqkv_ref, wf_ref, bf_ref, gains_ref, bd_ref, ltri_ref,
                   selq_ref, selk_ref, oneq_ref, onek_ref,
                   qa_ref, ka_ref, va_ref, kaf_ref, vaf_ref, qcat_ref, kcat_ref, vb_ref,
                   kbf_ref, vbf_ref, logf_ref, carry_ref, *, tiles_per_seq, seq_minor):
    i = pl.program_id(0)
    tm = x_ref.shape[0]

    @pl.when(i % tiles_per_seq == 0)
    def _():
        carry_ref[...] = jnp.zeros_like(carry_ref)

    def put_f32(ref, y, last_tile_only=False):
        if not seq_minor:
            ref[...] = y
        elif last_tile_only:
            @pl.when(i % tiles_per_seq == tiles_per_seq - 1)
            def _():
                ref[...] = y.T.reshape(N_HEADS, HEAD_DIM, tm)
        else:
            ref[...] = y.T.reshape(N_HEADS, HEAD_DIM, tm)

    h = _rms(x_ref[...], g_ref[...]).astype(BF16)

    def seg(s):
        return _dot(h, wqkv_ref[:, s * WIDTH:(s + 1) * WIDTH])

    def headnorm(y, n):
        ss = _dot((y * y).astype(BF16), bd_ref[...])
        return y * lax.rsqrt(ss * (1.0 / HEAD_DIM) + EPS) * gains_ref[n:n + 1, :]

    q_a = headnorm(seg(0), 0)
    qa_ref[...] = (q_a * (HEAD_DIM ** -0.5 * LOG2E)).astype(BF16)
    k_a = headnorm(seg(1), 1)
    ka_ref[...] = k_a.astype(BF16)
    put_f32(kaf_ref, k_a, last_tile_only=True)
    v_a = seg(2)
    va_ref[...] = v_a.astype(BF16)
    put_f32(vaf_ref, v_a, last_tile_only=True)

    z = _dot(h, wf_ref[...]) + bf_ref[...]
    logf = jnp.minimum(z, 0.0) - jnp.log(1.0 + jnp.exp(-jnp.abs(z)))
    if seq_minor:
        logf_ref[...] = logf.T[0:N_HEADS, :]
    else:
        logf_ref[...] = logf[:, :N_HEADS]
    c = _dot3(ltri_ref[...], logf) + carry_ref[0:1, :]
    carry_ref[...] = jnp.broadcast_to(c[-1:, :], carry_ref.shape)
    cs = jnp.concatenate(_split3(c * LOG2E), axis=-1)
    q_aug = (_dot(cs, selq_ref[...]) + oneq_ref[...]).astype(BF16)
    k_aug = (_dot(cs, selk_ref[...]) + onek_ref[...]).astype(BF16)

    q_b = (headnorm(seg(3), 2) * (HEAD_DIM ** -0.5 * LOG2E)).astype(BF16)
    k_b = headnorm(seg(4), 3)
    put_f32(kbf_ref, k_b)
    k_b = k_b.astype(BF16)
    v_b = seg(5)
    put_f32(vbf_ref, v_b)
    vb_ref[...] = v_b.astype(BF16)
    for p in range(N_PAIRS):
        cols = slice(p * LANES, (p + 1) * LANES)
        qcat_ref[p, :, 0:LANES] = q_b[:, cols]
        qcat_ref[p, :, LANES:2 * LANES] = q_aug
        kcat_ref[p, :, 0:LANES] = k_b[:, cols]
        kcat_ref[p, :, LANES:2 * LANES] = k_aug


def _aug_constants():
    selq = np.zeros((3 * LANES, LANES), np.float32)
    selk = np.zeros((3 * LANES, LANES), np.float32)
    oneq = np.zeros((1, LANES), np.float32)
    onek = np.zeros((1, LANES), np.float32)
    for h in range(N_HEADS):
        for k in range(3):
            selq[k * LANES + h, AUG_STRIDE * h + k] = 1.0
            selk[k * LANES + h, AUG_STRIDE * h + 3 + k] = -1.0
            oneq[0, AUG_STRIDE * h + 3 + k] = 1.0
            onek[0, AUG_STRIDE * h + k] = 1.0
    return (jnp.asarray(selq, BF16), jnp.asarray(selk, BF16), jnp.asarray(oneq), jnp.asarray(onek))


def _inproj(x, n_seq, w, seq_minor):
    t_total = x.shape[0]
    tm = 512
    n_tiles = t_total // tm
    seq = t_total // n_seq
    tps = seq // tm
    row = lambda i: (i, 0)
    flat = lambda dt: (jax.ShapeDtypeStruct((t_total, WIDTH), dt), pl.BlockSpec((tm, WIDTH), row))
    if seq_minor:
        assert tm == WINDOW_ROWS
        band_f32 = (jax.ShapeDtypeStruct((n_seq, N_HEADS, HEAD_DIM, WINDOW_ROWS), F32),
                    pl.BlockSpec((None, N_HEADS, HEAD_DIM, tm), lambda i: (i // tps, 0, 0, 0)))
        fox_f32 = (jax.ShapeDtypeStruct((n_seq, N_HEADS, HEAD_DIM, seq), F32),
                   pl.BlockSpec((None, N_HEADS, HEAD_DIM, tm), lambda i: (i // tps, 0, 0, i % tps)))
        logf = (jax.ShapeDtypeStruct((n_seq, N_HEADS, seq), F32),
                pl.BlockSpec((None, N_HEADS, tm), lambda i: (i // tps, 0, i % tps)))
    else:
        band_f32 = fox_f32 = flat(F32)
        logf = (jax.ShapeDtypeStruct((t_total, N_HEADS), F32), pl.BlockSpec((tm, N_HEADS), row))
    cat = (jax.ShapeDtypeStruct((n_seq, N_PAIRS, seq, 2 * LANES), BF16),
           pl.BlockSpec((None, N_PAIRS, tm, 2 * LANES), lambda i: (i // tps, 0, i % tps, 0)))
    outs = [flat(BF16), flat(BF16), flat(BF16), band_f32, band_f32, cat, cat, flat(BF16),
            fox_f32, fox_f32, logf]
    bd = jnp.asarray(np.kron(np.eye(N_HEADS), np.ones((HEAD_DIM, HEAD_DIM))), BF16)
    ltri = jnp.asarray(np.tril(np.ones((tm, tm))), BF16)
    consts = [w["g_mix"], w["w_qkv"], w["w_f"], w["b_f"], w["gains"], bd, ltri, *_aug_constants()]
    in_specs = [pl.BlockSpec((tm, D_MODEL), row)] + [_const_spec(c.shape) for c in consts]
    return pl.pallas_call(
        functools.partial(_inproj_kernel, tiles_per_seq=tps, seq_minor=seq_minor),
        grid=(n_tiles,),
        in_specs=in_specs,
        out_specs=[o[1] for o in outs],
        out_shape=[o[0] for o in outs],
        scratch_shapes=[pltpu.VMEM((8, LANES), F32)],
        compiler_params=_params(1),
        name="inproj",
    )(x, *consts)


def _softmax_pv(scores, values):
    chunks = [[s[:, c * LANES:(c + 1) * LANES] for c in range(s.shape[1] // LANES)] for s in scores]
    cmax = None
    for cs in chunks:
        for c in cs:
            cmax = c if cmax is None else jnp.maximum(cmax, c)
    m = jnp.max(cmax, axis=-1, keepdims=True)
    pv = None
    lsum = None
    for cs, v in zip(chunks, values):
        ps = [jnp.exp2(c - m) for c in cs]
        for p in ps:
            lsum = p if lsum is None else lsum + p
        o = _dot(jnp.concatenate(ps, axis=-1).astype(BF16), v)
        pv = o if pv is None else pv + o
    return pv, jnp.sum(lsum, axis=-1, keepdims=True)


def _head_mask(hh):
    lane = lax.broadcasted_iota(jnp.int32, (1, LANES), 1)
    return (lane // HEAD_DIM) == hh


def _band_prompt_kernel(q_ref, k0_ref, k1_ref, k2_ref, k3_ref, v0_ref, v1_ref, v2_ref, v3_ref, e_ref, mask_ref,
                        o_ref, bias_ref, *, tq):
    i = pl.program_id(1)

    @pl.when((pl.program_id(0) == 0) & (i == 0))
    def _():
        _build_bias(e_ref, mask_ref, bias_ref)

    k_refs = (k0_ref, k1_ref, k2_ref, k3_ref)
    v_refs = (v0_ref, v1_ref, v2_ref, v3_ref)
    for qb in range(2):
        pad = [jnp.where(2 * i + qb - 2 + j >= 0, 0.0, NEG) for j in range(2)] + [0.0]
        for p in range(N_PAIRS):
            cols = slice(p * LANES, (p + 1) * LANES)
            q = q_ref[qb * tq:(qb + 1) * tq, cols]
            ks = [r[:, cols] for r in k_refs[qb:qb + 3]]
            vs = [r[:, cols] for r in v_refs[qb:qb + 3]]
            out = jnp.zeros((tq, LANES), F32)
            for hh in range(2):
                msk = _head_mask(hh)
                qh = jnp.where(msk, q, jnp.zeros_like(q))
                scores = [_dot_nt(qh, ks[j]) + bias_ref[2 * p + hh, :, j * tq:(j + 1) * tq] + pad[j]
                          for j in range(3)]
                pv, l = _softmax_pv(scores, vs)
                out = jnp.where(msk, pv / l, out)
            o_ref[qb * tq:(qb + 1) * tq, cols] = out.astype(BF16)


def _band_prompt(qa, ka, va, bias_e, bias_mask, n_seq):
    t_total = qa.shape[0]
    tq = 256
    seq = t_total // n_seq
    nq = seq // tq
    qmap = lambda b, i: (b * (nq // 2) + i, 0)

    def kmap(j):
        return lambda b, i: (b * nq + jnp.maximum(2 * i - 2 + j, 0), 0)

    blk = lambda m: pl.BlockSpec((tq, WIDTH), m)
    qblk = pl.BlockSpec((2 * tq, WIDTH), qmap)
    return pl.pallas_call(
        functools.partial(_band_prompt_kernel, tq=tq),
        grid=(n_seq, nq // 2),
        in_specs=[qblk] + [blk(kmap(j)) for j in range(4)] + [blk(kmap(j)) for j in range(4)]
        + [_const_spec(bias_e.shape), _const_spec(bias_mask.shape)],
        out_specs=qblk,
        out_shape=jax.ShapeDtypeStruct((t_total, WIDTH), BF16),
        scratch_shapes=[pltpu.VMEM((N_HEADS,) + bias_mask.shape, F32)],
        compiler_params=_params(2),
        name="band_prompt",
    )(qa, ka, ka, ka, ka, va, va, va, va, bias_e, bias_mask)


def _band_sample_kernel(q_ref, kc_ref, vc_ref, kn_ref, vn_ref, e_ref, mask_ref, o_ref, ko_ref, vo_ref,
                        bias_ref):
    p_rows = kc_ref.shape[2]
    t = q_ref.shape[0]

    @pl.when(pl.program_id(0) == 0)
    def _():
        _build_bias(e_ref, mask_ref, bias_ref)

    pad = jnp.zeros((LANES - t, HEAD_DIM), BF16)
    for h in range(N_HEADS):
        cols = slice(h * HEAD_DIM, (h + 1) * HEAD_DIM)
        q = q_ref[:, cols]
        k_new, v_new = kn_ref[:, cols], vn_ref[:, cols]
        k_cache, v_cache = kc_ref[h], vc_ref[h]
        s_cache = _dot(q, k_cache.astype(BF16)) + bias_ref[h, :, 0:p_rows]
        s_new = (_dot_nt(q, jnp.concatenate([k_new.astype(BF16), pad], axis=0))
                 + bias_ref[h, :, p_rows:p_rows + LANES])
        chunks = [s_cache[:, c * LANES:(c + 1) * LANES] for c in range(p_rows // LANES)] + [s_new]
        cmax = chunks[0]
        for c in chunks[1:]:
            cmax = jnp.maximum(cmax, c)
        m = jnp.max(cmax, axis=-1, keepdims=True)
        ps = [jnp.exp2(c - m) for c in chunks]
        lsum = ps[0]
        for p in ps[1:]:
            lsum = lsum + p
        pv = (_dot_nt(jnp.concatenate(ps[:-1], axis=-1).astype(BF16), v_cache.astype(BF16))
              + _dot(ps[-1].astype(BF16), jnp.concatenate([v_new.astype(BF16), pad], axis=0)))
        o_ref[:, cols] = (pv / jnp.sum(lsum, axis=-1, keepdims=True)).astype(BF16)
        ko_ref[h] = jnp.concatenate([k_cache[:, t:], k_new.T], axis=1)
        vo_ref[h] = jnp.concatenate([v_cache[:, t:], v_new.T], axis=1)


def _band_sample(qa, ka_f, va_f, cache_kt, cache_vt, bias_e, bias_mask):
    n = cache_kt.shape[0]
    t = qa.shape[0] // n
    row = pl.BlockSpec((t, WIDTH), lambda b: (b, 0))
    cache = pl.BlockSpec((None,) + cache_kt.shape[1:], lambda b: (b, 0, 0, 0))
    return pl.pallas_call(
        _band_sample_kernel,
        grid=(n,),
        in_specs=[row, cache, cache, row, row, _const_spec(bias_e.shape), _const_spec(bias_mask.shape)],
        out_specs=[row, cache, cache],
        out_shape=[jax.ShapeDtypeStruct(qa.shape, BF16),
                   jax.ShapeDtypeStruct(cache_kt.shape, F32),
                   jax.ShapeDtypeStruct(cache_vt.shape, F32)],
        scratch_shapes=[pltpu.VMEM((N_HEADS,) + bias_mask.shape, F32)],
        compiler_params=_params(1),
        name="band_sample",
    )(qa, cache_kt, cache_vt, ka_f, va_f, bias_e, bias_mask)


def _band_bias(rel_bias, n_q, n_k, offset, band_mask, valid_k):
    assert n_q + n_k - 1 <= BIAS_PERIOD
    m = np.arange(BIAS_PERIOD)
    w = np.where(m < n_k, m, m - BIAS_PERIOD)
    idx = np.clip(offset - w, -MAX_REL, MAX_REL) + MAX_REL
    e = jnp.take(rel_bias.astype(F32), jnp.asarray(idx), axis=1) * LOG2E
    qi = np.arange(n_q)[:, None]
    kj = np.arange(n_k)[None, :]
    valid = np.broadcast_to(kj < valid_k, (n_q, n_k))
    if band_mask:
        rel_chunk = (qi + offset) // CHUNK - kj // CHUNK
        valid = valid & (rel_chunk >= 0) & (rel_chunk <= LEFT_CHUNKS)
    return e, jnp.asarray(np.where(valid, 0.0, NEG), F32)


def _build_bias(e_ref, mask_ref, bias_sc):
    rows, cols = mask_ref.shape
    for h in range(N_HEADS):
        spread = jnp.broadcast_to(e_ref[h:h + 1, :], (rows, BIAS_PERIOD))
        bias_sc[h] = pltpu.roll(spread, 0, 1, stride=1, stride_axis=0)[:, :cols] + mask_ref[...]


def _aug_head_mask(h_in_pair, pair):
    lane = lax.broadcasted_iota(jnp.int32, (1, 2 * LANES), 1)
    head = 2 * pair + h_in_pair
    in_q = (lane < LANES) & ((lane // HEAD_DIM) == h_in_pair)
    in_aug = (lane >= LANES) & (((lane - LANES) // AUG_STRIDE) == head)
    return in_q | in_aug


def _flash_update(s, v, m_ref, l_ref, acc_ref, h, row_bias=None, v_seq_minor=False):
    chunks = [s[:, c * LANES:(c + 1) * LANES] for c in range(s.shape[1] // LANES)]
    if row_bias is not None:
        chunks = [c + row_bias for c in chunks]
    cmax = chunks[0]
    for c in chunks[1:]:
        cmax = jnp.maximum(cmax, c)
    m_prev = m_ref[h]
    m_new = jnp.maximum(m_prev, jnp.max(cmax, axis=-1, keepdims=True))
    alpha = jnp.exp2(m_prev - m_new)
    ps = [jnp.exp2(c - m_new) for c in chunks]
    lsum = ps[0]
    for p in ps[1:]:
        lsum = lsum + p
    l_ref[h] = alpha * l_ref[h] + lsum
    p = jnp.concatenate(ps, axis=-1).astype(BF16)
    pv = _dot_nt(p, v) if v_seq_minor else _dot(p, v)
    acc_ref[h] = alpha[:, :pv.shape[1]] * acc_ref[h] + pv
    m_ref[h] = m_new


def _flash_result(l_ref, acc_ref, pair):
    outs = [acc_ref[2 * pair + hh] / jnp.sum(l_ref[2 * pair + hh], axis=-1, keepdims=True)
            for hh in range(2)]
    return jnp.where(_head_mask(0), outs[0], outs[1])


def _fox_prompt_kernel(qi_ref, kj_ref, q_ref, k_ref, v_ref, o_ref, qh_sc, m_sc, l_sc, acc_sc, *, tq, tk):
    t = pl.program_id(1)
    i = qi_ref[t]
    j = kj_ref[t]

    @pl.when(j == 0)
    def _():
        for p in range(N_PAIRS):
            q = q_ref[p]
            for hh in range(2):
                qh_sc[2 * p + hh] = jnp.where(_aug_head_mask(hh, p), q, jnp.zeros_like(q))
        m_sc[...] = jnp.full_like(m_sc, NEG)
        l_sc[...] = jnp.zeros_like(l_sc)
        acc_sc[...] = jnp.zeros_like(acc_sc)

    def step(diagonal):
        if diagonal:
            keep = (lax.broadcasted_iota(jnp.int32, (tq, tk), 1)
                    <= lax.broadcasted_iota(jnp.int32, (tq, tk), 0))
        for p in range(N_PAIRS):
            k = k_ref[p]
            v = v_ref[:, p * LANES:(p + 1) * LANES]
            for hh in range(2):
                s = _dot_nt(qh_sc[2 * p + hh], k)
                if diagonal:
                    s = jnp.where(keep, s, NEG)
                _flash_update(s, v, m_sc, l_sc, acc_sc, 2 * p + hh)

    @pl.when(j < i)
    def _():
        step(False)

    @pl.when(j == i)
    def _():
        step(True)
        for p in range(N_PAIRS):
            o_ref[:, p * LANES:(p + 1) * LANES] = _flash_result(l_sc, acc_sc, p).astype(BF16)


def _fox_prompt(qcat, kcat, vb, n_seq):
    seq = qcat.shape[2]
    tq = tk = 1024
    nt = seq // tq
    qi = np.concatenate([np.full(i + 1, i) for i in range(nt)]).astype(np.int32)
    kj = np.concatenate([np.arange(i + 1) for i in range(nt)]).astype(np.int32)
    grid_spec = pltpu.PrefetchScalarGridSpec(
        num_scalar_prefetch=2,
        grid=(n_seq, len(qi)),
        in_specs=[
            pl.BlockSpec((None, N_PAIRS, tq, 2 * LANES), lambda b, t, qi, kj: (b, 0, qi[t], 0)),
            pl.BlockSpec((None, N_PAIRS, tk, 2 * LANES), lambda b, t, qi, kj: (b, 0, kj[t], 0)),
            pl.BlockSpec((tk, WIDTH), lambda b, t, qi, kj: (b * nt + kj[t], 0)),
        ],
        out_specs=pl.BlockSpec((tq, WIDTH), lambda b, t, qi, kj: (b * nt + qi[t], 0)),
        scratch_shapes=[pltpu.VMEM((N_HEADS, tq, 2 * LANES), BF16), pltpu.VMEM((N_HEADS, tq, LANES), F32),
                        pltpu.VMEM((N_HEADS, tq, LANES), F32), pltpu.VMEM((N_HEADS, tq, LANES), F32)],
    )
    return pl.pallas_call(
        functools.partial(_fox_prompt_kernel, tq=tq, tk=tk),
        grid_spec=grid_spec,
        out_shape=jax.ShapeDtypeStruct(vb.shape, BF16),
        compiler_params=_params(2),
        name="fox_prompt",
    )(jnp.asarray(qi), jnp.asarray(kj), qcat, kcat, vb)


def _suffix_sum_exclusive(x):
    n = x.shape[1]
    lane = lax.broadcasted_iota(jnp.int32, x.shape, 1)
    y = jnp.where(lane + 1 < n, pltpu.roll(x, n - 1, axis=1), 0.0)
    shift = 1
    while shift < n:
        y = y + jnp.where(lane + shift < n, pltpu.roll(y, n - shift, axis=1), 0.0)
        shift *= 2
    return y


def _fox_sample_kernel(q_ref, kc_ref, vc_ref, kn_ref, vn_ref, lfc_ref, lfn_ref, u_ref,
                       o_ref, cq_sc, car_sc, m_sc, l_sc, acc_sc, *, n_cache_tiles, sub_keys):
    j = pl.program_id(1)
    t = q_ref.shape[1]

    def cum_new():
        hi, mid, lo = _split3(lfn_ref[...])
        u = u_ref[...]
        return _dot(hi, u) + _dot(mid, u) + _dot(lo, u)

    def q_head(h):
        return q_ref[h // 2, :, (h % 2) * HEAD_DIM:(h % 2 + 1) * HEAD_DIM]

    @pl.when(j == 0)
    def _():
        cn = cum_new() * LOG2E
        eye = (lax.broadcasted_iota(jnp.int32, (t, LANES), 0)
               == lax.broadcasted_iota(jnp.int32, (t, LANES), 1))
        for h in range(N_HEADS):
            col = jnp.sum(jnp.where(eye, jnp.broadcast_to(cn[h:h + 1, :], (t, LANES)), 0.0),
                          axis=-1, keepdims=True)
            cq_sc[h] = jnp.broadcast_to(col, (t, LANES))
        car_sc[...] = jnp.zeros_like(car_sc)
        m_sc[...] = jnp.full_like(m_sc, NEG)
        l_sc[...] = jnp.zeros_like(l_sc)
        acc_sc[...] = jnp.zeros_like(acc_sc)

    @pl.when(j < n_cache_tiles)
    def _():
        lf = lfc_ref[...]
        r = _suffix_sum_exclusive(lf) + car_sc[:, 0:1]
        car_sc[...] = jnp.broadcast_to(r[:, 0:1] + lf[:, 0:1], car_sc.shape)
        r = r * LOG2E
        for sub in range(lf.shape[1] // sub_keys):
            keys = slice(sub * sub_keys, (sub + 1) * sub_keys)
            for h in range(N_HEADS):
                s = _dot(q_head(h), kc_ref[h, :, keys].astype(BF16)) + r[h:h + 1, keys]
                _flash_update(s, vc_ref[h, :, keys].astype(BF16), m_sc, l_sc, acc_sc, h,
                              row_bias=cq_sc[h], v_seq_minor=True)

    @pl.when(j == n_cache_tiles)
    def _():
        cn = cum_new() * LOG2E
        causal = (lax.broadcasted_iota(jnp.int32, (t, LANES), 1)
                  <= lax.broadcasted_iota(jnp.int32, (t, LANES), 0))
        pad = jnp.zeros((LANES - t, HEAD_DIM), BF16)
        for h in range(N_HEADS):
            cols = slice(h * HEAD_DIM, (h + 1) * HEAD_DIM)
            k = jnp.concatenate([kn_ref[h // 2, :, (h % 2) * HEAD_DIM:(h % 2 + 1) * HEAD_DIM], pad], axis=0)
            v = jnp.concatenate([vn_ref[:, cols], pad], axis=0)
            s = _dot_nt(q_head(h), k) + cq_sc[h] - cn[h:h + 1, :]
            _flash_update(jnp.where(causal, s, NEG), v, m_sc, l_sc, acc_sc, h)
            o_ref[:, cols] = (acc_sc[h] / jnp.sum(l_sc[h], axis=-1, keepdims=True)).astype(BF16)


def _fox_sample(qcat, kcat, vb, cache_kt, cache_vt, lf_cache_t, lf_new_t):
    n, _, _, p_rows = cache_kt.shape
    t = vb.shape[0] // n
    tk = 2048
    nct = p_rows // tk
    u = jnp.asarray(np.pad(np.triu(np.ones((t, t))), ((0, 0), (0, LANES - t))), BF16)
    rev = lambda j: jnp.maximum(nct - 1 - j, 0)
    cache = pl.BlockSpec((None, N_HEADS, HEAD_DIM, tk), lambda b, j: (b, 0, 0, rev(j)))
    new_cat = pl.BlockSpec((None, N_PAIRS, t, 2 * LANES), lambda b, j: (0, 0, b, 0))
    return pl.pallas_call(
        functools.partial(_fox_sample_kernel, n_cache_tiles=nct, sub_keys=tk),
        grid=(n, nct + 1),
        in_specs=[
            new_cat, cache, cache, new_cat,
            pl.BlockSpec((t, WIDTH), lambda b, j: (b, 0)),
            pl.BlockSpec((None, N_HEADS, tk), lambda b, j: (b, 0, rev(j))),
            pl.BlockSpec((None, N_HEADS, t), lambda b, j: (b, 0, 0)),
            _const_spec(u.shape),
        ],
        out_specs=pl.BlockSpec((t, WIDTH), lambda b, j: (b, 0)),
        out_shape=jax.ShapeDtypeStruct(vb.shape, BF16),
        scratch_shapes=[pltpu.VMEM((N_HEADS, t, LANES), F32), pltpu.VMEM((N_HEADS, LANES), F32),
                        pltpu.VMEM((N_HEADS, t, LANES), F32), pltpu.VMEM((N_HEADS, t, LANES), F32),
                        pltpu.VMEM((N_HEADS, t, HEAD_DIM), F32)],
        compiler_params=_params(2),
        name="fox_sample",
    )(qcat, cache_kt, cache_vt, kcat, vb, lf_cache_t, lf_new_t, u)


GROUP_LANE = 64
ROW_ALIGN = 16
MOE_TILE = 512
SORT_ROWS = 640
XS_WIDTH = D_MODEL + 2 * LANES


def _route(r):
    lane_i = lax.broadcasted_iota(jnp.int32, r.shape, 1)
    lane = lane_i.astype(F32)
    lane_grp = (lane_i // EXPERTS_PER_GROUP).astype(F32)
    big = float(LANES)
    is_coarse = (lane_i >= N_EXPERTS) & (lane_i < N_EXPERTS + N_GROUPS)
    cm = jnp.where(is_coarse, r, NEG)
    cmax = cm.max(axis=-1, keepdims=True)
    grp = jnp.min(jnp.where(cm == cmax, lane - N_EXPERTS, big), axis=-1, keepdims=True)
    pg_sel = 1.0 / jnp.sum(jnp.exp(cm - cmax), axis=-1, keepdims=True)
    in_grp = (lane_i < N_EXPERTS) & (lane_grp == grp)
    fm = jnp.where(in_grp, r, NEG)
    m1 = fm.max(axis=-1, keepdims=True)
    denom = jnp.sum(jnp.exp(fm - m1), axis=-1, keepdims=True)
    i1 = jnp.min(jnp.where(fm == m1, lane, big), axis=-1, keepdims=True)
    fm2 = jnp.where(lane == i1, NEG, fm)
    m2 = fm2.max(axis=-1, keepdims=True)
    i2 = jnp.min(jnp.where(fm2 == m2, lane, big), axis=-1, keepdims=True)
    p1 = 1.0 / denom
    p2 = jnp.exp(m2 - m1) / denom
    tot = p1 + p2
    comb = (jnp.where(lane == i1, pg_sel * (p1 / tot), 0.0)
            + jnp.where(lane == i2, pg_sel * (p2 / tot), 0.0))
    return comb, grp


def _merge_kernel(x_ref, oa_ref, ob_ref, g_ref, wg_ref, wpa_ref, wpb_ref, wo_ref, gf_ref, wrh_ref, wrm_ref,
                  br_ref, y_ref, hx_ref, route_ref, grp_t_ref, cnt_ref):
    x = x_ref[...]
    h = _rms(x, g_ref[...]).astype(BF16)
    gate = jax.nn.sigmoid(_dot(h, wg_ref[...]))
    mix = (gate[:, :D_MODEL] * _dot(oa_ref[...], wpa_ref[...])
           + gate[:, D_MODEL:] * _dot(ob_ref[...], wpb_ref[...]))
    y = x + _dot(mix.astype(BF16), wo_ref[...])
    y_ref[...] = y

    hx = _rms(y, gf_ref[...])
    hx_ref[:, 0:D_MODEL] = hx.astype(BF16)
    h_hi, h_mid, _ = _split3(hx)
    r = _dot(h_hi, wrh_ref[...]) + _dot(h_hi, wrm_ref[...]) + _dot(h_mid, wrh_ref[...]) + br_ref[...]
    comb, grp = _route(r)
    lane = lax.broadcasted_iota(jnp.int32, comb.shape, 1)
    route = jnp.where(lane == GROUP_LANE, grp, comb)
    route_ref[...] = route
    r_hi = route.astype(BF16)
    hx_ref[:, D_MODEL:D_MODEL + LANES] = r_hi
    hx_ref[:, D_MODEL + LANES:] = (route - r_hi.astype(F32)).astype(BF16)
    grp_t_ref[...] = route.T[GROUP_LANE:GROUP_LANE + 8, :]
    cnt = jnp.sum(jnp.where(lane.astype(F32) == grp, 1.0, 0.0), axis=0, keepdims=True)
    cnt_ref[...] = jnp.broadcast_to(cnt, cnt_ref.shape)


def _merge(x, oa, ob, w):
    t_total = x.shape[0]
    tm = MOE_TILE
    n_tiles = t_total // tm
    row = lambda n: pl.BlockSpec((tm, n), lambda i: (i, 0))
    consts = [w["g_mix"], w["w_gate"], w["w_pa"], w["w_pb"], w["w_o"], w["g_ffn"], w["w_router_hi"],
              w["w_router_mid"], w["b_router"]]
    return pl.pallas_call(
        _merge_kernel,
        grid=(n_tiles,),
        in_specs=[row(D_MODEL), row(WIDTH), row(WIDTH)] + [_const_spec(c.shape) for c in consts],
        out_specs=[row(D_MODEL), row(XS_WIDTH), row(LANES), pl.BlockSpec((8, tm), lambda i: (0, i)),
                   pl.BlockSpec((None, 8, LANES), lambda i: (i, 0, 0))],
        out_shape=[jax.ShapeDtypeStruct(x.shape, F32), jax.ShapeDtypeStruct((t_total, XS_WIDTH), BF16),
                   jax.ShapeDtypeStruct((t_total, LANES), F32), jax.ShapeDtypeStruct((8, t_total), F32),
                   jax.ShapeDtypeStruct((n_tiles, 8, LANES), F32)],
        compiler_params=_params(1),
        name="merge",
    )(x, oa, ob, *consts)


def _plan_kernel(cnt_ref, ib_ref, dst_ref, len_ref, fill_ref, inblk_ref, outblk_ref, grp_ref, kind_ref, *,
                 n_tiles, n_steps):
    align_bits = ROW_ALIGN.bit_length() - 1
    tile_bits = MOE_TILE.bit_length() - 1
    zero = jnp.int32(0)

    def per_tile(t, tot):
        run = zero
        new_tot = []
        for g in range(N_GROUPS):
            n = ((cnt_ref[t * N_GROUPS + g] + (ROW_ALIGN - 1)) >> align_bits) << align_bits
            ib_ref[t * N_GROUPS + g] = run
            len_ref[t * N_GROUPS + g] = n
            dst_ref[t * N_GROUPS + g] = tot[g]
            run = run + n
            new_tot.append(tot[g] + n)
        return tuple(new_tot)

    tot = lax.fori_loop(0, n_tiles, per_tile, (zero,) * N_GROUPS)
    n_ffn = [(tot[g] + (MOE_TILE - 1)) >> tile_bits for g in range(N_GROUPS)]
    g_base, ends, acc_rows, acc_tiles = [], [], zero, zero
    for g in range(N_GROUPS):
        g_base.append(acc_rows)
        acc_rows = acc_rows + ((n_ffn[g] + 1) << tile_bits)
        acc_tiles = acc_tiles + n_ffn[g] + 1
        ends.append(acc_tiles)

    def add_base(t, carry):
        for g in range(N_GROUPS):
            dst_ref[t * N_GROUPS + g] = dst_ref[t * N_GROUPS + g] + g_base[g]
        return carry

    lax.fori_loop(0, n_tiles, add_base, zero)
    for g in range(N_GROUPS):
        fill_ref[g] = g_base[g] + tot[g]
        fill_ref[N_GROUPS + g] = g_base[g] + (n_ffn[g] << tile_bits)
    fill_ref[2 * N_GROUPS] = ends[-1]
    first_blk = g_base[N_GROUPS - 1]
    for g in reversed(range(N_GROUPS - 1)):
        first_blk = jnp.where(n_ffn[g] > 0, g_base[g], first_blk)
    first_blk = first_blk >> tile_bits

    def pick(vals, g):
        out = vals[N_GROUPS - 1]
        for i in reversed(range(N_GROUPS - 1)):
            out = jnp.where(g == i, vals[i], out)
        return out

    def per_step(k, carry):
        kk = jnp.minimum(k, ends[-1] - 1)
        g = zero
        for i in range(N_GROUPS - 1):
            g = g + jnp.where(kk >= ends[i], 1, 0)
        j = kk - pick([zero] + ends[:-1], g)
        is_expert = j < pick(n_ffn, g)
        out_blk = jnp.where(k < ends[-1], (pick(g_base, g) >> tile_bits) + j, k)
        outblk_ref[k] = out_blk
        inblk_ref[k] = jnp.where(is_expert, out_blk, first_blk)
        grp_ref[k] = g
        kind_ref[k] = jnp.where(is_expert, 1, 0)
        return carry

    lax.fori_loop(0, n_steps, per_step, zero)


def _moe_plan(cnt, t_total):
    n_tiles = t_total // MOE_TILE
    n_steps = (n_tiles + 2 * N_GROUPS + (N_GROUPS * (ROW_ALIGN - 1) * n_tiles + MOE_TILE - 1) // MOE_TILE)
    smem = pl.BlockSpec(memory_space=pltpu.SMEM)
    sizes = [n_tiles * N_GROUPS] * 3 + [2 * N_GROUPS + 1] + [n_steps] * 4
    outs = pl.pallas_call(
        functools.partial(_plan_kernel, n_tiles=n_tiles, n_steps=n_steps),
        grid_spec=pltpu.PrefetchScalarGridSpec(num_scalar_prefetch=1, grid=(1,), in_specs=[],
                                               out_specs=[smem] * len(sizes)),
        out_shape=[jax.ShapeDtypeStruct((n,), jnp.int32) for n in sizes],
        compiler_params=_params(1),
        name="moe_plan",
    )(cnt.reshape(-1))
    return (*outs, n_steps)


def _sort_kernel(ib_ref, dst_ref, len_ref, fill_ref, hx_ref, grp_t_ref, tri_ref, xs_ref, xsort, sems):
    i = pl.program_id(0)
    n = pl.num_programs(0)
    tm = hx_ref.shape[0]
    half = tm // 2
    slot = i % 2

    def copy(src, dst, rows, sl, g, part):
        return pltpu.make_async_copy(xsort.at[sl, pl.ds(pl.multiple_of(src, ROW_ALIGN), rows)],
                                     xs_ref.at[pl.ds(pl.multiple_of(dst, ROW_ALIGN), rows)], sems.at[sl, g, part])

    def for_segments(step, sl, act):
        for g in range(N_GROUPS):
            src, dst = ib_ref[step * N_GROUPS + g], dst_ref[step * N_GROUPS + g]
            act(copy(src, dst, half, sl, g, 0))

            @pl.when(len_ref[step * N_GROUPS + g] > half)
            def _():
                act(copy(src + half, dst + half, half, sl, g, 1))

    def run(cs):
        for c in cs:
            c.start()
        for c in cs:
            c.wait()

    @pl.when(i == 0)
    def _():
        for sl in range(2):
            xsort[sl, SORT_ROWS:, :] = jnp.zeros((xsort.shape[1] - SORT_ROWS, XS_WIDTH), BF16)

    g_row = grp_t_ref[0:1, :]
    sub = lax.broadcasted_iota(jnp.int32, (8, tm), 0).astype(F32)
    mine = sub == g_row
    before = _dot(jnp.where(mine, 1.0, 0.0).astype(BF16), tri_ref[...])
    dest = jnp.sum(jnp.where(mine, before, 0.0), axis=0, keepdims=True)
    for g in range(N_GROUPS):
        base = jnp.full(dest.shape, ib_ref[i * N_GROUPS + g], jnp.int32).astype(F32)
        dest = dest + jnp.where(g_row == g, base, 0.0)
    rows = lax.broadcasted_iota(jnp.int32, (SORT_ROWS, tm), 0).astype(F32)
    perm = jnp.where(rows == dest, 1.0, 0.0).astype(BF16)
    xsort[slot, 0:SORT_ROWS, :] = _dot(perm, hx_ref[...]).astype(BF16)

    @pl.when(i > 0)
    def _():
        for_segments(i - 1, 1 - slot, lambda c: c.wait())

    for_segments(i, slot, lambda c: c.start())

    @pl.when(i == n - 1)
    def _():
        for_segments(i, slot, lambda c: c.wait())
        run([copy(SORT_ROWS, fill_ref[g], tm, slot, g, 0) for g in range(N_GROUPS)])
        run([copy(SORT_ROWS, fill_ref[N_GROUPS + g], tm, slot, g, 0) for g in range(N_GROUPS)])

        @pl.loop(fill_ref[2 * N_GROUPS], xs_ref.shape[0] // tm)
        def _(b):
            run([copy(SORT_ROWS, b * tm, tm, slot, 0, 0)])


def _expert_kernel(ib_ref, ob_ref, grp_ref, kind_ref, xs_ref, w1_ref, w3_ref, w2_ref, ys_ref):
    del ib_ref, ob_ref
    k = pl.program_id(0)

    @pl.when(kind_ref[k] == 1)
    def _():
        x = xs_ref[:, 0:D_MODEL]
        side = lambda w_ref: jnp.concatenate([w_ref[e].astype(BF16) for e in range(EXPERTS_PER_GROUP)], axis=1)
        hid = jax.nn.silu(_dot(x, side(w1_ref))) * _dot(x, side(w3_ref))
        comb = (xs_ref[:, D_MODEL:D_MODEL + LANES].astype(F32) + xs_ref[:, D_MODEL + LANES:].astype(F32))
        lane = lax.broadcasted_iota(jnp.int32, comb.shape, 1)
        first = grp_ref[k] * EXPERTS_PER_GROUP
        scale = []
        for e in range(EXPERTS_PER_GROUP):
            w_e = jnp.sum(jnp.where(lane == first + e, comb, 0.0), axis=-1, keepdims=True)
            scale.append(jnp.broadcast_to(w_e, (comb.shape[0], D_EXPERT)))
        hid = hid * jnp.concatenate(scale, axis=-1)
        w2 = w2_ref[...].astype(BF16).reshape(EXPERTS_PER_GROUP * D_EXPERT, D_MODEL)
        ys_ref[...] = _dot(hid.astype(BF16), w2).astype(BF16)

    @pl.when(kind_ref[k] == 0)
    def _():
        ys_ref[...] = jnp.zeros_like(ys_ref)


def _unsort_kernel(ib_ref, dst_ref, len_ref, y_ref, route_ref, tri_ref, ys_ref, o_ref, ybuf, yasm, sems):
    i = pl.program_id(0)
    n = pl.num_programs(0)
    tm = y_ref.shape[0]
    half = tm // 2
    slot = i % 2

    def fetch(step, sl, act):
        for g in range(N_GROUPS):
            src = pl.multiple_of(dst_ref[step * N_GROUPS + g], ROW_ALIGN)
            act(pltpu.make_async_copy(ys_ref.at[pl.ds(src, half)], ybuf.at[sl, g, 0:half], sems.at[sl, g, 0]))

            @pl.when(len_ref[step * N_GROUPS + g] > half)
            def _():
                act(pltpu.make_async_copy(ys_ref.at[pl.ds(src + half, half)], ybuf.at[sl, g, half:tm],
                                          sems.at[sl, g, 1]))

    @pl.when(i == 0)
    def _():
        yasm[...] = jnp.zeros_like(yasm)
        fetch(0, 0, lambda c: c.start())

    @pl.when(i + 1 < n)
    def _():
        fetch(i + 1, 1 - slot, lambda c: c.start())

    route = route_ref[...]
    lane = lax.broadcasted_iota(jnp.int32, route.shape, 1).astype(F32)
    grp = jnp.sum(jnp.where(lane == GROUP_LANE, route, 0.0), axis=-1, keepdims=True)
    mine = lane == grp
    before = _dot(tri_ref[...], jnp.where(mine, 1.0, 0.0).astype(BF16))
    dest = jnp.sum(jnp.where(mine, before, 0.0), axis=-1, keepdims=True)
    for g in range(N_GROUPS):
        base = jnp.full(dest.shape, ib_ref[i * N_GROUPS + g], jnp.int32).astype(F32)
        dest = dest + jnp.where(grp == g, base, 0.0)
    cols = lax.broadcasted_iota(jnp.int32, (tm, SORT_ROWS), 1).astype(F32)
    perm_t = jnp.where(cols == dest, 1.0, 0.0).astype(BF16)

    fetch(i, slot, lambda c: c.wait())
    for g in range(N_GROUPS):
        row0 = pl.multiple_of(ib_ref[i * N_GROUPS + g], ROW_ALIGN)
        yasm[pl.ds(row0, half), :] = ybuf[slot, g, 0:half]

        @pl.when(len_ref[i * N_GROUPS + g] > half)
        def _():
            yasm[pl.ds(row0 + half, half), :] = ybuf[slot, g, half:tm]
    o_ref[...] = y_ref[...] + _dot(perm_t, yasm[0:SORT_ROWS, :])


def _moe(y, hx, route, grp_t, cnt, w):
    t_total = y.shape[0]
    tm = MOE_TILE
    n_tiles = t_total // tm
    assert SORT_ROWS >= tm + N_GROUPS * (ROW_ALIGN - 1)
    in_base, dst, seg_len, fill, in_blk, out_blk, grp_of_step, is_expert, n_steps = _moe_plan(
        cnt[:, 0, :N_GROUPS].astype(jnp.int32), t_total)
    cap_rows = n_steps * tm
    lower = jnp.asarray(np.tril(np.ones((tm, tm)), -1), BF16)
    upper = jnp.asarray(np.triu(np.ones((tm, tm)), 1), BF16)
    stage_rows = SORT_ROWS + tm
    any_spec = pl.BlockSpec(memory_space=pl.ANY)

    xs = pl.pallas_call(
        _sort_kernel,
        grid_spec=pltpu.PrefetchScalarGridSpec(
            num_scalar_prefetch=4,
            grid=(n_tiles,),
            in_specs=[pl.BlockSpec((tm, XS_WIDTH), lambda i, *_: (i, 0)),
                      pl.BlockSpec((8, tm), lambda i, *_: (0, i)),
                      pl.BlockSpec((tm, tm), lambda i, *_: (0, 0))],
            out_specs=any_spec,
            scratch_shapes=[pltpu.VMEM((2, stage_rows, XS_WIDTH), BF16),
                            pltpu.SemaphoreType.DMA((2, N_GROUPS, 2))],
        ),
        out_shape=jax.ShapeDtypeStruct((cap_rows, XS_WIDTH), BF16),
        compiler_params=_params(1),
        name="moe_sort",
    )(in_base, dst, seg_len, fill, hx, grp_t, upper)

    step_map = lambda which: (lambda k, ib, ob, grp, kind: ((ib, ob, grp)[which][k], 0))
    wmap = lambda k, ib, ob, grp, kind: (grp[k], 0, 0)
    wspec = pl.BlockSpec((EXPERTS_PER_GROUP, D_MODEL, D_EXPERT), wmap)
    ys = pl.pallas_call(
        _expert_kernel,
        grid_spec=pltpu.PrefetchScalarGridSpec(
            num_scalar_prefetch=4,
            grid=(n_steps,),
            in_specs=[pl.BlockSpec((tm, XS_WIDTH), step_map(0)), wspec, wspec, pl.BlockSpec((EXPERTS_PER_GROUP, D_EXPERT, D_MODEL), wmap)],
            out_specs=pl.BlockSpec((tm, D_MODEL), step_map(1)),
        ),
        out_shape=jax.ShapeDtypeStruct((cap_rows, D_MODEL), BF16),
        compiler_params=_params(1),
        name="moe_experts",
    )(in_blk, out_blk, grp_of_step, is_expert, xs, w["w1"], w["w3"], w["w2"])

    return pl.pallas_call(
        _unsort_kernel,
        grid_spec=pltpu.PrefetchScalarGridSpec(
            num_scalar_prefetch=3,
            grid=(n_tiles,),
            in_specs=[pl.BlockSpec((tm, D_MODEL), lambda i, *_: (i, 0)),
                      pl.BlockSpec((tm, LANES), lambda i, *_: (i, 0)),
                      pl.BlockSpec((tm, tm), lambda i, *_: (0, 0)),
                      any_spec],
            out_specs=pl.BlockSpec((tm, D_MODEL), lambda i, *_: (i, 0)),
            scratch_shapes=[pltpu.VMEM((2, N_GROUPS, tm, D_MODEL), BF16),
                            pltpu.VMEM((stage_rows, D_MODEL), BF16),
                            pltpu.SemaphoreType.DMA((2, N_GROUPS, 2))],
        ),
        out_shape=jax.ShapeDtypeStruct(y.shape, F32),
        compiler_params=_params(1),
        name="moe_unsort",
    )(in_base, dst, seg_len, y, route, lower, ys)


def _prep_weights(g_mix, w_in, b_f, q_norm_a, k_norm_a, q_norm_b, k_norm_b, w_pa, w_pb, w_o,
                  g_ffn, w_rg, b_rg, w_re, b_re, w1, w3, w2):
    n_qkv = 6 * WIDTH
    tile = lambda g: jnp.tile(g, N_HEADS)
    w_router = jnp.concatenate(
        [jnp.transpose(w_re, (1, 0, 2)).reshape(D_MODEL, N_EXPERTS), w_rg,
         jnp.zeros((D_MODEL, LANES - N_EXPERTS - N_GROUPS), F32)], axis=1)
    b_router = jnp.concatenate(
        [b_re.reshape(N_EXPERTS), b_rg, jnp.zeros((LANES - N_EXPERTS - N_GROUPS,), F32)])[None, :]
    return {
        "g_mix": g_mix[None, :],
        "w_qkv": w_in[:, :n_qkv].astype(BF16),
        "w_f": jnp.pad(w_in[:, n_qkv:n_qkv + N_HEADS], ((0, 0), (0, LANES - N_HEADS))).astype(BF16),
        "b_f": jnp.pad(b_f, (0, LANES - N_HEADS))[None, :],
        "w_gate": w_in[:, n_qkv + N_HEADS:].astype(BF16),
        "gains": jnp.stack([tile(q_norm_a), tile(k_norm_a), tile(q_norm_b), tile(k_norm_b)]),
        "w_pa": w_pa.astype(BF16), "w_pb": w_pb.astype(BF16), "w_o": w_o.astype(BF16),
        "g_ffn": g_ffn[None, :],
        "w_router_hi": w_router.astype(BF16),
        "w_router_mid": (w_router - w_router.astype(BF16).astype(F32)).astype(BF16),
        "b_router": b_router,
        "w1": w1, "w3": w3, "w2": w2,
    }


def kernel(x_prompt, x_sample, cache_a_k, cache_a_v, cache_b_k, cache_b_v, cache_b_logf, g_mix, w_in, b_f, q_norm_a, k_norm_a, q_norm_b, k_norm_b, rel_bias, w_pa, w_pb, w_o, g_ffn, w_rg, b_rg, w_re, b_re, w1, w3, w2):
    assert g_mix.shape[0] == 1, "single-layer step"
    n_p, seq, _ = x_prompt.shape
    n_s, t_s, _ = x_sample.shape
    a_rows = cache_a_k.shape[2]
    w = _prep_weights(g_mix[0], w_in[0], b_f[0], q_norm_a[0], k_norm_a[0], q_norm_b[0], k_norm_b[0],
                      w_pa[0], w_pb[0], w_o[0], g_ffn[0], w_rg[0], b_rg[0], w_re[0], b_re[0],
                      w1[0], w3[0], w2[0])
    band_tq = 256
    bias_prompt = _band_bias(rel_bias[0], band_tq, 3 * band_tq, 2 * band_tq, True, 3 * band_tq)
    bias_sample = _band_bias(rel_bias[0], t_s, a_rows + LANES, a_rows, False, a_rows + t_s)

    seq_minor = lambda a: jnp.transpose(a, (0, 2, 3, 1))
    seq_major = lambda a: jnp.transpose(a, (0, 3, 1, 2))[None]

    xp = x_prompt.reshape(n_p * seq, D_MODEL)
    (qa, ka, va, ka_t, va_t, qcat, kcat, vb, kb_t, vb_t, logf_t) = _inproj(xp, n_p, w, seq_minor=True)
    o_a = _band_prompt(qa, ka, va, *bias_prompt, n_p)
    o_b = _fox_prompt(qcat, kcat, vb, n_p)
    y_p = _moe(*_merge(xp, o_a, o_b, w), w)

    xs = x_sample.reshape(n_s * t_s, D_MODEL)
    (qa_s, _, _, ka_fs, va_fs, qcat_s, kcat_s, vb_s, kb_fs, vb_fs, logf_s) = _inproj(
        xs, 1, w, seq_minor=False)
    o_as, new_ak_t, new_av_t = _band_sample(
        qa_s, ka_fs, va_fs, seq_minor(cache_a_k[0]), seq_minor(cache_a_v[0]), *bias_sample)
    lf_cache_t = jnp.transpose(cache_b_logf[0], (0, 2, 1))
    lf_new_t = jnp.transpose(logf_s.reshape(n_s, t_s, N_HEADS), (0, 2, 1))
    o_bs = _fox_sample(qcat_s, kcat_s, vb_s, seq_minor(cache_b_k[0]), seq_minor(cache_b_v[0]),
                       lf_cache_t, lf_new_t)
    y_s = _moe(*_merge(xs, o_as, o_bs, w), w)

    heads = lambda a, n, r: a.reshape(1, n, r, N_HEADS, HEAD_DIM)
    return (y_p.reshape(n_p, seq, D_MODEL), y_s.reshape(n_s, t_s, D_MODEL),
            seq_major(ka_t), seq_major(va_t), seq_major(kb_t), seq_major(vb_t),
            jnp.transpose(logf_t, (0, 2, 1))[None],
            seq_major(new_ak_t), seq_major(new_av_t),
            heads(kb_fs, n_s, t_s), heads(vb_fs, n_s, t_s), logf_s.reshape(1, n_s, t_s, N_HEADS))
```

```python
import functools

import numpy as np
import jax
import jax.numpy as jnp
from jax import lax
from jax.experimental import pallas as pl
from jax.experimental.pallas import tpu as pltpu

F32 = jnp.float32
BF16 = jnp.bfloat16

D_MODEL = 1024
HEAD_DIM = 64
N_HEADS = 8
WIDTH = N_HEADS * HEAD_DIM
N_PAIRS = N_HEADS // 2
CHUNK = 64
LEFT_CHUNKS = 8
WINDOW_ROWS = LEFT_CHUNKS * CHUNK
MAX_REL = 256
N_GROUPS = 4
EXPERTS_PER_GROUP = 8
N_EXPERTS = N_GROUPS * EXPERTS_PER_GROUP
D_EXPERT = 128
EPS = 1e-6
NEG = -1e30
LOG2E = 1.4426950408889634
LANES = 128
BIAS_PERIOD = 1024
AUG_STRIDE = 8
VMEM_LIMIT = 56 * 1024 * 1024

_NT = (((1,), (1,)), ((), ()))


def _dot(a, b):
    return jnp.dot(a, b, preferred_element_type=F32)


def _dot_nt(a, b):
    return lax.dot_general(a, b, _NT, preferred_element_type=F32)


def _split3(x):
    hi = x.astype(BF16)
    r = x - hi.astype(F32)
    mid = r.astype(BF16)
    lo = (r - mid.astype(F32)).astype(BF16)
    return hi, mid, lo


def _dot3(a_bf, x):
    hi, mid, lo = _split3(x)
    return _dot(a_bf, hi) + _dot(a_bf, mid) + _dot(a_bf, lo)


def _rms(x, g):
    ms = jnp.mean(x * x, axis=-1, keepdims=True)
    return x * lax.rsqrt(ms + EPS) * g


def _params(n_axes):
    return pltpu.CompilerParams(dimension_semantics=("arbitrary",) * n_axes,
                                vmem_limit_bytes=VMEM_LIMIT)


def _const_spec(shape):
    nd = len(shape)
    return pl.BlockSpec(shape, lambda *_: (0,) * nd)


def _inproj_kernel(x_ref, g_ref, wqkv_ref, wf_ref, bf_ref, gains_ref, bd_ref, ltri_ref,
                   selq_ref, selk_ref, oneq_ref, onek_ref,
                   qa_ref, ka_ref, va_ref, kaf_ref, vaf_ref, qcat_ref, kcat_ref, vb_ref,
                   kbf_ref, vbf_ref, logf_ref, carry_ref, *, tiles_per_seq, seq_minor):
    i = pl.program_id(0)
    tm = x_ref.shape[0]

    @pl.when(i % tiles_per_seq == 0)
    def _():
        carry_ref[...] = jnp.zeros_like(carry_ref)

    def put_f32(ref, y, last_tile_only=False):
        if not seq_minor:
            ref[...] = y
        elif last_tile_only:
            @pl.when(i % tiles_per_seq == tiles_per_seq - 1)
            def _():
                ref[...] = y.T.reshape(N_HEADS, HEAD_DIM, tm)
        else:
            ref[...] = y.T.reshape(N_HEADS, HEAD_DIM, tm)

    h = _rms(x_ref[...], g_ref[...]).astype(BF16)

    def seg(s):
        return _dot(h, wqkv_ref[:, s * WIDTH:(s + 1) * WIDTH])

    def headnorm(y, n):
        ss = _dot((y * y).astype(BF16), bd_ref[...])
        return y * lax.rsqrt(ss * (1.0 / HEAD_DIM) + EPS) * gains_ref[n:n + 1, :]

    q_a = headnorm(seg(0), 0)
    qa_ref[...] = (q_a * (HEAD_DIM ** -0.5 * LOG2E)).astype(BF16)
    k_a = headnorm(seg(1), 1)
    ka_ref[...] = k_a.astype(BF16)
    put_f32(kaf_ref, k_a, last_tile_only=True)
    v_a = seg(2)
    va_ref[...] = v_a.astype(BF16)
    put_f32(vaf_ref, v_a, last_tile_only=True)

    z = _dot(h, wf_ref[...]) + bf_ref[...]
    logf = jnp.minimum(z, 0.0) - jnp.log(1.0 + jnp.exp(-jnp.abs(z)))
    if seq_minor:
        logf_ref[...] = logf.T[0:N_HEADS, :]
    else:
        logf_ref[...] = logf[:, :N_HEADS]
    c = _dot3(ltri_ref[...], logf) + carry_ref[0:1, :]
    carry_ref[...] = jnp.broadcast_to(c[-1:, :], carry_ref.shape)
    cs = jnp.concatenate(_split3(c * LOG2E), axis=-1)
    q_aug = (_dot(cs, selq_ref[...]) + oneq_ref[...]).astype(BF16)
    k_aug = (_dot(cs, selk_ref[...]) + onek_ref[...]).astype(BF16)

    q_b = (headnorm(seg(3), 2) * (HEAD_DIM ** -0.5 * LOG2E)).astype(BF16)
    k_b = headnorm(seg(4), 3)
    put_f32(kbf_ref, k_b)
    k_b = k_b.astype(BF16)
    v_b = seg(5)
    put_f32(vbf_ref, v_b)
    vb_ref[...] = v_b.astype(BF16)
    for p in range(N_PAIRS):
        cols = slice(p * LANES, (p + 1) * LANES)
        qcat_ref[p, :, 0:LANES] = q_b[:, cols]
        qcat_ref[p, :, LANES:2 * LANES] = q_aug
        kcat_ref[p, :, 0:LANES] = k_b[:, cols]
        kcat_ref[p, :, LANES:2 * LANES] = k_aug


def _aug_constants():
    selq = np.zeros((3 * LANES, LANES), np.float32)
    selk = np.zeros((3 * LANES, LANES), np.float32)
    oneq = np.zeros((1, LANES), np.float32)
    onek = np.zeros((1, LANES), np.float32)
    for h in range(N_HEADS):
        for k in range(3):
            selq[k * LANES + h, AUG_STRIDE * h + k] = 1.0
            selk[k * LANES + h, AUG_STRIDE * h + 3 + k] = -1.0
            oneq[0, AUG_STRIDE * h + 3 + k] = 1.0
            onek[0, AUG_STRIDE * h + k] = 1.0
    return (jnp.asarray(selq, BF16), jnp.asarray(selk, BF16), jnp.asarray(oneq), jnp.asarray(onek))


def _inproj(x, n_seq, w, seq_minor):
    t_total = x.shape[0]
    tm = 512
    n_tiles = t_total // tm
    seq = t_total // n_seq
    tps = seq // tm
    row = lambda i: (i, 0)
    flat = lambda dt: (jax.ShapeDtypeStruct((t_total, WIDTH), dt), pl.BlockSpec((tm, WIDTH), row))
    if seq_minor:
        assert tm == WINDOW_ROWS
        band_f32 = (jax.ShapeDtypeStruct((n_seq, N_HEADS, HEAD_DIM, WINDOW_ROWS), F32),
                    pl.BlockSpec((None, N_HEADS, HEAD_DIM, tm), lambda i: (i // tps, 0, 0, 0)))
        fox_f32 = (jax.ShapeDtypeStruct((n_seq, N_HEADS, HEAD_DIM, seq), F32),
                   pl.BlockSpec((None, N_HEADS, HEAD_DIM, tm), lambda i: (i // tps, 0, 0, i % tps)))
        logf = (jax.ShapeDtypeStruct((n_seq, N_HEADS, seq), F32),
                pl.BlockSpec((None, N_HEADS, tm), lambda i: (i // tps, 0, i % tps)))
    else:
        band_f32 = fox_f32 = flat(F32)
        logf = (jax.ShapeDtypeStruct((t_total, N_HEADS), F32), pl.BlockSpec((tm, N_HEADS), row))
    cat = (jax.ShapeDtypeStruct((n_seq, N_PAIRS, seq, 2 * LANES), BF16),
           pl.BlockSpec((None, N_PAIRS, tm, 2 * LANES), lambda i: (i // tps, 0, i % tps, 0)))
    outs = [flat(BF16), flat(BF16), flat(BF16), band_f32, band_f32, cat, cat, flat(BF16),
            fox_f32, fox_f32, logf]
    bd = jnp.asarray(np.kron(np.eye(N_HEADS), np.ones((HEAD_DIM, HEAD_DIM))), BF16)
    ltri = jnp.asarray(np.tril(np.ones((tm, tm))), BF16)
    consts = [w["g_mix"], w["w_qkv"], w["w_f"], w["b_f"], w["gains"], bd, ltri, *_aug_constants()]
    in_specs = [pl.BlockSpec((tm, D_MODEL), row)] + [_const_spec(c.shape) for c in consts]
    return pl.pallas_call(
        functools.partial(_inproj_kernel, tiles_per_seq=tps, seq_minor=seq_minor),
        grid=(n_tiles,),
        in_specs=in_specs,
        out_specs=[o[1] for o in outs],
        out_shape=[o[0] for o in outs],
        scratch_shapes=[pltpu.VMEM((8, LANES), F32)],
        compiler_params=_params(1),
        name="inproj",
    )(x, *consts)


def _softmax_pv(scores, values):
    chunks = [[s[:, c * LANES:(c + 1) * LANES] for c in range(s.shape[1] // LANES)] for s in scores]
    cmax = None
    for cs in chunks:
        for c in cs:
            cmax = c if cmax is None else jnp.maximum(cmax, c)
    m = jnp.max(cmax, axis=-1, keepdims=True)
    pv = None
    lsum = None
    for cs, v in zip(chunks, values):
        ps = [jnp.exp2(c - m) for c in cs]
        for p in ps:
            lsum = p if lsum is None else lsum + p
        o = _dot(jnp.concatenate(ps, axis=-1).astype(BF16), v)
        pv = o if pv is None else pv + o
    return pv, jnp.sum(lsum, axis=-1, keepdims=True)


def _head_mask(hh):
    lane = lax.broadcasted_iota(jnp.int32, (1, LANES), 1)
    return (lane // HEAD_DIM) == hh


def _band_prompt_kernel(q_ref, k0_ref, k1_ref, k2_ref, k3_ref, v0_ref, v1_ref, v2_ref, v3_ref, e_ref, mask_ref,
                        o_ref, bias_ref, *, tq):
    i = pl.program_id(1)

    @pl.when((pl.program_id(0) == 0) & (i == 0))
    def _():
        _build_bias(e_ref, mask_ref, bias_ref)

    k_refs = (k0_ref, k1_ref, k2_ref, k3_ref)
    v_refs = (v0_ref, v1_ref, v2_ref, v3_ref)
    for qb in range(2):
        pad = [jnp.where(2 * i + qb - 2 + j >= 0, 0.0, NEG) for j in range(2)] + [0.0]
        for p in range(N_PAIRS):
            cols = slice(p * LANES, (p + 1) * LANES)
            q = q_ref[qb * tq:(qb + 1) * tq, cols]
            ks = [r[:, cols] for r in k_refs[qb:qb + 3]]
            vs = [r[:, cols] for r in v_refs[qb:qb + 3]]
            out = jnp.zeros((tq, LANES), F32)
            for hh in range(2):
                msk = _head_mask(hh)
                qh = jnp.where(msk, q, jnp.zeros_like(q))
                scores = [_dot_nt(qh, ks[j]) + bias_ref[2 * p + hh, :, j * tq:(j + 1) * tq] + pad[j]
                          for j in range(3)]
                pv, l = _softmax_pv(scores, vs)
                out = jnp.where(msk, pv / l, out)
            o_ref[qb * tq:(qb + 1) * tq, cols] = out.astype(BF16)


def _band_prompt(qa, ka, va, bias_e, bias_mask, n_seq):
    t_total = qa.shape[0]
    tq = 256
    seq = t_total // n_seq
    nq = seq // tq
    qmap = lambda b, i: (b * (nq // 2) + i, 0)

    def kmap(j):
        return lambda b, i: (b * nq + jnp.maximum(2 * i - 2 + j, 0), 0)

    blk = lambda m: pl.BlockSpec((tq, WIDTH), m)
    qblk = pl.BlockSpec((2 * tq, WIDTH), qmap)
    return pl.pallas_call(
        functools.partial(_band_prompt_kernel, tq=tq),
        grid=(n_seq, nq // 2),
        in_specs=[qblk] + [blk(kmap(j)) for j in range(4)] + [blk(kmap(j)) for j in range(4)]
        + [_const_spec(bias_e.shape), _const_spec(bias_mask.shape)],
        out_specs=qblk,
        out_shape=jax.ShapeDtypeStruct((t_total, WIDTH), BF16),
        scratch_shapes=[pltpu.VMEM((N_HEADS,) + bias_mask.shape, F32)],
        compiler_params=_params(2),
        name="band_prompt",
    )(qa, ka, ka, ka, ka, va, va, va, va, bias_e, bias_mask)


def _band_sample_kernel(q_ref, kc_ref, vc_ref, kn_ref, vn_ref, e_ref, mask_ref, o_ref, ko_ref, vo_ref,
                        bias_ref):
    p_rows = kc_ref.shape[2]
    t = q_ref.shape[0]

    @pl.when(pl.program_id(0) == 0)
    def _():
        _build_bias(e_ref, mask_ref, bias_ref)

    pad = jnp.zeros((LANES - t, HEAD_DIM), BF16)
    for h in range(N_HEADS):
        cols = slice(h * HEAD_DIM, (h + 1) * HEAD_DIM)
        q = q_ref[:, cols]
        k_new, v_new = kn_ref[:, cols], vn_ref[:, cols]
        k_cache, v_cache = kc_ref[h], vc_ref[h]
        s_cache = _dot(q, k_cache.astype(BF16)) + bias_ref[h, :, 0:p_rows]
        s_new = (_dot_nt(q, jnp.concatenate([k_new.astype(BF16), pad], axis=0))
                 + bias_ref[h, :, p_rows:p_rows + LANES])
        chunks = [s_cache[:, c * LANES:(c + 1) * LANES] for c in range(p_rows // LANES)] + [s_new]
        cmax = chunks[0]
        for c in chunks[1:]:
            cmax = jnp.maximum(cmax, c)
        m = jnp.max(cmax, axis=-1, keepdims=True)
        ps = [jnp.exp2(c - m) for c in chunks]
        lsum = ps[0]
        for p in ps[1:]:
            lsum = lsum + p
        pv = (_dot_nt(jnp.concatenate(ps[:-1], axis=-1).astype(BF16), v_cache.astype(BF16))
              + _dot(ps[-1].astype(BF16), jnp.concatenate([v_new.astype(BF16), pad], axis=0)))
        o_ref[:, cols] = (pv / jnp.sum(lsum, axis=-1, keepdims=True)).astype(BF16)
        ko_ref[h] = jnp.concatenate([k_cache[:, t:], k_new.T], axis=1)
        vo_ref[h] = jnp.concatenate([v_cache[:, t:], v_new.T], axis=1)


def _band_sample(qa, ka_f, va_f, cache_kt, cache_vt, bias_e, bias_mask):
    n = cache_kt.shape[0]
    t = qa.shape[0] // n
    row = pl.BlockSpec((t, WIDTH), lambda b: (b, 0))
    cache = pl.BlockSpec((None,) + cache_kt.shape[1:], lambda b: (b, 0, 0, 0))
    return pl.pallas_call(
        _band_sample_kernel,
        grid=(n,),
        in_specs=[row, cache, cache, row, row, _const_spec(bias_e.shape), _const_spec(bias_mask.shape)],
        out_specs=[row, cache, cache],
        out_shape=[jax.ShapeDtypeStruct(qa.shape, BF16),
                   jax.ShapeDtypeStruct(cache_kt.shape, F32),
                   jax.ShapeDtypeStruct(cache_vt.shape, F32)],
        scratch_shapes=[pltpu.VMEM((N_HEADS,) + bias_mask.shape, F32)],
        compiler_params=_params(1),
        name="band_sample",
    )(qa, cache_kt, cache_vt, ka_f, va_f, bias_e, bias_mask)


def _band_bias(rel_bias, n_q, n_k, offset, band_mask, valid_k):
    assert n_q + n_k - 1 <= BIAS_PERIOD
    m = np.arange(BIAS_PERIOD)
    w = np.where(m < n_k, m, m - BIAS_PERIOD)
    idx = np.clip(offset - w, -MAX_REL, MAX_REL) + MAX_REL
    e = jnp.take(rel_bias.astype(F32), jnp.asarray(idx), axis=1) * LOG2E
    qi = np.arange(n_q)[:, None]
    kj = np.arange(n_k)[None, :]
    valid = np.broadcast_to(kj < valid_k, (n_q, n_k))
    if band_mask:
        rel_chunk = (qi + offset) // CHUNK - kj // CHUNK
        valid = valid & (rel_chunk >= 0) & (rel_chunk <= LEFT_CHUNKS)
    return e, jnp.asarray(np.where(valid, 0.0, NEG), F32)


def _build_bias(e_ref, mask_ref, bias_sc):
    rows, cols = mask_ref.shape
    for h in range(N_HEADS):
        spread = jnp.broadcast_to(e_ref[h:h + 1, :], (rows, BIAS_PERIOD))
        bias_sc[h] = pltpu.roll(spread, 0, 1, stride=1, stride_axis=0)[:, :cols] + mask_ref[...]


def _aug_head_mask(h_in_pair, pair):
    lane = lax.broadcasted_iota(jnp.int32, (1, 2 * LANES), 1)
    head = 2 * pair + h_in_pair
    in_q = (lane < LANES) & ((lane // HEAD_DIM) == h_in_pair)
    in_aug = (lane >= LANES) & (((lane - LANES) // AUG_STRIDE) == head)
    return in_q | in_aug


def _flash_update(s, v, m_ref, l_ref, acc_ref, h, row_bias=None, v_seq_minor=False):
    chunks = [s[:, c * LANES:(c + 1) * LANES] for c in range(s.shape[1] // LANES)]
    if row_bias is not None:
        chunks = [c + row_bias for c in chunks]
    cmax = chunks[0]
    for c in chunks[1:]:
        cmax = jnp.maximum(cmax, c)
    m_prev = m_ref[h]
    m_new = jnp.maximum(m_prev, jnp.max(cmax, axis=-1, keepdims=True))
    alpha = jnp.exp2(m_prev - m_new)
    ps = [jnp.exp2(c - m_new) for c in chunks]
    lsum = ps[0]
    for p in ps[1:]:
        lsum = lsum + p
    l_ref[h] = alpha * l_ref[h] + lsum
    p = jnp.concatenate(ps, axis=-1).astype(BF16)
    pv = _dot_nt(p, v) if v_seq_minor else _dot(p, v)
    acc_ref[h] = alpha[:, :pv.shape[1]] * acc_ref[h] + pv
    m_ref[h] = m_new


def _flash_result(l_ref, acc_ref, pair):
    outs = [acc_ref[2 * pair + hh] / jnp.sum(l_ref[2 * pair + hh], axis=-1, keepdims=True)
            for hh in range(2)]
    return jnp.where(_head_mask(0), outs[0], outs[1])


def _fox_prompt_kernel(qi_ref, kj_ref, q_ref, k_ref, v_ref, o_ref, qh_sc, m_sc, l_sc, acc_sc, *, tq, tk):
    t = pl.program_id(1)
    i = qi_ref[t]
    j = kj_ref[t]

    @pl.when(j == 0)
    def _():
        for p in range(N_PAIRS):
            q = q_ref[p]
            for hh in range(2):
                qh_sc[2 * p + hh] = jnp.where(_aug_head_mask(hh, p), q, jnp.zeros_like(q))
        m_sc[...] = jnp.full_like(m_sc, NEG)
        l_sc[...] = jnp.zeros_like(l_sc)
        acc_sc[...] = jnp.zeros_like(acc_sc)

    def step(diagonal):
        if diagonal:
            keep = (lax.broadcasted_iota(jnp.int32, (tq, tk), 1)
                    <= lax.broadcasted_iota(jnp.int32, (tq, tk), 0))
        for p in range(N_PAIRS):
            k = k_ref[p]
            v = v_ref[:, p * LANES:(p + 1) * LANES]
            for hh in range(2):
                s = _dot_nt(qh_sc[2 * p + hh], k)
                if diagonal:
                    s = jnp.where(keep, s, NEG)
                _flash_update(s, v, m_sc, l_sc, acc_sc, 2 * p + hh)

    @pl.when(j < i)
    def _():
        step(False)

    @pl.when(j == i)
    def _():
        step(True)
        for p in range(N_PAIRS):
            o_ref[:, p * LANES:(p + 1) * LANES] = _flash_result(l_sc, acc_sc, p).astype(BF16)


def _fox_prompt(qcat, kcat, vb, n_seq):
    seq = qcat.shape[2]
    tq = tk = 1024
    nt = seq // tq
    qi = np.concatenate([np.full(i + 1, i) for i in range(nt)]).astype(np.int32)
    kj = np.concatenate([np.arange(i + 1) for i in range(nt)]).astype(np.int32)
    grid_spec = pltpu.PrefetchScalarGridSpec(
        num_scalar_prefetch=2,
        grid=(n_seq, len(qi)),
        in_specs=[
            pl.BlockSpec((None, N_PAIRS, tq, 2 * LANES), lambda b, t, qi, kj: (b, 0, qi[t], 0)),
            pl.BlockSpec((None, N_PAIRS, tk, 2 * LANES), lambda b, t, qi, kj: (b, 0, kj[t], 0)),
            pl.BlockSpec((tk, WIDTH), lambda b, t, qi, kj: (b * nt + kj[t], 0)),
        ],
        out_specs=pl.BlockSpec((tq, WIDTH), lambda b, t, qi, kj: (b * nt + qi[t], 0)),
        scratch_shapes=[pltpu.VMEM((N_HEADS, tq, 2 * LANES), BF16), pltpu.VMEM((N_HEADS, tq, LANES), F32),
                        pltpu.VMEM((N_HEADS, tq, LANES), F32), pltpu.VMEM((N_HEADS, tq, LANES), F32)],
    )
    return pl.pallas_call(
        functools.partial(_fox_prompt_kernel, tq=tq, tk=tk),
        grid_spec=grid_spec,
        out_shape=jax.ShapeDtypeStruct(vb.shape, BF16),
        compiler_params=_params(2),
        name="fox_prompt",
    )(jnp.asarray(qi), jnp.asarray(kj), qcat, kcat, vb)


def _suffix_sum_exclusive(x):
    n = x.shape[1]
    lane = lax.broadcasted_iota(jnp.int32, x.shape, 1)
    y = jnp.where(lane + 1 < n, pltpu.roll(x, n - 1, axis=1), 0.0)
    shift = 1
    while shift < n:
        y = y + jnp.where(lane + shift < n, pltpu.roll(y, n - shift, axis=1), 0.0)
        shift *= 2
    return y


def _fox_sample_kernel(q_ref, kc_ref, vc_ref, kn_ref, vn_ref, lfc_ref, lfn_ref, u_ref,
                       o_ref, cq_sc, car_sc, m_sc, l_sc, acc_sc, *, n_cache_tiles, sub_keys):
    j = pl.program_id(1)
    t = q_ref.shape[1]

    def cum_new():
        hi, mid, lo = _split3(lfn_ref[...])
        u = u_ref[...]
        return _dot(hi, u) + _dot(mid, u) + _dot(lo, u)

    def q_head(h):
        return q_ref[h // 2, :, (h % 2) * HEAD_DIM:(h % 2 + 1) * HEAD_DIM]

    @pl.when(j == 0)
    def _():
        cn = cum_new() * LOG2E
        eye = (lax.broadcasted_iota(jnp.int32, (t, LANES), 0)
               == lax.broadcasted_iota(jnp.int32, (t, LANES), 1))
        for h in range(N_HEADS):
            col = jnp.sum(jnp.where(eye, jnp.broadcast_to(cn[h:h + 1, :], (t, LANES)), 0.0),
                          axis=-1, keepdims=True)
            cq_sc[h] = jnp.broadcast_to(col, (t, LANES))
        car_sc[...] = jnp.zeros_like(car_sc)
        m_sc[...] = jnp.full_like(m_sc, NEG)
        l_sc[...] = jnp.zeros_like(l_sc)
        acc_sc[...] = jnp.zeros_like(acc_sc)

    lf = lfc_ref[...]
    r = _suffix_sum_exclusive(lf) + car_sc[:, 0:1]
    car_sc[...] = jnp.broadcast_to(r[:, 0:1] + lf[:, 0:1], car_sc.shape)
    r = r * LOG2E
    for sub in range(lf.shape[1] // sub_keys):
        keys = slice(sub * sub_keys, (sub + 1) * sub_keys)
        for h in range(N_HEADS):
            s = _dot(q_head(h), kc_ref[h, :, keys].astype(BF16)) + r[h:h + 1, keys]
            _flash_update(s, vc_ref[h, :, keys].astype(BF16), m_sc, l_sc, acc_sc, h,
                          row_bias=cq_sc[h], v_seq_minor=True)

    @pl.when(j == n_cache_tiles - 1)
    def _():
        cn = cum_new() * LOG2E
        causal = (lax.broadcasted_iota(jnp.int32, (t, LANES), 1)
                  <= lax.broadcasted_iota(jnp.int32, (t, LANES), 0))
        pad = jnp.zeros((LANES - t, HEAD_DIM), BF16)
        for h in range(N_HEADS):
            cols = slice(h * HEAD_DIM, (h + 1) * HEAD_DIM)
            k = jnp.concatenate([kn_ref[h // 2, :, (h % 2) * HEAD_DIM:(h % 2 + 1) * HEAD_DIM], pad], axis=0)
            v = jnp.concatenate([vn_ref[:, cols], pad], axis=0)
            s = _dot_nt(q_head(h), k) + cq_sc[h] - cn[h:h + 1, :]
            _flash_update(jnp.where(causal, s, NEG), v, m_sc, l_sc, acc_sc, h)
            o_ref[:, cols] = (acc_sc[h] / jnp.sum(l_sc[h], axis=-1, keepdims=True)).astype(BF16)


def _fox_sample(qcat, kcat, vb, cache_kt, cache_vt, lf_cache_t, lf_new_t):
    n, _, _, p_rows = cache_kt.shape
    t = vb.shape[0] // n
    tk = 2048
    nct = p_rows // tk
    u = jnp.asarray(np.pad(np.triu(np.ones((t, t))), ((0, 0), (0, LANES - t))), BF16)
    rev = lambda j: nct - 1 - j
    cache = pl.BlockSpec((None, N_HEADS, HEAD_DIM, tk), lambda b, j: (b, 0, 0, rev(j)))
    new_cat = pl.BlockSpec((None, N_PAIRS, t, 2 * LANES), lambda b, j: (0, 0, b, 0))
    return pl.pallas_call(
        functools.partial(_fox_sample_kernel, n_cache_tiles=nct, sub_keys=tk),
        grid=(n, nct),
        in_specs=[
            new_cat, cache, cache, new_cat,
            pl.BlockSpec((t, WIDTH), lambda b, j: (b, 0)),
            pl.BlockSpec((None, N_HEADS, tk), lambda b, j: (b, 0, rev(j))),
            pl.BlockSpec((None, N_HEADS, t), lambda b, j: (b, 0, 0)),
            _const_spec(u.shape),
        ],
        out_specs=pl.BlockSpec((t, WIDTH), lambda b, j: (b, 0)),
        out_shape=jax.ShapeDtypeStruct(vb.shape, BF16),
        scratch_shapes=[pltpu.VMEM((N_HEADS, t, LANES), F32), pltpu.VMEM((N_HEADS, LANES), F32),
                        pltpu.VMEM((N_HEADS, t, LANES), F32), pltpu.VMEM((N_HEADS, t, LANES), F32),
                        pltpu.VMEM((N_HEADS, t, HEAD_DIM), F32)],
        compiler_params=_params(2),
        name="fox_sample",
    )(qcat, cache_kt, cache_vt, kcat, vb, lf_cache_t, lf_new_t, u)


GROUP_LANE = 64
ROW_ALIGN = 16
MOE_TILE = 512
SORT_ROWS = 640
XS_WIDTH = D_MODEL + 2 * LANES


def _route(r):
    lane_i = lax.broadcasted_iota(jnp.int32, r.shape, 1)
    lane = lane_i.astype(F32)
    lane_grp = (lane_i // EXPERTS_PER_GROUP).astype(F32)
    big = float(LANES)
    is_coarse = (lane_i >= N_EXPERTS) & (lane_i < N_EXPERTS + N_GROUPS)
    cm = jnp.where(is_coarse, r, NEG)
    cmax = cm.max(axis=-1, keepdims=True)
    grp = jnp.min(jnp.where(cm == cmax, lane - N_EXPERTS, big), axis=-1, keepdims=True)
    pg_sel = 1.0 / jnp.sum(jnp.exp(cm - cmax), axis=-1, keepdims=True)
    in_grp = (lane_i < N_EXPERTS) & (lane_grp == grp)
    fm = jnp.where(in_grp, r, NEG)
    m1 = fm.max(axis=-1, keepdims=True)
    denom = jnp.sum(jnp.exp(fm - m1), axis=-1, keepdims=True)
    i1 = jnp.min(jnp.where(fm == m1, lane, big), axis=-1, keepdims=True)
    fm2 = jnp.where(lane == i1, NEG, fm)
    m2 = fm2.max(axis=-1, keepdims=True)
    i2 = jnp.min(jnp.where(fm2 == m2, lane, big), axis=-1, keepdims=True)
    p1 = 1.0 / denom
    p2 = jnp.exp(m2 - m1) / denom
    tot = p1 + p2
    comb = (jnp.where(lane == i1, pg_sel * (p1 / tot), 0.0)
            + jnp.where(lane == i2, pg_sel * (p2 / tot), 0.0))
    return comb, grp


def _merge_kernel(x_ref, oa_ref, ob_ref, g_ref, wg_ref, wpa_ref, wpb_ref, wo_ref, gf_ref, wrh_ref, wrm_ref,
                  br_ref, y_ref, hx_ref, route_ref, grp_t_ref, cnt_ref):
    x = x_ref[...]
    h = _rms(x, g_ref[...]).astype(BF16)
    gate = jax.nn.sigmoid(_dot(h, wg_ref[...]))
    mix = (gate[:, :D_MODEL] * _dot(oa_ref[...], wpa_ref[...])
           + gate[:, D_MODEL:] * _dot(ob_ref[...], wpb_ref[...]))
    y = x + _dot(mix.astype(BF16), wo_ref[...])
    y_ref[...] = y

    hx = _rms(y, gf_ref[...])
    hx_ref[:, 0:D_MODEL] = hx.astype(BF16)
    h_hi, h_mid, _ = _split3(hx)
    r = _dot(h_hi, wrh_ref[...]) + _dot(h_hi, wrm_ref[...]) + _dot(h_mid, wrh_ref[...]) + br_ref[...]
    comb, grp = _route(r)
    lane = lax.broadcasted_iota(jnp.int32, comb.shape, 1)
    route = jnp.where(lane == GROUP_LANE, grp, comb)
    route_ref[...] = route
    r_hi = route.astype(BF16)
    hx_ref[:, D_MODEL:D_MODEL + LANES] = r_hi
    hx_ref[:, D_MODEL + LANES:] = (route - r_hi.astype(F32)).astype(BF16)
    grp_t_ref[...] = route.T[GROUP_LANE:GROUP_LANE + 8, :]
    cnt = jnp.sum(jnp.where(lane.astype(F32) == grp, 1.0, 0.0), axis=0, keepdims=True)
    cnt_ref[...] = jnp.broadcast_to(cnt, cnt_ref.shape)


def _merge(x, oa, ob, w):
    t_total = x.shape[0]
    tm = MOE_TILE
    n_tiles = t_total // tm
    row = lambda n: pl.BlockSpec((tm, n), lambda i: (i, 0))
    consts = [w["g_mix"], w["w_gate"], w["w_pa"], w["w_pb"], w["w_o"], w["g_ffn"], w["w_router_hi"],
              w["w_router_mid"], w["b_router"]]
    return pl.pallas_call(
        _merge_kernel,
        grid=(n_tiles,),
        in_specs=[row(D_MODEL), row(WIDTH), row(WIDTH)] + [_const_spec(c.shape) for c in consts],
        out_specs=[row(D_MODEL), row(XS_WIDTH), row(LANES), pl.BlockSpec((8, tm), lambda i: (0, i)),
                   pl.BlockSpec((None, 8, LANES), lambda i: (i, 0, 0))],
        out_shape=[jax.ShapeDtypeStruct(x.shape, F32), jax.ShapeDtypeStruct((t_total, XS_WIDTH), BF16),
                   jax.ShapeDtypeStruct((t_total, LANES), F32), jax.ShapeDtypeStruct((8, t_total), F32),
                   jax.ShapeDtypeStruct((n_tiles, 8, LANES), F32)],
        compiler_params=_params(1),
        name="merge",
    )(x, oa, ob, *consts)


def _plan_kernel(cnt_ref, ib_ref, dst_ref, len_ref, fill_ref, inblk_ref, outblk_ref, grp_ref, kind_ref, *,
                 n_tiles, n_steps):
    align_bits = ROW_ALIGN.bit_length() - 1
    tile_bits = MOE_TILE.bit_length() - 1
    zero = jnp.int32(0)

    def per_tile(t, tot):
        run = zero
        new_tot = []
        for g in range(N_GROUPS):
            n = ((cnt_ref[t * N_GROUPS + g] + (ROW_ALIGN - 1)) >> align_bits) << align_bits
            ib_ref[t * N_GROUPS + g] = run
            len_ref[t * N_GROUPS + g] = n
            dst_ref[t * N_GROUPS + g] = tot[g]
            run = run + n
            new_tot.append(tot[g] + n)
        return tuple(new_tot)

    tot = lax.fori_loop(0, n_tiles, per_tile, (zero,) * N_GROUPS)
    n_ffn = [(tot[g] + (MOE_TILE - 1)) >> tile_bits for g in range(N_GROUPS)]
    g_base, ends, acc_rows, acc_tiles = [], [], zero, zero
    for g in range(N_GROUPS):
        g_base.append(acc_rows)
        acc_rows = acc_rows + ((n_ffn[g] + 1) << tile_bits)
        acc_tiles = acc_tiles + n_ffn[g] + 1
        ends.append(acc_tiles)

    def add_base(t, carry):
        for g in range(N_GROUPS):
            dst_ref[t * N_GROUPS + g] = dst_ref[t * N_GROUPS + g] + g_base[g]
        return carry

    lax.fori_loop(0, n_tiles, add_base, zero)
    for g in range(N_GROUPS):
        fill_ref[g] = g_base[g] + tot[g]
        fill_ref[N_GROUPS + g] = g_base[g] + (n_ffn[g] << tile_bits)
    fill_ref[2 * N_GROUPS] = ends[-1]
    first_blk = g_base[N_GROUPS - 1]
    for g in reversed(range(N_GROUPS - 1)):
        first_blk = jnp.where(n_ffn[g] > 0, g_base[g], first_blk)
    first_blk = first_blk >> tile_bits

    def pick(vals, g):
        out = vals[N_GROUPS - 1]
        for i in reversed(range(N_GROUPS - 1)):
            out = jnp.where(g == i, vals[i], out)
        return out

    def per_step(k, carry):
        kk = jnp.minimum(k, ends[-1] - 1)
        g = zero
        for i in range(N_GROUPS - 1):
            g = g + jnp.where(kk >= ends[i], 1, 0)
        j = kk - pick([zero] + ends[:-1], g)
        is_expert = j < pick(n_ffn, g)
        out_blk = jnp.where(k < ends[-1], (pick(g_base, g) >> tile_bits) + j, k)
        outblk_ref[k] = out_blk
        inblk_ref[k] = jnp.where(is_expert, out_blk, first_blk)
        grp_ref[k] = g
        kind_ref[k] = jnp.where(is_expert, 1, 0)
        return carry

    lax.fori_loop(0, n_steps, per_step, zero)


def _moe_plan(cnt, t_total):
    n_tiles = t_total // MOE_TILE
    n_steps = (n_tiles + 2 * N_GROUPS + (N_GROUPS * (ROW_ALIGN - 1) * n_tiles + MOE_TILE - 1) // MOE_TILE)
    smem = pl.BlockSpec(memory_space=pltpu.SMEM)
    sizes = [n_tiles * N_GROUPS] * 3 + [2 * N_GROUPS + 1] + [n_steps] * 4
    outs = pl.pallas_call(
        functools.partial(_plan_kernel, n_tiles=n_tiles, n_steps=n_steps),
        grid_spec=pltpu.PrefetchScalarGridSpec(num_scalar_prefetch=1, grid=(1,), in_specs=[],
                                               out_specs=[smem] * len(sizes)),
        out_shape=[jax.ShapeDtypeStruct((n,), jnp.int32) for n in sizes],
        compiler_params=_params(1),
        name="moe_plan",
    )(cnt.reshape(-1))
    return (*outs, n_steps)


def _sort_kernel(ib_ref, dst_ref, len_ref, fill_ref, hx_ref, grp_t_ref, tri_ref, xs_ref, xsort, sems):
    i = pl.program_id(0)
    n = pl.num_programs(0)
    tm = hx_ref.shape[0]
    half = tm // 2
    slot = i % 2

    def copy(src, dst, rows, sl, g, part):
        return pltpu.make_async_copy(xsort.at[sl, pl.ds(pl.multiple_of(src, ROW_ALIGN), rows)],
                                     xs_ref.at[pl.ds(pl.multiple_of(dst, ROW_ALIGN), rows)], sems.at[sl, g, part])

    def for_segments(step, sl, act):
        for g in range(N_GROUPS):
            src, dst = ib_ref[step * N_GROUPS + g], dst_ref[step * N_GROUPS + g]
            act(copy(src, dst, half, sl, g, 0))

            @pl.when(len_ref[step * N_GROUPS + g] > half)
            def _():
                act(copy(src + half, dst + half, half, sl, g, 1))

    def run(cs):
        for c in cs:
            c.start()
        for c in cs:
            c.wait()

    @pl.when(i == 0)
    def _():
        for sl in range(2):
            xsort[sl, SORT_ROWS:, :] = jnp.zeros((xsort.shape[1] - SORT_ROWS, XS_WIDTH), BF16)

    g_row = grp_t_ref[0:1, :]
    sub = lax.broadcasted_iota(jnp.int32, (8, tm), 0).astype(F32)
    mine = sub == g_row
    before = _dot(jnp.where(mine, 1.0, 0.0).astype(BF16), tri_ref[...])
    dest = jnp.sum(jnp.where(mine, before, 0.0), axis=0, keepdims=True)
    for g in range(N_GROUPS):
        base = jnp.full(dest.shape, ib_ref[i * N_GROUPS + g], jnp.int32).astype(F32)
        dest = dest + jnp.where(g_row == g, base, 0.0)
    rows = lax.broadcasted_iota(jnp.int32, (SORT_ROWS, tm), 0).astype(F32)
    perm = jnp.where(rows == dest, 1.0, 0.0).astype(BF16)
    xsort[slot, 0:SORT_ROWS, :] = _dot(perm, hx_ref[...]).astype(BF16)

    @pl.when(i > 0)
    def _():
        for_segments(i - 1, 1 - slot, lambda c: c.wait())

    for_segments(i, slot, lambda c: c.start())

    @pl.when(i == n - 1)
    def _():
        for_segments(i, slot, lambda c: c.wait())
        run([copy(SORT_ROWS, fill_ref[g], tm, slot, g, 0) for g in range(N_GROUPS)])
        run([copy(SORT_ROWS, fill_ref[N_GROUPS + g], tm, slot, g, 0) for g in range(N_GROUPS)])

        @pl.loop(fill_ref[2 * N_GROUPS], xs_ref.shape[0] // tm)
        def _(b):
            run([copy(SORT_ROWS, b * tm, tm, slot, 0, 0)])


def _expert_kernel(ib_ref, ob_ref, grp_ref, kind_ref, xs_ref, w1_ref, w3_ref, w2_ref, ys_ref):
    del ib_ref, ob_ref
    k = pl.program_id(0)

    @pl.when(kind_ref[k] == 1)
    def _():
        x = xs_ref[:, 0:D_MODEL]
        side = lambda w_ref: jnp.concatenate([w_ref[e].astype(BF16) for e in range(EXPERTS_PER_GROUP)], axis=1)
        hid = jax.nn.silu(_dot(x, side(w1_ref))) * _dot(x, side(w3_ref))
        comb = (xs_ref[:, D_MODEL:D_MODEL + LANES].astype(F32) + xs_ref[:, D_MODEL + LANES:].astype(F32))
        lane = lax.broadcasted_iota(jnp.int32, comb.shape, 1)
        first = grp_ref[k] * EXPERTS_PER_GROUP
        scale = []
        for e in range(EXPERTS_PER_GROUP):
            w_e = jnp.sum(jnp.where(lane == first + e, comb, 0.0), axis=-1, keepdims=True)
            scale.append(jnp.broadcast_to(w_e, (comb.shape[0], D_EXPERT)))
        hid = hid * jnp.concatenate(scale, axis=-1)
        w2 = w2_ref[...].astype(BF16).reshape(EXPERTS_PER_GROUP * D_EXPERT, D_MODEL)
        ys_ref[...] = _dot(hid.astype(BF16), w2).astype(BF16)

    @pl.when(kind_ref[k] == 0)
    def _():
        ys_ref[...] = jnp.zeros_like(ys_ref)


def _unsort_kernel(ib_ref, dst_ref, len_ref, y_ref, route_ref, tri_ref, ys_ref, o_ref, ybuf, yasm, sems):
    i = pl.program_id(0)
    n = pl.num_programs(0)
    tm = y_ref.shape[0]
    half = tm // 2
    slot = i % 2

    def fetch(step, sl, act):
        for g in range(N_GROUPS):
            src = pl.multiple_of(dst_ref[step * N_GROUPS + g], ROW_ALIGN)
            act(pltpu.make_async_copy(ys_ref.at[pl.ds(src, half)], ybuf.at[sl, g, 0:half], sems.at[sl, g, 0]))

            @pl.when(len_ref[step * N_GROUPS + g] > half)
            def _():
                act(pltpu.make_async_copy(ys_ref.at[pl.ds(src + half, half)], ybuf.at[sl, g, half:tm],
                                          sems.at[sl, g, 1]))

    @pl.when(i == 0)
    def _():
        yasm[...] = jnp.zeros_like(yasm)
        fetch(0, 0, lambda c: c.start())

    @pl.when(i + 1 < n)
    def _():
        fetch(i + 1, 1 - slot, lambda c: c.start())

    route = route_ref[...]
    lane = lax.broadcasted_iota(jnp.int32, route.shape, 1).astype(F32)
    grp = jnp.sum(jnp.where(lane == GROUP_LANE, route, 0.0), axis=-1, keepdims=True)
    mine = lane == grp
    before = _dot(tri_ref[...], jnp.where(mine, 1.0, 0.0).astype(BF16))
    dest = jnp.sum(jnp.where(mine, before, 0.0), axis=-1, keepdims=True)
    for g in range(N_GROUPS):
        base = jnp.full(dest.shape, ib_ref[i * N_GROUPS + g], jnp.int32).astype(F32)
        dest = dest + jnp.where(grp == g, base, 0.0)
    cols = lax.broadcasted_iota(jnp.int32, (tm, SORT_ROWS), 1).astype(F32)
    perm_t = jnp.where(cols == dest, 1.0, 0.0).astype(BF16)

    fetch(i, slot, lambda c: c.wait())
    for g in range(N_GROUPS):
        row0 = pl.multiple_of(ib_ref[i * N_GROUPS + g], ROW_ALIGN)
        yasm[pl.ds(row0, half), :] = ybuf[slot, g, 0:half]

        @pl.when(len_ref[i * N_GROUPS + g] > half)
        def _():
            yasm[pl.ds(row0 + half, half), :] = ybuf[slot, g, half:tm]
    o_ref[...] = y_ref[...] + _dot(perm_t, yasm[0:SORT_ROWS, :])


def _moe(y, hx, route, grp_t, cnt, w):
    t_total = y.shape[0]
    tm = MOE_TILE
    n_tiles = t_total // tm
    assert SORT_ROWS >= tm + N_GROUPS * (ROW_ALIGN - 1)
    in_base, dst, seg_len, fill, in_blk, out_blk, grp_of_step, is_expert, n_steps = _moe_plan(
        cnt[:, 0, :N_GROUPS].astype(jnp.int32), t_total)
    cap_rows = n_steps * tm
    lower = jnp.asarray(np.tril(np.ones((tm, tm)), -1), BF16)
    upper = jnp.asarray(np.triu(np.ones((tm, tm)), 1), BF16)
    stage_rows = SORT_ROWS + tm
    any_spec = pl.BlockSpec(memory_space=pl.ANY)

    xs = pl.pallas_call(
        _sort_kernel,
        grid_spec=pltpu.PrefetchScalarGridSpec(
            num_scalar_prefetch=4,
            grid=(n_tiles,),
            in_specs=[pl.BlockSpec((tm, XS_WIDTH), lambda i, *_: (i, 0)),
                      pl.BlockSpec((8, tm), lambda i, *_: (0, i)),
                      pl.BlockSpec((tm, tm), lambda i, *_: (0, 0))],
            out_specs=any_spec,
            scratch_shapes=[pltpu.VMEM((2, stage_rows, XS_WIDTH), BF16),
                            pltpu.SemaphoreType.DMA((2, N_GROUPS, 2))],
        ),
        out_shape=jax.ShapeDtypeStruct((cap_rows, XS_WIDTH), BF16),
        compiler_params=_params(1),
        name="moe_sort",
    )(in_base, dst, seg_len, fill, hx, grp_t, upper)

    step_map = lambda which: (lambda k, ib, ob, grp, kind: ((ib, ob, grp)[which][k], 0))
    wmap = lambda k, ib, ob, grp, kind: (grp[k], 0, 0)
    wspec = pl.BlockSpec((EXPERTS_PER_GROUP, D_MODEL, D_EXPERT), wmap)
    ys = pl.pallas_call(
        _expert_kernel,
        grid_spec=pltpu.PrefetchScalarGridSpec(
            num_scalar_prefetch=4,
            grid=(n_steps,),
            in_specs=[pl.BlockSpec((tm, XS_WIDTH), step_map(0)), wspec, wspec, pl.BlockSpec((EXPERTS_PER_GROUP, D_EXPERT, D_MODEL), wmap)],
            out_specs=pl.BlockSpec((tm, D_MODEL), step_map(1)),
        ),
        out_shape=jax.ShapeDtypeStruct((cap_rows, D_MODEL), BF16),
        compiler_params=_params(1),
        name="moe_experts",
    )(in_blk, out_blk, grp_of_step, is_expert, xs, w["w1"], w["w3"], w["w2"])

    return pl.pallas_call(
        _unsort_kernel,
        grid_spec=pltpu.PrefetchScalarGridSpec(
            num_scalar_prefetch=3,
            grid=(n_tiles,),
            in_specs=[pl.BlockSpec((tm, D_MODEL), lambda i, *_: (i, 0)),
                      pl.BlockSpec((tm, LANES), lambda i, *_: (i, 0)),
                      pl.BlockSpec((tm, tm), lambda i, *_: (0, 0)),
                      any_spec],
            out_specs=pl.BlockSpec((tm, D_MODEL), lambda i, *_: (i, 0)),
            scratch_shapes=[pltpu.VMEM((2, N_GROUPS, tm, D_MODEL), BF16),
                            pltpu.VMEM((stage_rows, D_MODEL), BF16),
                            pltpu.SemaphoreType.DMA((2, N_GROUPS, 2))],
        ),
        out_shape=jax.ShapeDtypeStruct(y.shape, F32),
        compiler_params=_params(1),
        name="moe_unsort",
    )(in_base, dst, seg_len, y, route, lower, ys)


def _prep_weights(g_mix, w_in, b_f, q_norm_a, k_norm_a, q_norm_b, k_norm_b, w_pa, w_pb, w_o,
                  g_ffn, w_rg, b_rg, w_re, b_re, w1, w3, w2):
    n_qkv = 6 * WIDTH
    tile = lambda g: jnp.tile(g, N_HEADS)
    w_router = jnp.concatenate(
        [jnp.transpose(w_re, (1, 0, 2)).reshape(D_MODEL, N_EXPERTS), w_rg,
         jnp.zeros((D_MODEL, LANES - N_EXPERTS - N_GROUPS), F32)], axis=1)
    b_router = jnp.concatenate(
        [b_re.reshape(N_EXPERTS), b_rg, jnp.zeros((LANES - N_EXPERTS - N_GROUPS,), F32)])[None, :]
    return {
        "g_mix": g_mix[None, :],
        "w_qkv": w_in[:, :n_qkv].astype(BF16),
        "w_f": jnp.pad(w_in[:, n_qkv:n_qkv + N_HEADS], ((0, 0), (0, LANES - N_HEADS))).astype(BF16),
        "b_f": jnp.pad(b_f, (0, LANES - N_HEADS))[None, :],
        "w_gate": w_in[:, n_qkv + N_HEADS:].astype(BF16),
        "gains": jnp.stack([tile(q_norm_a), tile(k_norm_a), tile(q_norm_b), tile(k_norm_b)]),
        "w_pa": w_pa.astype(BF16), "w_pb": w_pb.astype(BF16), "w_o": w_o.astype(BF16),
        "g_ffn": g_ffn[None, :],
        "w_router_hi": w_router.astype(BF16),
        "w_router_mid": (w_router - w_router.astype(BF16).astype(F32)).astype(BF16),
        "b_router": b_router,
        "w1": w1, "w3": w3, "w2": w2,
    }


def kernel(x_prompt, x_sample, cache_a_k, cache_a_v, cache_b_k, cache_b_v, cache_b_logf, g_mix, w_in, b_f, q_norm_a, k_norm_a, q_norm_b, k_norm_b, rel_bias, w_pa, w_pb, w_o, g_ffn, w_rg, b_rg, w_re, b_re, w1, w3, w2):
    assert g_mix.shape[0] == 1, "single-layer step"
    n_p, seq, _ = x_prompt.shape
    n_s, t_s, _ = x_sample.shape
    a_rows = cache_a_k.shape[2]
    w = _prep_weights(g_mix[0], w_in[0], b_f[0], q_norm_a[0], k_norm_a[0], q_norm_b[0], k_norm_b[0],
                      w_pa[0], w_pb[0], w_o[0], g_ffn[0], w_rg[0], b_rg[0], w_re[0], b_re[0],
                      w1[0], w3[0], w2[0])
    band_tq = 256
    bias_prompt = _band_bias(rel_bias[0], band_tq, 3 * band_tq, 2 * band_tq, True, 3 * band_tq)
    bias_sample = _band_bias(rel_bias[0], t_s, a_rows + LANES, a_rows, False, a_rows + t_s)

    seq_minor = lambda a: jnp.transpose(a, (0, 2, 3, 1))
    seq_major = lambda a: jnp.transpose(a, (0, 3, 1, 2))[None]

    xp = x_prompt.reshape(n_p * seq, D_MODEL)
    (qa, ka, va, ka_t, va_t, qcat, kcat, vb, kb_t, vb_t, logf_t) = _inproj(xp, n_p, w, seq_minor=True)
    o_a = _band_prompt(qa, ka, va, *bias_prompt, n_p)
    o_b = _fox_prompt(qcat, kcat, vb, n_p)
    y_p = _moe(*_merge(xp, o_a, o_b, w), w)

    xs = x_sample.reshape(n_s * t_s, D_MODEL)
    (qa_s, _, _, ka_fs, va_fs, qcat_s, kcat_s, vb_s, kb_fs, vb_fs, logf_s) = _inproj(
        xs, 1, w, seq_minor=False)
    o_as, new_ak_t, new_av_t = _band_sample(
        qa_s, ka_fs, va_fs, seq_minor(cache_a_k[0]), seq_minor(cache_a_v[0]), *bias_sample)
    lf_cache_t = jnp.transpose(cache_b_logf[0], (0, 2, 1))
    lf_new_t = jnp.transpose(logf_s.reshape(n_s, t_s, N_HEADS), (0, 2, 1))
    o_bs = _fox_sample(qcat_s, kcat_s, vb_s, seq_minor(cache_b_k[0]), seq_minor(cache_b_v[0]),
                       lf_cache_t, lf_new_t)
    y_s = _moe(*_merge(xs, o_as, o_bs, w), w)

    heads = lambda a, n, r: a.reshape(1, n, r, N_HEADS, HEAD_DIM)
    return (y_p.reshape(n_p, seq, D_MODEL), y_s.reshape(n_s, t_s, D_MODEL),
            seq_major(ka_t), seq_major(va_t), seq_major(kb_t), seq_major(vb_t),
            jnp.transpose(logf_t, (0, 2, 1))[None],
            seq_major(new_ak_t), seq_major(new_av_t),
            heads(kb_fs, n_s, t_s), heads(vb_fs, n_s, t_s), logf_s.reshape(1, n_s, t_s, N_HEADS))
```

```python
import functools

import numpy as np
import jax
import jax.numpy as jnp
from jax import lax
from jax.experimental import pallas as pl
from jax.experimental.pallas import tpu as pltpu

F32 = jnp.float32
BF16 = jnp.bfloat16

D_MODEL = 1024
HEAD_DIM = 64
N_HEADS = 8
WIDTH = N_HEADS * HEAD_DIM
N_PAIRS = N_HEADS // 2
CHUNK = 64
LEFT_CHUNKS = 8
WINDOW_ROWS = LEFT_CHUNKS * CHUNK
MAX_REL = 256
N_GROUPS = 4
EXPERTS_PER_GROUP = 8
N_EXPERTS = N_GROUPS * EXPERTS_PER_GROUP
D_EXPERT = 128
EPS = 1e-6
NEG = -1e30
LOG2E = 1.4426950408889634
LANES = 128
BIAS_PERIOD = 1024
AUG_STRIDE = 8
VMEM_LIMIT = 56 * 1024 * 1024

_NT = (((1,), (1,)), ((), ()))


def _dot(a, b):
    return jnp.dot(a, b, preferred_element_type=F32)


def _dot_nt(a, b):
    return lax.dot_general(a, b, _NT, preferred_element_type=F32)


def _split3(x):
    hi = x.astype(BF16)
    r = x - hi.astype(F32)
    mid = r.astype(BF16)
    lo = (r - mid.astype(F32)).astype(BF16)
    return hi, mid, lo


def _dot3(a_bf, x):
    hi, mid, lo = _split3(x)
    return _dot(a_bf, hi) + _dot(a_bf, mid) + _dot(a_bf, lo)


def _rms(x, g):
    ms = jnp.mean(x * x, axis=-1, keepdims=True)
    return x * lax.rsqrt(ms + EPS) * g


def _params(n_axes):
    return pltpu.CompilerParams(dimension_semantics=("arbitrary",) * n_axes,
                                vmem_limit_bytes=VMEM_LIMIT)


def _const_spec(shape):
    nd = len(shape)
    return pl.BlockSpec(shape, lambda *_: (0,) * nd)


def _inproj_kernel(x_ref, g_ref, wqkv_ref, wf_ref, bf_ref, gains_ref, bd_ref, ltri_ref,
                   selq_ref, selk_ref, oneq_ref, onek_ref,
                   qa_ref, ka_ref, va_ref, kaf_ref, vaf_ref, qcat_ref, kcat_ref, vb_ref,
                   kbf_ref, vbf_ref, logf_ref, carry_ref, *, tiles_per_seq, seq_minor):
    i = pl.program_id(0)
    tm = x_ref.shape[0]

    @pl.when(i % tiles_per_seq == 0)
    def _():
        carry_ref[...] = jnp.zeros_like(carry_ref)

    def put_f32(ref, y, last_tile_only=False):
        if not seq_minor:
            ref[...] = y
        elif last_tile_only:
            @pl.when(i % tiles_per_seq == tiles_per_seq - 1)
            def _():
                ref[...] = y.T.reshape(N_HEADS, HEAD_DIM, tm)
        else:
            ref[...] = y.T.reshape(N_HEADS, HEAD_DIM, tm)

    h = _rms(x_ref[...], g_ref[...]).astype(BF16)

    def seg(s):
        return _dot(h, wqkv_ref[:, s * WIDTH:(s + 1) * WIDTH])

    def headnorm(y, n):
        ss = _dot((y * y).astype(BF16), bd_ref[...])
        return y * lax.rsqrt(ss * (1.0 / HEAD_DIM) + EPS) * gains_ref[n:n + 1, :]

    q_a = headnorm(seg(0), 0)
    qa_ref[...] = (q_a * (HEAD_DIM ** -0.5 * LOG2E)).astype(BF16)
    k_a = headnorm(seg(1), 1)
    ka_ref[...] = k_a.astype(BF16)
    put_f32(kaf_ref, k_a, last_tile_only=True)
    v_a = seg(2)
    va_ref[...] = v_a.astype(BF16)
    put_f32(vaf_ref, v_a, last_tile_only=True)

    z = _dot(h, wf_ref[...]) + bf_ref[...]
    logf = jnp.minimum(z, 0.0) - jnp.log(1.0 + jnp.exp(-jnp.abs(z)))
    if seq_minor:
        logf_ref[...] = logf.T[0:N_HEADS, :]
    else:
        logf_ref[...] = logf[:, :N_HEADS]
    c = _dot3(ltri_ref[...], logf) + carry_ref[0:1, :]
    carry_ref[...] = jnp.broadcast_to(c[-1:, :], carry_ref.shape)
    cs = jnp.concatenate(_split3(c * LOG2E), axis=-1)
    q_aug = (_dot(cs, selq_ref[...]) + oneq_ref[...]).astype(BF16)
    k_aug = (_dot(cs, selk_ref[...]) + onek_ref[...]).astype(BF16)

    q_b = (headnorm(seg(3), 2) * (HEAD_DIM ** -0.5 * LOG2E)).astype(BF16)
    k_b = headnorm(seg(4), 3)
    put_f32(kbf_ref, k_b)
    k_b = k_b.astype(BF16)
    v_b = seg(5)
    put_f32(vbf_ref, v_b)
    vb_ref[...] = v_b.astype(BF16)
    for p in range(N_PAIRS):
        cols = slice(p * LANES, (p + 1) * LANES)
        qcat_ref[p, :, 0:LANES] = q_b[:, cols]
        qcat_ref[p, :, LANES:2 * LANES] = q_aug
        kcat_ref[p, :, 0:LANES] = k_b[:, cols]
        kcat_ref[p, :, LANES:2 * LANES] = k_aug


def _aug_constants():
    selq = np.zeros((3 * LANES, LANES), np.float32)
    selk = np.zeros((3 * LANES, LANES), np.float32)
    oneq = np.zeros((1, LANES), np.float32)
    onek = np.zeros((1, LANES), np.float32)
    for h in range(N_HEADS):
        for k in range(3):
            selq[k * LANES + h, AUG_STRIDE * h + k] = 1.0
            selk[k * LANES + h, AUG_STRIDE * h + 3 + k] = -1.0
            oneq[0, AUG_STRIDE * h + 3 + k] = 1.0
            onek[0, AUG_STRIDE * h + k] = 1.0
    return (jnp.asarray(selq, BF16), jnp.asarray(selk, BF16), jnp.asarray(oneq), jnp.asarray(onek))


def _inproj(x, n_seq, w, seq_minor):
    t_total = x.shape[0]
    tm = 512
    n_tiles = t_total // tm
    seq = t_total // n_seq
    tps = seq // tm
    row = lambda i: (i, 0)
    flat = lambda dt: (jax.ShapeDtypeStruct((t_total, WIDTH), dt), pl.BlockSpec((tm, WIDTH), row))
    if seq_minor:
        assert tm == WINDOW_ROWS
        band_f32 = (jax.ShapeDtypeStruct((n_seq, N_HEADS, HEAD_DIM, WINDOW_ROWS), F32),
                    pl.BlockSpec((None, N_HEADS, HEAD_DIM, tm), lambda i: (i // tps, 0, 0, 0)))
        fox_f32 = (jax.ShapeDtypeStruct((n_seq, N_HEADS, HEAD_DIM, seq), F32),
                   pl.BlockSpec((None, N_HEADS, HEAD_DIM, tm), lambda i: (i // tps, 0, 0, i % tps)))
        logf = (jax.ShapeDtypeStruct((n_seq, N_HEADS, seq), F32),
                pl.BlockSpec((None, N_HEADS, tm), lambda i: (i // tps, 0, i % tps)))
    else:
        band_f32 = fox_f32 = flat(F32)
        logf = (jax.ShapeDtypeStruct((t_total, N_HEADS), F32), pl.BlockSpec((tm, N_HEADS), row))
    cat = (jax.ShapeDtypeStruct((n_seq, N_PAIRS, seq, 2 * LANES), BF16),
           pl.BlockSpec((None, N_PAIRS, tm, 2 * LANES), lambda i: (i // tps, 0, i % tps, 0)))
    outs = [flat(BF16), flat(BF16), flat(BF16), band_f32, band_f32, cat, cat, flat(BF16),
            fox_f32, fox_f32, logf]
    bd = jnp.asarray(np.kron(np.eye(N_HEADS), np.ones((HEAD_DIM, HEAD_DIM))), BF16)
    ltri = jnp.asarray(np.tril(np.ones((tm, tm))), BF16)
    consts = [w["g_mix"], w["w_qkv"], w["w_f"], w["b_f"], w["gains"], bd, ltri, *_aug_constants()]
    in_specs = [pl.BlockSpec((tm, D_MODEL), row)] + [_const_spec(c.shape) for c in consts]
    return pl.pallas_call(
        functools.partial(_inproj_kernel, tiles_per_seq=tps, seq_minor=seq_minor),
        grid=(n_tiles,),
        in_specs=in_specs,
        out_specs=[o[1] for o in outs],
        out_shape=[o[0] for o in outs],
        scratch_shapes=[pltpu.VMEM((8, LANES), F32)],
        compiler_params=_params(1),
        name="inproj",
    )(x, *consts)


def _softmax_pv(scores, values):
    chunks = [[s[:, c * LANES:(c + 1) * LANES] for c in range(s.shape[1] // LANES)] for s in scores]
    cmax = None
    for cs in chunks:
        for c in cs:
            cmax = c if cmax is None else jnp.maximum(cmax, c)
    m = jnp.max(cmax, axis=-1, keepdims=True)
    pv = None
    lsum = None
    for cs, v in zip(chunks, values):
        ps = [jnp.exp2(c - m) for c in cs]
        for p in ps:
            lsum = p if lsum is None else lsum + p
        o = _dot(jnp.concatenate(ps, axis=-1).astype(BF16), v)
        pv = o if pv is None else pv + o
    return pv, jnp.sum(lsum, axis=-1, keepdims=True)


def _head_mask(hh):
    lane = lax.broadcasted_iota(jnp.int32, (1, LANES), 1)
    return (lane // HEAD_DIM) == hh


def _band_prompt_kernel(q_ref, k0_ref, k1_ref, k2_ref, k3_ref, v0_ref, v1_ref, v2_ref, v3_ref, e_ref, mask_ref,
                        o_ref, bias_ref, *, tq):
    i = pl.program_id(1)

    @pl.when((pl.program_id(0) == 0) & (i == 0))
    def _():
        _build_bias(e_ref, mask_ref, bias_ref)

    k_refs = (k0_ref, k1_ref, k2_ref, k3_ref)
    v_refs = (v0_ref, v1_ref, v2_ref, v3_ref)
    for qb in range(2):
        pad = [jnp.where(2 * i + qb - 2 + j >= 0, 0.0, NEG) for j in range(2)] + [0.0]
        for p in range(N_PAIRS):
            cols = slice(p * LANES, (p + 1) * LANES)
            q = q_ref[qb * tq:(qb + 1) * tq, cols]
            ks = [r[:, cols] for r in k_refs[qb:qb + 3]]
            vs = [r[:, cols] for r in v_refs[qb:qb + 3]]
            out = jnp.zeros((tq, LANES), F32)
            for hh in range(2):
                msk = _head_mask(hh)
                qh = jnp.where(msk, q, jnp.zeros_like(q))
                scores = [_dot_nt(qh, ks[j]) + bias_ref[2 * p + hh, :, j * tq:(j + 1) * tq] + pad[j]
                          for j in range(3)]
                pv, l = _softmax_pv(scores, vs)
                out = jnp.where(msk, pv / l, out)
            o_ref[qb * tq:(qb + 1) * tq, cols] = out.astype(BF16)


def _band_prompt(qa, ka, va, bias_e, bias_mask, n_seq):
    t_total = qa.shape[0]
    tq = 256
    seq = t_total // n_seq
    nq = seq // tq
    qmap = lambda b, i: (b * (nq // 2) + i, 0)

    def kmap(j):
        return lambda b, i: (b * nq + jnp.maximum(2 * i - 2 + j, 0), 0)

    blk = lambda m: pl.BlockSpec((tq, WIDTH), m)
    qblk = pl.BlockSpec((2 * tq, WIDTH), qmap)
    return pl.pallas_call(
        functools.partial(_band_prompt_kernel, tq=tq),
        grid=(n_seq, nq // 2),
        in_specs=[qblk] + [blk(kmap(j)) for j in range(4)] + [blk(kmap(j)) for j in range(4)]
        + [_const_spec(bias_e.shape), _const_spec(bias_mask.shape)],
        out_specs=qblk,
        out_shape=jax.ShapeDtypeStruct((t_total, WIDTH), BF16),
        scratch_shapes=[pltpu.VMEM((N_HEADS,) + bias_mask.shape, F32)],
        compiler_params=_params(2),
        name="band_prompt",
    )(qa, ka, ka, ka, ka, va, va, va, va, bias_e, bias_mask)


def _band_sample_kernel(q_ref, kc_ref, vc_ref, kn_ref, vn_ref, e_ref, mask_ref, o_ref, ko_ref, vo_ref,
                        bias_ref):
    p_rows = kc_ref.shape[2]
    t = q_ref.shape[0]

    @pl.when(pl.program_id(0) == 0)
    def _():
        _build_bias(e_ref, mask_ref, bias_ref)

    pad = jnp.zeros((LANES - t, HEAD_DIM), BF16)
    for h in range(N_HEADS):
        cols = slice(h * HEAD_DIM, (h + 1) * HEAD_DIM)
        q = q_ref[:, cols]
        k_new, v_new = kn_ref[:, cols], vn_ref[:, cols]
        k_cache, v_cache = kc_ref[h], vc_ref[h]
        s_cache = _dot(q, k_cache.astype(BF16)) + bias_ref[h, :, 0:p_rows]
        s_new = (_dot_nt(q, jnp.concatenate([k_new.astype(BF16), pad], axis=0))
                 + bias_ref[h, :, p_rows:p_rows + LANES])
        chunks = [s_cache[:, c * LANES:(c + 1) * LANES] for c in range(p_rows // LANES)] + [s_new]
        cmax = chunks[0]
        for c in chunks[1:]:
            cmax = jnp.maximum(cmax, c)
        m = jnp.max(cmax, axis=-1, keepdims=True)
        ps = [jnp.exp2(c - m) for c in chunks]
        lsum = ps[0]
        for p in ps[1:]:
            lsum = lsum + p
        pv = (_dot_nt(jnp.concatenate(ps[:-1], axis=-1).astype(BF16), v_cache.astype(BF16))
              + _dot(ps[-1].astype(BF16), jnp.concatenate([v_new.astype(BF16), pad], axis=0)))
        o_ref[:, cols] = (pv / jnp.sum(lsum, axis=-1, keepdims=True)).astype(BF16)
        ko_ref[h] = jnp.concatenate([k_cache[:, t:], k_new.T], axis=1)
        vo_ref[h] = jnp.concatenate([v_cache[:, t:], v_new.T], axis=1)


def _band_sample(qa, ka_f, va_f, cache_kt, cache_vt, bias_e, bias_mask):
    n = cache_kt.shape[0]
    t = qa.shape[0] // n
    row = pl.BlockSpec((t, WIDTH), lambda b: (b, 0))
    cache = pl.BlockSpec((None,) + cache_kt.shape[1:], lambda b: (b, 0, 0, 0))
    return pl.pallas_call(
        _band_sample_kernel,
        grid=(n,),
        in_specs=[row, cache, cache, row, row, _const_spec(bias_e.shape), _const_spec(bias_mask.shape)],
        out_specs=[row, cache, cache],
        out_shape=[jax.ShapeDtypeStruct(qa.shape, BF16),
                   jax.ShapeDtypeStruct(cache_kt.shape, F32),
                   jax.ShapeDtypeStruct(cache_vt.shape, F32)],
        scratch_shapes=[pltpu.VMEM((N_HEADS,) + bias_mask.shape, F32)],
        compiler_params=_params(1),
        name="band_sample",
    )(qa, cache_kt, cache_vt, ka_f, va_f, bias_e, bias_mask)


def _band_bias(rel_bias, n_q, n_k, offset, band_mask, valid_k):
    assert n_q + n_k - 1 <= BIAS_PERIOD
    m = np.arange(BIAS_PERIOD)
    w = np.where(m < n_k, m, m - BIAS_PERIOD)
    idx = np.clip(offset - w, -MAX_REL, MAX_REL) + MAX_REL
    e = jnp.take(rel_bias.astype(F32), jnp.asarray(idx), axis=1) * LOG2E
    qi = np.arange(n_q)[:, None]
    kj = np.arange(n_k)[None, :]
    valid = np.broadcast_to(kj < valid_k, (n_q, n_k))
    if band_mask:
        rel_chunk = (qi + offset) // CHUNK - kj // CHUNK
        valid = valid & (rel_chunk >= 0) & (rel_chunk <= LEFT_CHUNKS)
    return e, jnp.asarray(np.where(valid, 0.0, NEG), F32)


def _build_bias(e_ref, mask_ref, bias_sc):
    rows, cols = mask_ref.shape
    for h in range(N_HEADS):
        spread = jnp.broadcast_to(e_ref[h:h + 1, :], (rows, BIAS_PERIOD))
        bias_sc[h] = pltpu.roll(spread, 0, 1, stride=1, stride_axis=0)[:, :cols] + mask_ref[...]


def _aug_head_mask(h_in_pair, pair):
    lane = lax.broadcasted_iota(jnp.int32, (1, 2 * LANES), 1)
    head = 2 * pair + h_in_pair
    in_q = (lane < LANES) & ((lane // HEAD_DIM) == h_in_pair)
    in_aug = (lane >= LANES) & (((lane - LANES) // AUG_STRIDE) == head)
    return in_q | in_aug


def _flash_update(s, v, m_ref, l_ref, acc_ref, h, row_bias=None, v_seq_minor=False):
    chunks = [s[:, c * LANES:(c + 1) * LANES] for c in range(s.shape[1] // LANES)]
    if row_bias is not None:
        chunks = [c + row_bias for c in chunks]
    cmax = chunks[0]
    for c in chunks[1:]:
        cmax = jnp.maximum(cmax, c)
    m_prev = m_ref[h]
    m_new = jnp.maximum(m_prev, jnp.max(cmax, axis=-1, keepdims=True))
    alpha = jnp.exp2(m_prev - m_new)
    ps = [jnp.exp2(c - m_new) for c in chunks]
    lsum = ps[0]
    for p in ps[1:]:
        lsum = lsum + p
    l_ref[h] = alpha * l_ref[h] + lsum
    p = jnp.concatenate(ps, axis=-1).astype(BF16)
    pv = _dot_nt(p, v) if v_seq_minor else _dot(p, v)
    acc_ref[h] = alpha[:, :pv.shape[1]] * acc_ref[h] + pv
    m_ref[h] = m_new


def _flash_result(l_ref, acc_ref, pair):
    outs = [acc_ref[2 * pair + hh] / jnp.sum(l_ref[2 * pair + hh], axis=-1, keepdims=True)
            for hh in range(2)]
    return jnp.where(_head_mask(0), outs[0], outs[1])


def _fox_prompt_kernel(qi_ref, kj_ref, q_ref, k_ref, v_ref, o_ref, qh_sc, m_sc, l_sc, acc_sc, *, tq, tk):
    t = pl.program_id(1)
    i = qi_ref[t]
    j = kj_ref[t]

    @pl.when(j == 0)
    def _():
        for p in range(N_PAIRS):
            q = q_ref[p]
            for hh in range(2):
                qh_sc[2 * p + hh] = jnp.where(_aug_head_mask(hh, p), q, jnp.zeros_like(q))
        m_sc[...] = jnp.full_like(m_sc, NEG)
        l_sc[...] = jnp.zeros_like(l_sc)
        acc_sc[...] = jnp.zeros_like(acc_sc)

    def step(diagonal):
        if diagonal:
            keep = (lax.broadcasted_iota(jnp.int32, (tq, tk), 1)
                    <= lax.broadcasted_iota(jnp.int32, (tq, tk), 0))
        for p in range(N_PAIRS):
            k = k_ref[p]
            v = v_ref[:, p * LANES:(p + 1) * LANES]
            for hh in range(2):
                s = _dot_nt(qh_sc[2 * p + hh], k)
                if diagonal:
                    s = jnp.where(keep, s, NEG)
                _flash_update(s, v, m_sc, l_sc, acc_sc, 2 * p + hh)

    @pl.when(j < i)
    def _():
        step(False)

    @pl.when(j == i)
    def _():
        step(True)
        for p in range(N_PAIRS):
            o_ref[:, p * LANES:(p + 1) * LANES] = _flash_result(l_sc, acc_sc, p).astype(BF16)


def _fox_prompt(qcat, kcat, vb, n_seq):
    seq = qcat.shape[2]
    tq = tk = 1024
    nt = seq // tq
    qi = np.concatenate([np.full(i + 1, i) for i in range(nt)]).astype(np.int32)
    kj = np.concatenate([np.arange(i + 1) for i in range(nt)]).astype(np.int32)
    grid_spec = pltpu.PrefetchScalarGridSpec(
        num_scalar_prefetch=2,
        grid=(n_seq, len(qi)),
        in_specs=[
            pl.BlockSpec((None, N_PAIRS, tq, 2 * LANES), lambda b, t, qi, kj: (b, 0, qi[t], 0)),
            pl.BlockSpec((None, N_PAIRS, tk, 2 * LANES), lambda b, t, qi, kj: (b, 0, kj[t], 0)),
            pl.BlockSpec((tk, WIDTH), lambda b, t, qi, kj: (b * nt + kj[t], 0)),
        ],
        out_specs=pl.BlockSpec((tq, WIDTH), lambda b, t, qi, kj: (b * nt + qi[t], 0)),
        scratch_shapes=[pltpu.VMEM((N_HEADS, tq, 2 * LANES), BF16), pltpu.VMEM((N_HEADS, tq, LANES), F32),
                        pltpu.VMEM((N_HEADS, tq, LANES), F32), pltpu.VMEM((N_HEADS, tq, LANES), F32)],
    )
    return pl.pallas_call(
        functools.partial(_fox_prompt_kernel, tq=tq, tk=tk),
        grid_spec=grid_spec,
        out_shape=jax.ShapeDtypeStruct(vb.shape, BF16),
        compiler_params=_params(2),
        name="fox_prompt",
    )(jnp.asarray(qi), jnp.asarray(kj), qcat, kcat, vb)


def _suffix_sum_exclusive(x):
    n = x.shape[1]
    lane = lax.broadcasted_iota(jnp.int32, x.shape, 1)
    y = jnp.where(lane + 1 < n, pltpu.roll(x, n - 1, axis=1), 0.0)
    shift = 1
    while shift < n:
        y = y + jnp.where(lane + shift < n, pltpu.roll(y, n - shift, axis=1), 0.0)
        shift *= 2
    return y


def _fox_sample_kernel(q_ref, kc_ref, vc_ref, kn_ref, vn_ref, lfc_ref, lfn_ref, u_ref,
                       o_ref, cq_sc, car_sc, m_sc, l_sc, acc_sc, *, n_cache_tiles, sub_keys):
    j = pl.program_id(1)
    t = q_ref.shape[1]

    def cum_new():
        hi, mid, lo = _split3(lfn_ref[...])
        u = u_ref[...]
        return _dot(hi, u) + _dot(mid, u) + _dot(lo, u)

    def q_head(h):
        return q_ref[h // 2, :, (h % 2) * HEAD_DIM:(h % 2 + 1) * HEAD_DIM]

    @pl.when(j == 0)
    def _():
        cn = cum_new() * LOG2E
        eye = (lax.broadcasted_iota(jnp.int32, (t, LANES), 0)
               == lax.broadcasted_iota(jnp.int32, (t, LANES), 1))
        for h in range(N_HEADS):
            col = jnp.sum(jnp.where(eye, jnp.broadcast_to(cn[h:h + 1, :], (t, LANES)), 0.0),
                          axis=-1, keepdims=True)
            cq_sc[h] = jnp.broadcast_to(col, (t, LANES))
        car_sc[...] = jnp.zeros_like(car_sc)
        m_sc[...] = jnp.full_like(m_sc, NEG)
        l_sc[...] = jnp.zeros_like(l_sc)
        acc_sc[...] = jnp.zeros_like(acc_sc)

    lf = lfc_ref[...]
    r = _suffix_sum_exclusive(lf) + car_sc[:, 0:1]
    car_sc[...] = jnp.broadcast_to(r[:, 0:1] + lf[:, 0:1], car_sc.shape)
    r = r * LOG2E
    for sub in range(lf.shape[1] // sub_keys):
        keys = slice(sub * sub_keys, (sub + 1) * sub_keys)
        for h in range(N_HEADS):
            s = _dot(q_head(h), kc_ref[h, :, keys].astype(BF16)) + r[h:h + 1, keys]
            _flash_update(s, vc_ref[h, :, keys].astype(BF16), m_sc, l_sc, acc_sc, h,
                          row_bias=cq_sc[h], v_seq_minor=True)

    @pl.when(j == n_cache_tiles - 1)
    def _():
        cn = cum_new() * LOG2E
        causal = (lax.broadcasted_iota(jnp.int32, (t, LANES), 1)
                  <= lax.broadcasted_iota(jnp.int32, (t, LANES), 0))
        pad = jnp.zeros((LANES - t, HEAD_DIM), BF16)
        for h in range(N_HEADS):
            cols = slice(h * HEAD_DIM, (h + 1) * HEAD_DIM)
            k = jnp.concatenate([kn_ref[h // 2, :, (h % 2) * HEAD_DIM:(h % 2 + 1) * HEAD_DIM], pad], axis=0)
            v = jnp.concatenate([vn_ref[:, cols], pad], axis=0)
            s = _dot_nt(q_head(h), k) + cq_sc[h] - cn[h:h + 1, :]
            _flash_update(jnp.where(causal, s, NEG), v, m_sc, l_sc, acc_sc, h)
            o_ref[:, cols] = (acc_sc[h] / jnp.sum(l_sc[h], axis=-1, keepdims=True)).astype(BF16)


def _fox_sample(qcat, kcat, vb, cache_kt, cache_vt, lf_cache_t, lf_new_t):
    n, _, _, p_rows = cache_kt.shape
    t = vb.shape[0] // n
    tk = 4096
    nct = p_rows // tk
    u = jnp.asarray(np.pad(np.triu(np.ones((t, t))), ((0, 0), (0, LANES - t))), BF16)
    rev = lambda j: nct - 1 - j
    cache = pl.BlockSpec((None, N_HEADS, HEAD_DIM, tk), lambda b, j: (b, 0, 0, rev(j)))
    new_cat = pl.BlockSpec((None, N_PAIRS, t, 2 * LANES), lambda b, j: (0, 0, b, 0))
    return pl.pallas_call(
        functools.partial(_fox_sample_kernel, n_cache_tiles=nct, sub_keys=tk),
        grid=(n, nct),
        in_specs=[
            new_cat, cache, cache, new_cat,
            pl.BlockSpec((t, WIDTH), lambda b, j: (b, 0)),
            pl.BlockSpec((None, N_HEADS, tk), lambda b, j: (b, 0, rev(j))),
            pl.BlockSpec((None, N_HEADS, t), lambda b, j: (b, 0, 0)),
            _const_spec(u.shape),
        ],
        out_specs=pl.BlockSpec((t, WIDTH), lambda b, j: (b, 0)),
        out_shape=jax.ShapeDtypeStruct(vb.shape, BF16),
        scratch_shapes=[pltpu.VMEM((N_HEADS, t, LANES), F32), pltpu.VMEM((N_HEADS, LANES), F32),
                        pltpu.VMEM((N_HEADS, t, LANES), F32), pltpu.VMEM((N_HEADS, t, LANES), F32),
                        pltpu.VMEM((N_HEADS, t, HEAD_DIM), F32)],
        compiler_params=_params(2),
        name="fox_sample",
    )(qcat, cache_kt, cache_vt, kcat, vb, lf_cache_t, lf_new_t, u)


GROUP_LANE = 64
ROW_ALIGN = 16
MOE_TILE = 512
SORT_ROWS = 640
XS_WIDTH = D_MODEL + 2 * LANES


def _route(r):
    lane_i = lax.broadcasted_iota(jnp.int32, r.shape, 1)
    lane = lane_i.astype(F32)
    lane_grp = (lane_i // EXPERTS_PER_GROUP).astype(F32)
    big = float(LANES)
    is_coarse = (lane_i >= N_EXPERTS) & (lane_i < N_EXPERTS + N_GROUPS)
    cm = jnp.where(is_coarse, r, NEG)
    cmax = cm.max(axis=-1, keepdims=True)
    grp = jnp.min(jnp.where(cm == cmax, lane - N_EXPERTS, big), axis=-1, keepdims=True)
    pg_sel = 1.0 / jnp.sum(jnp.exp(cm - cmax), axis=-1, keepdims=True)
    in_grp = (lane_i < N_EXPERTS) & (lane_grp == grp)
    fm = jnp.where(in_grp, r, NEG)
    m1 = fm.max(axis=-1, keepdims=True)
    denom = jnp.sum(jnp.exp(fm - m1), axis=-1, keepdims=True)
    i1 = jnp.min(jnp.where(fm == m1, lane, big), axis=-1, keepdims=True)
    fm2 = jnp.where(lane == i1, NEG, fm)
    m2 = fm2.max(axis=-1, keepdims=True)
    i2 = jnp.min(jnp.where(fm2 == m2, lane, big), axis=-1, keepdims=True)
    p1 = 1.0 / denom
    p2 = jnp.exp(m2 - m1) / denom
    tot = p1 + p2
    comb = (jnp.where(lane == i1, pg_sel * (p1 / tot), 0.0)
            + jnp.where(lane == i2, pg_sel * (p2 / tot), 0.0))
    return comb, grp


def _merge_kernel(x_ref, oa_ref, ob_ref, g_ref, wg_ref, wpa_ref, wpb_ref, wo_ref, gf_ref, wrh_ref, wrm_ref,
                  br_ref, y_ref, hx_ref, route_ref, grp_t_ref, cnt_ref):
    x = x_ref[...]
    h = _rms(x, g_ref[...]).astype(BF16)
    gate = jax.nn.sigmoid(_dot(h, wg_ref[...]))
    mix = (gate[:, :D_MODEL] * _dot(oa_ref[...], wpa_ref[...])
           + gate[:, D_MODEL:] * _dot(ob_ref[...], wpb_ref[...]))
    y = x + _dot(mix.astype(BF16), wo_ref[...])
    y_ref[...] = y

    hx = _rms(y, gf_ref[...])
    hx_ref[:, 0:D_MODEL] = hx.astype(BF16)
    h_hi, h_mid, _ = _split3(hx)
    r = _dot(h_hi, wrh_ref[...]) + _dot(h_hi, wrm_ref[...]) + _dot(h_mid, wrh_ref[...]) + br_ref[...]
    comb, grp = _route(r)
    lane = lax.broadcasted_iota(jnp.int32, comb.shape, 1)
    route = jnp.where(lane == GROUP_LANE, grp, comb)
    route_ref[...] = route
    r_hi = route.astype(BF16)
    hx_ref[:, D_MODEL:D_MODEL + LANES] = r_hi
    hx_ref[:, D_MODEL + LANES:] = (route - r_hi.astype(F32)).astype(BF16)
    grp_t_ref[...] = route.T[GROUP_LANE:GROUP_LANE + 8, :]
    cnt = jnp.sum(jnp.where(lane.astype(F32) == grp, 1.0, 0.0), axis=0, keepdims=True)
    cnt_ref[...] = jnp.broadcast_to(cnt, cnt_ref.shape)


def _merge(x, oa, ob, w):
    t_total = x.shape[0]
    tm = MOE_TILE
    n_tiles = t_total // tm
    row = lambda n: pl.BlockSpec((tm, n), lambda i: (i, 0))
    consts = [w["g_mix"], w["w_gate"], w["w_pa"], w["w_pb"], w["w_o"], w["g_ffn"], w["w_router_hi"],
              w["w_router_mid"], w["b_router"]]
    return pl.pallas_call(
        _merge_kernel,
        grid=(n_tiles,),
        in_specs=[row(D_MODEL), row(WIDTH), row(WIDTH)] + [_const_spec(c.shape) for c in consts],
        out_specs=[row(D_MODEL), row(XS_WIDTH), row(LANES), pl.BlockSpec((8, tm), lambda i: (0, i)),
                   pl.BlockSpec((None, 8, LANES), lambda i: (i, 0, 0))],
        out_shape=[jax.ShapeDtypeStruct(x.shape, F32), jax.ShapeDtypeStruct((t_total, XS_WIDTH), BF16),
                   jax.ShapeDtypeStruct((t_total, LANES), F32), jax.ShapeDtypeStruct((8, t_total), F32),
                   jax.ShapeDtypeStruct((n_tiles, 8, LANES), F32)],
        compiler_params=_params(1),
        name="merge",
    )(x, oa, ob, *consts)


def _plan_kernel(cnt_ref, ib_ref, dst_ref, len_ref, fill_ref, inblk_ref, outblk_ref, grp_ref, kind_ref, *,
                 n_tiles, n_steps):
    align_bits = ROW_ALIGN.bit_length() - 1
    tile_bits = MOE_TILE.bit_length() - 1
    zero = jnp.int32(0)

    def per_tile(t, tot):
        run = zero
        new_tot = []
        for g in range(N_GROUPS):
            n = ((cnt_ref[t * N_GROUPS + g] + (ROW_ALIGN - 1)) >> align_bits) << align_bits
            ib_ref[t * N_GROUPS + g] = run
            len_ref[t * N_GROUPS + g] = n
            dst_ref[t * N_GROUPS + g] = tot[g]
            run = run + n
            new_tot.append(tot[g] + n)
        return tuple(new_tot)

    tot = lax.fori_loop(0, n_tiles, per_tile, (zero,) * N_GROUPS)
    n_ffn = [(tot[g] + (MOE_TILE - 1)) >> tile_bits for g in range(N_GROUPS)]
    g_base, ends, acc_rows, acc_tiles = [], [], zero, zero
    for g in range(N_GROUPS):
        g_base.append(acc_rows)
        acc_rows = acc_rows + ((n_ffn[g] + 1) << tile_bits)
        acc_tiles = acc_tiles + n_ffn[g] + 1
        ends.append(acc_tiles)

    def add_base(t, carry):
        for g in range(N_GROUPS):
            dst_ref[t * N_GROUPS + g] = dst_ref[t * N_GROUPS + g] + g_base[g]
        return carry

    lax.fori_loop(0, n_tiles, add_base, zero)
    for g in range(N_GROUPS):
        fill_ref[g] = g_base[g] + tot[g]
        fill_ref[N_GROUPS + g] = g_base[g] + (n_ffn[g] << tile_bits)
    fill_ref[2 * N_GROUPS] = ends[-1]
    first_blk = g_base[N_GROUPS - 1]
    for g in reversed(range(N_GROUPS - 1)):
        first_blk = jnp.where(n_ffn[g] > 0, g_base[g], first_blk)
    first_blk = first_blk >> tile_bits

    def pick(vals, g):
        out = vals[N_GROUPS - 1]
        for i in reversed(range(N_GROUPS - 1)):
            out = jnp.where(g == i, vals[i], out)
        return out

    def per_step(k, carry):
        kk = jnp.minimum(k, ends[-1] - 1)
        g = zero
        for i in range(N_GROUPS - 1):
            g = g + jnp.where(kk >= ends[i], 1, 0)
        j = kk - pick([zero] + ends[:-1], g)
        is_expert = j < pick(n_ffn, g)
        out_blk = jnp.where(k < ends[-1], (pick(g_base, g) >> tile_bits) + j, k)
        outblk_ref[k] = out_blk
        inblk_ref[k] = jnp.where(is_expert, out_blk, first_blk)
        grp_ref[k] = g
        kind_ref[k] = jnp.where(is_expert, 1, 0)
        return carry

    lax.fori_loop(0, n_steps, per_step, zero)


def _moe_plan(cnt, t_total):
    n_tiles = t_total // MOE_TILE
    n_steps = (n_tiles + 2 * N_GROUPS + (N_GROUPS * (ROW_ALIGN - 1) * n_tiles + MOE_TILE - 1) // MOE_TILE)
    smem = pl.BlockSpec(memory_space=pltpu.SMEM)
    sizes = [n_tiles * N_GROUPS] * 3 + [2 * N_GROUPS + 1] + [n_steps] * 4
    outs = pl.pallas_call(
        functools.partial(_plan_kernel, n_tiles=n_tiles, n_steps=n_steps),
        grid_spec=pltpu.PrefetchScalarGridSpec(num_scalar_prefetch=1, grid=(1,), in_specs=[],
                                               out_specs=[smem] * len(sizes)),
        out_shape=[jax.ShapeDtypeStruct((n,), jnp.int32) for n in sizes],
        compiler_params=_params(1),
        name="moe_plan",
    )(cnt.reshape(-1))
    return (*outs, n_steps)


def _sort_kernel(ib_ref, dst_ref, len_ref, fill_ref, hx_ref, grp_t_ref, tri_ref, xs_ref, xsort, sems):
    i = pl.program_id(0)
    n = pl.num_programs(0)
    tm = hx_ref.shape[0]
    half = tm // 2
    slot = i % 2

    def copy(src, dst, rows, sl, g, part):
        return pltpu.make_async_copy(xsort.at[sl, pl.ds(pl.multiple_of(src, ROW_ALIGN), rows)],
                                     xs_ref.at[pl.ds(pl.multiple_of(dst, ROW_ALIGN), rows)], sems.at[sl, g, part])

    def for_segments(step, sl, act):
        for g in range(N_GROUPS):
            src, dst = ib_ref[step * N_GROUPS + g], dst_ref[step * N_GROUPS + g]
            act(copy(src, dst, half, sl, g, 0))

            @pl.when(len_ref[step * N_GROUPS + g] > half)
            def _():
                act(copy(src + half, dst + half, half, sl, g, 1))

    def run(cs):
        for c in cs:
            c.start()
        for c in cs:
            c.wait()

    @pl.when(i == 0)
    def _():
        for sl in range(2):
            xsort[sl, SORT_ROWS:, :] = jnp.zeros((xsort.shape[1] - SORT_ROWS, XS_WIDTH), BF16)

    g_row = grp_t_ref[0:1, :]
    sub = lax.broadcasted_iota(jnp.int32, (8, tm), 0).astype(F32)
    mine = sub == g_row
    before = _dot(jnp.where(mine, 1.0, 0.0).astype(BF16), tri_ref[...])
    dest = jnp.sum(jnp.where(mine, before, 0.0), axis=0, keepdims=True)
    for g in range(N_GROUPS):
        base = jnp.full(dest.shape, ib_ref[i * N_GROUPS + g], jnp.int32).astype(F32)
        dest = dest + jnp.where(g_row == g, base, 0.0)
    rows = lax.broadcasted_iota(jnp.int32, (SORT_ROWS, tm), 0).astype(F32)
    perm = jnp.where(rows == dest, 1.0, 0.0).astype(BF16)
    xsort[slot, 0:SORT_ROWS, :] = _dot(perm, hx_ref[...]).astype(BF16)

    @pl.when(i > 0)
    def _():
        for_segments(i - 1, 1 - slot, lambda c: c.wait())

    for_segments(i, slot, lambda c: c.start())

    @pl.when(i == n - 1)
    def _():
        for_segments(i, slot, lambda c: c.wait())
        run([copy(SORT_ROWS, fill_ref[g], tm, slot, g, 0) for g in range(N_GROUPS)])
        run([copy(SORT_ROWS, fill_ref[N_GROUPS + g], tm, slot, g, 0) for g in range(N_GROUPS)])

        @pl.loop(fill_ref[2 * N_GROUPS], xs_ref.shape[0] // tm)
        def _(b):
            run([copy(SORT_ROWS, b * tm, tm, slot, 0, 0)])


def _expert_kernel(ib_ref, ob_ref, grp_ref, kind_ref, xs_ref, w1_ref, w3_ref, w2_ref, ys_ref):
    del ib_ref, ob_ref
    k = pl.program_id(0)

    @pl.when(kind_ref[k] == 1)
    def _():
        x = xs_ref[:, 0:D_MODEL]
        side = lambda w_ref: jnp.concatenate([w_ref[e].astype(BF16) for e in range(EXPERTS_PER_GROUP)], axis=1)
        hid = jax.nn.silu(_dot(x, side(w1_ref))) * _dot(x, side(w3_ref))
        comb = (xs_ref[:, D_MODEL:D_MODEL + LANES].astype(F32) + xs_ref[:, D_MODEL + LANES:].astype(F32))
        lane = lax.broadcasted_iota(jnp.int32, comb.shape, 1)
        first = grp_ref[k] * EXPERTS_PER_GROUP
        scale = []
        for e in range(EXPERTS_PER_GROUP):
            w_e = jnp.sum(jnp.where(lane == first + e, comb, 0.0), axis=-1, keepdims=True)
            scale.append(jnp.broadcast_to(w_e, (comb.shape[0], D_EXPERT)))
        hid = hid * jnp.concatenate(scale, axis=-1)
        w2 = w2_ref[...].astype(BF16).reshape(EXPERTS_PER_GROUP * D_EXPERT, D_MODEL)
        ys_ref[...] = _dot(hid.astype(BF16), w2).astype(BF16)

    @pl.when(kind_ref[k] == 0)
    def _():
        ys_ref[...] = jnp.zeros_like(ys_ref)


def _unsort_kernel(ib_ref, dst_ref, len_ref, y_ref, route_ref, tri_ref, ys_ref, o_ref, ybuf, yasm, sems):
    i = pl.program_id(0)
    n = pl.num_programs(0)
    tm = y_ref.shape[0]
    half = tm // 2
    slot = i % 2

    def fetch(step, sl, act):
        for g in range(N_GROUPS):
            src = pl.multiple_of(dst_ref[step * N_GROUPS + g], ROW_ALIGN)
            act(pltpu.make_async_copy(ys_ref.at[pl.ds(src, half)], ybuf.at[sl, g, 0:half], sems.at[sl, g, 0]))

            @pl.when(len_ref[step * N_GROUPS + g] > half)
            def _():
                act(pltpu.make_async_copy(ys_ref.at[pl.ds(src + half, half)], ybuf.at[sl, g, half:tm],
                                          sems.at[sl, g, 1]))

    @pl.when(i == 0)
    def _():
        yasm[...] = jnp.zeros_like(yasm)
        fetch(0, 0, lambda c: c.start())

    @pl.when(i + 1 < n)
    def _():
        fetch(i + 1, 1 - slot, lambda c: c.start())

    route = route_ref[...]
    lane = lax.broadcasted_iota(jnp.int32, route.shape, 1).astype(F32)
    grp = jnp.sum(jnp.where(lane == GROUP_LANE, route, 0.0), axis=-1, keepdims=True)
    mine = lane == grp
    before = _dot(tri_ref[...], jnp.where(mine, 1.0, 0.0).astype(BF16))
    dest = jnp.sum(jnp.where(mine, before, 0.0), axis=-1, keepdims=True)
    for g in range(N_GROUPS):
        base = jnp.full(dest.shape, ib_ref[i * N_GROUPS + g], jnp.int32).astype(F32)
        dest = dest + jnp.where(grp == g, base, 0.0)
    cols = lax.broadcasted_iota(jnp.int32, (tm, SORT_ROWS), 1).astype(F32)
    perm_t = jnp.where(cols == dest, 1.0, 0.0).astype(BF16)

    fetch(i, slot, lambda c: c.wait())
    for g in range(N_GROUPS):
        row0 = pl.multiple_of(ib_ref[i * N_GROUPS + g], ROW_ALIGN)
        yasm[pl.ds(row0, half), :] = ybuf[slot, g, 0:half]

        @pl.when(len_ref[i * N_GROUPS + g] > half)
        def _():
            yasm[pl.ds(row0 + half, half), :] = ybuf[slot, g, half:tm]
    o_ref[...] = y_ref[...] + _dot(perm_t, yasm[0:SORT_ROWS, :])


def _moe(y, hx, route, grp_t, cnt, w):
    t_total = y.shape[0]
    tm = MOE_TILE
    n_tiles = t_total // tm
    assert SORT_ROWS >= tm + N_GROUPS * (ROW_ALIGN - 1)
    in_base, dst, seg_len, fill, in_blk, out_blk, grp_of_step, is_expert, n_steps = _moe_plan(
        cnt[:, 0, :N_GROUPS].astype(jnp.int32), t_total)
    cap_rows = n_steps * tm
    lower = jnp.asarray(np.tril(np.ones((tm, tm)), -1), BF16)
    upper = jnp.asarray(np.triu(np.ones((tm, tm)), 1), BF16)
    stage_rows = SORT_ROWS + tm
    any_spec = pl.BlockSpec(memory_space=pl.ANY)

    xs = pl.pallas_call(
        _sort_kernel,
        grid_spec=pltpu.PrefetchScalarGridSpec(
            num_scalar_prefetch=4,
            grid=(n_tiles,),
            in_specs=[pl.BlockSpec((tm, XS_WIDTH), lambda i, *_: (i, 0)),
                      pl.BlockSpec((8, tm), lambda i, *_: (0, i)),
                      pl.BlockSpec((tm, tm), lambda i, *_: (0, 0))],
            out_specs=any_spec,
            scratch_shapes=[pltpu.VMEM((2, stage_rows, XS_WIDTH), BF16),
                            pltpu.SemaphoreType.DMA((2, N_GROUPS, 2))],
        ),
        out_shape=jax.ShapeDtypeStruct((cap_rows, XS_WIDTH), BF16),
        compiler_params=_params(1),
        name="moe_sort",
    )(in_base, dst, seg_len, fill, hx, grp_t, upper)

    step_map = lambda which: (lambda k, ib, ob, grp, kind: ((ib, ob, grp)[which][k], 0))
    wmap = lambda k, ib, ob, grp, kind: (grp[k], 0, 0)
    wspec = pl.BlockSpec((EXPERTS_PER_GROUP, D_MODEL, D_EXPERT), wmap)
    ys = pl.pallas_call(
        _expert_kernel,
        grid_spec=pltpu.PrefetchScalarGridSpec(
            num_scalar_prefetch=4,
            grid=(n_steps,),
            in_specs=[pl.BlockSpec((tm, XS_WIDTH), step_map(0)), wspec, wspec, pl.BlockSpec((EXPERTS_PER_GROUP, D_EXPERT, D_MODEL), wmap)],
            out_specs=pl.BlockSpec((tm, D_MODEL), step_map(1)),
        ),
        out_shape=jax.ShapeDtypeStruct((cap_rows, D_MODEL), BF16),
        compiler_params=_params(1),
        name="moe_experts",
    )(in_blk, out_blk, grp_of_step, is_expert, xs, w["w1"], w["w3"], w["w2"])

    return pl.pallas_call(
        _unsort_kernel,
        grid_spec=pltpu.PrefetchScalarGridSpec(
            num_scalar_prefetch=3,
            grid=(n_tiles,),
            in_specs=[pl.BlockSpec((tm, D_MODEL), lambda i, *_: (i, 0)),
                      pl.BlockSpec((tm, LANES), lambda i, *_: (i, 0)),
                      pl.BlockSpec((tm, tm), lambda i, *_: (0, 0)),
                      any_spec],
            out_specs=pl.BlockSpec((tm, D_MODEL), lambda i, *_: (i, 0)),
            scratch_shapes=[pltpu.VMEM((2, N_GROUPS, tm, D_MODEL), BF16),
                            pltpu.VMEM((stage_rows, D_MODEL), BF16),
                            pltpu.SemaphoreType.DMA((2, N_GROUPS, 2))],
        ),
        out_shape=jax.ShapeDtypeStruct(y.shape, F32),
        compiler_params=_params(1),
        name="moe_unsort",
    )(in_base, dst, seg_len, y, route, lower, ys)


def _prep_weights(g_mix, w_in, b_f, q_norm_a, k_norm_a, q_norm_b, k_norm_b, w_pa, w_pb, w_o,
                  g_ffn, w_rg, b_rg, w_re, b_re, w1, w3, w2):
    n_qkv = 6 * WIDTH
    tile = lambda g: jnp.tile(g, N_HEADS)
    w_router = jnp.concatenate(
        [jnp.transpose(w_re, (1, 0, 2)).reshape(D_MODEL, N_EXPERTS), w_rg,
         jnp.zeros((D_MODEL, LANES - N_EXPERTS - N_GROUPS), F32)], axis=1)
    b_router = jnp.concatenate(
        [b_re.reshape(N_EXPERTS), b_rg, jnp.zeros((LANES - N_EXPERTS - N_GROUPS,), F32)])[None, :]
    return {
        "g_mix": g_mix[None, :],
        "w_qkv": w_in[:, :n_qkv].astype(BF16),
        "w_f": jnp.pad(w_in[:, n_qkv:n_qkv + N_HEADS], ((0, 0), (0, LANES - N_HEADS))).astype(BF16),
        "b_f": jnp.pad(b_f, (0, LANES - N_HEADS))[None, :],
        "w_gate": w_in[:, n_qkv + N_HEADS:].astype(BF16),
        "gains": jnp.stack([tile(q_norm_a), tile(k_norm_a), tile(q_norm_b), tile(k_norm_b)]),
        "w_pa": w_pa.astype(BF16), "w_pb": w_pb.astype(BF16), "w_o": w_o.astype(BF16),
        "g_ffn": g_ffn[None, :],
        "w_router_hi": w_router.astype(BF16),
        "w_router_mid": (w_router - w_router.astype(BF16).astype(F32)).astype(BF16),
        "b_router": b_router,
        "w1": w1, "w3": w3, "w2": w2,
    }


def kernel(x_prompt, x_sample, cache_a_k, cache_a_v, cache_b_k, cache_b_v, cache_b_logf, g_mix, w_in, b_f, q_norm_a, k_norm_a, q_norm_b, k_norm_b, rel_bias, w_pa, w_pb, w_o, g_ffn, w_rg, b_rg, w_re, b_re, w1, w3, w2):
    assert g_mix.shape[0] == 1, "single-layer step"
    n_p, seq, _ = x_prompt.shape
    n_s, t_s, _ = x_sample.shape
    a_rows = cache_a_k.shape[2]
    w = _prep_weights(g_mix[0], w_in[0], b_f[0], q_norm_a[0], k_norm_a[0], q_norm_b[0], k_norm_b[0],
                      w_pa[0], w_pb[0], w_o[0], g_ffn[0], w_rg[0], b_rg[0], w_re[0], b_re[0],
                      w1[0], w3[0], w2[0])
    band_tq = 256
    bias_prompt = _band_bias(rel_bias[0], band_tq, 3 * band_tq, 2 * band_tq, True, 3 * band_tq)
    bias_sample = _band_bias(rel_bias[0], t_s, a_rows + LANES, a_rows, False, a_rows + t_s)

    seq_minor = lambda a: jnp.transpose(a, (0, 2, 3, 1))
    seq_major = lambda a: jnp.transpose(a, (0, 3, 1, 2))[None]

    xp = x_prompt.reshape(n_p * seq, D_MODEL)
    (qa, ka, va, ka_t, va_t, qcat, kcat, vb, kb_t, vb_t, logf_t) = _inproj(xp, n_p, w, seq_minor=True)
    o_a = _band_prompt(qa, ka, va, *bias_prompt, n_p)
    o_b = _fox_prompt(qcat, kcat, vb, n_p)
    y_p = _moe(*_merge(xp, o_a, o_b, w), w)

    xs = x_sample.reshape(n_s * t_s, D_MODEL)
    (qa_s, _, _, ka_fs, va_fs, qcat_s, kcat_s, vb_s, kb_fs, vb_fs, logf_s) = _inproj(
        xs, 1, w, seq_minor=False)
    o_as, new_ak_t, new_av_t = _band_sample(
        qa_s, ka_fs, va_fs, seq_minor(cache_a_k[0]), seq_minor(cache_a_v[0]), *bias_sample)
    lf_cache_t = jnp.transpose(cache_b_logf[0], (0, 2, 1))
    lf_new_t = jnp.transpose(logf_s.reshape(n_s, t_s, N_HEADS), (0, 2, 1))
    o_bs = _fox_sample(qcat_s, kcat_s, vb_s, seq_minor(cache_b_k[0]), seq_minor(cache_b_v[0]),
                       lf_cache_t, lf_new_t)
    y_s = _moe(*_merge(xs, o_as, o_bs, w), w)

    heads = lambda a, n, r: a.reshape(1, n, r, N_HEADS, HEAD_DIM)
    return (y_p.reshape(n_p, seq, D_MODEL), y_s.reshape(n_s, t_s, D_MODEL),
            seq_major(ka_t), seq_major(va_t), seq_major(kb_t), seq_major(vb_t),
            jnp.transpose(logf_t, (0, 2, 1))[None],
            seq_major(new_ak_t), seq_major(new_av_t),
            heads(kb_fs, n_s, t_s), heads(vb_fs, n_s, t_s), logf_s.reshape(1, n_s, t_s, N_HEADS))
```

```python
import functools

import numpy as np
import jax
import jax.numpy as jnp
from jax import lax
from jax.experimental import pallas as pl
from jax.experimental.pallas import tpu as pltpu

F32 = jnp.float32
BF16 = jnp.bfloat16

D_MODEL = 1024
HEAD_DIM = 64
N_HEADS = 8
WIDTH = N_HEADS * HEAD_DIM
N_PAIRS = N_HEADS // 2
CHUNK = 64
LEFT_CHUNKS = 8
WINDOW_ROWS = LEFT_CHUNKS * CHUNK
MAX_REL = 256
N_GROUPS = 4
EXPERTS_PER_GROUP = 8
N_EXPERTS = N_GROUPS * EXPERTS_PER_GROUP
D_EXPERT = 128
EPS = 1e-6
NEG = -1e30
LOG2E = 1.4426950408889634
LANES = 128
BIAS_PERIOD = 1024
AUG_STRIDE = 8
VMEM_LIMIT = 56 * 1024 * 1024

_NT = (((1,), (1,)), ((), ()))


def _dot(a, b):
    return jnp.dot(a, b, preferred_element_type=F32)


def _dot_nt(a, b):
    return lax.dot_general(a, b, _NT, preferred_element_type=F32)


def _split3(x):
    hi = x.astype(BF16)
    r = x - hi.astype(F32)
    mid = r.astype(BF16)
    lo = (r - mid.astype(F32)).astype(BF16)
    return hi, mid, lo


def _dot3(a_bf, x):
    hi, mid, lo = _split3(x)
    return _dot(a_bf, hi) + _dot(a_bf, mid) + _dot(a_bf, lo)


def _rms(x, g):
    ms = jnp.mean(x * x, axis=-1, keepdims=True)
    return x * lax.rsqrt(ms + EPS) * g


def _params(n_axes):
    return pltpu.CompilerParams(dimension_semantics=("arbitrary",) * n_axes,
                                vmem_limit_bytes=VMEM_LIMIT)


def _const_spec(shape):
    nd = len(shape)
    return pl.BlockSpec(shape, lambda *_: (0,) * nd)


def _inproj_kernel(x_ref, g_ref, wqkv_ref, wf_ref, bf_ref, gains_ref, bd_ref, ltri_ref,
                   selq_ref, selk_ref, oneq_ref, onek_ref,
                   qa_ref, ka_ref, va_ref, kaf_ref, vaf_ref, qcat_ref, kcat_ref, vb_ref,
                   kbf_ref, vbf_ref, logf_ref, carry_ref, *, tiles_per_seq, seq_minor):
    i = pl.program_id(0)
    tm = x_ref.shape[0]

    @pl.when(i % tiles_per_seq == 0)
    def _():
        carry_ref[...] = jnp.zeros_like(carry_ref)

    def put_f32(ref, y, last_tile_only=False):
        if not seq_minor:
            ref[...] = y
        elif last_tile_only:
            @pl.when(i % tiles_per_seq == tiles_per_seq - 1)
            def _():
                ref[...] = y.T.reshape(N_HEADS, HEAD_DIM, tm)
        else:
            ref[...] = y.T.reshape(N_HEADS, HEAD_DIM, tm)

    h = _rms(x_ref[...], g_ref[...]).astype(BF16)

    def seg(s):
        return _dot(h, wqkv_ref[:, s * WIDTH:(s + 1) * WIDTH])

    def headnorm(y, n):
        ss = _dot((y * y).astype(BF16), bd_ref[...])
        return y * lax.rsqrt(ss * (1.0 / HEAD_DIM) + EPS) * gains_ref[n:n + 1, :]

    q_a = headnorm(seg(0), 0)
    qa_ref[...] = (q_a * (HEAD_DIM ** -0.5 * LOG2E)).astype(BF16)
    k_a = headnorm(seg(1), 1)
    ka_ref[...] = k_a.astype(BF16)
    put_f32(kaf_ref, k_a, last_tile_only=True)
    v_a = seg(2)
    va_ref[...] = v_a.astype(BF16)
    put_f32(vaf_ref, v_a, last_tile_only=True)

    z = _dot(h, wf_ref[...]) + bf_ref[...]
    logf = jnp.minimum(z, 0.0) - jnp.log(1.0 + jnp.exp(-jnp.abs(z)))
    if seq_minor:
        logf_ref[...] = logf.T[0:N_HEADS, :]
    else:
        logf_ref[...] = logf[:, :N_HEADS]
    c = _dot3(ltri_ref[...], logf) + carry_ref[0:1, :]
    carry_ref[...] = jnp.broadcast_to(c[-1:, :], carry_ref.shape)
    cs = jnp.concatenate(_split3(c * LOG2E), axis=-1)
    q_aug = (_dot(cs, selq_ref[...]) + oneq_ref[...]).astype(BF16)
    k_aug = (_dot(cs, selk_ref[...]) + onek_ref[...]).astype(BF16)

    q_b = (headnorm(seg(3), 2) * (HEAD_DIM ** -0.5 * LOG2E)).astype(BF16)
    k_b = headnorm(seg(4), 3)
    put_f32(kbf_ref, k_b)
    k_b = k_b.astype(BF16)
    v_b = seg(5)
    put_f32(vbf_ref, v_b)
    vb_ref[...] = v_b.astype(BF16)
    for p in range(N_PAIRS):
        cols = slice(p * LANES, (p + 1) * LANES)
        qcat_ref[p, :, 0:LANES] = q_b[:, cols]
        qcat_ref[p, :, LANES:2 * LANES] = q_aug
        kcat_ref[p, :, 0:LANES] = k_b[:, cols]
        kcat_ref[p, :, LANES:2 * LANES] = k_aug


def _aug_constants():
    selq = np.zeros((3 * LANES, LANES), np.float32)
    selk = np.zeros((3 * LANES, LANES), np.float32)
    oneq = np.zeros((1, LANES), np.float32)
    onek = np.zeros((1, LANES), np.float32)
    for h in range(N_HEADS):
        for k in range(3):
            selq[k * LANES + h, AUG_STRIDE * h + k] = 1.0
            selk[k * LANES + h, AUG_STRIDE * h + 3 + k] = -1.0
            oneq[0, AUG_STRIDE * h + 3 + k] = 1.0
            onek[0, AUG_STRIDE * h + k] = 1.0
    return (jnp.asarray(selq, BF16), jnp.asarray(selk, BF16), jnp.asarray(oneq), jnp.asarray(onek))


def _inproj(x, n_seq, w, seq_minor):
    t_total = x.shape[0]
    tm = 512
    n_tiles = t_total // tm
    seq = t_total // n_seq
    tps = seq // tm
    row = lambda i: (i, 0)
    flat = lambda dt: (jax.ShapeDtypeStruct((t_total, WIDTH), dt), pl.BlockSpec((tm, WIDTH), row))
    if seq_minor:
        assert tm == WINDOW_ROWS
        band_f32 = (jax.ShapeDtypeStruct((n_seq, N_HEADS, HEAD_DIM, WINDOW_ROWS), F32),
                    pl.BlockSpec((None, N_HEADS, HEAD_DIM, tm), lambda i: (i // tps, 0, 0, 0)))
        fox_f32 = (jax.ShapeDtypeStruct((n_seq, N_HEADS, HEAD_DIM, seq), F32),
                   pl.BlockSpec((None, N_HEADS, HEAD_DIM, tm), lambda i: (i // tps, 0, 0, i % tps)))
        logf = (jax.ShapeDtypeStruct((n_seq, N_HEADS, seq), F32),
                pl.BlockSpec((None, N_HEADS, tm), lambda i: (i // tps, 0, i % tps)))
    else:
        band_f32 = fox_f32 = flat(F32)
        logf = (jax.ShapeDtypeStruct((t_total, N_HEADS), F32), pl.BlockSpec((tm, N_HEADS), row))
    cat = (jax.ShapeDtypeStruct((n_seq, N_PAIRS, seq, 2 * LANES), BF16),
           pl.BlockSpec((None, N_PAIRS, tm, 2 * LANES), lambda i: (i // tps, 0, i % tps, 0)))
    outs = [flat(BF16), flat(BF16), flat(BF16), band_f32, band_f32, cat, cat, flat(BF16),
            fox_f32, fox_f32, logf]
    bd = jnp.asarray(np.kron(np.eye(N_HEADS), np.ones((HEAD_DIM, HEAD_DIM))), BF16)
    ltri = jnp.asarray(np.tril(np.ones((tm, tm))), BF16)
    consts = [w["g_mix"], w["w_qkv"], w["w_f"], w["b_f"], w["gains"], bd, ltri, *_aug_constants()]
    in_specs = [pl.BlockSpec((tm, D_MODEL), row)] + [_const_spec(c.shape) for c in consts]
    return pl.pallas_call(
        functools.partial(_inproj_kernel, tiles_per_seq=tps, seq_minor=seq_minor),
        grid=(n_tiles,),
        in_specs=in_specs,
        out_specs=[o[1] for o in outs],
        out_shape=[o[0] for o in outs],
        scratch_shapes=[pltpu.VMEM((8, LANES), F32)],
        compiler_params=_params(1),
        name="inproj",
    )(x, *consts)


def _softmax_pv(scores, values):
    chunks = [[s[:, c * LANES:(c + 1) * LANES] for c in range(s.shape[1] // LANES)] for s in scores]
    cmax = None
    for cs in chunks:
        for c in cs:
            cmax = c if cmax is None else jnp.maximum(cmax, c)
    m = jnp.max(cmax, axis=-1, keepdims=True)
    pv = None
    lsum = None
    for cs, v in zip(chunks, values):
        ps = [jnp.exp2(c - m) for c in cs]
        for p in ps:
            lsum = p if lsum is None else lsum + p
        o = _dot(jnp.concatenate(ps, axis=-1).astype(BF16), v)
        pv = o if pv is None else pv + o
    return pv, jnp.sum(lsum, axis=-1, keepdims=True)


def _head_mask(hh):
    lane = lax.broadcasted_iota(jnp.int32, (1, LANES), 1)
    return (lane // HEAD_DIM) == hh


def _band_prompt_kernel(q_ref, k0_ref, k1_ref, k2_ref, k3_ref, v0_ref, v1_ref, v2_ref, v3_ref, e_ref, mask_ref,
                        o_ref, bias_ref, *, tq):
    i = pl.program_id(1)

    @pl.when((pl.program_id(0) == 0) & (i == 0))
    def _():
        _build_bias(e_ref, mask_ref, bias_ref)

    k_refs = (k0_ref, k1_ref, k2_ref, k3_ref)
    v_refs = (v0_ref, v1_ref, v2_ref, v3_ref)
    for qb in range(2):
        pad = [jnp.where(2 * i + qb - 2 + j >= 0, 0.0, NEG) for j in range(2)] + [0.0]
        for p in range(N_PAIRS):
            cols = slice(p * LANES, (p + 1) * LANES)
            q = q_ref[qb * tq:(qb + 1) * tq, cols]
            ks = [r[:, cols] for r in k_refs[qb:qb + 3]]
            vs = [r[:, cols] for r in v_refs[qb:qb + 3]]
            out = jnp.zeros((tq, LANES), F32)
            for hh in range(2):
                msk = _head_mask(hh)
                qh = jnp.where(msk, q, jnp.zeros_like(q))
                scores = [_dot_nt(qh, ks[j]) + bias_ref[2 * p + hh, :, j * tq:(j + 1) * tq] + pad[j]
                          for j in range(3)]
                pv, l = _softmax_pv(scores, vs)
                out = jnp.where(msk, pv / l, out)
            o_ref[qb * tq:(qb + 1) * tq, cols] = out.astype(BF16)


def _band_prompt(qa, ka, va, bias_e, bias_mask, n_seq):
    t_total = qa.shape[0]
    tq = 256
    seq = t_total // n_seq
    nq = seq // tq
    qmap = lambda b, i: (b * (nq // 2) + i, 0)

    def kmap(j):
        return lambda b, i: (b * nq + jnp.maximum(2 * i - 2 + j, 0), 0)

    blk = lambda m: pl.BlockSpec((tq, WIDTH), m)
    qblk = pl.BlockSpec((2 * tq, WIDTH), qmap)
    return pl.pallas_call(
        functools.partial(_band_prompt_kernel, tq=tq),
        grid=(n_seq, nq // 2),
        in_specs=[qblk] + [blk(kmap(j)) for j in range(4)] + [blk(kmap(j)) for j in range(4)]
        + [_const_spec(bias_e.shape), _const_spec(bias_mask.shape)],
        out_specs=qblk,
        out_shape=jax.ShapeDtypeStruct((t_total, WIDTH), BF16),
        scratch_shapes=[pltpu.VMEM((N_HEADS,) + bias_mask.shape, F32)],
        compiler_params=_params(2),
        name="band_prompt",
    )(qa, ka, ka, ka, ka, va, va, va, va, bias_e, bias_mask)


def _band_sample_kernel(q_ref, kc_ref, vc_ref, kn_ref, vn_ref, e_ref, mask_ref, o_ref, ko_ref, vo_ref,
                        bias_ref):
    p_rows = kc_ref.shape[2]
    t = q_ref.shape[0]

    @pl.when(pl.program_id(0) == 0)
    def _():
        _build_bias(e_ref, mask_ref, bias_ref)

    pad = jnp.zeros((LANES - t, HEAD_DIM), BF16)
    for h in range(N_HEADS):
        cols = slice(h * HEAD_DIM, (h + 1) * HEAD_DIM)
        q = q_ref[:, cols]
        k_new, v_new = kn_ref[:, cols], vn_ref[:, cols]
        k_cache, v_cache = kc_ref[h], vc_ref[h]
        s_cache = _dot(q, k_cache.astype(BF16)) + bias_ref[h, :, 0:p_rows]
        s_new = (_dot_nt(q, jnp.concatenate([k_new.astype(BF16), pad], axis=0))
                 + bias_ref[h, :, p_rows:p_rows + LANES])
        chunks = [s_cache[:, c * LANES:(c + 1) * LANES] for c in range(p_rows // LANES)] + [s_new]
        cmax = chunks[0]
        for c in chunks[1:]:
            cmax = jnp.maximum(cmax, c)
        m = jnp.max(cmax, axis=-1, keepdims=True)
        ps = [jnp.exp2(c - m) for c in chunks]
        lsum = ps[0]
        for p in ps[1:]:
            lsum = lsum + p
        pv = (_dot_nt(jnp.concatenate(ps[:-1], axis=-1).astype(BF16), v_cache.astype(BF16))
              + _dot(ps[-1].astype(BF16), jnp.concatenate([v_new.astype(BF16), pad], axis=0)))
        o_ref[:, cols] = (pv / jnp.sum(lsum, axis=-1, keepdims=True)).astype(BF16)
        ko_ref[h] = jnp.concatenate([k_cache[:, t:], k_new.T], axis=1)
        vo_ref[h] = jnp.concatenate([v_cache[:, t:], v_new.T], axis=1)


def _band_sample(qa, ka_f, va_f, cache_kt, cache_vt, bias_e, bias_mask):
    n = cache_kt.shape[0]
    t = qa.shape[0] // n
    row = pl.BlockSpec((t, WIDTH), lambda b: (b, 0))
    cache = pl.BlockSpec((None,) + cache_kt.shape[1:], lambda b: (b, 0, 0, 0))
    return pl.pallas_call(
        _band_sample_kernel,
        grid=(n,),
        in_specs=[row, cache, cache, row, row, _const_spec(bias_e.shape), _const_spec(bias_mask.shape)],
        out_specs=[row, cache, cache],
        out_shape=[jax.ShapeDtypeStruct(qa.shape, BF16),
                   jax.ShapeDtypeStruct(cache_kt.shape, F32),
                   jax.ShapeDtypeStruct(cache_vt.shape, F32)],
        scratch_shapes=[pltpu.VMEM((N_HEADS,) + bias_mask.shape, F32)],
        compiler_params=_params(1),
        name="band_sample",
    )(qa, cache_kt, cache_vt, ka_f, va_f, bias_e, bias_mask)


def _band_bias(rel_bias, n_q, n_k, offset, band_mask, valid_k):
    assert n_q + n_k - 1 <= BIAS_PERIOD
    m = np.arange(BIAS_PERIOD)
    w = np.where(m < n_k, m, m - BIAS_PERIOD)
    idx = np.clip(offset - w, -MAX_REL, MAX_REL) + MAX_REL
    e = jnp.take(rel_bias.astype(F32), jnp.asarray(idx), axis=1) * LOG2E
    qi = np.arange(n_q)[:, None]
    kj = np.arange(n_k)[None, :]
    valid = np.broadcast_to(kj < valid_k, (n_q, n_k))
    if band_mask:
        rel_chunk = (qi + offset) // CHUNK - kj // CHUNK
        valid = valid & (rel_chunk >= 0) & (rel_chunk <= LEFT_CHUNKS)
    return e, jnp.asarray(np.where(valid, 0.0, NEG), F32)


def _build_bias(e_ref, mask_ref, bias_sc):
    rows, cols = mask_ref.shape
    for h in range(N_HEADS):
        spread = jnp.broadcast_to(e_ref[h:h + 1, :], (rows, BIAS_PERIOD))
        bias_sc[h] = pltpu.roll(spread, 0, 1, stride=1, stride_axis=0)[:, :cols] + mask_ref[...]


def _aug_head_mask(h_in_pair, pair):
    lane = lax.broadcasted_iota(jnp.int32, (1, 2 * LANES), 1)
    head = 2 * pair + h_in_pair
    in_q = (lane < LANES) & ((lane // HEAD_DIM) == h_in_pair)
    in_aug = (lane >= LANES) & (((lane - LANES) // AUG_STRIDE) == head)
    return in_q | in_aug


def _flash_update(s, v, m_ref, l_ref, acc_ref, h, row_bias=None, v_seq_minor=False):
    chunks = [s[:, c * LANES:(c + 1) * LANES] for c in range(s.shape[1] // LANES)]
    if row_bias is not None:
        chunks = [c + row_bias for c in chunks]
    cmax = chunks[0]
    for c in chunks[1:]:
        cmax = jnp.maximum(cmax, c)
    m_prev = m_ref[h]
    m_new = jnp.maximum(m_prev, jnp.max(cmax, axis=-1, keepdims=True))
    alpha = jnp.exp2(m_prev - m_new)
    ps = [jnp.exp2(c - m_new) for c in chunks]
    lsum = ps[0]
    for p in ps[1:]:
        lsum = lsum + p
    l_ref[h] = alpha * l_ref[h] + lsum
    p = jnp.concatenate(ps, axis=-1).astype(BF16)
    pv = _dot_nt(p, v) if v_seq_minor else _dot(p, v)
    acc_ref[h] = alpha[:, :pv.shape[1]] * acc_ref[h] + pv
    m_ref[h] = m_new


def _flash_result(l_ref, acc_ref, pair):
    outs = [acc_ref[2 * pair + hh] / jnp.sum(l_ref[2 * pair + hh], axis=-1, keepdims=True)
            for hh in range(2)]
    return jnp.where(_head_mask(0), outs[0], outs[1])


def _fox_prompt_kernel(qi_ref, kj_ref, q_ref, k_ref, v_ref, o_ref, qh_sc, m_sc, l_sc, acc_sc, *, tq, tk):
    t = pl.program_id(1)
    i = qi_ref[t]
    j = kj_ref[t]

    @pl.when(j == 0)
    def _():
        for p in range(N_PAIRS):
            q = q_ref[p]
            for hh in range(2):
                qh_sc[2 * p + hh] = jnp.where(_aug_head_mask(hh, p), q, jnp.zeros_like(q))
        m_sc[...] = jnp.full_like(m_sc, NEG)
        l_sc[...] = jnp.zeros_like(l_sc)
        acc_sc[...] = jnp.zeros_like(acc_sc)

    def step(diagonal):
        if diagonal:
            keep = (lax.broadcasted_iota(jnp.int32, (tq, tk), 1)
                    <= lax.broadcasted_iota(jnp.int32, (tq, tk), 0))
        for p in range(N_PAIRS):
            k = k_ref[p]
            v = v_ref[:, p * LANES:(p + 1) * LANES]
            for hh in range(2):
                s = _dot_nt(qh_sc[2 * p + hh], k)
                if diagonal:
                    s = jnp.where(keep, s, NEG)
                _flash_update(s, v, m_sc, l_sc, acc_sc, 2 * p + hh)

    @pl.when(j < i)
    def _():
        step(False)

    @pl.when(j == i)
    def _():
        step(True)
        for p in range(N_PAIRS):
            o_ref[:, p * LANES:(p + 1) * LANES] = _flash_result(l_sc, acc_sc, p).astype(BF16)


def _fox_prompt(qcat, kcat, vb, n_seq):
    seq = qcat.shape[2]
    tq = tk = 1024
    nt = seq // tq
    qi = np.concatenate([np.full(i + 1, i) for i in range(nt)]).astype(np.int32)
    kj = np.concatenate([np.arange(i + 1) for i in range(nt)]).astype(np.int32)
    grid_spec = pltpu.PrefetchScalarGridSpec(
        num_scalar_prefetch=2,
        grid=(n_seq, len(qi)),
        in_specs=[
            pl.BlockSpec((None, N_PAIRS, tq, 2 * LANES), lambda b, t, qi, kj: (b, 0, qi[t], 0)),
            pl.BlockSpec((None, N_PAIRS, tk, 2 * LANES), lambda b, t, qi, kj: (b, 0, kj[t], 0)),
            pl.BlockSpec((tk, WIDTH), lambda b, t, qi, kj: (b * nt + kj[t], 0)),
        ],
        out_specs=pl.BlockSpec((tq, WIDTH), lambda b, t, qi, kj: (b * nt + qi[t], 0)),
        scratch_shapes=[pltpu.VMEM((N_HEADS, tq, 2 * LANES), BF16), pltpu.VMEM((N_HEADS, tq, LANES), F32),
                        pltpu.VMEM((N_HEADS, tq, LANES), F32), pltpu.VMEM((N_HEADS, tq, LANES), F32)],
    )
    return pl.pallas_call(
        functools.partial(_fox_prompt_kernel, tq=tq, tk=tk),
        grid_spec=grid_spec,
        out_shape=jax.ShapeDtypeStruct(vb.shape, BF16),
        compiler_params=_params(2),
        name="fox_prompt",
    )(jnp.asarray(qi), jnp.asarray(kj), qcat, kcat, vb)


def _suffix_sum_exclusive(x):
    n = x.shape[1]
    lane = lax.broadcasted_iota(jnp.int32, x.shape, 1)
    y = jnp.where(lane + 1 < n, pltpu.roll(x, n - 1, axis=1), 0.0)
    shift = 1
    while shift < n:
        y = y + jnp.where(lane + shift < n, pltpu.roll(y, n - shift, axis=1), 0.0)
        shift *= 2
    return y


def _fox_sample_kernel(q_ref, kc_ref, vc_ref, kn_ref, vn_ref, lfc_ref, lfn_ref, u_ref,
                       o_ref, cq_sc, car_sc, m_sc, l_sc, acc_sc, *, n_cache_tiles, sub_keys):
    j = pl.program_id(1)
    t = q_ref.shape[1]

    def cum_new():
        hi, mid, lo = _split3(lfn_ref[...])
        u = u_ref[...]
        return _dot(hi, u) + _dot(mid, u) + _dot(lo, u)

    def q_head(h):
        return q_ref[h // 2, :, (h % 2) * HEAD_DIM:(h % 2 + 1) * HEAD_DIM]

    @pl.when(j == 0)
    def _():
        cn = cum_new() * LOG2E
        eye = (lax.broadcasted_iota(jnp.int32, (t, LANES), 0)
               == lax.broadcasted_iota(jnp.int32, (t, LANES), 1))
        for h in range(N_HEADS):
            col = jnp.sum(jnp.where(eye, jnp.broadcast_to(cn[h:h + 1, :], (t, LANES)), 0.0),
                          axis=-1, keepdims=True)
            cq_sc[h] = jnp.broadcast_to(col, (t, LANES))
        car_sc[...] = jnp.zeros_like(car_sc)
        m_sc[...] = jnp.full_like(m_sc, NEG)
        l_sc[...] = jnp.zeros_like(l_sc)
        acc_sc[...] = jnp.zeros_like(acc_sc)

    lf = lfc_ref[...]
    r = _suffix_sum_exclusive(lf) + car_sc[:, 0:1]
    car_sc[...] = jnp.broadcast_to(r[:, 0:1] + lf[:, 0:1], car_sc.shape)
    r = r * LOG2E
    for sub in range(lf.shape[1] // sub_keys):
        keys = slice(sub * sub_keys, (sub + 1) * sub_keys)
        for h in range(N_HEADS):
            s = _dot(q_head(h), kc_ref[h, :, keys].astype(BF16)) + r[h:h + 1, keys]
            _flash_update(s, vc_ref[h, :, keys].astype(BF16), m_sc, l_sc, acc_sc, h,
                          row_bias=cq_sc[h], v_seq_minor=True)

    @pl.when(j == n_cache_tiles - 1)
    def _():
        cn = cum_new() * LOG2E
        causal = (lax.broadcasted_iota(jnp.int32, (t, LANES), 1)
                  <= lax.broadcasted_iota(jnp.int32, (t, LANES), 0))
        pad = jnp.zeros((LANES - t, HEAD_DIM), BF16)
        for h in range(N_HEADS):
            cols = slice(h * HEAD_DIM, (h + 1) * HEAD_DIM)
            k = jnp.concatenate([kn_ref[h // 2, :, (h % 2) * HEAD_DIM:(h % 2 + 1) * HEAD_DIM], pad], axis=0)
            v = jnp.concatenate([vn_ref[:, cols], pad], axis=0)
            s = _dot_nt(q_head(h), k) + cq_sc[h] - cn[h:h + 1, :]
            _flash_update(jnp.where(causal, s, NEG), v, m_sc, l_sc, acc_sc, h)
            o_ref[:, cols] = (acc_sc[h] / jnp.sum(l_sc[h], axis=-1, keepdims=True)).astype(BF16)


def _fox_sample(qcat, kcat, vb, cache_kt, cache_vt, lf_cache_t, lf_new_t):
    n, _, _, p_rows = cache_kt.shape
    t = vb.shape[0] // n
    tk = 4096
    nct = p_rows // tk
    u = jnp.asarray(np.pad(np.triu(np.ones((t, t))), ((0, 0), (0, LANES - t))), BF16)
    rev = lambda j: nct - 1 - j
    cache = pl.BlockSpec((None, N_HEADS, HEAD_DIM, tk), lambda b, j: (b, 0, 0, rev(j)))
    new_cat = pl.BlockSpec((None, N_PAIRS, t, 2 * LANES), lambda b, j: (0, 0, b, 0))
    return pl.pallas_call(
        functools.partial(_fox_sample_kernel, n_cache_tiles=nct, sub_keys=tk),
        grid=(n, nct),
        in_specs=[
            new_cat, cache, cache, new_cat,
            pl.BlockSpec((t, WIDTH), lambda b, j: (b, 0)),
            pl.BlockSpec((None, N_HEADS, tk), lambda b, j: (b, 0, rev(j))),
            pl.BlockSpec((None, N_HEADS, t), lambda b, j: (b, 0, 0)),
            _const_spec(u.shape),
        ],
        out_specs=pl.BlockSpec((t, WIDTH), lambda b, j: (b, 0)),
        out_shape=jax.ShapeDtypeStruct(vb.shape, BF16),
        scratch_shapes=[pltpu.VMEM((N_HEADS, t, LANES), F32), pltpu.VMEM((N_HEADS, LANES), F32),
                        pltpu.VMEM((N_HEADS, t, LANES), F32), pltpu.VMEM((N_HEADS, t, LANES), F32),
                        pltpu.VMEM((N_HEADS, t, HEAD_DIM), F32)],
        compiler_params=_params(2),
        name="fox_sample",
    )(qcat, cache_kt, cache_vt, kcat, vb, lf_cache_t, lf_new_t, u)


GROUP_LANE = 64
ROW_ALIGN = 16
MOE_TILE = 512
SORT_ROWS = 640
XS_WIDTH = D_MODEL + 2 * LANES


def _route(r):
    lane_i = lax.broadcasted_iota(jnp.int32, r.shape, 1)
    lane = lane_i.astype(F32)
    lane_grp = (lane_i // EXPERTS_PER_GROUP).astype(F32)
    big = float(LANES)
    is_coarse = (lane_i >= N_EXPERTS) & (lane_i < N_EXPERTS + N_GROUPS)
    cm = jnp.where(is_coarse, r, NEG)
    cmax = cm.max(axis=-1, keepdims=True)
    grp = jnp.min(jnp.where(cm == cmax, lane - N_EXPERTS, big), axis=-1, keepdims=True)
    pg_sel = 1.0 / jnp.sum(jnp.exp(cm - cmax), axis=-1, keepdims=True)
    in_grp = (lane_i < N_EXPERTS) & (lane_grp == grp)
    fm = jnp.where(in_grp, r, NEG)
    m1 = fm.max(axis=-1, keepdims=True)
    denom = jnp.sum(jnp.exp(fm - m1), axis=-1, keepdims=True)
    i1 = jnp.min(jnp.where(fm == m1, lane, big), axis=-1, keepdims=True)
    fm2 = jnp.where(lane == i1, NEG, fm)
    m2 = fm2.max(axis=-1, keepdims=True)
    i2 = jnp.min(jnp.where(fm2 == m2, lane, big), axis=-1, keepdims=True)
    p1 = 1.0 / denom
    p2 = jnp.exp(m2 - m1) / denom
    tot = p1 + p2
    comb = (jnp.where(lane == i1, pg_sel * (p1 / tot), 0.0)
            + jnp.where(lane == i2, pg_sel * (p2 / tot), 0.0))
    return comb, grp


def _merge_kernel(x1_ref, oa1_ref, ob1_ref, x2_ref, oa2_ref, ob2_ref, g_ref, wg_ref, wpa_ref, wpb_ref, wo_ref,
                  gf_ref, wrh_ref, wrm_ref, br_ref, y_ref, hx_ref, route_ref, grp_t_ref, cnt_ref,
                  x_sc, oa_sc, ob_sc, *, tiles_first):
    i = pl.program_id(0)

    @pl.when(i < tiles_first)
    def _():
        x_sc[...], oa_sc[...], ob_sc[...] = x1_ref[...], oa1_ref[...], ob1_ref[...]

    @pl.when(i >= tiles_first)
    def _():
        x_sc[...], oa_sc[...], ob_sc[...] = x2_ref[...], oa2_ref[...], ob2_ref[...]

    x = x_sc[...]
    h = _rms(x, g_ref[...]).astype(BF16)
    gate = jax.nn.sigmoid(_dot(h, wg_ref[...]))
    mix = (gate[:, :D_MODEL] * _dot(oa_sc[...], wpa_ref[...])
           + gate[:, D_MODEL:] * _dot(ob_sc[...], wpb_ref[...]))
    y = x + _dot(mix.astype(BF16), wo_ref[...])
    y_ref[...] = y

    hx = _rms(y, gf_ref[...])
    hx_ref[:, 0:D_MODEL] = hx.astype(BF16)
    h_hi, h_mid, _ = _split3(hx)
    r = _dot(h_hi, wrh_ref[...]) + _dot(h_hi, wrm_ref[...]) + _dot(h_mid, wrh_ref[...]) + br_ref[...]
    comb, grp = _route(r)
    lane = lax.broadcasted_iota(jnp.int32, comb.shape, 1)
    route = jnp.where(lane == GROUP_LANE, grp, comb)
    route_ref[...] = route
    r_hi = route.astype(BF16)
    hx_ref[:, D_MODEL:D_MODEL + LANES] = r_hi
    hx_ref[:, D_MODEL + LANES:] = (route - r_hi.astype(F32)).astype(BF16)
    grp_t_ref[...] = route.T[GROUP_LANE:GROUP_LANE + 8, :]
    cnt = jnp.sum(jnp.where(lane.astype(F32) == grp, 1.0, 0.0), axis=0, keepdims=True)
    cnt_ref[...] = jnp.broadcast_to(cnt, cnt_ref.shape)


def _two_stream_maps(tiles_first):
    first = lambda i, *_: (jnp.minimum(i, tiles_first - 1), 0)
    second = lambda i, *_: (jnp.maximum(i - tiles_first, 0), 0)
    return first, second


def _merge(x1, oa1, ob1, x2, oa2, ob2, w):
    tm = MOE_TILE
    tiles_first = x1.shape[0] // tm
    t_total = x1.shape[0] + x2.shape[0]
    n_tiles = t_total // tm
    first, second = _two_stream_maps(tiles_first)
    row = lambda n: pl.BlockSpec((tm, n), lambda i: (i, 0))
    stream = lambda m: [pl.BlockSpec((tm, D_MODEL), m), pl.BlockSpec((tm, WIDTH), m), pl.BlockSpec((tm, WIDTH), m)]
    consts = [w["g_mix"], w["w_gate"], w["w_pa"], w["w_pb"], w["w_o"], w["g_ffn"], w["w_router_hi"],
              w["w_router_mid"], w["b_router"]]
    return pl.pallas_call(
        functools.partial(_merge_kernel, tiles_first=tiles_first),
        grid=(n_tiles,),
        in_specs=stream(first) + stream(second) + [_const_spec(c.shape) for c in consts],
        out_specs=[row(D_MODEL), row(XS_WIDTH), row(LANES), pl.BlockSpec((8, tm), lambda i: (0, i)),
                   pl.BlockSpec((None, 8, LANES), lambda i: (i, 0, 0))],
        out_shape=[jax.ShapeDtypeStruct((t_total, D_MODEL), F32), jax.ShapeDtypeStruct((t_total, XS_WIDTH), BF16),
                   jax.ShapeDtypeStruct((t_total, LANES), F32), jax.ShapeDtypeStruct((8, t_total), F32),
                   jax.ShapeDtypeStruct((n_tiles, 8, LANES), F32)],
        scratch_shapes=[pltpu.VMEM((tm, D_MODEL), F32), pltpu.VMEM((tm, WIDTH), BF16),
                        pltpu.VMEM((tm, WIDTH), BF16)],
        compiler_params=_params(1),
        name="merge",
    )(x1, oa1, ob1, x2, oa2, ob2, *consts)


def _plan_kernel(cnt_ref, ib_ref, dst_ref, len_ref, fill_ref, inblk_ref, outblk_ref, grp_ref, kind_ref, *,
                 n_tiles, n_steps):
    align_bits = ROW_ALIGN.bit_length() - 1
    tile_bits = MOE_TILE.bit_length() - 1
    zero = jnp.int32(0)

    def per_tile(t, tot):
        run = zero
        new_tot = []
        for g in range(N_GROUPS):
            n = ((cnt_ref[t * N_GROUPS + g] + (ROW_ALIGN - 1)) >> align_bits) << align_bits
            ib_ref[t * N_GROUPS + g] = run
            len_ref[t * N_GROUPS + g] = n
            dst_ref[t * N_GROUPS + g] = tot[g]
            run = run + n
            new_tot.append(tot[g] + n)
        return tuple(new_tot)

    tot = lax.fori_loop(0, n_tiles, per_tile, (zero,) * N_GROUPS)
    n_ffn = [(tot[g] + (MOE_TILE - 1)) >> tile_bits for g in range(N_GROUPS)]
    g_base, ends, acc_rows, acc_tiles = [], [], zero, zero
    for g in range(N_GROUPS):
        g_base.append(acc_rows)
        acc_rows = acc_rows + ((n_ffn[g] + 1) << tile_bits)
        acc_tiles = acc_tiles + n_ffn[g] + 1
        ends.append(acc_tiles)

    def add_base(t, carry):
        for g in range(N_GROUPS):
            dst_ref[t * N_GROUPS + g] = dst_ref[t * N_GROUPS + g] + g_base[g]
        return carry

    lax.fori_loop(0, n_tiles, add_base, zero)
    for g in range(N_GROUPS):
        fill_ref[g] = g_base[g] + tot[g]
        fill_ref[N_GROUPS + g] = g_base[g] + (n_ffn[g] << tile_bits)
    fill_ref[2 * N_GROUPS] = ends[-1]
    first_blk = g_base[N_GROUPS - 1]
    for g in reversed(range(N_GROUPS - 1)):
        first_blk = jnp.where(n_ffn[g] > 0, g_base[g], first_blk)
    first_blk = first_blk >> tile_bits

    def pick(vals, g):
        out = vals[N_GROUPS - 1]
        for i in reversed(range(N_GROUPS - 1)):
            out = jnp.where(g == i, vals[i], out)
        return out

    def per_step(k, carry):
        kk = jnp.minimum(k, ends[-1] - 1)
        g = zero
        for i in range(N_GROUPS - 1):
            g = g + jnp.where(kk >= ends[i], 1, 0)
        j = kk - pick([zero] + ends[:-1], g)
        is_expert = j < pick(n_ffn, g)
        out_blk = jnp.where(k < ends[-1], (pick(g_base, g) >> tile_bits) + j, k)
        outblk_ref[k] = out_blk
        inblk_ref[k] = jnp.where(is_expert, out_blk, first_blk)
        grp_ref[k] = g
        kind_ref[k] = jnp.where(is_expert, 1, 0)
        return carry

    lax.fori_loop(0, n_steps, per_step, zero)


def _moe_plan(cnt, t_total):
    n_tiles = t_total // MOE_TILE
    n_steps = (n_tiles + 2 * N_GROUPS + (N_GROUPS * (ROW_ALIGN - 1) * n_tiles + MOE_TILE - 1) // MOE_TILE)
    smem = pl.BlockSpec(memory_space=pltpu.SMEM)
    sizes = [n_tiles * N_GROUPS] * 3 + [2 * N_GROUPS + 1] + [n_steps] * 4
    outs = pl.pallas_call(
        functools.partial(_plan_kernel, n_tiles=n_tiles, n_steps=n_steps),
        grid_spec=pltpu.PrefetchScalarGridSpec(num_scalar_prefetch=1, grid=(1,), in_specs=[],
                                               out_specs=[smem] * len(sizes)),
        out_shape=[jax.ShapeDtypeStruct((n,), jnp.int32) for n in sizes],
        compiler_params=_params(1),
        name="moe_plan",
    )(cnt.reshape(-1))
    return (*outs, n_steps)


def _sort_kernel(ib_ref, dst_ref, len_ref, fill_ref, hx_ref, grp_t_ref, tri_ref, xs_ref, xsort, sems):
    i = pl.program_id(0)
    n = pl.num_programs(0)
    tm = hx_ref.shape[0]
    half = tm // 2
    slot = i % 2

    def copy(src, dst, rows, sl, g, part):
        return pltpu.make_async_copy(xsort.at[sl, pl.ds(pl.multiple_of(src, ROW_ALIGN), rows)],
                                     xs_ref.at[pl.ds(pl.multiple_of(dst, ROW_ALIGN), rows)], sems.at[sl, g, part])

    def for_segments(step, sl, act):
        for g in range(N_GROUPS):
            src, dst = ib_ref[step * N_GROUPS + g], dst_ref[step * N_GROUPS + g]
            act(copy(src, dst, half, sl, g, 0))

            @pl.when(len_ref[step * N_GROUPS + g] > half)
            def _():
                act(copy(src + half, dst + half, half, sl, g, 1))

    def run(cs):
        for c in cs:
            c.start()
        for c in cs:
            c.wait()

    @pl.when(i == 0)
    def _():
        for sl in range(2):
            xsort[sl, SORT_ROWS:, :] = jnp.zeros((xsort.shape[1] - SORT_ROWS, XS_WIDTH), BF16)

    g_row = grp_t_ref[0:1, :]
    sub = lax.broadcasted_iota(jnp.int32, (8, tm), 0).astype(F32)
    mine = sub == g_row
    before = _dot(jnp.where(mine, 1.0, 0.0).astype(BF16), tri_ref[...])
    dest = jnp.sum(jnp.where(mine, before, 0.0), axis=0, keepdims=True)
    for g in range(N_GROUPS):
        base = jnp.full(dest.shape, ib_ref[i * N_GROUPS + g], jnp.int32).astype(F32)
        dest = dest + jnp.where(g_row == g, base, 0.0)
    rows = lax.broadcasted_iota(jnp.int32, (SORT_ROWS, tm), 0).astype(F32)
    perm = jnp.where(rows == dest, 1.0, 0.0).astype(BF16)
    xsort[slot, 0:SORT_ROWS, :] = _dot(perm, hx_ref[...]).astype(BF16)

    @pl.when(i > 0)
    def _():
        for_segments(i - 1, 1 - slot, lambda c: c.wait())

    for_segments(i, slot, lambda c: c.start())

    @pl.when(i == n - 1)
    def _():
        for_segments(i, slot, lambda c: c.wait())
        run([copy(SORT_ROWS, fill_ref[g], tm, slot, g, 0) for g in range(N_GROUPS)])
        run([copy(SORT_ROWS, fill_ref[N_GROUPS + g], tm, slot, g, 0) for g in range(N_GROUPS)])

        @pl.loop(fill_ref[2 * N_GROUPS], xs_ref.shape[0] // tm)
        def _(b):
            run([copy(SORT_ROWS, b * tm, tm, slot, 0, 0)])


def _expert_kernel(ib_ref, ob_ref, grp_ref, kind_ref, xs_ref, w1_ref, w3_ref, w2_ref, ys_ref):
    del ib_ref, ob_ref
    k = pl.program_id(0)

    @pl.when(kind_ref[k] == 1)
    def _():
        x = xs_ref[:, 0:D_MODEL]
        side = lambda w_ref: jnp.concatenate([w_ref[e].astype(BF16) for e in range(EXPERTS_PER_GROUP)], axis=1)
        hid = jax.nn.silu(_dot(x, side(w1_ref))) * _dot(x, side(w3_ref))
        comb = (xs_ref[:, D_MODEL:D_MODEL + LANES].astype(F32) + xs_ref[:, D_MODEL + LANES:].astype(F32))
        lane = lax.broadcasted_iota(jnp.int32, comb.shape, 1)
        first = grp_ref[k] * EXPERTS_PER_GROUP
        scale = []
        for e in range(EXPERTS_PER_GROUP):
            w_e = jnp.sum(jnp.where(lane == first + e, comb, 0.0), axis=-1, keepdims=True)
            scale.append(jnp.broadcast_to(w_e, (comb.shape[0], D_EXPERT)))
        hid = hid * jnp.concatenate(scale, axis=-1)
        w2 = w2_ref[...].astype(BF16).reshape(EXPERTS_PER_GROUP * D_EXPERT, D_MODEL)
        ys_ref[...] = _dot(hid.astype(BF16), w2).astype(BF16)

    @pl.when(kind_ref[k] == 0)
    def _():
        ys_ref[...] = jnp.zeros_like(ys_ref)


def _unsort_kernel(ib_ref, dst_ref, len_ref, y_ref, route_ref, tri_ref, ys_ref, o1_ref, o2_ref, ybuf, yasm, sems,
                   *, tiles_first):
    i = pl.program_id(0)
    n = pl.num_programs(0)
    tm = y_ref.shape[0]
    half = tm // 2
    slot = i % 2

    def fetch(step, sl, act):
        for g in range(N_GROUPS):
            src = pl.multiple_of(dst_ref[step * N_GROUPS + g], ROW_ALIGN)
            act(pltpu.make_async_copy(ys_ref.at[pl.ds(src, half)], ybuf.at[sl, g, 0:half], sems.at[sl, g, 0]))

            @pl.when(len_ref[step * N_GROUPS + g] > half)
            def _():
                act(pltpu.make_async_copy(ys_ref.at[pl.ds(src + half, half)], ybuf.at[sl, g, half:tm],
                                          sems.at[sl, g, 1]))

    @pl.when(i == 0)
    def _():
        yasm[...] = jnp.zeros_like(yasm)
        fetch(0, 0, lambda c: c.start())

    @pl.when(i + 1 < n)
    def _():
        fetch(i + 1, 1 - slot, lambda c: c.start())

    route = route_ref[...]
    lane = lax.broadcasted_iota(jnp.int32, route.shape, 1).astype(F32)
    grp = jnp.sum(jnp.where(lane == GROUP_LANE, route, 0.0), axis=-1, keepdims=True)
    mine = lane == grp
    before = _dot(tri_ref[...], jnp.where(mine, 1.0, 0.0).astype(BF16))
    dest = jnp.sum(jnp.where(mine, before, 0.0), axis=-1, keepdims=True)
    for g in range(N_GROUPS):
        base = jnp.full(dest.shape, ib_ref[i * N_GROUPS + g], jnp.int32).astype(F32)
        dest = dest + jnp.where(grp == g, base, 0.0)
    cols = lax.broadcasted_iota(jnp.int32, (tm, SORT_ROWS), 1).astype(F32)
    perm_t = jnp.where(cols == dest, 1.0, 0.0).astype(BF16)

    fetch(i, slot, lambda c: c.wait())
    for g in range(N_GROUPS):
        row0 = pl.multiple_of(ib_ref[i * N_GROUPS + g], ROW_ALIGN)
        yasm[pl.ds(row0, half), :] = ybuf[slot, g, 0:half]

        @pl.when(len_ref[i * N_GROUPS + g] > half)
        def _():
            yasm[pl.ds(row0 + half, half), :] = ybuf[slot, g, half:tm]
    out = y_ref[...] + _dot(perm_t, yasm[0:SORT_ROWS, :])

    @pl.when(i < tiles_first)
    def _():
        o1_ref[...] = out

    @pl.when(i >= tiles_first)
    def _():
        o2_ref[...] = out


def _moe(y, hx, route, grp_t, cnt, w, rows_first):
    t_total = y.shape[0]
    tm = MOE_TILE
    n_tiles = t_total // tm
    assert SORT_ROWS >= tm + N_GROUPS * (ROW_ALIGN - 1)
    in_base, dst, seg_len, fill, in_blk, out_blk, grp_of_step, is_expert, n_steps = _moe_plan(
        cnt[:, 0, :N_GROUPS].astype(jnp.int32), t_total)
    cap_rows = n_steps * tm
    lower = jnp.asarray(np.tril(np.ones((tm, tm)), -1), BF16)
    upper = jnp.asarray(np.triu(np.ones((tm, tm)), 1), BF16)
    stage_rows = SORT_ROWS + tm
    any_spec = pl.BlockSpec(memory_space=pl.ANY)

    xs = pl.pallas_call(
        _sort_kernel,
        grid_spec=pltpu.PrefetchScalarGridSpec(
            num_scalar_prefetch=4,
            grid=(n_tiles,),
            in_specs=[pl.BlockSpec((tm, XS_WIDTH), lambda i, *_: (i, 0)),
                      pl.BlockSpec((8, tm), lambda i, *_: (0, i)),
                      pl.BlockSpec((tm, tm), lambda i, *_: (0, 0))],
            out_specs=any_spec,
            scratch_shapes=[pltpu.VMEM((2, stage_rows, XS_WIDTH), BF16),
                            pltpu.SemaphoreType.DMA((2, N_GROUPS, 2))],
        ),
        out_shape=jax.ShapeDtypeStruct((cap_rows, XS_WIDTH), BF16),
        compiler_params=_params(1),
        name="moe_sort",
    )(in_base, dst, seg_len, fill, hx, grp_t, upper)

    step_map = lambda which: (lambda k, ib, ob, grp, kind: ((ib, ob, grp)[which][k], 0))
    wmap = lambda k, ib, ob, grp, kind: (grp[k], 0, 0)
    wspec = pl.BlockSpec((EXPERTS_PER_GROUP, D_MODEL, D_EXPERT), wmap)
    ys = pl.pallas_call(
        _expert_kernel,
        grid_spec=pltpu.PrefetchScalarGridSpec(
            num_scalar_prefetch=4,
            grid=(n_steps,),
            in_specs=[pl.BlockSpec((tm, XS_WIDTH), step_map(0)), wspec, wspec, pl.BlockSpec((EXPERTS_PER_GROUP, D_EXPERT, D_MODEL), wmap)],
            out_specs=pl.BlockSpec((tm, D_MODEL), step_map(1)),
        ),
        out_shape=jax.ShapeDtypeStruct((cap_rows, D_MODEL), BF16),
        compiler_params=_params(1),
        name="moe_experts",
    )(in_blk, out_blk, grp_of_step, is_expert, xs, w["w1"], w["w3"], w["w2"])

    first, second = _two_stream_maps(rows_first // tm)
    return pl.pallas_call(
        functools.partial(_unsort_kernel, tiles_first=rows_first // tm),
        grid_spec=pltpu.PrefetchScalarGridSpec(
            num_scalar_prefetch=3,
            grid=(n_tiles,),
            in_specs=[pl.BlockSpec((tm, D_MODEL), lambda i, *_: (i, 0)),
                      pl.BlockSpec((tm, LANES), lambda i, *_: (i, 0)),
                      pl.BlockSpec((tm, tm), lambda i, *_: (0, 0)),
                      any_spec],
            out_specs=[pl.BlockSpec((tm, D_MODEL), first), pl.BlockSpec((tm, D_MODEL), second)],
            scratch_shapes=[pltpu.VMEM((2, N_GROUPS, tm, D_MODEL), BF16),
                            pltpu.VMEM((stage_rows, D_MODEL), BF16),
                            pltpu.SemaphoreType.DMA((2, N_GROUPS, 2))],
        ),
        out_shape=[jax.ShapeDtypeStruct((rows_first, D_MODEL), F32),
                   jax.ShapeDtypeStruct((t_total - rows_first, D_MODEL), F32)],
        compiler_params=_params(1),
        name="moe_unsort",
    )(in_base, dst, seg_len, y, route, lower, ys)


def _prep_weights(g_mix, w_in, b_f, q_norm_a, k_norm_a, q_norm_b, k_norm_b, w_pa, w_pb, w_o,
                  g_ffn, w_rg, b_rg, w_re, b_re, w1, w3, w2):
    n_qkv = 6 * WIDTH
    tile = lambda g: jnp.tile(g, N_HEADS)
    w_router = jnp.concatenate(
        [jnp.transpose(w_re, (1, 0, 2)).reshape(D_MODEL, N_EXPERTS), w_rg,
         jnp.zeros((D_MODEL, LANES - N_EXPERTS - N_GROUPS), F32)], axis=1)
    b_router = jnp.concatenate(
        [b_re.reshape(N_EXPERTS), b_rg, jnp.zeros((LANES - N_EXPERTS - N_GROUPS,), F32)])[None, :]
    return {
        "g_mix": g_mix[None, :],
        "w_qkv": w_in[:, :n_qkv].astype(BF16),
        "w_f": jnp.pad(w_in[:, n_qkv:n_qkv + N_HEADS], ((0, 0), (0, LANES - N_HEADS))).astype(BF16),
        "b_f": jnp.pad(b_f, (0, LANES - N_HEADS))[None, :],
        "w_gate": w_in[:, n_qkv + N_HEADS:].astype(BF16),
        "gains": jnp.stack([tile(q_norm_a), tile(k_norm_a), tile(q_norm_b), tile(k_norm_b)]),
        "w_pa": w_pa.astype(BF16), "w_pb": w_pb.astype(BF16), "w_o": w_o.astype(BF16),
        "g_ffn": g_ffn[None, :],
        "w_router_hi": w_router.astype(BF16),
        "w_router_mid": (w_router - w_router.astype(BF16).astype(F32)).astype(BF16),
        "b_router": b_router,
        "w1": w1, "w3": w3, "w2": w2,
    }


def kernel(x_prompt, x_sample, cache_a_k, cache_a_v, cache_b_k, cache_b_v, cache_b_logf, g_mix, w_in, b_f, q_norm_a, k_norm_a, q_norm_b, k_norm_b, rel_bias, w_pa, w_pb, w_o, g_ffn, w_rg, b_rg, w_re, b_re, w1, w3, w2):
    assert g_mix.shape[0] == 1, "single-layer step"
    n_p, seq, _ = x_prompt.shape
    n_s, t_s, _ = x_sample.shape
    a_rows = cache_a_k.shape[2]
    w = _prep_weights(g_mix[0], w_in[0], b_f[0], q_norm_a[0], k_norm_a[0], q_norm_b[0], k_norm_b[0],
                      w_pa[0], w_pb[0], w_o[0], g_ffn[0], w_rg[0], b_rg[0], w_re[0], b_re[0],
                      w1[0], w3[0], w2[0])
    band_tq = 256
    bias_prompt = _band_bias(rel_bias[0], band_tq, 3 * band_tq, 2 * band_tq, True, 3 * band_tq)
    bias_sample = _band_bias(rel_bias[0], t_s, a_rows + LANES, a_rows, False, a_rows + t_s)

    seq_minor = lambda a: jnp.transpose(a, (0, 2, 3, 1))
    seq_major = lambda a: jnp.transpose(a, (0, 3, 1, 2))[None]

    xp = x_prompt.reshape(n_p * seq, D_MODEL)
    (qa, ka, va, ka_t, va_t, qcat, kcat, vb, kb_t, vb_t, logf_t) = _inproj(xp, n_p, w, seq_minor=True)
    o_a = _band_prompt(qa, ka, va, *bias_prompt, n_p)
    o_b = _fox_prompt(qcat, kcat, vb, n_p)

    xs = x_sample.reshape(n_s * t_s, D_MODEL)
    (qa_s, _, _, ka_fs, va_fs, qcat_s, kcat_s, vb_s, kb_fs, vb_fs, logf_s) = _inproj(
        xs, 1, w, seq_minor=False)
    o_as, new_ak_t, new_av_t = _band_sample(
        qa_s, ka_fs, va_fs, seq_minor(cache_a_k[0]), seq_minor(cache_a_v[0]), *bias_sample)
    lf_cache_t = jnp.transpose(cache_b_logf[0], (0, 2, 1))
    lf_new_t = jnp.transpose(logf_s.reshape(n_s, t_s, N_HEADS), (0, 2, 1))
    o_bs = _fox_sample(qcat_s, kcat_s, vb_s, seq_minor(cache_b_k[0]), seq_minor(cache_b_v[0]),
                       lf_cache_t, lf_new_t)

    y_p, y_s = _moe(*_merge(xp, o_a, o_b, xs, o_as, o_bs, w), w, rows_first=n_p * seq)

    heads = lambda a, n, r: a.reshape(1, n, r, N_HEADS, HEAD_DIM)
    return (y_p.reshape(n_p, seq, D_MODEL), y_s.reshape(n_s, t_s, D_MODEL),
            seq_major(ka_t), seq_major(va_t), seq_major(kb_t), seq_major(vb_t),
            jnp.transpose(logf_t, (0, 2, 1))[None],
            seq_major(new_ak_t), seq_major(new_av_t),
            heads(kb_fs, n_s, t_s), heads(vb_fs, n_s, t_s), logf_s.reshape(1, n_s, t_s, N_HEADS))
```

```python
import functools

import numpy as np
import jax
import jax.numpy as jnp
from jax import lax
from jax.experimental import pallas as pl
from jax.experimental.pallas import tpu as pltpu

F32 = jnp.float32
BF16 = jnp.bfloat16

D_MODEL = 1024
HEAD_DIM = 64
N_HEADS = 8
WIDTH = N_HEADS * HEAD_DIM
N_PAIRS = N_HEADS // 2
CHUNK = 64
LEFT_CHUNKS = 8
WINDOW_ROWS = LEFT_CHUNKS * CHUNK
MAX_REL = 256
N_GROUPS = 4
EXPERTS_PER_GROUP = 8
N_EXPERTS = N_GROUPS * EXPERTS_PER_GROUP
D_EXPERT = 128
EPS = 1e-6
NEG = -1e30
LOG2E = 1.4426950408889634
LANES = 128
BIAS_PERIOD = 1024
AUG_STRIDE = 8
VMEM_LIMIT = 56 * 1024 * 1024

_NT = (((1,), (1,)), ((), ()))


def _dot(a, b):
    return jnp.dot(a, b, preferred_element_type=F32)


def _dot_nt(a, b):
    return lax.dot_general(a, b, _NT, preferred_element_type=F32)


def _split3(x):
    hi = x.astype(BF16)
    r = x - hi.astype(F32)
    mid = r.astype(BF16)
    lo = (r - mid.astype(F32)).astype(BF16)
    return hi, mid, lo


def _dot3(a_bf, x):
    hi, mid, lo = _split3(x)
    return _dot(a_bf, hi) + _dot(a_bf, mid) + _dot(a_bf, lo)


def _rms(x, g):
    ms = jnp.mean(x * x, axis=-1, keepdims=True)
    return x * lax.rsqrt(ms + EPS) * g


def _params(n_axes):
    return pltpu.CompilerParams(dimension_semantics=("arbitrary",) * n_axes,
                                vmem_limit_bytes=VMEM_LIMIT)


def _const_spec(shape):
    nd = len(shape)
    return pl.BlockSpec(shape, lambda *_: (0,) * nd)


def _inproj_kernel(x_ref, g_ref, wqkv_ref, wf_ref, bf_ref, gains_ref, bd_ref, ltri_ref,
                   selq_ref, selk_ref, oneq_ref, onek_ref,
                   qa_ref, ka_ref, va_ref, kaf_ref, vaf_ref, qcat_ref, kcat_ref, vb_ref,
                   kbf_ref, vbf_ref, logf_ref, carry_ref, *, tiles_per_seq, seq_minor):
    i = pl.program_id(0)
    tm = x_ref.shape[0]

    @pl.when(i % tiles_per_seq == 0)
    def _():
        carry_ref[...] = jnp.zeros_like(carry_ref)

    def put_f32(ref, y, last_tile_only=False):
        if not seq_minor:
            ref[...] = y
        elif last_tile_only:
            @pl.when(i % tiles_per_seq == tiles_per_seq - 1)
            def _():
                ref[...] = y.T.reshape(N_HEADS, HEAD_DIM, tm)
        else:
            ref[...] = y.T.reshape(N_HEADS, HEAD_DIM, tm)

    h = _rms(x_ref[...], g_ref[...]).astype(BF16)

    def seg(s):
        return _dot(h, wqkv_ref[:, s * WIDTH:(s + 1) * WIDTH])

    def headnorm(y, n):
        ss = _dot((y * y).astype(BF16), bd_ref[...])
        return y * lax.rsqrt(ss * (1.0 / HEAD_DIM) + EPS) * gains_ref[n:n + 1, :]

    q_a = headnorm(seg(0), 0)
    qa_ref[...] = (q_a * (HEAD_DIM ** -0.5 * LOG2E)).astype(BF16)
    k_a = headnorm(seg(1), 1)
    ka_ref[...] = k_a.astype(BF16)
    put_f32(kaf_ref, k_a, last_tile_only=True)
    v_a = seg(2)
    va_ref[...] = v_a.astype(BF16)
    put_f32(vaf_ref, v_a, last_tile_only=True)

    z = _dot(h, wf_ref[...]) + bf_ref[...]
    logf = jnp.minimum(z, 0.0) - jnp.log(1.0 + jnp.exp(-jnp.abs(z)))
    if seq_minor:
        logf_ref[...] = logf.T[0:N_HEADS, :]
    else:
        logf_ref[...] = logf[:, :N_HEADS]
    c = _dot3(ltri_ref[...], logf) + carry_ref[0:1, :]
    carry_ref[...] = jnp.broadcast_to(c[-1:, :], carry_ref.shape)
    cs = jnp.concatenate(_split3(c * LOG2E), axis=-1)
    q_aug = (_dot(cs, selq_ref[...]) + oneq_ref[...]).astype(BF16)
    k_aug = (_dot(cs, selk_ref[...]) + onek_ref[...]).astype(BF16)

    q_b = (headnorm(seg(3), 2) * (HEAD_DIM ** -0.5 * LOG2E)).astype(BF16)
    k_b = headnorm(seg(4), 3)
    put_f32(kbf_ref, k_b)
    k_b = k_b.astype(BF16)
    v_b = seg(5)
    put_f32(vbf_ref, v_b)
    vb_ref[...] = v_b.astype(BF16)
    for p in range(N_PAIRS):
        cols = slice(p * LANES, (p + 1) * LANES)
        qcat_ref[p, :, 0:LANES] = q_b[:, cols]
        qcat_ref[p, :, LANES:2 * LANES] = q_aug
        kcat_ref[p, :, 0:LANES] = k_b[:, cols]
        kcat_ref[p, :, LANES:2 * LANES] = k_aug


def _aug_constants():
    selq = np.zeros((3 * LANES, LANES), np.float32)
    selk = np.zeros((3 * LANES, LANES), np.float32)
    oneq = np.zeros((1, LANES), np.float32)
    onek = np.zeros((1, LANES), np.float32)
    for h in range(N_HEADS):
        for k in range(3):
            selq[k * LANES + h, AUG_STRIDE * h + k] = 1.0
            selk[k * LANES + h, AUG_STRIDE * h + 3 + k] = -1.0
            oneq[0, AUG_STRIDE * h + 3 + k] = 1.0
            onek[0, AUG_STRIDE * h + k] = 1.0
    return (jnp.asarray(selq, BF16), jnp.asarray(selk, BF16), jnp.asarray(oneq), jnp.asarray(onek))


def _inproj(x, n_seq, w, seq_minor):
    t_total = x.shape[0]
    tm = 512
    n_tiles = t_total // tm
    seq = t_total // n_seq
    tps = seq // tm
    row = lambda i: (i, 0)
    flat = lambda dt: (jax.ShapeDtypeStruct((t_total, WIDTH), dt), pl.BlockSpec((tm, WIDTH), row))
    if seq_minor:
        assert tm == WINDOW_ROWS
        band_f32 = (jax.ShapeDtypeStruct((n_seq, N_HEADS, HEAD_DIM, WINDOW_ROWS), F32),
                    pl.BlockSpec((None, N_HEADS, HEAD_DIM, tm), lambda i: (i // tps, 0, 0, 0)))
        fox_f32 = (jax.ShapeDtypeStruct((n_seq, N_HEADS, HEAD_DIM, seq), F32),
                   pl.BlockSpec((None, N_HEADS, HEAD_DIM, tm), lambda i: (i // tps, 0, 0, i % tps)))
        logf = (jax.ShapeDtypeStruct((n_seq, N_HEADS, seq), F32),
                pl.BlockSpec((None, N_HEADS, tm), lambda i: (i // tps, 0, i % tps)))
    else:
        band_f32 = fox_f32 = flat(F32)
        logf = (jax.ShapeDtypeStruct((t_total, N_HEADS), F32), pl.BlockSpec((tm, N_HEADS), row))
    cat = (jax.ShapeDtypeStruct((n_seq, N_PAIRS, seq, 2 * LANES), BF16),
           pl.BlockSpec((None, N_PAIRS, tm, 2 * LANES), lambda i: (i // tps, 0, i % tps, 0)))
    outs = [flat(BF16), flat(BF16), flat(BF16), band_f32, band_f32, cat, cat, flat(BF16),
            fox_f32, fox_f32, logf]
    bd = jnp.asarray(np.kron(np.eye(N_HEADS), np.ones((HEAD_DIM, HEAD_DIM))), BF16)
    ltri = jnp.asarray(np.tril(np.ones((tm, tm))), BF16)
    consts = [w["g_mix"], w["w_qkv"], w["w_f"], w["b_f"], w["gains"], bd, ltri, *_aug_constants()]
    in_specs = [pl.BlockSpec((tm, D_MODEL), row)] + [_const_spec(c.shape) for c in consts]
    return pl.pallas_call(
        functools.partial(_inproj_kernel, tiles_per_seq=tps, seq_minor=seq_minor),
        grid=(n_tiles,),
        in_specs=in_specs,
        out_specs=[o[1] for o in outs],
        out_shape=[o[0] for o in outs],
        scratch_shapes=[pltpu.VMEM((8, LANES), F32)],
        compiler_params=_params(1),
        name="inproj",
    )(x, *consts)


def _softmax_pv(scores, values):
    chunks = [[s[:, c * LANES:(c + 1) * LANES] for c in range(s.shape[1] // LANES)] for s in scores]
    cmax = None
    for cs in chunks:
        for c in cs:
            cmax = c if cmax is None else jnp.maximum(cmax, c)
    m = jnp.max(cmax, axis=-1, keepdims=True)
    pv = None
    lsum = None
    for cs, v in zip(chunks, values):
        ps = [jnp.exp2(c - m) for c in cs]
        for p in ps:
            lsum = p if lsum is None else lsum + p
        o = _dot(jnp.concatenate(ps, axis=-1).astype(BF16), v)
        pv = o if pv is None else pv + o
    return pv, jnp.sum(lsum, axis=-1, keepdims=True)


def _head_mask(hh):
    lane = lax.broadcasted_iota(jnp.int32, (1, LANES), 1)
    return (lane // HEAD_DIM) == hh


def _band_prompt_kernel(q_ref, k0_ref, k1_ref, k2_ref, k3_ref, v0_ref, v1_ref, v2_ref, v3_ref, e_ref, mask_ref,
                        o_ref, bias_ref, *, tq):
    i = pl.program_id(1)

    @pl.when((pl.program_id(0) == 0) & (i == 0))
    def _():
        _build_bias(e_ref, mask_ref, bias_ref)

    k_refs = (k0_ref, k1_ref, k2_ref, k3_ref)
    v_refs = (v0_ref, v1_ref, v2_ref, v3_ref)
    for qb in range(2):
        pad = [jnp.where(2 * i + qb - 2 + j >= 0, 0.0, NEG) for j in range(2)] + [0.0]
        for p in range(N_PAIRS):
            cols = slice(p * LANES, (p + 1) * LANES)
            q = q_ref[qb * tq:(qb + 1) * tq, cols]
            ks = [r[:, cols] for r in k_refs[qb:qb + 3]]
            vs = [r[:, cols] for r in v_refs[qb:qb + 3]]
            out = jnp.zeros((tq, LANES), F32)
            for hh in range(2):
                msk = _head_mask(hh)
                qh = jnp.where(msk, q, jnp.zeros_like(q))
                scores = [_dot_nt(qh, ks[j]) + bias_ref[2 * p + hh, :, j * tq:(j + 1) * tq] + pad[j]
                          for j in range(3)]
                pv, l = _softmax_pv(scores, vs)
                out = jnp.where(msk, pv / l, out)
            o_ref[qb * tq:(qb + 1) * tq, cols] = out.astype(BF16)


def _band_prompt(qa, ka, va, bias_e, bias_mask, n_seq):
    t_total = qa.shape[0]
    tq = 256
    seq = t_total // n_seq
    nq = seq // tq
    qmap = lambda b, i: (b * (nq // 2) + i, 0)

    def kmap(j):
        return lambda b, i: (b * nq + jnp.maximum(2 * i - 2 + j, 0), 0)

    blk = lambda m: pl.BlockSpec((tq, WIDTH), m)
    qblk = pl.BlockSpec((2 * tq, WIDTH), qmap)
    return pl.pallas_call(
        functools.partial(_band_prompt_kernel, tq=tq),
        grid=(n_seq, nq // 2),
        in_specs=[qblk] + [blk(kmap(j)) for j in range(4)] + [blk(kmap(j)) for j in range(4)]
        + [_const_spec(bias_e.shape), _const_spec(bias_mask.shape)],
        out_specs=qblk,
        out_shape=jax.ShapeDtypeStruct((t_total, WIDTH), BF16),
        scratch_shapes=[pltpu.VMEM((N_HEADS,) + bias_mask.shape, F32)],
        compiler_params=_params(2),
        name="band_prompt",
    )(qa, ka, ka, ka, ka, va, va, va, va, bias_e, bias_mask)


def _band_sample_kernel(q_ref, kc_ref, vc_ref, kn_ref, vn_ref, e_ref, mask_ref, o_ref, ko_ref, vo_ref,
                        bias_ref):
    p_rows = kc_ref.shape[2]
    t = q_ref.shape[0]

    @pl.when(pl.program_id(0) == 0)
    def _():
        _build_bias(e_ref, mask_ref, bias_ref)

    pad = jnp.zeros((LANES - t, HEAD_DIM), BF16)
    for h in range(N_HEADS):
        cols = slice(h * HEAD_DIM, (h + 1) * HEAD_DIM)
        q = q_ref[:, cols]
        k_new, v_new = kn_ref[:, cols], vn_ref[:, cols]
        k_cache, v_cache = kc_ref[h], vc_ref[h]
        s_cache = _dot(q, k_cache.astype(BF16)) + bias_ref[h, :, 0:p_rows]
        s_new = (_dot_nt(q, jnp.concatenate([k_new.astype(BF16), pad], axis=0))
                 + bias_ref[h, :, p_rows:p_rows + LANES])
        chunks = [s_cache[:, c * LANES:(c + 1) * LANES] for c in range(p_rows // LANES)] + [s_new]
        cmax = chunks[0]
        for c in chunks[1:]:
            cmax = jnp.maximum(cmax, c)
        m = jnp.max(cmax, axis=-1, keepdims=True)
        ps = [jnp.exp2(c - m) for c in chunks]
        lsum = ps[0]
        for p in ps[1:]:
            lsum = lsum + p
        pv = (_dot_nt(jnp.concatenate(ps[:-1], axis=-1).astype(BF16), v_cache.astype(BF16))
              + _dot(ps[-1].astype(BF16), jnp.concatenate([v_new.astype(BF16), pad], axis=0)))
        o_ref[:, cols] = (pv / jnp.sum(lsum, axis=-1, keepdims=True)).astype(BF16)
        ko_ref[h] = jnp.concatenate([k_cache[:, t:], k_new.T], axis=1)
        vo_ref[h] = jnp.concatenate([v_cache[:, t:], v_new.T], axis=1)


def _band_sample(qa, ka_f, va_f, cache_kt, cache_vt, bias_e, bias_mask):
    n = cache_kt.shape[0]
    t = qa.shape[0] // n
    row = pl.BlockSpec((t, WIDTH), lambda b: (b, 0))
    cache = pl.BlockSpec((None,) + cache_kt.shape[1:], lambda b: (b, 0, 0, 0))
    return pl.pallas_call(
        _band_sample_kernel,
        grid=(n,),
        in_specs=[row, cache, cache, row, row, _const_spec(bias_e.shape), _const_spec(bias_mask.shape)],
        out_specs=[row, cache, cache],
        out_shape=[jax.ShapeDtypeStruct(qa.shape, BF16),
                   jax.ShapeDtypeStruct(cache_kt.shape, F32),
                   jax.ShapeDtypeStruct(cache_vt.shape, F32)],
        scratch_shapes=[pltpu.VMEM((N_HEADS,) + bias_mask.shape, F32)],
        compiler_params=_params(1),
        name="band_sample",
    )(qa, cache_kt, cache_vt, ka_f, va_f, bias_e, bias_mask)


def _band_bias(rel_bias, n_q, n_k, offset, band_mask, valid_k):
    assert n_q + n_k - 1 <= BIAS_PERIOD
    m = np.arange(BIAS_PERIOD)
    w = np.where(m < n_k, m, m - BIAS_PERIOD)
    idx = np.clip(offset - w, -MAX_REL, MAX_REL) + MAX_REL
    e = jnp.take(rel_bias.astype(F32), jnp.asarray(idx), axis=1) * LOG2E
    qi = np.arange(n_q)[:, None]
    kj = np.arange(n_k)[None, :]
    valid = np.broadcast_to(kj < valid_k, (n_q, n_k))
    if band_mask:
        rel_chunk = (qi + offset) // CHUNK - kj // CHUNK
        valid = valid & (rel_chunk >= 0) & (rel_chunk <= LEFT_CHUNKS)
    return e, jnp.asarray(np.where(valid, 0.0, NEG), F32)


def _build_bias(e_ref, mask_ref, bias_sc):
    rows, cols = mask_ref.shape
    for h in range(N_HEADS):
        spread = jnp.broadcast_to(e_ref[h:h + 1, :], (rows, BIAS_PERIOD))
        bias_sc[h] = pltpu.roll(spread, 0, 1, stride=1, stride_axis=0)[:, :cols] + mask_ref[...]


def _aug_head_mask(h_in_pair, pair):
    lane = lax.broadcasted_iota(jnp.int32, (1, 2 * LANES), 1)
    head = 2 * pair + h_in_pair
    in_q = (lane < LANES) & ((lane // HEAD_DIM) == h_in_pair)
    in_aug = (lane >= LANES) & (((lane - LANES) // AUG_STRIDE) == head)
    return in_q | in_aug


def _flash_step(state, s, v, row_bias=None, v_seq_minor=False):
    m_prev, l_prev, acc_prev = state
    chunks = [s[:, c * LANES:(c + 1) * LANES] for c in range(s.shape[1] // LANES)]
    if row_bias is not None:
        chunks = [c + row_bias for c in chunks]
    cmax = chunks[0]
    for c in chunks[1:]:
        cmax = jnp.maximum(cmax, c)
    m_new = jnp.maximum(m_prev, jnp.max(cmax, axis=-1, keepdims=True))
    alpha = jnp.exp2(m_prev - m_new)
    ps = [jnp.exp2(c - m_new) for c in chunks]
    lsum = ps[0]
    for p in ps[1:]:
        lsum = lsum + p
    p = jnp.concatenate(ps, axis=-1).astype(BF16)
    pv = _dot_nt(p, v) if v_seq_minor else _dot(p, v)
    return m_new, alpha * l_prev + lsum, alpha[:, :pv.shape[1]] * acc_prev + pv


def _flash_update(s, v, m_ref, l_ref, acc_ref, h, row_bias=None, v_seq_minor=False, rows=slice(None)):
    state = (m_ref[h, rows], l_ref[h, rows], acc_ref[h, rows])
    m_ref[h, rows], l_ref[h, rows], acc_ref[h, rows] = _flash_step(state, s, v, row_bias, v_seq_minor)


def _flash_result(l_ref, acc_ref, pair):
    outs = [acc_ref[2 * pair + hh] / jnp.sum(l_ref[2 * pair + hh], axis=-1, keepdims=True)
            for hh in range(2)]
    return jnp.where(_head_mask(0), outs[0], outs[1])


def _fox_prompt_kernel(qi_ref, kj_ref, q_ref, k_ref, v_ref, o_ref, qh_sc, m_sc, l_sc, acc_sc, *, tq, tk):
    t = pl.program_id(1)
    i = qi_ref[t]
    j = kj_ref[t]

    @pl.when(j == 0)
    def _():
        for p in range(N_PAIRS):
            q = q_ref[p]
            for hh in range(2):
                qh_sc[2 * p + hh] = jnp.where(_aug_head_mask(hh, p), q, jnp.zeros_like(q))
        m_sc[...] = jnp.full_like(m_sc, NEG)
        l_sc[...] = jnp.zeros_like(l_sc)
        acc_sc[...] = jnp.zeros_like(acc_sc)

    def step(diagonal):
        half = tq // 2
        if diagonal:
            row = lax.broadcasted_iota(jnp.int32, (half, tk), 0)
            col = lax.broadcasted_iota(jnp.int32, (half, tk), 1)
            keep_first = (col <= row)[:, 0:half]
            keep_second = col <= row + half
        for p in range(N_PAIRS):
            cols = slice(p * LANES, (p + 1) * LANES)
            for hh in range(2):
                h = 2 * p + hh
                if diagonal:
                    s = _dot_nt(qh_sc[h, 0:half, :], k_ref[p, 0:half, :])
                    _flash_update(jnp.where(keep_first, s, NEG), v_ref[0:half, cols], m_sc, l_sc, acc_sc, h,
                                  rows=slice(0, half))
                    s = _dot_nt(qh_sc[h, half:tq, :], k_ref[p])
                    _flash_update(jnp.where(keep_second, s, NEG), v_ref[:, cols], m_sc, l_sc, acc_sc, h,
                                  rows=slice(half, tq))
                else:
                    _flash_update(_dot_nt(qh_sc[h], k_ref[p]), v_ref[:, cols], m_sc, l_sc, acc_sc, h)

    @pl.when(j < i)
    def _():
        step(False)

    @pl.when(j == i)
    def _():
        step(True)
        for p in range(N_PAIRS):
            o_ref[:, p * LANES:(p + 1) * LANES] = _flash_result(l_sc, acc_sc, p).astype(BF16)


def _fox_prompt(qcat, kcat, vb, n_seq):
    seq = qcat.shape[2]
    tq = tk = 1024
    nt = seq // tq
    qi = np.concatenate([np.full(i + 1, i) for i in range(nt)]).astype(np.int32)
    kj = np.concatenate([np.arange(i + 1) for i in range(nt)]).astype(np.int32)
    grid_spec = pltpu.PrefetchScalarGridSpec(
        num_scalar_prefetch=2,
        grid=(n_seq, len(qi)),
        in_specs=[
            pl.BlockSpec((None, N_PAIRS, tq, 2 * LANES), lambda b, t, qi, kj: (b, 0, qi[t], 0)),
            pl.BlockSpec((None, N_PAIRS, tk, 2 * LANES), lambda b, t, qi, kj: (b, 0, kj[t], 0)),
            pl.BlockSpec((tk, WIDTH), lambda b, t, qi, kj: (b * nt + kj[t], 0)),
        ],
        out_specs=pl.BlockSpec((tq, WIDTH), lambda b, t, qi, kj: (b * nt + qi[t], 0)),
        scratch_shapes=[pltpu.VMEM((N_HEADS, tq, 2 * LANES), BF16), pltpu.VMEM((N_HEADS, tq, LANES), F32),
                        pltpu.VMEM((N_HEADS, tq, LANES), F32), pltpu.VMEM((N_HEADS, tq, LANES), F32)],
    )
    return pl.pallas_call(
        functools.partial(_fox_prompt_kernel, tq=tq, tk=tk),
        grid_spec=grid_spec,
        out_shape=jax.ShapeDtypeStruct(vb.shape, BF16),
        compiler_params=_params(2),
        name="fox_prompt",
    )(jnp.asarray(qi), jnp.asarray(kj), qcat, kcat, vb)


def _suffix_sum_exclusive(x):
    n = x.shape[1]
    lane = lax.broadcasted_iota(jnp.int32, x.shape, 1)
    y = jnp.where(lane + 1 < n, pltpu.roll(x, n - 1, axis=1), 0.0)
    shift = 1
    while shift < n:
        y = y + jnp.where(lane + shift < n, pltpu.roll(y, n - shift, axis=1), 0.0)
        shift *= 2
    return y


def _fox_sample_kernel(q_ref, kc_ref, vc_ref, kn_ref, vn_ref, lfc_ref, lfn_ref, u_ref,
                       o_ref, cq_sc, car_sc, m_sc, l_sc, acc_sc, *, n_cache_tiles, sub_keys):
    j = pl.program_id(1)
    t = q_ref.shape[1]

    def cum_new():
        hi, mid, lo = _split3(lfn_ref[...])
        u = u_ref[...]
        return _dot(hi, u) + _dot(mid, u) + _dot(lo, u)

    def q_head(h):
        return q_ref[h // 2, :, (h % 2) * HEAD_DIM:(h % 2 + 1) * HEAD_DIM]

    @pl.when(j == 0)
    def _():
        cn = cum_new() * LOG2E
        eye = (lax.broadcasted_iota(jnp.int32, (t, LANES), 0)
               == lax.broadcasted_iota(jnp.int32, (t, LANES), 1))
        for h in range(N_HEADS):
            col = jnp.sum(jnp.where(eye, jnp.broadcast_to(cn[h:h + 1, :], (t, LANES)), 0.0),
                          axis=-1, keepdims=True)
            cq_sc[h] = jnp.broadcast_to(col, (t, LANES))
        car_sc[...] = jnp.zeros_like(car_sc)
        m_sc[...] = jnp.full_like(m_sc, NEG)
        l_sc[...] = jnp.zeros_like(l_sc)
        acc_sc[...] = jnp.zeros_like(acc_sc)

    lf = lfc_ref[...]
    r = _suffix_sum_exclusive(lf) + car_sc[:, 0:1]
    car_sc[...] = jnp.broadcast_to(r[:, 0:1] + lf[:, 0:1], car_sc.shape)
    r = r * LOG2E
    for sub in range(lf.shape[1] // sub_keys):
        keys = slice(sub * sub_keys, (sub + 1) * sub_keys)
        for h in range(N_HEADS):
            s = _dot(q_head(h), kc_ref[h, :, keys].astype(BF16)) + r[h:h + 1, keys]
            _flash_update(s, vc_ref[h, :, keys].astype(BF16), m_sc, l_sc, acc_sc, h,
                          row_bias=cq_sc[h], v_seq_minor=True)

    @pl.when(j == n_cache_tiles - 1)
    def _():
        cn = cum_new() * LOG2E
        causal = (lax.broadcasted_iota(jnp.int32, (t, LANES), 1)
                  <= lax.broadcasted_iota(jnp.int32, (t, LANES), 0))
        pad = jnp.zeros((LANES - t, HEAD_DIM), BF16)
        for h in range(N_HEADS):
            cols = slice(h * HEAD_DIM, (h + 1) * HEAD_DIM)
            k = jnp.concatenate([kn_ref[h // 2, :, (h % 2) * HEAD_DIM:(h % 2 + 1) * HEAD_DIM], pad], axis=0)
            v = jnp.concatenate([vn_ref[:, cols], pad], axis=0)
            s = _dot_nt(q_head(h), k) + cq_sc[h] - cn[h:h + 1, :]
            _flash_update(jnp.where(causal, s, NEG), v, m_sc, l_sc, acc_sc, h)
            o_ref[:, cols] = (acc_sc[h] / jnp.sum(l_sc[h], axis=-1, keepdims=True)).astype(BF16)


def _fox_sample(qcat, kcat, vb, cache_kt, cache_vt, lf_cache_t, lf_new_t):
    n, _, _, p_rows = cache_kt.shape
    t = vb.shape[0] // n
    tk = 4096
    nct = p_rows // tk
    u = jnp.asarray(np.pad(np.triu(np.ones((t, t))), ((0, 0), (0, LANES - t))), BF16)
    rev = lambda j: nct - 1 - j
    cache = pl.BlockSpec((None, N_HEADS, HEAD_DIM, tk), lambda b, j: (b, 0, 0, rev(j)))
    new_cat = pl.BlockSpec((None, N_PAIRS, t, 2 * LANES), lambda b, j: (0, 0, b, 0))
    return pl.pallas_call(
        functools.partial(_fox_sample_kernel, n_cache_tiles=nct, sub_keys=tk),
        grid=(n, nct),
        in_specs=[
            new_cat, cache, cache, new_cat,
            pl.BlockSpec((t, WIDTH), lambda b, j: (b, 0)),
            pl.BlockSpec((None, N_HEADS, tk), lambda b, j: (b, 0, rev(j))),
            pl.BlockSpec((None, N_HEADS, t), lambda b, j: (b, 0, 0)),
            _const_spec(u.shape),
        ],
        out_specs=pl.BlockSpec((t, WIDTH), lambda b, j: (b, 0)),
        out_shape=jax.ShapeDtypeStruct(vb.shape, BF16),
        scratch_shapes=[pltpu.VMEM((N_HEADS, t, LANES), F32), pltpu.VMEM((N_HEADS, LANES), F32),
                        pltpu.VMEM((N_HEADS, t, LANES), F32), pltpu.VMEM((N_HEADS, t, LANES), F32),
                        pltpu.VMEM((N_HEADS, t, HEAD_DIM), F32)],
        compiler_params=_params(2),
        name="fox_sample",
    )(qcat, cache_kt, cache_vt, kcat, vb, lf_cache_t, lf_new_t, u)


GROUP_LANE = 64
ROW_ALIGN = 16
MOE_TILE = 512
SORT_ROWS = 640
XS_WIDTH = D_MODEL + 2 * LANES


def _route(r):
    lane_i = lax.broadcasted_iota(jnp.int32, r.shape, 1)
    lane = lane_i.astype(F32)
    lane_grp = (lane_i // EXPERTS_PER_GROUP).astype(F32)
    big = float(LANES)
    is_coarse = (lane_i >= N_EXPERTS) & (lane_i < N_EXPERTS + N_GROUPS)
    cm = jnp.where(is_coarse, r, NEG)
    cmax = cm.max(axis=-1, keepdims=True)
    grp = jnp.min(jnp.where(cm == cmax, lane - N_EXPERTS, big), axis=-1, keepdims=True)
    pg_sel = 1.0 / jnp.sum(jnp.exp(cm - cmax), axis=-1, keepdims=True)
    in_grp = (lane_i < N_EXPERTS) & (lane_grp == grp)
    fm = jnp.where(in_grp, r, NEG)
    m1 = fm.max(axis=-1, keepdims=True)
    denom = jnp.sum(jnp.exp(fm - m1), axis=-1, keepdims=True)
    i1 = jnp.min(jnp.where(fm == m1, lane, big), axis=-1, keepdims=True)
    fm2 = jnp.where(lane == i1, NEG, fm)
    m2 = fm2.max(axis=-1, keepdims=True)
    i2 = jnp.min(jnp.where(fm2 == m2, lane, big), axis=-1, keepdims=True)
    p1 = 1.0 / denom
    p2 = jnp.exp(m2 - m1) / denom
    tot = p1 + p2
    comb = (jnp.where(lane == i1, pg_sel * (p1 / tot), 0.0)
            + jnp.where(lane == i2, pg_sel * (p2 / tot), 0.0))
    return comb, grp


def _merge_kernel(x1_ref, oa1_ref, ob1_ref, x2_ref, oa2_ref, ob2_ref, g_ref, wg_ref, wpa_ref, wpb_ref, wo_ref,
                  gf_ref, wrh_ref, wrm_ref, br_ref, y_ref, hx_ref, route_ref, grp_t_ref, cnt_ref,
                  x_sc, oa_sc, ob_sc, *, tiles_first):
    i = pl.program_id(0)

    @pl.when(i < tiles_first)
    def _():
        x_sc[...], oa_sc[...], ob_sc[...] = x1_ref[...], oa1_ref[...], ob1_ref[...]

    @pl.when(i >= tiles_first)
    def _():
        x_sc[...], oa_sc[...], ob_sc[...] = x2_ref[...], oa2_ref[...], ob2_ref[...]

    x = x_sc[...]
    h = _rms(x, g_ref[...]).astype(BF16)
    gate = jax.nn.sigmoid(_dot(h, wg_ref[...]))
    mix = (gate[:, :D_MODEL] * _dot(oa_sc[...], wpa_ref[...])
           + gate[:, D_MODEL:] * _dot(ob_sc[...], wpb_ref[...]))
    y = x + _dot(mix.astype(BF16), wo_ref[...])
    y_ref[...] = y

    hx = _rms(y, gf_ref[...])
    hx_ref[:, 0:D_MODEL] = hx.astype(BF16)
    h_hi, h_mid, _ = _split3(hx)
    r = _dot(h_hi, wrh_ref[...]) + _dot(h_hi, wrm_ref[...]) + _dot(h_mid, wrh_ref[...]) + br_ref[...]
    comb, grp = _route(r)
    lane = lax.broadcasted_iota(jnp.int32, comb.shape, 1)
    route = jnp.where(lane == GROUP_LANE, grp, comb)
    route_ref[...] = route
    r_hi = route.astype(BF16)
    hx_ref[:, D_MODEL:D_MODEL + LANES] = r_hi
    hx_ref[:, D_MODEL + LANES:] = (route - r_hi.astype(F32)).astype(BF16)
    grp_t_ref[...] = route.T[GROUP_LANE:GROUP_LANE + 8, :]
    cnt = jnp.sum(jnp.where(lane.astype(F32) == grp, 1.0, 0.0), axis=0, keepdims=True)
    cnt_ref[...] = jnp.broadcast_to(cnt, cnt_ref.shape)


def _two_stream_maps(tiles_first):
    first = lambda i, *_: (jnp.minimum(i, tiles_first - 1), 0)
    second = lambda i, *_: (jnp.maximum(i - tiles_first, 0), 0)
    return first, second


def _merge(x1, oa1, ob1, x2, oa2, ob2, w):
    tm = MOE_TILE
    tiles_first = x1.shape[0] // tm
    t_total = x1.shape[0] + x2.shape[0]
    n_tiles = t_total // tm
    first, second = _two_stream_maps(tiles_first)
    row = lambda n: pl.BlockSpec((tm, n), lambda i: (i, 0))
    stream = lambda m: [pl.BlockSpec((tm, D_MODEL), m), pl.BlockSpec((tm, WIDTH), m), pl.BlockSpec((tm, WIDTH), m)]
    consts = [w["g_mix"], w["w_gate"], w["w_pa"], w["w_pb"], w["w_o"], w["g_ffn"], w["w_router_hi"],
              w["w_router_mid"], w["b_router"]]
    return pl.pallas_call(
        functools.partial(_merge_kernel, tiles_first=tiles_first),
        grid=(n_tiles,),
        in_specs=stream(first) + stream(second) + [_const_spec(c.shape) for c in consts],
        out_specs=[row(D_MODEL), row(XS_WIDTH), row(LANES), pl.BlockSpec((8, tm), lambda i: (0, i)),
                   pl.BlockSpec((None, 8, LANES), lambda i: (i, 0, 0))],
        out_shape=[jax.ShapeDtypeStruct((t_total, D_MODEL), F32), jax.ShapeDtypeStruct((t_total, XS_WIDTH), BF16),
                   jax.ShapeDtypeStruct((t_total, LANES), F32), jax.ShapeDtypeStruct((8, t_total), F32),
                   jax.ShapeDtypeStruct((n_tiles, 8, LANES), F32)],
        scratch_shapes=[pltpu.VMEM((tm, D_MODEL), F32), pltpu.VMEM((tm, WIDTH), BF16),
                        pltpu.VMEM((tm, WIDTH), BF16)],
        compiler_params=_params(1),
        name="merge",
    )(x1, oa1, ob1, x2, oa2, ob2, *consts)


def _plan_kernel(cnt_ref, ib_ref, dst_ref, len_ref, fill_ref, inblk_ref, outblk_ref, grp_ref, kind_ref, *,
                 n_tiles, n_steps):
    align_bits = ROW_ALIGN.bit_length() - 1
    tile_bits = MOE_TILE.bit_length() - 1
    zero = jnp.int32(0)

    def per_tile(t, tot):
        run = zero
        new_tot = []
        for g in range(N_GROUPS):
            n = ((cnt_ref[t * N_GROUPS + g] + (ROW_ALIGN - 1)) >> align_bits) << align_bits
            ib_ref[t * N_GROUPS + g] = run
            len_ref[t * N_GROUPS + g] = n
            dst_ref[t * N_GROUPS + g] = tot[g]
            run = run + n
            new_tot.append(tot[g] + n)
        return tuple(new_tot)

    tot = lax.fori_loop(0, n_tiles, per_tile, (zero,) * N_GROUPS)
    n_ffn = [(tot[g] + (MOE_TILE - 1)) >> tile_bits for g in range(N_GROUPS)]
    g_base, ends, acc_rows, acc_tiles = [], [], zero, zero
    for g in range(N_GROUPS):
        g_base.append(acc_rows)
        acc_rows = acc_rows + ((n_ffn[g] + 1) << tile_bits)
        acc_tiles = acc_tiles + n_ffn[g] + 1
        ends.append(acc_tiles)

    def add_base(t, carry):
        for g in range(N_GROUPS):
            dst_ref[t * N_GROUPS + g] = dst_ref[t * N_GROUPS + g] + g_base[g]
        return carry

    lax.fori_loop(0, n_tiles, add_base, zero)
    for g in range(N_GROUPS):
        fill_ref[g] = g_base[g] + tot[g]
        fill_ref[N_GROUPS + g] = g_base[g] + (n_ffn[g] << tile_bits)
    fill_ref[2 * N_GROUPS] = ends[-1]
    first_blk = g_base[N_GROUPS - 1]
    for g in reversed(range(N_GROUPS - 1)):
        first_blk = jnp.where(n_ffn[g] > 0, g_base[g], first_blk)
    first_blk = first_blk >> tile_bits

    def pick(vals, g):
        out = vals[N_GROUPS - 1]
        for i in reversed(range(N_GROUPS - 1)):
            out = jnp.where(g == i, vals[i], out)
        return out

    def per_step(k, carry):
        kk = jnp.minimum(k, ends[-1] - 1)
        g = zero
        for i in range(N_GROUPS - 1):
            g = g + jnp.where(kk >= ends[i], 1, 0)
        j = kk - pick([zero] + ends[:-1], g)
        is_expert = j < pick(n_ffn, g)
        out_blk = jnp.where(k < ends[-1], (pick(g_base, g) >> tile_bits) + j, k)
        outblk_ref[k] = out_blk
        inblk_ref[k] = jnp.where(is_expert, out_blk, first_blk)
        grp_ref[k] = g
        kind_ref[k] = jnp.where(is_expert, 1, 0)
        return carry

    lax.fori_loop(0, n_steps, per_step, zero)


def _moe_plan(cnt, t_total):
    n_tiles = t_total // MOE_TILE
    n_steps = (n_tiles + 2 * N_GROUPS + (N_GROUPS * (ROW_ALIGN - 1) * n_tiles + MOE_TILE - 1) // MOE_TILE)
    smem = pl.BlockSpec(memory_space=pltpu.SMEM)
    sizes = [n_tiles * N_GROUPS] * 3 + [2 * N_GROUPS + 1] + [n_steps] * 4
    outs = pl.pallas_call(
        functools.partial(_plan_kernel, n_tiles=n_tiles, n_steps=n_steps),
        grid_spec=pltpu.PrefetchScalarGridSpec(num_scalar_prefetch=1, grid=(1,), in_specs=[],
                                               out_specs=[smem] * len(sizes)),
        out_shape=[jax.ShapeDtypeStruct((n,), jnp.int32) for n in sizes],
        compiler_params=_params(1),
        name="moe_plan",
    )(cnt.reshape(-1))
    return (*outs, n_steps)


def _sort_kernel(ib_ref, dst_ref, len_ref, fill_ref, hx_ref, grp_t_ref, tri_ref, xs_ref, xsort, sems):
    i = pl.program_id(0)
    n = pl.num_programs(0)
    tm = hx_ref.shape[0]
    half = tm // 2
    slot = i % 2

    def copy(src, dst, rows, sl, g, part):
        return pltpu.make_async_copy(xsort.at[sl, pl.ds(pl.multiple_of(src, ROW_ALIGN), rows)],
                                     xs_ref.at[pl.ds(pl.multiple_of(dst, ROW_ALIGN), rows)], sems.at[sl, g, part])

    def for_segments(step, sl, act):
        for g in range(N_GROUPS):
            src, dst = ib_ref[step * N_GROUPS + g], dst_ref[step * N_GROUPS + g]
            act(copy(src, dst, half, sl, g, 0))

            @pl.when(len_ref[step * N_GROUPS + g] > half)
            def _():
                act(copy(src + half, dst + half, half, sl, g, 1))

    def run(cs):
        for c in cs:
            c.start()
        for c in cs:
            c.wait()

    @pl.when(i == 0)
    def _():
        for sl in range(2):
            xsort[sl, SORT_ROWS:, :] = jnp.zeros((xsort.shape[1] - SORT_ROWS, XS_WIDTH), BF16)

    g_row = grp_t_ref[0:1, :]
    sub = lax.broadcasted_iota(jnp.int32, (8, tm), 0).astype(F32)
    mine = sub == g_row
    before = _dot(jnp.where(mine, 1.0, 0.0).astype(BF16), tri_ref[...])
    dest = jnp.sum(jnp.where(mine, before, 0.0), axis=0, keepdims=True)
    for g in range(N_GROUPS):
        base = jnp.full(dest.shape, ib_ref[i * N_GROUPS + g], jnp.int32).astype(F32)
        dest = dest + jnp.where(g_row == g, base, 0.0)
    rows = lax.broadcasted_iota(jnp.int32, (SORT_ROWS, tm), 0).astype(F32)
    perm = jnp.where(rows == dest, 1.0, 0.0).astype(BF16)
    xsort[slot, 0:SORT_ROWS, :] = _dot(perm, hx_ref[...]).astype(BF16)

    @pl.when(i > 0)
    def _():
        for_segments(i - 1, 1 - slot, lambda c: c.wait())

    for_segments(i, slot, lambda c: c.start())

    @pl.when(i == n - 1)
    def _():
        for_segments(i, slot, lambda c: c.wait())
        run([copy(SORT_ROWS, fill_ref[g], tm, slot, g, 0) for g in range(N_GROUPS)])
        run([copy(SORT_ROWS, fill_ref[N_GROUPS + g], tm, slot, g, 0) for g in range(N_GROUPS)])

        @pl.loop(fill_ref[2 * N_GROUPS], xs_ref.shape[0] // tm)
        def _(b):
            run([copy(SORT_ROWS, b * tm, tm, slot, 0, 0)])


def _expert_kernel(ib_ref, ob_ref, grp_ref, kind_ref, xs_ref, w1_ref, w3_ref, w2_ref, ys_ref):
    del ib_ref, ob_ref
    k = pl.program_id(0)

    @pl.when(kind_ref[k] == 1)
    def _():
        x = xs_ref[:, 0:D_MODEL]
        side = lambda w_ref: jnp.concatenate([w_ref[e].astype(BF16) for e in range(EXPERTS_PER_GROUP)], axis=1)
        hid = jax.nn.silu(_dot(x, side(w1_ref))) * _dot(x, side(w3_ref))
        comb = (xs_ref[:, D_MODEL:D_MODEL + LANES].astype(F32) + xs_ref[:, D_MODEL + LANES:].astype(F32))
        lane = lax.broadcasted_iota(jnp.int32, comb.shape, 1)
        first = grp_ref[k] * EXPERTS_PER_GROUP
        scale = []
        for e in range(EXPERTS_PER_GROUP):
            w_e = jnp.sum(jnp.where(lane == first + e, comb, 0.0), axis=-1, keepdims=True)
            scale.append(jnp.broadcast_to(w_e, (comb.shape[0], D_EXPERT)))
        hid = hid * jnp.concatenate(scale, axis=-1)
        w2 = w2_ref[...].astype(BF16).reshape(EXPERTS_PER_GROUP * D_EXPERT, D_MODEL)
        ys_ref[...] = _dot(hid.astype(BF16), w2).astype(BF16)

    @pl.when(kind_ref[k] == 0)
    def _():
        ys_ref[...] = jnp.zeros_like(ys_ref)


def _unsort_kernel(ib_ref, dst_ref, len_ref, y_ref, route_ref, tri_ref, ys_ref, o1_ref, o2_ref, ybuf, yasm, sems,
                   *, tiles_first):
    i = pl.program_id(0)
    n = pl.num_programs(0)
    tm = y_ref.shape[0]
    half = tm // 2
    slot = i % 2

    def fetch(step, sl, act):
        for g in range(N_GROUPS):
            src = pl.multiple_of(dst_ref[step * N_GROUPS + g], ROW_ALIGN)
            act(pltpu.make_async_copy(ys_ref.at[pl.ds(src, half)], ybuf.at[sl, g, 0:half], sems.at[sl, g, 0]))

            @pl.when(len_ref[step * N_GROUPS + g] > half)
            def _():
                act(pltpu.make_async_copy(ys_ref.at[pl.ds(src + half, half)], ybuf.at[sl, g, half:tm],
                                          sems.at[sl, g, 1]))

    @pl.when(i == 0)
    def _():
        yasm[...] = jnp.zeros_like(yasm)
        fetch(0, 0, lambda c: c.start())

    @pl.when(i + 1 < n)
    def _():
        fetch(i + 1, 1 - slot, lambda c: c.start())

    route = route_ref[...]
    lane = lax.broadcasted_iota(jnp.int32, route.shape, 1).astype(F32)
    grp = jnp.sum(jnp.where(lane == GROUP_LANE, route, 0.0), axis=-1, keepdims=True)
    mine = lane == grp
    before = _dot(tri_ref[...], jnp.where(mine, 1.0, 0.0).astype(BF16))
    dest = jnp.sum(jnp.where(mine, before, 0.0), axis=-1, keepdims=True)
    for g in range(N_GROUPS):
        base = jnp.full(dest.shape, ib_ref[i * N_GROUPS + g], jnp.int32).astype(F32)
        dest = dest + jnp.where(grp == g, base, 0.0)
    cols = lax.broadcasted_iota(jnp.int32, (tm, SORT_ROWS), 1).astype(F32)
    perm_t = jnp.where(cols == dest, 1.0, 0.0).astype(BF16)

    fetch(i, slot, lambda c: c.wait())
    for g in range(N_GROUPS):
        row0 = pl.multiple_of(ib_ref[i * N_GROUPS + g], ROW_ALIGN)
        yasm[pl.ds(row0, half), :] = ybuf[slot, g, 0:half]

        @pl.when(len_ref[i * N_GROUPS + g] > half)
        def _():
            yasm[pl.ds(row0 + half, half), :] = ybuf[slot, g, half:tm]
    out = y_ref[...] + _dot(perm_t, yasm[0:SORT_ROWS, :])

    @pl.when(i < tiles_first)
    def _():
        o1_ref[...] = out

    @pl.when(i >= tiles_first)
    def _():
        o2_ref[...] = out


def _moe(y, hx, route, grp_t, cnt, w, rows_first):
    t_total = y.shape[0]
    tm = MOE_TILE
    n_tiles = t_total // tm
    assert SORT_ROWS >= tm + N_GROUPS * (ROW_ALIGN - 1)
    in_base, dst, seg_len, fill, in_blk, out_blk, grp_of_step, is_expert, n_steps = _moe_plan(
        cnt[:, 0, :N_GROUPS].astype(jnp.int32), t_total)
    cap_rows = n_steps * tm
    lower = jnp.asarray(np.tril(np.ones((tm, tm)), -1), BF16)
    upper = jnp.asarray(np.triu(np.ones((tm, tm)), 1), BF16)
    stage_rows = SORT_ROWS + tm
    any_spec = pl.BlockSpec(memory_space=pl.ANY)

    xs = pl.pallas_call(
        _sort_kernel,
        grid_spec=pltpu.PrefetchScalarGridSpec(
            num_scalar_prefetch=4,
            grid=(n_tiles,),
            in_specs=[pl.BlockSpec((tm, XS_WIDTH), lambda i, *_: (i, 0)),
                      pl.BlockSpec((8, tm), lambda i, *_: (0, i)),
                      pl.BlockSpec((tm, tm), lambda i, *_: (0, 0))],
            out_specs=any_spec,
            scratch_shapes=[pltpu.VMEM((2, stage_rows, XS_WIDTH), BF16),
                            pltpu.SemaphoreType.DMA((2, N_GROUPS, 2))],
        ),
        out_shape=jax.ShapeDtypeStruct((cap_rows, XS_WIDTH), BF16),
        compiler_params=_params(1),
        name="moe_sort",
    )(in_base, dst, seg_len, fill, hx, grp_t, upper)

    step_map = lambda which: (lambda k, ib, ob, grp, kind: ((ib, ob, grp)[which][k], 0))
    wmap = lambda k, ib, ob, grp, kind: (grp[k], 0, 0)
    wspec = pl.BlockSpec((EXPERTS_PER_GROUP, D_MODEL, D_EXPERT), wmap)
    ys = pl.pallas_call(
        _expert_kernel,
        grid_spec=pltpu.PrefetchScalarGridSpec(
            num_scalar_prefetch=4,
            grid=(n_steps,),
            in_specs=[pl.BlockSpec((tm, XS_WIDTH), step_map(0)), wspec, wspec, pl.BlockSpec((EXPERTS_PER_GROUP, D_EXPERT, D_MODEL), wmap)],
            out_specs=pl.BlockSpec((tm, D_MODEL), step_map(1)),
        ),
        out_shape=jax.ShapeDtypeStruct((cap_rows, D_MODEL), BF16),
        compiler_params=_params(1),
        name="moe_experts",
    )(in_blk, out_blk, grp_of_step, is_expert, xs, w["w1"], w["w3"], w["w2"])

    first, second = _two_stream_maps(rows_first // tm)
    return pl.pallas_call(
        functools.partial(_unsort_kernel, tiles_first=rows_first // tm),
        grid_spec=pltpu.PrefetchScalarGridSpec(
            num_scalar_prefetch=3,
            grid=(n_tiles,),
            in_specs=[pl.BlockSpec((tm, D_MODEL), lambda i, *_: (i, 0)),
                      pl.BlockSpec((tm, LANES), lambda i, *_: (i, 0)),
                      pl.BlockSpec((tm, tm), lambda i, *_: (0, 0)),
                      any_spec],
            out_specs=[pl.BlockSpec((tm, D_MODEL), first), pl.BlockSpec((tm, D_MODEL), second)],
            scratch_shapes=[pltpu.VMEM((2, N_GROUPS, tm, D_MODEL), BF16),
                            pltpu.VMEM((stage_rows, D_MODEL), BF16),
                            pltpu.SemaphoreType.DMA((2, N_GROUPS, 2))],
        ),
        out_shape=[jax.ShapeDtypeStruct((rows_first, D_MODEL), F32),
                   jax.ShapeDtypeStruct((t_total - rows_first, D_MODEL), F32)],
        compiler_params=_params(1),
        name="moe_unsort",
    )(in_base, dst, seg_len, y, route, lower, ys)


def _prep_weights(g_mix, w_in, b_f, q_norm_a, k_norm_a, q_norm_b, k_norm_b, w_pa, w_pb, w_o,
                  g_ffn, w_rg, b_rg, w_re, b_re, w1, w3, w2):
    n_qkv = 6 * WIDTH
    tile = lambda g: jnp.tile(g, N_HEADS)
    w_router = jnp.concatenate(
        [jnp.transpose(w_re, (1, 0, 2)).reshape(D_MODEL, N_EXPERTS), w_rg,
         jnp.zeros((D_MODEL, LANES - N_EXPERTS - N_GROUPS), F32)], axis=1)
    b_router = jnp.concatenate(
        [b_re.reshape(N_EXPERTS), b_rg, jnp.zeros((LANES - N_EXPERTS - N_GROUPS,), F32)])[None, :]
    return {
        "g_mix": g_mix[None, :],
        "w_qkv": w_in[:, :n_qkv].astype(BF16),
        "w_f": jnp.pad(w_in[:, n_qkv:n_qkv + N_HEADS], ((0, 0), (0, LANES - N_HEADS))).astype(BF16),
        "b_f": jnp.pad(b_f, (0, LANES - N_HEADS))[None, :],
        "w_gate": w_in[:, n_qkv + N_HEADS:].astype(BF16),
        "gains": jnp.stack([tile(q_norm_a), tile(k_norm_a), tile(q_norm_b), tile(k_norm_b)]),
        "w_pa": w_pa.astype(BF16), "w_pb": w_pb.astype(BF16), "w_o": w_o.astype(BF16),
        "g_ffn": g_ffn[None, :],
        "w_router_hi": w_router.astype(BF16),
        "w_router_mid": (w_router - w_router.astype(BF16).astype(F32)).astype(BF16),
        "b_router": b_router,
        "w1": w1, "w3": w3, "w2": w2,
    }


def kernel(x_prompt, x_sample, cache_a_k, cache_a_v, cache_b_k, cache_b_v, cache_b_logf, g_mix, w_in, b_f, q_norm_a, k_norm_a, q_norm_b, k_norm_b, rel_bias, w_pa, w_pb, w_o, g_ffn, w_rg, b_rg, w_re, b_re, w1, w3, w2):
    assert g_mix.shape[0] == 1, "single-layer step"
    n_p, seq, _ = x_prompt.shape
    n_s, t_s, _ = x_sample.shape
    a_rows = cache_a_k.shape[2]
    w = _prep_weights(g_mix[0], w_in[0], b_f[0], q_norm_a[0], k_norm_a[0], q_norm_b[0], k_norm_b[0],
                      w_pa[0], w_pb[0], w_o[0], g_ffn[0], w_rg[0], b_rg[0], w_re[0], b_re[0],
                      w1[0], w3[0], w2[0])
    band_tq = 256
    bias_prompt = _band_bias(rel_bias[0], band_tq, 3 * band_tq, 2 * band_tq, True, 3 * band_tq)
    bias_sample = _band_bias(rel_bias[0], t_s, a_rows + LANES, a_rows, False, a_rows + t_s)

    seq_minor = lambda a: jnp.transpose(a, (0, 2, 3, 1))
    seq_major = lambda a: jnp.transpose(a, (0, 3, 1, 2))[None]

    xp = x_prompt.reshape(n_p * seq, D_MODEL)
    (qa, ka, va, ka_t, va_t, qcat, kcat, vb, kb_t, vb_t, logf_t) = _inproj(xp, n_p, w, seq_minor=True)
    o_a = _band_prompt(qa, ka, va, *bias_prompt, n_p)
    o_b = _fox_prompt(qcat, kcat, vb, n_p)

    xs = x_sample.reshape(n_s * t_s, D_MODEL)
    (qa_s, _, _, ka_fs, va_fs, qcat_s, kcat_s, vb_s, kb_fs, vb_fs, logf_s) = _inproj(
        xs, 1, w, seq_minor=False)
    o_as, new_ak_t, new_av_t = _band_sample(
        qa_s, ka_fs, va_fs, seq_minor(cache_a_k[0]), seq_minor(cache_a_v[0]), *bias_sample)
    lf_cache_t = jnp.transpose(cache_b_logf[0], (0, 2, 1))
    lf_new_t = jnp.transpose(logf_s.reshape(n_s, t_s, N_HEADS), (0, 2, 1))
    o_bs = _fox_sample(qcat_s, kcat_s, vb_s, seq_minor(cache_b_k[0]), seq_minor(cache_b_v[0]),
                       lf_cache_t, lf_new_t)

    y_p, y_s = _moe(*_merge(xp, o_a, o_b, xs, o_as, o_bs, w), w, rows_first=n_p * seq)

    heads = lambda a, n, r: a.reshape(1, n, r, N_HEADS, HEAD_DIM)
    return (y_p.reshape(n_p, seq, D_MODEL), y_s.reshape(n_s, t_s, D_MODEL),
            seq_major(ka_t), seq_major(va_t), seq_major(kb_t), seq_major(vb_t),
            jnp.transpose(logf_t, (0, 2, 1))[None],
            seq_major(new_ak_t), seq_major(new_av_t),
            heads(kb_fs, n_s, t_s), heads(vb_fs, n_s, t_s), logf_s.reshape(1, n_s, t_s, N_HEADS))
```

```python
import functools

import numpy as np
import jax
import jax.numpy as jnp
from jax import lax
from jax.experimental import pallas as pl
from jax.experimental.pallas import tpu as pltpu

F32 = jnp.float32
BF16 = jnp.bfloat16

D_MODEL = 1024
HEAD_DIM = 64
N_HEADS = 8
WIDTH = N_HEADS * HEAD_DIM
N_PAIRS = N_HEADS // 2
CHUNK = 64
LEFT_CHUNKS = 8
WINDOW_ROWS = LEFT_CHUNKS * CHUNK
MAX_REL = 256
N_GROUPS = 4
EXPERTS_PER_GROUP = 8
N_EXPERTS = N_GROUPS * EXPERTS_PER_GROUP
D_EXPERT = 128
EPS = 1e-6
NEG = -1e30
LOG2E = 1.4426950408889634
LANES = 128
BIAS_PERIOD = 1024
AUG_STRIDE = 8
VMEM_LIMIT = 56 * 1024 * 1024

_NT = (((1,), (1,)), ((), ()))


def _dot(a, b):
    return jnp.dot(a, b, preferred_element_type=F32)


def _dot_nt(a, b):
    return lax.dot_general(a, b, _NT, preferred_element_type=F32)


def _split3(x):
    hi = x.astype(BF16)
    r = x - hi.astype(F32)
    mid = r.astype(BF16)
    lo = (r - mid.astype(F32)).astype(BF16)
    return hi, mid, lo


def _dot3(a_bf, x):
    hi, mid, lo = _split3(x)
    return _dot(a_bf, hi) + _dot(a_bf, mid) + _dot(a_bf, lo)


def _rms(x, g):
    ms = jnp.mean(x * x, axis=-1, keepdims=True)
    return x * lax.rsqrt(ms + EPS) * g


def _params(n_axes):
    return pltpu.CompilerParams(dimension_semantics=("arbitrary",) * n_axes,
                                vmem_limit_bytes=VMEM_LIMIT)


def _const_spec(shape):
    nd = len(shape)
    return pl.BlockSpec(shape, lambda *_: (0,) * nd)


def _inproj_kernel(x_ref, g_ref, wqkv_ref, wf_ref, bf_ref, gains_ref, bd_ref, ltri_ref,
                   selq_ref, selk_ref, oneq_ref, onek_ref,
                   qa_ref, ka_ref, va_ref, kaf_ref, vaf_ref, qcat_ref, kcat_ref, vb_ref,
                   kbf_ref, vbf_ref, logf_ref, carry_ref, *, tiles_per_seq, seq_minor):
    i = pl.program_id(0)
    tm = x_ref.shape[0]

    @pl.when(i % tiles_per_seq == 0)
    def _():
        carry_ref[...] = jnp.zeros_like(carry_ref)

    def put_f32(ref, y, last_tile_only=False):
        if not seq_minor:
            ref[...] = y
        elif last_tile_only:
            @pl.when(i % tiles_per_seq == tiles_per_seq - 1)
            def _():
                ref[...] = y.T.reshape(N_HEADS, HEAD_DIM, tm)
        else:
            ref[...] = y.T.reshape(N_HEADS, HEAD_DIM, tm)

    h = _rms(x_ref[...], g_ref[...]).astype(BF16)

    def seg(s):
        return _dot(h, wqkv_ref[:, s * WIDTH:(s + 1) * WIDTH])

    def headnorm(y, n):
        ss = _dot((y * y).astype(BF16), bd_ref[...])
        return y * lax.rsqrt(ss * (1.0 / HEAD_DIM) + EPS) * gains_ref[n:n + 1, :]

    q_a = headnorm(seg(0), 0)
    qa_ref[...] = (q_a * (HEAD_DIM ** -0.5 * LOG2E)).astype(BF16)
    k_a = headnorm(seg(1), 1)
    ka_ref[...] = k_a.astype(BF16)
    put_f32(kaf_ref, k_a, last_tile_only=True)
    v_a = seg(2)
    va_ref[...] = v_a.astype(BF16)
    put_f32(vaf_ref, v_a, last_tile_only=True)

    z = _dot(h, wf_ref[...]) + bf_ref[...]
    logf = jnp.minimum(z, 0.0) - jnp.log(1.0 + jnp.exp(-jnp.abs(z)))
    if seq_minor:
        logf_ref[...] = logf.T[0:N_HEADS, :]
    else:
        logf_ref[...] = logf[:, :N_HEADS]
    c = _dot3(ltri_ref[...], logf) + carry_ref[0:1, :]
    carry_ref[...] = jnp.broadcast_to(c[-1:, :], carry_ref.shape)
    cs = jnp.concatenate(_split3(c * LOG2E), axis=-1)
    q_aug = (_dot(cs, selq_ref[...]) + oneq_ref[...]).astype(BF16)
    k_aug = (_dot(cs, selk_ref[...]) + onek_ref[...]).astype(BF16)

    q_b = (headnorm(seg(3), 2) * (HEAD_DIM ** -0.5 * LOG2E)).astype(BF16)
    k_b = headnorm(seg(4), 3)
    put_f32(kbf_ref, k_b)
    k_b = k_b.astype(BF16)
    v_b = seg(5)
    put_f32(vbf_ref, v_b)
    vb_ref[...] = v_b.astype(BF16)
    for p in range(N_PAIRS):
        cols = slice(p * LANES, (p + 1) * LANES)
        qcat_ref[p, :, 0:LANES] = q_b[:, cols]
        qcat_ref[p, :, LANES:2 * LANES] = q_aug
        kcat_ref[p, :, 0:LANES] = k_b[:, cols]
        kcat_ref[p, :, LANES:2 * LANES] = k_aug


def _aug_constants():
    selq = np.zeros((3 * LANES, LANES), np.float32)
    selk = np.zeros((3 * LANES, LANES), np.float32)
    oneq = np.zeros((1, LANES), np.float32)
    onek = np.zeros((1, LANES), np.float32)
    for h in range(N_HEADS):
        for k in range(3):
            selq[k * LANES + h, AUG_STRIDE * h + k] = 1.0
            selk[k * LANES + h, AUG_STRIDE * h + 3 + k] = -1.0
            oneq[0, AUG_STRIDE * h + 3 + k] = 1.0
            onek[0, AUG_STRIDE * h + k] = 1.0
    return (jnp.asarray(selq, BF16), jnp.asarray(selk, BF16), jnp.asarray(oneq), jnp.asarray(onek))


def _inproj(x, n_seq, w, seq_minor):
    t_total = x.shape[0]
    tm = 512
    n_tiles = t_total // tm
    seq = t_total // n_seq
    tps = seq // tm
    row = lambda i: (i, 0)
    flat = lambda dt: (jax.ShapeDtypeStruct((t_total, WIDTH), dt), pl.BlockSpec((tm, WIDTH), row))
    if seq_minor:
        assert tm == WINDOW_ROWS
        band_f32 = (jax.ShapeDtypeStruct((n_seq, N_HEADS, HEAD_DIM, WINDOW_ROWS), F32),
                    pl.BlockSpec((None, N_HEADS, HEAD_DIM, tm), lambda i: (i // tps, 0, 0, 0)))
        fox_f32 = (jax.ShapeDtypeStruct((n_seq, N_HEADS, HEAD_DIM, seq), F32),
                   pl.BlockSpec((None, N_HEADS, HEAD_DIM, tm), lambda i: (i // tps, 0, 0, i % tps)))
        logf = (jax.ShapeDtypeStruct((n_seq, N_HEADS, seq), F32),
                pl.BlockSpec((None, N_HEADS, tm), lambda i: (i // tps, 0, i % tps)))
    else:
        band_f32 = fox_f32 = flat(F32)
        logf = (jax.ShapeDtypeStruct((t_total, N_HEADS), F32), pl.BlockSpec((tm, N_HEADS), row))
    cat = (jax.ShapeDtypeStruct((n_seq, N_PAIRS, seq, 2 * LANES), BF16),
           pl.BlockSpec((None, N_PAIRS, tm, 2 * LANES), lambda i: (i // tps, 0, i % tps, 0)))
    outs = [flat(BF16), flat(BF16), flat(BF16), band_f32, band_f32, cat, cat, flat(BF16),
            fox_f32, fox_f32, logf]
    bd = jnp.asarray(np.kron(np.eye(N_HEADS), np.ones((HEAD_DIM, HEAD_DIM))), BF16)
    ltri = jnp.asarray(np.tril(np.ones((tm, tm))), BF16)
    consts = [w["g_mix"], w["w_qkv"], w["w_f"], w["b_f"], w["gains"], bd, ltri, *_aug_constants()]
    in_specs = [pl.BlockSpec((tm, D_MODEL), row)] + [_const_spec(c.shape) for c in consts]
    return pl.pallas_call(
        functools.partial(_inproj_kernel, tiles_per_seq=tps, seq_minor=seq_minor),
        grid=(n_tiles,),
        in_specs=in_specs,
        out_specs=[o[1] for o in outs],
        out_shape=[o[0] for o in outs],
        scratch_shapes=[pltpu.VMEM((8, LANES), F32)],
        compiler_params=_params(1),
        name="inproj",
    )(x, *consts)


def _softmax_pv(scores, values):
    chunks = [[s[:, c * LANES:(c + 1) * LANES] for c in range(s.shape[1] // LANES)] for s in scores]
    cmax = None
    for cs in chunks:
        for c in cs:
            cmax = c if cmax is None else jnp.maximum(cmax, c)
    m = jnp.max(cmax, axis=-1, keepdims=True)
    pv = None
    lsum = None
    for cs, v in zip(chunks, values):
        ps = [jnp.exp2(c - m) for c in cs]
        for p in ps:
            lsum = p if lsum is None else lsum + p
        o = _dot(jnp.concatenate(ps, axis=-1).astype(BF16), v)
        pv = o if pv is None else pv + o
    return pv, jnp.sum(lsum, axis=-1, keepdims=True)


def _head_mask(hh):
    lane = lax.broadcasted_iota(jnp.int32, (1, LANES), 1)
    return (lane // HEAD_DIM) == hh


def _band_prompt_kernel(q_ref, k0_ref, k1_ref, k2_ref, k3_ref, v0_ref, v1_ref, v2_ref, v3_ref, e_ref, mask_ref,
                        o_ref, bias_ref, *, tq):
    i = pl.program_id(1)

    @pl.when((pl.program_id(0) == 0) & (i == 0))
    def _():
        _build_bias(e_ref, mask_ref, bias_ref)

    k_refs = (k0_ref, k1_ref, k2_ref, k3_ref)
    v_refs = (v0_ref, v1_ref, v2_ref, v3_ref)
    for qb in range(2):
        pad = [jnp.where(2 * i + qb - 2 + j >= 0, 0.0, NEG) for j in range(2)] + [0.0]
        for p in range(N_PAIRS):
            cols = slice(p * LANES, (p + 1) * LANES)
            q = q_ref[qb * tq:(qb + 1) * tq, cols]
            ks = [r[:, cols] for r in k_refs[qb:qb + 3]]
            vs = [r[:, cols] for r in v_refs[qb:qb + 3]]
            out = jnp.zeros((tq, LANES), F32)
            for hh in range(2):
                msk = _head_mask(hh)
                qh = jnp.where(msk, q, jnp.zeros_like(q))
                scores = [_dot_nt(qh, ks[j]) + bias_ref[2 * p + hh, :, j * tq:(j + 1) * tq] + pad[j]
                          for j in range(3)]
                pv, l = _softmax_pv(scores, vs)
                out = jnp.where(msk, pv / l, out)
            o_ref[qb * tq:(qb + 1) * tq, cols] = out.astype(BF16)


def _band_prompt(qa, ka, va, bias_e, bias_mask, n_seq):
    t_total = qa.shape[0]
    tq = 256
    seq = t_total // n_seq
    nq = seq // tq
    qmap = lambda b, i: (b * (nq // 2) + i, 0)

    def kmap(j):
        return lambda b, i: (b * nq + jnp.maximum(2 * i - 2 + j, 0), 0)

    blk = lambda m: pl.BlockSpec((tq, WIDTH), m)
    qblk = pl.BlockSpec((2 * tq, WIDTH), qmap)
    return pl.pallas_call(
        functools.partial(_band_prompt_kernel, tq=tq),
        grid=(n_seq, nq // 2),
        in_specs=[qblk] + [blk(kmap(j)) for j in range(4)] + [blk(kmap(j)) for j in range(4)]
        + [_const_spec(bias_e.shape), _const_spec(bias_mask.shape)],
        out_specs=qblk,
        out_shape=jax.ShapeDtypeStruct((t_total, WIDTH), BF16),
        scratch_shapes=[pltpu.VMEM((N_HEADS,) + bias_mask.shape, F32)],
        compiler_params=_params(2),
        name="band_prompt",
    )(qa, ka, ka, ka, ka, va, va, va, va, bias_e, bias_mask)


def _band_sample_kernel(q_ref, kc_ref, vc_ref, kn_ref, vn_ref, e_ref, mask_ref, o_ref, ko_ref, vo_ref,
                        bias_ref):
    p_rows = kc_ref.shape[2]
    t = q_ref.shape[0]

    @pl.when(pl.program_id(0) == 0)
    def _():
        _build_bias(e_ref, mask_ref, bias_ref)

    pad = jnp.zeros((LANES - t, HEAD_DIM), BF16)
    for h in range(N_HEADS):
        cols = slice(h * HEAD_DIM, (h + 1) * HEAD_DIM)
        q = q_ref[:, cols]
        k_new, v_new = kn_ref[:, cols], vn_ref[:, cols]
        k_cache, v_cache = kc_ref[h], vc_ref[h]
        s_cache = _dot(q, k_cache.astype(BF16)) + bias_ref[h, :, 0:p_rows]
        s_new = (_dot_nt(q, jnp.concatenate([k_new.astype(BF16), pad], axis=0))
                 + bias_ref[h, :, p_rows:p_rows + LANES])
        chunks = [s_cache[:, c * LANES:(c + 1) * LANES] for c in range(p_rows // LANES)] + [s_new]
        cmax = chunks[0]
        for c in chunks[1:]:
            cmax = jnp.maximum(cmax, c)
        m = jnp.max(cmax, axis=-1, keepdims=True)
        ps = [jnp.exp2(c - m) for c in chunks]
        lsum = ps[0]
        for p in ps[1:]:
            lsum = lsum + p
        pv = (_dot_nt(jnp.concatenate(ps[:-1], axis=-1).astype(BF16), v_cache.astype(BF16))
              + _dot(ps[-1].astype(BF16), jnp.concatenate([v_new.astype(BF16), pad], axis=0)))
        o_ref[:, cols] = (pv / jnp.sum(lsum, axis=-1, keepdims=True)).astype(BF16)
        ko_ref[h] = jnp.concatenate([k_cache[:, t:], k_new.T], axis=1)
        vo_ref[h] = jnp.concatenate([v_cache[:, t:], v_new.T], axis=1)


def _band_sample(qa, ka_f, va_f, cache_kt, cache_vt, bias_e, bias_mask):
    n = cache_kt.shape[0]
    t = qa.shape[0] // n
    row = pl.BlockSpec((t, WIDTH), lambda b: (b, 0))
    cache = pl.BlockSpec((None,) + cache_kt.shape[1:], lambda b: (b, 0, 0, 0))
    return pl.pallas_call(
        _band_sample_kernel,
        grid=(n,),
        in_specs=[row, cache, cache, row, row, _const_spec(bias_e.shape), _const_spec(bias_mask.shape)],
        out_specs=[row, cache, cache],
        out_shape=[jax.ShapeDtypeStruct(qa.shape, BF16),
                   jax.ShapeDtypeStruct(cache_kt.shape, F32),
                   jax.ShapeDtypeStruct(cache_vt.shape, F32)],
        scratch_shapes=[pltpu.VMEM((N_HEADS,) + bias_mask.shape, F32)],
        compiler_params=_params(1),
        name="band_sample",
    )(qa, cache_kt, cache_vt, ka_f, va_f, bias_e, bias_mask)


def _band_bias(rel_bias, n_q, n_k, offset, band_mask, valid_k):
    assert n_q + n_k - 1 <= BIAS_PERIOD
    m = np.arange(BIAS_PERIOD)
    w = np.where(m < n_k, m, m - BIAS_PERIOD)
    idx = np.clip(offset - w, -MAX_REL, MAX_REL) + MAX_REL
    e = jnp.take(rel_bias.astype(F32), jnp.asarray(idx), axis=1) * LOG2E
    qi = np.arange(n_q)[:, None]
    kj = np.arange(n_k)[None, :]
    valid = np.broadcast_to(kj < valid_k, (n_q, n_k))
    if band_mask:
        rel_chunk = (qi + offset) // CHUNK - kj // CHUNK
        valid = valid & (rel_chunk >= 0) & (rel_chunk <= LEFT_CHUNKS)
    return e, jnp.asarray(np.where(valid, 0.0, NEG), F32)


def _build_bias(e_ref, mask_ref, bias_sc):
    rows, cols = mask_ref.shape
    for h in range(N_HEADS):
        spread = jnp.broadcast_to(e_ref[h:h + 1, :], (rows, BIAS_PERIOD))
        bias_sc[h] = pltpu.roll(spread, 0, 1, stride=1, stride_axis=0)[:, :cols] + mask_ref[...]


def _aug_head_mask(h_in_pair, pair):
    lane = lax.broadcasted_iota(jnp.int32, (1, 2 * LANES), 1)
    head = 2 * pair + h_in_pair
    in_q = (lane < LANES) & ((lane // HEAD_DIM) == h_in_pair)
    in_aug = (lane >= LANES) & (((lane - LANES) // AUG_STRIDE) == head)
    return in_q | in_aug


def _flash_step(state, s, v, row_bias=None, v_seq_minor=False):
    m_prev, l_prev, acc_prev = state
    chunks = [s[:, c * LANES:(c + 1) * LANES] for c in range(s.shape[1] // LANES)]
    if row_bias is not None:
        chunks = [c + row_bias for c in chunks]
    cmax = chunks[0]
    for c in chunks[1:]:
        cmax = jnp.maximum(cmax, c)
    m_new = jnp.maximum(m_prev, jnp.max(cmax, axis=-1, keepdims=True))
    alpha = jnp.exp2(m_prev - m_new)
    ps = [jnp.exp2(c - m_new) for c in chunks]
    lsum = ps[0]
    for p in ps[1:]:
        lsum = lsum + p
    p = jnp.concatenate(ps, axis=-1).astype(BF16)
    pv = _dot_nt(p, v) if v_seq_minor else _dot(p, v)
    return m_new, alpha * l_prev + lsum, alpha[:, :pv.shape[1]] * acc_prev + pv


def _flash_update(s, v, m_ref, l_ref, acc_ref, h, row_bias=None, v_seq_minor=False, rows=slice(None)):
    state = (m_ref[h, rows], l_ref[h, rows], acc_ref[h, rows])
    m_ref[h, rows], l_ref[h, rows], acc_ref[h, rows] = _flash_step(state, s, v, row_bias, v_seq_minor)


def _flash_result(l_ref, acc_ref, pair):
    outs = [acc_ref[2 * pair + hh] / jnp.sum(l_ref[2 * pair + hh], axis=-1, keepdims=True)
            for hh in range(2)]
    return jnp.where(_head_mask(0), outs[0], outs[1])


def _fox_prompt_kernel(qi_ref, kj_ref, q_ref, k_ref, v_ref, o_ref, qh_sc, m_sc, l_sc, acc_sc, *, tq, tk):
    t = pl.program_id(1)
    i = qi_ref[t]
    j = kj_ref[t]

    @pl.when(j == 0)
    def _():
        for p in range(N_PAIRS):
            q = q_ref[p]
            for hh in range(2):
                qh_sc[2 * p + hh] = jnp.where(_aug_head_mask(hh, p), q, jnp.zeros_like(q))
        m_sc[...] = jnp.full_like(m_sc, NEG)
        l_sc[...] = jnp.zeros_like(l_sc)
        acc_sc[...] = jnp.zeros_like(acc_sc)

    def step(diagonal):
        half = tq // 2
        if diagonal:
            row = lax.broadcasted_iota(jnp.int32, (half, tk), 0)
            col = lax.broadcasted_iota(jnp.int32, (half, tk), 1)
            keep_first = (col <= row)[:, 0:half]
            keep_second = col <= row + half
        for p in range(N_PAIRS):
            cols = slice(p * LANES, (p + 1) * LANES)
            for hh in range(2):
                h = 2 * p + hh
                if diagonal:
                    s = _dot_nt(qh_sc[h, 0:half, :], k_ref[p, 0:half, :])
                    _flash_update(jnp.where(keep_first, s, NEG), v_ref[0:half, cols], m_sc, l_sc, acc_sc, h,
                                  rows=slice(0, half))
                    s = _dot_nt(qh_sc[h, half:tq, :], k_ref[p])
                    _flash_update(jnp.where(keep_second, s, NEG), v_ref[:, cols], m_sc, l_sc, acc_sc, h,
                                  rows=slice(half, tq))
                else:
                    _flash_update(_dot_nt(qh_sc[h], k_ref[p]), v_ref[:, cols], m_sc, l_sc, acc_sc, h)

    @pl.when(j < i)
    def _():
        step(False)

    @pl.when(j == i)
    def _():
        step(True)
        for p in range(N_PAIRS):
            o_ref[:, p * LANES:(p + 1) * LANES] = _flash_result(l_sc, acc_sc, p).astype(BF16)


def _fox_prompt(qcat, kcat, vb, n_seq):
    seq = qcat.shape[2]
    tq = tk = 1024
    nt = seq // tq
    qi = np.concatenate([np.full(i + 1, i) for i in range(nt)]).astype(np.int32)
    kj = np.concatenate([np.arange(i + 1) for i in range(nt)]).astype(np.int32)
    grid_spec = pltpu.PrefetchScalarGridSpec(
        num_scalar_prefetch=2,
        grid=(n_seq, len(qi)),
        in_specs=[
            pl.BlockSpec((None, N_PAIRS, tq, 2 * LANES), lambda b, t, qi, kj: (b, 0, qi[t], 0)),
            pl.BlockSpec((None, N_PAIRS, tk, 2 * LANES), lambda b, t, qi, kj: (b, 0, kj[t], 0)),
            pl.BlockSpec((tk, WIDTH), lambda b, t, qi, kj: (b * nt + kj[t], 0)),
        ],
        out_specs=pl.BlockSpec((tq, WIDTH), lambda b, t, qi, kj: (b * nt + qi[t], 0)),
        scratch_shapes=[pltpu.VMEM((N_HEADS, tq, 2 * LANES), BF16), pltpu.VMEM((N_HEADS, tq, LANES), F32),
                        pltpu.VMEM((N_HEADS, tq, LANES), F32), pltpu.VMEM((N_HEADS, tq, LANES), F32)],
    )
    return pl.pallas_call(
        functools.partial(_fox_prompt_kernel, tq=tq, tk=tk),
        grid_spec=grid_spec,
        out_shape=jax.ShapeDtypeStruct(vb.shape, BF16),
        compiler_params=_params(2),
        name="fox_prompt",
    )(jnp.asarray(qi), jnp.asarray(kj), qcat, kcat, vb)


def _suffix_sum_exclusive(x):
    n = x.shape[1]
    lane = lax.broadcasted_iota(jnp.int32, x.shape, 1)
    y = jnp.where(lane + 1 < n, pltpu.roll(x, n - 1, axis=1), 0.0)
    shift = 1
    while shift < n:
        y = y + jnp.where(lane + shift < n, pltpu.roll(y, n - shift, axis=1), 0.0)
        shift *= 2
    return y


def _fox_sample_kernel(q_ref, kc_ref, vc_ref, kn_ref, vn_ref, lfc_ref, lfn_ref, u_ref,
                       o_ref, cq_sc, car_sc, m_sc, l_sc, acc_sc, *, n_cache_tiles):
    j = pl.program_id(1)
    t = q_ref.shape[1]

    def cum_new():
        hi, mid, lo = _split3(lfn_ref[...])
        u = u_ref[...]
        return _dot(hi, u) + _dot(mid, u) + _dot(lo, u)

    def q_head(h):
        return q_ref[h // 2, :, (h % 2) * HEAD_DIM:(h % 2 + 1) * HEAD_DIM]

    @pl.when(j == 0)
    def _():
        cn = cum_new() * LOG2E
        eye = (lax.broadcasted_iota(jnp.int32, (t, LANES), 0)
               == lax.broadcasted_iota(jnp.int32, (t, LANES), 1))
        for h in range(N_HEADS):
            col = jnp.sum(jnp.where(eye, jnp.broadcast_to(cn[h:h + 1, :], (t, LANES)), 0.0),
                          axis=-1, keepdims=True)
            cq_sc[h] = jnp.broadcast_to(col, (t, LANES))
        car_sc[...] = jnp.zeros_like(car_sc)
        m_sc[...] = jnp.full_like(m_sc, NEG)
        l_sc[...] = jnp.zeros_like(l_sc)
        acc_sc[...] = jnp.zeros_like(acc_sc)

    lf = lfc_ref[...]
    r = _suffix_sum_exclusive(lf) + car_sc[:, 0:1]
    car_sc[...] = jnp.broadcast_to(r[:, 0:1] + lf[:, 0:1], car_sc.shape)
    r = r * LOG2E
    for h in range(N_HEADS):
        s = _dot(q_head(h), kc_ref[h].astype(BF16)) + r[h:h + 1, :]
        _flash_update(s, vc_ref[h].astype(BF16), m_sc, l_sc, acc_sc, h, row_bias=cq_sc[h], v_seq_minor=True)

    @pl.when(j == n_cache_tiles - 1)
    def _():
        cn = cum_new() * LOG2E
        causal = (lax.broadcasted_iota(jnp.int32, (t, LANES), 1)
                  <= lax.broadcasted_iota(jnp.int32, (t, LANES), 0))
        pad = jnp.zeros((LANES - t, HEAD_DIM), BF16)
        for h in range(N_HEADS):
            cols = slice(h * HEAD_DIM, (h + 1) * HEAD_DIM)
            k = jnp.concatenate([kn_ref[h // 2, :, (h % 2) * HEAD_DIM:(h % 2 + 1) * HEAD_DIM], pad], axis=0)
            v = jnp.concatenate([vn_ref[:, cols], pad], axis=0)
            s = _dot_nt(q_head(h), k) + cq_sc[h] - cn[h:h + 1, :]
            _flash_update(jnp.where(causal, s, NEG), v, m_sc, l_sc, acc_sc, h)
            o_ref[:, cols] = (acc_sc[h] / jnp.sum(l_sc[h], axis=-1, keepdims=True)).astype(BF16)


def _fox_sample(qcat, kcat, vb, cache_kt, cache_vt, lf_cache_t, lf_new_t):
    n, _, _, p_rows = cache_kt.shape
    t = vb.shape[0] // n
    tk = 4096
    nct = p_rows // tk
    u = jnp.asarray(np.pad(np.triu(np.ones((t, t))), ((0, 0), (0, LANES - t))), BF16)
    rev = lambda j: nct - 1 - j
    cache = pl.BlockSpec((None, N_HEADS, HEAD_DIM, tk), lambda b, j: (b, 0, 0, rev(j)))
    new_cat = pl.BlockSpec((None, N_PAIRS, t, 2 * LANES), lambda b, j: (0, 0, b, 0))
    return pl.pallas_call(
        functools.partial(_fox_sample_kernel, n_cache_tiles=nct),
        grid=(n, nct),
        in_specs=[
            new_cat, cache, cache, new_cat,
            pl.BlockSpec((t, WIDTH), lambda b, j: (b, 0)),
            pl.BlockSpec((None, N_HEADS, tk), lambda b, j: (b, 0, rev(j))),
            pl.BlockSpec((None, N_HEADS, t), lambda b, j: (b, 0, 0)),
            _const_spec(u.shape),
        ],
        out_specs=pl.BlockSpec((t, WIDTH), lambda b, j: (b, 0)),
        out_shape=jax.ShapeDtypeStruct(vb.shape, BF16),
        scratch_shapes=[pltpu.VMEM((N_HEADS, t, LANES), F32), pltpu.VMEM((N_HEADS, LANES), F32),
                        pltpu.VMEM((N_HEADS, t, LANES), F32), pltpu.VMEM((N_HEADS, t, LANES), F32),
                        pltpu.VMEM((N_HEADS, t, HEAD_DIM), F32)],
        compiler_params=_params(2),
        name="fox_sample",
    )(qcat, cache_kt, cache_vt, kcat, vb, lf_cache_t, lf_new_t, u)


GROUP_LANE = 64
ROW_ALIGN = 16
MOE_TILE = 512
SORT_ROWS = 640
XS_WIDTH = D_MODEL + 2 * LANES


def _route(r):
    lane_i = lax.broadcasted_iota(jnp.int32, r.shape, 1)
    lane = lane_i.astype(F32)
    lane_grp = (lane_i // EXPERTS_PER_GROUP).astype(F32)
    big = float(LANES)
    is_coarse = (lane_i >= N_EXPERTS) & (lane_i < N_EXPERTS + N_GROUPS)
    cm = jnp.where(is_coarse, r, NEG)
    cmax = cm.max(axis=-1, keepdims=True)
    grp = jnp.min(jnp.where(cm == cmax, lane - N_EXPERTS, big), axis=-1, keepdims=True)
    pg_sel = 1.0 / jnp.sum(jnp.exp(cm - cmax), axis=-1, keepdims=True)
    in_grp = (lane_i < N_EXPERTS) & (lane_grp == grp)
    fm = jnp.where(in_grp, r, NEG)
    m1 = fm.max(axis=-1, keepdims=True)
    denom = jnp.sum(jnp.exp(fm - m1), axis=-1, keepdims=True)
    i1 = jnp.min(jnp.where(fm == m1, lane, big), axis=-1, keepdims=True)
    fm2 = jnp.where(lane == i1, NEG, fm)
    m2 = fm2.max(axis=-1, keepdims=True)
    i2 = jnp.min(jnp.where(fm2 == m2, lane, big), axis=-1, keepdims=True)
    p1 = 1.0 / denom
    p2 = jnp.exp(m2 - m1) / denom
    tot = p1 + p2
    comb = (jnp.where(lane == i1, pg_sel * (p1 / tot), 0.0)
            + jnp.where(lane == i2, pg_sel * (p2 / tot), 0.0))
    return comb, grp


def _merge_kernel(x1_ref, oa1_ref, ob1_ref, x2_ref, oa2_ref, ob2_ref, g_ref, wg_ref, wpa_ref, wpb_ref, wo_ref,
                  gf_ref, wrh_ref, wrm_ref, br_ref, y_ref, hx_ref, route_ref, grp_t_ref, cnt_ref,
                  x_sc, oa_sc, ob_sc, *, tiles_first):
    i = pl.program_id(0)

    @pl.when(i < tiles_first)
    def _():
        x_sc[...], oa_sc[...], ob_sc[...] = x1_ref[...], oa1_ref[...], ob1_ref[...]

    @pl.when(i >= tiles_first)
    def _():
        x_sc[...], oa_sc[...], ob_sc[...] = x2_ref[...], oa2_ref[...], ob2_ref[...]

    x = x_sc[...]
    h = _rms(x, g_ref[...]).astype(BF16)
    gate = jax.nn.sigmoid(_dot(h, wg_ref[...]))
    mix = (gate[:, :D_MODEL] * _dot(oa_sc[...], wpa_ref[...])
           + gate[:, D_MODEL:] * _dot(ob_sc[...], wpb_ref[...]))
    y = x + _dot(mix.astype(BF16), wo_ref[...])
    y_ref[...] = y

    hx = _rms(y, gf_ref[...])
    hx_ref[:, 0:D_MODEL] = hx.astype(BF16)
    h_hi, h_mid, _ = _split3(hx)
    r = _dot(h_hi, wrh_ref[...]) + _dot(h_hi, wrm_ref[...]) + _dot(h_mid, wrh_ref[...]) + br_ref[...]
    comb, grp = _route(r)
    lane = lax.broadcasted_iota(jnp.int32, comb.shape, 1)
    route = jnp.where(lane == GROUP_LANE, grp, comb)
    route_ref[...] = route
    r_hi = route.astype(BF16)
    hx_ref[:, D_MODEL:D_MODEL + LANES] = r_hi
    hx_ref[:, D_MODEL + LANES:] = (route - r_hi.astype(F32)).astype(BF16)
    grp_t_ref[...] = route.T[GROUP_LANE:GROUP_LANE + 8, :]
    cnt = jnp.sum(jnp.where(lane.astype(F32) == grp, 1.0, 0.0), axis=0, keepdims=True)
    cnt_ref[...] = jnp.broadcast_to(cnt, cnt_ref.shape)


def _two_stream_maps(tiles_first):
    first = lambda i, *_: (jnp.minimum(i, tiles_first - 1), 0)
    second = lambda i, *_: (jnp.maximum(i - tiles_first, 0), 0)
    return first, second


def _merge(x1, oa1, ob1, x2, oa2, ob2, w):
    tm = MOE_TILE
    tiles_first = x1.shape[0] // tm
    t_total = x1.shape[0] + x2.shape[0]
    n_tiles = t_total // tm
    first, second = _two_stream_maps(tiles_first)
    row = lambda n: pl.BlockSpec((tm, n), lambda i: (i, 0))
    stream = lambda m: [pl.BlockSpec((tm, D_MODEL), m), pl.BlockSpec((tm, WIDTH), m), pl.BlockSpec((tm, WIDTH), m)]
    consts = [w["g_mix"], w["w_gate"], w["w_pa"], w["w_pb"], w["w_o"], w["g_ffn"], w["w_router_hi"],
              w["w_router_mid"], w["b_router"]]
    return pl.pallas_call(
        functools.partial(_merge_kernel, tiles_first=tiles_first),
        grid=(n_tiles,),
        in_specs=stream(first) + stream(second) + [_const_spec(c.shape) for c in consts],
        out_specs=[row(D_MODEL), row(XS_WIDTH), row(LANES), pl.BlockSpec((8, tm), lambda i: (0, i)),
                   pl.BlockSpec((None, 8, LANES), lambda i: (i, 0, 0))],
        out_shape=[jax.ShapeDtypeStruct((t_total, D_MODEL), F32), jax.ShapeDtypeStruct((t_total, XS_WIDTH), BF16),
                   jax.ShapeDtypeStruct((t_total, LANES), F32), jax.ShapeDtypeStruct((8, t_total), F32),
                   jax.ShapeDtypeStruct((n_tiles, 8, LANES), F32)],
        scratch_shapes=[pltpu.VMEM((tm, D_MODEL), F32), pltpu.VMEM((tm, WIDTH), BF16),
                        pltpu.VMEM((tm, WIDTH), BF16)],
        compiler_params=_params(1),
        name="merge",
    )(x1, oa1, ob1, x2, oa2, ob2, *consts)


def _plan_kernel(cnt_ref, ib_ref, dst_ref, len_ref, fill_ref, inblk_ref, outblk_ref, grp_ref, kind_ref, *,
                 n_tiles, n_steps):
    align_bits = ROW_ALIGN.bit_length() - 1
    tile_bits = MOE_TILE.bit_length() - 1
    zero = jnp.int32(0)

    def per_tile(t, tot):
        run = zero
        new_tot = []
        for g in range(N_GROUPS):
            n = ((cnt_ref[t * N_GROUPS + g] + (ROW_ALIGN - 1)) >> align_bits) << align_bits
            ib_ref[t * N_GROUPS + g] = run
            len_ref[t * N_GROUPS + g] = n
            dst_ref[t * N_GROUPS + g] = tot[g]
            run = run + n
            new_tot.append(tot[g] + n)
        return tuple(new_tot)

    tot = lax.fori_loop(0, n_tiles, per_tile, (zero,) * N_GROUPS)
    n_ffn = [(tot[g] + (MOE_TILE - 1)) >> tile_bits for g in range(N_GROUPS)]
    g_base, ends, acc_rows, acc_tiles = [], [], zero, zero
    for g in range(N_GROUPS):
        g_base.append(acc_rows)
        acc_rows = acc_rows + ((n_ffn[g] + 1) << tile_bits)
        acc_tiles = acc_tiles + n_ffn[g] + 1
        ends.append(acc_tiles)

    def add_base(t, carry):
        for g in range(N_GROUPS):
            dst_ref[t * N_GROUPS + g] = dst_ref[t * N_GROUPS + g] + g_base[g]
        return carry

    lax.fori_loop(0, n_tiles, add_base, zero)
    for g in range(N_GROUPS):
        fill_ref[g] = g_base[g] + tot[g]
        fill_ref[N_GROUPS + g] = g_base[g] + (n_ffn[g] << tile_bits)
    fill_ref[2 * N_GROUPS] = ends[-1]
    first_blk = g_base[N_GROUPS - 1]
    for g in reversed(range(N_GROUPS - 1)):
        first_blk = jnp.where(n_ffn[g] > 0, g_base[g], first_blk)
    first_blk = first_blk >> tile_bits

    def pick(vals, g):
        out = vals[N_GROUPS - 1]
        for i in reversed(range(N_GROUPS - 1)):
            out = jnp.where(g == i, vals[i], out)
        return out

    def per_step(k, carry):
        kk = jnp.minimum(k, ends[-1] - 1)
        g = zero
        for i in range(N_GROUPS - 1):
            g = g + jnp.where(kk >= ends[i], 1, 0)
        j = kk - pick([zero] + ends[:-1], g)
        is_expert = j < pick(n_ffn, g)
        out_blk = jnp.where(k < ends[-1], (pick(g_base, g) >> tile_bits) + j, k)
        outblk_ref[k] = out_blk
        inblk_ref[k] = jnp.where(is_expert, out_blk, first_blk)
        grp_ref[k] = g
        kind_ref[k] = jnp.where(is_expert, 1, 0)
        return carry

    lax.fori_loop(0, n_steps, per_step, zero)


def _moe_plan(cnt, t_total):
    n_tiles = t_total // MOE_TILE
    n_steps = (n_tiles + 2 * N_GROUPS + (N_GROUPS * (ROW_ALIGN - 1) * n_tiles + MOE_TILE - 1) // MOE_TILE)
    smem = pl.BlockSpec(memory_space=pltpu.SMEM)
    sizes = [n_tiles * N_GROUPS] * 3 + [2 * N_GROUPS + 1] + [n_steps] * 4
    outs = pl.pallas_call(
        functools.partial(_plan_kernel, n_tiles=n_tiles, n_steps=n_steps),
        grid_spec=pltpu.PrefetchScalarGridSpec(num_scalar_prefetch=1, grid=(1,), in_specs=[],
                                               out_specs=[smem] * len(sizes)),
        out_shape=[jax.ShapeDtypeStruct((n,), jnp.int32) for n in sizes],
        compiler_params=_params(1),
        name="moe_plan",
    )(cnt.reshape(-1))
    return (*outs, n_steps)


def _sort_kernel(ib_ref, dst_ref, len_ref, fill_ref, hx_ref, grp_t_ref, tri_ref, xs_ref, xsort, sems):
    i = pl.program_id(0)
    n = pl.num_programs(0)
    tm = hx_ref.shape[0]
    half = tm // 2
    slot = i % 2

    def copy(src, dst, rows, sl, g, part):
        return pltpu.make_async_copy(xsort.at[sl, pl.ds(pl.multiple_of(src, ROW_ALIGN), rows)],
                                     xs_ref.at[pl.ds(pl.multiple_of(dst, ROW_ALIGN), rows)], sems.at[sl, g, part])

    def for_segments(step, sl, act):
        for g in range(N_GROUPS):
            src, dst = ib_ref[step * N_GROUPS + g], dst_ref[step * N_GROUPS + g]
            act(copy(src, dst, half, sl, g, 0))

            @pl.when(len_ref[step * N_GROUPS + g] > half)
            def _():
                act(copy(src + half, dst + half, half, sl, g, 1))

    def run(cs):
        for c in cs:
            c.start()
        for c in cs:
            c.wait()

    @pl.when(i == 0)
    def _():
        for sl in range(2):
            xsort[sl, SORT_ROWS:, :] = jnp.zeros((xsort.shape[1] - SORT_ROWS, XS_WIDTH), BF16)

    g_row = grp_t_ref[0:1, :]
    sub = lax.broadcasted_iota(jnp.int32, (8, tm), 0).astype(F32)
    mine = sub == g_row
    before = _dot(jnp.where(mine, 1.0, 0.0).astype(BF16), tri_ref[...])
    dest = jnp.sum(jnp.where(mine, before, 0.0), axis=0, keepdims=True)
    for g in range(N_GROUPS):
        base = jnp.full(dest.shape, ib_ref[i * N_GROUPS + g], jnp.int32).astype(F32)
        dest = dest + jnp.where(g_row == g, base, 0.0)
    rows = lax.broadcasted_iota(jnp.int32, (SORT_ROWS, tm), 0).astype(F32)
    perm = jnp.where(rows == dest, 1.0, 0.0).astype(BF16)
    xsort[slot, 0:SORT_ROWS, :] = _dot(perm, hx_ref[...]).astype(BF16)

    @pl.when(i > 0)
    def _():
        for_segments(i - 1, 1 - slot, lambda c: c.wait())

    for_segments(i, slot, lambda c: c.start())

    @pl.when(i == n - 1)
    def _():
        for_segments(i, slot, lambda c: c.wait())
        run([copy(SORT_ROWS, fill_ref[g], tm, slot, g, 0) for g in range(N_GROUPS)])
        run([copy(SORT_ROWS, fill_ref[N_GROUPS + g], tm, slot, g, 0) for g in range(N_GROUPS)])

        @pl.loop(fill_ref[2 * N_GROUPS], xs_ref.shape[0] // tm)
        def _(b):
            run([copy(SORT_ROWS, b * tm, tm, slot, 0, 0)])


def _expert_kernel(ib_ref, ob_ref, grp_ref, kind_ref, xs_ref, w1_ref, w3_ref, w2_ref, ys_ref, w1_sc, w3_sc, w2_sc):
    del ib_ref, ob_ref
    k = pl.program_id(0)

    @pl.when(kind_ref[k] == 1)
    def _():
        @pl.when((k == 0) | (grp_ref[k] != grp_ref[jnp.maximum(k - 1, 0)]))
        def _():
            for e in range(EXPERTS_PER_GROUP):
                cols = slice(e * D_EXPERT, (e + 1) * D_EXPERT)
                w1_sc[:, cols] = w1_ref[e].astype(BF16)
                w3_sc[:, cols] = w3_ref[e].astype(BF16)
                w2_sc[cols, :] = w2_ref[e].astype(BF16)

        x = xs_ref[:, 0:D_MODEL]
        hid = jax.nn.silu(_dot(x, w1_sc[...])) * _dot(x, w3_sc[...])
        comb = (xs_ref[:, D_MODEL:D_MODEL + LANES].astype(F32) + xs_ref[:, D_MODEL + LANES:].astype(F32))
        lane = lax.broadcasted_iota(jnp.int32, comb.shape, 1)
        first = grp_ref[k] * EXPERTS_PER_GROUP
        scale = []
        for e in range(EXPERTS_PER_GROUP):
            w_e = jnp.sum(jnp.where(lane == first + e, comb, 0.0), axis=-1, keepdims=True)
            scale.append(jnp.broadcast_to(w_e, (comb.shape[0], D_EXPERT)))
        hid = hid * jnp.concatenate(scale, axis=-1)
        ys_ref[...] = _dot(hid.astype(BF16), w2_sc[...]).astype(BF16)

    @pl.when(kind_ref[k] == 0)
    def _():
        ys_ref[...] = jnp.zeros_like(ys_ref)


def _unsort_kernel(ib_ref, dst_ref, len_ref, y_ref, route_ref, tri_ref, ys_ref, o1_ref, o2_ref, ybuf, yasm, sems,
                   *, tiles_first):
    i = pl.program_id(0)
    n = pl.num_programs(0)
    tm = y_ref.shape[0]
    half = tm // 2
    slot = i % 2

    def fetch(step, sl, act):
        for g in range(N_GROUPS):
            src = pl.multiple_of(dst_ref[step * N_GROUPS + g], ROW_ALIGN)
            act(pltpu.make_async_copy(ys_ref.at[pl.ds(src, half)], ybuf.at[sl, g, 0:half], sems.at[sl, g, 0]))

            @pl.when(len_ref[step * N_GROUPS + g] > half)
            def _():
                act(pltpu.make_async_copy(ys_ref.at[pl.ds(src + half, half)], ybuf.at[sl, g, half:tm],
                                          sems.at[sl, g, 1]))

    @pl.when(i == 0)
    def _():
        yasm[...] = jnp.zeros_like(yasm)
        fetch(0, 0, lambda c: c.start())

    @pl.when(i + 1 < n)
    def _():
        fetch(i + 1, 1 - slot, lambda c: c.start())

    route = route_ref[...]
    lane = lax.broadcasted_iota(jnp.int32, route.shape, 1).astype(F32)
    grp = jnp.sum(jnp.where(lane == GROUP_LANE, route, 0.0), axis=-1, keepdims=True)
    mine = lane == grp
    before = _dot(tri_ref[...], jnp.where(mine, 1.0, 0.0).astype(BF16))
    dest = jnp.sum(jnp.where(mine, before, 0.0), axis=-1, keepdims=True)
    for g in range(N_GROUPS):
        base = jnp.full(dest.shape, ib_ref[i * N_GROUPS + g], jnp.int32).astype(F32)
        dest = dest + jnp.where(grp == g, base, 0.0)
    cols = lax.broadcasted_iota(jnp.int32, (tm, SORT_ROWS), 1).astype(F32)
    perm_t = jnp.where(cols == dest, 1.0, 0.0).astype(BF16)

    fetch(i, slot, lambda c: c.wait())
    for g in range(N_GROUPS):
        row0 = pl.multiple_of(ib_ref[i * N_GROUPS + g], ROW_ALIGN)
        yasm[pl.ds(row0, half), :] = ybuf[slot, g, 0:half]

        @pl.when(len_ref[i * N_GROUPS + g] > half)
        def _():
            yasm[pl.ds(row0 + half, half), :] = ybuf[slot, g, half:tm]
    out = y_ref[...] + _dot(perm_t, yasm[0:SORT_ROWS, :])

    @pl.when(i < tiles_first)
    def _():
        o1_ref[...] = out

    @pl.when(i >= tiles_first)
    def _():
        o2_ref[...] = out


def _moe(y, hx, route, grp_t, cnt, w, rows_first):
    t_total = y.shape[0]
    tm = MOE_TILE
    n_tiles = t_total // tm
    assert SORT_ROWS >= tm + N_GROUPS * (ROW_ALIGN - 1)
    in_base, dst, seg_len, fill, in_blk, out_blk, grp_of_step, is_expert, n_steps = _moe_plan(
        cnt[:, 0, :N_GROUPS].astype(jnp.int32), t_total)
    cap_rows = n_steps * tm
    lower = jnp.asarray(np.tril(np.ones((tm, tm)), -1), BF16)
    upper = jnp.asarray(np.triu(np.ones((tm, tm)), 1), BF16)
    stage_rows = SORT_ROWS + tm
    any_spec = pl.BlockSpec(memory_space=pl.ANY)

    xs = pl.pallas_call(
        _sort_kernel,
        grid_spec=pltpu.PrefetchScalarGridSpec(
            num_scalar_prefetch=4,
            grid=(n_tiles,),
            in_specs=[pl.BlockSpec((tm, XS_WIDTH), lambda i, *_: (i, 0)),
                      pl.BlockSpec((8, tm), lambda i, *_: (0, i)),
                      pl.BlockSpec((tm, tm), lambda i, *_: (0, 0))],
            out_specs=any_spec,
            scratch_shapes=[pltpu.VMEM((2, stage_rows, XS_WIDTH), BF16),
                            pltpu.SemaphoreType.DMA((2, N_GROUPS, 2))],
        ),
        out_shape=jax.ShapeDtypeStruct((cap_rows, XS_WIDTH), BF16),
        compiler_params=_params(1),
        name="moe_sort",
    )(in_base, dst, seg_len, fill, hx, grp_t, upper)

    step_map = lambda which: (lambda k, ib, ob, grp, kind: ((ib, ob, grp)[which][k], 0))
    wmap = lambda k, ib, ob, grp, kind: (grp[k], 0, 0)
    wspec = pl.BlockSpec((EXPERTS_PER_GROUP, D_MODEL, D_EXPERT), wmap)
    ys = pl.pallas_call(
        _expert_kernel,
        grid_spec=pltpu.PrefetchScalarGridSpec(
            num_scalar_prefetch=4,
            grid=(n_steps,),
            in_specs=[pl.BlockSpec((tm, XS_WIDTH), step_map(0)), wspec, wspec, pl.BlockSpec((EXPERTS_PER_GROUP, D_EXPERT, D_MODEL), wmap)],
            out_specs=pl.BlockSpec((tm, D_MODEL), step_map(1)),
            scratch_shapes=[pltpu.VMEM((D_MODEL, EXPERTS_PER_GROUP * D_EXPERT), BF16),
                            pltpu.VMEM((D_MODEL, EXPERTS_PER_GROUP * D_EXPERT), BF16),
                            pltpu.VMEM((EXPERTS_PER_GROUP * D_EXPERT, D_MODEL), BF16)],
        ),
        out_shape=jax.ShapeDtypeStruct((cap_rows, D_MODEL), BF16),
        compiler_params=_params(1),
        name="moe_experts",
    )(in_blk, out_blk, grp_of_step, is_expert, xs, w["w1"], w["w3"], w["w2"])

    first, second = _two_stream_maps(rows_first // tm)
    return pl.pallas_call(
        functools.partial(_unsort_kernel, tiles_first=rows_first // tm),
        grid_spec=pltpu.PrefetchScalarGridSpec(
            num_scalar_prefetch=3,
            grid=(n_tiles,),
            in_specs=[pl.BlockSpec((tm, D_MODEL), lambda i, *_: (i, 0)),
                      pl.BlockSpec((tm, LANES), lambda i, *_: (i, 0)),
                      pl.BlockSpec((tm, tm), lambda i, *_: (0, 0)),
                      any_spec],
            out_specs=[pl.BlockSpec((tm, D_MODEL), first), pl.BlockSpec((tm, D_MODEL), second)],
            scratch_shapes=[pltpu.VMEM((2, N_GROUPS, tm, D_MODEL), BF16),
                            pltpu.VMEM((stage_rows, D_MODEL), BF16),
                            pltpu.SemaphoreType.DMA((2, N_GROUPS, 2))],
        ),
        out_shape=[jax.ShapeDtypeStruct((rows_first, D_MODEL), F32),
                   jax.ShapeDtypeStruct((t_total - rows_first, D_MODEL), F32)],
        compiler_params=_params(1),
        name="moe_unsort",
    )(in_base, dst, seg_len, y, route, lower, ys)


def _prep_weights(g_mix, w_in, b_f, q_norm_a, k_norm_a, q_norm_b, k_norm_b, w_pa, w_pb, w_o,
                  g_ffn, w_rg, b_rg, w_re, b_re, w1, w3, w2):
    n_qkv = 6 * WIDTH
    tile = lambda g: jnp.tile(g, N_HEADS)
    w_router = jnp.concatenate(
        [jnp.transpose(w_re, (1, 0, 2)).reshape(D_MODEL, N_EXPERTS), w_rg,
         jnp.zeros((D_MODEL, LANES - N_EXPERTS - N_GROUPS), F32)], axis=1)
    b_router = jnp.concatenate(
        [b_re.reshape(N_EXPERTS), b_rg, jnp.zeros((LANES - N_EXPERTS - N_GROUPS,), F32)])[None, :]
    return {
        "g_mix": g_mix[None, :],
        "w_qkv": w_in[:, :n_qkv].astype(BF16),
        "w_f": jnp.pad(w_in[:, n_qkv:n_qkv + N_HEADS], ((0, 0), (0, LANES - N_HEADS))).astype(BF16),
        "b_f": jnp.pad(b_f, (0, LANES - N_HEADS))[None, :],
        "w_gate": w_in[:, n_qkv + N_HEADS:].astype(BF16),
        "gains": jnp.stack([tile(q_norm_a), tile(k_norm_a), tile(q_norm_b), tile(k_norm_b)]),
        "w_pa": w_pa.astype(BF16), "w_pb": w_pb.astype(BF16), "w_o": w_o.astype(BF16),
        "g_ffn": g_ffn[None, :],
        "w_router_hi": w_router.astype(BF16),
        "w_router_mid": (w_router - w_router.astype(BF16).astype(F32)).astype(BF16),
        "b_router": b_router,
        "w1": w1, "w3": w3, "w2": w2,
    }


def kernel(x_prompt, x_sample, cache_a_k, cache_a_v, cache_b_k, cache_b_v, cache_b_logf, g_mix, w_in, b_f, q_norm_a, k_norm_a, q_norm_b, k_norm_b, rel_bias, w_pa, w_pb, w_o, g_ffn, w_rg, b_rg, w_re, b_re, w1, w3, w2):
    assert g_mix.shape[0] == 1, "single-layer step"
    n_p, seq, _ = x_prompt.shape
    n_s, t_s, _ = x_sample.shape
    a_rows = cache_a_k.shape[2]
    w = _prep_weights(g_mix[0], w_in[0], b_f[0], q_norm_a[0], k_norm_a[0], q_norm_b[0], k_norm_b[0],
                      w_pa[0], w_pb[0], w_o[0], g_ffn[0], w_rg[0], b_rg[0], w_re[0], b_re[0],
                      w1[0], w3[0], w2[0])
    band_tq = 256
    bias_prompt = _band_bias(rel_bias[0], band_tq, 3 * band_tq, 2 * band_tq, True, 3 * band_tq)
    bias_sample = _band_bias(rel_bias[0], t_s, a_rows + LANES, a_rows, False, a_rows + t_s)

    seq_minor = lambda a: jnp.transpose(a, (0, 2, 3, 1))
    seq_major = lambda a: jnp.transpose(a, (0, 3, 1, 2))[None]

    xp = x_prompt.reshape(n_p * seq, D_MODEL)
    (qa, ka, va, ka_t, va_t, qcat, kcat, vb, kb_t, vb_t, logf_t) = _inproj(xp, n_p, w, seq_minor=True)
    o_a = _band_prompt(qa, ka, va, *bias_prompt, n_p)
    o_b = _fox_prompt(qcat, kcat, vb, n_p)

    xs = x_sample.reshape(n_s * t_s, D_MODEL)
    (qa_s, _, _, ka_fs, va_fs, qcat_s, kcat_s, vb_s, kb_fs, vb_fs, logf_s) = _inproj(
        xs, 1, w, seq_minor=False)
    o_as, new_ak_t, new_av_t = _band_sample(
        qa_s, ka_fs, va_fs, seq_minor(cache_a_k[0]), seq_minor(cache_a_v[0]), *bias_sample)
    lf_cache_t = jnp.transpose(cache_b_logf[0], (0, 2, 1))
    lf_new_t = jnp.transpose(logf_s.reshape(n_s, t_s, N_HEADS), (0, 2, 1))
    o_bs = _fox_sample(qcat_s, kcat_s, vb_s, seq_minor(cache_b_k[0]), seq_minor(cache_b_v[0]),
                       lf_cache_t, lf_new_t)

    y_p, y_s = _moe(*_merge(xp, o_a, o_b, xs, o_as, o_bs, w), w, rows_first=n_p * seq)

    heads = lambda a, n, r: a.reshape(1, n, r, N_HEADS, HEAD_DIM)
    return (y_p.reshape(n_p, seq, D_MODEL), y_s.reshape(n_s, t_s, D_MODEL),
            seq_major(ka_t), seq_major(va_t), seq_major(kb_t), seq_major(vb_t),
            jnp.transpose(logf_t, (0, 2, 1))[None],
            seq_major(new_ak_t), seq_major(new_av_t),
            heads(kb_fs, n_s, t_s), heads(vb_fs, n_s, t_s), logf_s.reshape(1, n_s, t_s, N_HEADS))
```

```python
import functools

import numpy as np
import jax
import jax.numpy as jnp
from jax import lax
from jax.experimental import pallas as pl
from jax.experimental.pallas import tpu as pltpu

F32 = jnp.float32
BF16 = jnp.bfloat16

D_MODEL = 1024
HEAD_DIM = 64
N_HEADS = 8
WIDTH = N_HEADS * HEAD_DIM
N_PAIRS = N_HEADS // 2
CHUNK = 64
LEFT_CHUNKS = 8
WINDOW_ROWS = LEFT_CHUNKS * CHUNK
MAX_REL = 256
N_GROUPS = 4
EXPERTS_PER_GROUP = 8
N_EXPERTS = N_GROUPS * EXPERTS_PER_GROUP
D_EXPERT = 128
EPS = 1e-6
NEG = -1e30
LOG2E = 1.4426950408889634
LANES = 128
BIAS_PERIOD = 1024
AUG_STRIDE = 8
VMEM_LIMIT = 56 * 1024 * 1024

_NT = (((1,), (1,)), ((), ()))


def _dot(a, b):
    return jnp.dot(a, b, preferred_element_type=F32)


def _dot_nt(a, b):
    return lax.dot_general(a, b, _NT, preferred_element_type=F32)


def _split3(x):
    hi = x.astype(BF16)
    r = x - hi.astype(F32)
    mid = r.astype(BF16)
    lo = (r - mid.astype(F32)).astype(BF16)
    return hi, mid, lo


def _dot3(a_bf, x):
    hi, mid, lo = _split3(x)
    return _dot(a_bf, hi) + _dot(a_bf, mid) + _dot(a_bf, lo)


def _rms(x, g):
    ms = jnp.mean(x * x, axis=-1, keepdims=True)
    return x * lax.rsqrt(ms + EPS) * g


def _params(n_axes):
    return pltpu.CompilerParams(dimension_semantics=("arbitrary",) * n_axes,
                                vmem_limit_bytes=VMEM_LIMIT)


def _const_spec(shape):
    nd = len(shape)
    return pl.BlockSpec(shape, lambda *_: (0,) * nd)


def _inproj_kernel(x_ref, g_ref, wqkv_ref, wf_ref, bf_ref, gains_ref, bd_ref, ltri_ref,
                   selq_ref, selk_ref, oneq_ref, onek_ref,
                   qa_ref, ka_ref, va_ref, kaf_ref, vaf_ref, qcat_ref, kcat_ref, vb_ref,
                   kbf_ref, vbf_ref, logf_ref, carry_ref, *, tiles_per_seq, seq_minor):
    i = pl.program_id(0)
    tm = x_ref.shape[0]

    @pl.when(i % tiles_per_seq == 0)
    def _():
        carry_ref[...] = jnp.zeros_like(carry_ref)

    def put_f32(ref, y, last_tile_only=False):
        if not seq_minor:
            ref[...] = y
        elif last_tile_only:
            @pl.when(i % tiles_per_seq == tiles_per_seq - 1)
            def _():
                ref[...] = y.T.reshape(N_HEADS, HEAD_DIM, tm)
        else:
            ref[...] = y.T.reshape(N_HEADS, HEAD_DIM, tm)

    h = _rms(x_ref[...], g_ref[...]).astype(BF16)

    def seg(s):
        return _dot_nt(h, wqkv_ref[s * WIDTH:(s + 1) * WIDTH, :])

    def headnorm(y, n):
        ss = _dot((y * y).astype(BF16), bd_ref[...])
        return y * lax.rsqrt(ss * (1.0 / HEAD_DIM) + EPS) * gains_ref[n:n + 1, :]

    q_a = headnorm(seg(0), 0)
    qa_ref[...] = (q_a * (HEAD_DIM ** -0.5 * LOG2E)).astype(BF16)
    k_a = headnorm(seg(1), 1)
    ka_ref[...] = k_a.astype(BF16)
    put_f32(kaf_ref, k_a, last_tile_only=True)
    v_a = seg(2)
    va_ref[...] = v_a.astype(BF16)
    put_f32(vaf_ref, v_a, last_tile_only=True)

    z = _dot_nt(h, wf_ref[...]) + bf_ref[...]
    logf = jnp.minimum(z, 0.0) - jnp.log(1.0 + jnp.exp(-jnp.abs(z)))
    if seq_minor:
        logf_ref[...] = logf.T[0:N_HEADS, :]
    else:
        logf_ref[...] = logf[:, :N_HEADS]
    c = _dot3(ltri_ref[...], logf) + carry_ref[0:1, :]
    carry_ref[...] = jnp.broadcast_to(c[-1:, :], carry_ref.shape)
    cs = jnp.concatenate(_split3(c * LOG2E), axis=-1)
    q_aug = (_dot(cs, selq_ref[...]) + oneq_ref[...]).astype(BF16)
    k_aug = (_dot(cs, selk_ref[...]) + onek_ref[...]).astype(BF16)

    q_b = (headnorm(seg(3), 2) * (HEAD_DIM ** -0.5 * LOG2E)).astype(BF16)
    k_b = headnorm(seg(4), 3)
    put_f32(kbf_ref, k_b)
    k_b = k_b.astype(BF16)
    v_b = seg(5)
    put_f32(vbf_ref, v_b)
    vb_ref[...] = v_b.astype(BF16)
    for p in range(N_PAIRS):
        cols = slice(p * LANES, (p + 1) * LANES)
        qcat_ref[p, :, 0:LANES] = q_b[:, cols]
        qcat_ref[p, :, LANES:2 * LANES] = q_aug
        kcat_ref[p, :, 0:LANES] = k_b[:, cols]
        kcat_ref[p, :, LANES:2 * LANES] = k_aug


def _aug_constants():
    selq = np.zeros((3 * LANES, LANES), np.float32)
    selk = np.zeros((3 * LANES, LANES), np.float32)
    oneq = np.zeros((1, LANES), np.float32)
    onek = np.zeros((1, LANES), np.float32)
    for h in range(N_HEADS):
        for k in range(3):
            selq[k * LANES + h, AUG_STRIDE * h + k] = 1.0
            selk[k * LANES + h, AUG_STRIDE * h + 3 + k] = -1.0
            oneq[0, AUG_STRIDE * h + 3 + k] = 1.0
            onek[0, AUG_STRIDE * h + k] = 1.0
    return (jnp.asarray(selq, BF16), jnp.asarray(selk, BF16), jnp.asarray(oneq), jnp.asarray(onek))


def _inproj(x, n_seq, w, seq_minor):
    t_total = x.shape[0]
    tm = 512
    n_tiles = t_total // tm
    seq = t_total // n_seq
    tps = seq // tm
    row = lambda i: (i, 0)
    flat = lambda dt: (jax.ShapeDtypeStruct((t_total, WIDTH), dt), pl.BlockSpec((tm, WIDTH), row))
    if seq_minor:
        assert tm == WINDOW_ROWS
        band_f32 = (jax.ShapeDtypeStruct((n_seq, N_HEADS, HEAD_DIM, WINDOW_ROWS), F32),
                    pl.BlockSpec((None, N_HEADS, HEAD_DIM, tm), lambda i: (i // tps, 0, 0, 0)))
        fox_f32 = (jax.ShapeDtypeStruct((n_seq, N_HEADS, HEAD_DIM, seq), F32),
                   pl.BlockSpec((None, N_HEADS, HEAD_DIM, tm), lambda i: (i // tps, 0, 0, i % tps)))
        logf = (jax.ShapeDtypeStruct((n_seq, N_HEADS, seq), F32),
                pl.BlockSpec((None, N_HEADS, tm), lambda i: (i // tps, 0, i % tps)))
    else:
        band_f32 = fox_f32 = flat(F32)
        logf = (jax.ShapeDtypeStruct((t_total, N_HEADS), F32), pl.BlockSpec((tm, N_HEADS), row))
    cat = (jax.ShapeDtypeStruct((n_seq, N_PAIRS, seq, 2 * LANES), BF16),
           pl.BlockSpec((None, N_PAIRS, tm, 2 * LANES), lambda i: (i // tps, 0, i % tps, 0)))
    outs = [flat(BF16), flat(BF16), flat(BF16), band_f32, band_f32, cat, cat, flat(BF16),
            fox_f32, fox_f32, logf]
    bd = jnp.asarray(np.kron(np.eye(N_HEADS), np.ones((HEAD_DIM, HEAD_DIM))), BF16)
    ltri = jnp.asarray(np.tril(np.ones((tm, tm))), BF16)
    consts = [w["g_mix"], w["w_qkv"], w["w_f"], w["b_f"], w["gains"], bd, ltri, *_aug_constants()]
    in_specs = [pl.BlockSpec((tm, D_MODEL), row)] + [_const_spec(c.shape) for c in consts]
    return pl.pallas_call(
        functools.partial(_inproj_kernel, tiles_per_seq=tps, seq_minor=seq_minor),
        grid=(n_tiles,),
        in_specs=in_specs,
        out_specs=[o[1] for o in outs],
        out_shape=[o[0] for o in outs],
        scratch_shapes=[pltpu.VMEM((8, LANES), F32)],
        compiler_params=_params(1),
        name="inproj",
    )(x, *consts)


def _softmax_pv(scores, values):
    chunks = [[s[:, c * LANES:(c + 1) * LANES] for c in range(s.shape[1] // LANES)] for s in scores]
    cmax = None
    for cs in chunks:
        for c in cs:
            cmax = c if cmax is None else jnp.maximum(cmax, c)
    m = jnp.max(cmax, axis=-1, keepdims=True)
    pv = None
    lsum = None
    for cs, v in zip(chunks, values):
        ps = [jnp.exp2(c - m) for c in cs]
        for p in ps:
            lsum = p if lsum is None else lsum + p
        o = _dot(jnp.concatenate(ps, axis=-1).astype(BF16), v)
        pv = o if pv is None else pv + o
    return pv, jnp.sum(lsum, axis=-1, keepdims=True)


def _head_mask(hh):
    lane = lax.broadcasted_iota(jnp.int32, (1, LANES), 1)
    return (lane // HEAD_DIM) == hh


def _band_prompt_kernel(q_ref, k0_ref, k1_ref, k2_ref, k3_ref, v0_ref, v1_ref, v2_ref, v3_ref, e_ref, mask_ref,
                        o_ref, bias_ref, *, tq):
    i = pl.program_id(1)

    @pl.when((pl.program_id(0) == 0) & (i == 0))
    def _():
        _build_bias(e_ref, mask_ref, bias_ref)

    k_refs = (k0_ref, k1_ref, k2_ref, k3_ref)
    v_refs = (v0_ref, v1_ref, v2_ref, v3_ref)
    for qb in range(2):
        pad = [jnp.where(2 * i + qb - 2 + j >= 0, 0.0, NEG) for j in range(2)] + [0.0]
        for p in range(N_PAIRS):
            cols = slice(p * LANES, (p + 1) * LANES)
            q = q_ref[qb * tq:(qb + 1) * tq, cols]
            ks = [r[:, cols] for r in k_refs[qb:qb + 3]]
            vs = [r[:, cols] for r in v_refs[qb:qb + 3]]
            out = jnp.zeros((tq, LANES), F32)
            for hh in range(2):
                msk = _head_mask(hh)
                qh = jnp.where(msk, q, jnp.zeros_like(q))
                scores = [_dot_nt(qh, ks[j]) + bias_ref[2 * p + hh, :, j * tq:(j + 1) * tq] + pad[j]
                          for j in range(3)]
                pv, l = _softmax_pv(scores, vs)
                out = jnp.where(msk, pv / l, out)
            o_ref[qb * tq:(qb + 1) * tq, cols] = out.astype(BF16)


def _band_prompt(qa, ka, va, bias_e, bias_mask, n_seq):
    t_total = qa.shape[0]
    tq = 256
    seq = t_total // n_seq
    nq = seq // tq
    qmap = lambda b, i: (b * (nq // 2) + i, 0)

    def kmap(j):
        return lambda b, i: (b * nq + jnp.maximum(2 * i - 2 + j, 0), 0)

    blk = lambda m: pl.BlockSpec((tq, WIDTH), m)
    qblk = pl.BlockSpec((2 * tq, WIDTH), qmap)
    return pl.pallas_call(
        functools.partial(_band_prompt_kernel, tq=tq),
        grid=(n_seq, nq // 2),
        in_specs=[qblk] + [blk(kmap(j)) for j in range(4)] + [blk(kmap(j)) for j in range(4)]
        + [_const_spec(bias_e.shape), _const_spec(bias_mask.shape)],
        out_specs=qblk,
        out_shape=jax.ShapeDtypeStruct((t_total, WIDTH), BF16),
        scratch_shapes=[pltpu.VMEM((N_HEADS,) + bias_mask.shape, F32)],
        compiler_params=_params(2),
        name="band_prompt",
    )(qa, ka, ka, ka, ka, va, va, va, va, bias_e, bias_mask)


def _band_sample_kernel(q_ref, kc_ref, vc_ref, kn_ref, vn_ref, e_ref, mask_ref, o_ref, ko_ref, vo_ref,
                        bias_ref):
    p_rows = kc_ref.shape[2]
    t = q_ref.shape[0]

    @pl.when(pl.program_id(0) == 0)
    def _():
        _build_bias(e_ref, mask_ref, bias_ref)

    pad = jnp.zeros((LANES - t, HEAD_DIM), BF16)
    for h in range(N_HEADS):
        cols = slice(h * HEAD_DIM, (h + 1) * HEAD_DIM)
        q = q_ref[:, cols]
        k_new, v_new = kn_ref[:, cols], vn_ref[:, cols]
        k_cache, v_cache = kc_ref[h], vc_ref[h]
        s_cache = _dot(q, k_cache.astype(BF16)) + bias_ref[h, :, 0:p_rows]
        s_new = (_dot_nt(q, jnp.concatenate([k_new.astype(BF16), pad], axis=0))
                 + bias_ref[h, :, p_rows:p_rows + LANES])
        chunks = [s_cache[:, c * LANES:(c + 1) * LANES] for c in range(p_rows // LANES)] + [s_new]
        cmax = chunks[0]
        for c in chunks[1:]:
            cmax = jnp.maximum(cmax, c)
        m = jnp.max(cmax, axis=-1, keepdims=True)
        ps = [jnp.exp2(c - m) for c in chunks]
        lsum = ps[0]
        for p in ps[1:]:
            lsum = lsum + p
        pv = (_dot_nt(jnp.concatenate(ps[:-1], axis=-1).astype(BF16), v_cache.astype(BF16))
              + _dot(ps[-1].astype(BF16), jnp.concatenate([v_new.astype(BF16), pad], axis=0)))
        o_ref[:, cols] = (pv / jnp.sum(lsum, axis=-1, keepdims=True)).astype(BF16)
        ko_ref[h] = jnp.concatenate([k_cache[:, t:], k_new.T], axis=1)
        vo_ref[h] = jnp.concatenate([v_cache[:, t:], v_new.T], axis=1)


def _band_sample(qa, ka_f, va_f, cache_kt, cache_vt, bias_e, bias_mask):
    n = cache_kt.shape[0]
    t = qa.shape[0] // n
    row = pl.BlockSpec((t, WIDTH), lambda b: (b, 0))
    cache = pl.BlockSpec((None,) + cache_kt.shape[1:], lambda b: (b, 0, 0, 0))
    return pl.pallas_call(
        _band_sample_kernel,
        grid=(n,),
        in_specs=[row, cache, cache, row, row, _const_spec(bias_e.shape), _const_spec(bias_mask.shape)],
        out_specs=[row, cache, cache],
        out_shape=[jax.ShapeDtypeStruct(qa.shape, BF16),
                   jax.ShapeDtypeStruct(cache_kt.shape, F32),
                   jax.ShapeDtypeStruct(cache_vt.shape, F32)],
        scratch_shapes=[pltpu.VMEM((N_HEADS,) + bias_mask.shape, F32)],
        compiler_params=_params(1),
        name="band_sample",
    )(qa, cache_kt, cache_vt, ka_f, va_f, bias_e, bias_mask)


def _band_bias(rel_bias, n_q, n_k, offset, band_mask, valid_k):
    assert n_q + n_k - 1 <= BIAS_PERIOD
    m = np.arange(BIAS_PERIOD)
    w = np.where(m < n_k, m, m - BIAS_PERIOD)
    idx = np.clip(offset - w, -MAX_REL, MAX_REL) + MAX_REL
    e = jnp.take(rel_bias.astype(F32), jnp.asarray(idx), axis=1) * LOG2E
    qi = np.arange(n_q)[:, None]
    kj = np.arange(n_k)[None, :]
    valid = np.broadcast_to(kj < valid_k, (n_q, n_k))
    if band_mask:
        rel_chunk = (qi + offset) // CHUNK - kj // CHUNK
        valid = valid & (rel_chunk >= 0) & (rel_chunk <= LEFT_CHUNKS)
    return e, jnp.asarray(np.where(valid, 0.0, NEG), F32)


def _build_bias(e_ref, mask_ref, bias_sc):
    rows, cols = mask_ref.shape
    for h in range(N_HEADS):
        spread = jnp.broadcast_to(e_ref[h:h + 1, :], (rows, BIAS_PERIOD))
        bias_sc[h] = pltpu.roll(spread, 0, 1, stride=1, stride_axis=0)[:, :cols] + mask_ref[...]


def _aug_head_mask(h_in_pair, pair):
    lane = lax.broadcasted_iota(jnp.int32, (1, 2 * LANES), 1)
    head = 2 * pair + h_in_pair
    in_q = (lane < LANES) & ((lane // HEAD_DIM) == h_in_pair)
    in_aug = (lane >= LANES) & (((lane - LANES) // AUG_STRIDE) == head)
    return in_q | in_aug


def _flash_step(state, s, v, row_bias=None, v_seq_minor=False):
    m_prev, l_prev, acc_prev = state
    chunks = [s[:, c * LANES:(c + 1) * LANES] for c in range(s.shape[1] // LANES)]
    if row_bias is not None:
        chunks = [c + row_bias for c in chunks]
    cmax = chunks[0]
    for c in chunks[1:]:
        cmax = jnp.maximum(cmax, c)
    m_new = jnp.maximum(m_prev, jnp.max(cmax, axis=-1, keepdims=True))
    alpha = jnp.exp2(m_prev - m_new)
    ps = [jnp.exp2(c - m_new) for c in chunks]
    lsum = ps[0]
    for p in ps[1:]:
        lsum = lsum + p
    p = jnp.concatenate(ps, axis=-1).astype(BF16)
    pv = _dot_nt(p, v) if v_seq_minor else _dot(p, v)
    return m_new, alpha * l_prev + lsum, alpha[:, :pv.shape[1]] * acc_prev + pv


def _flash_update(s, v, m_ref, l_ref, acc_ref, h, row_bias=None, v_seq_minor=False, rows=slice(None)):
    state = (m_ref[h, rows], l_ref[h, rows], acc_ref[h, rows])
    m_ref[h, rows], l_ref[h, rows], acc_ref[h, rows] = _flash_step(state, s, v, row_bias, v_seq_minor)


def _flash_result(l_ref, acc_ref, pair):
    outs = [acc_ref[2 * pair + hh] / jnp.sum(l_ref[2 * pair + hh], axis=-1, keepdims=True)
            for hh in range(2)]
    return jnp.where(_head_mask(0), outs[0], outs[1])


def _fox_prompt_kernel(qi_ref, kj_ref, q_ref, k_ref, v_ref, o_ref, qh_sc, m_sc, l_sc, acc_sc, *, tq, tk):
    t = pl.program_id(1)
    i = qi_ref[t]
    j = kj_ref[t]

    @pl.when(j == 0)
    def _():
        for p in range(N_PAIRS):
            q = q_ref[p]
            for hh in range(2):
                qh_sc[2 * p + hh] = jnp.where(_aug_head_mask(hh, p), q, jnp.zeros_like(q))
        m_sc[...] = jnp.full_like(m_sc, NEG)
        l_sc[...] = jnp.zeros_like(l_sc)
        acc_sc[...] = jnp.zeros_like(acc_sc)

    def step(diagonal):
        half = tq // 2
        if diagonal:
            row = lax.broadcasted_iota(jnp.int32, (half, tk), 0)
            col = lax.broadcasted_iota(jnp.int32, (half, tk), 1)
            keep_first = (col <= row)[:, 0:half]
            keep_second = col <= row + half
        for p in range(N_PAIRS):
            cols = slice(p * LANES, (p + 1) * LANES)
            for hh in range(2):
                h = 2 * p + hh
                if diagonal:
                    s = _dot_nt(qh_sc[h, 0:half, :], k_ref[p, 0:half, :])
                    _flash_update(jnp.where(keep_first, s, NEG), v_ref[0:half, cols], m_sc, l_sc, acc_sc, h,
                                  rows=slice(0, half))
                    s = _dot_nt(qh_sc[h, half:tq, :], k_ref[p])
                    _flash_update(jnp.where(keep_second, s, NEG), v_ref[:, cols], m_sc, l_sc, acc_sc, h,
                                  rows=slice(half, tq))
                else:
                    _flash_update(_dot_nt(qh_sc[h], k_ref[p]), v_ref[:, cols], m_sc, l_sc, acc_sc, h)

    @pl.when(j < i)
    def _():
        step(False)

    @pl.when(j == i)
    def _():
        step(True)
        for p in range(N_PAIRS):
            o_ref[:, p * LANES:(p + 1) * LANES] = _flash_result(l_sc, acc_sc, p).astype(BF16)


def _fox_prompt(qcat, kcat, vb, n_seq):
    seq = qcat.shape[2]
    tq = tk = 1024
    nt = seq // tq
    qi = np.concatenate([np.full(i + 1, i) for i in range(nt)]).astype(np.int32)
    kj = np.concatenate([np.arange(i + 1) for i in range(nt)]).astype(np.int32)
    grid_spec = pltpu.PrefetchScalarGridSpec(
        num_scalar_prefetch=2,
        grid=(n_seq, len(qi)),
        in_specs=[
            pl.BlockSpec((None, N_PAIRS, tq, 2 * LANES), lambda b, t, qi, kj: (b, 0, qi[t], 0)),
            pl.BlockSpec((None, N_PAIRS, tk, 2 * LANES), lambda b, t, qi, kj: (b, 0, kj[t], 0)),
            pl.BlockSpec((tk, WIDTH), lambda b, t, qi, kj: (b * nt + kj[t], 0)),
        ],
        out_specs=pl.BlockSpec((tq, WIDTH), lambda b, t, qi, kj: (b * nt + qi[t], 0)),
        scratch_shapes=[pltpu.VMEM((N_HEADS, tq, 2 * LANES), BF16), pltpu.VMEM((N_HEADS, tq, LANES), F32),
                        pltpu.VMEM((N_HEADS, tq, LANES), F32), pltpu.VMEM((N_HEADS, tq, LANES), F32)],
    )
    return pl.pallas_call(
        functools.partial(_fox_prompt_kernel, tq=tq, tk=tk),
        grid_spec=grid_spec,
        out_shape=jax.ShapeDtypeStruct(vb.shape, BF16),
        compiler_params=_params(2),
        name="fox_prompt",
    )(jnp.asarray(qi), jnp.asarray(kj), qcat, kcat, vb)


def _suffix_sum_exclusive(x):
    n = x.shape[1]
    lane = lax.broadcasted_iota(jnp.int32, x.shape, 1)
    y = jnp.where(lane + 1 < n, pltpu.roll(x, n - 1, axis=1), 0.0)
    shift = 1
    while shift < n:
        y = y + jnp.where(lane + shift < n, pltpu.roll(y, n - shift, axis=1), 0.0)
        shift *= 2
    return y


def _fox_sample_kernel(q_ref, kc_ref, vc_ref, kn_ref, vn_ref, lfc_ref, lfn_ref, u_ref,
                       o_ref, cq_sc, car_sc, m_sc, l_sc, acc_sc, *, n_cache_tiles, sub_keys):
    j = pl.program_id(1)
    t = q_ref.shape[1]

    def cum_new():
        hi, mid, lo = _split3(lfn_ref[...])
        u = u_ref[...]
        return _dot(hi, u) + _dot(mid, u) + _dot(lo, u)

    def q_head(h):
        return q_ref[h // 2, :, (h % 2) * HEAD_DIM:(h % 2 + 1) * HEAD_DIM]

    @pl.when(j == 0)
    def _():
        cn = cum_new() * LOG2E
        eye = (lax.broadcasted_iota(jnp.int32, (t, LANES), 0)
               == lax.broadcasted_iota(jnp.int32, (t, LANES), 1))
        for h in range(N_HEADS):
            col = jnp.sum(jnp.where(eye, jnp.broadcast_to(cn[h:h + 1, :], (t, LANES)), 0.0),
                          axis=-1, keepdims=True)
            cq_sc[h] = jnp.broadcast_to(col, (t, LANES))
        car_sc[...] = jnp.zeros_like(car_sc)
        m_sc[...] = jnp.full_like(m_sc, NEG)
        l_sc[...] = jnp.zeros_like(l_sc)
        acc_sc[...] = jnp.zeros_like(acc_sc)

    lf = lfc_ref[...]
    r = _suffix_sum_exclusive(lf) + car_sc[:, 0:1]
    car_sc[...] = jnp.broadcast_to(r[:, 0:1] + lf[:, 0:1], car_sc.shape)
    r = r * LOG2E
    for sub in range(lf.shape[1] // sub_keys):
        keys = slice(sub * sub_keys, (sub + 1) * sub_keys)
        for h in range(N_HEADS):
            s = _dot(q_head(h), kc_ref[h, :, keys].astype(BF16)) + r[h:h + 1, keys]
            _flash_update(s, vc_ref[h, :, keys].astype(BF16), m_sc, l_sc, acc_sc, h,
                          row_bias=cq_sc[h], v_seq_minor=True)

    @pl.when(j == n_cache_tiles - 1)
    def _():
        cn = cum_new() * LOG2E
        causal = (lax.broadcasted_iota(jnp.int32, (t, LANES), 1)
                  <= lax.broadcasted_iota(jnp.int32, (t, LANES), 0))
        pad = jnp.zeros((LANES - t, HEAD_DIM), BF16)
        for h in range(N_HEADS):
            cols = slice(h * HEAD_DIM, (h + 1) * HEAD_DIM)
            k = jnp.concatenate([kn_ref[h // 2, :, (h % 2) * HEAD_DIM:(h % 2 + 1) * HEAD_DIM], pad], axis=0)
            v = jnp.concatenate([vn_ref[:, cols], pad], axis=0)
            s = _dot_nt(q_head(h), k) + cq_sc[h] - cn[h:h + 1, :]
            _flash_update(jnp.where(causal, s, NEG), v, m_sc, l_sc, acc_sc, h)
            o_ref[:, cols] = (acc_sc[h] / jnp.sum(l_sc[h], axis=-1, keepdims=True)).astype(BF16)


def _fox_sample(qcat, kcat, vb, cache_kt, cache_vt, lf_cache_t, lf_new_t):
    n, _, _, p_rows = cache_kt.shape
    t = vb.shape[0] // n
    tk = 4096
    nct = p_rows // tk
    u = jnp.asarray(np.pad(np.triu(np.ones((t, t))), ((0, 0), (0, LANES - t))), BF16)
    rev = lambda j: nct - 1 - j
    cache = pl.BlockSpec((None, N_HEADS, HEAD_DIM, tk), lambda b, j: (b, 0, 0, rev(j)))
    new_cat = pl.BlockSpec((None, N_PAIRS, t, 2 * LANES), lambda b, j: (0, 0, b, 0))
    return pl.pallas_call(
        functools.partial(_fox_sample_kernel, n_cache_tiles=nct, sub_keys=tk),
        grid=(n, nct),
        in_specs=[
            new_cat, cache, cache, new_cat,
            pl.BlockSpec((t, WIDTH), lambda b, j: (b, 0)),
            pl.BlockSpec((None, N_HEADS, tk), lambda b, j: (b, 0, rev(j))),
            pl.BlockSpec((None, N_HEADS, t), lambda b, j: (b, 0, 0)),
            _const_spec(u.shape),
        ],
        out_specs=pl.BlockSpec((t, WIDTH), lambda b, j: (b, 0)),
        out_shape=jax.ShapeDtypeStruct(vb.shape, BF16),
        scratch_shapes=[pltpu.VMEM((N_HEADS, t, LANES), F32), pltpu.VMEM((N_HEADS, LANES), F32),
                        pltpu.VMEM((N_HEADS, t, LANES), F32), pltpu.VMEM((N_HEADS, t, LANES), F32),
                        pltpu.VMEM((N_HEADS, t, HEAD_DIM), F32)],
        compiler_params=_params(2),
        name="fox_sample",
    )(qcat, cache_kt, cache_vt, kcat, vb, lf_cache_t, lf_new_t, u)


GROUP_LANE = 64
ROW_ALIGN = 16
MOE_TILE = 512
SORT_ROWS = 640
XS_WIDTH = D_MODEL + 2 * LANES


def _route(r):
    lane_i = lax.broadcasted_iota(jnp.int32, r.shape, 1)
    lane = lane_i.astype(F32)
    lane_grp = (lane_i // EXPERTS_PER_GROUP).astype(F32)
    big = float(LANES)
    is_coarse = (lane_i >= N_EXPERTS) & (lane_i < N_EXPERTS + N_GROUPS)
    cm = jnp.where(is_coarse, r, NEG)
    cmax = cm.max(axis=-1, keepdims=True)
    grp = jnp.min(jnp.where(cm == cmax, lane - N_EXPERTS, big), axis=-1, keepdims=True)
    pg_sel = 1.0 / jnp.sum(jnp.exp(cm - cmax), axis=-1, keepdims=True)
    in_grp = (lane_i < N_EXPERTS) & (lane_grp == grp)
    fm = jnp.where(in_grp, r, NEG)
    m1 = fm.max(axis=-1, keepdims=True)
    denom = jnp.sum(jnp.exp(fm - m1), axis=-1, keepdims=True)
    i1 = jnp.min(jnp.where(fm == m1, lane, big), axis=-1, keepdims=True)
    fm2 = jnp.where(lane == i1, NEG, fm)
    m2 = fm2.max(axis=-1, keepdims=True)
    i2 = jnp.min(jnp.where(fm2 == m2, lane, big), axis=-1, keepdims=True)
    p1 = 1.0 / denom
    p2 = jnp.exp(m2 - m1) / denom
    tot = p1 + p2
    comb = (jnp.where(lane == i1, pg_sel * (p1 / tot), 0.0)
            + jnp.where(lane == i2, pg_sel * (p2 / tot), 0.0))
    return comb, grp


def _merge_kernel(x1_ref, oa1_ref, ob1_ref, x2_ref, oa2_ref, ob2_ref, g_ref, wg_ref, wpa_ref, wpb_ref, wo_ref,
                  gf_ref, wrh_ref, wrm_ref, br_ref, y_ref, hx_ref, route_ref, grp_t_ref, cnt_ref,
                  x_sc, oa_sc, ob_sc, *, tiles_first):
    i = pl.program_id(0)

    @pl.when(i < tiles_first)
    def _():
        x_sc[...], oa_sc[...], ob_sc[...] = x1_ref[...], oa1_ref[...], ob1_ref[...]

    @pl.when(i >= tiles_first)
    def _():
        x_sc[...], oa_sc[...], ob_sc[...] = x2_ref[...], oa2_ref[...], ob2_ref[...]

    x = x_sc[...]
    h = _rms(x, g_ref[...]).astype(BF16)
    gate = jax.nn.sigmoid(_dot_nt(h, wg_ref[...]))
    mix = (gate[:, :D_MODEL] * _dot(oa_sc[...], wpa_ref[...])
           + gate[:, D_MODEL:] * _dot(ob_sc[...], wpb_ref[...]))
    y = x + _dot(mix.astype(BF16), wo_ref[...])
    y_ref[...] = y

    hx = _rms(y, gf_ref[...])
    hx_ref[:, 0:D_MODEL] = hx.astype(BF16)
    h_hi, h_mid, _ = _split3(hx)
    r = _dot(h_hi, wrh_ref[...]) + _dot(h_hi, wrm_ref[...]) + _dot(h_mid, wrh_ref[...]) + br_ref[...]
    comb, grp = _route(r)
    lane = lax.broadcasted_iota(jnp.int32, comb.shape, 1)
    route = jnp.where(lane == GROUP_LANE, grp, comb)
    route_ref[...] = route
    r_hi = route.astype(BF16)
    hx_ref[:, D_MODEL:D_MODEL + LANES] = r_hi
    hx_ref[:, D_MODEL + LANES:] = (route - r_hi.astype(F32)).astype(BF16)
    grp_t_ref[...] = route.T[GROUP_LANE:GROUP_LANE + 8, :]
    cnt = jnp.sum(jnp.where(lane.astype(F32) == grp, 1.0, 0.0), axis=0, keepdims=True)
    cnt_ref[...] = jnp.broadcast_to(cnt, cnt_ref.shape)


def _two_stream_maps(tiles_first):
    first = lambda i, *_: (jnp.minimum(i, tiles_first - 1), 0)
    second = lambda i, *_: (jnp.maximum(i - tiles_first, 0), 0)
    return first, second


def _merge(x1, oa1, ob1, x2, oa2, ob2, w):
    tm = MOE_TILE
    tiles_first = x1.shape[0] // tm
    t_total = x1.shape[0] + x2.shape[0]
    n_tiles = t_total // tm
    first, second = _two_stream_maps(tiles_first)
    row = lambda n: pl.BlockSpec((tm, n), lambda i: (i, 0))
    stream = lambda m: [pl.BlockSpec((tm, D_MODEL), m), pl.BlockSpec((tm, WIDTH), m), pl.BlockSpec((tm, WIDTH), m)]
    consts = [w["g_mix"], w["w_gate"], w["w_pa"], w["w_pb"], w["w_o"], w["g_ffn"], w["w_router_hi"],
              w["w_router_mid"], w["b_router"]]
    return pl.pallas_call(
        functools.partial(_merge_kernel, tiles_first=tiles_first),
        grid=(n_tiles,),
        in_specs=stream(first) + stream(second) + [_const_spec(c.shape) for c in consts],
        out_specs=[row(D_MODEL), row(XS_WIDTH), row(LANES), pl.BlockSpec((8, tm), lambda i: (0, i)),
                   pl.BlockSpec((None, 8, LANES), lambda i: (i, 0, 0))],
        out_shape=[jax.ShapeDtypeStruct((t_total, D_MODEL), F32), jax.ShapeDtypeStruct((t_total, XS_WIDTH), BF16),
                   jax.ShapeDtypeStruct((t_total, LANES), F32), jax.ShapeDtypeStruct((8, t_total), F32),
                   jax.ShapeDtypeStruct((n_tiles, 8, LANES), F32)],
        scratch_shapes=[pltpu.VMEM((tm, D_MODEL), F32), pltpu.VMEM((tm, WIDTH), BF16),
                        pltpu.VMEM((tm, WIDTH), BF16)],
        compiler_params=_params(1),
        name="merge",
    )(x1, oa1, ob1, x2, oa2, ob2, *consts)


def _plan_kernel(cnt_ref, ib_ref, dst_ref, len_ref, fill_ref, inblk_ref, outblk_ref, grp_ref, kind_ref, *,
                 n_tiles, n_steps):
    align_bits = ROW_ALIGN.bit_length() - 1
    tile_bits = MOE_TILE.bit_length() - 1
    zero = jnp.int32(0)

    def per_tile(t, tot):
        run = zero
        new_tot = []
        for g in range(N_GROUPS):
            n = ((cnt_ref[t * N_GROUPS + g] + (ROW_ALIGN - 1)) >> align_bits) << align_bits
            ib_ref[t * N_GROUPS + g] = run
            len_ref[t * N_GROUPS + g] = n
            dst_ref[t * N_GROUPS + g] = tot[g]
            run = run + n
            new_tot.append(tot[g] + n)
        return tuple(new_tot)

    tot = lax.fori_loop(0, n_tiles, per_tile, (zero,) * N_GROUPS)
    n_ffn = [(tot[g] + (MOE_TILE - 1)) >> tile_bits for g in range(N_GROUPS)]
    g_base, ends, acc_rows, acc_tiles = [], [], zero, zero
    for g in range(N_GROUPS):
        g_base.append(acc_rows)
        acc_rows = acc_rows + ((n_ffn[g] + 1) << tile_bits)
        acc_tiles = acc_tiles + n_ffn[g] + 1
        ends.append(acc_tiles)

    def add_base(t, carry):
        for g in range(N_GROUPS):
            dst_ref[t * N_GROUPS + g] = dst_ref[t * N_GROUPS + g] + g_base[g]
        return carry

    lax.fori_loop(0, n_tiles, add_base, zero)
    for g in range(N_GROUPS):
        fill_ref[g] = g_base[g] + tot[g]
        fill_ref[N_GROUPS + g] = g_base[g] + (n_ffn[g] << tile_bits)
    fill_ref[2 * N_GROUPS] = ends[-1]
    first_blk = g_base[N_GROUPS - 1]
    for g in reversed(range(N_GROUPS - 1)):
        first_blk = jnp.where(n_ffn[g] > 0, g_base[g], first_blk)
    first_blk = first_blk >> tile_bits

    def pick(vals, g):
        out = vals[N_GROUPS - 1]
        for i in reversed(range(N_GROUPS - 1)):
            out = jnp.where(g == i, vals[i], out)
        return out

    def per_step(k, carry):
        kk = jnp.minimum(k, ends[-1] - 1)
        g = zero
        for i in range(N_GROUPS - 1):
            g = g + jnp.where(kk >= ends[i], 1, 0)
        j = kk - pick([zero] + ends[:-1], g)
        is_expert = j < pick(n_ffn, g)
        out_blk = jnp.where(k < ends[-1], (pick(g_base, g) >> tile_bits) + j, k)
        outblk_ref[k] = out_blk
        inblk_ref[k] = jnp.where(is_expert, out_blk, first_blk)
        grp_ref[k] = g
        kind_ref[k] = jnp.where(is_expert, 1, 0)
        return carry

    lax.fori_loop(0, n_steps, per_step, zero)


def _moe_plan(cnt, t_total):
    n_tiles = t_total // MOE_TILE
    n_steps = (n_tiles + 2 * N_GROUPS + (N_GROUPS * (ROW_ALIGN - 1) * n_tiles + MOE_TILE - 1) // MOE_TILE)
    smem = pl.BlockSpec(memory_space=pltpu.SMEM)
    sizes = [n_tiles * N_GROUPS] * 3 + [2 * N_GROUPS + 1] + [n_steps] * 4
    outs = pl.pallas_call(
        functools.partial(_plan_kernel, n_tiles=n_tiles, n_steps=n_steps),
        grid_spec=pltpu.PrefetchScalarGridSpec(num_scalar_prefetch=1, grid=(1,), in_specs=[],
                                               out_specs=[smem] * len(sizes)),
        out_shape=[jax.ShapeDtypeStruct((n,), jnp.int32) for n in sizes],
        compiler_params=_params(1),
        name="moe_plan",
    )(cnt.reshape(-1))
    return (*outs, n_steps)


def _sort_kernel(ib_ref, dst_ref, len_ref, fill_ref, hx_ref, grp_t_ref, tri_ref, xs_ref, xsort, sems):
    i = pl.program_id(0)
    n = pl.num_programs(0)
    tm = hx_ref.shape[0]
    half = tm // 2
    slot = i % 2

    def copy(src, dst, rows, sl, g, part):
        return pltpu.make_async_copy(xsort.at[sl, pl.ds(pl.multiple_of(src, ROW_ALIGN), rows)],
                                     xs_ref.at[pl.ds(pl.multiple_of(dst, ROW_ALIGN), rows)], sems.at[sl, g, part])

    def for_segments(step, sl, act):
        for g in range(N_GROUPS):
            src, dst = ib_ref[step * N_GROUPS + g], dst_ref[step * N_GROUPS + g]
            act(copy(src, dst, half, sl, g, 0))

            @pl.when(len_ref[step * N_GROUPS + g] > half)
            def _():
                act(copy(src + half, dst + half, half, sl, g, 1))

    def run(cs):
        for c in cs:
            c.start()
        for c in cs:
            c.wait()

    @pl.when(i == 0)
    def _():
        for sl in range(2):
            xsort[sl, SORT_ROWS:, :] = jnp.zeros((xsort.shape[1] - SORT_ROWS, XS_WIDTH), BF16)

    g_row = grp_t_ref[0:1, :]
    sub = lax.broadcasted_iota(jnp.int32, (8, tm), 0).astype(F32)
    mine = sub == g_row
    before = _dot(jnp.where(mine, 1.0, 0.0).astype(BF16), tri_ref[...])
    dest = jnp.sum(jnp.where(mine, before, 0.0), axis=0, keepdims=True)
    for g in range(N_GROUPS):
        base = jnp.full(dest.shape, ib_ref[i * N_GROUPS + g], jnp.int32).astype(F32)
        dest = dest + jnp.where(g_row == g, base, 0.0)
    rows = lax.broadcasted_iota(jnp.int32, (SORT_ROWS, tm), 0).astype(F32)
    perm = jnp.where(rows == dest, 1.0, 0.0).astype(BF16)
    xsort[slot, 0:SORT_ROWS, :] = _dot(perm, hx_ref[...]).astype(BF16)

    @pl.when(i > 0)
    def _():
        for_segments(i - 1, 1 - slot, lambda c: c.wait())

    for_segments(i, slot, lambda c: c.start())

    @pl.when(i == n - 1)
    def _():
        for_segments(i, slot, lambda c: c.wait())
        run([copy(SORT_ROWS, fill_ref[g], tm, slot, g, 0) for g in range(N_GROUPS)])
        run([copy(SORT_ROWS, fill_ref[N_GROUPS + g], tm, slot, g, 0) for g in range(N_GROUPS)])

        @pl.loop(fill_ref[2 * N_GROUPS], xs_ref.shape[0] // tm)
        def _(b):
            run([copy(SORT_ROWS, b * tm, tm, slot, 0, 0)])


def _expert_kernel(ib_ref, ob_ref, grp_ref, kind_ref, xs_ref, w1_ref, w3_ref, w2_ref, ys_ref):
    del ib_ref, ob_ref
    k = pl.program_id(0)

    @pl.when(kind_ref[k] == 1)
    def _():
        x = xs_ref[:, 0:D_MODEL]
        side = lambda w_ref: jnp.concatenate([w_ref[e].astype(BF16) for e in range(EXPERTS_PER_GROUP)], axis=1)
        hid = jax.nn.silu(_dot(x, side(w1_ref))) * _dot(x, side(w3_ref))
        comb = (xs_ref[:, D_MODEL:D_MODEL + LANES].astype(F32) + xs_ref[:, D_MODEL + LANES:].astype(F32))
        lane = lax.broadcasted_iota(jnp.int32, comb.shape, 1)
        first = grp_ref[k] * EXPERTS_PER_GROUP
        scale = []
        for e in range(EXPERTS_PER_GROUP):
            w_e = jnp.sum(jnp.where(lane == first + e, comb, 0.0), axis=-1, keepdims=True)
            scale.append(jnp.broadcast_to(w_e, (comb.shape[0], D_EXPERT)))
        hid = hid * jnp.concatenate(scale, axis=-1)
        w2 = w2_ref[...].astype(BF16).reshape(EXPERTS_PER_GROUP * D_EXPERT, D_MODEL)
        ys_ref[...] = _dot(hid.astype(BF16), w2).astype(BF16)

    @pl.when(kind_ref[k] == 0)
    def _():
        ys_ref[...] = jnp.zeros_like(ys_ref)


def _unsort_kernel(ib_ref, dst_ref, len_ref, y_ref, route_ref, tri_ref, ys_ref, o1_ref, o2_ref, ybuf, yasm, sems,
                   *, tiles_first):
    i = pl.program_id(0)
    n = pl.num_programs(0)
    tm = y_ref.shape[0]
    half = tm // 2
    slot = i % 2

    def fetch(step, sl, act):
        for g in range(N_GROUPS):
            src = pl.multiple_of(dst_ref[step * N_GROUPS + g], ROW_ALIGN)
            act(pltpu.make_async_copy(ys_ref.at[pl.ds(src, half)], ybuf.at[sl, g, 0:half], sems.at[sl, g, 0]))

            @pl.when(len_ref[step * N_GROUPS + g] > half)
            def _():
                act(pltpu.make_async_copy(ys_ref.at[pl.ds(src + half, half)], ybuf.at[sl, g, half:tm],
                                          sems.at[sl, g, 1]))

    @pl.when(i == 0)
    def _():
        yasm[...] = jnp.zeros_like(yasm)
        fetch(0, 0, lambda c: c.start())

    @pl.when(i + 1 < n)
    def _():
        fetch(i + 1, 1 - slot, lambda c: c.start())

    route = route_ref[...]
    lane = lax.broadcasted_iota(jnp.int32, route.shape, 1).astype(F32)
    grp = jnp.sum(jnp.where(lane == GROUP_LANE, route, 0.0), axis=-1, keepdims=True)
    mine = lane == grp
    before = _dot(tri_ref[...], jnp.where(mine, 1.0, 0.0).astype(BF16))
    dest = jnp.sum(jnp.where(mine, before, 0.0), axis=-1, keepdims=True)
    for g in range(N_GROUPS):
        base = jnp.full(dest.shape, ib_ref[i * N_GROUPS + g], jnp.int32).astype(F32)
        dest = dest + jnp.where(grp == g, base, 0.0)
    cols = lax.broadcasted_iota(jnp.int32, (tm, SORT_ROWS), 1).astype(F32)
    perm_t = jnp.where(cols == dest, 1.0, 0.0).astype(BF16)

    fetch(i, slot, lambda c: c.wait())
    for g in range(N_GROUPS):
        row0 = pl.multiple_of(ib_ref[i * N_GROUPS + g], ROW_ALIGN)
        yasm[pl.ds(row0, half), :] = ybuf[slot, g, 0:half]

        @pl.when(len_ref[i * N_GROUPS + g] > half)
        def _():
            yasm[pl.ds(row0 + half, half), :] = ybuf[slot, g, half:tm]
    out = y_ref[...] + _dot(perm_t, yasm[0:SORT_ROWS, :])

    @pl.when(i < tiles_first)
    def _():
        o1_ref[...] = out

    @pl.when(i >= tiles_first)
    def _():
        o2_ref[...] = out


def _moe(y, hx, route, grp_t, cnt, w, rows_first):
    t_total = y.shape[0]
    tm = MOE_TILE
    n_tiles = t_total // tm
    assert SORT_ROWS >= tm + N_GROUPS * (ROW_ALIGN - 1)
    in_base, dst, seg_len, fill, in_blk, out_blk, grp_of_step, is_expert, n_steps = _moe_plan(
        cnt[:, 0, :N_GROUPS].astype(jnp.int32), t_total)
    cap_rows = n_steps * tm
    lower = jnp.asarray(np.tril(np.ones((tm, tm)), -1), BF16)
    upper = jnp.asarray(np.triu(np.ones((tm, tm)), 1), BF16)
    stage_rows = SORT_ROWS + tm
    any_spec = pl.BlockSpec(memory_space=pl.ANY)

    xs = pl.pallas_call(
        _sort_kernel,
        grid_spec=pltpu.PrefetchScalarGridSpec(
            num_scalar_prefetch=4,
            grid=(n_tiles,),
            in_specs=[pl.BlockSpec((tm, XS_WIDTH), lambda i, *_: (i, 0)),
                      pl.BlockSpec((8, tm), lambda i, *_: (0, i)),
                      pl.BlockSpec((tm, tm), lambda i, *_: (0, 0))],
            out_specs=any_spec,
            scratch_shapes=[pltpu.VMEM((2, stage_rows, XS_WIDTH), BF16),
                            pltpu.SemaphoreType.DMA((2, N_GROUPS, 2))],
        ),
        out_shape=jax.ShapeDtypeStruct((cap_rows, XS_WIDTH), BF16),
        compiler_params=_params(1),
        name="moe_sort",
    )(in_base, dst, seg_len, fill, hx, grp_t, upper)

    step_map = lambda which: (lambda k, ib, ob, grp, kind: ((ib, ob, grp)[which][k], 0))
    wmap = lambda k, ib, ob, grp, kind: (grp[k], 0, 0)
    wspec = pl.BlockSpec((EXPERTS_PER_GROUP, D_MODEL, D_EXPERT), wmap)
    ys = pl.pallas_call(
        _expert_kernel,
        grid_spec=pltpu.PrefetchScalarGridSpec(
            num_scalar_prefetch=4,
            grid=(n_steps,),
            in_specs=[pl.BlockSpec((tm, XS_WIDTH), step_map(0)), wspec, wspec, pl.BlockSpec((EXPERTS_PER_GROUP, D_EXPERT, D_MODEL), wmap)],
            out_specs=pl.BlockSpec((tm, D_MODEL), step_map(1)),
        ),
        out_shape=jax.ShapeDtypeStruct((cap_rows, D_MODEL), BF16),
        compiler_params=_params(1),
        name="moe_experts",
    )(in_blk, out_blk, grp_of_step, is_expert, xs, w["w1"], w["w3"], w["w2"])

    first, second = _two_stream_maps(rows_first // tm)
    return pl.pallas_call(
        functools.partial(_unsort_kernel, tiles_first=rows_first // tm),
        grid_spec=pltpu.PrefetchScalarGridSpec(
            num_scalar_prefetch=3,
            grid=(n_tiles,),
            in_specs=[pl.BlockSpec((tm, D_MODEL), lambda i, *_: (i, 0)),
                      pl.BlockSpec((tm, LANES), lambda i, *_: (i, 0)),
                      pl.BlockSpec((tm, tm), lambda i, *_: (0, 0)),
                      any_spec],
            out_specs=[pl.BlockSpec((tm, D_MODEL), first), pl.BlockSpec((tm, D_MODEL), second)],
            scratch_shapes=[pltpu.VMEM((2, N_GROUPS, tm, D_MODEL), BF16),
                            pltpu.VMEM((stage_rows, D_MODEL), BF16),
                            pltpu.SemaphoreType.DMA((2, N_GROUPS, 2))],
        ),
        out_shape=[jax.ShapeDtypeStruct((rows_first, D_MODEL), F32),
                   jax.ShapeDtypeStruct((t_total - rows_first, D_MODEL), F32)],
        compiler_params=_params(1),
        name="moe_unsort",
    )(in_base, dst, seg_len, y, route, lower, ys)


def _prep_weights(g_mix, w_in, b_f, q_norm_a, k_norm_a, q_norm_b, k_norm_b, w_pa, w_pb, w_o,
                  g_ffn, w_rg, b_rg, w_re, b_re, w1, w3, w2):
    n_qkv = 6 * WIDTH
    tile = lambda g: jnp.tile(g, N_HEADS)
    w_router = jnp.concatenate(
        [jnp.transpose(w_re, (1, 0, 2)).reshape(D_MODEL, N_EXPERTS), w_rg,
         jnp.zeros((D_MODEL, LANES - N_EXPERTS - N_GROUPS), F32)], axis=1)
    b_router = jnp.concatenate(
        [b_re.reshape(N_EXPERTS), b_rg, jnp.zeros((LANES - N_EXPERTS - N_GROUPS,), F32)])[None, :]
    return {
        "g_mix": g_mix[None, :],
        "w_qkv": w_in.T[:n_qkv].astype(BF16),
        "w_f": jnp.pad(w_in.T[n_qkv:n_qkv + N_HEADS], ((0, LANES - N_HEADS), (0, 0))).astype(BF16),
        "b_f": jnp.pad(b_f, (0, LANES - N_HEADS))[None, :],
        "w_gate": w_in.T[n_qkv + N_HEADS:].astype(BF16),
        "gains": jnp.stack([tile(q_norm_a), tile(k_norm_a), tile(q_norm_b), tile(k_norm_b)]),
        "w_pa": w_pa.astype(BF16), "w_pb": w_pb.astype(BF16), "w_o": w_o.astype(BF16),
        "g_ffn": g_ffn[None, :],
        "w_router_hi": w_router.astype(BF16),
        "w_router_mid": (w_router - w_router.astype(BF16).astype(F32)).astype(BF16),
        "b_router": b_router,
        "w1": w1, "w3": w3, "w2": w2,
    }


def kernel(x_prompt, x_sample, cache_a_k, cache_a_v, cache_b_k, cache_b_v, cache_b_logf, g_mix, w_in, b_f, q_norm_a, k_norm_a, q_norm_b, k_norm_b, rel_bias, w_pa, w_pb, w_o, g_ffn, w_rg, b_rg, w_re, b_re, w1, w3, w2):
    assert g_mix.shape[0] == 1, "single-layer step"
    n_p, seq, _ = x_prompt.shape
    n_s, t_s, _ = x_sample.shape
    a_rows = cache_a_k.shape[2]
    w = _prep_weights(g_mix[0], w_in[0], b_f[0], q_norm_a[0], k_norm_a[0], q_norm_b[0], k_norm_b[0],
                      w_pa[0], w_pb[0], w_o[0], g_ffn[0], w_rg[0], b_rg[0], w_re[0], b_re[0],
                      w1[0], w3[0], w2[0])
    band_tq = 256
    bias_prompt = _band_bias(rel_bias[0], band_tq, 3 * band_tq, 2 * band_tq, True, 3 * band_tq)
    bias_sample = _band_bias(rel_bias[0], t_s, a_rows + LANES, a_rows, False, a_rows + t_s)

    seq_minor = lambda a: jnp.transpose(a, (0, 2, 3, 1))
    seq_major = lambda a: jnp.transpose(a, (0, 3, 1, 2))[None]

    xp = x_prompt.reshape(n_p * seq, D_MODEL)
    (qa, ka, va, ka_t, va_t, qcat, kcat, vb, kb_t, vb_t, logf_t) = _inproj(xp, n_p, w, seq_minor=True)
    o_a = _band_prompt(qa, ka, va, *bias_prompt, n_p)
    o_b = _fox_prompt(qcat, kcat, vb, n_p)

    xs = x_sample.reshape(n_s * t_s, D_MODEL)
    (qa_s, _, _, ka_fs, va_fs, qcat_s, kcat_s, vb_s, kb_fs, vb_fs, logf_s) = _inproj(
        xs, 1, w, seq_minor=False)
    o_as, new_ak_t, new_av_t = _band_sample(
        qa_s, ka_fs, va_fs, seq_minor(cache_a_k[0]), seq_minor(cache_a_v[0]), *bias_sample)
    lf_cache_t = jnp.transpose(cache_b_logf[0], (0, 2, 1))
    lf_new_t = jnp.transpose(logf_s.reshape(n_s, t_s, N_HEADS), (0, 2, 1))
    o_bs = _fox_sample(qcat_s, kcat_s, vb_s, seq_minor(cache_b_k[0]), seq_minor(cache_b_v[0]),
                       lf_cache_t, lf_new_t)

    y_p, y_s = _moe(*_merge(xp, o_a, o_b, xs, o_as, o_bs, w), w, rows_first=n_p * seq)

    heads = lambda a, n, r: a.reshape(1, n, r, N_HEADS, HEAD_DIM)
    return (y_p.reshape(n_p, seq, D_MODEL), y_s.reshape(n_s, t_s, D_MODEL),
            seq_major(ka_t), seq_major(va_t), seq_major(kb_t), seq_major(vb_t),
            jnp.transpose(logf_t, (0, 2, 1))[None],
            seq_major(new_ak_t), seq_major(new_av_t),
            heads(kb_fs, n_s, t_s), heads(vb_fs, n_s, t_s), logf_s.reshape(1, n_s, t_s, N_HEADS))
```

```python
import functools

import numpy as np
import jax
import jax.numpy as jnp
from jax import lax
from jax.experimental import pallas as pl
from jax.experimental.pallas import tpu as pltpu

F32 = jnp.float32
BF16 = jnp.bfloat16

D_MODEL = 1024
HEAD_DIM = 64
N_HEADS = 8
WIDTH = N_HEADS * HEAD_DIM
N_PAIRS = N_HEADS // 2
CHUNK = 64
LEFT_CHUNKS = 8
WINDOW_ROWS = LEFT_CHUNKS * CHUNK
MAX_REL = 256
N_GROUPS = 4
EXPERTS_PER_GROUP = 8
N_EXPERTS = N_GROUPS * EXPERTS_PER_GROUP
D_EXPERT = 128
EPS = 1e-6
NEG = -1e30
LOG2E = 1.4426950408889634
LANES = 128
BIAS_PERIOD = 1024
AUG_STRIDE = 8
VMEM_LIMIT = 56 * 1024 * 1024

_NT = (((1,), (1,)), ((), ()))


def _dot(a, b):
    return jnp.dot(a, b, preferred_element_type=F32)


def _dot_nt(a, b):
    return lax.dot_general(a, b, _NT, preferred_element_type=F32)


def _split3(x):
    hi = x.astype(BF16)
    r = x - hi.astype(F32)
    mid = r.astype(BF16)
    lo = (r - mid.astype(F32)).astype(BF16)
    return hi, mid, lo


def _rms(x, g):
    ms = jnp.mean(x * x, axis=-1, keepdims=True)
    return x * lax.rsqrt(ms + EPS) * g


def _params(n_axes):
    return pltpu.CompilerParams(dimension_semantics=("arbitrary",) * n_axes,
                                vmem_limit_bytes=VMEM_LIMIT)


def _const_spec(shape):
    nd = len(shape)
    return pl.BlockSpec(shape, lambda *_: (0,) * nd)


def _prefix_sum_rows(x):
    n = x.shape[0]
    row = lax.broadcasted_iota(jnp.int32, x.shape, 0)
    shift = 1
    while shift < n:
        x = x + jnp.where(row >= shift, pltpu.roll(x, shift, axis=0), 0.0)
        shift *= 2
    return x


def _inproj_kernel(x_ref, g_ref, wqkv_ref, wf_ref, bf_ref, gains_ref, bd_ref,
                   selq_ref, selk_ref, oneq_ref, onek_ref,
                   qa_ref, ka_ref, va_ref, kaf_ref, vaf_ref, qcat_ref, kcat_ref, vb_ref,
                   kbf_ref, vbf_ref, logf_ref, carry_ref, *, tiles_per_seq, seq_minor):
    i = pl.program_id(0)
    tm = x_ref.shape[0]

    @pl.when(i % tiles_per_seq == 0)
    def _():
        carry_ref[...] = jnp.zeros_like(carry_ref)

    def put_f32(ref, y, last_tile_only=False):
        if not seq_minor:
            ref[...] = y
        elif last_tile_only:
            @pl.when(i % tiles_per_seq == tiles_per_seq - 1)
            def _():
                ref[...] = y.T.reshape(N_HEADS, HEAD_DIM, tm)
        else:
            ref[...] = y.T.reshape(N_HEADS, HEAD_DIM, tm)

    h = _rms(x_ref[...], g_ref[...]).astype(BF16)

    def seg(s):
        return _dot_nt(h, wqkv_ref[s * WIDTH:(s + 1) * WIDTH, :])

    def headnorm(y, n):
        ss = _dot((y * y).astype(BF16), bd_ref[...])
        return y * lax.rsqrt(ss * (1.0 / HEAD_DIM) + EPS) * gains_ref[n:n + 1, :]

    q_a = headnorm(seg(0), 0)
    qa_ref[...] = (q_a * (HEAD_DIM ** -0.5 * LOG2E)).astype(BF16)
    k_a = headnorm(seg(1), 1)
    ka_ref[...] = k_a.astype(BF16)
    put_f32(kaf_ref, k_a, last_tile_only=True)
    v_a = seg(2)
    va_ref[...] = v_a.astype(BF16)
    put_f32(vaf_ref, v_a, last_tile_only=True)

    z = _dot_nt(h, wf_ref[...]) + bf_ref[...]
    logf = jnp.minimum(z, 0.0) - jnp.log(1.0 + jnp.exp(-jnp.abs(z)))
    if seq_minor:
        logf_ref[...] = logf.T[0:N_HEADS, :]
    else:
        logf_ref[...] = logf[:, :N_HEADS]
    c = _prefix_sum_rows(logf) + carry_ref[0:1, :]
    carry_ref[...] = jnp.broadcast_to(c[-1:, :], carry_ref.shape)
    cs = jnp.concatenate(_split3(c * LOG2E), axis=-1)
    q_aug = (_dot(cs, selq_ref[...]) + oneq_ref[...]).astype(BF16)
    k_aug = (_dot(cs, selk_ref[...]) + onek_ref[...]).astype(BF16)

    q_b = (headnorm(seg(3), 2) * (HEAD_DIM ** -0.5 * LOG2E)).astype(BF16)
    k_b = headnorm(seg(4), 3)
    put_f32(kbf_ref, k_b)
    k_b = k_b.astype(BF16)
    v_b = seg(5)
    put_f32(vbf_ref, v_b)
    vb_ref[...] = v_b.astype(BF16)
    for p in range(N_PAIRS):
        cols = slice(p * LANES, (p + 1) * LANES)
        qcat_ref[p, :, 0:LANES] = q_b[:, cols]
        qcat_ref[p, :, LANES:2 * LANES] = q_aug
        kcat_ref[p, :, 0:LANES] = k_b[:, cols]
        kcat_ref[p, :, LANES:2 * LANES] = k_aug


def _aug_constants():
    selq = np.zeros((3 * LANES, LANES), np.float32)
    selk = np.zeros((3 * LANES, LANES), np.float32)
    oneq = np.zeros((1, LANES), np.float32)
    onek = np.zeros((1, LANES), np.float32)
    for h in range(N_HEADS):
        for k in range(3):
            selq[k * LANES + h, AUG_STRIDE * h + k] = 1.0
            selk[k * LANES + h, AUG_STRIDE * h + 3 + k] = -1.0
            oneq[0, AUG_STRIDE * h + 3 + k] = 1.0
            onek[0, AUG_STRIDE * h + k] = 1.0
    return (jnp.asarray(selq, BF16), jnp.asarray(selk, BF16), jnp.asarray(oneq), jnp.asarray(onek))


def _inproj(x, n_seq, w, seq_minor):
    t_total = x.shape[0]
    tm = 512
    n_tiles = t_total // tm
    seq = t_total // n_seq
    tps = seq // tm
    row = lambda i: (i, 0)
    flat = lambda dt: (jax.ShapeDtypeStruct((t_total, WIDTH), dt), pl.BlockSpec((tm, WIDTH), row))
    if seq_minor:
        assert tm == WINDOW_ROWS
        band_f32 = (jax.ShapeDtypeStruct((n_seq, N_HEADS, HEAD_DIM, WINDOW_ROWS), F32),
                    pl.BlockSpec((None, N_HEADS, HEAD_DIM, tm), lambda i: (i // tps, 0, 0, 0)))
        fox_f32 = (jax.ShapeDtypeStruct((n_seq, N_HEADS, HEAD_DIM, seq), F32),
                   pl.BlockSpec((None, N_HEADS, HEAD_DIM, tm), lambda i: (i // tps, 0, 0, i % tps)))
        logf = (jax.ShapeDtypeStruct((n_seq, N_HEADS, seq), F32),
                pl.BlockSpec((None, N_HEADS, tm), lambda i: (i // tps, 0, i % tps)))
    else:
        band_f32 = fox_f32 = flat(F32)
        logf = (jax.ShapeDtypeStruct((t_total, N_HEADS), F32), pl.BlockSpec((tm, N_HEADS), row))
    cat = (jax.ShapeDtypeStruct((n_seq, N_PAIRS, seq, 2 * LANES), BF16),
           pl.BlockSpec((None, N_PAIRS, tm, 2 * LANES), lambda i: (i // tps, 0, i % tps, 0)))
    outs = [flat(BF16), flat(BF16), flat(BF16), band_f32, band_f32, cat, cat, flat(BF16),
            fox_f32, fox_f32, logf]
    bd = jnp.asarray(np.kron(np.eye(N_HEADS), np.ones((HEAD_DIM, HEAD_DIM))), BF16)
    consts = [w["g_mix"], w["w_qkv"], w["w_f"], w["b_f"], w["gains"], bd, *_aug_constants()]
    in_specs = [pl.BlockSpec((tm, D_MODEL), row)] + [_const_spec(c.shape) for c in consts]
    return pl.pallas_call(
        functools.partial(_inproj_kernel, tiles_per_seq=tps, seq_minor=seq_minor),
        grid=(n_tiles,),
        in_specs=in_specs,
        out_specs=[o[1] for o in outs],
        out_shape=[o[0] for o in outs],
        scratch_shapes=[pltpu.VMEM((8, LANES), F32)],
        compiler_params=_params(1),
        name="inproj",
    )(x, *consts)


def _softmax_pv(scores, values):
    chunks = [[s[:, c * LANES:(c + 1) * LANES] for c in range(s.shape[1] // LANES)] for s in scores]
    cmax = None
    for cs in chunks:
        for c in cs:
            cmax = c if cmax is None else jnp.maximum(cmax, c)
    m = jnp.max(cmax, axis=-1, keepdims=True)
    pv = None
    lsum = None
    for cs, v in zip(chunks, values):
        ps = [jnp.exp2(c - m) for c in cs]
        for p in ps:
            lsum = p if lsum is None else lsum + p
        o = _dot(jnp.concatenate(ps, axis=-1).astype(BF16), v)
        pv = o if pv is None else pv + o
    return pv, jnp.sum(lsum, axis=-1, keepdims=True)


def _head_mask(hh):
    lane = lax.broadcasted_iota(jnp.int32, (1, LANES), 1)
    return (lane // HEAD_DIM) == hh


def _band_prompt_kernel(q_ref, k0_ref, k1_ref, k2_ref, k3_ref, v0_ref, v1_ref, v2_ref, v3_ref, e_ref, mask_ref,
                        o_ref, bias_ref, *, tq):
    i = pl.program_id(1)

    @pl.when((pl.program_id(0) == 0) & (i == 0))
    def _():
        _build_bias(e_ref, mask_ref, bias_ref)

    k_refs = (k0_ref, k1_ref, k2_ref, k3_ref)
    v_refs = (v0_ref, v1_ref, v2_ref, v3_ref)
    for qb in range(2):
        pad = [jnp.where(2 * i + qb - 2 + j >= 0, 0.0, NEG) for j in range(2)] + [0.0]
        for p in range(N_PAIRS):
            cols = slice(p * LANES, (p + 1) * LANES)
            q = q_ref[qb * tq:(qb + 1) * tq, cols]
            ks = [r[:, cols] for r in k_refs[qb:qb + 3]]
            vs = [r[:, cols] for r in v_refs[qb:qb + 3]]
            out = jnp.zeros((tq, LANES), F32)
            for hh in range(2):
                msk = _head_mask(hh)
                qh = jnp.where(msk, q, jnp.zeros_like(q))
                scores = [_dot_nt(qh, ks[j]) + bias_ref[2 * p + hh, :, j * tq:(j + 1) * tq] + pad[j]
                          for j in range(3)]
                pv, l = _softmax_pv(scores, vs)
                out = jnp.where(msk, pv / l, out)
            o_ref[qb * tq:(qb + 1) * tq, cols] = out.astype(BF16)


def _band_prompt(qa, ka, va, bias_e, bias_mask, n_seq):
    t_total = qa.shape[0]
    tq = 256
    seq = t_total // n_seq
    nq = seq // tq
    qmap = lambda b, i: (b * (nq // 2) + i, 0)

    def kmap(j):
        return lambda b, i: (b * nq + jnp.maximum(2 * i - 2 + j, 0), 0)

    blk = lambda m: pl.BlockSpec((tq, WIDTH), m)
    qblk = pl.BlockSpec((2 * tq, WIDTH), qmap)
    return pl.pallas_call(
        functools.partial(_band_prompt_kernel, tq=tq),
        grid=(n_seq, nq // 2),
        in_specs=[qblk] + [blk(kmap(j)) for j in range(4)] + [blk(kmap(j)) for j in range(4)]
        + [_const_spec(bias_e.shape), _const_spec(bias_mask.shape)],
        out_specs=qblk,
        out_shape=jax.ShapeDtypeStruct((t_total, WIDTH), BF16),
        scratch_shapes=[pltpu.VMEM((N_HEADS,) + bias_mask.shape, F32)],
        compiler_params=_params(2),
        name="band_prompt",
    )(qa, ka, ka, ka, ka, va, va, va, va, bias_e, bias_mask)


def _band_sample_kernel(q_ref, kc_ref, vc_ref, kn_ref, vn_ref, e_ref, mask_ref, o_ref, ko_ref, vo_ref,
                        bias_ref):
    p_rows = kc_ref.shape[2]
    t = q_ref.shape[0]

    @pl.when(pl.program_id(0) == 0)
    def _():
        _build_bias(e_ref, mask_ref, bias_ref)

    pad = jnp.zeros((LANES - t, HEAD_DIM), BF16)
    for h in range(N_HEADS):
        cols = slice(h * HEAD_DIM, (h + 1) * HEAD_DIM)
        q = q_ref[:, cols]
        k_new, v_new = kn_ref[:, cols], vn_ref[:, cols]
        k_cache, v_cache = kc_ref[h], vc_ref[h]
        s_cache = _dot(q, k_cache.astype(BF16)) + bias_ref[h, :, 0:p_rows]
        s_new = (_dot_nt(q, jnp.concatenate([k_new.astype(BF16), pad], axis=0))
                 + bias_ref[h, :, p_rows:p_rows + LANES])
        chunks = [s_cache[:, c * LANES:(c + 1) * LANES] for c in range(p_rows // LANES)] + [s_new]
        cmax = chunks[0]
        for c in chunks[1:]:
            cmax = jnp.maximum(cmax, c)
        m = jnp.max(cmax, axis=-1, keepdims=True)
        ps = [jnp.exp2(c - m) for c in chunks]
        lsum = ps[0]
        for p in ps[1:]:
            lsum = lsum + p
        pv = (_dot_nt(jnp.concatenate(ps[:-1], axis=-1).astype(BF16), v_cache.astype(BF16))
              + _dot(ps[-1].astype(BF16), jnp.concatenate([v_new.astype(BF16), pad], axis=0)))
        o_ref[:, cols] = (pv / jnp.sum(lsum, axis=-1, keepdims=True)).astype(BF16)
        ko_ref[h] = jnp.concatenate([k_cache[:, t:], k_new.T], axis=1)
        vo_ref[h] = jnp.concatenate([v_cache[:, t:], v_new.T], axis=1)


def _band_sample(qa, ka_f, va_f, cache_kt, cache_vt, bias_e, bias_mask):
    n = cache_kt.shape[0]
    t = qa.shape[0] // n
    row = pl.BlockSpec((t, WIDTH), lambda b: (b, 0))
    cache = pl.BlockSpec((None,) + cache_kt.shape[1:], lambda b: (b, 0, 0, 0))
    return pl.pallas_call(
        _band_sample_kernel,
        grid=(n,),
        in_specs=[row, cache, cache, row, row, _const_spec(bias_e.shape), _const_spec(bias_mask.shape)],
        out_specs=[row, cache, cache],
        out_shape=[jax.ShapeDtypeStruct(qa.shape, BF16),
                   jax.ShapeDtypeStruct(cache_kt.shape, F32),
                   jax.ShapeDtypeStruct(cache_vt.shape, F32)],
        scratch_shapes=[pltpu.VMEM((N_HEADS,) + bias_mask.shape, F32)],
        compiler_params=_params(1),
        name="band_sample",
    )(qa, cache_kt, cache_vt, ka_f, va_f, bias_e, bias_mask)


def _band_bias(rel_bias, n_q, n_k, offset, band_mask, valid_k):
    assert n_q + n_k - 1 <= BIAS_PERIOD
    m = np.arange(BIAS_PERIOD)
    w = np.where(m < n_k, m, m - BIAS_PERIOD)
    idx = np.clip(offset - w, -MAX_REL, MAX_REL) + MAX_REL
    e = jnp.take(rel_bias.astype(F32), jnp.asarray(idx), axis=1) * LOG2E
    qi = np.arange(n_q)[:, None]
    kj = np.arange(n_k)[None, :]
    valid = np.broadcast_to(kj < valid_k, (n_q, n_k))
    if band_mask:
        rel_chunk = (qi + offset) // CHUNK - kj // CHUNK
        valid = valid & (rel_chunk >= 0) & (rel_chunk <= LEFT_CHUNKS)
    return e, jnp.asarray(np.where(valid, 0.0, NEG), F32)


def _build_bias(e_ref, mask_ref, bias_sc):
    rows, cols = mask_ref.shape
    for h in range(N_HEADS):
        spread = jnp.broadcast_to(e_ref[h:h + 1, :], (rows, BIAS_PERIOD))
        bias_sc[h] = pltpu.roll(spread, 0, 1, stride=1, stride_axis=0)[:, :cols] + mask_ref[...]


def _aug_head_mask(h_in_pair, pair):
    lane = lax.broadcasted_iota(jnp.int32, (1, 2 * LANES), 1)
    head = 2 * pair + h_in_pair
    in_q = (lane < LANES) & ((lane // HEAD_DIM) == h_in_pair)
    in_aug = (lane >= LANES) & (((lane - LANES) // AUG_STRIDE) == head)
    return in_q | in_aug


def _flash_step(state, s, v, row_bias=None, v_seq_minor=False):
    m_prev, l_prev, acc_prev = state
    chunks = [s[:, c * LANES:(c + 1) * LANES] for c in range(s.shape[1] // LANES)]
    if row_bias is not None:
        chunks = [c + row_bias for c in chunks]
    cmax = chunks[0]
    for c in chunks[1:]:
        cmax = jnp.maximum(cmax, c)
    m_new = jnp.maximum(m_prev, jnp.max(cmax, axis=-1, keepdims=True))
    alpha = jnp.exp2(m_prev - m_new)
    ps = [jnp.exp2(c - m_new) for c in chunks]
    lsum = ps[0]
    for p in ps[1:]:
        lsum = lsum + p
    p = jnp.concatenate(ps, axis=-1).astype(BF16)
    pv = _dot_nt(p, v) if v_seq_minor else _dot(p, v)
    return m_new, alpha * l_prev + lsum, alpha[:, :pv.shape[1]] * acc_prev + pv


def _flash_update(s, v, m_ref, l_ref, acc_ref, h, row_bias=None, v_seq_minor=False, rows=slice(None)):
    state = (m_ref[h, rows], l_ref[h, rows], acc_ref[h, rows])
    m_ref[h, rows], l_ref[h, rows], acc_ref[h, rows] = _flash_step(state, s, v, row_bias, v_seq_minor)


def _flash_result(l_ref, acc_ref, pair):
    outs = [acc_ref[2 * pair + hh] / jnp.sum(l_ref[2 * pair + hh], axis=-1, keepdims=True)
            for hh in range(2)]
    return jnp.where(_head_mask(0), outs[0], outs[1])


def _fox_prompt_kernel(qi_ref, kj_ref, q_ref, k_ref, v_ref, o_ref, qh_sc, m_sc, l_sc, acc_sc, *, tq, tk):
    t = pl.program_id(1)
    i = qi_ref[t]
    j = kj_ref[t]

    @pl.when(j == 0)
    def _():
        for p in range(N_PAIRS):
            q = q_ref[p]
            for hh in range(2):
                qh_sc[2 * p + hh] = jnp.where(_aug_head_mask(hh, p), q, jnp.zeros_like(q))
        m_sc[...] = jnp.full_like(m_sc, NEG)
        l_sc[...] = jnp.zeros_like(l_sc)
        acc_sc[...] = jnp.zeros_like(acc_sc)

    def step(diagonal):
        half = tq // 2
        if diagonal:
            row = lax.broadcasted_iota(jnp.int32, (half, tk), 0)
            col = lax.broadcasted_iota(jnp.int32, (half, tk), 1)
            keep_first = (col <= row)[:, 0:half]
            keep_second = col <= row + half
        for p in range(N_PAIRS):
            cols = slice(p * LANES, (p + 1) * LANES)
            for hh in range(2):
                h = 2 * p + hh
                if diagonal:
                    s = _dot_nt(qh_sc[h, 0:half, :], k_ref[p, 0:half, :])
                    _flash_update(jnp.where(keep_first, s, NEG), v_ref[0:half, cols], m_sc, l_sc, acc_sc, h,
                                  rows=slice(0, half))
                    s = _dot_nt(qh_sc[h, half:tq, :], k_ref[p])
                    _flash_update(jnp.where(keep_second, s, NEG), v_ref[:, cols], m_sc, l_sc, acc_sc, h,
                                  rows=slice(half, tq))
                else:
                    _flash_update(_dot_nt(qh_sc[h], k_ref[p]), v_ref[:, cols], m_sc, l_sc, acc_sc, h)

    @pl.when(j < i)
    def _():
        step(False)

    @pl.when(j == i)
    def _():
        step(True)
        for p in range(N_PAIRS):
            o_ref[:, p * LANES:(p + 1) * LANES] = _flash_result(l_sc, acc_sc, p).astype(BF16)


def _fox_prompt(qcat, kcat, vb, n_seq):
    seq = qcat.shape[2]
    tq = tk = 1024
    nt = seq // tq
    qi = np.concatenate([np.full(i + 1, i) for i in range(nt)]).astype(np.int32)
    kj = np.concatenate([np.arange(i + 1) for i in range(nt)]).astype(np.int32)
    grid_spec = pltpu.PrefetchScalarGridSpec(
        num_scalar_prefetch=2,
        grid=(n_seq, len(qi)),
        in_specs=[
            pl.BlockSpec((None, N_PAIRS, tq, 2 * LANES), lambda b, t, qi, kj: (b, 0, qi[t], 0)),
            pl.BlockSpec((None, N_PAIRS, tk, 2 * LANES), lambda b, t, qi, kj: (b, 0, kj[t], 0)),
            pl.BlockSpec((tk, WIDTH), lambda b, t, qi, kj: (b * nt + kj[t], 0)),
        ],
        out_specs=pl.BlockSpec((tq, WIDTH), lambda b, t, qi, kj: (b * nt + qi[t], 0)),
        scratch_shapes=[pltpu.VMEM((N_HEADS, tq, 2 * LANES), BF16), pltpu.VMEM((N_HEADS, tq, LANES), F32),
                        pltpu.VMEM((N_HEADS, tq, LANES), F32), pltpu.VMEM((N_HEADS, tq, LANES), F32)],
    )
    return pl.pallas_call(
        functools.partial(_fox_prompt_kernel, tq=tq, tk=tk),
        grid_spec=grid_spec,
        out_shape=jax.ShapeDtypeStruct(vb.shape, BF16),
        compiler_params=_params(2),
        name="fox_prompt",
    )(jnp.asarray(qi), jnp.asarray(kj), qcat, kcat, vb)


def _suffix_sum_exclusive(x):
    n = x.shape[1]
    lane = lax.broadcasted_iota(jnp.int32, x.shape, 1)
    y = jnp.where(lane + 1 < n, pltpu.roll(x, n - 1, axis=1), 0.0)
    shift = 1
    while shift < n:
        y = y + jnp.where(lane + shift < n, pltpu.roll(y, n - shift, axis=1), 0.0)
        shift *= 2
    return y


def _fox_sample_kernel(q_ref, kc_ref, vc_ref, kn_ref, vn_ref, lfc_ref, lfn_ref, u_ref,
                       o_ref, cq_sc, car_sc, m_sc, l_sc, acc_sc, *, n_cache_tiles, sub_keys):
    j = pl.program_id(1)
    t = q_ref.shape[1]

    def cum_new():
        hi, mid, lo = _split3(lfn_ref[...])
        u = u_ref[...]
        return _dot(hi, u) + _dot(mid, u) + _dot(lo, u)

    def q_head(h):
        return q_ref[h // 2, :, (h % 2) * HEAD_DIM:(h % 2 + 1) * HEAD_DIM]

    @pl.when(j == 0)
    def _():
        cn = cum_new() * LOG2E
        eye = (lax.broadcasted_iota(jnp.int32, (t, LANES), 0)
               == lax.broadcasted_iota(jnp.int32, (t, LANES), 1))
        for h in range(N_HEADS):
            col = jnp.sum(jnp.where(eye, jnp.broadcast_to(cn[h:h + 1, :], (t, LANES)), 0.0),
                          axis=-1, keepdims=True)
            cq_sc[h] = jnp.broadcast_to(col, (t, LANES))
        car_sc[...] = jnp.zeros_like(car_sc)
        m_sc[...] = jnp.full_like(m_sc, NEG)
        l_sc[...] = jnp.zeros_like(l_sc)
        acc_sc[...] = jnp.zeros_like(acc_sc)

    lf = lfc_ref[...]
    r = _suffix_sum_exclusive(lf) + car_sc[:, 0:1]
    car_sc[...] = jnp.broadcast_to(r[:, 0:1] + lf[:, 0:1], car_sc.shape)
    r = r * LOG2E
    for sub in range(lf.shape[1] // sub_keys):
        keys = slice(sub * sub_keys, (sub + 1) * sub_keys)
        for h in range(N_HEADS):
            s = _dot(q_head(h), kc_ref[h, :, keys].astype(BF16)) + r[h:h + 1, keys]
            _flash_update(s, vc_ref[h, :, keys].astype(BF16), m_sc, l_sc, acc_sc, h,
                          row_bias=cq_sc[h], v_seq_minor=True)

    @pl.when(j == n_cache_tiles - 1)
    def _():
        cn = cum_new() * LOG2E
        causal = (lax.broadcasted_iota(jnp.int32, (t, LANES), 1)
                  <= lax.broadcasted_iota(jnp.int32, (t, LANES), 0))
        pad = jnp.zeros((LANES - t, HEAD_DIM), BF16)
        for h in range(N_HEADS):
            cols = slice(h * HEAD_DIM, (h + 1) * HEAD_DIM)
            k = jnp.concatenate([kn_ref[h // 2, :, (h % 2) * HEAD_DIM:(h % 2 + 1) * HEAD_DIM], pad], axis=0)
            v = jnp.concatenate([vn_ref[:, cols], pad], axis=0)
            s = _dot_nt(q_head(h), k) + cq_sc[h] - cn[h:h + 1, :]
            _flash_update(jnp.where(causal, s, NEG), v, m_sc, l_sc, acc_sc, h)
            o_ref[:, cols] = (acc_sc[h] / jnp.sum(l_sc[h], axis=-1, keepdims=True)).astype(BF16)


def _fox_sample(qcat, kcat, vb, cache_kt, cache_vt, lf_cache_t, lf_new_t):
    n, _, _, p_rows = cache_kt.shape
    t = vb.shape[0] // n
    tk = 4096
    nct = p_rows // tk
    u = jnp.asarray(np.pad(np.triu(np.ones((t, t))), ((0, 0), (0, LANES - t))), BF16)
    rev = lambda j: nct - 1 - j
    cache = pl.BlockSpec((None, N_HEADS, HEAD_DIM, tk), lambda b, j: (b, 0, 0, rev(j)))
    new_cat = pl.BlockSpec((None, N_PAIRS, t, 2 * LANES), lambda b, j: (0, 0, b, 0))
    return pl.pallas_call(
        functools.partial(_fox_sample_kernel, n_cache_tiles=nct, sub_keys=tk),
        grid=(n, nct),
        in_specs=[
            new_cat, cache, cache, new_cat,
            pl.BlockSpec((t, WIDTH), lambda b, j: (b, 0)),
            pl.BlockSpec((None, N_HEADS, tk), lambda b, j: (b, 0, rev(j))),
            pl.BlockSpec((None, N_HEADS, t), lambda b, j: (b, 0, 0)),
            _const_spec(u.shape),
        ],
        out_specs=pl.BlockSpec((t, WIDTH), lambda b, j: (b, 0)),
        out_shape=jax.ShapeDtypeStruct(vb.shape, BF16),
        scratch_shapes=[pltpu.VMEM((N_HEADS, t, LANES), F32), pltpu.VMEM((N_HEADS, LANES), F32),
                        pltpu.VMEM((N_HEADS, t, LANES), F32), pltpu.VMEM((N_HEADS, t, LANES), F32),
                        pltpu.VMEM((N_HEADS, t, HEAD_DIM), F32)],
        compiler_params=_params(2),
        name="fox_sample",
    )(qcat, cache_kt, cache_vt, kcat, vb, lf_cache_t, lf_new_t, u)


GROUP_LANE = 64
ROW_ALIGN = 16
MOE_TILE = 512
SORT_ROWS = 640
XS_WIDTH = D_MODEL + 2 * LANES


def _route(r):
    lane_i = lax.broadcasted_iota(jnp.int32, r.shape, 1)
    lane = lane_i.astype(F32)
    lane_grp = (lane_i // EXPERTS_PER_GROUP).astype(F32)
    big = float(LANES)
    is_coarse = (lane_i >= N_EXPERTS) & (lane_i < N_EXPERTS + N_GROUPS)
    cm = jnp.where(is_coarse, r, NEG)
    cmax = cm.max(axis=-1, keepdims=True)
    grp = jnp.min(jnp.where(cm == cmax, lane - N_EXPERTS, big), axis=-1, keepdims=True)
    pg_sel = 1.0 / jnp.sum(jnp.exp(cm - cmax), axis=-1, keepdims=True)
    in_grp = (lane_i < N_EXPERTS) & (lane_grp == grp)
    fm = jnp.where(in_grp, r, NEG)
    m1 = fm.max(axis=-1, keepdims=True)
    denom = jnp.sum(jnp.exp(fm - m1), axis=-1, keepdims=True)
    i1 = jnp.min(jnp.where(fm == m1, lane, big), axis=-1, keepdims=True)
    fm2 = jnp.where(lane == i1, NEG, fm)
    m2 = fm2.max(axis=-1, keepdims=True)
    i2 = jnp.min(jnp.where(fm2 == m2, lane, big), axis=-1, keepdims=True)
    p1 = 1.0 / denom
    p2 = jnp.exp(m2 - m1) / denom
    tot = p1 + p2
    comb = (jnp.where(lane == i1, pg_sel * (p1 / tot), 0.0)
            + jnp.where(lane == i2, pg_sel * (p2 / tot), 0.0))
    return comb, grp


def _merge_kernel(x1_ref, oa1_ref, ob1_ref, x2_ref, oa2_ref, ob2_ref, g_ref, wg_ref, wpa_ref, wpb_ref, wo_ref,
                  gf_ref, wrh_ref, wrm_ref, br_ref, y_ref, hx_ref, route_ref, grp_t_ref, cnt_ref,
                  x_sc, oa_sc, ob_sc, *, tiles_first):
    i = pl.program_id(0)

    @pl.when(i < tiles_first)
    def _():
        x_sc[...], oa_sc[...], ob_sc[...] = x1_ref[...], oa1_ref[...], ob1_ref[...]

    @pl.when(i >= tiles_first)
    def _():
        x_sc[...], oa_sc[...], ob_sc[...] = x2_ref[...], oa2_ref[...], ob2_ref[...]

    x = x_sc[...]
    h = _rms(x, g_ref[...]).astype(BF16)
    gate = jax.nn.sigmoid(_dot_nt(h, wg_ref[...]))
    mix = (gate[:, :D_MODEL] * _dot(oa_sc[...], wpa_ref[...])
           + gate[:, D_MODEL:] * _dot(ob_sc[...], wpb_ref[...]))
    y = x + _dot(mix.astype(BF16), wo_ref[...])
    y_ref[...] = y

    hx = _rms(y, gf_ref[...])
    hx_ref[:, 0:D_MODEL] = hx.astype(BF16)
    h_hi, h_mid, _ = _split3(hx)
    r = _dot(h_hi, wrh_ref[...]) + _dot(h_hi, wrm_ref[...]) + _dot(h_mid, wrh_ref[...]) + br_ref[...]
    comb, grp = _route(r)
    lane = lax.broadcasted_iota(jnp.int32, comb.shape, 1)
    route = jnp.where(lane == GROUP_LANE, grp, comb)
    route_ref[...] = route
    r_hi = route.astype(BF16)
    hx_ref[:, D_MODEL:D_MODEL + LANES] = r_hi
    hx_ref[:, D_MODEL + LANES:] = (route - r_hi.astype(F32)).astype(BF16)
    grp_t_ref[...] = route.T[GROUP_LANE:GROUP_LANE + 8, :]
    cnt = jnp.sum(jnp.where(lane.astype(F32) == grp, 1.0, 0.0), axis=0, keepdims=True)
    cnt_ref[...] = jnp.broadcast_to(cnt, cnt_ref.shape)


def _two_stream_maps(tiles_first):
    first = lambda i, *_: (jnp.minimum(i, tiles_first - 1), 0)
    second = lambda i, *_: (jnp.maximum(i - tiles_first, 0), 0)
    return first, second


def _merge(x1, oa1, ob1, x2, oa2, ob2, w):
    tm = MOE_TILE
    tiles_first = x1.shape[0] // tm
    t_total = x1.shape[0] + x2.shape[0]
    n_tiles = t_total // tm
    first, second = _two_stream_maps(tiles_first)
    row = lambda n: pl.BlockSpec((tm, n), lambda i: (i, 0))
    stream = lambda m: [pl.BlockSpec((tm, D_MODEL), m), pl.BlockSpec((tm, WIDTH), m), pl.BlockSpec((tm, WIDTH), m)]
    consts = [w["g_mix"], w["w_gate"], w["w_pa"], w["w_pb"], w["w_o"], w["g_ffn"], w["w_router_hi"],
              w["w_router_mid"], w["b_router"]]
    return pl.pallas_call(
        functools.partial(_merge_kernel, tiles_first=tiles_first),
        grid=(n_tiles,),
        in_specs=stream(first) + stream(second) + [_const_spec(c.shape) for c in consts],
        out_specs=[row(D_MODEL), row(XS_WIDTH), row(LANES), pl.BlockSpec((8, tm), lambda i: (0, i)),
                   pl.BlockSpec((None, 8, LANES), lambda i: (i, 0, 0))],
        out_shape=[jax.ShapeDtypeStruct((t_total, D_MODEL), F32), jax.ShapeDtypeStruct((t_total, XS_WIDTH), BF16),
                   jax.ShapeDtypeStruct((t_total, LANES), F32), jax.ShapeDtypeStruct((8, t_total), F32),
                   jax.ShapeDtypeStruct((n_tiles, 8, LANES), F32)],
        scratch_shapes=[pltpu.VMEM((tm, D_MODEL), F32), pltpu.VMEM((tm, WIDTH), BF16),
                        pltpu.VMEM((tm, WIDTH), BF16)],
        compiler_params=_params(1),
        name="merge",
    )(x1, oa1, ob1, x2, oa2, ob2, *consts)


def _plan_kernel(cnt_ref, ib_ref, dst_ref, len_ref, fill_ref, inblk_ref, outblk_ref, grp_ref, kind_ref, *,
                 n_tiles, n_steps):
    align_bits = ROW_ALIGN.bit_length() - 1
    tile_bits = MOE_TILE.bit_length() - 1
    zero = jnp.int32(0)

    def per_tile(t, tot):
        run = zero
        new_tot = []
        for g in range(N_GROUPS):
            n = ((cnt_ref[t * N_GROUPS + g] + (ROW_ALIGN - 1)) >> align_bits) << align_bits
            ib_ref[t * N_GROUPS + g] = run
            len_ref[t * N_GROUPS + g] = n
            dst_ref[t * N_GROUPS + g] = tot[g]
            run = run + n
            new_tot.append(tot[g] + n)
        return tuple(new_tot)

    tot = lax.fori_loop(0, n_tiles, per_tile, (zero,) * N_GROUPS)
    n_ffn = [(tot[g] + (MOE_TILE - 1)) >> tile_bits for g in range(N_GROUPS)]
    g_base, ends, acc_rows, acc_tiles = [], [], zero, zero
    for g in range(N_GROUPS):
        g_base.append(acc_rows)
        acc_rows = acc_rows + ((n_ffn[g] + 1) << tile_bits)
        acc_tiles = acc_tiles + n_ffn[g] + 1
        ends.append(acc_tiles)

    def add_base(t, carry):
        for g in range(N_GROUPS):
            dst_ref[t * N_GROUPS + g] = dst_ref[t * N_GROUPS + g] + g_base[g]
        return carry

    lax.fori_loop(0, n_tiles, add_base, zero)
    for g in range(N_GROUPS):
        fill_ref[g] = g_base[g] + tot[g]
        fill_ref[N_GROUPS + g] = g_base[g] + (n_ffn[g] << tile_bits)
    fill_ref[2 * N_GROUPS] = ends[-1]
    first_blk = g_base[N_GROUPS - 1]
    for g in reversed(range(N_GROUPS - 1)):
        first_blk = jnp.where(n_ffn[g] > 0, g_base[g], first_blk)
    first_blk = first_blk >> tile_bits

    def pick(vals, g):
        out = vals[N_GROUPS - 1]
        for i in reversed(range(N_GROUPS - 1)):
            out = jnp.where(g == i, vals[i], out)
        return out

    def per_step(k, carry):
        kk = jnp.minimum(k, ends[-1] - 1)
        g = zero
        for i in range(N_GROUPS - 1):
            g = g + jnp.where(kk >= ends[i], 1, 0)
        j = kk - pick([zero] + ends[:-1], g)
        is_expert = j < pick(n_ffn, g)
        out_blk = jnp.where(k < ends[-1], (pick(g_base, g) >> tile_bits) + j, k)
        outblk_ref[k] = out_blk
        inblk_ref[k] = jnp.where(is_expert, out_blk, first_blk)
        grp_ref[k] = g
        kind_ref[k] = jnp.where(is_expert, 1, 0)
        return carry

    lax.fori_loop(0, n_steps, per_step, zero)


def _moe_plan(cnt, t_total):
    n_tiles = t_total // MOE_TILE
    n_steps = (n_tiles + 2 * N_GROUPS + (N_GROUPS * (ROW_ALIGN - 1) * n_tiles + MOE_TILE - 1) // MOE_TILE)
    smem = pl.BlockSpec(memory_space=pltpu.SMEM)
    sizes = [n_tiles * N_GROUPS] * 3 + [2 * N_GROUPS + 1] + [n_steps] * 4
    outs = pl.pallas_call(
        functools.partial(_plan_kernel, n_tiles=n_tiles, n_steps=n_steps),
        grid_spec=pltpu.PrefetchScalarGridSpec(num_scalar_prefetch=1, grid=(1,), in_specs=[],
                                               out_specs=[smem] * len(sizes)),
        out_shape=[jax.ShapeDtypeStruct((n,), jnp.int32) for n in sizes],
        compiler_params=_params(1),
        name="moe_plan",
    )(cnt.reshape(-1))
    return (*outs, n_steps)


def _sort_kernel(ib_ref, dst_ref, len_ref, fill_ref, hx_ref, grp_t_ref, tri_ref, xs_ref, xsort, sems):
    i = pl.program_id(0)
    n = pl.num_programs(0)
    tm = hx_ref.shape[0]
    half = tm // 2
    slot = i % 2

    def copy(src, dst, rows, sl, g, part):
        return pltpu.make_async_copy(xsort.at[sl, pl.ds(pl.multiple_of(src, ROW_ALIGN), rows)],
                                     xs_ref.at[pl.ds(pl.multiple_of(dst, ROW_ALIGN), rows)], sems.at[sl, g, part])

    def for_segments(step, sl, act):
        for g in range(N_GROUPS):
            src, dst = ib_ref[step * N_GROUPS + g], dst_ref[step * N_GROUPS + g]
            act(copy(src, dst, half, sl, g, 0))

            @pl.when(len_ref[step * N_GROUPS + g] > half)
            def _():
                act(copy(src + half, dst + half, half, sl, g, 1))

    def run(cs):
        for c in cs:
            c.start()
        for c in cs:
            c.wait()

    @pl.when(i == 0)
    def _():
        for sl in range(2):
            xsort[sl, SORT_ROWS:, :] = jnp.zeros((xsort.shape[1] - SORT_ROWS, XS_WIDTH), BF16)

    g_row = grp_t_ref[0:1, :]
    sub = lax.broadcasted_iota(jnp.int32, (8, tm), 0).astype(F32)
    mine = sub == g_row
    before = _dot(jnp.where(mine, 1.0, 0.0).astype(BF16), tri_ref[...])
    dest = jnp.sum(jnp.where(mine, before, 0.0), axis=0, keepdims=True)
    for g in range(N_GROUPS):
        base = jnp.full(dest.shape, ib_ref[i * N_GROUPS + g], jnp.int32).astype(F32)
        dest = dest + jnp.where(g_row == g, base, 0.0)
    rows = lax.broadcasted_iota(jnp.int32, (SORT_ROWS, tm), 0).astype(F32)
    perm = jnp.where(rows == dest, 1.0, 0.0).astype(BF16)
    xsort[slot, 0:SORT_ROWS, :] = _dot(perm, hx_ref[...]).astype(BF16)

    @pl.when(i > 0)
    def _():
        for_segments(i - 1, 1 - slot, lambda c: c.wait())

    for_segments(i, slot, lambda c: c.start())

    @pl.when(i == n - 1)
    def _():
        for_segments(i, slot, lambda c: c.wait())
        run([copy(SORT_ROWS, fill_ref[g], tm, slot, g, 0) for g in range(N_GROUPS)])
        run([copy(SORT_ROWS, fill_ref[N_GROUPS + g], tm, slot, g, 0) for g in range(N_GROUPS)])

        @pl.loop(fill_ref[2 * N_GROUPS], xs_ref.shape[0] // tm)
        def _(b):
            run([copy(SORT_ROWS, b * tm, tm, slot, 0, 0)])


def _expert_kernel(ib_ref, ob_ref, grp_ref, kind_ref, xs_ref, w1_ref, w3_ref, w2_ref, ys_ref):
    del ib_ref, ob_ref
    k = pl.program_id(0)

    @pl.when(kind_ref[k] == 1)
    def _():
        x = xs_ref[:, 0:D_MODEL]
        side = lambda w_ref: jnp.concatenate([w_ref[e].astype(BF16) for e in range(EXPERTS_PER_GROUP)], axis=1)
        hid = jax.nn.silu(_dot(x, side(w1_ref))) * _dot(x, side(w3_ref))
        comb = (xs_ref[:, D_MODEL:D_MODEL + LANES].astype(F32) + xs_ref[:, D_MODEL + LANES:].astype(F32))
        lane = lax.broadcasted_iota(jnp.int32, comb.shape, 1)
        first = grp_ref[k] * EXPERTS_PER_GROUP
        scale = []
        for e in range(EXPERTS_PER_GROUP):
            w_e = jnp.sum(jnp.where(lane == first + e, comb, 0.0), axis=-1, keepdims=True)
            scale.append(jnp.broadcast_to(w_e, (comb.shape[0], D_EXPERT)))
        hid = hid * jnp.concatenate(scale, axis=-1)
        w2 = w2_ref[...].astype(BF16).reshape(EXPERTS_PER_GROUP * D_EXPERT, D_MODEL)
        ys_ref[...] = _dot(hid.astype(BF16), w2).astype(BF16)

    @pl.when(kind_ref[k] == 0)
    def _():
        ys_ref[...] = jnp.zeros_like(ys_ref)


def _unsort_kernel(ib_ref, dst_ref, len_ref, y_ref, route_ref, tri_ref, ys_ref, o1_ref, o2_ref, ybuf, yasm, sems,
                   *, tiles_first):
    i = pl.program_id(0)
    n = pl.num_programs(0)
    tm = y_ref.shape[0]
    half = tm // 2
    slot = i % 2

    def fetch(step, sl, act):
        for g in range(N_GROUPS):
            src = pl.multiple_of(dst_ref[step * N_GROUPS + g], ROW_ALIGN)
            act(pltpu.make_async_copy(ys_ref.at[pl.ds(src, half)], ybuf.at[sl, g, 0:half], sems.at[sl, g, 0]))

            @pl.when(len_ref[step * N_GROUPS + g] > half)
            def _():
                act(pltpu.make_async_copy(ys_ref.at[pl.ds(src + half, half)], ybuf.at[sl, g, half:tm],
                                          sems.at[sl, g, 1]))

    @pl.when(i == 0)
    def _():
        yasm[...] = jnp.zeros_like(yasm)
        fetch(0, 0, lambda c: c.start())

    @pl.when(i + 1 < n)
    def _():
        fetch(i + 1, 1 - slot, lambda c: c.start())

    route = route_ref[...]
    lane = lax.broadcasted_iota(jnp.int32, route.shape, 1).astype(F32)
    grp = jnp.sum(jnp.where(lane == GROUP_LANE, route, 0.0), axis=-1, keepdims=True)
    mine = lane == grp
    before = _dot(tri_ref[...], jnp.where(mine, 1.0, 0.0).astype(BF16))
    dest = jnp.sum(jnp.where(mine, before, 0.0), axis=-1, keepdims=True)
    for g in range(N_GROUPS):
        base = jnp.full(dest.shape, ib_ref[i * N_GROUPS + g], jnp.int32).astype(F32)
        dest = dest + jnp.where(grp == g, base, 0.0)
    cols = lax.broadcasted_iota(jnp.int32, (tm, SORT_ROWS), 1).astype(F32)
    perm_t = jnp.where(cols == dest, 1.0, 0.0).astype(BF16)

    fetch(i, slot, lambda c: c.wait())
    for g in range(N_GROUPS):
        row0 = pl.multiple_of(ib_ref[i * N_GROUPS + g], ROW_ALIGN)
        yasm[pl.ds(row0, half), :] = ybuf[slot, g, 0:half]

        @pl.when(len_ref[i * N_GROUPS + g] > half)
        def _():
            yasm[pl.ds(row0 + half, half), :] = ybuf[slot, g, half:tm]
    out = y_ref[...] + _dot(perm_t, yasm[0:SORT_ROWS, :])

    @pl.when(i < tiles_first)
    def _():
        o1_ref[...] = out

    @pl.when(i >= tiles_first)
    def _():
        o2_ref[...] = out


def _moe(y, hx, route, grp_t, cnt, w, rows_first):
    t_total = y.shape[0]
    tm = MOE_TILE
    n_tiles = t_total // tm
    assert SORT_ROWS >= tm + N_GROUPS * (ROW_ALIGN - 1)
    in_base, dst, seg_len, fill, in_blk, out_blk, grp_of_step, is_expert, n_steps = _moe_plan(
        cnt[:, 0, :N_GROUPS].astype(jnp.int32), t_total)
    cap_rows = n_steps * tm
    lower = jnp.asarray(np.tril(np.ones((tm, tm)), -1), BF16)
    upper = jnp.asarray(np.triu(np.ones((tm, tm)), 1), BF16)
    stage_rows = SORT_ROWS + tm
    any_spec = pl.BlockSpec(memory_space=pl.ANY)

    xs = pl.pallas_call(
        _sort_kernel,
        grid_spec=pltpu.PrefetchScalarGridSpec(
            num_scalar_prefetch=4,
            grid=(n_tiles,),
            in_specs=[pl.BlockSpec((tm, XS_WIDTH), lambda i, *_: (i, 0)),
                      pl.BlockSpec((8, tm), lambda i, *_: (0, i)),
                      pl.BlockSpec((tm, tm), lambda i, *_: (0, 0))],
            out_specs=any_spec,
            scratch_shapes=[pltpu.VMEM((2, stage_rows, XS_WIDTH), BF16),
                            pltpu.SemaphoreType.DMA((2, N_GROUPS, 2))],
        ),
        out_shape=jax.ShapeDtypeStruct((cap_rows, XS_WIDTH), BF16),
        compiler_params=_params(1),
        name="moe_sort",
    )(in_base, dst, seg_len, fill, hx, grp_t, upper)

    step_map = lambda which: (lambda k, ib, ob, grp, kind: ((ib, ob, grp)[which][k], 0))
    wmap = lambda k, ib, ob, grp, kind: (grp[k], 0, 0)
    wspec = pl.BlockSpec((EXPERTS_PER_GROUP, D_MODEL, D_EXPERT), wmap)
    ys = pl.pallas_call(
        _expert_kernel,
        grid_spec=pltpu.PrefetchScalarGridSpec(
            num_scalar_prefetch=4,
            grid=(n_steps,),
            in_specs=[pl.BlockSpec((tm, XS_WIDTH), step_map(0)), wspec, wspec, pl.BlockSpec((EXPERTS_PER_GROUP, D_EXPERT, D_MODEL), wmap)],
            out_specs=pl.BlockSpec((tm, D_MODEL), step_map(1)),
        ),
        out_shape=jax.ShapeDtypeStruct((cap_rows, D_MODEL), BF16),
        compiler_params=_params(1),
        name="moe_experts",
    )(in_blk, out_blk, grp_of_step, is_expert, xs, w["w1"], w["w3"], w["w2"])

    first, second = _two_stream_maps(rows_first // tm)
    return pl.pallas_call(
        functools.partial(_unsort_kernel, tiles_first=rows_first // tm),
        grid_spec=pltpu.PrefetchScalarGridSpec(
            num_scalar_prefetch=3,
            grid=(n_tiles,),
            in_specs=[pl.BlockSpec((tm, D_MODEL), lambda i, *_: (i, 0)),
                      pl.BlockSpec((tm, LANES), lambda i, *_: (i, 0)),
                      pl.BlockSpec((tm, tm), lambda i, *_: (0, 0)),
                      any_spec],
            out_specs=[pl.BlockSpec((tm, D_MODEL), first), pl.BlockSpec((tm, D_MODEL), second)],
            scratch_shapes=[pltpu.VMEM((2, N_GROUPS, tm, D_MODEL), BF16),
                            pltpu.VMEM((stage_rows, D_MODEL), BF16),
                            pltpu.SemaphoreType.DMA((2, N_GROUPS, 2))],
        ),
        out_shape=[jax.ShapeDtypeStruct((rows_first, D_MODEL), F32),
                   jax.ShapeDtypeStruct((t_total - rows_first, D_MODEL), F32)],
        compiler_params=_params(1),
        name="moe_unsort",
    )(in_base, dst, seg_len, y, route, lower, ys)


def _prep_weights(g_mix, w_in, b_f, q_norm_a, k_norm_a, q_norm_b, k_norm_b, w_pa, w_pb, w_o,
                  g_ffn, w_rg, b_rg, w_re, b_re, w1, w3, w2):
    n_qkv = 6 * WIDTH
    tile = lambda g: jnp.tile(g, N_HEADS)
    w_router = jnp.concatenate(
        [jnp.transpose(w_re, (1, 0, 2)).reshape(D_MODEL, N_EXPERTS), w_rg,
         jnp.zeros((D_MODEL, LANES - N_EXPERTS - N_GROUPS), F32)], axis=1)
    b_router = jnp.concatenate(
        [b_re.reshape(N_EXPERTS), b_rg, jnp.zeros((LANES - N_EXPERTS - N_GROUPS,), F32)])[None, :]
    return {
        "g_mix": g_mix[None, :],
        "w_qkv": w_in.T[:n_qkv].astype(BF16),
        "w_f": jnp.pad(w_in.T[n_qkv:n_qkv + N_HEADS], ((0, LANES - N_HEADS), (0, 0))).astype(BF16),
        "b_f": jnp.pad(b_f, (0, LANES - N_HEADS))[None, :],
        "w_gate": w_in.T[n_qkv + N_HEADS:].astype(BF16),
        "gains": jnp.stack([tile(q_norm_a), tile(k_norm_a), tile(q_norm_b), tile(k_norm_b)]),
        "w_pa": w_pa.astype(BF16), "w_pb": w_pb.astype(BF16), "w_o": w_o.astype(BF16),
        "g_ffn": g_ffn[None, :],
        "w_router_hi": w_router.astype(BF16),
        "w_router_mid": (w_router - w_router.astype(BF16).astype(F32)).astype(BF16),
        "b_router": b_router,
        "w1": w1, "w3": w3, "w2": w2,
    }


def kernel(x_prompt, x_sample, cache_a_k, cache_a_v, cache_b_k, cache_b_v, cache_b_logf, g_mix, w_in, b_f, q_norm_a, k_norm_a, q_norm_b, k_norm_b, rel_bias, w_pa, w_pb, w_o, g_ffn, w_rg, b_rg, w_re, b_re, w1, w3, w2):
    assert g_mix.shape[0] == 1, "single-layer step"
    n_p, seq, _ = x_prompt.shape
    n_s, t_s, _ = x_sample.shape
    a_rows = cache_a_k.shape[2]
    w = _prep_weights(g_mix[0], w_in[0], b_f[0], q_norm_a[0], k_norm_a[0], q_norm_b[0], k_norm_b[0],
                      w_pa[0], w_pb[0], w_o[0], g_ffn[0], w_rg[0], b_rg[0], w_re[0], b_re[0],
                      w1[0], w3[0], w2[0])
    band_tq = 256
    bias_prompt = _band_bias(rel_bias[0], band_tq, 3 * band_tq, 2 * band_tq, True, 3 * band_tq)
    bias_sample = _band_bias(rel_bias[0], t_s, a_rows + LANES, a_rows, False, a_rows + t_s)

    seq_minor = lambda a: jnp.transpose(a, (0, 2, 3, 1))
    seq_major = lambda a: jnp.transpose(a, (0, 3, 1, 2))[None]

    xp = x_prompt.reshape(n_p * seq, D_MODEL)
    (qa, ka, va, ka_t, va_t, qcat, kcat, vb, kb_t, vb_t, logf_t) = _inproj(xp, n_p, w, seq_minor=True)
    o_a = _band_prompt(qa, ka, va, *bias_prompt, n_p)
    o_b = _fox_prompt(qcat, kcat, vb, n_p)

    xs = x_sample.reshape(n_s * t_s, D_MODEL)
    (qa_s, _, _, ka_fs, va_fs, qcat_s, kcat_s, vb_s, kb_fs, vb_fs, logf_s) = _inproj(
        xs, 1, w, seq_minor=False)
    o_as, new_ak_t, new_av_t = _band_sample(
        qa_s, ka_fs, va_fs, seq_minor(cache_a_k[0]), seq_minor(cache_a_v[0]), *bias_sample)
    lf_cache_t = jnp.transpose(cache_b_logf[0], (0, 2, 1))
    lf_new_t = jnp.transpose(logf_s.reshape(n_s, t_s, N_HEADS), (0, 2, 1))
    o_bs = _fox_sample(qcat_s, kcat_s, vb_s, seq_minor(cache_b_k[0]), seq_minor(cache_b_v[0]),
                       lf_cache_t, lf_new_t)

    y_p, y_s = _moe(*_merge(xp, o_a, o_b, xs, o_as, o_bs, w), w, rows_first=n_p * seq)

    heads = lambda a, n, r: a.reshape(1, n, r, N_HEADS, HEAD_DIM)
    return (y_p.reshape(n_p, seq, D_MODEL), y_s.reshape(n_s, t_s, D_MODEL),
            seq_major(ka_t), seq_major(va_t), seq_major(kb_t), seq_major(vb_t),
            jnp.transpose(logf_t, (0, 2, 1))[None],
            seq_major(new_ak_t), seq_major(new_av_t),
            heads(kb_fs, n_s, t_s), heads(vb_fs, n_s, t_s), logf_s.reshape(1, n_s, t_s, N_HEADS))
```

```python
import functools

import numpy as np
import jax
import jax.numpy as jnp
from jax import lax
from jax.experimental import pallas as pl
from jax.experimental.pallas import tpu as pltpu

F32 = jnp.float32
BF16 = jnp.bfloat16

D_MODEL = 1024
HEAD_DIM = 64
N_HEADS = 8
WIDTH = N_HEADS * HEAD_DIM
N_PAIRS = N_HEADS // 2
CHUNK = 64
LEFT_CHUNKS = 8
WINDOW_ROWS = LEFT_CHUNKS * CHUNK
MAX_REL = 256
N_GROUPS = 4
EXPERTS_PER_GROUP = 8
N_EXPERTS = N_GROUPS * EXPERTS_PER_GROUP
D_EXPERT = 128
EPS = 1e-6
NEG = -1e30
LOG2E = 1.4426950408889634
LANES = 128
BIAS_PERIOD = 1024
AUG_STRIDE = 8
VMEM_LIMIT = 56 * 1024 * 1024

_NT = (((1,), (1,)), ((), ()))


def _dot(a, b):
    return jnp.dot(a, b, preferred_element_type=F32)


def _dot_nt(a, b):
    return lax.dot_general(a, b, _NT, preferred_element_type=F32)


def _split3(x):
    hi = x.astype(BF16)
    r = x - hi.astype(F32)
    mid = r.astype(BF16)
    lo = (r - mid.astype(F32)).astype(BF16)
    return hi, mid, lo


def _dot3(a_bf, x):
    hi, mid, lo = _split3(x)
    return _dot(a_bf, hi) + _dot(a_bf, mid) + _dot(a_bf, lo)


def _rms(x, g):
    ms = jnp.mean(x * x, axis=-1, keepdims=True)
    return x * lax.rsqrt(ms + EPS) * g


def _params(n_axes):
    return pltpu.CompilerParams(dimension_semantics=("arbitrary",) * n_axes,
                                vmem_limit_bytes=VMEM_LIMIT)


def _const_spec(shape):
    nd = len(shape)
    return pl.BlockSpec(shape, lambda *_: (0,) * nd)


def _inproj_kernel(x_ref, g_ref, wqkv_ref, wf_ref, bf_ref, gains_ref, bd_ref, ltri_ref,
                   selq_ref, selk_ref, oneq_ref, onek_ref,
                   qa_ref, ka_ref, va_ref, kaf_ref, vaf_ref, qcat_ref, kcat_ref, vb_ref,
                   kbf_ref, vbf_ref, logf_ref, carry_ref, *, tiles_per_seq, seq_minor):
    i = pl.program_id(0)
    tm = x_ref.shape[0]

    @pl.when(i % tiles_per_seq == 0)
    def _():
        carry_ref[...] = jnp.zeros_like(carry_ref)

    def put_f32(ref, y, last_tile_only=False):
        if not seq_minor:
            ref[...] = y
        elif last_tile_only:
            @pl.when(i % tiles_per_seq == tiles_per_seq - 1)
            def _():
                ref[...] = y.T.reshape(N_HEADS, HEAD_DIM, tm)
        else:
            ref[...] = y.T.reshape(N_HEADS, HEAD_DIM, tm)

    h = _rms(x_ref[...], g_ref[...]).astype(BF16)

    def seg(s):
        return _dot_nt(h, wqkv_ref[s * WIDTH:(s + 1) * WIDTH, :])

    def headnorm(y, n):
        ss = _dot((y * y).astype(BF16), bd_ref[...])
        return y * lax.rsqrt(ss * (1.0 / HEAD_DIM) + EPS) * gains_ref[n:n + 1, :]

    q_a = headnorm(seg(0), 0)
    qa_ref[...] = (q_a * (HEAD_DIM ** -0.5 * LOG2E)).astype(BF16)
    k_a = headnorm(seg(1), 1)
    ka_ref[...] = k_a.astype(BF16)
    put_f32(kaf_ref, k_a, last_tile_only=True)
    v_a = seg(2)
    va_ref[...] = v_a.astype(BF16)
    put_f32(vaf_ref, v_a, last_tile_only=True)

    z = _dot_nt(h, wf_ref[...]) + bf_ref[...]
    logf = jnp.minimum(z, 0.0) - jnp.log(1.0 + jnp.exp(-jnp.abs(z)))
    if seq_minor:
        logf_ref[...] = logf.T[0:N_HEADS, :]
    else:
        logf_ref[...] = logf[:, :N_HEADS]
    c = _dot3(ltri_ref[...], logf) + carry_ref[0:1, :]
    carry_ref[...] = jnp.broadcast_to(c[-1:, :], carry_ref.shape)
    cs = jnp.concatenate(_split3(c * LOG2E), axis=-1)
    q_aug = (_dot(cs, selq_ref[...]) + oneq_ref[...]).astype(BF16)
    k_aug = (_dot(cs, selk_ref[...]) + onek_ref[...]).astype(BF16)

    q_b = (headnorm(seg(3), 2) * (HEAD_DIM ** -0.5 * LOG2E)).astype(BF16)
    k_b = headnorm(seg(4), 3)
    put_f32(kbf_ref, k_b)
    k_b = k_b.astype(BF16)
    v_b = seg(5)
    put_f32(vbf_ref, v_b)
    vb_ref[...] = v_b.astype(BF16)
    for p in range(N_PAIRS):
        cols = slice(p * LANES, (p + 1) * LANES)
        qcat_ref[p, :, 0:LANES] = q_b[:, cols]
        qcat_ref[p, :, LANES:2 * LANES] = q_aug
        kcat_ref[p, :, 0:LANES] = k_b[:, cols]
        kcat_ref[p, :, LANES:2 * LANES] = k_aug


def _aug_constants():
    selq = np.zeros((3 * LANES, LANES), np.float32)
    selk = np.zeros((3 * LANES, LANES), np.float32)
    oneq = np.zeros((1, LANES), np.float32)
    onek = np.zeros((1, LANES), np.float32)
    for h in range(N_HEADS):
        for k in range(3):
            selq[k * LANES + h, AUG_STRIDE * h + k] = 1.0
            selk[k * LANES + h, AUG_STRIDE * h + 3 + k] = -1.0
            oneq[0, AUG_STRIDE * h + 3 + k] = 1.0
            onek[0, AUG_STRIDE * h + k] = 1.0
    return (jnp.asarray(selq, BF16), jnp.asarray(selk, BF16), jnp.asarray(oneq), jnp.asarray(onek))


def _inproj(x, n_seq, w, seq_minor):
    t_total = x.shape[0]
    tm = 512
    n_tiles = t_total // tm
    seq = t_total // n_seq
    tps = seq // tm
    row = lambda i: (i, 0)
    flat = lambda dt: (jax.ShapeDtypeStruct((t_total, WIDTH), dt), pl.BlockSpec((tm, WIDTH), row))
    if seq_minor:
        assert tm == WINDOW_ROWS
        band_f32 = (jax.ShapeDtypeStruct((n_seq, N_HEADS, HEAD_DIM, WINDOW_ROWS), F32),
                    pl.BlockSpec((None, N_HEADS, HEAD_DIM, tm), lambda i: (i // tps, 0, 0, 0)))
        fox_f32 = (jax.ShapeDtypeStruct((n_seq, N_HEADS, HEAD_DIM, seq), F32),
                   pl.BlockSpec((None, N_HEADS, HEAD_DIM, tm), lambda i: (i // tps, 0, 0, i % tps)))
        logf = (jax.ShapeDtypeStruct((n_seq, N_HEADS, seq), F32),
                pl.BlockSpec((None, N_HEADS, tm), lambda i: (i // tps, 0, i % tps)))
    else:
        band_f32 = fox_f32 = flat(F32)
        logf = (jax.ShapeDtypeStruct((t_total, N_HEADS), F32), pl.BlockSpec((tm, N_HEADS), row))
    cat = (jax.ShapeDtypeStruct((n_seq, N_PAIRS, seq, 2 * LANES), BF16),
           pl.BlockSpec((None, N_PAIRS, tm, 2 * LANES), lambda i: (i // tps, 0, i % tps, 0)))
    outs = [flat(BF16), flat(BF16), flat(BF16), band_f32, band_f32, cat, cat, flat(BF16),
            fox_f32, fox_f32, logf]
    bd = jnp.asarray(np.kron(np.eye(N_HEADS), np.ones((HEAD_DIM, HEAD_DIM))), BF16)
    ltri = jnp.asarray(np.tril(np.ones((tm, tm))), BF16)
    consts = [w["g_mix"], w["w_qkv"], w["w_f"], w["b_f"], w["gains"], bd, ltri, *_aug_constants()]
    in_specs = [pl.BlockSpec((tm, D_MODEL), row)] + [_const_spec(c.shape) for c in consts]
    return pl.pallas_call(
        functools.partial(_inproj_kernel, tiles_per_seq=tps, seq_minor=seq_minor),
        grid=(n_tiles,),
        in_specs=in_specs,
        out_specs=[o[1] for o in outs],
        out_shape=[o[0] for o in outs],
        scratch_shapes=[pltpu.VMEM((8, LANES), F32)],
        compiler_params=_params(1),
        name="inproj",
    )(x, *consts)


def _softmax_pv(scores, values):
    chunks = [[s[:, c * LANES:(c + 1) * LANES] for c in range(s.shape[1] // LANES)] for s in scores]
    cmax = None
    for cs in chunks:
        for c in cs:
            cmax = c if cmax is None else jnp.maximum(cmax, c)
    m = jnp.max(cmax, axis=-1, keepdims=True)
    pv = None
    lsum = None
    for cs, v in zip(chunks, values):
        ps = [jnp.exp2(c - m) for c in cs]
        for p in ps:
            lsum = p if lsum is None else lsum + p
        o = _dot(jnp.concatenate(ps, axis=-1).astype(BF16), v)
        pv = o if pv is None else pv + o
    return pv, jnp.sum(lsum, axis=-1, keepdims=True)


def _head_mask(hh):
    lane = lax.broadcasted_iota(jnp.int32, (1, LANES), 1)
    return (lane // HEAD_DIM) == hh


def _band_prompt_kernel(q_ref, k0_ref, k1_ref, k2_ref, k3_ref, v0_ref, v1_ref, v2_ref, v3_ref, e_ref, mask_ref,
                        o_ref, bias_ref, *, tq):
    i = pl.program_id(1)

    @pl.when((pl.program_id(0) == 0) & (i == 0))
    def _():
        _build_bias(e_ref, mask_ref, bias_ref)

    k_refs = (k0_ref, k1_ref, k2_ref, k3_ref)
    v_refs = (v0_ref, v1_ref, v2_ref, v3_ref)
    for qb in range(2):
        pad = [jnp.where(2 * i + qb - 2 + j >= 0, 0.0, NEG) for j in range(2)] + [0.0]
        for p in range(N_PAIRS):
            cols = slice(p * LANES, (p + 1) * LANES)
            q = q_ref[qb * tq:(qb + 1) * tq, cols]
            ks = [r[:, cols] for r in k_refs[qb:qb + 3]]
            vs = [r[:, cols] for r in v_refs[qb:qb + 3]]
            out = jnp.zeros((tq, LANES), F32)
            for hh in range(2):
                msk = _head_mask(hh)
                qh = jnp.where(msk, q, jnp.zeros_like(q))
                scores = [_dot_nt(qh, ks[j]) + bias_ref[2 * p + hh, :, j * tq:(j + 1) * tq] + pad[j]
                          for j in range(3)]
                pv, l = _softmax_pv(scores, vs)
                out = jnp.where(msk, pv / l, out)
            o_ref[qb * tq:(qb + 1) * tq, cols] = out.astype(BF16)


def _band_prompt(qa, ka, va, bias_e, bias_mask, n_seq):
    t_total = qa.shape[0]
    tq = 256
    seq = t_total // n_seq
    nq = seq // tq
    qmap = lambda b, i: (b * (nq // 2) + i, 0)

    def kmap(j):
        return lambda b, i: (b * nq + jnp.maximum(2 * i - 2 + j, 0), 0)

    blk = lambda m: pl.BlockSpec((tq, WIDTH), m)
    qblk = pl.BlockSpec((2 * tq, WIDTH), qmap)
    return pl.pallas_call(
        functools.partial(_band_prompt_kernel, tq=tq),
        grid=(n_seq, nq // 2),
        in_specs=[qblk] + [blk(kmap(j)) for j in range(4)] + [blk(kmap(j)) for j in range(4)]
        + [_const_spec(bias_e.shape), _const_spec(bias_mask.shape)],
        out_specs=qblk,
        out_shape=jax.ShapeDtypeStruct((t_total, WIDTH), BF16),
        scratch_shapes=[pltpu.VMEM((N_HEADS,) + bias_mask.shape, F32)],
        compiler_params=_params(2),
        name="band_prompt",
    )(qa, ka, ka, ka, ka, va, va, va, va, bias_e, bias_mask)


def _band_sample_kernel(q_ref, kc_ref, vc_ref, kn_ref, vn_ref, e_ref, mask_ref, o_ref, ko_ref, vo_ref,
                        bias_ref):
    p_rows = kc_ref.shape[2]
    t = q_ref.shape[0]

    @pl.when(pl.program_id(0) == 0)
    def _():
        _build_bias(e_ref, mask_ref, bias_ref)

    pad = jnp.zeros((LANES - t, HEAD_DIM), BF16)
    for h in range(N_HEADS):
        cols = slice(h * HEAD_DIM, (h + 1) * HEAD_DIM)
        q = q_ref[:, cols]
        k_new, v_new = kn_ref[:, cols], vn_ref[:, cols]
        k_cache, v_cache = kc_ref[h], vc_ref[h]
        s_cache = _dot(q, k_cache.astype(BF16)) + bias_ref[h, :, 0:p_rows]
        s_new = (_dot_nt(q, jnp.concatenate([k_new.astype(BF16), pad], axis=0))
                 + bias_ref[h, :, p_rows:p_rows + LANES])
        chunks = [s_cache[:, c * LANES:(c + 1) * LANES] for c in range(p_rows // LANES)] + [s_new]
        cmax = chunks[0]
        for c in chunks[1:]:
            cmax = jnp.maximum(cmax, c)
        m = jnp.max(cmax, axis=-1, keepdims=True)
        ps = [jnp.exp2(c - m) for c in chunks]
        lsum = ps[0]
        for p in ps[1:]:
            lsum = lsum + p
        pv = (_dot_nt(jnp.concatenate(ps[:-1], axis=-1).astype(BF16), v_cache.astype(BF16))
              + _dot(ps[-1].astype(BF16), jnp.concatenate([v_new.astype(BF16), pad], axis=0)))
        o_ref[:, cols] = (pv / jnp.sum(lsum, axis=-1, keepdims=True)).astype(BF16)
        ko_ref[h] = jnp.concatenate([k_cache[:, t:], k_new.T], axis=1)
        vo_ref[h] = jnp.concatenate([v_cache[:, t:], v_new.T], axis=1)


def _band_sample(qa, ka_f, va_f, cache_kt, cache_vt, bias_e, bias_mask):
    n = cache_kt.shape[0]
    t = qa.shape[0] // n
    row = pl.BlockSpec((t, WIDTH), lambda b: (b, 0))
    cache = pl.BlockSpec((None,) + cache_kt.shape[1:], lambda b: (b, 0, 0, 0))
    return pl.pallas_call(
        _band_sample_kernel,
        grid=(n,),
        in_specs=[row, cache, cache, row, row, _const_spec(bias_e.shape), _const_spec(bias_mask.shape)],
        out_specs=[row, cache, cache],
        out_shape=[jax.ShapeDtypeStruct(qa.shape, BF16),
                   jax.ShapeDtypeStruct(cache_kt.shape, F32),
                   jax.ShapeDtypeStruct(cache_vt.shape, F32)],
        scratch_shapes=[pltpu.VMEM((N_HEADS,) + bias_mask.shape, F32)],
        compiler_params=_params(1),
        name="band_sample",
    )(qa, cache_kt, cache_vt, ka_f, va_f, bias_e, bias_mask)


def _band_bias(rel_bias, n_q, n_k, offset, band_mask, valid_k):
    assert n_q + n_k - 1 <= BIAS_PERIOD
    m = np.arange(BIAS_PERIOD)
    w = np.where(m < n_k, m, m - BIAS_PERIOD)
    idx = np.clip(offset - w, -MAX_REL, MAX_REL) + MAX_REL
    e = jnp.take(rel_bias.astype(F32), jnp.asarray(idx), axis=1) * LOG2E
    qi = np.arange(n_q)[:, None]
    kj = np.arange(n_k)[None, :]
    valid = np.broadcast_to(kj < valid_k, (n_q, n_k))
    if band_mask:
        rel_chunk = (qi + offset) // CHUNK - kj // CHUNK
        valid = valid & (rel_chunk >= 0) & (rel_chunk <= LEFT_CHUNKS)
    return e, jnp.asarray(np.where(valid, 0.0, NEG), F32)


def _build_bias(e_ref, mask_ref, bias_sc):
    rows, cols = mask_ref.shape
    for h in range(N_HEADS):
        spread = jnp.broadcast_to(e_ref[h:h + 1, :], (rows, BIAS_PERIOD))
        bias_sc[h] = pltpu.roll(spread, 0, 1, stride=1, stride_axis=0)[:, :cols] + mask_ref[...]


def _aug_head_mask(h_in_pair, pair):
    lane = lax.broadcasted_iota(jnp.int32, (1, 2 * LANES), 1)
    head = 2 * pair + h_in_pair
    in_q = (lane < LANES) & ((lane // HEAD_DIM) == h_in_pair)
    in_aug = (lane >= LANES) & (((lane - LANES) // AUG_STRIDE) == head)
    return in_q | in_aug


def _flash_step(state, s, v, row_bias=None, v_seq_minor=False):
    m_prev, l_prev, acc_prev = state
    chunks = [s[:, c * LANES:(c + 1) * LANES] for c in range(s.shape[1] // LANES)]
    if row_bias is not None:
        chunks = [c + row_bias for c in chunks]
    cmax = chunks[0]
    for c in chunks[1:]:
        cmax = jnp.maximum(cmax, c)
    m_new = jnp.maximum(m_prev, jnp.max(cmax, axis=-1, keepdims=True))
    alpha = jnp.exp2(m_prev - m_new)
    ps = [jnp.exp2(c - m_new) for c in chunks]
    lsum = ps[0]
    for p in ps[1:]:
        lsum = lsum + p
    p = jnp.concatenate(ps, axis=-1).astype(BF16)
    pv = _dot_nt(p, v) if v_seq_minor else _dot(p, v)
    return m_new, alpha * l_prev + lsum, alpha[:, :pv.shape[1]] * acc_prev + pv


def _flash_update(s, v, m_ref, l_ref, acc_ref, h, row_bias=None, v_seq_minor=False, rows=slice(None)):
    state = (m_ref[h, rows], l_ref[h, rows], acc_ref[h, rows])
    m_ref[h, rows], l_ref[h, rows], acc_ref[h, rows] = _flash_step(state, s, v, row_bias, v_seq_minor)


def _flash_result(l_ref, acc_ref, pair):
    outs = [acc_ref[2 * pair + hh] / jnp.sum(l_ref[2 * pair + hh], axis=-1, keepdims=True)
            for hh in range(2)]
    return jnp.where(_head_mask(0), outs[0], outs[1])


def _fox_prompt_kernel(qi_ref, kj_ref, q_ref, k_ref, v_ref, o_ref, qh_sc, m_sc, l_sc, acc_sc, *, tq, tk):
    t = pl.program_id(1)
    i = qi_ref[t]
    j = kj_ref[t]

    @pl.when(j == 0)
    def _():
        for p in range(N_PAIRS):
            q = q_ref[p]
            for hh in range(2):
                qh_sc[2 * p + hh] = jnp.where(_aug_head_mask(hh, p), q, jnp.zeros_like(q))
        m_sc[...] = jnp.full_like(m_sc, NEG)
        l_sc[...] = jnp.zeros_like(l_sc)
        acc_sc[...] = jnp.zeros_like(acc_sc)

    def step(diagonal):
        half = tq // 2
        if diagonal:
            row = lax.broadcasted_iota(jnp.int32, (half, tk), 0)
            col = lax.broadcasted_iota(jnp.int32, (half, tk), 1)
            keep_first = (col <= row)[:, 0:half]
            keep_second = col <= row + half
        for p in range(N_PAIRS):
            cols = slice(p * LANES, (p + 1) * LANES)
            for hh in range(2):
                h = 2 * p + hh
                if diagonal:
                    s = _dot_nt(qh_sc[h, 0:half, :], k_ref[p, 0:half, :])
                    _flash_update(jnp.where(keep_first, s, NEG), v_ref[0:half, cols], m_sc, l_sc, acc_sc, h,
                                  rows=slice(0, half))
                    s = _dot_nt(qh_sc[h, half:tq, :], k_ref[p])
                    _flash_update(jnp.where(keep_second, s, NEG), v_ref[:, cols], m_sc, l_sc, acc_sc, h,
                                  rows=slice(half, tq))
                else:
                    _flash_update(_dot_nt(qh_sc[h], k_ref[p]), v_ref[:, cols], m_sc, l_sc, acc_sc, h)

    @pl.when(j < i)
    def _():
        step(False)

    @pl.when(j == i)
    def _():
        step(True)
        for p in range(N_PAIRS):
            o_ref[:, p * LANES:(p + 1) * LANES] = _flash_result(l_sc, acc_sc, p).astype(BF16)


def _fox_prompt(qcat, kcat, vb, n_seq):
    seq = qcat.shape[2]
    tq = tk = 1024
    nt = seq // tq
    qi = np.concatenate([np.full(i + 1, i) for i in range(nt)]).astype(np.int32)
    kj = np.concatenate([np.arange(i + 1) for i in range(nt)]).astype(np.int32)
    grid_spec = pltpu.PrefetchScalarGridSpec(
        num_scalar_prefetch=2,
        grid=(n_seq, len(qi)),
        in_specs=[
            pl.BlockSpec((None, N_PAIRS, tq, 2 * LANES), lambda b, t, qi, kj: (b, 0, qi[t], 0)),
            pl.BlockSpec((None, N_PAIRS, tk, 2 * LANES), lambda b, t, qi, kj: (b, 0, kj[t], 0)),
            pl.BlockSpec((tk, WIDTH), lambda b, t, qi, kj: (b * nt + kj[t], 0)),
        ],
        out_specs=pl.BlockSpec((tq, WIDTH), lambda b, t, qi, kj: (b * nt + qi[t], 0)),
        scratch_shapes=[pltpu.VMEM((N_HEADS, tq, 2 * LANES), BF16), pltpu.VMEM((N_HEADS, tq, LANES), F32),
                        pltpu.VMEM((N_HEADS, tq, LANES), F32), pltpu.VMEM((N_HEADS, tq, LANES), F32)],
    )
    return pl.pallas_call(
        functools.partial(_fox_prompt_kernel, tq=tq, tk=tk),
        grid_spec=grid_spec,
        out_shape=jax.ShapeDtypeStruct(vb.shape, BF16),
        compiler_params=_params(2),
        name="fox_prompt",
    )(jnp.asarray(qi), jnp.asarray(kj), qcat, kcat, vb)


def _suffix_sum_exclusive(x):
    n = x.shape[1]
    lane = lax.broadcasted_iota(jnp.int32, x.shape, 1)
    y = jnp.where(lane + 1 < n, pltpu.roll(x, n - 1, axis=1), 0.0)
    shift = 1
    while shift < n:
        y = y + jnp.where(lane + shift < n, pltpu.roll(y, n - shift, axis=1), 0.0)
        shift *= 2
    return y


def _fox_sample_kernel(q_ref, kc_ref, vc_ref, kn_ref, vn_ref, lfc_ref, lfn_ref, u_ref,
                       o_ref, cq_sc, car_sc, m_sc, l_sc, acc_sc, *, n_cache_tiles, sub_keys):
    j = pl.program_id(1)
    t = q_ref.shape[1]

    def cum_new():
        hi, mid, lo = _split3(lfn_ref[...])
        u = u_ref[...]
        return _dot(hi, u) + _dot(mid, u) + _dot(lo, u)

    def q_head(h):
        return q_ref[h // 2, :, (h % 2) * HEAD_DIM:(h % 2 + 1) * HEAD_DIM]

    @pl.when(j == 0)
    def _():
        cn = cum_new() * LOG2E
        eye = (lax.broadcasted_iota(jnp.int32, (t, LANES), 0)
               == lax.broadcasted_iota(jnp.int32, (t, LANES), 1))
        for h in range(N_HEADS):
            col = jnp.sum(jnp.where(eye, jnp.broadcast_to(cn[h:h + 1, :], (t, LANES)), 0.0),
                          axis=-1, keepdims=True)
            cq_sc[h] = jnp.broadcast_to(col, (t, LANES))
        car_sc[...] = jnp.zeros_like(car_sc)
        m_sc[...] = jnp.full_like(m_sc, NEG)
        l_sc[...] = jnp.zeros_like(l_sc)
        acc_sc[...] = jnp.zeros_like(acc_sc)

    lf = lfc_ref[...]
    r = _suffix_sum_exclusive(lf) + car_sc[:, 0:1]
    car_sc[...] = jnp.broadcast_to(r[:, 0:1] + lf[:, 0:1], car_sc.shape)
    r = r * LOG2E
    for sub in range(lf.shape[1] // sub_keys):
        keys = slice(sub * sub_keys, (sub + 1) * sub_keys)
        for h in range(N_HEADS):
            s = _dot(q_head(h), kc_ref[h, :, keys].astype(BF16)) + r[h:h + 1, keys]
            _flash_update(s, vc_ref[h, :, keys].astype(BF16), m_sc, l_sc, acc_sc, h,
                          row_bias=cq_sc[h], v_seq_minor=True)

    @pl.when(j == n_cache_tiles - 1)
    def _():
        cn = cum_new() * LOG2E
        causal = (lax.broadcasted_iota(jnp.int32, (t, LANES), 1)
                  <= lax.broadcasted_iota(jnp.int32, (t, LANES), 0))
        pad = jnp.zeros((LANES - t, HEAD_DIM), BF16)
        for h in range(N_HEADS):
            cols = slice(h * HEAD_DIM, (h + 1) * HEAD_DIM)
            k = jnp.concatenate([kn_ref[h // 2, :, (h % 2) * HEAD_DIM:(h % 2 + 1) * HEAD_DIM], pad], axis=0)
            v = jnp.concatenate([vn_ref[:, cols], pad], axis=0)
            s = _dot_nt(q_head(h), k) + cq_sc[h] - cn[h:h + 1, :]
            _flash_update(jnp.where(causal, s, NEG), v, m_sc, l_sc, acc_sc, h)
            o_ref[:, cols] = (acc_sc[h] / jnp.sum(l_sc[h], axis=-1, keepdims=True)).astype(BF16)


def _fox_sample(qcat, kcat, vb, cache_kt, cache_vt, lf_cache_t, lf_new_t):
    n, _, _, p_rows = cache_kt.shape
    t = vb.shape[0] // n
    tk = 4096
    nct = p_rows // tk
    u = jnp.asarray(np.pad(np.triu(np.ones((t, t))), ((0, 0), (0, LANES - t))), BF16)
    rev = lambda j: nct - 1 - j
    cache = pl.BlockSpec((None, N_HEADS, HEAD_DIM, tk), lambda b, j: (b, 0, 0, rev(j)))
    new_cat = pl.BlockSpec((None, N_PAIRS, t, 2 * LANES), lambda b, j: (0, 0, b, 0))
    return pl.pallas_call(
        functools.partial(_fox_sample_kernel, n_cache_tiles=nct, sub_keys=tk),
        grid=(n, nct),
        in_specs=[
            new_cat, cache, cache, new_cat,
            pl.BlockSpec((t, WIDTH), lambda b, j: (b, 0)),
            pl.BlockSpec((None, N_HEADS, tk), lambda b, j: (b, 0, rev(j))),
            pl.BlockSpec((None, N_HEADS, t), lambda b, j: (b, 0, 0)),
            _const_spec(u.shape),
        ],
        out_specs=pl.BlockSpec((t, WIDTH), lambda b, j: (b, 0)),
        out_shape=jax.ShapeDtypeStruct(vb.shape, BF16),
        scratch_shapes=[pltpu.VMEM((N_HEADS, t, LANES), F32), pltpu.VMEM((N_HEADS, LANES), F32),
                        pltpu.VMEM((N_HEADS, t, LANES), F32), pltpu.VMEM((N_HEADS, t, LANES), F32),
                        pltpu.VMEM((N_HEADS, t, HEAD_DIM), F32)],
        compiler_params=_params(2),
        name="fox_sample",
    )(qcat, cache_kt, cache_vt, kcat, vb, lf_cache_t, lf_new_t, u)


GROUP_LANE = 64
ROW_ALIGN = 16
MOE_TILE = 512
SORT_ROWS = 640
XS_WIDTH = D_MODEL + 2 * LANES


def _route(r):
    lane_i = lax.broadcasted_iota(jnp.int32, r.shape, 1)
    lane = lane_i.astype(F32)
    lane_grp = (lane_i // EXPERTS_PER_GROUP).astype(F32)
    big = float(LANES)
    is_coarse = (lane_i >= N_EXPERTS) & (lane_i < N_EXPERTS + N_GROUPS)
    cm = jnp.where(is_coarse, r, NEG)
    cmax = cm.max(axis=-1, keepdims=True)
    grp = jnp.min(jnp.where(cm == cmax, lane - N_EXPERTS, big), axis=-1, keepdims=True)
    pg_sel = 1.0 / jnp.sum(jnp.exp(cm - cmax), axis=-1, keepdims=True)
    in_grp = (lane_i < N_EXPERTS) & (lane_grp == grp)
    fm = jnp.where(in_grp, r, NEG)
    m1 = fm.max(axis=-1, keepdims=True)
    denom = jnp.sum(jnp.exp(fm - m1), axis=-1, keepdims=True)
    i1 = jnp.min(jnp.where(fm == m1, lane, big), axis=-1, keepdims=True)
    fm2 = jnp.where(lane == i1, NEG, fm)
    m2 = fm2.max(axis=-1, keepdims=True)
    i2 = jnp.min(jnp.where(fm2 == m2, lane, big), axis=-1, keepdims=True)
    p1 = 1.0 / denom
    p2 = jnp.exp(m2 - m1) / denom
    tot = p1 + p2
    comb = (jnp.where(lane == i1, pg_sel * (p1 / tot), 0.0)
            + jnp.where(lane == i2, pg_sel * (p2 / tot), 0.0))
    return comb, grp


def _merge_kernel(x1_ref, oa1_ref, ob1_ref, x2_ref, oa2_ref, ob2_ref, g_ref, wg_ref, wpa_ref, wpb_ref, wo_ref,
                  gf_ref, wrh_ref, wrm_ref, br_ref, y_ref, hx_ref, route_ref, grp_t_ref, cnt_ref,
                  x_sc, oa_sc, ob_sc, *, tiles_first):
    i = pl.program_id(0)

    @pl.when(i < tiles_first)
    def _():
        x_sc[...], oa_sc[...], ob_sc[...] = x1_ref[...], oa1_ref[...], ob1_ref[...]

    @pl.when(i >= tiles_first)
    def _():
        x_sc[...], oa_sc[...], ob_sc[...] = x2_ref[...], oa2_ref[...], ob2_ref[...]

    x = x_sc[...]
    h = _rms(x, g_ref[...]).astype(BF16)
    gate = jax.nn.sigmoid(_dot_nt(h, wg_ref[...]))
    mix = (gate[:, :D_MODEL] * _dot(oa_sc[...], wpa_ref[...])
           + gate[:, D_MODEL:] * _dot(ob_sc[...], wpb_ref[...]))
    y = x + _dot(mix.astype(BF16), wo_ref[...])
    y_ref[...] = y

    hx = _rms(y, gf_ref[...])
    hx_ref[:, 0:D_MODEL] = hx.astype(BF16)
    h_hi, h_mid, _ = _split3(hx)
    r = _dot(h_hi, wrh_ref[...]) + _dot(h_hi, wrm_ref[...]) + _dot(h_mid, wrh_ref[...]) + br_ref[...]
    comb, grp = _route(r)
    lane = lax.broadcasted_iota(jnp.int32, comb.shape, 1)
    route = jnp.where(lane == GROUP_LANE, grp, comb)
    route_ref[...] = route
    r_hi = route.astype(BF16)
    hx_ref[:, D_MODEL:D_MODEL + LANES] = r_hi
    hx_ref[:, D_MODEL + LANES:] = (route - r_hi.astype(F32)).astype(BF16)
    grp_t_ref[...] = route.T[GROUP_LANE:GROUP_LANE + 8, :]
    cnt = jnp.sum(jnp.where(lane.astype(F32) == grp, 1.0, 0.0), axis=0, keepdims=True)
    cnt_ref[...] = jnp.broadcast_to(cnt, cnt_ref.shape)


def _two_stream_maps(tiles_first):
    first = lambda i, *_: (jnp.minimum(i, tiles_first - 1), 0)
    second = lambda i, *_: (jnp.maximum(i - tiles_first, 0), 0)
    return first, second


def _merge(x1, oa1, ob1, x2, oa2, ob2, w):
    tm = MOE_TILE
    tiles_first = x1.shape[0] // tm
    t_total = x1.shape[0] + x2.shape[0]
    n_tiles = t_total // tm
    first, second = _two_stream_maps(tiles_first)
    row = lambda n: pl.BlockSpec((tm, n), lambda i: (i, 0))
    stream = lambda m: [pl.BlockSpec((tm, D_MODEL), m), pl.BlockSpec((tm, WIDTH), m), pl.BlockSpec((tm, WIDTH), m)]
    consts = [w["g_mix"], w["w_gate"], w["w_pa"], w["w_pb"], w["w_o"], w["g_ffn"], w["w_router_hi"],
              w["w_router_mid"], w["b_router"]]
    return pl.pallas_call(
        functools.partial(_merge_kernel, tiles_first=tiles_first),
        grid=(n_tiles,),
        in_specs=stream(first) + stream(second) + [_const_spec(c.shape) for c in consts],
        out_specs=[row(D_MODEL), row(XS_WIDTH), row(LANES), pl.BlockSpec((8, tm), lambda i: (0, i)),
                   pl.BlockSpec((None, 8, LANES), lambda i: (i, 0, 0))],
        out_shape=[jax.ShapeDtypeStruct((t_total, D_MODEL), F32), jax.ShapeDtypeStruct((t_total, XS_WIDTH), BF16),
                   jax.ShapeDtypeStruct((t_total, LANES), F32), jax.ShapeDtypeStruct((8, t_total), F32),
                   jax.ShapeDtypeStruct((n_tiles, 8, LANES), F32)],
        scratch_shapes=[pltpu.VMEM((tm, D_MODEL), F32), pltpu.VMEM((tm, WIDTH), BF16),
                        pltpu.VMEM((tm, WIDTH), BF16)],
        compiler_params=_params(1),
        name="merge",
    )(x1, oa1, ob1, x2, oa2, ob2, *consts)


def _plan_kernel(cnt_ref, ib_ref, dst_ref, len_ref, fill_ref, inblk_ref, outblk_ref, grp_ref, kind_ref, *,
                 n_tiles, n_steps):
    align_bits = ROW_ALIGN.bit_length() - 1
    tile_bits = MOE_TILE.bit_length() - 1
    zero = jnp.int32(0)

    def per_tile(t, tot):
        run = zero
        new_tot = []
        for g in range(N_GROUPS):
            n = ((cnt_ref[t * N_GROUPS + g] + (ROW_ALIGN - 1)) >> align_bits) << align_bits
            ib_ref[t * N_GROUPS + g] = run
            len_ref[t * N_GROUPS + g] = n
            dst_ref[t * N_GROUPS + g] = tot[g]
            run = run + n
            new_tot.append(tot[g] + n)
        return tuple(new_tot)

    tot = lax.fori_loop(0, n_tiles, per_tile, (zero,) * N_GROUPS)
    n_ffn = [(tot[g] + (MOE_TILE - 1)) >> tile_bits for g in range(N_GROUPS)]
    g_base, ends, acc_rows, acc_tiles = [], [], zero, zero
    for g in range(N_GROUPS):
        g_base.append(acc_rows)
        acc_rows = acc_rows + ((n_ffn[g] + 1) << tile_bits)
        acc_tiles = acc_tiles + n_ffn[g] + 1
        ends.append(acc_tiles)

    def add_base(t, carry):
        for g in range(N_GROUPS):
            dst_ref[t * N_GROUPS + g] = dst_ref[t * N_GROUPS + g] + g_base[g]
        return carry

    lax.fori_loop(0, n_tiles, add_base, zero)
    for g in range(N_GROUPS):
        fill_ref[g] = g_base[g] + tot[g]
        fill_ref[N_GROUPS + g] = g_base[g] + (n_ffn[g] << tile_bits)
    fill_ref[2 * N_GROUPS] = ends[-1]
    first_blk = g_base[N_GROUPS - 1]
    for g in reversed(range(N_GROUPS - 1)):
        first_blk = jnp.where(n_ffn[g] > 0, g_base[g], first_blk)
    first_blk = first_blk >> tile_bits

    def pick(vals, g):
        out = vals[N_GROUPS - 1]
        for i in reversed(range(N_GROUPS - 1)):
            out = jnp.where(g == i, vals[i], out)
        return out

    def per_step(k, carry):
        kk = jnp.minimum(k, ends[-1] - 1)
        g = zero
        for i in range(N_GROUPS - 1):
            g = g + jnp.where(kk >= ends[i], 1, 0)
        j = kk - pick([zero] + ends[:-1], g)
        is_expert = j < pick(n_ffn, g)
        out_blk = jnp.where(k < ends[-1], (pick(g_base, g) >> tile_bits) + j, k)
        outblk_ref[k] = out_blk
        inblk_ref[k] = jnp.where(is_expert, out_blk, first_blk)
        grp_ref[k] = g
        kind_ref[k] = jnp.where(is_expert, 1, 0)
        return carry

    lax.fori_loop(0, n_steps, per_step, zero)


def _moe_plan(cnt, t_total):
    n_tiles = t_total // MOE_TILE
    n_steps = (n_tiles + 2 * N_GROUPS + (N_GROUPS * (ROW_ALIGN - 1) * n_tiles + MOE_TILE - 1) // MOE_TILE)
    smem = pl.BlockSpec(memory_space=pltpu.SMEM)
    sizes = [n_tiles * N_GROUPS] * 3 + [2 * N_GROUPS + 1] + [n_steps] * 4
    outs = pl.pallas_call(
        functools.partial(_plan_kernel, n_tiles=n_tiles, n_steps=n_steps),
        grid_spec=pltpu.PrefetchScalarGridSpec(num_scalar_prefetch=1, grid=(1,), in_specs=[],
                                               out_specs=[smem] * len(sizes)),
        out_shape=[jax.ShapeDtypeStruct((n,), jnp.int32) for n in sizes],
        compiler_params=_params(1),
        name="moe_plan",
    )(cnt.reshape(-1))
    return (*outs, n_steps)


def _sort_kernel(ib_ref, dst_ref, len_ref, fill_ref, hx_ref, grp_t_ref, tri_ref, xs_ref, xsort, sems):
    i = pl.program_id(0)
    n = pl.num_programs(0)
    tm = hx_ref.shape[0]
    half = tm // 2
    slot = i % 2

    def copy(src, dst, rows, sl, g, part):
        return pltpu.make_async_copy(xsort.at[sl, pl.ds(pl.multiple_of(src, ROW_ALIGN), rows)],
                                     xs_ref.at[pl.ds(pl.multiple_of(dst, ROW_ALIGN), rows)], sems.at[sl, g, part])

    def for_segments(step, sl, act):
        for g in range(N_GROUPS):
            src, dst = ib_ref[step * N_GROUPS + g], dst_ref[step * N_GROUPS + g]
            act(copy(src, dst, half, sl, g, 0), g)

            @pl.when(len_ref[step * N_GROUPS + g] > half)
            def _():
                act(copy(src + half, dst + half, half, sl, g, 1), g)

    def run(cs):
        for c in cs:
            c.start()
        for c in cs:
            c.wait()

    @pl.when(i == 0)
    def _():
        for sl in range(2):
            xsort[sl, SORT_ROWS:, :] = jnp.zeros((xsort.shape[1] - SORT_ROWS, XS_WIDTH), BF16)

    g_row = grp_t_ref[0:1, :]
    sub = lax.broadcasted_iota(jnp.int32, (8, tm), 0).astype(F32)
    mine = sub == g_row
    before = _dot(jnp.where(mine, 1.0, 0.0).astype(BF16), tri_ref[...])
    dest = jnp.sum(jnp.where(mine, before, 0.0), axis=0, keepdims=True)
    for g in range(N_GROUPS):
        base = jnp.full(dest.shape, ib_ref[i * N_GROUPS + g], jnp.int32).astype(F32)
        dest = dest + jnp.where(g_row == g, base, 0.0)
    rows = lax.broadcasted_iota(jnp.int32, (SORT_ROWS, tm), 0).astype(F32)
    perm = jnp.where(rows == dest, 1.0, 0.0).astype(BF16)
    xsort[slot, 0:SORT_ROWS, :] = _dot(perm, hx_ref[...]).astype(BF16)

    @pl.when(i > 0)
    def _():
        for_segments(i - 1, 1 - slot, lambda c, g: c.wait())

    for_segments(i, slot, lambda c, g: c.start(priority=g % 2))

    @pl.when(i == n - 1)
    def _():
        for_segments(i, slot, lambda c, g: c.wait())
        run([copy(SORT_ROWS, fill_ref[g], tm, slot, g, 0) for g in range(N_GROUPS)])
        run([copy(SORT_ROWS, fill_ref[N_GROUPS + g], tm, slot, g, 0) for g in range(N_GROUPS)])

        @pl.loop(fill_ref[2 * N_GROUPS], xs_ref.shape[0] // tm)
        def _(b):
            run([copy(SORT_ROWS, b * tm, tm, slot, 0, 0)])


def _expert_kernel(ib_ref, ob_ref, grp_ref, kind_ref, xs_ref, w1_ref, w3_ref, w2_ref, ys_ref):
    del ib_ref, ob_ref
    k = pl.program_id(0)

    @pl.when(kind_ref[k] == 1)
    def _():
        x = xs_ref[:, 0:D_MODEL]
        side = lambda w_ref: jnp.concatenate([w_ref[e].astype(BF16) for e in range(EXPERTS_PER_GROUP)], axis=1)
        hid = jax.nn.silu(_dot(x, side(w1_ref))) * _dot(x, side(w3_ref))
        comb = (xs_ref[:, D_MODEL:D_MODEL + LANES].astype(F32) + xs_ref[:, D_MODEL + LANES:].astype(F32))
        lane = lax.broadcasted_iota(jnp.int32, comb.shape, 1)
        first = grp_ref[k] * EXPERTS_PER_GROUP
        scale = []
        for e in range(EXPERTS_PER_GROUP):
            w_e = jnp.sum(jnp.where(lane == first + e, comb, 0.0), axis=-1, keepdims=True)
            scale.append(jnp.broadcast_to(w_e, (comb.shape[0], D_EXPERT)))
        hid = hid * jnp.concatenate(scale, axis=-1)
        w2 = w2_ref[...].astype(BF16).reshape(EXPERTS_PER_GROUP * D_EXPERT, D_MODEL)
        ys_ref[...] = _dot(hid.astype(BF16), w2).astype(BF16)

    @pl.when(kind_ref[k] == 0)
    def _():
        ys_ref[...] = jnp.zeros_like(ys_ref)


def _unsort_kernel(ib_ref, dst_ref, len_ref, y_ref, route_ref, tri_ref, ys_ref, o1_ref, o2_ref, ybuf, yasm, sems,
                   *, tiles_first):
    i = pl.program_id(0)
    n = pl.num_programs(0)
    tm = y_ref.shape[0]
    half = tm // 2
    slot = i % 2

    def fetch(step, sl, act):
        for g in range(N_GROUPS):
            src = pl.multiple_of(dst_ref[step * N_GROUPS + g], ROW_ALIGN)
            act(pltpu.make_async_copy(ys_ref.at[pl.ds(src, half)], ybuf.at[sl, g, 0:half], sems.at[sl, g, 0]), g)

            @pl.when(len_ref[step * N_GROUPS + g] > half)
            def _():
                act(pltpu.make_async_copy(ys_ref.at[pl.ds(src + half, half)], ybuf.at[sl, g, half:tm],
                                          sems.at[sl, g, 1]), g)

    @pl.when(i == 0)
    def _():
        yasm[...] = jnp.zeros_like(yasm)
        fetch(0, 0, lambda c, g: c.start(priority=g % 2))

    @pl.when(i + 1 < n)
    def _():
        fetch(i + 1, 1 - slot, lambda c, g: c.start(priority=g % 2))

    route = route_ref[...]
    lane = lax.broadcasted_iota(jnp.int32, route.shape, 1).astype(F32)
    grp = jnp.sum(jnp.where(lane == GROUP_LANE, route, 0.0), axis=-1, keepdims=True)
    mine = lane == grp
    before = _dot(tri_ref[...], jnp.where(mine, 1.0, 0.0).astype(BF16))
    dest = jnp.sum(jnp.where(mine, before, 0.0), axis=-1, keepdims=True)
    for g in range(N_GROUPS):
        base = jnp.full(dest.shape, ib_ref[i * N_GROUPS + g], jnp.int32).astype(F32)
        dest = dest + jnp.where(grp == g, base, 0.0)
    cols = lax.broadcasted_iota(jnp.int32, (tm, SORT_ROWS), 1).astype(F32)
    perm_t = jnp.where(cols == dest, 1.0, 0.0).astype(BF16)

    fetch(i, slot, lambda c, g: c.wait())
    for g in range(N_GROUPS):
        row0 = pl.multiple_of(ib_ref[i * N_GROUPS + g], ROW_ALIGN)
        yasm[pl.ds(row0, half), :] = ybuf[slot, g, 0:half]

        @pl.when(len_ref[i * N_GROUPS + g] > half)
        def _():
            yasm[pl.ds(row0 + half, half), :] = ybuf[slot, g, half:tm]
    out = y_ref[...] + _dot(perm_t, yasm[0:SORT_ROWS, :])

    @pl.when(i < tiles_first)
    def _():
        o1_ref[...] = out

    @pl.when(i >= tiles_first)
    def _():
        o2_ref[...] = out


def _moe(y, hx, route, grp_t, cnt, w, rows_first):
    t_total = y.shape[0]
    tm = MOE_TILE
    n_tiles = t_total // tm
    assert SORT_ROWS >= tm + N_GROUPS * (ROW_ALIGN - 1)
    in_base, dst, seg_len, fill, in_blk, out_blk, grp_of_step, is_expert, n_steps = _moe_plan(
        cnt[:, 0, :N_GROUPS].astype(jnp.int32), t_total)
    cap_rows = n_steps * tm
    lower = jnp.asarray(np.tril(np.ones((tm, tm)), -1), BF16)
    upper = jnp.asarray(np.triu(np.ones((tm, tm)), 1), BF16)
    stage_rows = SORT_ROWS + tm
    any_spec = pl.BlockSpec(memory_space=pl.ANY)

    xs = pl.pallas_call(
        _sort_kernel,
        grid_spec=pltpu.PrefetchScalarGridSpec(
            num_scalar_prefetch=4,
            grid=(n_tiles,),
            in_specs=[pl.BlockSpec((tm, XS_WIDTH), lambda i, *_: (i, 0)),
                      pl.BlockSpec((8, tm), lambda i, *_: (0, i)),
                      pl.BlockSpec((tm, tm), lambda i, *_: (0, 0))],
            out_specs=any_spec,
            scratch_shapes=[pltpu.VMEM((2, stage_rows, XS_WIDTH), BF16),
                            pltpu.SemaphoreType.DMA((2, N_GROUPS, 2))],
        ),
        out_shape=jax.ShapeDtypeStruct((cap_rows, XS_WIDTH), BF16),
        compiler_params=_params(1),
        name="moe_sort",
    )(in_base, dst, seg_len, fill, hx, grp_t, upper)

    step_map = lambda which: (lambda k, ib, ob, grp, kind: ((ib, ob, grp)[which][k], 0))
    wmap = lambda k, ib, ob, grp, kind: (grp[k], 0, 0)
    wspec = pl.BlockSpec((EXPERTS_PER_GROUP, D_MODEL, D_EXPERT), wmap)
    ys = pl.pallas_call(
        _expert_kernel,
        grid_spec=pltpu.PrefetchScalarGridSpec(
            num_scalar_prefetch=4,
            grid=(n_steps,),
            in_specs=[pl.BlockSpec((tm, XS_WIDTH), step_map(0)), wspec, wspec, pl.BlockSpec((EXPERTS_PER_GROUP, D_EXPERT, D_MODEL), wmap)],
            out_specs=pl.BlockSpec((tm, D_MODEL), step_map(1)),
        ),
        out_shape=jax.ShapeDtypeStruct((cap_rows, D_MODEL), BF16),
        compiler_params=_params(1),
        name="moe_experts",
    )(in_blk, out_blk, grp_of_step, is_expert, xs, w["w1"], w["w3"], w["w2"])

    first, second = _two_stream_maps(rows_first // tm)
    return pl.pallas_call(
        functools.partial(_unsort_kernel, tiles_first=rows_first // tm),
        grid_spec=pltpu.PrefetchScalarGridSpec(
            num_scalar_prefetch=3,
            grid=(n_tiles,),
            in_specs=[pl.BlockSpec((tm, D_MODEL), lambda i, *_: (i, 0)),
                      pl.BlockSpec((tm, LANES), lambda i, *_: (i, 0)),
                      pl.BlockSpec((tm, tm), lambda i, *_: (0, 0)),
                      any_spec],
            out_specs=[pl.BlockSpec((tm, D_MODEL), first), pl.BlockSpec((tm, D_MODEL), second)],
            scratch_shapes=[pltpu.VMEM((2, N_GROUPS, tm, D_MODEL), BF16),
                            pltpu.VMEM((stage_rows, D_MODEL), BF16),
                            pltpu.SemaphoreType.DMA((2, N_GROUPS, 2))],
        ),
        out_shape=[jax.ShapeDtypeStruct((rows_first, D_MODEL), F32),
                   jax.ShapeDtypeStruct((t_total - rows_first, D_MODEL), F32)],
        compiler_params=_params(1),
        name="moe_unsort",
    )(in_base, dst, seg_len, y, route, lower, ys)


def _prep_weights(g_mix, w_in, b_f, q_norm_a, k_norm_a, q_norm_b, k_norm_b, w_pa, w_pb, w_o,
                  g_ffn, w_rg, b_rg, w_re, b_re, w1, w3, w2):
    n_qkv = 6 * WIDTH
    tile = lambda g: jnp.tile(g, N_HEADS)
    w_router = jnp.concatenate(
        [jnp.transpose(w_re, (1, 0, 2)).reshape(D_MODEL, N_EXPERTS), w_rg,
         jnp.zeros((D_MODEL, LANES - N_EXPERTS - N_GROUPS), F32)], axis=1)
    b_router = jnp.concatenate(
        [b_re.reshape(N_EXPERTS), b_rg, jnp.zeros((LANES - N_EXPERTS - N_GROUPS,), F32)])[None, :]
    return {
        "g_mix": g_mix[None, :],
        "w_qkv": w_in.T[:n_qkv].astype(BF16),
        "w_f": jnp.pad(w_in.T[n_qkv:n_qkv + N_HEADS], ((0, LANES - N_HEADS), (0, 0))).astype(BF16),
        "b_f": jnp.pad(b_f, (0, LANES - N_HEADS))[None, :],
        "w_gate": w_in.T[n_qkv + N_HEADS:].astype(BF16),
        "gains": jnp.stack([tile(q_norm_a), tile(k_norm_a), tile(q_norm_b), tile(k_norm_b)]),
        "w_pa": w_pa.astype(BF16), "w_pb": w_pb.astype(BF16), "w_o": w_o.astype(BF16),
        "g_ffn": g_ffn[None, :],
        "w_router_hi": w_router.astype(BF16),
        "w_router_mid": (w_router - w_router.astype(BF16).astype(F32)).astype(BF16),
        "b_router": b_router,
        "w1": w1, "w3": w3, "w2": w2,
    }


def kernel(x_prompt, x_sample, cache_a_k, cache_a_v, cache_b_k, cache_b_v, cache_b_logf, g_mix, w_in, b_f, q_norm_a, k_norm_a, q_norm_b, k_norm_b, rel_bias, w_pa, w_pb, w_o, g_ffn, w_rg, b_rg, w_re, b_re, w1, w3, w2):
    assert g_mix.shape[0] == 1, "single-layer step"
    n_p, seq, _ = x_prompt.shape
    n_s, t_s, _ = x_sample.shape
    a_rows = cache_a_k.shape[2]
    w = _prep_weights(g_mix[0], w_in[0], b_f[0], q_norm_a[0], k_norm_a[0], q_norm_b[0], k_norm_b[0],
                      w_pa[0], w_pb[0], w_o[0], g_ffn[0], w_rg[0], b_rg[0], w_re[0], b_re[0],
                      w1[0], w3[0], w2[0])
    band_tq = 256
    bias_prompt = _band_bias(rel_bias[0], band_tq, 3 * band_tq, 2 * band_tq, True, 3 * band_tq)
    bias_sample = _band_bias(rel_bias[0], t_s, a_rows + LANES, a_rows, False, a_rows + t_s)

    seq_minor = lambda a: jnp.transpose(a, (0, 2, 3, 1))
    seq_major = lambda a: jnp.transpose(a, (0, 3, 1, 2))[None]

    xp = x_prompt.reshape(n_p * seq, D_MODEL)
    (qa, ka, va, ka_t, va_t, qcat, kcat, vb, kb_t, vb_t, logf_t) = _inproj(xp, n_p, w, seq_minor=True)
    o_a = _band_prompt(qa, ka, va, *bias_prompt, n_p)
    o_b = _fox_prompt(qcat, kcat, vb, n_p)

    xs = x_sample.reshape(n_s * t_s, D_MODEL)
    (qa_s, _, _, ka_fs, va_fs, qcat_s, kcat_s, vb_s, kb_fs, vb_fs, logf_s) = _inproj(
        xs, 1, w, seq_minor=False)
    o_as, new_ak_t, new_av_t = _band_sample(
        qa_s, ka_fs, va_fs, seq_minor(cache_a_k[0]), seq_minor(cache_a_v[0]), *bias_sample)
    lf_cache_t = jnp.transpose(cache_b_logf[0], (0, 2, 1))
    lf_new_t = jnp.transpose(logf_s.reshape(n_s, t_s, N_HEADS), (0, 2, 1))
    o_bs = _fox_sample(qcat_s, kcat_s, vb_s, seq_minor(cache_b_k[0]), seq_minor(cache_b_v[0]),
                       lf_cache_t, lf_new_t)

    y_p, y_s = _moe(*_merge(xp, o_a, o_b, xs, o_as, o_bs, w), w, rows_first=n_p * seq)

    heads = lambda a, n, r: a.reshape(1, n, r, N_HEADS, HEAD_DIM)
    return (y_p.reshape(n_p, seq, D_MODEL), y_s.reshape(n_s, t_s, D_MODEL),
            seq_major(ka_t), seq_major(va_t), seq_major(kb_t), seq_major(vb_t),
            jnp.transpose(logf_t, (0, 2, 1))[None],
            seq_major(new_ak_t), seq_major(new_av_t),
            heads(kb_fs, n_s, t_s), heads(vb_fs, n_s, t_s), logf_s.reshape(1, n_s, t_s, N_HEADS))
```
